```python
import math
import jax
import jax.numpy as jnp
from jax import lax
import numpy as np

D_MODEL = 1024
BATCH = 16
SEQ = 256
DEPTH = 4
DEC_BATCH = 8
DEC_SEQ = 4096
PAST_LEN = 256

GRID_W = 64
H_A = 4
DK_A = 128
DV_A = 128
CONV_K = 3
CHUNK = 64
H_B = 4
DQK_B = 64
DV_B = 2 * DQK_B
H_C = 8
NOPE_C = 128
ROPE_C = 64
V_C = 128
Q_LORA = 384
KV_LORA = 256
D_FF = 2816
N_EXPERTS = 8
TOP_K = 2
D_FF_EXPERT = 1408
ROPE_BASE = 10000.0
Q_BLOCK = 128
EPS = 1e-6
EVEN_IN_SIZES = (H_A * DK_A, H_A * DK_A, H_A * DV_A, H_A * DV_A, H_A, H_A, H_A, H_A,
                 H_B * 2 * DQK_B, H_B * 2 * DQK_B, H_B * DV_B)
EVEN_IN = sum(EVEN_IN_SIZES)
ODD_IN = Q_LORA + KV_LORA + ROPE_C

kernel_name = 'hybrid_deltanet_diffattn_mla_dit_step'

f32 = jnp.float32


def rmsnorm(x, w):
    xf = x.astype(f32)
    y = xf * lax.rsqrt(jnp.mean(xf * xf, axis=-1, keepdims=True) + EPS)
    return (y * w.astype(f32)).astype(x.dtype)


def l2norm(x):
    xf = x.astype(f32)
    return xf * lax.rsqrt(jnp.sum(xf * xf, axis=-1, keepdims=True) + EPS)


def modulate(x, w, shift, scale):
    return rmsnorm(x, w) * (1.0 + scale) + shift


def split_cols(x, sizes):
    idx = [int(i) for i in np.cumsum(sizes)[:-1]]
    return jnp.split(x, idx, axis=-1)


def axial_rope_tables(n_tokens, rot_dim):
    n_rows = n_tokens // GRID_W
    rr, cc = jnp.meshgrid(jnp.arange(n_rows), jnp.arange(GRID_W), indexing='ij')
    rows = rr.reshape(-1).astype(f32)
    cols = cc.reshape(-1).astype(f32)
    per_axis = rot_dim // 2
    inv = ROPE_BASE ** (-jnp.arange(0, per_axis, 2, dtype=f32) / per_axis)
    ang_r = rows[:, None] * inv[None, :]
    ang_c = cols[:, None] * inv[None, :]
    ang = jnp.concatenate([ang_r, ang_r, ang_c, ang_c], axis=-1)
    return jnp.cos(ang), jnp.sin(ang)


def apply_rope(x, cos, sin):
    shape = (cos.shape[0],) + (1,) * (x.ndim - 3) + (cos.shape[1],)
    cos, sin = cos.reshape(shape), sin.reshape(shape)
    half = x.shape[-1] // 2
    xf = x.astype(f32)

    def rot_half(u):
        u1, u2 = jnp.split(u, 2, axis=-1)
        return jnp.concatenate([-u2, u1], axis=-1)

    xr = jnp.concatenate([rot_half(xf[..., :half]), rot_half(xf[..., half:])], axis=-1)
    return (xf * cos + xr * sin).astype(x.dtype)


def short_conv(x, w):
    pad = CONV_K // 2
    L = x.shape[1]
    xp = jnp.pad(x, ((0, 0), (pad, CONV_K - 1 - pad), (0, 0)))
    y = xp[:, 0:L] * w[0]
    for j in range(1, CONV_K):
        y = y + xp[:, j:j + L] * w[j]
    return jax.nn.silu(y)


def gated_delta_chunked(q, k, v, g, beta, s0):
    B, H, L, dk = q.shape
    dv = v.shape[-1]
    n = L // CHUNK

    def chunks(t):
        return t.astype(f32).reshape((B, H, n, CHUNK) + t.shape[3:])

    q, k, v, g, beta = chunks(q), chunks(k), chunks(v), chunks(g), chunks(beta)
    g = jnp.cumsum(g, axis=-1)
    incl = jnp.tril(jnp.ones((CHUNK, CHUNK), bool))
    strict = jnp.tril(jnp.ones((CHUNK, CHUNK), bool), -1)
    gdiff = g[..., :, None] - g[..., None, :]
    decay = jnp.where(incl, jnp.exp(jnp.where(incl, gdiff, 0.0)), 0.0)
    kb = k * beta[..., None]
    m = jnp.where(strict, jnp.einsum('bhncd,bhnsd->bhncs', kb, k) * decay, 0.0)
    a = m + jnp.eye(CHUNK, dtype=f32)
    rhs = jnp.concatenate([v * beta[..., None], kb * jnp.exp(g)[..., None]], axis=-1)
    sol = lax.linalg.triangular_solve(a, rhs, left_side=True, lower=True, unit_diagonal=True)
    u, w = sol[..., :dv], sol[..., dv:]
    a_intra = jnp.einsum('bhncd,bhnsd->bhncs', q, k) * decay
    xs = tuple(jnp.moveaxis(t, 2, 0) for t in (q, k, u, w, g, a_intra))

    def step(s, inp):
        q_i, k_i, u_i, w_i, g_i, a_i = inp
        v_new = u_i - jnp.einsum('bhcd,bhde->bhce', w_i, s)
        o = (jnp.einsum('bhcd,bhde->bhce', q_i * jnp.exp(g_i)[..., None], s)
             + jnp.einsum('bhcs,bhse->bhce', a_i, v_new))
        g_last = g_i[..., -1]
        k_dec = k_i * jnp.exp(g_last[..., None] - g_i)[..., None]
        s = s * jnp.exp(g_last)[..., None, None] + jnp.einsum('bhcd,bhce->bhde', k_dec, v_new)
        return s, o

    s_fin, o = lax.scan(step, s0.astype(f32), xs)
    o = jnp.moveaxis(o, 0, 2).reshape(B, H, L, dv)
    return o, s_fin


def bidir_gated_delta(q, k, v, g, beta, s0_f, s0_b):
    o_f, s_f = gated_delta_chunked(q, k, v, g[0], beta[0], s0_f)

    def rev(t):
        return jnp.flip(t, axis=2)

    o_b, s_b = gated_delta_chunked(rev(q), rev(k), rev(v), rev(g[1]), rev(beta[1]), s0_b)
    return o_f + rev(o_b), s_f, s_b


def map_query_blocks(fn, q):
    B, H, L = q.shape[:3]
    nb = L // Q_BLOCK
    qb = jnp.moveaxis(q.reshape((B, H, nb, Q_BLOCK) + q.shape[3:]), 2, 0)
    out = jnp.moveaxis(lax.map(fn, qb), 0, 2)
    return out.reshape(B, H, L, out.shape[-1])


def diff_attend(q, k, v, lam):
    s = jnp.einsum('bhqtd,bhktd->bhtqk', q, k).astype(f32) * (DQK_B ** -0.5)
    p = jax.nn.softmax(s, axis=-1)
    a = p[:, :, 0] - lam * p[:, :, 1]
    return jnp.einsum('bhqk,bhke->bhqe', a, v.astype(f32))


def softmax_attend(q, k, v):
    s = jnp.einsum('bhqd,bhkd->bhqk', q, k).astype(f32) * (q.shape[-1] ** -0.5)
    p = jax.nn.softmax(s, axis=-1)
    return jnp.einsum('bhqk,bhke->bhqe', p, v.astype(f32))


def delta_diff_project(h, w_in, conv_w, a_log, dt_bias, cos, sin):
    B, L, _ = h.shape
    qa, ka, va, z, af, ab, bf, bb, qb, kb, vb = split_cols(h @ w_in, EVEN_IN_SIZES)
    qkv = short_conv(jnp.concatenate([qa, ka, va], axis=-1), conv_w)
    qa, ka, va = split_cols(qkv, (H_A * DK_A, H_A * DK_A, H_A * DV_A))

    def heads(t, d):
        return t.reshape(B, L, -1, d).transpose(0, 2, 1, 3)

    qa = l2norm(heads(qa, DK_A)) * (DK_A ** -0.5)
    ka = l2norm(heads(ka, DK_A))
    va = heads(va, DV_A).astype(f32)
    a = jnp.stack([af, ab]).astype(f32)
    g = -jnp.exp(a_log.astype(f32))[:, None, None, :] * jax.nn.softplus(a + dt_bias.astype(f32)[:, None, None, :])
    beta = jax.nn.sigmoid(jnp.stack([bf, bb]).astype(f32))
    g = g.transpose(0, 1, 3, 2)
    beta = beta.transpose(0, 1, 3, 2)
    z = z.reshape(B, L, H_A, DV_A)
    qb = qb.reshape(B, L, H_B, 2, DQK_B)
    kb = kb.reshape(B, L, H_B, 2, DQK_B)
    if cos is not None:
        qb = apply_rope(qb, cos, sin)
        kb = apply_rope(kb, cos, sin)
    qb = qb.transpose(0, 2, 1, 3, 4)
    kb = kb.transpose(0, 2, 1, 3, 4)
    vb = heads(vb, DV_B)
    return qa, ka, va, g, beta, z, qb, kb, vb


def delta_diff_output(o_a, z, o_b, onorm, subln, lam_init, w_out):
    B, _, L, _ = o_a.shape
    ya = (rmsnorm(o_a.transpose(0, 2, 1, 3), onorm) * jax.nn.silu(z.astype(f32))).reshape(B, L, H_A * DV_A)
    yb = (rmsnorm(o_b, subln) * (1.0 - lam_init)).transpose(0, 2, 1, 3).reshape(B, L, H_B * DV_B)
    y = jnp.concatenate([ya, yb], axis=-1)
    return y.astype(w_out.dtype) @ w_out


def mla_project(h, w_in, q_norm, kv_norm, w_uq, cos, sin):
    B, L, _ = h.shape
    cq, ckv, kr = split_cols(h @ w_in, (Q_LORA, KV_LORA, ROPE_C))
    q = (rmsnorm(cq, q_norm) @ w_uq).reshape(B, L, H_C, NOPE_C + ROPE_C)
    ckv = rmsnorm(ckv, kv_norm)
    if cos is not None:
        q = jnp.concatenate([q[..., :NOPE_C], apply_rope(q[..., NOPE_C:], cos, sin)], axis=-1)
        kr = apply_rope(kr, cos, sin)
    return q.transpose(0, 2, 1, 3), ckv, kr


def mla_keys_values(ckv, kr, w_ukv):
    B, L, _ = ckv.shape
    kv = (ckv @ w_ukv).reshape(B, L, H_C, NOPE_C + V_C)
    k = jnp.concatenate([kv[..., :NOPE_C], jnp.broadcast_to(kr[:, :, None, :], (B, L, H_C, ROPE_C))], axis=-1)
    return k.transpose(0, 2, 1, 3), kv[..., NOPE_C:].transpose(0, 2, 1, 3)


def mla_output(o, w_out):
    B, _, L, _ = o.shape
    return o.transpose(0, 2, 1, 3).reshape(B, L, H_C * V_C).astype(w_out.dtype) @ w_out


def swiglu(h, w_gu, w_down):
    gt, up = jnp.split(h @ w_gu, 2, axis=-1)
    return (jax.nn.silu(gt) * up) @ w_down


def moe_ffn(h, router_w, w_gu, w_down):
    B, L, D = h.shape
    x = h.reshape(B * L, D)
    probs = jax.nn.softmax((x @ router_w).astype(f32), axis=-1)
    top_p, top_i = lax.top_k(probs, TOP_K)
    top_p = top_p / jnp.sum(top_p, axis=-1, keepdims=True)
    gates = jnp.einsum('nk,nke->ne', top_p, jax.nn.one_hot(top_i, N_EXPERTS, dtype=f32))
    y = jnp.zeros((B * L, D), f32)
    for e in range(N_EXPERTS):
        y = y + gates[:, e:e + 1] * swiglu(x, w_gu[e], w_down[e]).astype(f32)
    return y.reshape(B, L, D)


def setup_inputs(seed: int = 0) -> dict:
    key = jax.random.key(seed)
    ks = iter(jax.random.split(key, 40))
    D = D_MODEL
    n_even = (DEPTH + 1) // 2
    n_odd = DEPTH // 2

    def nrm(shape, scale):
        return jax.random.normal(next(ks), shape, jnp.float32) * scale

    def gain(shape):
        return 1.0 + nrm(shape, 0.02)

    w_a_out = H_A * DV_A + H_B * DV_B
    return {
        'x_prompt': nrm((BATCH, SEQ, D), 1.0),
        'x_sample': nrm((DEC_BATCH, DEC_SEQ, D), 1.0),
        'state_delta': nrm((DEC_BATCH, n_even, 2, H_A, DK_A, DV_A), 0.1),
        'cache_diff_k': nrm((DEC_BATCH, n_even, H_B, PAST_LEN, 2 * DQK_B), 1.0),
        'cache_diff_v': nrm((DEC_BATCH, n_even, H_B, PAST_LEN, DV_B), 1.0),
        'cache_mla_ckv': nrm((DEC_BATCH, n_odd, PAST_LEN, KV_LORA), 1.0),
        'cache_mla_krope': nrm((DEC_BATCH, n_odd, PAST_LEN, ROPE_C), 1.0),
        'c': nrm((DEC_BATCH, D), 1.0),
        'c_ctx': nrm((D,), 1.0),
        'mod_w': nrm((DEPTH, D, 6 * D), 0.5 * D ** -0.5),
        'mod_b': nrm((DEPTH, 6 * D), 0.02),
        'norm_mix': gain((DEPTH, D)),
        'norm_ffn': gain((DEPTH, D)),
        'final_norm': gain((D,)),
        'ev_w_in': nrm((n_even, D, EVEN_IN), D ** -0.5),
        'ev_conv_w': nrm((n_even, CONV_K, 2 * H_A * DK_A + H_A * DV_A), CONV_K ** -0.5),
        'ev_a_log': jnp.log(jax.random.uniform(next(ks), (n_even, 2, H_A), jnp.float32, 0.05, 1.0)),
        'ev_dt_bias': nrm((n_even, 2, H_A), 0.1),
        'ev_onorm': gain((n_even, DV_A)),
        'ev_lambda': nrm((n_even, 4, DQK_B), 0.1),
        'ev_subln': gain((n_even, DV_B)),
        'ev_w_out': nrm((n_even, w_a_out, D), w_a_out ** -0.5),
        'ffn_w_gu': nrm((n_even, D, 2 * D_FF), D ** -0.5),
        'ffn_w_down': nrm((n_even, D_FF, D), D_FF ** -0.5),
        'od_w_in': nrm((n_odd, D, ODD_IN), D ** -0.5),
        'od_q_norm': gain((n_odd, Q_LORA)),
        'od_kv_norm': gain((n_odd, KV_LORA)),
        'od_w_uq': nrm((n_odd, Q_LORA, H_C * (NOPE_C + ROPE_C)), Q_LORA ** -0.5),
        'od_w_ukv': nrm((n_odd, KV_LORA, H_C * (NOPE_C + V_C)), KV_LORA ** -0.5),
        'od_w_out': nrm((n_odd, H_C * V_C, D), (H_C * V_C) ** -0.5),
        'moe_router': nrm((n_odd, D, N_EXPERTS), D ** -0.5),
        'moe_w_gu': nrm((n_odd, N_EXPERTS, D, 2 * D_FF_EXPERT), D ** -0.5),
        'moe_w_down': nrm((n_odd, N_EXPERTS, D_FF_EXPERT, D), D_FF_EXPERT ** -0.5),
    }


def reference(x_prompt, x_sample, state_delta, cache_diff_k, cache_diff_v, cache_mla_ckv, cache_mla_krope,
              c, c_ctx, mod_w, mod_b, norm_mix, norm_ffn, final_norm,
              ev_w_in, ev_conv_w, ev_a_log, ev_dt_bias, ev_onorm, ev_lambda, ev_subln, ev_w_out,
              ffn_w_gu, ffn_w_down, od_w_in, od_q_norm, od_kv_norm, od_w_uq, od_w_ukv, od_w_out,
              moe_router, moe_w_gu, moe_w_down):
    bp, lp, _ = x_prompt.shape
    bs, ls, _ = x_sample.shape
    pl = cache_diff_k.shape[3]
    cos_b, sin_b = axial_rope_tables(ls, DQK_B)
    cos_c, sin_c = axial_rope_tables(ls, ROPE_C)
    xp, xs = x_prompt, x_sample
    sd_new, dk_new, dv_new, ckv_new, kr_new = [], [], [], [], []
    for layer in range(DEPTH):
        j = layer // 2
        m_p = jnp.split((jax.nn.silu(c_ctx) @ mod_w[layer] + mod_b[layer])[None, None, :], 6, axis=-1)
        m_s = jnp.split((jax.nn.silu(c) @ mod_w[layer] + mod_b[layer])[:, None, :], 6, axis=-1)
        hp = modulate(xp, norm_mix[layer], m_p[0], m_p[1])
        hs = modulate(xs, norm_mix[layer], m_s[0], m_s[1])
        if layer % 2 == 0:
            lam_init = 0.8 - 0.6 * math.exp(-0.3 * layer)
            lp_ = ev_lambda[j].astype(f32)
            lam = jnp.exp(jnp.sum(lp_[0] * lp_[1])) - jnp.exp(jnp.sum(lp_[2] * lp_[3])) + lam_init
            qa, ka, va, g, beta, z, qb, kb, vb = delta_diff_project(
                hp, ev_w_in[j], ev_conv_w[j], ev_a_log[j], ev_dt_bias[j], None, None)
            s0 = jnp.zeros((bp, H_A, DK_A, DV_A), f32)
            o_a, s_f, s_b = bidir_gated_delta(qa, ka, va, g, beta, s0, s0)
            o_b = map_query_blocks(lambda qq: diff_attend(qq, kb, vb, lam), qb)
            mix_p = delta_diff_output(o_a, z, o_b, ev_onorm[j], ev_subln[j], lam_init, ev_w_out[j])
            sd_new.append(jnp.stack([s_f, s_b], axis=1))
            dk_new.append(kb.reshape(bp, H_B, lp, 2 * DQK_B))
            dv_new.append(vb)
            qa, ka, va, g, beta, z, qb, kb, vb = delta_diff_project(
                hs, ev_w_in[j], ev_conv_w[j], ev_a_log[j], ev_dt_bias[j], cos_b, sin_b)
            o_a, _, _ = bidir_gated_delta(qa, ka, va, g, beta, state_delta[:, j, 0], state_delta[:, j, 1])
            k_all = jnp.concatenate([kb, cache_diff_k[:, j].reshape(bs, H_B, pl, 2, DQK_B)], axis=2)
            v_all = jnp.concatenate([vb, cache_diff_v[:, j]], axis=2)
            o_b = map_query_blocks(lambda qq: diff_attend(qq, k_all, v_all, lam), qb)
            mix_s = delta_diff_output(o_a, z, o_b, ev_onorm[j], ev_subln[j], lam_init, ev_w_out[j])
        else:
            q, ckv, kr = mla_project(hp, od_w_in[j], od_q_norm[j], od_kv_norm[j], od_w_uq[j], None, None)
            k, v = mla_keys_values(ckv, kr, od_w_ukv[j])
            o = map_query_blocks(lambda qq: softmax_attend(qq, k, v), q)
            mix_p = mla_output(o, od_w_out[j])
            ckv_new.append(ckv)
            kr_new.append(kr)
            q, ckv, kr = mla_project(hs, od_w_in[j], od_q_norm[j], od_kv_norm[j], od_w_uq[j], cos_c, sin_c)
            ckv_all = jnp.concatenate([ckv, cache_mla_ckv[:, j]], axis=1)
            kr_all = jnp.concatenate([kr, cache_mla_krope[:, j]], axis=1)
            k_s, v_s = mla_keys_values(ckv_all, kr_all, od_w_ukv[j])
            o = map_query_blocks(lambda qq: softmax_attend(qq, k_s, v_s), q)
            mix_s = mla_output(o, od_w_out[j])
        xp = xp + (m_p[2] * mix_p).astype(xp.dtype)
        xs = xs + (m_s[2] * mix_s).astype(xs.dtype)
        hp = modulate(xp, norm_ffn[layer], m_p[3], m_p[4])
        hs = modulate(xs, norm_ffn[layer], m_s[3], m_s[4])
        if layer % 2 == 0:
            fp = swiglu(hp, ffn_w_gu[j], ffn_w_down[j])
            fs = swiglu(hs, ffn_w_gu[j], ffn_w_down[j])
        else:
            fp = moe_ffn(hp, moe_router[j], moe_w_gu[j], moe_w_down[j])
            fs = moe_ffn(hs, moe_router[j], moe_w_gu[j], moe_w_down[j])
        xp = xp + (m_p[5] * fp).astype(xp.dtype)
        xs = xs + (m_s[5] * fs).astype(xs.dtype)
    y_prompt = rmsnorm(xp, final_norm)
    y_sample = rmsnorm(xs, final_norm)
    return (y_prompt, y_sample, jnp.stack(sd_new, axis=1), jnp.stack(dk_new, axis=1), jnp.stack(dv_new, axis=1),
            jnp.stack(ckv_new, axis=1), jnp.stack(kr_new, axis=1))
```

```python
import functools
import math

import numpy as np
import jax
import jax.numpy as jnp
from jax import lax
from jax.experimental import pallas as pl
from jax.experimental.pallas import tpu as pltpu

F32 = jnp.float32
BF16 = jnp.bfloat16

EPS = 1e-6
LOG2E = 1.4426950408889634
GRID_W = 64
ROPE_BASE = 10000.0
H_A, DK_A, DV_A = 4, 128, 128
CONV_K = 3
CHUNK = 64
H_B, DQK_B, DV_B = 4, 64, 128
H_C, NOPE_C, ROPE_C, V_C = 8, 128, 64, 128
Q_LORA, KV_LORA = 384, 256
LANES = 128
SUBLANES = 8
VMEM_LIMIT_BYTES = 56 * 1024 * 1024


def _cparams(*sem):
    return pltpu.CompilerParams(dimension_semantics=sem, vmem_limit_bytes=VMEM_LIMIT_BYTES)


def _dot(a, b):
    return jnp.dot(a, b, preferred_element_type=F32)


def _dot_nt(a, b):
    return lax.dot_general(a, b, (((1,), (1,)), ((), ())), preferred_element_type=F32)


def _dot_tn(a, b):
    return lax.dot_general(a, b, (((0,), (0,)), ((), ())), preferred_element_type=F32)


def _split3(x):
    hi = x.astype(BF16)
    r = x - hi.astype(F32)
    mid = r.astype(BF16)
    lo = (r - mid.astype(F32)).astype(BF16)
    return hi, mid, lo


def _dot_f32(a, b):
    ah, am, al = _split3(a)
    bh, bm, bl = _split3(b)
    return (_dot(ah, bh) + (_dot(ah, bm) + _dot(am, bh))
            + (_dot(am, bm) + _dot(ah, bl) + _dot(al, bh)))


def _silu(x):
    return x * jax.nn.sigmoid(x)


def _tile(n, pref, mult=SUBLANES):
    t = min(n, pref)
    while t > mult and (n % t or t % mult):
        t -= mult
    assert n % t == 0, (n, pref)
    return t


class _Rows:
    def __init__(self, n_ctx, n_lat, lat_len, tm):
        assert n_ctx % tm == 0 and lat_len % tm == 0
        self.tm = tm
        self.n = n_ctx + n_lat
        self.ctx_tiles = n_ctx // tm
        self.seq_tiles = lat_len // tm

    def group(self, i):
        return jnp.where(i < self.ctx_tiles, 0, 1 + jnp.maximum(i - self.ctx_tiles, 0) // self.seq_tiles)

    def pos_block(self, i):
        return jnp.where(i < self.ctx_tiles, 0, 1 + jnp.maximum(i - self.ctx_tiles, 0) % self.seq_tiles)


def _mod_kernel(c_ref, w_ref, b_ref, o_ref):
    o_ref[...] = _dot_f32(_silu(c_ref[...]), w_ref[...]) + b_ref[...]


def _mods(cc, mod_w, mod_b):
    depth, d, n6 = mod_w.shape
    tn = _tile(n6, 512, LANES)
    return pl.pallas_call(
        _mod_kernel,
        grid=(depth, n6 // tn),
        in_specs=[pl.BlockSpec(cc.shape, lambda l, j: (0, 0)),
                  pl.BlockSpec((None, d, tn), lambda l, j: (l, 0, j)),
                  pl.BlockSpec((None, 1, tn), lambda l, j: (l, 0, j))],
        out_specs=pl.BlockSpec((None, cc.shape[0], tn), lambda l, j: (l, 0, j)),
        out_shape=jax.ShapeDtypeStruct((depth, cc.shape[0], n6), F32),
        compiler_params=_cparams("parallel", "parallel"),
        name="mods",
    )(cc, mod_w, mod_b.reshape(depth, 1, n6))


def _mod_chunk(mods_ref, g, idx, d):
    return mods_ref[pl.ds(g, 1), idx * d:(idx + 1) * d]


def _rmsnorm_rows(x, w):
    ms = jnp.mean(x * x, axis=-1, keepdims=True)
    return x * lax.rsqrt(ms + EPS) * w


def _rope(x, cos, sin_signed):
    lane = lax.broadcasted_iota(jnp.int32, (1, LANES), 1)
    low = (lane % 32) < 16
    outs = []
    for c in range(x.shape[1] // LANES):
        xc = x[:, c * LANES:(c + 1) * LANES]
        fwd = pltpu.roll(xc, LANES - 16, axis=1)
        bwd = pltpu.roll(xc, 16, axis=1)
        outs.append(xc * cos + jnp.where(low, fwd, bwd) * sin_signed)
    return outs[0] if len(outs) == 1 else jnp.concatenate(outs, axis=1)


def _rope_tables(rows, lat_len):
    pos = jnp.arange(lat_len)
    r = (pos // GRID_W).astype(F32)
    c = (pos % GRID_W).astype(F32)
    inv = ROPE_BASE ** (-jnp.arange(0, 32, 2, dtype=F32) / 32)
    ang_r = r[:, None] * inv[None, :]
    ang_c = c[:, None] * inv[None, :]
    ang = jnp.concatenate([ang_r, ang_r, ang_c, ang_c] * 2, axis=-1)
    sign = jnp.where((jnp.arange(LANES) % 32) < 16, -1.0, 1.0).astype(F32)
    cos = jnp.concatenate([jnp.ones((rows.tm, LANES), F32), jnp.cos(ang)], axis=0)
    sin = jnp.concatenate([jnp.zeros((rows.tm, LANES), F32), jnp.sin(ang) * sign[None, :]], axis=0)
    return cos, sin


def _nm_kernel(*refs, rows, d_mod, has_norm, mod_idx, has_rope, epi, emit_h):
    it = iter(refs)
    x_ref = next(it)
    w_ref = next(it)
    nw_ref = next(it) if has_norm else None
    mods_ref = next(it) if mod_idx is not None else None
    cos_ref, sin_ref = (next(it), next(it)) if has_rope else (None, None)
    o_ref = next(it)
    h_out_ref = next(it) if emit_h else None
    h_scr = next(it)
    i, j = pl.program_id(0), pl.program_id(1)

    @pl.when(j == 0)
    def _():
        x = x_ref[...].astype(F32)
        if has_norm:
            x = _rmsnorm_rows(x, nw_ref[...])
        if mod_idx is not None:
            g = rows.group(i)
            x = x * (1.0 + _mod_chunk(mods_ref, g, mod_idx[1], d_mod)) + _mod_chunk(mods_ref, g, mod_idx[0], d_mod)
        h_scr[...] = x.astype(BF16)
        if emit_h:
            h_out_ref[...] = x

    acc = _dot(h_scr[...], w_ref[...])
    kinds = sorted(set(epi))
    for kind in kinds:
        cond = None
        for jj, e in enumerate(epi):
            if e == kind:
                cond = (j == jj) if cond is None else (cond | (j == jj))

        def _store(kind=kind):
            y = acc
            if kind[0]:
                y = _rope(y, cos_ref[...], sin_ref[...])
            if kind[1] != 1.0:
                y = y * kind[1]
            o_ref[...] = y.astype(o_ref.dtype)

        if len(kinds) == 1:
            _store()
        else:
            pl.when(cond)(_store)


def _norm_matmul(x, w, rows, *, tn, norm_w=None, mods=None, mod_idx=None, rope=None, epi=None,
                 emit_h=False, out_dtype=F32):
    n, k = x.shape
    nout = w.shape[1]
    nj = nout // tn
    tm = rows.tm
    epi = tuple(epi) if epi is not None else ((False, 1.0),) * nj
    assert len(epi) == nj and n % tm == 0
    has_rope = any(e[0] for e in epi)
    args = [x, w]
    in_specs = [pl.BlockSpec((tm, k), lambda i, j: (i, 0)), pl.BlockSpec((k, tn), lambda i, j: (0, j))]
    if norm_w is not None:
        args.append(norm_w.reshape(1, k).astype(F32))
        in_specs.append(pl.BlockSpec((1, k), lambda i, j: (0, 0)))
    if mod_idx is not None:
        args.append(mods)
        in_specs.append(pl.BlockSpec(mods.shape, lambda i, j: (0, 0)))
    if has_rope:
        args += list(rope)
        in_specs += [pl.BlockSpec((tm, LANES), lambda i, j: (rows.pos_block(i), 0))] * 2
    out_shape = [jax.ShapeDtypeStruct((n, nout), out_dtype)]
    out_specs = [pl.BlockSpec((tm, tn), lambda i, j: (i, j))]
    if emit_h:
        out_shape.append(jax.ShapeDtypeStruct((n, k), F32))
        out_specs.append(pl.BlockSpec((tm, k), lambda i, j: (i, 0)))
    kern = functools.partial(_nm_kernel, rows=rows, d_mod=k, has_norm=norm_w is not None, mod_idx=mod_idx,
                             has_rope=has_rope, epi=epi, emit_h=emit_h)
    out = pl.pallas_call(
        kern, grid=(n // tm, nj), in_specs=in_specs, out_specs=out_specs, out_shape=out_shape,
        scratch_shapes=[pltpu.VMEM((tm, k), BF16)],
        compiler_params=_cparams("parallel", "arbitrary"), name="norm_matmul",
    )(*args)
    return out if emit_h else out[0]


def _conv_kernel(x_ref, prev_ref, next_ref, sm_ref, cw_ref, alog_ref, dt_ref, o_ref, gc_ref, gr_ref,
                 *, tiles_ctx, tiles_seq, halo, n_qk):
    i = pl.program_id(0)
    t = jnp.maximum(i - tiles_ctx, 0) % tiles_seq
    lat = i >= tiles_ctx
    has_prev = lat & (t > 0)
    has_next = lat & (t < tiles_seq - 1)
    x = x_ref[...]
    tm = x.shape[0]
    row = lax.broadcasted_iota(jnp.int32, (tm, 1), 0)
    p_row = jnp.where(has_prev, prev_ref[halo - 1:halo, :], 0.0)
    n_row = jnp.where(has_next, next_ref[0:1, :], 0.0)
    x_prev = jnp.where(row == 0, p_row, pltpu.roll(x, 1, axis=0))
    x_next = jnp.where(row == tm - 1, n_row, pltpu.roll(x, tm - 1, axis=0))
    cw = cw_ref[...]
    y = _silu(x_prev * cw[0:1, :] + x * cw[1:2, :] + x_next * cw[2:3, :])
    outs = []
    for hh in range(y.shape[1] // LANES):
        yh = y[:, hh * LANES:(hh + 1) * LANES]
        if hh < n_qk:
            yh = yh * lax.rsqrt(jnp.sum(yh * yh, axis=-1, keepdims=True) + EPS)
            if hh < n_qk // 2:
                yh = yh * (DK_A ** -0.5)
        outs.append(yh)
    o_ref[...] = jnp.concatenate(outs, axis=1)
    s = sm_ref[...]
    lane = lax.broadcasted_iota(jnp.int32, (1, LANES), 1)
    is_g = (lane % 4) < 2
    z = s + dt_ref[...]
    softplus = jnp.maximum(z, 0.0) + jnp.log(1.0 + jnp.exp(-jnp.abs(z)))
    g = jnp.where(is_g, -jnp.exp(alog_ref[...]) * softplus, jax.nn.sigmoid(s))
    gc_ref[...] = g
    gr_ref[...] = jnp.transpose(g)[:gr_ref.shape[0], :]


def _conv_stage(proj, small, conv_w, a_log, dt_bias, rows_c, n_conv):
    n = proj.shape[0]
    tm = rows_c.tm
    halo = 16
    hb = tm // halo
    nblk16 = n // halo
    alog_row = jnp.zeros((1, LANES), F32).at[0, :4 * H_A].set(
        jnp.stack([a_log[0], a_log[1], a_log[0], a_log[1]], axis=-1).reshape(-1).astype(F32))
    dt_row = jnp.zeros((1, LANES), F32).at[0, :4 * H_A].set(
        jnp.stack([dt_bias[0], dt_bias[1], jnp.zeros_like(dt_bias[0]), jnp.zeros_like(dt_bias[0])],
                  axis=-1).reshape(-1).astype(F32))
    kern = functools.partial(_conv_kernel, tiles_ctx=rows_c.ctx_tiles, tiles_seq=rows_c.seq_tiles, halo=halo,
                             n_qk=2 * H_A)
    return pl.pallas_call(
        kern, grid=(n // tm,),
        in_specs=[pl.BlockSpec((tm, n_conv), lambda i: (i, 0)),
                  pl.BlockSpec((halo, n_conv), lambda i: (jnp.maximum(i * hb - 1, 0), 0)),
                  pl.BlockSpec((halo, n_conv), lambda i: (jnp.minimum((i + 1) * hb, nblk16 - 1), 0)),
                  pl.BlockSpec((tm, LANES), lambda i: (i, 0)),
                  pl.BlockSpec((CONV_K, n_conv), lambda i: (0, 0)),
                  pl.BlockSpec((1, LANES), lambda i: (0, 0)),
                  pl.BlockSpec((1, LANES), lambda i: (0, 0))],
        out_specs=[pl.BlockSpec((tm, n_conv), lambda i: (i, 0)),
                   pl.BlockSpec((tm, LANES), lambda i: (i, 0)),
                   pl.BlockSpec((4 * H_A, tm), lambda i: (0, i))],
        out_shape=[jax.ShapeDtypeStruct((n, n_conv), F32),
                   jax.ShapeDtypeStruct((n, LANES), F32),
                   jax.ShapeDtypeStruct((4 * H_A, n), F32)],
        compiler_params=_cparams("parallel"), name="conv_gate",
    )(proj, proj, proj, small, conv_w.astype(F32), alog_row, dt_row)


def _tri_solve(m, r, rev):
    c = m.shape[0]
    ng = c // SUBLANES
    mg = [m[SUBLANES * g:SUBLANES * (g + 1), :] for g in range(ng)]
    xg = [r[SUBLANES * g:SUBLANES * (g + 1), :] for g in range(ng)]
    for j in (range(c - 1, 0, -1) if rev else range(c - 1)):
        xj = xg[j // SUBLANES][j % SUBLANES:j % SUBLANES + 1, :]
        groups = range(0, (j - 1) // SUBLANES + 1) if rev else range((j + 1) // SUBLANES, ng)
        for g in groups:
            xg[g] = xg[g] - mg[g][:, j:j + 1] * xj
    return jnp.concatenate(xg, axis=0)


def _delta_kernel(qf, kf, vf, gcf, grf, qb, kb, vb, gcb, grb, s0_ref, of_ref, ob_ref, so_ref, s_scr, *, chunk):
    h = pl.program_id(1)
    i = pl.program_id(2)
    n_i = pl.num_programs(2)

    @pl.when(i == 0)
    def _():
        s_scr[...] = s0_ref[...]

    rg = qf.shape[0]
    nc = rg // chunk
    rowi = lax.broadcasted_iota(jnp.int32, (rg, rg), 0)
    coli = lax.broadcasted_iota(jnp.int32, (rg, rg), 1)
    same = (rowi // chunk) == (coli // chunk)
    lane = lax.broadcasted_iota(jnp.int32, (1, LANES), 1)
    r_c = lax.broadcasted_iota(jnp.int32, (chunk, chunk), 0)
    c_c = lax.broadcasted_iota(jnp.int32, (chunk, chunk), 1)

    for d, (q_ref, k_ref, v_ref, gc_ref, gr_ref, o_ref) in enumerate(
            ((qf, kf, vf, gcf, grf, of_ref), (qb, kb, vb, gcb, grb, ob_ref))):
        rev = d == 1
        incl_big = same & ((rowi <= coli) if rev else (rowi >= coli))
        lm = jnp.where(incl_big, 1.0, 0.0).astype(BF16)
        gcols = gc_ref[...]
        grows = gr_ref[...]
        gh, gm, gl_ = _split3(gcols)
        cum_c = _dot(lm, gh) + _dot(lm, gm) + _dot(lm, gl_)
        th, tm_, tl = _split3(grows)
        cum_r = _dot_nt(th, lm) + _dot_nt(tm_, lm) + _dot_nt(tl, lm)

        def col(a, ln):
            return jnp.sum(jnp.where(lane == ln, a, 0.0), axis=1, keepdims=True)

        sub = lax.broadcasted_iota(jnp.int32, (cum_r.shape[0], 1), 0)
        gcum = col(cum_c, 4 * h + d)
        beta = col(gcols, 4 * h + 2 + d)
        grow = jnp.sum(jnp.where(sub == 4 * h + d, cum_r, 0.0), axis=0, keepdims=True)

        q = q_ref[...]
        k = k_ref[...]
        v = v_ref[...]
        kbeta = k * beta
        eg = jnp.exp(gcum)
        rhs = jnp.concatenate([v * beta, kbeta * eg], axis=1)
        qg = (q * eg).astype(BF16)
        kb16 = kbeta.astype(BF16)
        k16 = k.astype(BF16)
        q16 = q.astype(BF16)
        incl = (r_c <= c_c) if rev else (r_c >= c_c)
        strict = (r_c < c_c) if rev else (r_c > c_c)
        xs, a_in = [], []
        for c in range(nc):
            sl = slice(c * chunk, (c + 1) * chunk)
            e = jnp.exp(jnp.where(incl, gcum[sl] - grow[:, sl], 0.0))
            kk = _dot_nt(kb16[sl], k16[sl])
            qk = _dot_nt(q16[sl], k16[sl])
            a_in.append(jnp.where(incl, qk * e, 0.0).astype(BF16))
            xs.append(_tri_solve(jnp.where(strict, kk * e, 0.0), rhs[sl], rev))
        s = s_scr[d]
        outs = [None] * nc
        for c in (range(nc - 1, -1, -1) if rev else range(nc)):
            sl = slice(c * chunk, (c + 1) * chunk)
            u = xs[c][:, :DV_A]
            w = xs[c][:, DV_A:]
            s16 = s.astype(BF16)
            v_new = (u - _dot(w.astype(BF16), s16)).astype(BF16)
            outs[c] = _dot(qg[sl], s16) + _dot(a_in[c], v_new)
            r_last = c * chunk if rev else (c + 1) * chunk - 1
            g_last = gcum[r_last:r_last + 1, :]
            k_dec = (k[sl] * jnp.exp(g_last - gcum[sl])).astype(BF16)
            s = s * jnp.exp(g_last) + _dot_tn(k_dec, v_new)
        s_scr[d] = s
        o_ref[...] = jnp.concatenate(outs, axis=0)

    @pl.when(i == n_i - 1)
    def _():
        so_ref[...] = s_scr[...]


def _delta_stage(qkvn, gcol, grow, s0, n_seq, seq_len, row0, rg):
    ng = seq_len // rg
    b0 = row0 // rg

    def fwd_blk(b, i):
        return b0 + b * ng + i

    def bwd_blk(b, i):
        return b0 + b * ng + (ng - 1 - i)

    def specs(blk):
        return [pl.BlockSpec((rg, LANES), lambda b, h, i: (blk(b, i), h)),
                pl.BlockSpec((rg, LANES), lambda b, h, i: (blk(b, i), H_A + h)),
                pl.BlockSpec((rg, LANES), lambda b, h, i: (blk(b, i), 2 * H_A + h)),
                pl.BlockSpec((rg, LANES), lambda b, h, i: (blk(b, i), 0)),
                pl.BlockSpec((4 * H_A, rg), lambda b, h, i: (0, blk(b, i)))]

    st_spec = pl.BlockSpec((None, 2, None, DK_A, DV_A), lambda b, h, i: (b, 0, h, 0, 0))
    return pl.pallas_call(
        functools.partial(_delta_kernel, chunk=CHUNK),
        grid=(n_seq, H_A, ng),
        in_specs=specs(fwd_blk) + specs(bwd_blk) + [st_spec],
        out_specs=[pl.BlockSpec((rg, LANES), lambda b, h, i: (b * ng + i, h)),
                   pl.BlockSpec((rg, LANES), lambda b, h, i: (b * ng + (ng - 1 - i), h)),
                   st_spec],
        out_shape=[jax.ShapeDtypeStruct((n_seq * seq_len, H_A * DV_A), F32),
                   jax.ShapeDtypeStruct((n_seq * seq_len, H_A * DV_A), F32),
                   jax.ShapeDtypeStruct(s0.shape, F32)],
        scratch_shapes=[pltpu.VMEM((2, DK_A, DV_A), F32)],
        compiler_params=_cparams("parallel", "parallel", "arbitrary"), name="gated_delta",
    )(qkvn, qkvn, qkvn, gcol, grow, qkvn, qkvn, qkvn, gcol, grow, s0)


def _flash_kernel(*refs, n_parts, has_cache, diff, tk, lam_init):
    it = iter(refs)
    q_refs = [next(it) for _ in range(n_parts)]
    k_refs = [next(it) for _ in range(n_parts)]
    v_ref = next(it)
    kc_refs = [next(it) for _ in range(n_parts)] if has_cache else []
    vc_ref = next(it) if has_cache else None
    lam_ref = next(it) if diff else None
    o_ref = next(it)

    qs = [r[...] for r in q_refs]
    tq = qs[0].shape[0]
    if diff:
        lane = lax.broadcasted_iota(jnp.int32, (1, LANES), 1)
        q = qs[0]
        qs = [jnp.concatenate([jnp.where(lane < DQK_B, q, 0), jnp.where(lane >= DQK_B, q, 0)], axis=0)]
    nrow = qs[0].shape[0]

    def block(carry, ks, vv):
        m, l, acc = carry
        s = _dot_nt(qs[0], ks[0])
        for p in range(1, n_parts):
            s = s + _dot_nt(qs[p], ks[p])
        m_new = jnp.maximum(m, jnp.max(s, axis=-1, keepdims=True))
        alpha = jnp.exp2(m - m_new)
        p_ = jnp.exp2(s - m_new)
        l = alpha * l + jnp.sum(p_, axis=-1, keepdims=True)
        acc = alpha * acc + _dot(p_.astype(BF16), vv)
        return m_new, l, acc

    carry = (jnp.full((nrow, 1), -1e30, F32), jnp.zeros((nrow, 1), F32), jnp.zeros((nrow, v_ref.shape[1]), F32))
    lk = k_refs[0].shape[0]

    def body(t, carry):
        off = pl.multiple_of(t * tk, tk)
        return block(carry, [r[pl.ds(off, tk), :] for r in k_refs], v_ref[pl.ds(off, tk), :])

    carry = lax.fori_loop(0, lk // tk, body, carry)
    if has_cache:
        carry = block(carry, [r[...] for r in kc_refs], vc_ref[...])
    _, l, acc = carry
    o = acc / l
    if diff:
        lp = lam_ref[...]
        lam = (jnp.exp(jnp.sum(lp[0:1] * lp[1:2], axis=-1, keepdims=True))
               - jnp.exp(jnp.sum(lp[2:3] * lp[3:4], axis=-1, keepdims=True)) + lam_init)
        o = o[:tq] - lam * o[tq:]
    o_ref[...] = o.astype(o_ref.dtype)


def _flash(q_parts, k_parts, v_part, cache, lam, *, n_seq, n_heads, lq, lk, q_row0, k_row0, tq, tk, diff,
           lam_init, out_cols):
    nqt = lq // tq
    qb0, kb0 = q_row0 // tq, k_row0 // lk
    args, in_specs = [], []
    for arr, cf in q_parts:
        args.append(arr)
        in_specs.append(pl.BlockSpec((tq, LANES), lambda b, h, t, cf=cf: (qb0 + b * nqt + t, cf(h))))
    for arr, cf in k_parts + [v_part]:
        args.append(arr)
        in_specs.append(pl.BlockSpec((lk, LANES), lambda b, h, t, cf=cf: (kb0 + b, cf(h))))
    if cache is not None:
        c_k, c_v, c_rows, c_blk = cache
        for arr, cf in c_k + [c_v]:
            args.append(arr)
            in_specs.append(pl.BlockSpec((c_rows, LANES), lambda b, h, t, cf=cf: (c_blk(b, h), cf(h))))
    if diff:
        args.append(lam)
        in_specs.append(pl.BlockSpec(lam.shape, lambda b, h, t: (0, 0)))
    kern = functools.partial(_flash_kernel, n_parts=len(q_parts), has_cache=cache is not None, diff=diff, tk=tk,
                             lam_init=lam_init)
    return pl.pallas_call(
        kern, grid=(n_seq, n_heads, nqt), in_specs=in_specs,
        out_specs=pl.BlockSpec((tq, LANES), lambda b, h, t: (b * nqt + t, h)),
        out_shape=jax.ShapeDtypeStruct((n_seq * lq, out_cols), F32),
        compiler_params=_cparams("parallel", "parallel", "arbitrary"), name="flash_attention",
    )(*args)


def _evout_kernel(x_ref, of_ref, ob_ref, z_ref, od_ref, on_ref, sn_ref, w_ref, mods_ref, o_ref,
                  *, rows, gate_idx, lam_init):
    i = pl.program_id(0)
    d = x_ref.shape[1]
    oa = of_ref[...] + ob_ref[...]
    z = z_ref[...]
    od = od_ref[...]
    on = on_ref[...]
    sn = sn_ref[...]
    ya, yb = [], []
    for hh in range(H_A):
        sl = slice(hh * LANES, (hh + 1) * LANES)
        ya.append(_rmsnorm_rows(oa[:, sl], on) * _silu(z[:, sl]))
    for hh in range(H_B):
        sl = slice(hh * LANES, (hh + 1) * LANES)
        yb.append(_rmsnorm_rows(od[:, sl], sn) * (1.0 - lam_init))
    y = jnp.concatenate(ya + yb, axis=1).astype(BF16)
    o_ref[...] = x_ref[...] + _mod_chunk(mods_ref, rows.group(i), gate_idx, d) * _dot(y, w_ref[...])


def _evout(x, o_f, o_b, proj, z_col, o_d, onorm, subln, w_out, mods, rows, gate_idx, lam_init):
    n, d = x.shape
    tm = rows.tm
    hw = H_A * DV_A
    kern = functools.partial(_evout_kernel, rows=rows, gate_idx=gate_idx, lam_init=lam_init)
    return pl.pallas_call(
        kern, grid=(n // tm,),
        in_specs=[pl.BlockSpec((tm, d), lambda i: (i, 0)),
                  pl.BlockSpec((tm, hw), lambda i: (i, 0)),
                  pl.BlockSpec((tm, hw), lambda i: (i, 0)),
                  pl.BlockSpec((tm, hw), lambda i: (i, z_col)),
                  pl.BlockSpec((tm, hw), lambda i: (i, 0)),
                  pl.BlockSpec((1, LANES), lambda i: (0, 0)),
                  pl.BlockSpec((1, LANES), lambda i: (0, 0)),
                  pl.BlockSpec(w_out.shape, lambda i: (0, 0)),
                  pl.BlockSpec(mods.shape, lambda i: (0, 0))],
        out_specs=pl.BlockSpec((tm, d), lambda i: (i, 0)),
        out_shape=jax.ShapeDtypeStruct((n, d), F32),
        compiler_params=_cparams("parallel"), name="even_out",
    )(x, o_f, o_b, proj, o_d, onorm.reshape(1, LANES).astype(F32), subln.reshape(1, LANES).astype(F32), w_out, mods)


def _mr_kernel(x_ref, a_ref, w_ref, mods_ref, o_ref, *, rows, gate_idx):
    i = pl.program_id(0)
    d = x_ref.shape[1]
    gate = _mod_chunk(mods_ref, rows.group(i), gate_idx, d)
    o_ref[...] = x_ref[...] + gate * _dot(a_ref[...].astype(BF16), w_ref[...])


def _matmul_residual(x, a, w, mods, rows, gate_idx):
    n, d = x.shape
    tm = rows.tm
    return pl.pallas_call(
        functools.partial(_mr_kernel, rows=rows, gate_idx=gate_idx), grid=(n // tm,),
        in_specs=[pl.BlockSpec((tm, d), lambda i: (i, 0)),
                  pl.BlockSpec((tm, a.shape[1]), lambda i: (i, 0)),
                  pl.BlockSpec(w.shape, lambda i: (0, 0)),
                  pl.BlockSpec(mods.shape, lambda i: (0, 0))],
        out_specs=pl.BlockSpec((tm, d), lambda i: (i, 0)),
        out_shape=jax.ShapeDtypeStruct((n, d), F32),
        compiler_params=_cparams("parallel"), name="matmul_residual",
    )(x, a, w, mods)


def _modulated(x_ref, nw_ref, mods_ref, g, idx, d):
    x = _rmsnorm_rows(x_ref[...], nw_ref[...])
    return x * (1.0 + _mod_chunk(mods_ref, g, idx + 1, d)) + _mod_chunk(mods_ref, g, idx, d)


def _ffn_kernel(x_ref, nw_ref, mods_ref, wg_ref, wu_ref, wd_ref, o_ref, h_scr, acc_scr, *, rows, mod0):
    i, f = pl.program_id(0), pl.program_id(1)
    d = x_ref.shape[1]
    g = rows.group(i)

    @pl.when(f == 0)
    def _():
        h_scr[...] = _modulated(x_ref, nw_ref, mods_ref, g, mod0, d).astype(BF16)
        acc_scr[...] = jnp.zeros_like(acc_scr)

    h = h_scr[...]
    act = _silu(_dot(h, wg_ref[...])) * _dot(h, wu_ref[...])
    acc_scr[...] += _dot(act.astype(BF16), wd_ref[...])

    @pl.when(f == pl.num_programs(1) - 1)
    def _():
        o_ref[...] = x_ref[...] + _mod_chunk(mods_ref, g, mod0 + 2, d) * acc_scr[...]


def _ffn(x, norm_w, mods, w_gu, w_down, rows, mod0, tf):
    n, d = x.shape
    tm = rows.tm
    ff = w_down.shape[0]
    nf = ff // tf
    return pl.pallas_call(
        functools.partial(_ffn_kernel, rows=rows, mod0=mod0), grid=(n // tm, nf),
        in_specs=[pl.BlockSpec((tm, d), lambda i, f: (i, 0)),
                  pl.BlockSpec((1, d), lambda i, f: (0, 0)),
                  pl.BlockSpec(mods.shape, lambda i, f: (0, 0)),
                  pl.BlockSpec((d, tf), lambda i, f: (0, f)),
                  pl.BlockSpec((d, tf), lambda i, f: (0, nf + f)),
                  pl.BlockSpec((tf, d), lambda i, f: (f, 0))],
        out_specs=pl.BlockSpec((tm, d), lambda i, f: (i, 0)),
        out_shape=jax.ShapeDtypeStruct((n, d), F32),
        scratch_shapes=[pltpu.VMEM((tm, d), BF16), pltpu.VMEM((tm, d), F32)],
        compiler_params=_cparams("parallel", "arbitrary"), name="ffn",
    )(x, norm_w.reshape(1, d).astype(F32), mods, w_gu, w_gu, w_down)


def _moe_kernel(x_ref, nw_ref, mods_ref, rw_ref, wgu_ref, wd_ref, o_ref, h_scr, g_scr, acc_scr,
                *, rows, mod0, n_exp):
    i, e = pl.program_id(0), pl.program_id(1)
    d = x_ref.shape[1]
    g = rows.group(i)
    lane = lax.broadcasted_iota(jnp.int32, (1, LANES), 1)

    @pl.when(e == 0)
    def _():
        h = _modulated(x_ref, nw_ref, mods_ref, g, mod0, d)
        h_scr[...] = h.astype(BF16)
        acc_scr[...] = jnp.zeros_like(acc_scr)
        logits = jnp.where(lane < n_exp, _dot_f32(h, rw_ref[...]), -1e30)
        pe = jnp.exp(logits - jnp.max(logits, axis=-1, keepdims=True))
        probs = pe / jnp.sum(pe, axis=-1, keepdims=True)
        p1 = jnp.max(probs, axis=-1, keepdims=True)
        i1 = jnp.min(jnp.where(probs == p1, lane, LANES), axis=-1, keepdims=True)
        rest = jnp.where(lane == i1, -1.0, probs)
        p2 = jnp.max(rest, axis=-1, keepdims=True)
        i2 = jnp.min(jnp.where(rest == p2, lane, LANES), axis=-1, keepdims=True)
        den = p1 + p2
        g_scr[...] = jnp.where(lane == i1, p1 / den, 0.0) + jnp.where(lane == i2, p2 / den, 0.0)

    h = h_scr[...]
    gu = _dot(h, wgu_ref[...])
    ff = wd_ref.shape[0]
    act = _silu(gu[:, :ff]) * gu[:, ff:]
    y = _dot(act.astype(BF16), wd_ref[...])
    ge = jnp.sum(jnp.where(lane == e, g_scr[...], 0.0), axis=-1, keepdims=True)
    acc_scr[...] += ge * y

    @pl.when(e == n_exp - 1)
    def _():
        o_ref[...] = x_ref[...] + _mod_chunk(mods_ref, g, mod0 + 2, d) * acc_scr[...]


def _moe(x, norm_w, mods, router_w, w_gu, w_down, rows, mod0):
    n, d = x.shape
    tm = rows.tm
    n_exp, _, ff2 = w_gu.shape
    rw = jnp.zeros((d, LANES), F32).at[:, :n_exp].set(router_w.astype(F32))
    return pl.pallas_call(
        functools.partial(_moe_kernel, rows=rows, mod0=mod0, n_exp=n_exp), grid=(n // tm, n_exp),
        in_specs=[pl.BlockSpec((tm, d), lambda i, e: (i, 0)),
                  pl.BlockSpec((1, d), lambda i, e: (0, 0)),
                  pl.BlockSpec(mods.shape, lambda i, e: (0, 0)),
                  pl.BlockSpec((d, LANES), lambda i, e: (0, 0)),
                  pl.BlockSpec((None, d, ff2), lambda i, e: (e, 0, 0)),
                  pl.BlockSpec((None, ff2 // 2, d), lambda i, e: (e, 0, 0))],
        out_specs=pl.BlockSpec((tm, d), lambda i, e: (i, 0)),
        out_shape=jax.ShapeDtypeStruct((n, d), F32),
        scratch_shapes=[pltpu.VMEM((tm, d), BF16), pltpu.VMEM((tm, LANES), F32), pltpu.VMEM((tm, d), F32)],
        compiler_params=_cparams("parallel", "arbitrary"), name="moe",
    )(x, norm_w.reshape(1, d).astype(F32), mods, rw, w_gu, w_down)


def _final_kernel(x_ref, w_ref, o_ref):
    o_ref[...] = _rmsnorm_rows(x_ref[...], w_ref[...])


def _final_norm(x, w, tm):
    n, d = x.shape
    return pl.pallas_call(
        _final_kernel, grid=(n // tm,),
        in_specs=[pl.BlockSpec((tm, d), lambda i: (i, 0)), pl.BlockSpec((1, d), lambda i: (0, 0))],
        out_specs=pl.BlockSpec((tm, d), lambda i: (i, 0)),
        out_shape=jax.ShapeDtypeStruct((n, d), F32),
        compiler_params=_cparams("parallel"), name="final_norm",
    )(x, w.reshape(1, d).astype(F32))


def kernel(x_prompt, x_sample, state_delta, cache_diff_k, cache_diff_v, cache_mla_ckv, cache_mla_krope, c, c_ctx,
           mod_w, mod_b, norm_mix, norm_ffn, final_norm, ev_w_in, ev_conv_w, ev_a_log, ev_dt_bias, ev_onorm,
           ev_lambda, ev_subln, ev_w_out, ffn_w_gu, ffn_w_down, od_w_in, od_q_norm, od_kv_norm, od_w_uq, od_w_ukv,
           od_w_out, moe_router, moe_w_gu, moe_w_down):
    bp, lp, d = x_prompt.shape
    bs, ls, _ = x_sample.shape
    past = cache_diff_k.shape[3]
    depth = mod_w.shape[0]
    n_p, n_s = bp * lp, bs * ls
    n = n_p + n_s
    assert n_p % ls == 0 and past == lp

    rows = _Rows(n_p, n_s, ls, _tile(math.gcd(n_p, ls), 512))
    rows_c = _Rows(n_p, n_s, ls, _tile(math.gcd(lp, ls), 256))
    rg = _tile(math.gcd(lp, ls), 256)
    cos_t, sin_t = _rope_tables(rows, ls)

    x = jnp.concatenate([x_prompt.reshape(n_p, d), x_sample.reshape(n_s, d)], axis=0)
    n_grp = 1 + bs
    cc = jnp.zeros((2 * SUBLANES * ((n_grp + 15) // 16), d), F32).at[0].set(c_ctx).at[1:n_grp].set(c)
    mods_all = _mods(cc, mod_w, mod_b)

    hq = H_A * DK_A
    hb2 = H_B * 2 * DQK_B
    sm0 = 4 * hq
    qb0 = sm0 + 4 * H_A
    q_scale = (DQK_B ** -0.5) * LOG2E
    mla_scale = ((NOPE_C + ROPE_C) ** -0.5) * LOG2E
    sd_new, dk_new, dv_new, ckv_new, kr_new = [], [], [], [], []

    for layer in range(depth):
        j = layer // 2
        mods = mods_all[layer]
        if layer % 2 == 0:
            lam_init = 0.8 - 0.6 * math.exp(-0.3 * layer)
            w = ev_w_in[j]
            sm = w[:, sm0:qb0].reshape(d, 4, H_A).transpose(0, 2, 1).reshape(d, 4 * H_A)
            w_in = jnp.concatenate([w[:, :sm0], w[:, qb0:], sm, jnp.zeros((d, LANES - 4 * H_A), w.dtype)],
                                   axis=1).astype(BF16)
            tn = 512
            n_main = sm0 + 3 * hb2
            epi = [(False, 1.0)] * (sm0 // tn) + [(True, q_scale)] * (hb2 // tn) + [(True, 1.0)] * (hb2 // tn) \
                + [(False, 1.0)] * (hb2 // tn)
            proj = _norm_matmul(x, w_in[:, :n_main], rows, tn=tn, norm_w=norm_mix[layer], mods=mods, mod_idx=(0, 1),
                                rope=(cos_t, sin_t), epi=epi)
            small = _norm_matmul(x, w_in[:, n_main:], rows, tn=LANES, norm_w=norm_mix[layer], mods=mods,
                                 mod_idx=(0, 1))
            qkvn, gcol, grow = _conv_stage(proj, small, ev_conv_w[j], ev_a_log[j], ev_dt_bias[j], rows_c, 3 * hq)
            of_p, ob_p, s_p = _delta_stage(qkvn, gcol, grow, jnp.zeros((bp, 2, H_A, DK_A, DV_A), F32), bp, lp, 0, rg)
            of_s, ob_s, _ = _delta_stage(qkvn, gcol, grow, state_delta[:, j].astype(F32), bs, ls, n_p, rg)
            o_f = jnp.concatenate([of_p, of_s], axis=0)
            o_b = jnp.concatenate([ob_p, ob_s], axis=0)
            sd_new.append(s_p)
            qkv_b = proj[:, sm0:].astype(BF16)
            lam_p = ev_lambda[j].astype(F32)
            nb = hb2 // LANES
            tq_p = _tile(lp, 256)
            od_p = _flash([(qkv_b, lambda h: h)], [(qkv_b, lambda h: nb + h)], (qkv_b, lambda h: 2 * nb + h), None,
                          lam_p, n_seq=bp, n_heads=H_B, lq=lp, lk=lp, q_row0=0, k_row0=0, tq=tq_p,
                          tk=_tile(lp, 512), diff=True, lam_init=lam_init, out_cols=hb2)
            ck = cache_diff_k[:, j].reshape(bs * H_B * past, 2 * DQK_B).astype(BF16)
            cv = cache_diff_v[:, j].reshape(bs * H_B * past, DV_B).astype(BF16)
            cache = ([(ck, lambda h: 0)], (cv, lambda h: 0), past, lambda b, h: b * H_B + h)
            od_s = _flash([(qkv_b, lambda h: h)], [(qkv_b, lambda h: nb + h)], (qkv_b, lambda h: 2 * nb + h), cache,
                          lam_p, n_seq=bs, n_heads=H_B, lq=ls, lk=ls, q_row0=n_p, k_row0=n_p, tq=_tile(ls, 512),
                          tk=_tile(ls, 512), diff=True, lam_init=lam_init, out_cols=hb2)
            o_d = jnp.concatenate([od_p, od_s], axis=0)
            dk_new.append(proj[:n_p, sm0 + hb2:sm0 + 2 * hb2].reshape(bp, lp, H_B, 2 * DQK_B).transpose(0, 2, 1, 3))
            dv_new.append(proj[:n_p, sm0 + 2 * hb2:].reshape(bp, lp, H_B, DV_B).transpose(0, 2, 1, 3))
            x = _evout(x, o_f, o_b, proj, 3 * hq // (H_A * DV_A), o_d, ev_onorm[j], ev_subln[j],
                       ev_w_out[j].astype(BF16), mods, rows, 2, lam_init)
            ff = ffn_w_down.shape[1]
            x = _ffn(x, norm_ffn[layer], mods, ffn_w_gu[j].astype(BF16), ffn_w_down[j].astype(BF16), rows, 3,
                     ff // 2 if (ff // 2) % LANES == 0 else ff)
        else:
            n_in = Q_LORA + KV_LORA + ROPE_C
            pad = (-n_in) % LANES
            w_in = jnp.concatenate([od_w_in[j], jnp.zeros((d, pad), od_w_in.dtype)], axis=1).astype(BF16)
            nblk = (n_in + pad) // LANES
            epi = [(False, 1.0)] * (nblk - 1) + [(True, 1.0)]
            proj = _norm_matmul(x, w_in, rows, tn=LANES, norm_w=norm_mix[layer], mods=mods, mod_idx=(0, 1),
                                rope=(cos_t, sin_t), epi=epi)
            wq = od_w_uq[j].reshape(Q_LORA, H_C, NOPE_C + ROPE_C)
            wq_rope = jnp.concatenate([wq[:, :, NOPE_C:], jnp.zeros((Q_LORA, H_C, LANES - ROPE_C), wq.dtype)], axis=2)
            wq2 = jnp.concatenate([wq[:, :, :NOPE_C].reshape(Q_LORA, H_C * NOPE_C),
                                   wq_rope.reshape(Q_LORA, H_C * LANES)], axis=1).astype(BF16)
            epi_q = [(False, mla_scale)] * (H_C * NOPE_C // 512) + [(True, mla_scale)] * (H_C * LANES // 512)
            q_all = _norm_matmul(proj[:, :Q_LORA], wq2, rows, tn=512, norm_w=od_q_norm[j], rope=(cos_t, sin_t),
                                 epi=epi_q, out_dtype=BF16)
            kv_tok, ckv_n = _norm_matmul(proj[:, Q_LORA:Q_LORA + KV_LORA], od_w_ukv[j].astype(BF16), rows, tn=512,
                                         norm_w=od_kv_norm[j], emit_h=True, out_dtype=BF16)
            kr_tok = proj[:, Q_LORA + KV_LORA:].astype(BF16)
            ckv_new.append(ckv_n[:n_p].reshape(bp, lp, KV_LORA))
            kr_new.append(proj[:n_p, Q_LORA + KV_LORA:n_in].reshape(bp, lp, ROPE_C))
            rows_k = _Rows(bs * past, 0, past, _tile(past, 256))
            kv_c = _norm_matmul(cache_mla_ckv[:, j].reshape(bs * past, KV_LORA), od_w_ukv[j].astype(BF16), rows_k,
                                tn=512, out_dtype=BF16)
            kr_c = jnp.concatenate([cache_mla_krope[:, j].reshape(bs * past, ROPE_C),
                                    jnp.zeros((bs * past, LANES - ROPE_C), F32)], axis=1).astype(BF16)
            qp = [(q_all, lambda h: h), (q_all, lambda h: H_C + h)]
            kp = [(kv_tok, lambda h: 2 * h), (kr_tok, lambda h: 0)]
            vp = (kv_tok, lambda h: 2 * h + 1)
            o_p = _flash(qp, kp, vp, None, None, n_seq=bp, n_heads=H_C, lq=lp, lk=lp, q_row0=0, k_row0=0,
                         tq=_tile(lp, 256), tk=_tile(lp, 512), diff=False, lam_init=0.0, out_cols=H_C * V_C)
            cache = ([(kv_c, lambda h: 2 * h), (kr_c, lambda h: 0)], (kv_c, lambda h: 2 * h + 1), past,
                     lambda b, h: b)
            o_s = _flash(qp, kp, vp, cache, None, n_seq=bs, n_heads=H_C, lq=ls, lk=ls, q_row0=n_p, k_row0=n_p,
                         tq=_tile(ls, 512), tk=_tile(ls, 512), diff=False, lam_init=0.0, out_cols=H_C * V_C)
            o = jnp.concatenate([o_p, o_s], axis=0)
            x = _matmul_residual(x, o, od_w_out[j].astype(BF16), mods, rows, 2)
            x = _moe(x, norm_ffn[layer], mods, moe_router[j], moe_w_gu[j].astype(BF16), moe_w_down[j].astype(BF16),
                     rows, 3)

    y = _final_norm(x, final_norm, rows.tm)
    return (y[:n_p].reshape(bp, lp, d), y[n_p:].reshape(bs, ls, d), jnp.stack(sd_new, axis=1),
            jnp.stack(dk_new, axis=1), jnp.stack(dv_new, axis=1), jnp.stack(ckv_new, axis=1),
            jnp.stack(kr_new, axis=1))
```

```python
import functools
import math

import numpy as np
import jax
import jax.numpy as jnp
from jax import lax
from jax.experimental import pallas as pl
from jax.experimental.pallas import tpu as pltpu

F32 = jnp.float32
BF16 = jnp.bfloat16

EPS = 1e-6
LOG2E = 1.4426950408889634
GRID_W = 64
ROPE_BASE = 10000.0
H_A, DK_A, DV_A = 4, 128, 128
CONV_K = 3
CHUNK = 64
H_B, DQK_B, DV_B = 4, 64, 128
H_C, NOPE_C, ROPE_C, V_C = 8, 128, 64, 128
Q_LORA, KV_LORA = 384, 256
LANES = 128
SUBLANES = 8
VMEM_LIMIT_BYTES = 56 * 1024 * 1024


def _cparams(*sem):
    return pltpu.CompilerParams(dimension_semantics=sem, vmem_limit_bytes=VMEM_LIMIT_BYTES)


def _dot(a, b):
    return jnp.dot(a, b, preferred_element_type=F32)


def _dot_nt(a, b):
    return lax.dot_general(a, b, (((1,), (1,)), ((), ())), preferred_element_type=F32)


def _dot_tn(a, b):
    return lax.dot_general(a, b, (((0,), (0,)), ((), ())), preferred_element_type=F32)


def _split3(x):
    hi = x.astype(BF16)
    r = x - hi.astype(F32)
    mid = r.astype(BF16)
    lo = (r - mid.astype(F32)).astype(BF16)
    return hi, mid, lo


def _dot_f32(a, b):
    ah, am, al = _split3(a)
    bh, bm, bl = _split3(b)
    return (_dot(ah, bh) + (_dot(ah, bm) + _dot(am, bh))
            + (_dot(am, bm) + _dot(ah, bl) + _dot(al, bh)))


def _silu(x):
    return x * jax.nn.sigmoid(x)


def _tile(n, pref, mult=SUBLANES):
    t = min(n, pref)
    while t > mult and (n % t or t % mult):
        t -= mult
    assert n % t == 0, (n, pref)
    return t


class _Rows:
    def __init__(self, n_ctx, n_lat, lat_len, tm):
        assert n_ctx % tm == 0 and lat_len % tm == 0
        self.tm = tm
        self.n = n_ctx + n_lat
        self.ctx_tiles = n_ctx // tm
        self.seq_tiles = lat_len // tm

    def group(self, i):
        return jnp.where(i < self.ctx_tiles, 0, 1 + jnp.maximum(i - self.ctx_tiles, 0) // self.seq_tiles)

    def pos_block(self, i):
        return jnp.where(i < self.ctx_tiles, 0, 1 + jnp.maximum(i - self.ctx_tiles, 0) % self.seq_tiles)


def _mod_kernel(c_ref, w_ref, b_ref, o_ref):
    o_ref[...] = _dot_f32(_silu(c_ref[...]), w_ref[...]) + b_ref[...]


def _mods(cc, mod_w, mod_b):
    depth, d, n6 = mod_w.shape
    tn = _tile(n6, 512, LANES)
    return pl.pallas_call(
        _mod_kernel,
        grid=(depth, n6 // tn),
        in_specs=[pl.BlockSpec(cc.shape, lambda l, j: (0, 0)),
                  pl.BlockSpec((None, d, tn), lambda l, j: (l, 0, j)),
                  pl.BlockSpec((None, 1, tn), lambda l, j: (l, 0, j))],
        out_specs=pl.BlockSpec((None, cc.shape[0], tn), lambda l, j: (l, 0, j)),
        out_shape=jax.ShapeDtypeStruct((depth, cc.shape[0], n6), F32),
        compiler_params=_cparams("parallel", "parallel"),
        name="mods",
    )(cc, mod_w, mod_b.reshape(depth, 1, n6))


def _mod_chunk(mods_ref, g, idx, d):
    return mods_ref[pl.ds(g, 1), idx * d:(idx + 1) * d]


def _rmsnorm_rows(x, w):
    ms = jnp.mean(x * x, axis=-1, keepdims=True)
    return x * lax.rsqrt(ms + EPS) * w


def _rope(x, cos, sin_signed):
    lane = lax.broadcasted_iota(jnp.int32, (1, LANES), 1)
    low = (lane % 32) < 16
    outs = []
    for c in range(x.shape[1] // LANES):
        xc = x[:, c * LANES:(c + 1) * LANES]
        fwd = pltpu.roll(xc, LANES - 16, axis=1)
        bwd = pltpu.roll(xc, 16, axis=1)
        outs.append(xc * cos + jnp.where(low, fwd, bwd) * sin_signed)
    return outs[0] if len(outs) == 1 else jnp.concatenate(outs, axis=1)


def _rope_tables(rows, lat_len):
    pos = jnp.arange(lat_len)
    r = (pos // GRID_W).astype(F32)
    c = (pos % GRID_W).astype(F32)
    inv = ROPE_BASE ** (-jnp.arange(0, 32, 2, dtype=F32) / 32)
    ang_r = r[:, None] * inv[None, :]
    ang_c = c[:, None] * inv[None, :]
    ang = jnp.concatenate([ang_r, ang_r, ang_c, ang_c] * 2, axis=-1)
    sign = jnp.where((jnp.arange(LANES) % 32) < 16, -1.0, 1.0).astype(F32)
    cos = jnp.concatenate([jnp.ones((rows.tm, LANES), F32), jnp.cos(ang)], axis=0)
    sin = jnp.concatenate([jnp.zeros((rows.tm, LANES), F32), jnp.sin(ang) * sign[None, :]], axis=0)
    return cos, sin


def _nm_kernel(*refs, rows, d_mod, has_norm, mod_idx, has_rope, epi, emit_h):
    it = iter(refs)
    x_ref = next(it)
    w_ref = next(it)
    nw_ref = next(it) if has_norm else None
    mods_ref = next(it) if mod_idx is not None else None
    cos_ref, sin_ref = (next(it), next(it)) if has_rope else (None, None)
    o_ref = next(it)
    h_out_ref = next(it) if emit_h else None
    h_scr = next(it)
    i, j = pl.program_id(0), pl.program_id(1)

    @pl.when(j == 0)
    def _():
        x = x_ref[...].astype(F32)
        if has_norm:
            x = _rmsnorm_rows(x, nw_ref[...])
        if mod_idx is not None:
            g = rows.group(i)
            x = x * (1.0 + _mod_chunk(mods_ref, g, mod_idx[1], d_mod)) + _mod_chunk(mods_ref, g, mod_idx[0], d_mod)
        h_scr[...] = x.astype(BF16)
        if emit_h:
            h_out_ref[...] = x

    acc = _dot(h_scr[...], w_ref[...])
    kinds = sorted(set(epi))
    for kind in kinds:
        cond = None
        for jj, e in enumerate(epi):
            if e == kind:
                cond = (j == jj) if cond is None else (cond | (j == jj))

        def _store(kind=kind):
            y = acc
            if kind[0]:
                y = _rope(y, cos_ref[...], sin_ref[...])
            if kind[1] != 1.0:
                y = y * kind[1]
            o_ref[...] = y.astype(o_ref.dtype)

        if len(kinds) == 1:
            _store()
        else:
            pl.when(cond)(_store)


def _norm_matmul(x, w, rows, *, tn, norm_w=None, mods=None, mod_idx=None, rope=None, epi=None,
                 emit_h=False, out_dtype=F32):
    n, k = x.shape
    nout = w.shape[1]
    nj = nout // tn
    tm = rows.tm
    epi = tuple(epi) if epi is not None else ((False, 1.0),) * nj
    assert len(epi) == nj and n % tm == 0
    has_rope = any(e[0] for e in epi)
    args = [x, w]
    in_specs = [pl.BlockSpec((tm, k), lambda i, j: (i, 0)), pl.BlockSpec((k, tn), lambda i, j: (0, j))]
    if norm_w is not None:
        args.append(norm_w.reshape(1, k).astype(F32))
        in_specs.append(pl.BlockSpec((1, k), lambda i, j: (0, 0)))
    if mod_idx is not None:
        args.append(mods)
        in_specs.append(pl.BlockSpec(mods.shape, lambda i, j: (0, 0)))
    if has_rope:
        args += list(rope)
        in_specs += [pl.BlockSpec((tm, LANES), lambda i, j: (rows.pos_block(i), 0))] * 2
    out_shape = [jax.ShapeDtypeStruct((n, nout), out_dtype)]
    out_specs = [pl.BlockSpec((tm, tn), lambda i, j: (i, j))]
    if emit_h:
        out_shape.append(jax.ShapeDtypeStruct((n, k), F32))
        out_specs.append(pl.BlockSpec((tm, k), lambda i, j: (i, 0)))
    kern = functools.partial(_nm_kernel, rows=rows, d_mod=k, has_norm=norm_w is not None, mod_idx=mod_idx,
                             has_rope=has_rope, epi=epi, emit_h=emit_h)
    out = pl.pallas_call(
        kern, grid=(n // tm, nj), in_specs=in_specs, out_specs=out_specs, out_shape=out_shape,
        scratch_shapes=[pltpu.VMEM((tm, k), BF16)],
        compiler_params=_cparams("parallel", "arbitrary"), name="norm_matmul",
    )(*args)
    return out if emit_h else out[0]


def _conv_kernel(x_ref, prev_ref, next_ref, sm_ref, cw_ref, alog_ref, dt_ref, o_ref, gc_ref, gr_ref,
                 *, tiles_ctx, tiles_seq, halo, n_qk):
    i = pl.program_id(0)
    t = jnp.maximum(i - tiles_ctx, 0) % tiles_seq
    lat = i >= tiles_ctx
    has_prev = lat & (t > 0)
    has_next = lat & (t < tiles_seq - 1)
    x = x_ref[...]
    tm = x.shape[0]
    row = lax.broadcasted_iota(jnp.int32, (tm, 1), 0)
    p_row = jnp.where(has_prev, prev_ref[halo - 1:halo, :], 0.0)
    n_row = jnp.where(has_next, next_ref[0:1, :], 0.0)
    x_prev = jnp.where(row == 0, p_row, pltpu.roll(x, 1, axis=0))
    x_next = jnp.where(row == tm - 1, n_row, pltpu.roll(x, tm - 1, axis=0))
    cw = cw_ref[...]
    y = _silu(x_prev * cw[0:1, :] + x * cw[1:2, :] + x_next * cw[2:3, :])
    outs = []
    for hh in range(y.shape[1] // LANES):
        yh = y[:, hh * LANES:(hh + 1) * LANES]
        if hh < n_qk:
            yh = yh * lax.rsqrt(jnp.sum(yh * yh, axis=-1, keepdims=True) + EPS)
            if hh < n_qk // 2:
                yh = yh * (DK_A ** -0.5)
        outs.append(yh)
    o_ref[...] = jnp.concatenate(outs, axis=1)
    s = sm_ref[...]
    lane = lax.broadcasted_iota(jnp.int32, (1, LANES), 1)
    is_g = (lane % 4) < 2
    z = s + dt_ref[...]
    softplus = jnp.maximum(z, 0.0) + jnp.log(1.0 + jnp.exp(-jnp.abs(z)))
    g = jnp.where(is_g, -jnp.exp(alog_ref[...]) * softplus, jax.nn.sigmoid(s))
    gc_ref[...] = g
    gr_ref[...] = jnp.transpose(g)[:gr_ref.shape[0], :]


def _conv_stage(proj, small, conv_w, a_log, dt_bias, rows_c, n_conv):
    n = proj.shape[0]
    tm = rows_c.tm
    halo = 16
    hb = tm // halo
    nblk16 = n // halo
    alog_row = jnp.zeros((1, LANES), F32).at[0, :4 * H_A].set(
        jnp.stack([a_log[0], a_log[1], a_log[0], a_log[1]], axis=-1).reshape(-1).astype(F32))
    dt_row = jnp.zeros((1, LANES), F32).at[0, :4 * H_A].set(
        jnp.stack([dt_bias[0], dt_bias[1], jnp.zeros_like(dt_bias[0]), jnp.zeros_like(dt_bias[0])],
                  axis=-1).reshape(-1).astype(F32))
    kern = functools.partial(_conv_kernel, tiles_ctx=rows_c.ctx_tiles, tiles_seq=rows_c.seq_tiles, halo=halo,
                             n_qk=2 * H_A)
    return pl.pallas_call(
        kern, grid=(n // tm,),
        in_specs=[pl.BlockSpec((tm, n_conv), lambda i: (i, 0)),
                  pl.BlockSpec((halo, n_conv), lambda i: (jnp.maximum(i * hb - 1, 0), 0)),
                  pl.BlockSpec((halo, n_conv), lambda i: (jnp.minimum((i + 1) * hb, nblk16 - 1), 0)),
                  pl.BlockSpec((tm, LANES), lambda i: (i, 0)),
                  pl.BlockSpec((CONV_K, n_conv), lambda i: (0, 0)),
                  pl.BlockSpec((1, LANES), lambda i: (0, 0)),
                  pl.BlockSpec((1, LANES), lambda i: (0, 0))],
        out_specs=[pl.BlockSpec((tm, n_conv), lambda i: (i, 0)),
                   pl.BlockSpec((tm, LANES), lambda i: (i, 0)),
                   pl.BlockSpec((4 * H_A, tm), lambda i: (0, i))],
        out_shape=[jax.ShapeDtypeStruct((n, n_conv), F32),
                   jax.ShapeDtypeStruct((n, LANES), F32),
                   jax.ShapeDtypeStruct((4 * H_A, n), F32)],
        compiler_params=_cparams("parallel"), name="conv_gate",
    )(proj, proj, proj, small, conv_w.astype(F32), alog_row, dt_row)


def _split2(x):
    hi = x.astype(BF16)
    return hi, (x - hi.astype(F32)).astype(BF16)


SOLVE_BLOCK = 2 * SUBLANES
MOE_ROW_BLOCK = 128
DELTA_HEADS_PER_STEP = 4


def _tri_solve_many(ms, rs, revs):
    c, width = rs[0].shape
    blk = SOLVE_BLOCK
    nb, gpb, ng = c // blk, blk // SUBLANES, c // SUBLANES
    col = lax.broadcasted_iota(jnp.int32, (blk, c), 1)
    mgs = [[m[SUBLANES * g:SUBLANES * (g + 1), :] for g in range(ng)] for m in ms]
    xgs = [[r[SUBLANES * g:SUBLANES * (g + 1), :] for g in range(ng)] for r in rs]
    zero_blk = jnp.zeros((blk, width), BF16)
    fin_h = [[zero_blk] * nb for _ in ms]
    fin_l = [[zero_blk] * nb for _ in ms]
    for bi in range(nb):
        blocks = [nb - 1 - bi if rev else bi for rev in revs]
        if bi > 0:
            for s, (rev, b) in enumerate(zip(revs, blocks)):
                done = (col >= (b + 1) * blk) if rev else (col < b * blk)
                lh, ll = _split2(jnp.where(done, ms[s][b * blk:(b + 1) * blk, :], 0.0))
                xh = jnp.concatenate(fin_h[s], axis=0)
                xl = jnp.concatenate(fin_l[s], axis=0)
                upd = _dot(lh, xh) + (_dot(lh, xl) + _dot(ll, xh))
                for gg in range(gpb):
                    g = b * gpb + gg
                    xgs[s][g] = xgs[s][g] - upd[SUBLANES * gg:SUBLANES * (gg + 1), :]
        for t in range(blk - 1):
            for s, (rev, b) in enumerate(zip(revs, blocks)):
                j = b * blk + (blk - 1 - t if rev else t)
                xj = xgs[s][j // SUBLANES][j % SUBLANES:j % SUBLANES + 1, :]
                groups = range(b * gpb, (j - 1) // SUBLANES + 1) if rev else range((j + 1) // SUBLANES, (b + 1) * gpb)
                for g in groups:
                    xgs[s][g] = xgs[s][g] - mgs[s][g][:, j:j + 1] * xj
        if bi < nb - 1:
            for s, b in enumerate(blocks):
                fin_h[s][b], fin_l[s][b] = _split2(jnp.concatenate(xgs[s][b * gpb:(b + 1) * gpb], axis=0))
    return [jnp.concatenate(xg, axis=0) for xg in xgs]


def _delta_kernel(qf, kf, vf, gcf, grf, qb, kb, vb, gcb, grb, s0_ref, of_ref, ob_ref, so_ref, s_scr, *, chunk, hpb):
    h0 = pl.program_id(1) * hpb
    i = pl.program_id(2)
    n_i = pl.num_programs(2)

    @pl.when(i == 0)
    def _():
        s_scr[...] = s0_ref[...]

    rg = qf.shape[0]
    nc = rg // chunk
    rowi = lax.broadcasted_iota(jnp.int32, (rg, rg), 0)
    coli = lax.broadcasted_iota(jnp.int32, (rg, rg), 1)
    same = (rowi // chunk) == (coli // chunk)
    lane = lax.broadcasted_iota(jnp.int32, (1, LANES), 1)
    r_c = lax.broadcasted_iota(jnp.int32, (chunk, chunk), 0)
    c_c = lax.broadcasted_iota(jnp.int32, (chunk, chunk), 1)

    prep = []
    for d, (q_ref, k_ref, v_ref, gc_ref, gr_ref) in enumerate(((qf, kf, vf, gcf, grf), (qb, kb, vb, gcb, grb))):
        rev = d == 1
        incl_big = same & ((rowi <= coli) if rev else (rowi >= coli))
        lm = jnp.where(incl_big, 1.0, 0.0).astype(BF16)
        gcols = gc_ref[...]
        grows = gr_ref[...]
        gh, gm, gl_ = _split3(gcols)
        cum_c = _dot(lm, gh) + _dot(lm, gm) + _dot(lm, gl_)
        th, tm_, tl = _split3(grows)
        cum_r = _dot_nt(th, lm) + _dot_nt(tm_, lm) + _dot_nt(tl, lm)

        def col(a, ln):
            return jnp.sum(jnp.where(lane == ln, a, 0.0), axis=1, keepdims=True)

        sub = lax.broadcasted_iota(jnp.int32, (cum_r.shape[0], 1), 0)
        incl = (r_c <= c_c) if rev else (r_c >= c_c)
        strict = (r_c < c_c) if rev else (r_c > c_c)
        for hh in range(hpb):
            h = h0 + hh
            hs = slice(hh * LANES, (hh + 1) * LANES)
            gcum = col(cum_c, 4 * h + d)
            beta = col(gcols, 4 * h + 2 + d)
            grow = jnp.sum(jnp.where(sub == 4 * h + d, cum_r, 0.0), axis=0, keepdims=True)
            q = q_ref[:, hs]
            k = k_ref[:, hs]
            v = v_ref[:, hs]
            kbeta = k * beta
            eg = jnp.exp(gcum)
            rhs = jnp.concatenate([v * beta, kbeta * eg], axis=1)
            kb16 = kbeta.astype(BF16)
            k16 = k.astype(BF16)
            q16 = q.astype(BF16)
            ms, rs, a_in = [], [], []
            for c in range(nc):
                sl = slice(c * chunk, (c + 1) * chunk)
                e = jnp.exp(jnp.where(incl, gcum[sl] - grow[:, sl], 0.0))
                kk = _dot_nt(kb16[sl], k16[sl])
                qk = _dot_nt(q16[sl], k16[sl])
                a_in.append(jnp.where(incl, qk * e, 0.0).astype(BF16))
                ms.append(jnp.where(strict, kk * e, 0.0))
                rs.append(rhs[sl])
            prep.append((rev, hh, ms, rs, a_in, q * eg, k, gcum))

    sols = _tri_solve_many([m for p in prep for m in p[2]], [r for p in prep for r in p[3]],
                           [p[0] for p in prep for _ in range(nc)])

    all_terms = []
    for n_u, (rev, hh, _, _, a_in, qg, k, gcum) in enumerate(prep):
        xs = sols[n_u * nc:(n_u + 1) * nc]
        terms = []
        for c in range(nc):
            sl = slice(c * chunk, (c + 1) * chunk)
            u16 = xs[c][:, :DV_A].astype(BF16)
            w16 = xs[c][:, DV_A:].astype(BF16)
            r_last = c * chunk if rev else (c + 1) * chunk - 1
            g_last = gcum[r_last:r_last + 1, :]
            k_dec = (k[sl] * jnp.exp(g_last - gcum[sl])).astype(BF16)
            terms.append(((qg[sl] - _dot(a_in[c], w16)).astype(BF16), _dot(a_in[c], u16),
                          _dot_tn(k_dec, w16).astype(BF16), _dot_tn(k_dec, u16), jnp.exp(g_last)))
        all_terms.append(terms)

    states = [s_scr[1 if p[0] else 0, p[1]] for p in prep]
    outs = [[None] * nc for _ in prep]
    for t in range(nc):
        for n_u, p in enumerate(prep):
            c = nc - 1 - t if p[0] else t
            o_s, o_0, w_s, q_0, decay = all_terms[n_u][c]
            s16 = states[n_u].astype(BF16)
            outs[n_u][c] = o_0 + _dot(o_s, s16)
            states[n_u] = states[n_u] * decay - _dot(w_s, s16) + q_0
    for n_u, (rev, hh) in enumerate((p[0], p[1]) for p in prep):
        s_scr[1 if rev else 0, hh] = states[n_u]
        (ob_ref if rev else of_ref)[:, hh * LANES:(hh + 1) * LANES] = jnp.concatenate(outs[n_u], axis=0)

    @pl.when(i == n_i - 1)
    def _():
        so_ref[...] = s_scr[...]


def _delta_stage(qkvn, gcol, grow, s0, n_seq, seq_len, row0, rg):
    ng = seq_len // rg
    b0 = row0 // rg

    def fwd_blk(b, i):
        return b0 + b * ng + i

    def bwd_blk(b, i):
        return b0 + b * ng + (ng - 1 - i)

    hpb = DELTA_HEADS_PER_STEP
    nhp = H_A // hpb
    hw = hpb * LANES

    def specs(blk):
        return [pl.BlockSpec((rg, hw), lambda b, h, i: (blk(b, i), h)),
                pl.BlockSpec((rg, hw), lambda b, h, i: (blk(b, i), nhp + h)),
                pl.BlockSpec((rg, hw), lambda b, h, i: (blk(b, i), 2 * nhp + h)),
                pl.BlockSpec((rg, LANES), lambda b, h, i: (blk(b, i), 0)),
                pl.BlockSpec((4 * H_A, rg), lambda b, h, i: (0, blk(b, i)))]

    st_spec = pl.BlockSpec((None, 2, hpb, DK_A, DV_A), lambda b, h, i: (b, 0, h, 0, 0))
    return pl.pallas_call(
        functools.partial(_delta_kernel, chunk=CHUNK, hpb=hpb),
        grid=(n_seq, nhp, ng),
        in_specs=specs(fwd_blk) + specs(bwd_blk) + [st_spec],
        out_specs=[pl.BlockSpec((rg, hw), lambda b, h, i: (b * ng + i, h)),
                   pl.BlockSpec((rg, hw), lambda b, h, i: (b * ng + (ng - 1 - i), h)),
                   st_spec],
        out_shape=[jax.ShapeDtypeStruct((n_seq * seq_len, H_A * DV_A), F32),
                   jax.ShapeDtypeStruct((n_seq * seq_len, H_A * DV_A), F32),
                   jax.ShapeDtypeStruct(s0.shape, F32)],
        scratch_shapes=[pltpu.VMEM((2, hpb, DK_A, DV_A), F32)],
        compiler_params=_cparams("parallel", "parallel", "arbitrary"), name="gated_delta",
    )(qkvn, qkvn, qkvn, gcol, grow, qkvn, qkvn, qkvn, gcol, grow, s0)


def _flash_kernel(*refs, n_parts, has_cache, diff, tk, lam_init):
    it = iter(refs)
    q_refs = [next(it) for _ in range(n_parts)]
    k_refs = [next(it) for _ in range(n_parts)]
    v_ref = next(it)
    kc_refs = [next(it) for _ in range(n_parts)] if has_cache else []
    vc_ref = next(it) if has_cache else None
    lam_ref = next(it) if diff else None
    o_ref = next(it)

    qs = [r[...] for r in q_refs]
    tq = qs[0].shape[0]
    if diff:
        lane = lax.broadcasted_iota(jnp.int32, (1, LANES), 1)
        q = qs[0]
        qs = [jnp.concatenate([jnp.where(lane < DQK_B, q, 0), jnp.where(lane >= DQK_B, q, 0)], axis=0)]
    nrow = qs[0].shape[0]
    q_cat = qs[0] if n_parts == 1 else jnp.concatenate(qs, axis=1)

    def block(carry, ks, vv):
        m, l, acc = carry
        s = _dot_nt(q_cat, ks[0] if n_parts == 1 else jnp.concatenate(ks, axis=1))
        m_new = jnp.maximum(m, jnp.max(s, axis=-1, keepdims=True))
        alpha = jnp.exp2(m - m_new)
        p_ = jnp.exp2(s - m_new)
        l = alpha * l + jnp.sum(p_, axis=-1, keepdims=True)
        acc = alpha * acc + _dot(p_.astype(BF16), vv)
        return m_new, l, acc

    carry = (jnp.full((nrow, 1), -1e30, F32), jnp.zeros((nrow, 1), F32), jnp.zeros((nrow, v_ref.shape[1]), F32))
    lk = k_refs[0].shape[0]

    def body(t, carry):
        off = pl.multiple_of(t * tk, tk)
        return block(carry, [r[pl.ds(off, tk), :] for r in k_refs], v_ref[pl.ds(off, tk), :])

    carry = lax.fori_loop(0, lk // tk, body, carry, unroll=True)
    if has_cache:
        carry = block(carry, [r[...] for r in kc_refs], vc_ref[...])
    _, l, acc = carry
    o = acc / l
    if diff:
        lp = lam_ref[...]
        lam = (jnp.exp(jnp.sum(lp[0:1] * lp[1:2], axis=-1, keepdims=True))
               - jnp.exp(jnp.sum(lp[2:3] * lp[3:4], axis=-1, keepdims=True)) + lam_init)
        o = o[:tq] - lam * o[tq:]
    o_ref[...] = o.astype(o_ref.dtype)


def _flash(q_parts, k_parts, v_part, cache, lam, *, n_seq, n_heads, lq, lk, q_row0, k_row0, tq, tk, diff,
           lam_init, out_cols):
    nqt = lq // tq
    qb0, kb0 = q_row0 // tq, k_row0 // lk
    args, in_specs = [], []
    for arr, cf in q_parts:
        args.append(arr)
        in_specs.append(pl.BlockSpec((tq, LANES), lambda b, h, t, cf=cf: (qb0 + b * nqt + t, cf(h))))
    for arr, cf in k_parts + [v_part]:
        args.append(arr)
        in_specs.append(pl.BlockSpec((lk, LANES), lambda b, h, t, cf=cf: (kb0 + b, cf(h))))
    if cache is not None:
        c_k, c_v, c_rows, c_blk = cache
        for arr, cf in c_k + [c_v]:
            args.append(arr)
            in_specs.append(pl.BlockSpec((c_rows, LANES), lambda b, h, t, cf=cf: (c_blk(b, h), cf(h))))
    if diff:
        args.append(lam)
        in_specs.append(pl.BlockSpec(lam.shape, lambda b, h, t: (0, 0)))
    kern = functools.partial(_flash_kernel, n_parts=len(q_parts), has_cache=cache is not None, diff=diff, tk=tk,
                             lam_init=lam_init)
    return pl.pallas_call(
        kern, grid=(n_seq, n_heads, nqt), in_specs=in_specs,
        out_specs=pl.BlockSpec((tq, LANES), lambda b, h, t: (b * nqt + t, h)),
        out_shape=jax.ShapeDtypeStruct((n_seq * lq, out_cols), F32),
        compiler_params=_cparams("parallel", "parallel", "arbitrary"), name="flash_attention",
    )(*args)


def _evout_kernel(x_ref, of_ref, ob_ref, z_ref, od_ref, on_ref, sn_ref, w_ref, mods_ref, o_ref,
                  *, rows, gate_idx, lam_init):
    i = pl.program_id(0)
    d = x_ref.shape[1]
    oa = of_ref[...] + ob_ref[...]
    z = z_ref[...]
    od = od_ref[...]
    on = on_ref[...]
    sn = sn_ref[...]
    ya, yb = [], []
    for hh in range(H_A):
        sl = slice(hh * LANES, (hh + 1) * LANES)
        ya.append(_rmsnorm_rows(oa[:, sl], on) * _silu(z[:, sl]))
    for hh in range(H_B):
        sl = slice(hh * LANES, (hh + 1) * LANES)
        yb.append(_rmsnorm_rows(od[:, sl], sn) * (1.0 - lam_init))
    y = jnp.concatenate(ya + yb, axis=1).astype(BF16)
    o_ref[...] = x_ref[...] + _mod_chunk(mods_ref, rows.group(i), gate_idx, d) * _dot(y, w_ref[...])


def _evout(x, o_f, o_b, proj, z_col, o_d, onorm, subln, w_out, mods, rows, gate_idx, lam_init):
    n, d = x.shape
    tm = rows.tm
    hw = H_A * DV_A
    kern = functools.partial(_evout_kernel, rows=rows, gate_idx=gate_idx, lam_init=lam_init)
    return pl.pallas_call(
        kern, grid=(n // tm,),
        in_specs=[pl.BlockSpec((tm, d), lambda i: (i, 0)),
                  pl.BlockSpec((tm, hw), lambda i: (i, 0)),
                  pl.BlockSpec((tm, hw), lambda i: (i, 0)),
                  pl.BlockSpec((tm, hw), lambda i: (i, z_col)),
                  pl.BlockSpec((tm, hw), lambda i: (i, 0)),
                  pl.BlockSpec((1, LANES), lambda i: (0, 0)),
                  pl.BlockSpec((1, LANES), lambda i: (0, 0)),
                  pl.BlockSpec(w_out.shape, lambda i: (0, 0)),
                  pl.BlockSpec(mods.shape, lambda i: (0, 0))],
        out_specs=pl.BlockSpec((tm, d), lambda i: (i, 0)),
        out_shape=jax.ShapeDtypeStruct((n, d), F32),
        compiler_params=_cparams("parallel"), name="even_out",
    )(x, o_f, o_b, proj, o_d, onorm.reshape(1, LANES).astype(F32), subln.reshape(1, LANES).astype(F32), w_out, mods)


def _mr_kernel(x_ref, a_ref, w_ref, mods_ref, o_ref, *, rows, gate_idx):
    i = pl.program_id(0)
    d = x_ref.shape[1]
    gate = _mod_chunk(mods_ref, rows.group(i), gate_idx, d)
    o_ref[...] = x_ref[...] + gate * _dot(a_ref[...].astype(BF16), w_ref[...])


def _matmul_residual(x, a, w, mods, rows, gate_idx):
    n, d = x.shape
    tm = rows.tm
    return pl.pallas_call(
        functools.partial(_mr_kernel, rows=rows, gate_idx=gate_idx), grid=(n // tm,),
        in_specs=[pl.BlockSpec((tm, d), lambda i: (i, 0)),
                  pl.BlockSpec((tm, a.shape[1]), lambda i: (i, 0)),
                  pl.BlockSpec(w.shape, lambda i: (0, 0)),
                  pl.BlockSpec(mods.shape, lambda i: (0, 0))],
        out_specs=pl.BlockSpec((tm, d), lambda i: (i, 0)),
        out_shape=jax.ShapeDtypeStruct((n, d), F32),
        compiler_params=_cparams("parallel"), name="matmul_residual",
    )(x, a, w, mods)


def _modulated(x_ref, nw_ref, mods_ref, g, idx, d):
    x = _rmsnorm_rows(x_ref[...], nw_ref[...])
    return x * (1.0 + _mod_chunk(mods_ref, g, idx + 1, d)) + _mod_chunk(mods_ref, g, idx, d)


def _ffn_kernel(x_ref, nw_ref, mods_ref, wg_ref, wu_ref, wd_ref, o_ref, h_scr, acc_scr, *, rows, mod0):
    i, f = pl.program_id(0), pl.program_id(1)
    d = x_ref.shape[1]
    g = rows.group(i)

    @pl.when(f == 0)
    def _():
        h_scr[...] = _modulated(x_ref, nw_ref, mods_ref, g, mod0, d).astype(BF16)
        acc_scr[...] = jnp.zeros_like(acc_scr)

    h = h_scr[...]
    act = _silu(_dot(h, wg_ref[...])) * _dot(h, wu_ref[...])
    acc_scr[...] += _dot(act.astype(BF16), wd_ref[...])

    @pl.when(f == pl.num_programs(1) - 1)
    def _():
        o_ref[...] = x_ref[...] + _mod_chunk(mods_ref, g, mod0 + 2, d) * acc_scr[...]


def _ffn(x, norm_w, mods, w_gu, w_down, rows, mod0, tf):
    n, d = x.shape
    tm = rows.tm
    ff = w_down.shape[0]
    nf = ff // tf
    return pl.pallas_call(
        functools.partial(_ffn_kernel, rows=rows, mod0=mod0), grid=(n // tm, nf),
        in_specs=[pl.BlockSpec((tm, d), lambda i, f: (i, 0)),
                  pl.BlockSpec((1, d), lambda i, f: (0, 0)),
                  pl.BlockSpec(mods.shape, lambda i, f: (0, 0)),
                  pl.BlockSpec((d, tf), lambda i, f: (0, f)),
                  pl.BlockSpec((d, tf), lambda i, f: (0, nf + f)),
                  pl.BlockSpec((tf, d), lambda i, f: (f, 0))],
        out_specs=pl.BlockSpec((tm, d), lambda i, f: (i, 0)),
        out_shape=jax.ShapeDtypeStruct((n, d), F32),
        scratch_shapes=[pltpu.VMEM((tm, d), BF16), pltpu.VMEM((tm, d), F32)],
        compiler_params=_cparams("parallel", "arbitrary"), name="ffn",
    )(x, norm_w.reshape(1, d).astype(F32), mods, w_gu, w_gu, w_down)


def _moe_kernel(x_ref, nw_ref, mods_ref, rw_ref, wgu_ref, wd_ref, o_ref, h_scr, g_scr, rk_scr, rkt_scr, acc_scr,
                *, rows, mod0, n_exp, sb):
    i, e = pl.program_id(0), pl.program_id(1)
    tm, d = x_ref.shape
    g = rows.group(i)
    lane = lax.broadcasted_iota(jnp.int32, (1, LANES), 1)

    @pl.when(e == 0)
    def _():
        h = _modulated(x_ref, nw_ref, mods_ref, g, mod0, d)
        h_scr[...] = h.astype(BF16)
        acc_scr[...] = jnp.zeros_like(acc_scr)
        logits = jnp.where(lane < n_exp, _dot_f32(h, rw_ref[...]), -1e30)
        pe = jnp.exp(logits - jnp.max(logits, axis=-1, keepdims=True))
        probs = pe / jnp.sum(pe, axis=-1, keepdims=True)
        p1 = jnp.max(probs, axis=-1, keepdims=True)
        i1 = jnp.min(jnp.where(probs == p1, lane, LANES), axis=-1, keepdims=True)
        rest = jnp.where(lane == i1, -1.0, probs)
        p2 = jnp.max(rest, axis=-1, keepdims=True)
        i2 = jnp.min(jnp.where(rest == p2, lane, LANES), axis=-1, keepdims=True)
        den = p1 + p2
        gates = jnp.where(lane == i1, p1 / den, 0.0) + jnp.where(lane == i2, p2 / den, 0.0)
        g_scr[...] = gates
        t_r = lax.broadcasted_iota(jnp.int32, (tm, tm), 0)
        t_c = lax.broadcasted_iota(jnp.int32, (tm, tm), 1)
        earlier = jnp.where(t_r > t_c, 1.0, 0.0).astype(BF16)
        sel = gates > 0.0
        rank = jnp.where(sel, _dot(earlier, jnp.where(sel, 1.0, 0.0).astype(BF16)), -1.0)
        rk_scr[...] = rank
        rkt_scr[...] = jnp.transpose(rank)[:rkt_scr.shape[0], :]

    ff = wd_ref.shape[0]
    g_e = jnp.sum(jnp.where(lane == e, g_scr[...], 0.0), axis=-1, keepdims=True)
    rk_col = jnp.sum(jnp.where(lane == e, rk_scr[...], 0.0), axis=-1, keepdims=True)
    rk_row = rkt_scr[pl.ds(e, 1), :]
    n_rows = (jnp.max(rk_row) + 1.0).astype(jnp.int32)
    g_hi = g_e.astype(BF16).astype(F32)
    g2 = jnp.where(lane == 0, g_hi, jnp.where(lane == 1, g_e - g_hi, 0.0)).astype(BF16)
    r_sub = lax.broadcasted_iota(jnp.int32, (sb, 1), 0).astype(F32)
    r_lane = lax.broadcasted_iota(jnp.int32, (1, sb), 1).astype(F32)

    def body(jb, carry):
        base = (jb * sb).astype(F32)
        pick = jnp.where(rk_row == r_sub + base, 1.0, 0.0).astype(BF16)
        xg = _dot(pick, h_scr[...]).astype(BF16)
        gr = _dot(pick, g2)
        gate_r = gr[:, 0:1] + gr[:, 1:2]
        gu = _dot(xg, wgu_ref[...])
        act = _silu(gu[:, :ff]) * gu[:, ff:]
        y = _dot(act.astype(BF16), wd_ref[...])
        put = jnp.where(rk_col == r_lane + base, 1.0, 0.0).astype(BF16)
        acc_scr[...] += _dot(put, (y * gate_r).astype(BF16))
        return carry

    lax.fori_loop(0, (n_rows + sb - 1) // sb, body, 0)

    @pl.when(e == n_exp - 1)
    def _():
        o_ref[...] = x_ref[...] + _mod_chunk(mods_ref, g, mod0 + 2, d) * acc_scr[...]


def _moe(x, norm_w, mods, router_w, w_gu, w_down, rows, mod0):
    n, d = x.shape
    tm = rows.tm
    n_exp, _, ff2 = w_gu.shape
    rw = jnp.zeros((d, LANES), F32).at[:, :n_exp].set(router_w.astype(F32))
    return pl.pallas_call(
        functools.partial(_moe_kernel, rows=rows, mod0=mod0, n_exp=n_exp, sb=MOE_ROW_BLOCK), grid=(n // tm, n_exp),
        in_specs=[pl.BlockSpec((tm, d), lambda i, e: (i, 0)),
                  pl.BlockSpec((1, d), lambda i, e: (0, 0)),
                  pl.BlockSpec(mods.shape, lambda i, e: (0, 0)),
                  pl.BlockSpec((d, LANES), lambda i, e: (0, 0)),
                  pl.BlockSpec((None, d, ff2), lambda i, e: (e, 0, 0)),
                  pl.BlockSpec((None, ff2 // 2, d), lambda i, e: (e, 0, 0))],
        out_specs=pl.BlockSpec((tm, d), lambda i, e: (i, 0)),
        out_shape=jax.ShapeDtypeStruct((n, d), F32),
        scratch_shapes=[pltpu.VMEM((tm, d), BF16), pltpu.VMEM((tm, LANES), F32), pltpu.VMEM((tm, LANES), F32),
                        pltpu.VMEM((SUBLANES * ((n_exp + SUBLANES - 1) // SUBLANES), tm), F32),
                        pltpu.VMEM((tm, d), F32)],
        compiler_params=_cparams("parallel", "arbitrary"), name="moe",
    )(x, norm_w.reshape(1, d).astype(F32), mods, rw, w_gu, w_down)


def _final_kernel(x_ref, w_ref, o_ref):
    o_ref[...] = _rmsnorm_rows(x_ref[...], w_ref[...])


def _final_norm(x, w, tm):
    n, d = x.shape
    return pl.pallas_call(
        _final_kernel, grid=(n // tm,),
        in_specs=[pl.BlockSpec((tm, d), lambda i: (i, 0)), pl.BlockSpec((1, d), lambda i: (0, 0))],
        out_specs=pl.BlockSpec((tm, d), lambda i: (i, 0)),
        out_shape=jax.ShapeDtypeStruct((n, d), F32),
        compiler_params=_cparams("parallel"), name="final_norm",
    )(x, w.reshape(1, d).astype(F32))


def kernel(x_prompt, x_sample, state_delta, cache_diff_k, cache_diff_v, cache_mla_ckv, cache_mla_krope, c, c_ctx,
           mod_w, mod_b, norm_mix, norm_ffn, final_norm, ev_w_in, ev_conv_w, ev_a_log, ev_dt_bias, ev_onorm,
           ev_lambda, ev_subln, ev_w_out, ffn_w_gu, ffn_w_down, od_w_in, od_q_norm, od_kv_norm, od_w_uq, od_w_ukv,
           od_w_out, moe_router, moe_w_gu, moe_w_down):
    bp, lp, d = x_prompt.shape
    bs, ls, _ = x_sample.shape
    past = cache_diff_k.shape[3]
    depth = mod_w.shape[0]
    n_p, n_s = bp * lp, bs * ls
    n = n_p + n_s
    assert n_p % ls == 0 and past == lp

    rows = _Rows(n_p, n_s, ls, _tile(math.gcd(n_p, ls), 512))
    rows_m = _Rows(n_p, n_s, ls, _tile(math.gcd(n_p, ls), 1024))
    rows_c = _Rows(n_p, n_s, ls, _tile(math.gcd(lp, ls), 256))
    rg = _tile(math.gcd(lp, ls), 256)
    cos_t, sin_t = _rope_tables(rows, ls)

    x = jnp.concatenate([x_prompt.reshape(n_p, d), x_sample.reshape(n_s, d)], axis=0)
    n_grp = 1 + bs
    cc = jnp.zeros((2 * SUBLANES * ((n_grp + 15) // 16), d), F32).at[0].set(c_ctx).at[1:n_grp].set(c)
    mods_all = _mods(cc, mod_w, mod_b)

    hq = H_A * DK_A
    hb2 = H_B * 2 * DQK_B
    sm0 = 4 * hq
    qb0 = sm0 + 4 * H_A
    q_scale = (DQK_B ** -0.5) * LOG2E
    mla_scale = ((NOPE_C + ROPE_C) ** -0.5) * LOG2E
    sd_new, dk_new, dv_new, ckv_new, kr_new = [], [], [], [], []

    for layer in range(depth):
        j = layer // 2
        mods = mods_all[layer]
        if layer % 2 == 0:
            lam_init = 0.8 - 0.6 * math.exp(-0.3 * layer)
            w = ev_w_in[j]
            sm = w[:, sm0:qb0].reshape(d, 4, H_A).transpose(0, 2, 1).reshape(d, 4 * H_A)
            w_in = jnp.concatenate([w[:, :sm0], w[:, qb0:], sm, jnp.zeros((d, LANES - 4 * H_A), w.dtype)],
                                   axis=1).astype(BF16)
            tn = 512
            n_main = sm0 + 3 * hb2
            epi = [(False, 1.0)] * (sm0 // tn) + [(True, q_scale)] * (hb2 // tn) + [(True, 1.0)] * (hb2 // tn) \
                + [(False, 1.0)] * (hb2 // tn)
            proj = _norm_matmul(x, w_in[:, :n_main], rows, tn=tn, norm_w=norm_mix[layer], mods=mods, mod_idx=(0, 1),
                                rope=(cos_t, sin_t), epi=epi)
            small = _norm_matmul(x, w_in[:, n_main:], rows, tn=LANES, norm_w=norm_mix[layer], mods=mods,
                                 mod_idx=(0, 1))
            qkvn, gcol, grow = _conv_stage(proj, small, ev_conv_w[j], ev_a_log[j], ev_dt_bias[j], rows_c, 3 * hq)
            of_p, ob_p, s_p = _delta_stage(qkvn, gcol, grow, jnp.zeros((bp, 2, H_A, DK_A, DV_A), F32), bp, lp, 0, rg)
            of_s, ob_s, _ = _delta_stage(qkvn, gcol, grow, state_delta[:, j].astype(F32), bs, ls, n_p, rg)
            o_f = jnp.concatenate([of_p, of_s], axis=0)
            o_b = jnp.concatenate([ob_p, ob_s], axis=0)
            sd_new.append(s_p)
            qkv_b = proj[:, sm0:].astype(BF16)
            lam_p = ev_lambda[j].astype(F32)
            nb = hb2 // LANES
            tq_p = _tile(lp, 256)
            od_p = _flash([(qkv_b, lambda h: h)], [(qkv_b, lambda h: nb + h)], (qkv_b, lambda h: 2 * nb + h), None,
                          lam_p, n_seq=bp, n_heads=H_B, lq=lp, lk=lp, q_row0=0, k_row0=0, tq=tq_p,
                          tk=_tile(lp, 512), diff=True, lam_init=lam_init, out_cols=hb2)
            ck = cache_diff_k[:, j].reshape(bs * H_B * past, 2 * DQK_B).astype(BF16)
            cv = cache_diff_v[:, j].reshape(bs * H_B * past, DV_B).astype(BF16)
            cache = ([(ck, lambda h: 0)], (cv, lambda h: 0), past, lambda b, h: b * H_B + h)
            od_s = _flash([(qkv_b, lambda h: h)], [(qkv_b, lambda h: nb + h)], (qkv_b, lambda h: 2 * nb + h), cache,
                          lam_p, n_seq=bs, n_heads=H_B, lq=ls, lk=ls, q_row0=n_p, k_row0=n_p, tq=_tile(ls, 512),
                          tk=_tile(ls, 512), diff=True, lam_init=lam_init, out_cols=hb2)
            o_d = jnp.concatenate([od_p, od_s], axis=0)
            dk_new.append(proj[:n_p, sm0 + hb2:sm0 + 2 * hb2].reshape(bp, lp, H_B, 2 * DQK_B).transpose(0, 2, 1, 3))
            dv_new.append(proj[:n_p, sm0 + 2 * hb2:].reshape(bp, lp, H_B, DV_B).transpose(0, 2, 1, 3))
            x = _evout(x, o_f, o_b, proj, 3 * hq // (H_A * DV_A), o_d, ev_onorm[j], ev_subln[j],
                       ev_w_out[j].astype(BF16), mods, rows, 2, lam_init)
            ff = ffn_w_down.shape[1]
            x = _ffn(x, norm_ffn[layer], mods, ffn_w_gu[j].astype(BF16), ffn_w_down[j].astype(BF16), rows, 3,
                     ff // 2 if (ff // 2) % LANES == 0 else ff)
        else:
            n_in = Q_LORA + KV_LORA + ROPE_C
            pad = (-n_in) % LANES
            w_in = jnp.concatenate([od_w_in[j], jnp.zeros((d, pad), od_w_in.dtype)], axis=1).astype(BF16)
            nblk = (n_in + pad) // LANES
            epi = [(False, 1.0)] * (nblk - 1) + [(True, 1.0)]
            proj = _norm_matmul(x, w_in, rows, tn=LANES, norm_w=norm_mix[layer], mods=mods, mod_idx=(0, 1),
                                rope=(cos_t, sin_t), epi=epi)
            wq = od_w_uq[j].reshape(Q_LORA, H_C, NOPE_C + ROPE_C)
            wq_rope = jnp.concatenate([wq[:, :, NOPE_C:], jnp.zeros((Q_LORA, H_C, LANES - ROPE_C), wq.dtype)], axis=2)
            wq2 = jnp.concatenate([wq[:, :, :NOPE_C].reshape(Q_LORA, H_C * NOPE_C),
                                   wq_rope.reshape(Q_LORA, H_C * LANES)], axis=1).astype(BF16)
            epi_q = [(False, mla_scale)] * (H_C * NOPE_C // 512) + [(True, mla_scale)] * (H_C * LANES // 512)
            q_all = _norm_matmul(proj[:, :Q_LORA], wq2, rows, tn=512, norm_w=od_q_norm[j], rope=(cos_t, sin_t),
                                 epi=epi_q, out_dtype=BF16)
            kv_tok, ckv_n = _norm_matmul(proj[:, Q_LORA:Q_LORA + KV_LORA], od_w_ukv[j].astype(BF16), rows, tn=512,
                                         norm_w=od_kv_norm[j], emit_h=True, out_dtype=BF16)
            kr_tok = proj[:, Q_LORA + KV_LORA:].astype(BF16)
            ckv_new.append(ckv_n[:n_p].reshape(bp, lp, KV_LORA))
            kr_new.append(proj[:n_p, Q_LORA + KV_LORA:n_in].reshape(bp, lp, ROPE_C))
            rows_k = _Rows(bs * past, 0, past, _tile(past, 256))
            kv_c = _norm_matmul(cache_mla_ckv[:, j].reshape(bs * past, KV_LORA), od_w_ukv[j].astype(BF16), rows_k,
                                tn=512, out_dtype=BF16)
            kr_c = jnp.concatenate([cache_mla_krope[:, j].reshape(bs * past, ROPE_C),
                                    jnp.zeros((bs * past, LANES - ROPE_C), F32)], axis=1).astype(BF16)
            qp = [(q_all, lambda h: h), (q_all, lambda h: H_C + h)]
            kp = [(kv_tok, lambda h: 2 * h), (kr_tok, lambda h: 0)]
            vp = (kv_tok, lambda h: 2 * h + 1)
            o_p = _flash(qp, kp, vp, None, None, n_seq=bp, n_heads=H_C, lq=lp, lk=lp, q_row0=0, k_row0=0,
                         tq=_tile(lp, 256), tk=_tile(lp, 512), diff=False, lam_init=0.0, out_cols=H_C * V_C)
            cache = ([(kv_c, lambda h: 2 * h), (kr_c, lambda h: 0)], (kv_c, lambda h: 2 * h + 1), past,
                     lambda b, h: b)
            o_s = _flash(qp, kp, vp, cache, None, n_seq=bs, n_heads=H_C, lq=ls, lk=ls, q_row0=n_p, k_row0=n_p,
                         tq=_tile(ls, 512), tk=_tile(ls, 512), diff=False, lam_init=0.0, out_cols=H_C * V_C)
            o = jnp.concatenate([o_p, o_s], axis=0)
            x = _matmul_residual(x, o, od_w_out[j].astype(BF16), mods, rows, 2)
            x = _moe(x, norm_ffn[layer], mods, moe_router[j], moe_w_gu[j].astype(BF16), moe_w_down[j].astype(BF16),
                     rows_m, 3)

    y = _final_norm(x, final_norm, rows.tm)
    return (y[:n_p].reshape(bp, lp, d), y[n_p:].reshape(bs, ls, d), jnp.stack(sd_new, axis=1),
            jnp.stack(dk_new, axis=1), jnp.stack(dv_new, axis=1), jnp.stack(ckv_new, axis=1),
            jnp.stack(kr_new, axis=1))
```

```python
import functools
import math

import numpy as np
import jax
import jax.numpy as jnp
from jax import lax
from jax.experimental import pallas as pl
from jax.experimental.pallas import tpu as pltpu

F32 = jnp.float32
BF16 = jnp.bfloat16

EPS = 1e-6
LOG2E = 1.4426950408889634
GRID_W = 64
ROPE_BASE = 10000.0
H_A, DK_A, DV_A = 4, 128, 128
CONV_K = 3
CHUNK = 64
H_B, DQK_B, DV_B = 4, 64, 128
H_C, NOPE_C, ROPE_C, V_C = 8, 128, 64, 128
Q_LORA, KV_LORA = 384, 256
LANES = 128
SUBLANES = 8
VMEM_LIMIT_BYTES = 56 * 1024 * 1024


def _cparams(*sem):
    return pltpu.CompilerParams(dimension_semantics=sem, vmem_limit_bytes=VMEM_LIMIT_BYTES)


def _dot(a, b):
    return jnp.dot(a, b, preferred_element_type=F32)


def _dot_nt(a, b):
    return lax.dot_general(a, b, (((1,), (1,)), ((), ())), preferred_element_type=F32)


def _dot_tn(a, b):
    return lax.dot_general(a, b, (((0,), (0,)), ((), ())), preferred_element_type=F32)


def _split3(x):
    hi = x.astype(BF16)
    r = x - hi.astype(F32)
    mid = r.astype(BF16)
    lo = (r - mid.astype(F32)).astype(BF16)
    return hi, mid, lo


def _dot_f32(a, b):
    ah, am, al = _split3(a)
    bh, bm, bl = _split3(b)
    return (_dot(ah, bh) + (_dot(ah, bm) + _dot(am, bh))
            + (_dot(am, bm) + _dot(ah, bl) + _dot(al, bh)))


def _silu(x):
    return x * jax.nn.sigmoid(x)


def _tile(n, pref, mult=SUBLANES):
    t = min(n, pref)
    while t > mult and (n % t or t % mult):
        t -= mult
    assert n % t == 0, (n, pref)
    return t


class _Rows:
    def __init__(self, n_ctx, n_lat, lat_len, tm):
        assert n_ctx % tm == 0 and lat_len % tm == 0
        self.tm = tm
        self.n = n_ctx + n_lat
        self.ctx_tiles = n_ctx // tm
        self.seq_tiles = lat_len // tm

    def group(self, i):
        return jnp.where(i < self.ctx_tiles, 0, 1 + jnp.maximum(i - self.ctx_tiles, 0) // self.seq_tiles)

    def pos_block(self, i):
        return jnp.where(i < self.ctx_tiles, 0, 1 + jnp.maximum(i - self.ctx_tiles, 0) % self.seq_tiles)


def _mod_kernel(c_ref, w_ref, b_ref, o_ref):
    o_ref[...] = _dot_f32(_silu(c_ref[...]), w_ref[...]) + b_ref[...]


def _mods(cc, mod_w, mod_b):
    depth, d, n6 = mod_w.shape
    tn = _tile(n6, 512, LANES)
    return pl.pallas_call(
        _mod_kernel,
        grid=(depth, n6 // tn),
        in_specs=[pl.BlockSpec(cc.shape, lambda l, j: (0, 0)),
                  pl.BlockSpec((None, d, tn), lambda l, j: (l, 0, j)),
                  pl.BlockSpec((None, 1, tn), lambda l, j: (l, 0, j))],
        out_specs=pl.BlockSpec((None, cc.shape[0], tn), lambda l, j: (l, 0, j)),
        out_shape=jax.ShapeDtypeStruct((depth, cc.shape[0], n6), F32),
        compiler_params=_cparams("parallel", "parallel"),
        name="mods",
    )(cc, mod_w, mod_b.reshape(depth, 1, n6))


def _mod_chunk(mods_ref, g, idx, d):
    return mods_ref[pl.ds(g, 1), idx * d:(idx + 1) * d]


def _rmsnorm_rows(x, w):
    ms = jnp.mean(x * x, axis=-1, keepdims=True)
    return x * lax.rsqrt(ms + EPS) * w


def _rope(x, cos, sin_signed):
    lane = lax.broadcasted_iota(jnp.int32, (1, LANES), 1)
    low = (lane % 32) < 16
    outs = []
    for c in range(x.shape[1] // LANES):
        xc = x[:, c * LANES:(c + 1) * LANES]
        fwd = pltpu.roll(xc, LANES - 16, axis=1)
        bwd = pltpu.roll(xc, 16, axis=1)
        outs.append(xc * cos + jnp.where(low, fwd, bwd) * sin_signed)
    return outs[0] if len(outs) == 1 else jnp.concatenate(outs, axis=1)


def _rope_tables(rows, lat_len):
    pos = jnp.arange(lat_len)
    r = (pos // GRID_W).astype(F32)
    c = (pos % GRID_W).astype(F32)
    inv = ROPE_BASE ** (-jnp.arange(0, 32, 2, dtype=F32) / 32)
    ang_r = r[:, None] * inv[None, :]
    ang_c = c[:, None] * inv[None, :]
    ang = jnp.concatenate([ang_r, ang_r, ang_c, ang_c] * 2, axis=-1)
    sign = jnp.where((jnp.arange(LANES) % 32) < 16, -1.0, 1.0).astype(F32)
    cos = jnp.concatenate([jnp.ones((rows.tm, LANES), F32), jnp.cos(ang)], axis=0)
    sin = jnp.concatenate([jnp.zeros((rows.tm, LANES), F32), jnp.sin(ang) * sign[None, :]], axis=0)
    return cos, sin


def _nm_kernel(*refs, rows, d_mod, has_norm, mod_idx, has_rope, epi, emit_h):
    it = iter(refs)
    x_ref = next(it)
    w_ref = next(it)
    nw_ref = next(it) if has_norm else None
    mods_ref = next(it) if mod_idx is not None else None
    cos_ref, sin_ref = (next(it), next(it)) if has_rope else (None, None)
    o_ref = next(it)
    h_out_ref = next(it) if emit_h else None
    h_scr = next(it)
    i, j = pl.program_id(0), pl.program_id(1)

    @pl.when(j == 0)
    def _():
        x = x_ref[...].astype(F32)
        if has_norm:
            x = _rmsnorm_rows(x, nw_ref[...])
        if mod_idx is not None:
            g = rows.group(i)
            x = x * (1.0 + _mod_chunk(mods_ref, g, mod_idx[1], d_mod)) + _mod_chunk(mods_ref, g, mod_idx[0], d_mod)
        h_scr[...] = x.astype(BF16)
        if emit_h:
            h_out_ref[...] = x

    acc = _dot(h_scr[...], w_ref[...])
    kinds = sorted(set(epi))
    for kind in kinds:
        cond = None
        for jj, e in enumerate(epi):
            if e == kind:
                cond = (j == jj) if cond is None else (cond | (j == jj))

        def _store(kind=kind):
            y = acc
            if kind[0]:
                y = _rope(y, cos_ref[...], sin_ref[...])
            if kind[1] != 1.0:
                y = y * kind[1]
            o_ref[...] = y.astype(o_ref.dtype)

        if len(kinds) == 1:
            _store()
        else:
            pl.when(cond)(_store)


def _norm_matmul(x, w, rows, *, tn, norm_w=None, mods=None, mod_idx=None, rope=None, epi=None,
                 emit_h=False, out_dtype=F32):
    n, k = x.shape
    nout = w.shape[1]
    nj = nout // tn
    tm = rows.tm
    epi = tuple(epi) if epi is not None else ((False, 1.0),) * nj
    assert len(epi) == nj and n % tm == 0
    has_rope = any(e[0] for e in epi)
    args = [x, w]
    in_specs = [pl.BlockSpec((tm, k), lambda i, j: (i, 0)), pl.BlockSpec((k, tn), lambda i, j: (0, j))]
    if norm_w is not None:
        args.append(norm_w.reshape(1, k).astype(F32))
        in_specs.append(pl.BlockSpec((1, k), lambda i, j: (0, 0)))
    if mod_idx is not None:
        args.append(mods)
        in_specs.append(pl.BlockSpec(mods.shape, lambda i, j: (0, 0)))
    if has_rope:
        args += list(rope)
        in_specs += [pl.BlockSpec((tm, LANES), lambda i, j: (rows.pos_block(i), 0))] * 2
    out_shape = [jax.ShapeDtypeStruct((n, nout), out_dtype)]
    out_specs = [pl.BlockSpec((tm, tn), lambda i, j: (i, j))]
    if emit_h:
        out_shape.append(jax.ShapeDtypeStruct((n, k), F32))
        out_specs.append(pl.BlockSpec((tm, k), lambda i, j: (i, 0)))
    kern = functools.partial(_nm_kernel, rows=rows, d_mod=k, has_norm=norm_w is not None, mod_idx=mod_idx,
                             has_rope=has_rope, epi=epi, emit_h=emit_h)
    out = pl.pallas_call(
        kern, grid=(n // tm, nj), in_specs=in_specs, out_specs=out_specs, out_shape=out_shape,
        scratch_shapes=[pltpu.VMEM((tm, k), BF16)],
        compiler_params=_cparams("parallel", "arbitrary"), name="norm_matmul",
    )(*args)
    return out if emit_h else out[0]


PROJ_CHUNK = 512


def _even_proj_kernel(x_ref, nw_ref, mods_ref, w_ref, cos_ref, sin_ref, main_ref, att_ref, kv_ref,
                      *, rows, n_main, hb2, q_scale):
    i = pl.program_id(0)
    d = x_ref.shape[1]
    h = _modulated(x_ref, nw_ref, mods_ref, rows.group(i), 0, d).astype(BF16)
    for c0 in range(0, n_main, PROJ_CHUNK):
        c1 = min(c0 + PROJ_CHUNK, n_main)
        main_ref[:, c0:c1] = _dot(h, w_ref[:, c0:c1])
    cos, sin = cos_ref[...], sin_ref[...]
    for part in range(3):
        for c0 in range(0, hb2, PROJ_CHUNK):
            c1 = min(c0 + PROJ_CHUNK, hb2)
            y = _dot(h, w_ref[:, n_main + part * hb2 + c0:n_main + part * hb2 + c1])
            if part < 2:
                y = _rope(y, cos, sin)
            if part == 0:
                y = y * q_scale
            else:
                kv_ref[:, (part - 1) * hb2 + c0:(part - 1) * hb2 + c1] = y
            att_ref[:, part * hb2 + c0:part * hb2 + c1] = y.astype(BF16)


def _even_proj(x, norm_w, mods, w, rope, rows, n_main, hb2, q_scale):
    n, d = x.shape
    tm = rows.tm
    kern = functools.partial(_even_proj_kernel, rows=rows, n_main=n_main, hb2=hb2, q_scale=q_scale)
    tab = pl.BlockSpec((tm, LANES), lambda i: (rows.pos_block(i), 0))
    return pl.pallas_call(
        kern, grid=(n // tm,),
        in_specs=[pl.BlockSpec((tm, d), lambda i: (i, 0)), pl.BlockSpec((1, d), lambda i: (0, 0)),
                  pl.BlockSpec(mods.shape, lambda i: (0, 0)), pl.BlockSpec(w.shape, lambda i: (0, 0)), tab, tab],
        out_specs=[pl.BlockSpec((tm, n_main), lambda i: (i, 0)), pl.BlockSpec((tm, 3 * hb2), lambda i: (i, 0)),
                   pl.BlockSpec((tm, 2 * hb2), lambda i: (i, 0))],
        out_shape=[jax.ShapeDtypeStruct((n, n_main), F32), jax.ShapeDtypeStruct((n, 3 * hb2), BF16),
                   jax.ShapeDtypeStruct((n, 2 * hb2), F32)],
        compiler_params=_cparams("parallel"), name="even_proj",
    )(x, norm_w.reshape(1, d).astype(F32), mods, w, *rope)


def _odd_proj_kernel(x_ref, nw_ref, mods_ref, w_ref, qn_ref, kvn_ref, wq_ref, wkv_ref, cos_ref, sin_ref,
                     q_ref, kv_ref, kr_ref, ckv_ref, krf_ref, *, rows, scale, n_nope):
    i = pl.program_id(0)
    d = x_ref.shape[1]
    h = _modulated(x_ref, nw_ref, mods_ref, rows.group(i), 0, d).astype(BF16)
    p = _dot(h, w_ref[...])
    cos, sin = cos_ref[...], sin_ref[...]
    cq = _rmsnorm_rows(p[:, :Q_LORA], qn_ref[...]).astype(BF16)
    for c0 in range(0, wq_ref.shape[1], PROJ_CHUNK):
        y = _dot(cq, wq_ref[:, c0:c0 + PROJ_CHUNK])
        if c0 >= n_nope:
            y = _rope(y, cos, sin)
        q_ref[:, c0:c0 + PROJ_CHUNK] = (y * scale).astype(BF16)
    ckv = _rmsnorm_rows(p[:, Q_LORA:Q_LORA + KV_LORA], kvn_ref[...])
    ckv_ref[...] = ckv
    ckv16 = ckv.astype(BF16)
    for c0 in range(0, wkv_ref.shape[1], PROJ_CHUNK):
        kv_ref[:, c0:c0 + PROJ_CHUNK] = _dot(ckv16, wkv_ref[:, c0:c0 + PROJ_CHUNK]).astype(BF16)
    kr = _rope(p[:, Q_LORA + KV_LORA:], cos, sin)
    krf_ref[...] = kr
    kr_ref[...] = kr.astype(BF16)


def _odd_proj(x, norm_w, mods, w_in, q_norm, kv_norm, wq, wkv, rope, rows, scale, n_nope):
    n, d = x.shape
    tm = rows.tm
    tab = pl.BlockSpec((tm, LANES), lambda i: (rows.pos_block(i), 0))
    full = lambda a: pl.BlockSpec(a.shape, lambda i: (0, 0))
    qn = q_norm.reshape(1, -1).astype(F32)
    kvn = kv_norm.reshape(1, -1).astype(F32)
    nw = norm_w.reshape(1, d).astype(F32)
    outs = [(wq.shape[1], BF16), (wkv.shape[1], BF16), (LANES, BF16), (KV_LORA, F32), (LANES, F32)]
    return pl.pallas_call(
        functools.partial(_odd_proj_kernel, rows=rows, scale=scale, n_nope=n_nope), grid=(n // tm,),
        in_specs=[pl.BlockSpec((tm, d), lambda i: (i, 0)), full(nw), full(mods), full(w_in), full(qn), full(kvn),
                  full(wq), full(wkv), tab, tab],
        out_specs=[pl.BlockSpec((tm, c), lambda i: (i, 0)) for c, _ in outs],
        out_shape=[jax.ShapeDtypeStruct((n, c), dt) for c, dt in outs],
        compiler_params=_cparams("parallel"), name="odd_proj",
    )(x, nw, mods, w_in, qn, kvn, wq, wkv, *rope)


def _conv_kernel(x_ref, prev_ref, next_ref, sm_ref, cw_ref, alog_ref, dt_ref, o_ref, gc_ref, gr_ref,
                 *, tiles_ctx, tiles_seq, halo, n_qk):
    i = pl.program_id(0)
    t = jnp.maximum(i - tiles_ctx, 0) % tiles_seq
    lat = i >= tiles_ctx
    has_prev = lat & (t > 0)
    has_next = lat & (t < tiles_seq - 1)
    x = x_ref[...]
    tm = x.shape[0]
    row = lax.broadcasted_iota(jnp.int32, (tm, 1), 0)
    p_row = jnp.where(has_prev, prev_ref[halo - 1:halo, :], 0.0)
    n_row = jnp.where(has_next, next_ref[0:1, :], 0.0)
    x_prev = jnp.where(row == 0, p_row, pltpu.roll(x, 1, axis=0))
    x_next = jnp.where(row == tm - 1, n_row, pltpu.roll(x, tm - 1, axis=0))
    cw = cw_ref[...]
    y = _silu(x_prev * cw[0:1, :] + x * cw[1:2, :] + x_next * cw[2:3, :])
    outs = []
    for hh in range(y.shape[1] // LANES):
        yh = y[:, hh * LANES:(hh + 1) * LANES]
        if hh < n_qk:
            yh = yh * lax.rsqrt(jnp.sum(yh * yh, axis=-1, keepdims=True) + EPS)
            if hh < n_qk // 2:
                yh = yh * (DK_A ** -0.5)
        outs.append(yh)
    o_ref[...] = jnp.concatenate(outs, axis=1)
    s = sm_ref[...]
    lane = lax.broadcasted_iota(jnp.int32, (1, LANES), 1)
    is_g = (lane % 4) < 2
    z = s + dt_ref[...]
    softplus = jnp.maximum(z, 0.0) + jnp.log(1.0 + jnp.exp(-jnp.abs(z)))
    g = jnp.where(is_g, -jnp.exp(alog_ref[...]) * softplus, jax.nn.sigmoid(s))
    gc_ref[...] = g
    gr_ref[...] = jnp.transpose(g)[:gr_ref.shape[0], :]


def _conv_stage(proj, conv_w, a_log, dt_bias, rows_c, n_conv, sm_col):
    n = proj.shape[0]
    tm = rows_c.tm
    halo = 16
    hb = tm // halo
    nblk16 = n // halo
    alog_row = jnp.zeros((1, LANES), F32).at[0, :4 * H_A].set(
        jnp.stack([a_log[0], a_log[1], a_log[0], a_log[1]], axis=-1).reshape(-1).astype(F32))
    dt_row = jnp.zeros((1, LANES), F32).at[0, :4 * H_A].set(
        jnp.stack([dt_bias[0], dt_bias[1], jnp.zeros_like(dt_bias[0]), jnp.zeros_like(dt_bias[0])],
                  axis=-1).reshape(-1).astype(F32))
    kern = functools.partial(_conv_kernel, tiles_ctx=rows_c.ctx_tiles, tiles_seq=rows_c.seq_tiles, halo=halo,
                             n_qk=2 * H_A)
    return pl.pallas_call(
        kern, grid=(n // tm,),
        in_specs=[pl.BlockSpec((tm, n_conv), lambda i: (i, 0)),
                  pl.BlockSpec((halo, n_conv), lambda i: (jnp.maximum(i * hb - 1, 0), 0)),
                  pl.BlockSpec((halo, n_conv), lambda i: (jnp.minimum((i + 1) * hb, nblk16 - 1), 0)),
                  pl.BlockSpec((tm, LANES), lambda i: (i, sm_col)),
                  pl.BlockSpec((CONV_K, n_conv), lambda i: (0, 0)),
                  pl.BlockSpec((1, LANES), lambda i: (0, 0)),
                  pl.BlockSpec((1, LANES), lambda i: (0, 0))],
        out_specs=[pl.BlockSpec((tm, n_conv), lambda i: (i, 0)),
                   pl.BlockSpec((tm, LANES), lambda i: (i, 0)),
                   pl.BlockSpec((4 * H_A, tm), lambda i: (0, i))],
        out_shape=[jax.ShapeDtypeStruct((n, n_conv), F32),
                   jax.ShapeDtypeStruct((n, LANES), F32),
                   jax.ShapeDtypeStruct((4 * H_A, n), F32)],
        compiler_params=_cparams("parallel"), name="conv_gate",
    )(proj, proj, proj, proj, conv_w.astype(F32), alog_row, dt_row)


def _split2(x):
    hi = x.astype(BF16)
    return hi, (x - hi.astype(F32)).astype(BF16)


SOLVE_BLOCK = 2 * SUBLANES
MOE_ROW_BLOCK = 128
DELTA_HEADS_PER_STEP = 4


def _tri_solve_many(ms, rs, revs):
    c, width = rs[0].shape
    blk = SOLVE_BLOCK
    nb, gpb, ng = c // blk, blk // SUBLANES, c // SUBLANES
    col = lax.broadcasted_iota(jnp.int32, (blk, c), 1)
    mgs = [[m[SUBLANES * g:SUBLANES * (g + 1), :] for g in range(ng)] for m in ms]
    xgs = [[r[SUBLANES * g:SUBLANES * (g + 1), :] for g in range(ng)] for r in rs]
    zero_blk = jnp.zeros((blk, width), BF16)
    fin_h = [[zero_blk] * nb for _ in ms]
    fin_l = [[zero_blk] * nb for _ in ms]
    for bi in range(nb):
        blocks = [nb - 1 - bi if rev else bi for rev in revs]
        if bi > 0:
            for s, (rev, b) in enumerate(zip(revs, blocks)):
                done = (col >= (b + 1) * blk) if rev else (col < b * blk)
                lh, ll = _split2(jnp.where(done, ms[s][b * blk:(b + 1) * blk, :], 0.0))
                xh = jnp.concatenate(fin_h[s], axis=0)
                xl = jnp.concatenate(fin_l[s], axis=0)
                upd = _dot(lh, xh) + (_dot(lh, xl) + _dot(ll, xh))
                for gg in range(gpb):
                    g = b * gpb + gg
                    xgs[s][g] = xgs[s][g] - upd[SUBLANES * gg:SUBLANES * (gg + 1), :]
        for t in range(blk - 1):
            for s, (rev, b) in enumerate(zip(revs, blocks)):
                j = b * blk + (blk - 1 - t if rev else t)
                xj = xgs[s][j // SUBLANES][j % SUBLANES:j % SUBLANES + 1, :]
                groups = range(b * gpb, (j - 1) // SUBLANES + 1) if rev else range((j + 1) // SUBLANES, (b + 1) * gpb)
                for g in groups:
                    xgs[s][g] = xgs[s][g] - mgs[s][g][:, j:j + 1] * xj
        if bi < nb - 1:
            for s, b in enumerate(blocks):
                fin_h[s][b], fin_l[s][b] = _split2(jnp.concatenate(xgs[s][b * gpb:(b + 1) * gpb], axis=0))
    return [jnp.concatenate(xg, axis=0) for xg in xgs]


def _delta_kernel(qf, kf, vf, gcf, grf, qb, kb, vb, gcb, grb, s0_ref, *rest, chunk, hpb, n_alias):
    of_ref, ob_ref, so_ref, s_scr = rest[n_alias:]
    h0 = pl.program_id(1) * hpb
    i = pl.program_id(2)
    n_i = pl.num_programs(2)

    @pl.when(i == 0)
    def _():
        s_scr[...] = s0_ref[...]

    rg = qf.shape[0]
    nc = rg // chunk
    rowi = lax.broadcasted_iota(jnp.int32, (rg, rg), 0)
    coli = lax.broadcasted_iota(jnp.int32, (rg, rg), 1)
    same = (rowi // chunk) == (coli // chunk)
    lane = lax.broadcasted_iota(jnp.int32, (1, LANES), 1)
    r_c = lax.broadcasted_iota(jnp.int32, (chunk, chunk), 0)
    c_c = lax.broadcasted_iota(jnp.int32, (chunk, chunk), 1)

    prep = []
    for d, (q_ref, k_ref, v_ref, gc_ref, gr_ref) in enumerate(((qf, kf, vf, gcf, grf), (qb, kb, vb, gcb, grb))):
        rev = d == 1
        incl_big = same & ((rowi <= coli) if rev else (rowi >= coli))
        lm = jnp.where(incl_big, 1.0, 0.0).astype(BF16)
        gcols = gc_ref[...]
        grows = gr_ref[...]
        gh, gm, gl_ = _split3(gcols)
        cum_c = _dot(lm, gh) + _dot(lm, gm) + _dot(lm, gl_)
        th, tm_, tl = _split3(grows)
        cum_r = _dot_nt(th, lm) + _dot_nt(tm_, lm) + _dot_nt(tl, lm)

        def col(a, ln):
            return jnp.sum(jnp.where(lane == ln, a, 0.0), axis=1, keepdims=True)

        sub = lax.broadcasted_iota(jnp.int32, (cum_r.shape[0], 1), 0)
        incl = (r_c <= c_c) if rev else (r_c >= c_c)
        strict = (r_c < c_c) if rev else (r_c > c_c)
        for hh in range(hpb):
            h = h0 + hh
            hs = slice(hh * LANES, (hh + 1) * LANES)
            gcum = col(cum_c, 4 * h + d)
            beta = col(gcols, 4 * h + 2 + d)
            grow = jnp.sum(jnp.where(sub == 4 * h + d, cum_r, 0.0), axis=0, keepdims=True)
            q = q_ref[:, hs]
            k = k_ref[:, hs]
            v = v_ref[:, hs]
            kbeta = k * beta
            eg = jnp.exp(gcum)
            rhs = jnp.concatenate([v * beta, kbeta * eg], axis=1)
            kb16 = kbeta.astype(BF16)
            k16 = k.astype(BF16)
            q16 = q.astype(BF16)
            ms, rs, a_in = [], [], []
            for c in range(nc):
                sl = slice(c * chunk, (c + 1) * chunk)
                e = jnp.exp(jnp.where(incl, gcum[sl] - grow[:, sl], 0.0))
                kk = _dot_nt(kb16[sl], k16[sl])
                qk = _dot_nt(q16[sl], k16[sl])
                a_in.append(jnp.where(incl, qk * e, 0.0).astype(BF16))
                ms.append(jnp.where(strict, kk * e, 0.0))
                rs.append(rhs[sl])
            prep.append((rev, hh, ms, rs, a_in, q * eg, k, gcum))

    sols = _tri_solve_many([m for p in prep for m in p[2]], [r for p in prep for r in p[3]],
                           [p[0] for p in prep for _ in range(nc)])

    all_terms = []
    for n_u, (rev, hh, _, _, a_in, qg, k, gcum) in enumerate(prep):
        xs = sols[n_u * nc:(n_u + 1) * nc]
        terms = []
        for c in range(nc):
            sl = slice(c * chunk, (c + 1) * chunk)
            u16 = xs[c][:, :DV_A].astype(BF16)
            w16 = xs[c][:, DV_A:].astype(BF16)
            r_last = c * chunk if rev else (c + 1) * chunk - 1
            g_last = gcum[r_last:r_last + 1, :]
            k_dec = (k[sl] * jnp.exp(g_last - gcum[sl])).astype(BF16)
            terms.append(((qg[sl] - _dot(a_in[c], w16)).astype(BF16), _dot(a_in[c], u16),
                          _dot_tn(k_dec, w16).astype(BF16), _dot_tn(k_dec, u16), jnp.exp(g_last)))
        all_terms.append(terms)

    states = [s_scr[1 if p[0] else 0, p[1]] for p in prep]
    outs = [[None] * nc for _ in prep]
    for t in range(nc):
        for n_u, p in enumerate(prep):
            c = nc - 1 - t if p[0] else t
            o_s, o_0, w_s, q_0, decay = all_terms[n_u][c]
            s16 = states[n_u].astype(BF16)
            outs[n_u][c] = o_0 + _dot(o_s, s16)
            states[n_u] = states[n_u] * decay - _dot(w_s, s16) + q_0
    for n_u, (rev, hh) in enumerate((p[0], p[1]) for p in prep):
        s_scr[1 if rev else 0, hh] = states[n_u]
        (ob_ref if rev else of_ref)[:, hh * LANES:(hh + 1) * LANES] = jnp.concatenate(outs[n_u], axis=0)

    @pl.when(i == n_i - 1)
    def _():
        so_ref[...] = s_scr[...]


def _delta_stage(qkvn, gcol, grow, s0, n_seq, seq_len, row0, rg, into=None):
    ng = seq_len // rg
    b0 = row0 // rg

    def fwd_blk(b, i):
        return b0 + b * ng + i

    def bwd_blk(b, i):
        return b0 + b * ng + (ng - 1 - i)

    hpb = DELTA_HEADS_PER_STEP
    nhp = H_A // hpb
    hw = hpb * LANES

    def specs(blk):
        return [pl.BlockSpec((rg, hw), lambda b, h, i: (blk(b, i), h)),
                pl.BlockSpec((rg, hw), lambda b, h, i: (blk(b, i), nhp + h)),
                pl.BlockSpec((rg, hw), lambda b, h, i: (blk(b, i), 2 * nhp + h)),
                pl.BlockSpec((rg, LANES), lambda b, h, i: (blk(b, i), 0)),
                pl.BlockSpec((4 * H_A, rg), lambda b, h, i: (0, blk(b, i)))]

    st_spec = pl.BlockSpec((None, 2, hpb, DK_A, DV_A), lambda b, h, i: (b, 0, h, 0, 0))
    args = [qkvn, qkvn, qkvn, gcol, grow, qkvn, qkvn, qkvn, gcol, grow, s0]
    in_specs = specs(fwd_blk) + specs(bwd_blk) + [st_spec]
    aliases = {}
    if into is not None:
        for k_out, arr in enumerate(into):
            aliases[len(args)] = k_out
            args.append(arr)
            in_specs.append(pl.BlockSpec(memory_space=pl.ANY))
    n = qkvn.shape[0]
    return pl.pallas_call(
        functools.partial(_delta_kernel, chunk=CHUNK, hpb=hpb, n_alias=len(aliases)),
        grid=(n_seq, nhp, ng),
        in_specs=in_specs,
        out_specs=[pl.BlockSpec((rg, hw), lambda b, h, i: (fwd_blk(b, i), h)),
                   pl.BlockSpec((rg, hw), lambda b, h, i: (bwd_blk(b, i), h)),
                   st_spec],
        out_shape=[jax.ShapeDtypeStruct((n, H_A * DV_A), F32),
                   jax.ShapeDtypeStruct((n, H_A * DV_A), F32),
                   jax.ShapeDtypeStruct(s0.shape, F32)],
        input_output_aliases=aliases,
        scratch_shapes=[pltpu.VMEM((2, hpb, DK_A, DV_A), F32)],
        compiler_params=_cparams("parallel", "parallel", "arbitrary"), name="gated_delta",
    )(*args)


def _flash_kernel(*refs, n_parts, has_cache, diff, tk, lam_init, aliased):
    it = iter(refs)
    q_refs = [next(it) for _ in range(n_parts)]
    k_refs = [next(it) for _ in range(n_parts)]
    v_ref = next(it)
    kc_refs = [next(it) for _ in range(n_parts)] if has_cache else []
    vc_ref = next(it) if has_cache else None
    lam_ref = next(it) if diff else None
    if aliased:
        next(it)
    o_ref = next(it)

    qs = [r[...] for r in q_refs]
    tq = qs[0].shape[0]
    if diff:
        lane = lax.broadcasted_iota(jnp.int32, (1, LANES), 1)
        q = qs[0]
        qs = [jnp.concatenate([jnp.where(lane < DQK_B, q, 0), jnp.where(lane >= DQK_B, q, 0)], axis=0)]
    nrow = qs[0].shape[0]
    q_cat = qs[0] if n_parts == 1 else jnp.concatenate(qs, axis=1)

    def block(carry, ks, vv):
        m, l, acc = carry
        s = _dot_nt(q_cat, ks[0] if n_parts == 1 else jnp.concatenate(ks, axis=1))
        m_new = jnp.maximum(m, jnp.max(s, axis=-1, keepdims=True))
        alpha = jnp.exp2(m - m_new)
        p_ = jnp.exp2(s - m_new)
        l = alpha * l + jnp.sum(p_, axis=-1, keepdims=True)
        acc = alpha * acc + _dot(p_.astype(BF16), vv)
        return m_new, l, acc

    carry = (jnp.full((nrow, 1), -1e30, F32), jnp.zeros((nrow, 1), F32), jnp.zeros((nrow, v_ref.shape[1]), F32))
    lk = k_refs[0].shape[0]

    def body(t, carry):
        off = pl.multiple_of(t * tk, tk)
        return block(carry, [r[pl.ds(off, tk), :] for r in k_refs], v_ref[pl.ds(off, tk), :])

    carry = lax.fori_loop(0, lk // tk, body, carry, unroll=True)
    if has_cache:
        carry = block(carry, [r[...] for r in kc_refs], vc_ref[...])
    _, l, acc = carry
    o = acc / l
    if diff:
        lp = lam_ref[...]
        lam = (jnp.exp(jnp.sum(lp[0:1] * lp[1:2], axis=-1, keepdims=True))
               - jnp.exp(jnp.sum(lp[2:3] * lp[3:4], axis=-1, keepdims=True)) + lam_init)
        o = o[:tq] - lam * o[tq:]
    o_ref[...] = o.astype(o_ref.dtype)


def _flash(q_parts, k_parts, v_part, cache, lam, *, n_seq, n_heads, lq, lk, q_row0, k_row0, tq, tk, diff,
           lam_init, out_cols, into=None):
    nqt = lq // tq
    qb0, kb0 = q_row0 // tq, k_row0 // lk
    args, in_specs = [], []
    for arr, cf in q_parts:
        args.append(arr)
        in_specs.append(pl.BlockSpec((tq, LANES), lambda b, h, t, cf=cf: (qb0 + b * nqt + t, cf(h))))
    for arr, cf in k_parts + [v_part]:
        args.append(arr)
        in_specs.append(pl.BlockSpec((lk, LANES), lambda b, h, t, cf=cf: (kb0 + b, cf(h))))
    if cache is not None:
        c_k, c_v, c_rows, c_blk = cache
        for arr, cf in c_k + [c_v]:
            args.append(arr)
            in_specs.append(pl.BlockSpec((c_rows, LANES), lambda b, h, t, cf=cf: (c_blk(b, h), cf(h))))
    if diff:
        args.append(lam)
        in_specs.append(pl.BlockSpec(lam.shape, lambda b, h, t: (0, 0)))
    aliases = {}
    if into is not None:
        aliases[len(args)] = 0
        args.append(into)
        in_specs.append(pl.BlockSpec(memory_space=pl.ANY))
    kern = functools.partial(_flash_kernel, n_parts=len(q_parts), has_cache=cache is not None, diff=diff, tk=tk,
                             lam_init=lam_init, aliased=into is not None)
    return pl.pallas_call(
        kern, grid=(n_seq, n_heads, nqt), in_specs=in_specs,
        out_specs=pl.BlockSpec((tq, LANES), lambda b, h, t: (qb0 + b * nqt + t, h)),
        out_shape=jax.ShapeDtypeStruct((q_parts[0][0].shape[0], out_cols), F32),
        input_output_aliases=aliases,
        compiler_params=_cparams("parallel", "parallel", "arbitrary"), name="flash_attention",
    )(*args)


def _evout_kernel(x_ref, of_ref, ob_ref, z_ref, od_ref, on_ref, sn_ref, w_ref, mods_ref, o_ref,
                  *, rows, gate_idx, lam_init):
    i = pl.program_id(0)
    d = x_ref.shape[1]
    oa = of_ref[...] + ob_ref[...]
    z = z_ref[...]
    od = od_ref[...]
    on = on_ref[...]
    sn = sn_ref[...]
    ya, yb = [], []
    for hh in range(H_A):
        sl = slice(hh * LANES, (hh + 1) * LANES)
        ya.append(_rmsnorm_rows(oa[:, sl], on) * _silu(z[:, sl]))
    for hh in range(H_B):
        sl = slice(hh * LANES, (hh + 1) * LANES)
        yb.append(_rmsnorm_rows(od[:, sl], sn) * (1.0 - lam_init))
    y = jnp.concatenate(ya + yb, axis=1).astype(BF16)
    o_ref[...] = x_ref[...] + _mod_chunk(mods_ref, rows.group(i), gate_idx, d) * _dot(y, w_ref[...])


def _evout(x, o_f, o_b, proj, z_col, o_d, onorm, subln, w_out, mods, rows, gate_idx, lam_init):
    n, d = x.shape
    tm = rows.tm
    hw = H_A * DV_A
    kern = functools.partial(_evout_kernel, rows=rows, gate_idx=gate_idx, lam_init=lam_init)
    return pl.pallas_call(
        kern, grid=(n // tm,),
        in_specs=[pl.BlockSpec((tm, d), lambda i: (i, 0)),
                  pl.BlockSpec((tm, hw), lambda i: (i, 0)),
                  pl.BlockSpec((tm, hw), lambda i: (i, 0)),
                  pl.BlockSpec((tm, hw), lambda i: (i, z_col)),
                  pl.BlockSpec((tm, hw), lambda i: (i, 0)),
                  pl.BlockSpec((1, LANES), lambda i: (0, 0)),
                  pl.BlockSpec((1, LANES), lambda i: (0, 0)),
                  pl.BlockSpec(w_out.shape, lambda i: (0, 0)),
                  pl.BlockSpec(mods.shape, lambda i: (0, 0))],
        out_specs=pl.BlockSpec((tm, d), lambda i: (i, 0)),
        out_shape=jax.ShapeDtypeStruct((n, d), F32),
        compiler_params=_cparams("parallel"), name="even_out",
    )(x, o_f, o_b, proj, o_d, onorm.reshape(1, LANES).astype(F32), subln.reshape(1, LANES).astype(F32), w_out, mods)


def _mr_kernel(x_ref, a_ref, w_ref, mods_ref, o_ref, *, rows, gate_idx):
    i = pl.program_id(0)
    d = x_ref.shape[1]
    gate = _mod_chunk(mods_ref, rows.group(i), gate_idx, d)
    o_ref[...] = x_ref[...] + gate * _dot(a_ref[...].astype(BF16), w_ref[...])


def _matmul_residual(x, a, w, mods, rows, gate_idx):
    n, d = x.shape
    tm = rows.tm
    return pl.pallas_call(
        functools.partial(_mr_kernel, rows=rows, gate_idx=gate_idx), grid=(n // tm,),
        in_specs=[pl.BlockSpec((tm, d), lambda i: (i, 0)),
                  pl.BlockSpec((tm, a.shape[1]), lambda i: (i, 0)),
                  pl.BlockSpec(w.shape, lambda i: (0, 0)),
                  pl.BlockSpec(mods.shape, lambda i: (0, 0))],
        out_specs=pl.BlockSpec((tm, d), lambda i: (i, 0)),
        out_shape=jax.ShapeDtypeStruct((n, d), F32),
        compiler_params=_cparams("parallel"), name="matmul_residual",
    )(x, a, w, mods)


def _modulated(x_ref, nw_ref, mods_ref, g, idx, d):
    x = _rmsnorm_rows(x_ref[...], nw_ref[...])
    return x * (1.0 + _mod_chunk(mods_ref, g, idx + 1, d)) + _mod_chunk(mods_ref, g, idx, d)


def _ffn_kernel(x_ref, nw_ref, mods_ref, wg_ref, wu_ref, wd_ref, o_ref, h_scr, acc_scr, *, rows, mod0):
    i, f = pl.program_id(0), pl.program_id(1)
    d = x_ref.shape[1]
    g = rows.group(i)

    @pl.when(f == 0)
    def _():
        h_scr[...] = _modulated(x_ref, nw_ref, mods_ref, g, mod0, d).astype(BF16)
        acc_scr[...] = jnp.zeros_like(acc_scr)

    h = h_scr[...]
    act = _silu(_dot(h, wg_ref[...])) * _dot(h, wu_ref[...])
    acc_scr[...] += _dot(act.astype(BF16), wd_ref[...])

    @pl.when(f == pl.num_programs(1) - 1)
    def _():
        o_ref[...] = x_ref[...] + _mod_chunk(mods_ref, g, mod0 + 2, d) * acc_scr[...]


def _ffn(x, norm_w, mods, w_gu, w_down, rows, mod0, tf):
    n, d = x.shape
    tm = rows.tm
    ff = w_down.shape[0]
    nf = ff // tf
    return pl.pallas_call(
        functools.partial(_ffn_kernel, rows=rows, mod0=mod0), grid=(n // tm, nf),
        in_specs=[pl.BlockSpec((tm, d), lambda i, f: (i, 0)),
                  pl.BlockSpec((1, d), lambda i, f: (0, 0)),
                  pl.BlockSpec(mods.shape, lambda i, f: (0, 0)),
                  pl.BlockSpec((d, tf), lambda i, f: (0, f)),
                  pl.BlockSpec((d, tf), lambda i, f: (0, nf + f)),
                  pl.BlockSpec((tf, d), lambda i, f: (f, 0))],
        out_specs=pl.BlockSpec((tm, d), lambda i, f: (i, 0)),
        out_shape=jax.ShapeDtypeStruct((n, d), F32),
        scratch_shapes=[pltpu.VMEM((tm, d), BF16), pltpu.VMEM((tm, d), F32)],
        compiler_params=_cparams("parallel", "arbitrary"), name="ffn",
    )(x, norm_w.reshape(1, d).astype(F32), mods, w_gu, w_gu, w_down)


def _moe_kernel(x_ref, nw_ref, mods_ref, rw_ref, wgu_ref, wd_ref, o_ref, h_scr, g_scr, rk_scr, rkt_scr, acc_scr,
                *, rows, mod0, n_exp, sb):
    i, e = pl.program_id(0), pl.program_id(1)
    tm, d = x_ref.shape
    g = rows.group(i)
    lane = lax.broadcasted_iota(jnp.int32, (1, LANES), 1)

    @pl.when(e == 0)
    def _():
        h = _modulated(x_ref, nw_ref, mods_ref, g, mod0, d)
        h_scr[...] = h.astype(BF16)
        acc_scr[...] = jnp.zeros_like(acc_scr)
        logits = jnp.where(lane < n_exp, _dot_f32(h, rw_ref[...]), -1e30)
        pe = jnp.exp(logits - jnp.max(logits, axis=-1, keepdims=True))
        probs = pe / jnp.sum(pe, axis=-1, keepdims=True)
        p1 = jnp.max(probs, axis=-1, keepdims=True)
        i1 = jnp.min(jnp.where(probs == p1, lane, LANES), axis=-1, keepdims=True)
        rest = jnp.where(lane == i1, -1.0, probs)
        p2 = jnp.max(rest, axis=-1, keepdims=True)
        i2 = jnp.min(jnp.where(rest == p2, lane, LANES), axis=-1, keepdims=True)
        den = p1 + p2
        gates = jnp.where(lane == i1, p1 / den, 0.0) + jnp.where(lane == i2, p2 / den, 0.0)
        g_scr[...] = gates
        t_r = lax.broadcasted_iota(jnp.int32, (tm, tm), 0)
        t_c = lax.broadcasted_iota(jnp.int32, (tm, tm), 1)
        earlier = jnp.where(t_r > t_c, 1.0, 0.0).astype(BF16)
        sel = gates > 0.0
        rank = jnp.where(sel, _dot(earlier, jnp.where(sel, 1.0, 0.0).astype(BF16)), -1.0)
        rk_scr[...] = rank
        rkt_scr[...] = jnp.transpose(rank)[:rkt_scr.shape[0], :]

    ff = wd_ref.shape[0]
    g_e = jnp.sum(jnp.where(lane == e, g_scr[...], 0.0), axis=-1, keepdims=True)
    rk_col = jnp.sum(jnp.where(lane == e, rk_scr[...], 0.0), axis=-1, keepdims=True)
    rk_row = rkt_scr[pl.ds(e, 1), :]
    n_rows = (jnp.max(rk_row) + 1.0).astype(jnp.int32)
    g_hi = g_e.astype(BF16).astype(F32)
    g2 = jnp.where(lane == 0, g_hi, jnp.where(lane == 1, g_e - g_hi, 0.0)).astype(BF16)
    r_sub = lax.broadcasted_iota(jnp.int32, (sb, 1), 0).astype(F32)
    r_lane = lax.broadcasted_iota(jnp.int32, (1, sb), 1).astype(F32)

    def body(jb, carry):
        base = (jb * sb).astype(F32)
        pick = jnp.where(rk_row == r_sub + base, 1.0, 0.0).astype(BF16)
        xg = _dot(pick, h_scr[...]).astype(BF16)
        gr = _dot(pick, g2)
        gate_r = gr[:, 0:1] + gr[:, 1:2]
        gu = _dot(xg, wgu_ref[...])
        act = _silu(gu[:, :ff]) * gu[:, ff:]
        y = _dot(act.astype(BF16), wd_ref[...])
        put = jnp.where(rk_col == r_lane + base, 1.0, 0.0).astype(BF16)
        acc_scr[...] += _dot(put, (y * gate_r).astype(BF16))
        return carry

    lax.fori_loop(0, (n_rows + sb - 1) // sb, body, 0)

    @pl.when(e == n_exp - 1)
    def _():
        o_ref[...] = x_ref[...] + _mod_chunk(mods_ref, g, mod0 + 2, d) * acc_scr[...]


def _moe(x, norm_w, mods, router_w, w_gu, w_down, rows, mod0):
    n, d = x.shape
    tm = rows.tm
    n_exp, _, ff2 = w_gu.shape
    rw = jnp.zeros((d, LANES), F32).at[:, :n_exp].set(router_w.astype(F32))
    return pl.pallas_call(
        functools.partial(_moe_kernel, rows=rows, mod0=mod0, n_exp=n_exp, sb=MOE_ROW_BLOCK), grid=(n // tm, n_exp),
        in_specs=[pl.BlockSpec((tm, d), lambda i, e: (i, 0)),
                  pl.BlockSpec((1, d), lambda i, e: (0, 0)),
                  pl.BlockSpec(mods.shape, lambda i, e: (0, 0)),
                  pl.BlockSpec((d, LANES), lambda i, e: (0, 0)),
                  pl.BlockSpec((None, d, ff2), lambda i, e: (e, 0, 0)),
                  pl.BlockSpec((None, ff2 // 2, d), lambda i, e: (e, 0, 0))],
        out_specs=pl.BlockSpec((tm, d), lambda i, e: (i, 0)),
        out_shape=jax.ShapeDtypeStruct((n, d), F32),
        scratch_shapes=[pltpu.VMEM((tm, d), BF16), pltpu.VMEM((tm, LANES), F32), pltpu.VMEM((tm, LANES), F32),
                        pltpu.VMEM((SUBLANES * ((n_exp + SUBLANES - 1) // SUBLANES), tm), F32),
                        pltpu.VMEM((tm, d), F32)],
        compiler_params=_cparams("parallel", "arbitrary"), name="moe",
    )(x, norm_w.reshape(1, d).astype(F32), mods, rw, w_gu, w_down)


def _final_kernel(x_ref, w_ref, o_ref):
    o_ref[...] = _rmsnorm_rows(x_ref[...], w_ref[...])


def _final_norm(x, w, tm):
    n, d = x.shape
    return pl.pallas_call(
        _final_kernel, grid=(n // tm,),
        in_specs=[pl.BlockSpec((tm, d), lambda i: (i, 0)), pl.BlockSpec((1, d), lambda i: (0, 0))],
        out_specs=pl.BlockSpec((tm, d), lambda i: (i, 0)),
        out_shape=jax.ShapeDtypeStruct((n, d), F32),
        compiler_params=_cparams("parallel"), name="final_norm",
    )(x, w.reshape(1, d).astype(F32))


def kernel(x_prompt, x_sample, state_delta, cache_diff_k, cache_diff_v, cache_mla_ckv, cache_mla_krope, c, c_ctx,
           mod_w, mod_b, norm_mix, norm_ffn, final_norm, ev_w_in, ev_conv_w, ev_a_log, ev_dt_bias, ev_onorm,
           ev_lambda, ev_subln, ev_w_out, ffn_w_gu, ffn_w_down, od_w_in, od_q_norm, od_kv_norm, od_w_uq, od_w_ukv,
           od_w_out, moe_router, moe_w_gu, moe_w_down):
    bp, lp, d = x_prompt.shape
    bs, ls, _ = x_sample.shape
    past = cache_diff_k.shape[3]
    depth = mod_w.shape[0]
    n_p, n_s = bp * lp, bs * ls
    n = n_p + n_s
    assert n_p % ls == 0 and past == lp

    rows = _Rows(n_p, n_s, ls, _tile(math.gcd(n_p, ls), 512))
    rows_m = _Rows(n_p, n_s, ls, _tile(math.gcd(n_p, ls), 1024))
    rows_c = _Rows(n_p, n_s, ls, _tile(math.gcd(lp, ls), 256))
    rg = _tile(math.gcd(lp, ls), 256)
    cos_t, sin_t = _rope_tables(rows, ls)

    x = jnp.concatenate([x_prompt.reshape(n_p, d), x_sample.reshape(n_s, d)], axis=0)
    n_grp = 1 + bs
    cc = jnp.zeros((2 * SUBLANES * ((n_grp + 15) // 16), d), F32).at[0].set(c_ctx).at[1:n_grp].set(c)
    mods_all = _mods(cc, mod_w, mod_b)

    hq = H_A * DK_A
    hb2 = H_B * 2 * DQK_B
    sm0 = 4 * hq
    qb0 = sm0 + 4 * H_A
    q_scale = (DQK_B ** -0.5) * LOG2E
    mla_scale = ((NOPE_C + ROPE_C) ** -0.5) * LOG2E
    sd_new, dk_new, dv_new, ckv_new, kr_new = [], [], [], [], []

    for layer in range(depth):
        j = layer // 2
        mods = mods_all[layer]
        if layer % 2 == 0:
            lam_init = 0.8 - 0.6 * math.exp(-0.3 * layer)
            w = ev_w_in[j]
            sm = w[:, sm0:qb0].reshape(d, 4, H_A).transpose(0, 2, 1).reshape(d, 4 * H_A)
            w_in = jnp.concatenate([w[:, :sm0], sm, jnp.zeros((d, LANES - 4 * H_A), w.dtype), w[:, qb0:]],
                                   axis=1).astype(BF16)
            n_main = sm0 + LANES
            main, att, kv_f = _even_proj(x, norm_mix[layer], mods, w_in, (cos_t, sin_t), rows, n_main, hb2, q_scale)
            qkvn, gcol, grow = _conv_stage(main, ev_conv_w[j], ev_a_log[j], ev_dt_bias[j], rows_c, 3 * hq,
                                           sm0 // LANES)
            o_f, o_b, s_p = _delta_stage(qkvn, gcol, grow, jnp.zeros((bp, 2, H_A, DK_A, DV_A), F32), bp, lp, 0, rg)
            o_f, o_b, _ = _delta_stage(qkvn, gcol, grow, state_delta[:, j].astype(F32), bs, ls, n_p, rg,
                                       into=(o_f, o_b))
            sd_new.append(s_p)
            lam_p = ev_lambda[j].astype(F32)
            nb = hb2 // LANES
            qp, kp, vp = [(att, lambda h: h)], [(att, lambda h: nb + h)], (att, lambda h: 2 * nb + h)
            o_d = _flash(qp, kp, vp, None, lam_p, n_seq=bp, n_heads=H_B, lq=lp, lk=lp, q_row0=0, k_row0=0,
                         tq=_tile(lp, 256), tk=_tile(lp, 512), diff=True, lam_init=lam_init, out_cols=hb2)
            ck = cache_diff_k[:, j].reshape(bs * H_B * past, 2 * DQK_B).astype(BF16)
            cv = cache_diff_v[:, j].reshape(bs * H_B * past, DV_B).astype(BF16)
            cache = ([(ck, lambda h: 0)], (cv, lambda h: 0), past, lambda b, h: b * H_B + h)
            o_d = _flash(qp, kp, vp, cache, lam_p, n_seq=bs, n_heads=H_B, lq=ls, lk=ls, q_row0=n_p, k_row0=n_p,
                         tq=_tile(ls, 512), tk=_tile(ls, 2048), diff=True, lam_init=lam_init, out_cols=hb2, into=o_d)
            dk_new.append(kv_f[:n_p, :hb2].reshape(bp, lp, H_B, 2 * DQK_B).transpose(0, 2, 1, 3))
            dv_new.append(kv_f[:n_p, hb2:].reshape(bp, lp, H_B, DV_B).transpose(0, 2, 1, 3))
            x = _evout(x, o_f, o_b, main, 3 * hq // (H_A * DV_A), o_d, ev_onorm[j], ev_subln[j],
                       ev_w_out[j].astype(BF16), mods, rows, 2, lam_init)
            ff = ffn_w_down.shape[1]
            x = _ffn(x, norm_ffn[layer], mods, ffn_w_gu[j].astype(BF16), ffn_w_down[j].astype(BF16), rows, 3,
                     ff // 2 if (ff // 2) % LANES == 0 else ff)
        else:
            n_in = Q_LORA + KV_LORA + ROPE_C
            pad = (-n_in) % LANES
            w_in = jnp.concatenate([od_w_in[j], jnp.zeros((d, pad), od_w_in.dtype)], axis=1).astype(BF16)
            wq = od_w_uq[j].reshape(Q_LORA, H_C, NOPE_C + ROPE_C)
            wq_rope = jnp.concatenate([wq[:, :, NOPE_C:], jnp.zeros((Q_LORA, H_C, LANES - ROPE_C), wq.dtype)], axis=2)
            wq2 = jnp.concatenate([wq[:, :, :NOPE_C].reshape(Q_LORA, H_C * NOPE_C),
                                   wq_rope.reshape(Q_LORA, H_C * LANES)], axis=1).astype(BF16)
            q_all, kv_tok, kr_tok, ckv_n, kr_f = _odd_proj(
                x, norm_mix[layer], mods, w_in, od_q_norm[j], od_kv_norm[j], wq2, od_w_ukv[j].astype(BF16),
                (cos_t, sin_t), rows, mla_scale, H_C * NOPE_C)
            ckv_new.append(ckv_n[:n_p].reshape(bp, lp, KV_LORA))
            kr_new.append(kr_f[:n_p, :ROPE_C].reshape(bp, lp, ROPE_C))
            rows_k = _Rows(bs * past, 0, past, _tile(past, 256))
            kv_c = _norm_matmul(cache_mla_ckv[:, j].reshape(bs * past, KV_LORA), od_w_ukv[j].astype(BF16), rows_k,
                                tn=512, out_dtype=BF16)
            kr_c = jnp.concatenate([cache_mla_krope[:, j].reshape(bs * past, ROPE_C),
                                    jnp.zeros((bs * past, LANES - ROPE_C), F32)], axis=1).astype(BF16)
            qp = [(q_all, lambda h: h), (q_all, lambda h: H_C + h)]
            kp = [(kv_tok, lambda h: 2 * h), (kr_tok, lambda h: 0)]
            vp = (kv_tok, lambda h: 2 * h + 1)
            o = _flash(qp, kp, vp, None, None, n_seq=bp, n_heads=H_C, lq=lp, lk=lp, q_row0=0, k_row0=0,
                       tq=_tile(lp, 256), tk=_tile(lp, 512), diff=False, lam_init=0.0, out_cols=H_C * V_C)
            cache = ([(kv_c, lambda h: 2 * h), (kr_c, lambda h: 0)], (kv_c, lambda h: 2 * h + 1), past,
                     lambda b, h: b)
            o = _flash(qp, kp, vp, cache, None, n_seq=bs, n_heads=H_C, lq=ls, lk=ls, q_row0=n_p, k_row0=n_p,
                       tq=_tile(ls, 512), tk=_tile(ls, 2048), diff=False, lam_init=0.0, out_cols=H_C * V_C, into=o)
            x = _matmul_residual(x, o, od_w_out[j].astype(BF16), mods, rows, 2)
            x = _moe(x, norm_ffn[layer], mods, moe_router[j], moe_w_gu[j].astype(BF16), moe_w_down[j].astype(BF16),
                     rows_m, 3)

    y = _final_norm(x, final_norm, rows.tm)
    return (y[:n_p].reshape(bp, lp, d), y[n_p:].reshape(bs, ls, d), jnp.stack(sd_new, axis=1),
            jnp.stack(dk_new, axis=1), jnp.stack(dv_new, axis=1), jnp.stack(ckv_new, axis=1),
            jnp.stack(kr_new, axis=1))
```

```python
import functools
import math

import numpy as np
import jax
import jax.numpy as jnp
from jax import lax
from jax.experimental import pallas as pl
from jax.experimental.pallas import tpu as pltpu

F32 = jnp.float32
BF16 = jnp.bfloat16

EPS = 1e-6
LOG2E = 1.4426950408889634
GRID_W = 64
ROPE_BASE = 10000.0
H_A, DK_A, DV_A = 4, 128, 128
CONV_K = 3
CHUNK = 64
H_B, DQK_B, DV_B = 4, 64, 128
H_C, NOPE_C, ROPE_C, V_C = 8, 128, 64, 128
Q_LORA, KV_LORA = 384, 256
LANES = 128
SUBLANES = 8
VMEM_LIMIT_BYTES = 56 * 1024 * 1024


def _cparams(*sem):
    return pltpu.CompilerParams(dimension_semantics=sem, vmem_limit_bytes=VMEM_LIMIT_BYTES)


def _dot(a, b):
    return jnp.dot(a, b, preferred_element_type=F32)


def _dot_nt(a, b):
    return lax.dot_general(a, b, (((1,), (1,)), ((), ())), preferred_element_type=F32)


def _dot_tn(a, b):
    return lax.dot_general(a, b, (((0,), (0,)), ((), ())), preferred_element_type=F32)


def _split3(x):
    hi = x.astype(BF16)
    r = x - hi.astype(F32)
    mid = r.astype(BF16)
    lo = (r - mid.astype(F32)).astype(BF16)
    return hi, mid, lo


def _dot_f32(a, b):
    ah, am, al = _split3(a)
    bh, bm, bl = _split3(b)
    return (_dot(ah, bh) + (_dot(ah, bm) + _dot(am, bh))
            + (_dot(am, bm) + _dot(ah, bl) + _dot(al, bh)))


def _silu(x):
    return x * jax.nn.sigmoid(x)


def _tile(n, pref, mult=SUBLANES):
    t = min(n, pref)
    while t > mult and (n % t or t % mult):
        t -= mult
    assert n % t == 0, (n, pref)
    return t


class _Rows:
    def __init__(self, n_ctx, n_lat, lat_len, tm):
        assert n_ctx % tm == 0 and lat_len % tm == 0
        self.tm = tm
        self.n = n_ctx + n_lat
        self.ctx_tiles = n_ctx // tm
        self.seq_tiles = lat_len // tm

    def group(self, i):
        return jnp.where(i < self.ctx_tiles, 0, 1 + jnp.maximum(i - self.ctx_tiles, 0) // self.seq_tiles)

    def pos_block(self, i):
        return jnp.where(i < self.ctx_tiles, 0, 1 + jnp.maximum(i - self.ctx_tiles, 0) % self.seq_tiles)


def _mod_kernel(c_ref, w_ref, b_ref, o_ref):
    o_ref[...] = _dot_f32(_silu(c_ref[...]), w_ref[...]) + b_ref[...]


def _mods(cc, mod_w, mod_b):
    depth, d, n6 = mod_w.shape
    tn = _tile(n6, 512, LANES)
    return pl.pallas_call(
        _mod_kernel,
        grid=(depth, n6 // tn),
        in_specs=[pl.BlockSpec(cc.shape, lambda l, j: (0, 0)),
                  pl.BlockSpec((None, d, tn), lambda l, j: (l, 0, j)),
                  pl.BlockSpec((None, 1, tn), lambda l, j: (l, 0, j))],
        out_specs=pl.BlockSpec((None, cc.shape[0], tn), lambda l, j: (l, 0, j)),
        out_shape=jax.ShapeDtypeStruct((depth, cc.shape[0], n6), F32),
        compiler_params=_cparams("parallel", "parallel"),
        name="mods",
    )(cc, mod_w, mod_b.reshape(depth, 1, n6))


def _mod_chunk(mods_ref, g, idx, d):
    return mods_ref[pl.ds(g, 1), idx * d:(idx + 1) * d]


def _rmsnorm_rows(x, w):
    ms = jnp.mean(x * x, axis=-1, keepdims=True)
    return x * lax.rsqrt(ms + EPS) * w


def _rope(x, cos, sin_signed):
    lane = lax.broadcasted_iota(jnp.int32, (1, LANES), 1)
    low = (lane % 32) < 16
    outs = []
    for c in range(x.shape[1] // LANES):
        xc = x[:, c * LANES:(c + 1) * LANES]
        fwd = pltpu.roll(xc, LANES - 16, axis=1)
        bwd = pltpu.roll(xc, 16, axis=1)
        outs.append(xc * cos + jnp.where(low, fwd, bwd) * sin_signed)
    return outs[0] if len(outs) == 1 else jnp.concatenate(outs, axis=1)


def _rope_tables(rows, lat_len):
    pos = jnp.arange(lat_len)
    r = (pos // GRID_W).astype(F32)
    c = (pos % GRID_W).astype(F32)
    inv = ROPE_BASE ** (-jnp.arange(0, 32, 2, dtype=F32) / 32)
    ang_r = r[:, None] * inv[None, :]
    ang_c = c[:, None] * inv[None, :]
    ang = jnp.concatenate([ang_r, ang_r, ang_c, ang_c] * 2, axis=-1)
    sign = jnp.where((jnp.arange(LANES) % 32) < 16, -1.0, 1.0).astype(F32)
    cos = jnp.concatenate([jnp.ones((rows.tm, LANES), F32), jnp.cos(ang)], axis=0)
    sin = jnp.concatenate([jnp.zeros((rows.tm, LANES), F32), jnp.sin(ang) * sign[None, :]], axis=0)
    return cos, sin


def _nm_kernel(*refs, rows, d_mod, has_norm, mod_idx, has_rope, epi, emit_h):
    it = iter(refs)
    x_ref = next(it)
    w_ref = next(it)
    nw_ref = next(it) if has_norm else None
    mods_ref = next(it) if mod_idx is not None else None
    cos_ref, sin_ref = (next(it), next(it)) if has_rope else (None, None)
    o_ref = next(it)
    h_out_ref = next(it) if emit_h else None
    h_scr = next(it)
    i, j = pl.program_id(0), pl.program_id(1)

    @pl.when(j == 0)
    def _():
        x = x_ref[...].astype(F32)
        if has_norm:
            x = _rmsnorm_rows(x, nw_ref[...])
        if mod_idx is not None:
            g = rows.group(i)
            x = x * (1.0 + _mod_chunk(mods_ref, g, mod_idx[1], d_mod)) + _mod_chunk(mods_ref, g, mod_idx[0], d_mod)
        h_scr[...] = x.astype(BF16)
        if emit_h:
            h_out_ref[...] = x

    acc = _dot(h_scr[...], w_ref[...])
    kinds = sorted(set(epi))
    for kind in kinds:
        cond = None
        for jj, e in enumerate(epi):
            if e == kind:
                cond = (j == jj) if cond is None else (cond | (j == jj))

        def _store(kind=kind):
            y = acc
            if kind[0]:
                y = _rope(y, cos_ref[...], sin_ref[...])
            if kind[1] != 1.0:
                y = y * kind[1]
            o_ref[...] = y.astype(o_ref.dtype)

        if len(kinds) == 1:
            _store()
        else:
            pl.when(cond)(_store)


def _norm_matmul(x, w, rows, *, tn, norm_w=None, mods=None, mod_idx=None, rope=None, epi=None,
                 emit_h=False, out_dtype=F32):
    n, k = x.shape
    nout = w.shape[1]
    nj = nout // tn
    tm = rows.tm
    epi = tuple(epi) if epi is not None else ((False, 1.0),) * nj
    assert len(epi) == nj and n % tm == 0
    has_rope = any(e[0] for e in epi)
    args = [x, w]
    in_specs = [pl.BlockSpec((tm, k), lambda i, j: (i, 0)), pl.BlockSpec((k, tn), lambda i, j: (0, j))]
    if norm_w is not None:
        args.append(norm_w.reshape(1, k).astype(F32))
        in_specs.append(pl.BlockSpec((1, k), lambda i, j: (0, 0)))
    if mod_idx is not None:
        args.append(mods)
        in_specs.append(pl.BlockSpec(mods.shape, lambda i, j: (0, 0)))
    if has_rope:
        args += list(rope)
        in_specs += [pl.BlockSpec((tm, LANES), lambda i, j: (rows.pos_block(i), 0))] * 2
    out_shape = [jax.ShapeDtypeStruct((n, nout), out_dtype)]
    out_specs = [pl.BlockSpec((tm, tn), lambda i, j: (i, j))]
    if emit_h:
        out_shape.append(jax.ShapeDtypeStruct((n, k), F32))
        out_specs.append(pl.BlockSpec((tm, k), lambda i, j: (i, 0)))
    kern = functools.partial(_nm_kernel, rows=rows, d_mod=k, has_norm=norm_w is not None, mod_idx=mod_idx,
                             has_rope=has_rope, epi=epi, emit_h=emit_h)
    out = pl.pallas_call(
        kern, grid=(n // tm, nj), in_specs=in_specs, out_specs=out_specs, out_shape=out_shape,
        scratch_shapes=[pltpu.VMEM((tm, k), BF16)],
        compiler_params=_cparams("parallel", "arbitrary"), name="norm_matmul",
    )(*args)
    return out if emit_h else out[0]


PROJ_CHUNK = 512


def _even_proj_kernel(x_ref, nw_ref, mods_ref, w_ref, cos_ref, sin_ref, main_ref, att_ref, kv_ref,
                      *, rows, n_main, hb2, q_scale):
    i = pl.program_id(0)
    d = x_ref.shape[1]
    h = _modulated(x_ref, nw_ref, mods_ref, rows.group(i), 0, d).astype(BF16)
    for c0 in range(0, n_main, PROJ_CHUNK):
        c1 = min(c0 + PROJ_CHUNK, n_main)
        main_ref[:, c0:c1] = _dot(h, w_ref[:, c0:c1])
    cos, sin = cos_ref[...], sin_ref[...]
    for part in range(3):
        for c0 in range(0, hb2, PROJ_CHUNK):
            c1 = min(c0 + PROJ_CHUNK, hb2)
            y = _dot(h, w_ref[:, n_main + part * hb2 + c0:n_main + part * hb2 + c1])
            if part < 2:
                y = _rope(y, cos, sin)
            if part == 0:
                y = y * q_scale
            else:
                kv_ref[:, (part - 1) * hb2 + c0:(part - 1) * hb2 + c1] = y
            att_ref[:, part * hb2 + c0:part * hb2 + c1] = y.astype(BF16)


def _even_proj(x, norm_w, mods, w, rope, rows, n_main, hb2, q_scale):
    n, d = x.shape
    tm = rows.tm
    kern = functools.partial(_even_proj_kernel, rows=rows, n_main=n_main, hb2=hb2, q_scale=q_scale)
    tab = pl.BlockSpec((tm, LANES), lambda i: (rows.pos_block(i), 0))
    return pl.pallas_call(
        kern, grid=(n // tm,),
        in_specs=[pl.BlockSpec((tm, d), lambda i: (i, 0)), pl.BlockSpec((1, d), lambda i: (0, 0)),
                  pl.BlockSpec(mods.shape, lambda i: (0, 0)), pl.BlockSpec(w.shape, lambda i: (0, 0)), tab, tab],
        out_specs=[pl.BlockSpec((tm, n_main), lambda i: (i, 0)), pl.BlockSpec((tm, 3 * hb2), lambda i: (i, 0)),
                   pl.BlockSpec((tm, 2 * hb2), lambda i: (i, 0))],
        out_shape=[jax.ShapeDtypeStruct((n, n_main), F32), jax.ShapeDtypeStruct((n, 3 * hb2), BF16),
                   jax.ShapeDtypeStruct((n, 2 * hb2), F32)],
        compiler_params=_cparams("parallel"), name="even_proj",
    )(x, norm_w.reshape(1, d).astype(F32), mods, w, *rope)


def _odd_proj_kernel(x_ref, nw_ref, mods_ref, w_ref, qn_ref, kvn_ref, wq_ref, wkv_ref, cos_ref, sin_ref,
                     q_ref, kv_ref, kr_ref, ckv_ref, krf_ref, *, rows, scale, n_nope):
    i = pl.program_id(0)
    d = x_ref.shape[1]
    h = _modulated(x_ref, nw_ref, mods_ref, rows.group(i), 0, d).astype(BF16)
    p = _dot(h, w_ref[...])
    cos, sin = cos_ref[...], sin_ref[...]
    cq = _rmsnorm_rows(p[:, :Q_LORA], qn_ref[...]).astype(BF16)
    for c0 in range(0, wq_ref.shape[1], PROJ_CHUNK):
        y = _dot(cq, wq_ref[:, c0:c0 + PROJ_CHUNK])
        if c0 >= n_nope:
            y = _rope(y, cos, sin)
        q_ref[:, c0:c0 + PROJ_CHUNK] = (y * scale).astype(BF16)
    ckv = _rmsnorm_rows(p[:, Q_LORA:Q_LORA + KV_LORA], kvn_ref[...])
    ckv_ref[...] = ckv
    ckv16 = ckv.astype(BF16)
    for c0 in range(0, wkv_ref.shape[1], PROJ_CHUNK):
        kv_ref[:, c0:c0 + PROJ_CHUNK] = _dot(ckv16, wkv_ref[:, c0:c0 + PROJ_CHUNK]).astype(BF16)
    kr = _rope(p[:, Q_LORA + KV_LORA:], cos, sin)
    krf_ref[...] = kr
    kr_ref[...] = kr.astype(BF16)


def _odd_proj(x, norm_w, mods, w_in, q_norm, kv_norm, wq, wkv, rope, rows, scale, n_nope):
    n, d = x.shape
    tm = rows.tm
    tab = pl.BlockSpec((tm, LANES), lambda i: (rows.pos_block(i), 0))
    full = lambda a: pl.BlockSpec(a.shape, lambda i: (0, 0))
    qn = q_norm.reshape(1, -1).astype(F32)
    kvn = kv_norm.reshape(1, -1).astype(F32)
    nw = norm_w.reshape(1, d).astype(F32)
    outs = [(wq.shape[1], BF16), (wkv.shape[1], BF16), (LANES, BF16), (KV_LORA, F32), (LANES, F32)]
    return pl.pallas_call(
        functools.partial(_odd_proj_kernel, rows=rows, scale=scale, n_nope=n_nope), grid=(n // tm,),
        in_specs=[pl.BlockSpec((tm, d), lambda i: (i, 0)), full(nw), full(mods), full(w_in), full(qn), full(kvn),
                  full(wq), full(wkv), tab, tab],
        out_specs=[pl.BlockSpec((tm, c), lambda i: (i, 0)) for c, _ in outs],
        out_shape=[jax.ShapeDtypeStruct((n, c), dt) for c, dt in outs],
        compiler_params=_cparams("parallel"), name="odd_proj",
    )(x, nw, mods, w_in, qn, kvn, wq, wkv, *rope)


def _conv_kernel(x_ref, prev_ref, next_ref, sm_ref, cw_ref, alog_ref, dt_ref, o_ref, gc_ref, gr_ref,
                 *, tiles_ctx, tiles_seq, halo, n_qk):
    i = pl.program_id(0)
    t = jnp.maximum(i - tiles_ctx, 0) % tiles_seq
    lat = i >= tiles_ctx
    has_prev = lat & (t > 0)
    has_next = lat & (t < tiles_seq - 1)
    x = x_ref[...]
    tm = x.shape[0]
    row = lax.broadcasted_iota(jnp.int32, (tm, 1), 0)
    p_row = jnp.where(has_prev, prev_ref[halo - 1:halo, :], 0.0)
    n_row = jnp.where(has_next, next_ref[0:1, :], 0.0)
    x_prev = jnp.where(row == 0, p_row, pltpu.roll(x, 1, axis=0))
    x_next = jnp.where(row == tm - 1, n_row, pltpu.roll(x, tm - 1, axis=0))
    cw = cw_ref[...]
    y = _silu(x_prev * cw[0:1, :] + x * cw[1:2, :] + x_next * cw[2:3, :])
    outs = []
    for hh in range(y.shape[1] // LANES):
        yh = y[:, hh * LANES:(hh + 1) * LANES]
        if hh < n_qk:
            yh = yh * lax.rsqrt(jnp.sum(yh * yh, axis=-1, keepdims=True) + EPS)
            if hh < n_qk // 2:
                yh = yh * (DK_A ** -0.5)
        outs.append(yh)
    o_ref[...] = jnp.concatenate(outs, axis=1)
    s = sm_ref[...]
    lane = lax.broadcasted_iota(jnp.int32, (1, LANES), 1)
    is_g = (lane % 4) < 2
    z = s + dt_ref[...]
    softplus = jnp.maximum(z, 0.0) + jnp.log(1.0 + jnp.exp(-jnp.abs(z)))
    g = jnp.where(is_g, -jnp.exp(alog_ref[...]) * softplus, jax.nn.sigmoid(s))
    gc_ref[...] = g
    gr_ref[...] = jnp.transpose(g)[:gr_ref.shape[0], :]


def _conv_stage(proj, conv_w, a_log, dt_bias, rows_c, n_conv, sm_col):
    n = proj.shape[0]
    tm = rows_c.tm
    halo = 16
    hb = tm // halo
    nblk16 = n // halo
    alog_row = jnp.zeros((1, LANES), F32).at[0, :4 * H_A].set(
        jnp.stack([a_log[0], a_log[1], a_log[0], a_log[1]], axis=-1).reshape(-1).astype(F32))
    dt_row = jnp.zeros((1, LANES), F32).at[0, :4 * H_A].set(
        jnp.stack([dt_bias[0], dt_bias[1], jnp.zeros_like(dt_bias[0]), jnp.zeros_like(dt_bias[0])],
                  axis=-1).reshape(-1).astype(F32))
    kern = functools.partial(_conv_kernel, tiles_ctx=rows_c.ctx_tiles, tiles_seq=rows_c.seq_tiles, halo=halo,
                             n_qk=2 * H_A)
    return pl.pallas_call(
        kern, grid=(n // tm,),
        in_specs=[pl.BlockSpec((tm, n_conv), lambda i: (i, 0)),
                  pl.BlockSpec((halo, n_conv), lambda i: (jnp.maximum(i * hb - 1, 0), 0)),
                  pl.BlockSpec((halo, n_conv), lambda i: (jnp.minimum((i + 1) * hb, nblk16 - 1), 0)),
                  pl.BlockSpec((tm, LANES), lambda i: (i, sm_col)),
                  pl.BlockSpec((CONV_K, n_conv), lambda i: (0, 0)),
                  pl.BlockSpec((1, LANES), lambda i: (0, 0)),
                  pl.BlockSpec((1, LANES), lambda i: (0, 0))],
        out_specs=[pl.BlockSpec((tm, n_conv), lambda i: (i, 0)),
                   pl.BlockSpec((tm, LANES), lambda i: (i, 0)),
                   pl.BlockSpec((4 * H_A, tm), lambda i: (0, i))],
        out_shape=[jax.ShapeDtypeStruct((n, n_conv), F32),
                   jax.ShapeDtypeStruct((n, LANES), F32),
                   jax.ShapeDtypeStruct((4 * H_A, n), F32)],
        compiler_params=_cparams("parallel"), name="conv_gate",
    )(proj, proj, proj, proj, conv_w.astype(F32), alog_row, dt_row)


def _split2(x):
    hi = x.astype(BF16)
    return hi, (x - hi.astype(F32)).astype(BF16)


def _mm(a, b, precise, dot=_dot):
    if not precise:
        return dot(a.astype(BF16), b.astype(BF16))
    ah, al = _split2(a)
    bh, bl = _split2(b)
    return dot(ah, bh) + (dot(ah, bl) + dot(al, bh))


SOLVE_BLOCK = 2 * SUBLANES
MOE_ROW_BLOCK = 128
DELTA_HEADS_PER_STEP = 4


def _tri_solve_many(ms, rs, revs):
    c, width = rs[0].shape
    blk = SOLVE_BLOCK
    nb, gpb, ng = c // blk, blk // SUBLANES, c // SUBLANES
    col = lax.broadcasted_iota(jnp.int32, (blk, c), 1)
    mgs = [[m[SUBLANES * g:SUBLANES * (g + 1), :] for g in range(ng)] for m in ms]
    xgs = [[r[SUBLANES * g:SUBLANES * (g + 1), :] for g in range(ng)] for r in rs]
    zero_blk = jnp.zeros((blk, width), BF16)
    fin_h = [[zero_blk] * nb for _ in ms]
    fin_l = [[zero_blk] * nb for _ in ms]
    for bi in range(nb):
        blocks = [nb - 1 - bi if rev else bi for rev in revs]
        if bi > 0:
            for s, (rev, b) in enumerate(zip(revs, blocks)):
                done = (col >= (b + 1) * blk) if rev else (col < b * blk)
                lh, ll = _split2(jnp.where(done, ms[s][b * blk:(b + 1) * blk, :], 0.0))
                xh = jnp.concatenate(fin_h[s], axis=0)
                xl = jnp.concatenate(fin_l[s], axis=0)
                upd = _dot(lh, xh) + (_dot(lh, xl) + _dot(ll, xh))
                for gg in range(gpb):
                    g = b * gpb + gg
                    xgs[s][g] = xgs[s][g] - upd[SUBLANES * gg:SUBLANES * (gg + 1), :]
        for t in range(blk - 1):
            for s, (rev, b) in enumerate(zip(revs, blocks)):
                j = b * blk + (blk - 1 - t if rev else t)
                xj = xgs[s][j // SUBLANES][j % SUBLANES:j % SUBLANES + 1, :]
                groups = range(b * gpb, (j - 1) // SUBLANES + 1) if rev else range((j + 1) // SUBLANES, (b + 1) * gpb)
                for g in groups:
                    xgs[s][g] = xgs[s][g] - mgs[s][g][:, j:j + 1] * xj
        if bi < nb - 1:
            for s, b in enumerate(blocks):
                fin_h[s][b], fin_l[s][b] = _split2(jnp.concatenate(xgs[s][b * gpb:(b + 1) * gpb], axis=0))
    return [jnp.concatenate(xg, axis=0) for xg in xgs]


def _delta_kernel(qf, kf, vf, gcf, grf, qb, kb, vb, gcb, grb, s0_ref, *rest, chunk, hpb, n_alias, precise):
    of_ref, ob_ref, so_ref, s_scr = rest[n_alias:]
    h0 = pl.program_id(1) * hpb
    i = pl.program_id(2)
    n_i = pl.num_programs(2)

    @pl.when(i == 0)
    def _():
        s_scr[...] = s0_ref[...]

    rg = qf.shape[0]
    nc = rg // chunk
    rowi = lax.broadcasted_iota(jnp.int32, (rg, rg), 0)
    coli = lax.broadcasted_iota(jnp.int32, (rg, rg), 1)
    same = (rowi // chunk) == (coli // chunk)
    lane = lax.broadcasted_iota(jnp.int32, (1, LANES), 1)
    r_c = lax.broadcasted_iota(jnp.int32, (chunk, chunk), 0)
    c_c = lax.broadcasted_iota(jnp.int32, (chunk, chunk), 1)

    prep = []
    for d, (q_ref, k_ref, v_ref, gc_ref, gr_ref) in enumerate(((qf, kf, vf, gcf, grf), (qb, kb, vb, gcb, grb))):
        rev = d == 1
        incl_big = same & ((rowi <= coli) if rev else (rowi >= coli))
        lm = jnp.where(incl_big, 1.0, 0.0).astype(BF16)
        gcols = gc_ref[...]
        grows = gr_ref[...]
        gh, gm, gl_ = _split3(gcols)
        cum_c = _dot(lm, gh) + _dot(lm, gm) + _dot(lm, gl_)
        th, tm_, tl = _split3(grows)
        cum_r = _dot_nt(th, lm) + _dot_nt(tm_, lm) + _dot_nt(tl, lm)

        def col(a, ln):
            return jnp.sum(jnp.where(lane == ln, a, 0.0), axis=1, keepdims=True)

        sub = lax.broadcasted_iota(jnp.int32, (cum_r.shape[0], 1), 0)
        incl = (r_c <= c_c) if rev else (r_c >= c_c)
        strict = (r_c < c_c) if rev else (r_c > c_c)
        for hh in range(hpb):
            h = h0 + hh
            hs = slice(hh * LANES, (hh + 1) * LANES)
            gcum = col(cum_c, 4 * h + d)
            beta = col(gcols, 4 * h + 2 + d)
            grow = jnp.sum(jnp.where(sub == 4 * h + d, cum_r, 0.0), axis=0, keepdims=True)
            q = q_ref[:, hs]
            k = k_ref[:, hs]
            v = v_ref[:, hs]
            kbeta = k * beta
            eg = jnp.exp(gcum)
            rhs = jnp.concatenate([v * beta, kbeta * eg], axis=1)
            ms, rs, a_in = [], [], []
            for c in range(nc):
                sl = slice(c * chunk, (c + 1) * chunk)
                e = jnp.exp(jnp.where(incl, gcum[sl] - grow[:, sl], 0.0))
                kk = _mm(kbeta[sl], k[sl], precise, _dot_nt)
                qk = _mm(q[sl], k[sl], precise, _dot_nt)
                a_in.append(jnp.where(incl, qk * e, 0.0))
                ms.append(jnp.where(strict, kk * e, 0.0))
                rs.append(rhs[sl])
            prep.append((rev, hh, ms, rs, a_in, q * eg, k, gcum))

    all_terms = []
    half = len(prep) // 2
    for wave in (prep[:half], prep[half:]) if half else (prep,):
        sols = _tri_solve_many([m for p in wave for m in p[2]], [r for p in wave for r in p[3]],
                               [p[0] for p in wave for _ in range(nc)])
        for n_u, (rev, hh, _, _, a_in, qg, k, gcum) in enumerate(wave):
            xs = sols[n_u * nc:(n_u + 1) * nc]
            terms = []
            for c in range(nc):
                sl = slice(c * chunk, (c + 1) * chunk)
                u = xs[c][:, :DV_A]
                w = xs[c][:, DV_A:]
                r_last = c * chunk if rev else (c + 1) * chunk - 1
                g_last = gcum[r_last:r_last + 1, :]
                k_dec = k[sl] * jnp.exp(g_last - gcum[sl])
                terms.append((qg[sl] - _mm(a_in[c], w, precise), _mm(a_in[c], u, precise),
                              _mm(k_dec, w, precise, _dot_tn), _mm(k_dec, u, precise, _dot_tn), jnp.exp(g_last)))
            all_terms.append(terms)

    states = [s_scr[1 if p[0] else 0, p[1]] for p in prep]
    outs = [[None] * nc for _ in prep]
    for t in range(nc):
        for n_u, p in enumerate(prep):
            c = nc - 1 - t if p[0] else t
            o_s, o_0, w_s, q_0, decay = all_terms[n_u][c]
            outs[n_u][c] = o_0 + _mm(o_s, states[n_u], precise)
            states[n_u] = states[n_u] * decay - _mm(w_s, states[n_u], precise) + q_0
    for n_u, (rev, hh) in enumerate((p[0], p[1]) for p in prep):
        s_scr[1 if rev else 0, hh] = states[n_u]
        (ob_ref if rev else of_ref)[:, hh * LANES:(hh + 1) * LANES] = jnp.concatenate(outs[n_u], axis=0)

    @pl.when(i == n_i - 1)
    def _():
        so_ref[...] = s_scr[...]


def _delta_stage(qkvn, gcol, grow, s0, n_seq, seq_len, row0, rg, into=None, precise=False):
    ng = seq_len // rg
    b0 = row0 // rg

    def fwd_blk(b, i):
        return b0 + b * ng + i

    def bwd_blk(b, i):
        return b0 + b * ng + (ng - 1 - i)

    hpb = DELTA_HEADS_PER_STEP
    nhp = H_A // hpb
    hw = hpb * LANES

    def specs(blk):
        return [pl.BlockSpec((rg, hw), lambda b, h, i: (blk(b, i), h)),
                pl.BlockSpec((rg, hw), lambda b, h, i: (blk(b, i), nhp + h)),
                pl.BlockSpec((rg, hw), lambda b, h, i: (blk(b, i), 2 * nhp + h)),
                pl.BlockSpec((rg, LANES), lambda b, h, i: (blk(b, i), 0)),
                pl.BlockSpec((4 * H_A, rg), lambda b, h, i: (0, blk(b, i)))]

    st_spec = pl.BlockSpec((None, 2, hpb, DK_A, DV_A), lambda b, h, i: (b, 0, h, 0, 0))
    args = [qkvn, qkvn, qkvn, gcol, grow, qkvn, qkvn, qkvn, gcol, grow, s0]
    in_specs = specs(fwd_blk) + specs(bwd_blk) + [st_spec]
    aliases = {}
    if into is not None:
        for k_out, arr in enumerate(into):
            aliases[len(args)] = k_out
            args.append(arr)
            in_specs.append(pl.BlockSpec(memory_space=pl.ANY))
    n = qkvn.shape[0]
    return pl.pallas_call(
        functools.partial(_delta_kernel, chunk=CHUNK, hpb=hpb, n_alias=len(aliases), precise=precise),
        grid=(n_seq, nhp, ng),
        in_specs=in_specs,
        out_specs=[pl.BlockSpec((rg, hw), lambda b, h, i: (fwd_blk(b, i), h)),
                   pl.BlockSpec((rg, hw), lambda b, h, i: (bwd_blk(b, i), h)),
                   st_spec],
        out_shape=[jax.ShapeDtypeStruct((n, H_A * DV_A), F32),
                   jax.ShapeDtypeStruct((n, H_A * DV_A), F32),
                   jax.ShapeDtypeStruct(s0.shape, F32)],
        input_output_aliases=aliases,
        scratch_shapes=[pltpu.VMEM((2, hpb, DK_A, DV_A), F32)],
        compiler_params=_cparams("parallel", "parallel", "arbitrary"), name="gated_delta",
    )(*args)


def _flash_kernel(*refs, n_parts, has_cache, diff, tk, lam_init, aliased):
    it = iter(refs)
    q_refs = [next(it) for _ in range(n_parts)]
    k_refs = [next(it) for _ in range(n_parts)]
    v_ref = next(it)
    kc_refs = [next(it) for _ in range(n_parts)] if has_cache else []
    vc_ref = next(it) if has_cache else None
    lam_ref = next(it) if diff else None
    if aliased:
        next(it)
    o_ref = next(it)

    qs = [r[...] for r in q_refs]
    tq = qs[0].shape[0]
    if diff:
        lane = lax.broadcasted_iota(jnp.int32, (1, LANES), 1)
        q = qs[0]
        qs = [jnp.concatenate([jnp.where(lane < DQK_B, q, 0), jnp.where(lane >= DQK_B, q, 0)], axis=0)]
    nrow = qs[0].shape[0]
    q_cat = qs[0] if n_parts == 1 else jnp.concatenate(qs, axis=1)

    def block(carry, ks, vv):
        m, l, acc = carry
        s = _dot_nt(q_cat, ks[0] if n_parts == 1 else jnp.concatenate(ks, axis=1))
        m_new = jnp.maximum(m, jnp.max(s, axis=-1, keepdims=True))
        alpha = jnp.exp2(m - m_new)
        p_ = jnp.exp2(s - m_new)
        l = alpha * l + jnp.sum(p_, axis=-1, keepdims=True)
        acc = alpha * acc + _dot(p_.astype(BF16), vv)
        return m_new, l, acc

    carry = (jnp.full((nrow, 1), -1e30, F32), jnp.zeros((nrow, 1), F32), jnp.zeros((nrow, v_ref.shape[1]), F32))
    lk = k_refs[0].shape[0]

    def body(t, carry):
        off = pl.multiple_of(t * tk, tk)
        return block(carry, [r[pl.ds(off, tk), :] for r in k_refs], v_ref[pl.ds(off, tk), :])

    carry = lax.fori_loop(0, lk // tk, body, carry, unroll=True)
    if has_cache:
        carry = block(carry, [r[...] for r in kc_refs], vc_ref[...])
    _, l, acc = carry
    o = acc / l
    if diff:
        lp = lam_ref[...]
        lam = (jnp.exp(jnp.sum(lp[0:1] * lp[1:2], axis=-1, keepdims=True))
               - jnp.exp(jnp.sum(lp[2:3] * lp[3:4], axis=-1, keepdims=True)) + lam_init)
        o = o[:tq] - lam * o[tq:]
    o_ref[...] = o.astype(o_ref.dtype)


def _flash(q_parts, k_parts, v_part, cache, lam, *, n_seq, n_heads, lq, lk, q_row0, k_row0, tq, tk, diff,
           lam_init, out_cols, into=None):
    nqt = lq // tq
    qb0, kb0 = q_row0 // tq, k_row0 // lk
    args, in_specs = [], []
    for arr, cf in q_parts:
        args.append(arr)
        in_specs.append(pl.BlockSpec((tq, LANES), lambda b, h, t, cf=cf: (qb0 + b * nqt + t, cf(h))))
    for arr, cf in k_parts + [v_part]:
        args.append(arr)
        in_specs.append(pl.BlockSpec((lk, LANES), lambda b, h, t, cf=cf: (kb0 + b, cf(h))))
    if cache is not None:
        c_k, c_v, c_rows, c_blk = cache
        for arr, cf in c_k + [c_v]:
            args.append(arr)
            in_specs.append(pl.BlockSpec((c_rows, LANES), lambda b, h, t, cf=cf: (c_blk(b, h), cf(h))))
    if diff:
        args.append(lam)
        in_specs.append(pl.BlockSpec(lam.shape, lambda b, h, t: (0, 0)))
    aliases = {}
    if into is not None:
        aliases[len(args)] = 0
        args.append(into)
        in_specs.append(pl.BlockSpec(memory_space=pl.ANY))
    kern = functools.partial(_flash_kernel, n_parts=len(q_parts), has_cache=cache is not None, diff=diff, tk=tk,
                             lam_init=lam_init, aliased=into is not None)
    return pl.pallas_call(
        kern, grid=(n_seq, n_heads, nqt), in_specs=in_specs,
        out_specs=pl.BlockSpec((tq, LANES), lambda b, h, t: (qb0 + b * nqt + t, h)),
        out_shape=jax.ShapeDtypeStruct((q_parts[0][0].shape[0], out_cols), F32 if diff else BF16),
        input_output_aliases=aliases,
        compiler_params=_cparams("parallel", "parallel", "arbitrary"), name="flash_attention",
    )(*args)


def _evout_kernel(x_ref, of_ref, ob_ref, z_ref, od_ref, on_ref, sn_ref, w_ref, mods_ref, o_ref,
                  *, rows, gate_idx, lam_init):
    i = pl.program_id(0)
    d = x_ref.shape[1]
    oa = of_ref[...].astype(F32) + ob_ref[...].astype(F32)
    z = z_ref[...]
    od = od_ref[...].astype(F32)
    on = on_ref[...]
    sn = sn_ref[...]
    ya, yb = [], []
    for hh in range(H_A):
        sl = slice(hh * LANES, (hh + 1) * LANES)
        ya.append(_rmsnorm_rows(oa[:, sl], on) * _silu(z[:, sl]))
    for hh in range(H_B):
        sl = slice(hh * LANES, (hh + 1) * LANES)
        yb.append(_rmsnorm_rows(od[:, sl], sn) * (1.0 - lam_init))
    y = jnp.concatenate(ya + yb, axis=1).astype(BF16)
    o_ref[...] = x_ref[...] + _mod_chunk(mods_ref, rows.group(i), gate_idx, d) * _dot(y, w_ref[...])


def _evout(x, o_f, o_b, proj, z_col, o_d, onorm, subln, w_out, mods, rows, gate_idx, lam_init):
    n, d = x.shape
    tm = rows.tm
    hw = H_A * DV_A
    kern = functools.partial(_evout_kernel, rows=rows, gate_idx=gate_idx, lam_init=lam_init)
    return pl.pallas_call(
        kern, grid=(n // tm,),
        in_specs=[pl.BlockSpec((tm, d), lambda i: (i, 0)),
                  pl.BlockSpec((tm, hw), lambda i: (i, 0)),
                  pl.BlockSpec((tm, hw), lambda i: (i, 0)),
                  pl.BlockSpec((tm, hw), lambda i: (i, z_col)),
                  pl.BlockSpec((tm, hw), lambda i: (i, 0)),
                  pl.BlockSpec((1, LANES), lambda i: (0, 0)),
                  pl.BlockSpec((1, LANES), lambda i: (0, 0)),
                  pl.BlockSpec(w_out.shape, lambda i: (0, 0)),
                  pl.BlockSpec(mods.shape, lambda i: (0, 0))],
        out_specs=pl.BlockSpec((tm, d), lambda i: (i, 0)),
        out_shape=jax.ShapeDtypeStruct((n, d), F32),
        compiler_params=_cparams("parallel"), name="even_out",
    )(x, o_f, o_b, proj, o_d, onorm.reshape(1, LANES).astype(F32), subln.reshape(1, LANES).astype(F32), w_out, mods)


def _mr_kernel(x_ref, a_ref, w_ref, mods_ref, o_ref, *, rows, gate_idx):
    i = pl.program_id(0)
    d = x_ref.shape[1]
    gate = _mod_chunk(mods_ref, rows.group(i), gate_idx, d)
    o_ref[...] = x_ref[...] + gate * _dot(a_ref[...].astype(BF16), w_ref[...])


def _matmul_residual(x, a, w, mods, rows, gate_idx):
    n, d = x.shape
    tm = rows.tm
    return pl.pallas_call(
        functools.partial(_mr_kernel, rows=rows, gate_idx=gate_idx), grid=(n // tm,),
        in_specs=[pl.BlockSpec((tm, d), lambda i: (i, 0)),
                  pl.BlockSpec((tm, a.shape[1]), lambda i: (i, 0)),
                  pl.BlockSpec(w.shape, lambda i: (0, 0)),
                  pl.BlockSpec(mods.shape, lambda i: (0, 0))],
        out_specs=pl.BlockSpec((tm, d), lambda i: (i, 0)),
        out_shape=jax.ShapeDtypeStruct((n, d), F32),
        compiler_params=_cparams("parallel"), name="matmul_residual",
    )(x, a, w, mods)


def _modulated(x_ref, nw_ref, mods_ref, g, idx, d):
    x = _rmsnorm_rows(x_ref[...], nw_ref[...])
    return x * (1.0 + _mod_chunk(mods_ref, g, idx + 1, d)) + _mod_chunk(mods_ref, g, idx, d)


def _ffn_kernel(x_ref, nw_ref, mods_ref, wg_ref, wu_ref, wd_ref, o_ref, h_scr, acc_scr, *, rows, mod0):
    i, f = pl.program_id(0), pl.program_id(1)
    d = x_ref.shape[1]
    g = rows.group(i)

    @pl.when(f == 0)
    def _():
        h_scr[...] = _modulated(x_ref, nw_ref, mods_ref, g, mod0, d).astype(BF16)
        acc_scr[...] = jnp.zeros_like(acc_scr)

    h = h_scr[...]
    act = _silu(_dot(h, wg_ref[...])) * _dot(h, wu_ref[...])
    acc_scr[...] += _dot(act.astype(BF16), wd_ref[...])

    @pl.when(f == pl.num_programs(1) - 1)
    def _():
        o_ref[...] = x_ref[...] + _mod_chunk(mods_ref, g, mod0 + 2, d) * acc_scr[...]


def _ffn(x, norm_w, mods, w_gu, w_down, rows, mod0, tf):
    n, d = x.shape
    tm = rows.tm
    ff = w_down.shape[0]
    nf = ff // tf
    return pl.pallas_call(
        functools.partial(_ffn_kernel, rows=rows, mod0=mod0), grid=(n // tm, nf),
        in_specs=[pl.BlockSpec((tm, d), lambda i, f: (i, 0)),
                  pl.BlockSpec((1, d), lambda i, f: (0, 0)),
                  pl.BlockSpec(mods.shape, lambda i, f: (0, 0)),
                  pl.BlockSpec((d, tf), lambda i, f: (0, f)),
                  pl.BlockSpec((d, tf), lambda i, f: (0, nf + f)),
                  pl.BlockSpec((tf, d), lambda i, f: (f, 0))],
        out_specs=pl.BlockSpec((tm, d), lambda i, f: (i, 0)),
        out_shape=jax.ShapeDtypeStruct((n, d), F32),
        scratch_shapes=[pltpu.VMEM((tm, d), BF16), pltpu.VMEM((tm, d), F32)],
        compiler_params=_cparams("parallel", "arbitrary"), name="ffn",
    )(x, norm_w.reshape(1, d).astype(F32), mods, w_gu, w_gu, w_down)


def _moe_kernel(x_ref, nw_ref, mods_ref, rw_ref, wgu_ref, wd_ref, *rest, rows, mod0, n_exp, sb, final):
    if final:
        fw_ref, o_ctx_ref, o_lat_ref, h_scr, g_scr, rk_scr, rkt_scr, acc_scr = rest
    else:
        o_ref, h_scr, g_scr, rk_scr, rkt_scr, acc_scr = rest
    i, e = pl.program_id(0), pl.program_id(1)
    tm, d = x_ref.shape
    g = rows.group(i)
    lane = lax.broadcasted_iota(jnp.int32, (1, LANES), 1)

    @pl.when(e == 0)
    def _():
        h = _modulated(x_ref, nw_ref, mods_ref, g, mod0, d)
        h_scr[...] = h.astype(BF16)
        acc_scr[...] = jnp.zeros_like(acc_scr)
        logits = jnp.where(lane < n_exp, _dot_f32(h, rw_ref[...]), -1e30)
        pe = jnp.exp(logits - jnp.max(logits, axis=-1, keepdims=True))
        probs = pe / jnp.sum(pe, axis=-1, keepdims=True)
        p1 = jnp.max(probs, axis=-1, keepdims=True)
        i1 = jnp.min(jnp.where(probs == p1, lane, LANES), axis=-1, keepdims=True)
        rest = jnp.where(lane == i1, -1.0, probs)
        p2 = jnp.max(rest, axis=-1, keepdims=True)
        i2 = jnp.min(jnp.where(rest == p2, lane, LANES), axis=-1, keepdims=True)
        den = p1 + p2
        gates = jnp.where(lane == i1, p1 / den, 0.0) + jnp.where(lane == i2, p2 / den, 0.0)
        g_scr[...] = gates
        t_r = lax.broadcasted_iota(jnp.int32, (tm, tm), 0)
        t_c = lax.broadcasted_iota(jnp.int32, (tm, tm), 1)
        earlier = jnp.where(t_r > t_c, 1.0, 0.0).astype(BF16)
        sel = gates > 0.0
        rank = jnp.where(sel, _dot(earlier, jnp.where(sel, 1.0, 0.0).astype(BF16)), -1.0)
        rk_scr[...] = rank
        rkt_scr[...] = jnp.transpose(rank)[:rkt_scr.shape[0], :]

    ff = wd_ref.shape[0]
    g_e = jnp.sum(jnp.where(lane == e, g_scr[...], 0.0), axis=-1, keepdims=True)
    rk_col = jnp.sum(jnp.where(lane == e, rk_scr[...], 0.0), axis=-1, keepdims=True)
    rk_row = rkt_scr[pl.ds(e, 1), :]
    n_rows = (jnp.max(rk_row) + 1.0).astype(jnp.int32)
    g_hi = g_e.astype(BF16).astype(F32)
    g2 = jnp.where(lane == 0, g_hi, jnp.where(lane == 1, g_e - g_hi, 0.0)).astype(BF16)
    r_sub = lax.broadcasted_iota(jnp.int32, (sb, 1), 0).astype(F32)
    r_lane = lax.broadcasted_iota(jnp.int32, (1, sb), 1).astype(F32)

    def body(jb, carry):
        base = (jb * sb).astype(F32)
        pick = jnp.where(rk_row == r_sub + base, 1.0, 0.0).astype(BF16)
        xg = _dot(pick, h_scr[...]).astype(BF16)
        gr = _dot(pick, g2)
        gate_r = gr[:, 0:1] + gr[:, 1:2]
        gu = _dot(xg, wgu_ref[...])
        act = _silu(gu[:, :ff]) * gu[:, ff:]
        y = _dot(act.astype(BF16), wd_ref[...])
        put = jnp.where(rk_col == r_lane + base, 1.0, 0.0).astype(BF16)
        acc_scr[...] += _dot(put, (y * gate_r).astype(BF16))
        return carry

    lax.fori_loop(0, (n_rows + sb - 1) // sb, body, 0)

    def result():
        return x_ref[...] + _mod_chunk(mods_ref, g, mod0 + 2, d) * acc_scr[...]

    last = e == n_exp - 1
    if final:
        @pl.when(last & (i < rows.ctx_tiles))
        def _():
            o_ctx_ref[...] = _rmsnorm_rows(result(), fw_ref[...])

        @pl.when(last & (i >= rows.ctx_tiles))
        def _():
            o_lat_ref[...] = _rmsnorm_rows(result(), fw_ref[...])
    else:
        @pl.when(last)
        def _():
            o_ref[...] = result()


def _moe(x, norm_w, mods, router_w, w_gu, w_down, rows, mod0, final_w=None):
    n, d = x.shape
    tm = rows.tm
    n_exp, _, ff2 = w_gu.shape
    rw = jnp.zeros((d, LANES), F32).at[:, :n_exp].set(router_w.astype(F32))
    final = final_w is not None
    args = [x, norm_w.reshape(1, d).astype(F32), mods, rw, w_gu, w_down]
    in_specs = [pl.BlockSpec((tm, d), lambda i, e: (i, 0)),
                pl.BlockSpec((1, d), lambda i, e: (0, 0)),
                pl.BlockSpec(mods.shape, lambda i, e: (0, 0)),
                pl.BlockSpec((d, LANES), lambda i, e: (0, 0)),
                pl.BlockSpec((None, d, ff2), lambda i, e: (e, 0, 0)),
                pl.BlockSpec((None, ff2 // 2, d), lambda i, e: (e, 0, 0))]
    if final:
        args.append(final_w.reshape(1, d).astype(F32))
        in_specs.append(pl.BlockSpec((1, d), lambda i, e: (0, 0)))
        ct = rows.ctx_tiles
        out_specs = [pl.BlockSpec((tm, d), lambda i, e: (jnp.minimum(i, ct - 1), 0), pipeline_mode=pl.Buffered(1)),
                     pl.BlockSpec((tm, d), lambda i, e: (jnp.maximum(i - ct, 0), 0), pipeline_mode=pl.Buffered(1))]
        out_shape = [jax.ShapeDtypeStruct((ct * tm, d), F32), jax.ShapeDtypeStruct((n - ct * tm, d), F32)]
    else:
        out_specs = pl.BlockSpec((tm, d), lambda i, e: (i, 0))
        out_shape = jax.ShapeDtypeStruct((n, d), F32)
    return pl.pallas_call(
        functools.partial(_moe_kernel, rows=rows, mod0=mod0, n_exp=n_exp, sb=MOE_ROW_BLOCK, final=final),
        grid=(n // tm, n_exp), in_specs=in_specs, out_specs=out_specs, out_shape=out_shape,
        scratch_shapes=[pltpu.VMEM((tm, d), BF16), pltpu.VMEM((tm, LANES), F32), pltpu.VMEM((tm, LANES), F32),
                        pltpu.VMEM((SUBLANES * ((n_exp + SUBLANES - 1) // SUBLANES), tm), F32),
                        pltpu.VMEM((tm, d), F32)],
        compiler_params=_cparams("arbitrary", "arbitrary"), name="moe",
    )(*args)


def _final_kernel(x_ref, w_ref, o_ref):
    o_ref[...] = _rmsnorm_rows(x_ref[...], w_ref[...])


def _final_norm(x, w, tm):
    n, d = x.shape
    return pl.pallas_call(
        _final_kernel, grid=(n // tm,),
        in_specs=[pl.BlockSpec((tm, d), lambda i: (i, 0)), pl.BlockSpec((1, d), lambda i: (0, 0))],
        out_specs=pl.BlockSpec((tm, d), lambda i: (i, 0)),
        out_shape=jax.ShapeDtypeStruct((n, d), F32),
        compiler_params=_cparams("parallel"), name="final_norm",
    )(x, w.reshape(1, d).astype(F32))


def kernel(x_prompt, x_sample, state_delta, cache_diff_k, cache_diff_v, cache_mla_ckv, cache_mla_krope, c, c_ctx,
           mod_w, mod_b, norm_mix, norm_ffn, final_norm, ev_w_in, ev_conv_w, ev_a_log, ev_dt_bias, ev_onorm,
           ev_lambda, ev_subln, ev_w_out, ffn_w_gu, ffn_w_down, od_w_in, od_q_norm, od_kv_norm, od_w_uq, od_w_ukv,
           od_w_out, moe_router, moe_w_gu, moe_w_down):
    bp, lp, d = x_prompt.shape
    bs, ls, _ = x_sample.shape
    past = cache_diff_k.shape[3]
    depth = mod_w.shape[0]
    n_p, n_s = bp * lp, bs * ls
    n = n_p + n_s
    assert n_p % ls == 0 and past == lp

    rows = _Rows(n_p, n_s, ls, _tile(math.gcd(n_p, ls), 512))
    rows_m = _Rows(n_p, n_s, ls, _tile(math.gcd(n_p, ls), 1024))
    rows_c = _Rows(n_p, n_s, ls, _tile(math.gcd(lp, ls), 256))
    rg = _tile(math.gcd(lp, ls), 256)
    cos_t, sin_t = _rope_tables(rows, ls)

    x = jnp.concatenate([x_prompt.reshape(n_p, d), x_sample.reshape(n_s, d)], axis=0)
    n_grp = 1 + bs
    cc = jnp.zeros((2 * SUBLANES * ((n_grp + 15) // 16), d), F32).at[0].set(c_ctx).at[1:n_grp].set(c)
    mods_all = _mods(cc, mod_w, mod_b)

    hq = H_A * DK_A
    hb2 = H_B * 2 * DQK_B
    sm0 = 4 * hq
    qb0 = sm0 + 4 * H_A
    q_scale = (DQK_B ** -0.5) * LOG2E
    mla_scale = ((NOPE_C + ROPE_C) ** -0.5) * LOG2E
    sd_new, dk_new, dv_new, ckv_new, kr_new = [], [], [], [], []

    for layer in range(depth):
        j = layer // 2
        mods = mods_all[layer]
        if layer % 2 == 0:
            lam_init = 0.8 - 0.6 * math.exp(-0.3 * layer)
            w = ev_w_in[j]
            sm = w[:, sm0:qb0].reshape(d, 4, H_A).transpose(0, 2, 1).reshape(d, 4 * H_A)
            w_in = jnp.concatenate([w[:, :sm0], sm, jnp.zeros((d, LANES - 4 * H_A), w.dtype), w[:, qb0:]],
                                   axis=1).astype(BF16)
            n_main = sm0 + LANES
            main, att, kv_f = _even_proj(x, norm_mix[layer], mods, w_in, (cos_t, sin_t), rows, n_main, hb2, q_scale)
            qkvn, gcol, grow = _conv_stage(main, ev_conv_w[j], ev_a_log[j], ev_dt_bias[j], rows_c, 3 * hq,
                                           sm0 // LANES)
            o_f, o_b, s_p = _delta_stage(qkvn, gcol, grow, jnp.zeros((bp, 2, H_A, DK_A, DV_A), F32), bp, lp, 0, rg,
                                         precise=True)
            o_f, o_b, _ = _delta_stage(qkvn, gcol, grow, state_delta[:, j].astype(F32), bs, ls, n_p, rg,
                                       into=(o_f, o_b))
            sd_new.append(s_p)
            lam_p = ev_lambda[j].astype(F32)
            nb = hb2 // LANES
            qp, kp, vp = [(att, lambda h: h)], [(att, lambda h: nb + h)], (att, lambda h: 2 * nb + h)
            o_d = _flash(qp, kp, vp, None, lam_p, n_seq=bp, n_heads=H_B, lq=lp, lk=lp, q_row0=0, k_row0=0,
                         tq=_tile(lp, 256), tk=_tile(lp, 512), diff=True, lam_init=lam_init, out_cols=hb2)
            ck = cache_diff_k[:, j].reshape(bs * H_B * past, 2 * DQK_B).astype(BF16)
            cv = cache_diff_v[:, j].reshape(bs * H_B * past, DV_B).astype(BF16)
            cache = ([(ck, lambda h: 0)], (cv, lambda h: 0), past, lambda b, h: b * H_B + h)
            o_d = _flash(qp, kp, vp, cache, lam_p, n_seq=bs, n_heads=H_B, lq=ls, lk=ls, q_row0=n_p, k_row0=n_p,
                         tq=_tile(ls, 512), tk=_tile(ls, 2048), diff=True, lam_init=lam_init, out_cols=hb2, into=o_d)
            dk_new.append(kv_f[:n_p, :hb2].reshape(bp, lp, H_B, 2 * DQK_B).transpose(0, 2, 1, 3))
            dv_new.append(kv_f[:n_p, hb2:].reshape(bp, lp, H_B, DV_B).transpose(0, 2, 1, 3))
            x = _evout(x, o_f, o_b, main, 3 * hq // (H_A * DV_A), o_d, ev_onorm[j], ev_subln[j],
                       ev_w_out[j].astype(BF16), mods, rows, 2, lam_init)
            ff = ffn_w_down.shape[1]
            x = _ffn(x, norm_ffn[layer], mods, ffn_w_gu[j].astype(BF16), ffn_w_down[j].astype(BF16), rows, 3,
                     ff // 2 if (ff // 2) % LANES == 0 else ff)
        else:
            n_in = Q_LORA + KV_LORA + ROPE_C
            pad = (-n_in) % LANES
            w_in = jnp.concatenate([od_w_in[j], jnp.zeros((d, pad), od_w_in.dtype)], axis=1).astype(BF16)
            wq = od_w_uq[j].reshape(Q_LORA, H_C, NOPE_C + ROPE_C)
            wq_rope = jnp.concatenate([wq[:, :, NOPE_C:], jnp.zeros((Q_LORA, H_C, LANES - ROPE_C), wq.dtype)], axis=2)
            wq2 = jnp.concatenate([wq[:, :, :NOPE_C].reshape(Q_LORA, H_C * NOPE_C),
                                   wq_rope.reshape(Q_LORA, H_C * LANES)], axis=1).astype(BF16)
            q_all, kv_tok, kr_tok, ckv_n, kr_f = _odd_proj(
                x, norm_mix[layer], mods, w_in, od_q_norm[j], od_kv_norm[j], wq2, od_w_ukv[j].astype(BF16),
                (cos_t, sin_t), rows, mla_scale, H_C * NOPE_C)
            ckv_new.append(ckv_n[:n_p].reshape(bp, lp, KV_LORA))
            kr_new.append(kr_f[:n_p, :ROPE_C].reshape(bp, lp, ROPE_C))
            rows_k = _Rows(bs * past, 0, past, _tile(past, 256))
            kv_c = _norm_matmul(cache_mla_ckv[:, j].reshape(bs * past, KV_LORA), od_w_ukv[j].astype(BF16), rows_k,
                                tn=512, out_dtype=BF16)
            kr_c = jnp.concatenate([cache_mla_krope[:, j].reshape(bs * past, ROPE_C),
                                    jnp.zeros((bs * past, LANES - ROPE_C), F32)], axis=1).astype(BF16)
            qp = [(q_all, lambda h: h), (q_all, lambda h: H_C + h)]
            kp = [(kv_tok, lambda h: 2 * h), (kr_tok, lambda h: 0)]
            vp = (kv_tok, lambda h: 2 * h + 1)
            o = _flash(qp, kp, vp, None, None, n_seq=bp, n_heads=H_C, lq=lp, lk=lp, q_row0=0, k_row0=0,
                       tq=_tile(lp, 256), tk=_tile(lp, 512), diff=False, lam_init=0.0, out_cols=H_C * V_C)
            cache = ([(kv_c, lambda h: 2 * h), (kr_c, lambda h: 0)], (kv_c, lambda h: 2 * h + 1), past,
                     lambda b, h: b)
            o = _flash(qp, kp, vp, cache, None, n_seq=bs, n_heads=H_C, lq=ls, lk=ls, q_row0=n_p, k_row0=n_p,
                       tq=_tile(ls, 1024), tk=_tile(ls, 2048), diff=False, lam_init=0.0, out_cols=H_C * V_C, into=o)
            x = _matmul_residual(x, o, od_w_out[j].astype(BF16), mods, rows, 2)
            x = _moe(x, norm_ffn[layer], mods, moe_router[j], moe_w_gu[j].astype(BF16), moe_w_down[j].astype(BF16),
                     rows_m, 3, final_w=final_norm if layer == depth - 1 else None)

    if depth % 2 == 0:
        y_p, y_s = x
    else:
        y = _final_norm(x, final_norm, rows.tm)
        y_p, y_s = y[:n_p], y[n_p:]
    return (y_p.reshape(bp, lp, d), y_s.reshape(bs, ls, d), jnp.stack(sd_new, axis=1),
            jnp.stack(dk_new, axis=1), jnp.stack(dv_new, axis=1), jnp.stack(ckv_new, axis=1),
            jnp.stack(kr_new, axis=1))
```

```python
import functools
import math

import numpy as np
import jax
import jax.numpy as jnp
from jax import lax
from jax.experimental import pallas as pl
from jax.experimental.pallas import tpu as pltpu

F32 = jnp.float32
BF16 = jnp.bfloat16

EPS = 1e-6
LOG2E = 1.4426950408889634
GRID_W = 64
ROPE_BASE = 10000.0
H_A, DK_A, DV_A = 4, 128, 128
CONV_K = 3
CHUNK = 64
H_B, DQK_B, DV_B = 4, 64, 128
H_C, NOPE_C, ROPE_C, V_C = 8, 128, 64, 128
Q_LORA, KV_LORA = 384, 256
LANES = 128
SUBLANES = 8
VMEM_LIMIT_BYTES = 56 * 1024 * 1024


def _cparams(*sem):
    return pltpu.CompilerParams(dimension_semantics=sem, vmem_limit_bytes=VMEM_LIMIT_BYTES)


def _dot(a, b):
    return jnp.dot(a, b, preferred_element_type=F32)


def _dot_nt(a, b):
    return lax.dot_general(a, b, (((1,), (1,)), ((), ())), preferred_element_type=F32)


def _dot_tn(a, b):
    return lax.dot_general(a, b, (((0,), (0,)), ((), ())), preferred_element_type=F32)


def _split3(x):
    hi = x.astype(BF16)
    r = x - hi.astype(F32)
    mid = r.astype(BF16)
    lo = (r - mid.astype(F32)).astype(BF16)
    return hi, mid, lo


def _dot_f32(a, b):
    ah, am, al = _split3(a)
    bh, bm, bl = _split3(b)
    return (_dot(ah, bh) + (_dot(ah, bm) + _dot(am, bh))
            + (_dot(am, bm) + _dot(ah, bl) + _dot(al, bh)))


def _silu(x):
    return x * jax.nn.sigmoid(x)


def _tile(n, pref, mult=SUBLANES):
    t = min(n, pref)
    while t > mult and (n % t or t % mult):
        t -= mult
    assert n % t == 0, (n, pref)
    return t


class _Rows:
    def __init__(self, n_ctx, n_lat, lat_len, tm):
        assert n_ctx % tm == 0 and lat_len % tm == 0
        self.tm = tm
        self.n = n_ctx + n_lat
        self.ctx_tiles = n_ctx // tm
        self.seq_tiles = lat_len // tm

    def group(self, i):
        return jnp.where(i < self.ctx_tiles, 0, 1 + jnp.maximum(i - self.ctx_tiles, 0) // self.seq_tiles)

    def pos_block(self, i):
        return jnp.where(i < self.ctx_tiles, 0, 1 + jnp.maximum(i - self.ctx_tiles, 0) % self.seq_tiles)


def _mod_kernel(c_ref, w_ref, b_ref, o_ref):
    o_ref[...] = _dot_f32(_silu(c_ref[...]), w_ref[...]) + b_ref[...]


def _mods(cc, mod_w, mod_b):
    depth, d, n6 = mod_w.shape
    tn = _tile(n6, 512, LANES)
    return pl.pallas_call(
        _mod_kernel,
        grid=(depth, n6 // tn),
        in_specs=[pl.BlockSpec(cc.shape, lambda l, j: (0, 0)),
                  pl.BlockSpec((None, d, tn), lambda l, j: (l, 0, j)),
                  pl.BlockSpec((None, 1, tn), lambda l, j: (l, 0, j))],
        out_specs=pl.BlockSpec((None, cc.shape[0], tn), lambda l, j: (l, 0, j)),
        out_shape=jax.ShapeDtypeStruct((depth, cc.shape[0], n6), F32),
        compiler_params=_cparams("parallel", "parallel"),
        name="mods",
    )(cc, mod_w, mod_b.reshape(depth, 1, n6))


def _mod_chunk(mods_ref, g, idx, d):
    return mods_ref[pl.ds(g, 1), idx * d:(idx + 1) * d]


def _rmsnorm_rows(x, w):
    ms = jnp.mean(x * x, axis=-1, keepdims=True)
    return x * lax.rsqrt(ms + EPS) * w


def _rope(x, cos, sin_signed):
    lane = lax.broadcasted_iota(jnp.int32, (1, LANES), 1)
    low = (lane % 32) < 16
    outs = []
    for c in range(x.shape[1] // LANES):
        xc = x[:, c * LANES:(c + 1) * LANES]
        fwd = pltpu.roll(xc, LANES - 16, axis=1)
        bwd = pltpu.roll(xc, 16, axis=1)
        outs.append(xc * cos + jnp.where(low, fwd, bwd) * sin_signed)
    return outs[0] if len(outs) == 1 else jnp.concatenate(outs, axis=1)


def _rope_tables(rows, lat_len):
    pos = jnp.arange(lat_len)
    r = (pos // GRID_W).astype(F32)
    c = (pos % GRID_W).astype(F32)
    inv = ROPE_BASE ** (-jnp.arange(0, 32, 2, dtype=F32) / 32)
    ang_r = r[:, None] * inv[None, :]
    ang_c = c[:, None] * inv[None, :]
    ang = jnp.concatenate([ang_r, ang_r, ang_c, ang_c] * 2, axis=-1)
    sign = jnp.where((jnp.arange(LANES) % 32) < 16, -1.0, 1.0).astype(F32)
    cos = jnp.concatenate([jnp.ones((rows.tm, LANES), F32), jnp.cos(ang)], axis=0)
    sin = jnp.concatenate([jnp.zeros((rows.tm, LANES), F32), jnp.sin(ang) * sign[None, :]], axis=0)
    return cos, sin


def _nm_kernel(*refs, rows, d_mod, has_norm, mod_idx, has_rope, epi, emit_h):
    it = iter(refs)
    x_ref = next(it)
    w_ref = next(it)
    nw_ref = next(it) if has_norm else None
    mods_ref = next(it) if mod_idx is not None else None
    cos_ref, sin_ref = (next(it), next(it)) if has_rope else (None, None)
    o_ref = next(it)
    h_out_ref = next(it) if emit_h else None
    h_scr = next(it)
    i, j = pl.program_id(0), pl.program_id(1)

    @pl.when(j == 0)
    def _():
        x = x_ref[...].astype(F32)
        if has_norm:
            x = _rmsnorm_rows(x, nw_ref[...])
        if mod_idx is not None:
            g = rows.group(i)
            x = x * (1.0 + _mod_chunk(mods_ref, g, mod_idx[1], d_mod)) + _mod_chunk(mods_ref, g, mod_idx[0], d_mod)
        h_scr[...] = x.astype(BF16)
        if emit_h:
            h_out_ref[...] = x

    acc = _dot(h_scr[...], w_ref[...])
    kinds = sorted(set(epi))
    for kind in kinds:
        cond = None
        for jj, e in enumerate(epi):
            if e == kind:
                cond = (j == jj) if cond is None else (cond | (j == jj))

        def _store(kind=kind):
            y = acc
            if kind[0]:
                y = _rope(y, cos_ref[...], sin_ref[...])
            if kind[1] != 1.0:
                y = y * kind[1]
            o_ref[...] = y.astype(o_ref.dtype)

        if len(kinds) == 1:
            _store()
        else:
            pl.when(cond)(_store)


def _norm_matmul(x, w, rows, *, tn, norm_w=None, mods=None, mod_idx=None, rope=None, epi=None,
                 emit_h=False, out_dtype=F32):
    n, k = x.shape
    nout = w.shape[1]
    nj = nout // tn
    tm = rows.tm
    epi = tuple(epi) if epi is not None else ((False, 1.0),) * nj
    assert len(epi) == nj and n % tm == 0
    has_rope = any(e[0] for e in epi)
    args = [x, w]
    in_specs = [pl.BlockSpec((tm, k), lambda i, j: (i, 0)), pl.BlockSpec((k, tn), lambda i, j: (0, j))]
    if norm_w is not None:
        args.append(norm_w.reshape(1, k).astype(F32))
        in_specs.append(pl.BlockSpec((1, k), lambda i, j: (0, 0)))
    if mod_idx is not None:
        args.append(mods)
        in_specs.append(pl.BlockSpec(mods.shape, lambda i, j: (0, 0)))
    if has_rope:
        args += list(rope)
        in_specs += [pl.BlockSpec((tm, LANES), lambda i, j: (rows.pos_block(i), 0))] * 2
    out_shape = [jax.ShapeDtypeStruct((n, nout), out_dtype)]
    out_specs = [pl.BlockSpec((tm, tn), lambda i, j: (i, j))]
    if emit_h:
        out_shape.append(jax.ShapeDtypeStruct((n, k), F32))
        out_specs.append(pl.BlockSpec((tm, k), lambda i, j: (i, 0)))
    kern = functools.partial(_nm_kernel, rows=rows, d_mod=k, has_norm=norm_w is not None, mod_idx=mod_idx,
                             has_rope=has_rope, epi=epi, emit_h=emit_h)
    out = pl.pallas_call(
        kern, grid=(n // tm, nj), in_specs=in_specs, out_specs=out_specs, out_shape=out_shape,
        scratch_shapes=[pltpu.VMEM((tm, k), BF16)],
        compiler_params=_cparams("parallel", "arbitrary"), name="norm_matmul",
    )(*args)
    return out if emit_h else out[0]


PROJ_CHUNK = 512


def _pos_in_seq(r, length):
    return r & (length - 1) if length & (length - 1) == 0 else lax.rem(r, length)


def _even_proj_kernel(x_ref, xp_ref, xn_ref, nw_ref, mods_ref, w_ref, cw_ref, alog_ref, dt_ref, cos_ref, sin_ref,
                      qkv_ref, z_ref, gc_ref, gr_ref, att_ref, kv_ref,
                      *, rows, n_conv, n_z, hb2, q_scale, ctx_len, lat_len, n_ctx_rows):
    i = pl.program_id(0)
    tm, d = x_ref.shape
    g = rows.group(i)

    def modulated(ref):
        v = _rmsnorm_rows(ref[...], nw_ref[...])
        return (v * (1.0 + _mod_chunk(mods_ref, g, 1, d)) + _mod_chunk(mods_ref, g, 0, d)).astype(BF16)

    h = modulated(x_ref)
    halo = xp_ref.shape[0]
    p = jnp.concatenate([_dot(h, w_ref[:, c0:min(c0 + PROJ_CHUNK, n_conv)]) for c0 in range(0, n_conv, PROJ_CHUNK)],
                        axis=1)
    p_before = _dot(modulated(xp_ref), w_ref[:, :n_conv])[halo - 1:halo, :]
    p_after = _dot(modulated(xn_ref), w_ref[:, :n_conv])[0:1, :]
    row = lax.broadcasted_iota(jnp.int32, (tm, 1), 0)
    r_glob = row + i * tm
    in_ctx = i < rows.ctx_tiles
    pos = jnp.where(in_ctx, _pos_in_seq(r_glob, ctx_len), _pos_in_seq(r_glob - n_ctx_rows, lat_len))
    seq_last = jnp.where(in_ctx, ctx_len - 1, lat_len - 1)
    x_prev = jnp.where(pos == 0, 0.0, jnp.where(row == 0, p_before, pltpu.roll(p, 1, axis=0)))
    x_next = jnp.where(pos == seq_last, 0.0, jnp.where(row == tm - 1, p_after, pltpu.roll(p, tm - 1, axis=0)))
    cw = cw_ref[...]
    y = _silu(x_prev * cw[0:1, :] + p * cw[1:2, :] + x_next * cw[2:3, :])
    n_qk = 2 * H_A
    for hh in range(n_conv // LANES):
        yh = y[:, hh * LANES:(hh + 1) * LANES]
        if hh < n_qk:
            yh = yh * lax.rsqrt(jnp.sum(yh * yh, axis=-1, keepdims=True) + EPS)
            if hh < n_qk // 2:
                yh = yh * (DK_A ** -0.5)
        qkv_ref[:, hh * LANES:(hh + 1) * LANES] = yh
    z_ref[...] = _dot(h, w_ref[:, n_conv:n_conv + n_z])
    s = _dot(h, w_ref[:, n_conv + n_z:n_conv + n_z + LANES])
    lane = lax.broadcasted_iota(jnp.int32, (1, LANES), 1)
    zg = s + dt_ref[...]
    softplus = jnp.maximum(zg, 0.0) + jnp.log(1.0 + jnp.exp(-jnp.abs(zg)))
    gates = jnp.where((lane % 4) < 2, -jnp.exp(alog_ref[...]) * softplus, jax.nn.sigmoid(s))
    gc_ref[...] = gates
    gr_ref[...] = jnp.transpose(gates)[:gr_ref.shape[0], :]
    n_main = n_conv + n_z + LANES
    cos, sin = cos_ref[...], sin_ref[...]
    for part in range(3):
        for c0 in range(0, hb2, PROJ_CHUNK):
            c1 = min(c0 + PROJ_CHUNK, hb2)
            y = _dot(h, w_ref[:, n_main + part * hb2 + c0:n_main + part * hb2 + c1])
            if part < 2:
                y = _rope(y, cos, sin)
            if part == 0:
                y = y * q_scale
            else:
                kv_ref[:, (part - 1) * hb2 + c0:(part - 1) * hb2 + c1] = y
            att_ref[:, part * hb2 + c0:part * hb2 + c1] = y.astype(BF16)


def _even_proj(x, norm_w, mods, w, conv_w, a_log, dt_bias, rope, rows, n_conv, n_z, hb2, q_scale, ctx_len,
               lat_len, n_ctx_rows):
    n, d = x.shape
    tm = rows.tm
    halo = 2 * SUBLANES
    hb = tm // halo
    alog_row = jnp.zeros((1, LANES), F32).at[0, :4 * H_A].set(
        jnp.stack([a_log[0], a_log[1], a_log[0], a_log[1]], axis=-1).reshape(-1).astype(F32))
    dt_row = jnp.zeros((1, LANES), F32).at[0, :4 * H_A].set(
        jnp.stack([dt_bias[0], dt_bias[1], jnp.zeros_like(dt_bias[0]), jnp.zeros_like(dt_bias[0])],
                  axis=-1).reshape(-1).astype(F32))
    kern = functools.partial(_even_proj_kernel, rows=rows, n_conv=n_conv, n_z=n_z, hb2=hb2, q_scale=q_scale,
                             ctx_len=ctx_len, lat_len=lat_len, n_ctx_rows=n_ctx_rows)
    tab = pl.BlockSpec((tm, LANES), lambda i: (rows.pos_block(i), 0))
    full = lambda a: pl.BlockSpec(a.shape, lambda i: (0, 0))
    nw = norm_w.reshape(1, d).astype(F32)
    cw = conv_w.astype(F32)
    outs = [(n_conv, F32), (n_z, F32), (LANES, F32), None, (3 * hb2, BF16), (2 * hb2, F32)]
    out_specs = [pl.BlockSpec((tm, c[0]), lambda i: (i, 0)) if c else pl.BlockSpec((4 * H_A, tm), lambda i: (0, i))
                 for c in outs]
    out_shape = [jax.ShapeDtypeStruct((n, c[0]), c[1]) if c else jax.ShapeDtypeStruct((4 * H_A, n), F32)
                 for c in outs]
    return pl.pallas_call(
        kern, grid=(n // tm,),
        in_specs=[pl.BlockSpec((tm, d), lambda i: (i, 0)),
                  pl.BlockSpec((halo, d), lambda i: (jnp.maximum(i * hb - 1, 0), 0)),
                  pl.BlockSpec((halo, d), lambda i: (jnp.minimum((i + 1) * hb, n // halo - 1), 0)),
                  full(nw), full(mods), full(w), full(cw), full(alog_row), full(dt_row), tab, tab],
        out_specs=out_specs, out_shape=out_shape,
        compiler_params=_cparams("parallel"), name="even_proj",
    )(x, x, x, nw, mods, w, cw, alog_row, dt_row, *rope)


def _odd_proj_kernel(x_ref, nw_ref, mods_ref, w_ref, qn_ref, kvn_ref, wq_ref, wkv_ref, cos_ref, sin_ref,
                     q_ref, kv_ref, kr_ref, ckv_ref, krf_ref, *, rows, scale, n_nope):
    i = pl.program_id(0)
    d = x_ref.shape[1]
    h = _modulated(x_ref, nw_ref, mods_ref, rows.group(i), 0, d).astype(BF16)
    p = _dot(h, w_ref[...])
    cos, sin = cos_ref[...], sin_ref[...]
    cq = _rmsnorm_rows(p[:, :Q_LORA], qn_ref[...]).astype(BF16)
    for c0 in range(0, wq_ref.shape[1], PROJ_CHUNK):
        y = _dot(cq, wq_ref[:, c0:c0 + PROJ_CHUNK])
        if c0 >= n_nope:
            y = _rope(y, cos, sin)
        q_ref[:, c0:c0 + PROJ_CHUNK] = (y * scale).astype(BF16)
    ckv = _rmsnorm_rows(p[:, Q_LORA:Q_LORA + KV_LORA], kvn_ref[...])
    ckv_ref[...] = ckv
    ckv16 = ckv.astype(BF16)
    for c0 in range(0, wkv_ref.shape[1], PROJ_CHUNK):
        kv_ref[:, c0:c0 + PROJ_CHUNK] = _dot(ckv16, wkv_ref[:, c0:c0 + PROJ_CHUNK]).astype(BF16)
    kr = _rope(p[:, Q_LORA + KV_LORA:], cos, sin)
    krf_ref[...] = kr
    kr_ref[...] = kr.astype(BF16)


def _odd_proj(x, norm_w, mods, w_in, q_norm, kv_norm, wq, wkv, rope, rows, scale, n_nope):
    n, d = x.shape
    tm = rows.tm
    tab = pl.BlockSpec((tm, LANES), lambda i: (rows.pos_block(i), 0))
    full = lambda a: pl.BlockSpec(a.shape, lambda i: (0, 0))
    qn = q_norm.reshape(1, -1).astype(F32)
    kvn = kv_norm.reshape(1, -1).astype(F32)
    nw = norm_w.reshape(1, d).astype(F32)
    outs = [(wq.shape[1], BF16), (wkv.shape[1], BF16), (LANES, BF16), (KV_LORA, F32), (LANES, F32)]
    return pl.pallas_call(
        functools.partial(_odd_proj_kernel, rows=rows, scale=scale, n_nope=n_nope), grid=(n // tm,),
        in_specs=[pl.BlockSpec((tm, d), lambda i: (i, 0)), full(nw), full(mods), full(w_in), full(qn), full(kvn),
                  full(wq), full(wkv), tab, tab],
        out_specs=[pl.BlockSpec((tm, c), lambda i: (i, 0)) for c, _ in outs],
        out_shape=[jax.ShapeDtypeStruct((n, c), dt) for c, dt in outs],
        compiler_params=_cparams("parallel"), name="odd_proj",
    )(x, nw, mods, w_in, qn, kvn, wq, wkv, *rope)


def _split2(x):
    hi = x.astype(BF16)
    return hi, (x - hi.astype(F32)).astype(BF16)


def _mm(a, b, precise, dot=_dot):
    if not precise:
        return dot(a.astype(BF16), b.astype(BF16))
    ah, al = _split2(a)
    bh, bl = _split2(b)
    return dot(ah, bh) + (dot(ah, bl) + dot(al, bh))


SOLVE_BLOCK = 2 * SUBLANES
MOE_ROW_BLOCK = 128
DELTA_HEADS_PER_STEP = 4


def _tri_solve_many(ms, rs, revs):
    c, width = rs[0].shape
    blk = SOLVE_BLOCK
    nb, gpb, ng = c // blk, blk // SUBLANES, c // SUBLANES
    col = lax.broadcasted_iota(jnp.int32, (blk, c), 1)
    mgs = [[m[SUBLANES * g:SUBLANES * (g + 1), :] for g in range(ng)] for m in ms]
    xgs = [[r[SUBLANES * g:SUBLANES * (g + 1), :] for g in range(ng)] for r in rs]
    zero_blk = jnp.zeros((blk, width), BF16)
    fin_h = [[zero_blk] * nb for _ in ms]
    fin_l = [[zero_blk] * nb for _ in ms]
    for bi in range(nb):
        blocks = [nb - 1 - bi if rev else bi for rev in revs]
        if bi > 0:
            for s, (rev, b) in enumerate(zip(revs, blocks)):
                done = (col >= (b + 1) * blk) if rev else (col < b * blk)
                lh, ll = _split2(jnp.where(done, ms[s][b * blk:(b + 1) * blk, :], 0.0))
                xh = jnp.concatenate(fin_h[s], axis=0)
                xl = jnp.concatenate(fin_l[s], axis=0)
                upd = _dot(lh, xh) + (_dot(lh, xl) + _dot(ll, xh))
                for gg in range(gpb):
                    g = b * gpb + gg
                    xgs[s][g] = xgs[s][g] - upd[SUBLANES * gg:SUBLANES * (gg + 1), :]
        for t in range(blk - 1):
            for s, (rev, b) in enumerate(zip(revs, blocks)):
                j = b * blk + (blk - 1 - t if rev else t)
                xj = xgs[s][j // SUBLANES][j % SUBLANES:j % SUBLANES + 1, :]
                groups = range(b * gpb, (j - 1) // SUBLANES + 1) if rev else range((j + 1) // SUBLANES, (b + 1) * gpb)
                for g in groups:
                    xgs[s][g] = xgs[s][g] - mgs[s][g][:, j:j + 1] * xj
        if bi < nb - 1:
            for s, b in enumerate(blocks):
                fin_h[s][b], fin_l[s][b] = _split2(jnp.concatenate(xgs[s][b * gpb:(b + 1) * gpb], axis=0))
    return [jnp.concatenate(xg, axis=0) for xg in xgs]


def _delta_kernel(qf, kf, vf, gcf, grf, qb, kb, vb, gcb, grb, s0_ref, *rest, chunk, hpb, n_alias, precise):
    of_ref, ob_ref, so_ref, s_scr = rest[n_alias:]
    h0 = pl.program_id(1) * hpb
    i = pl.program_id(2)
    n_i = pl.num_programs(2)

    @pl.when(i == 0)
    def _():
        s_scr[...] = s0_ref[...]

    rg = qf.shape[0]
    nc = rg // chunk
    rowi = lax.broadcasted_iota(jnp.int32, (rg, rg), 0)
    coli = lax.broadcasted_iota(jnp.int32, (rg, rg), 1)
    same = (rowi // chunk) == (coli // chunk)
    lane = lax.broadcasted_iota(jnp.int32, (1, LANES), 1)
    r_c = lax.broadcasted_iota(jnp.int32, (chunk, chunk), 0)
    c_c = lax.broadcasted_iota(jnp.int32, (chunk, chunk), 1)

    prep = []
    for d, (q_ref, k_ref, v_ref, gc_ref, gr_ref) in enumerate(((qf, kf, vf, gcf, grf), (qb, kb, vb, gcb, grb))):
        rev = d == 1
        incl_big = same & ((rowi <= coli) if rev else (rowi >= coli))
        lm = jnp.where(incl_big, 1.0, 0.0).astype(BF16)
        gcols = gc_ref[...]
        grows = gr_ref[...]
        gh, gm, gl_ = _split3(gcols)
        cum_c = _dot(lm, gh) + _dot(lm, gm) + _dot(lm, gl_)
        th, tm_, tl = _split3(grows)
        cum_r = _dot_nt(th, lm) + _dot_nt(tm_, lm) + _dot_nt(tl, lm)

        def col(a, ln):
            return jnp.sum(jnp.where(lane == ln, a, 0.0), axis=1, keepdims=True)

        sub = lax.broadcasted_iota(jnp.int32, (cum_r.shape[0], 1), 0)
        incl = (r_c <= c_c) if rev else (r_c >= c_c)
        strict = (r_c < c_c) if rev else (r_c > c_c)
        for hh in range(hpb):
            h = h0 + hh
            hs = slice(hh * LANES, (hh + 1) * LANES)
            gcum = col(cum_c, 4 * h + d)
            beta = col(gcols, 4 * h + 2 + d)
            grow = jnp.sum(jnp.where(sub == 4 * h + d, cum_r, 0.0), axis=0, keepdims=True)
            q = q_ref[:, hs]
            k = k_ref[:, hs]
            v = v_ref[:, hs]
            kbeta = k * beta
            eg = jnp.exp(gcum)
            rhs = jnp.concatenate([v * beta, kbeta * eg], axis=1)
            ms, rs, a_in = [], [], []
            for c in range(nc):
                sl = slice(c * chunk, (c + 1) * chunk)
                e = jnp.exp(jnp.where(incl, gcum[sl] - grow[:, sl], 0.0))
                kk = _mm(kbeta[sl], k[sl], precise, _dot_nt)
                qk = _mm(q[sl], k[sl], precise, _dot_nt)
                a_in.append(jnp.where(incl, qk * e, 0.0))
                ms.append(jnp.where(strict, kk * e, 0.0))
                rs.append(rhs[sl])
            prep.append((rev, hh, ms, rs, a_in, q * eg, k, gcum))

    all_terms = []
    half = len(prep) // 2
    for wave in (prep[:half], prep[half:]) if half else (prep,):
        sols = _tri_solve_many([m for p in wave for m in p[2]], [r for p in wave for r in p[3]],
                               [p[0] for p in wave for _ in range(nc)])
        for n_u, (rev, hh, _, _, a_in, qg, k, gcum) in enumerate(wave):
            xs = sols[n_u * nc:(n_u + 1) * nc]
            terms = []
            for c in range(nc):
                sl = slice(c * chunk, (c + 1) * chunk)
                u = xs[c][:, :DV_A]
                w = xs[c][:, DV_A:]
                r_last = c * chunk if rev else (c + 1) * chunk - 1
                g_last = gcum[r_last:r_last + 1, :]
                k_dec = k[sl] * jnp.exp(g_last - gcum[sl])
                terms.append((qg[sl] - _mm(a_in[c], w, precise), _mm(a_in[c], u, precise),
                              _mm(k_dec, w, precise, _dot_tn), _mm(k_dec, u, precise, _dot_tn), jnp.exp(g_last)))
            all_terms.append(terms)

    states = [s_scr[1 if p[0] else 0, p[1]] for p in prep]
    outs = [[None] * nc for _ in prep]
    for t in range(nc):
        for n_u, p in enumerate(prep):
            c = nc - 1 - t if p[0] else t
            o_s, o_0, w_s, q_0, decay = all_terms[n_u][c]
            outs[n_u][c] = o_0 + _mm(o_s, states[n_u], precise)
            states[n_u] = states[n_u] * decay - _mm(w_s, states[n_u], precise) + q_0
    for n_u, (rev, hh) in enumerate((p[0], p[1]) for p in prep):
        s_scr[1 if rev else 0, hh] = states[n_u]
        (ob_ref if rev else of_ref)[:, hh * LANES:(hh + 1) * LANES] = jnp.concatenate(outs[n_u], axis=0)

    @pl.when(i == n_i - 1)
    def _():
        so_ref[...] = s_scr[...]


def _delta_stage(qkvn, gcol, grow, s0, n_seq, seq_len, row0, rg, into=None, precise=False):
    ng = seq_len // rg
    b0 = row0 // rg

    def fwd_blk(b, i):
        return b0 + b * ng + i

    def bwd_blk(b, i):
        return b0 + b * ng + (ng - 1 - i)

    hpb = DELTA_HEADS_PER_STEP
    nhp = H_A // hpb
    hw = hpb * LANES

    def specs(blk):
        return [pl.BlockSpec((rg, hw), lambda b, h, i: (blk(b, i), h)),
                pl.BlockSpec((rg, hw), lambda b, h, i: (blk(b, i), nhp + h)),
                pl.BlockSpec((rg, hw), lambda b, h, i: (blk(b, i), 2 * nhp + h)),
                pl.BlockSpec((rg, LANES), lambda b, h, i: (blk(b, i), 0)),
                pl.BlockSpec((4 * H_A, rg), lambda b, h, i: (0, blk(b, i)))]

    st_spec = pl.BlockSpec((None, 2, hpb, DK_A, DV_A), lambda b, h, i: (b, 0, h, 0, 0))
    args = [qkvn, qkvn, qkvn, gcol, grow, qkvn, qkvn, qkvn, gcol, grow, s0]
    in_specs = specs(fwd_blk) + specs(bwd_blk) + [st_spec]
    aliases = {}
    if into is not None:
        for k_out, arr in enumerate(into):
            aliases[len(args)] = k_out
            args.append(arr)
            in_specs.append(pl.BlockSpec(memory_space=pl.ANY))
    n = qkvn.shape[0]
    return pl.pallas_call(
        functools.partial(_delta_kernel, chunk=CHUNK, hpb=hpb, n_alias=len(aliases), precise=precise),
        grid=(n_seq, nhp, ng),
        in_specs=in_specs,
        out_specs=[pl.BlockSpec((rg, hw), lambda b, h, i: (fwd_blk(b, i), h)),
                   pl.BlockSpec((rg, hw), lambda b, h, i: (bwd_blk(b, i), h)),
                   st_spec],
        out_shape=[jax.ShapeDtypeStruct((n, H_A * DV_A), F32),
                   jax.ShapeDtypeStruct((n, H_A * DV_A), F32),
                   jax.ShapeDtypeStruct(s0.shape, F32)],
        input_output_aliases=aliases,
        scratch_shapes=[pltpu.VMEM((2, hpb, DK_A, DV_A), F32)],
        compiler_params=_cparams("parallel", "parallel", "arbitrary"), name="gated_delta",
    )(*args)


def _flash_kernel(*refs, n_parts, has_cache, diff, tk, lam_init, aliased):
    it = iter(refs)
    q_refs = [next(it) for _ in range(n_parts)]
    k_refs = [next(it) for _ in range(n_parts)]
    v_ref = next(it)
    kc_refs = [next(it) for _ in range(n_parts)] if has_cache else []
    vc_ref = next(it) if has_cache else None
    lam_ref = next(it) if diff else None
    if aliased:
        next(it)
    o_ref = next(it)

    qs = [r[...] for r in q_refs]
    tq = qs[0].shape[0]
    if diff:
        lane = lax.broadcasted_iota(jnp.int32, (1, LANES), 1)
        q = qs[0]
        qs = [jnp.concatenate([jnp.where(lane < DQK_B, q, 0), jnp.where(lane >= DQK_B, q, 0)], axis=0)]
    nrow = qs[0].shape[0]
    q_cat = qs[0] if n_parts == 1 else jnp.concatenate(qs, axis=1)

    def block(carry, ks, vv):
        m, l, acc = carry
        s = _dot_nt(q_cat, ks[0] if n_parts == 1 else jnp.concatenate(ks, axis=1))
        m_new = jnp.maximum(m, jnp.max(s, axis=-1, keepdims=True))
        alpha = jnp.exp2(m - m_new)
        p_ = jnp.exp2(s - m_new)
        l = alpha * l + jnp.sum(p_, axis=-1, keepdims=True)
        acc = alpha * acc + _dot(p_.astype(BF16), vv)
        return m_new, l, acc

    carry = (jnp.full((nrow, 1), -1e30, F32), jnp.zeros((nrow, 1), F32), jnp.zeros((nrow, v_ref.shape[1]), F32))
    lk = k_refs[0].shape[0]

    def body(t, carry):
        off = pl.multiple_of(t * tk, tk)
        return block(carry, [r[pl.ds(off, tk), :] for r in k_refs], v_ref[pl.ds(off, tk), :])

    carry = lax.fori_loop(0, lk // tk, body, carry, unroll=True)
    if has_cache:
        carry = block(carry, [r[...] for r in kc_refs], vc_ref[...])
    _, l, acc = carry
    o = acc / l
    if diff:
        lp = lam_ref[...]
        lam = (jnp.exp(jnp.sum(lp[0:1] * lp[1:2], axis=-1, keepdims=True))
               - jnp.exp(jnp.sum(lp[2:3] * lp[3:4], axis=-1, keepdims=True)) + lam_init)
        o = o[:tq] - lam * o[tq:]
    o_ref[...] = o.astype(o_ref.dtype)


def _flash(q_parts, k_parts, v_part, cache, lam, *, n_seq, n_heads, lq, lk, q_row0, k_row0, tq, tk, diff,
           lam_init, out_cols, into=None):
    nqt = lq // tq
    qb0, kb0 = q_row0 // tq, k_row0 // lk
    args, in_specs = [], []
    for arr, cf in q_parts:
        args.append(arr)
        in_specs.append(pl.BlockSpec((tq, LANES), lambda b, h, t, cf=cf: (qb0 + b * nqt + t, cf(h))))
    for arr, cf in k_parts + [v_part]:
        args.append(arr)
        in_specs.append(pl.BlockSpec((lk, LANES), lambda b, h, t, cf=cf: (kb0 + b, cf(h))))
    if cache is not None:
        c_k, c_v, c_rows, c_blk = cache
        for arr, cf in c_k + [c_v]:
            args.append(arr)
            in_specs.append(pl.BlockSpec((c_rows, LANES), lambda b, h, t, cf=cf: (c_blk(b, h), cf(h))))
    if diff:
        args.append(lam)
        in_specs.append(pl.BlockSpec(lam.shape, lambda b, h, t: (0, 0)))
    aliases = {}
    if into is not None:
        aliases[len(args)] = 0
        args.append(into)
        in_specs.append(pl.BlockSpec(memory_space=pl.ANY))
    kern = functools.partial(_flash_kernel, n_parts=len(q_parts), has_cache=cache is not None, diff=diff, tk=tk,
                             lam_init=lam_init, aliased=into is not None)
    return pl.pallas_call(
        kern, grid=(n_seq, n_heads, nqt), in_specs=in_specs,
        out_specs=pl.BlockSpec((tq, LANES), lambda b, h, t: (qb0 + b * nqt + t, h)),
        out_shape=jax.ShapeDtypeStruct((q_parts[0][0].shape[0], out_cols), F32 if diff else BF16),
        input_output_aliases=aliases,
        compiler_params=_cparams("parallel", "parallel", "arbitrary"), name="flash_attention",
    )(*args)


def _mr_kernel(x_ref, a_ref, w_ref, mods_ref, o_ref, *, rows, gate_idx):
    i = pl.program_id(0)
    d = x_ref.shape[1]
    gate = _mod_chunk(mods_ref, rows.group(i), gate_idx, d)
    o_ref[...] = x_ref[...] + gate * _dot(a_ref[...].astype(BF16), w_ref[...])


def _matmul_residual(x, a, w, mods, rows, gate_idx):
    n, d = x.shape
    tm = rows.tm
    return pl.pallas_call(
        functools.partial(_mr_kernel, rows=rows, gate_idx=gate_idx), grid=(n // tm,),
        in_specs=[pl.BlockSpec((tm, d), lambda i: (i, 0)),
                  pl.BlockSpec((tm, a.shape[1]), lambda i: (i, 0)),
                  pl.BlockSpec(w.shape, lambda i: (0, 0)),
                  pl.BlockSpec(mods.shape, lambda i: (0, 0))],
        out_specs=pl.BlockSpec((tm, d), lambda i: (i, 0)),
        out_shape=jax.ShapeDtypeStruct((n, d), F32),
        compiler_params=_cparams("parallel"), name="matmul_residual",
    )(x, a, w, mods)


def _modulated(x_ref, nw_ref, mods_ref, g, idx, d):
    x = _rmsnorm_rows(x_ref[...], nw_ref[...])
    return x * (1.0 + _mod_chunk(mods_ref, g, idx + 1, d)) + _mod_chunk(mods_ref, g, idx, d)


def _ffn_kernel(x_ref, nw_ref, mods_ref, wg_ref, wu_ref, wd_ref, of_ref, ob_ref, z_ref, od_ref, on_ref, sn_ref,
                wo_ref, o_ref, h_scr, acc_scr, xn_scr, *, rows, mod0, mix_gate, lam_init):
    i, f = pl.program_id(0), pl.program_id(1)
    d = x_ref.shape[1]
    g = rows.group(i)

    @pl.when(f == 0)
    def _():
        oa = of_ref[...] + ob_ref[...]
        z = z_ref[...]
        od = od_ref[...]
        ya, yb = [], []
        for hh in range(H_A):
            sl = slice(hh * LANES, (hh + 1) * LANES)
            ya.append(_rmsnorm_rows(oa[:, sl], on_ref[...]) * _silu(z[:, sl]))
        for hh in range(H_B):
            sl = slice(hh * LANES, (hh + 1) * LANES)
            yb.append(_rmsnorm_rows(od[:, sl], sn_ref[...]) * (1.0 - lam_init))
        y = jnp.concatenate(ya + yb, axis=1).astype(BF16)
        x1 = x_ref[...] + _mod_chunk(mods_ref, g, mix_gate, d) * _dot(y, wo_ref[...])
        xn_scr[...] = x1
        v = _rmsnorm_rows(x1, nw_ref[...])
        h_scr[...] = (v * (1.0 + _mod_chunk(mods_ref, g, mod0 + 1, d)) + _mod_chunk(mods_ref, g, mod0, d)).astype(BF16)
        acc_scr[...] = jnp.zeros_like(acc_scr)

    h = h_scr[...]
    act = _silu(_dot(h, wg_ref[...])) * _dot(h, wu_ref[...])
    acc_scr[...] += _dot(act.astype(BF16), wd_ref[...])

    @pl.when(f == pl.num_programs(1) - 1)
    def _():
        o_ref[...] = xn_scr[...] + _mod_chunk(mods_ref, g, mod0 + 2, d) * acc_scr[...]


def _even_tail(x, o_f, o_b, z, o_d, onorm, subln, w_out, norm_w, mods, w_gu, w_down, rows, mix_gate, mod0, tf,
               lam_init):
    n, d = x.shape
    tm = rows.tm
    ff = w_down.shape[0]
    nf = ff // tf
    hw = H_A * DV_A
    row = lambda c: pl.BlockSpec((tm, c), lambda i, f: (i, 0))
    one = lambda c: pl.BlockSpec((1, c), lambda i, f: (0, 0))
    return pl.pallas_call(
        functools.partial(_ffn_kernel, rows=rows, mod0=mod0, mix_gate=mix_gate, lam_init=lam_init),
        grid=(n // tm, nf),
        in_specs=[row(d), one(d), pl.BlockSpec(mods.shape, lambda i, f: (0, 0)),
                  pl.BlockSpec((d, tf), lambda i, f: (0, f)),
                  pl.BlockSpec((d, tf), lambda i, f: (0, nf + f)),
                  pl.BlockSpec((tf, d), lambda i, f: (f, 0)),
                  row(hw), row(hw), row(hw), row(hw), one(LANES), one(LANES),
                  pl.BlockSpec(w_out.shape, lambda i, f: (0, 0))],
        out_specs=row(d),
        out_shape=jax.ShapeDtypeStruct((n, d), F32),
        scratch_shapes=[pltpu.VMEM((tm, d), BF16), pltpu.VMEM((tm, d), F32), pltpu.VMEM((tm, d), F32)],
        compiler_params=_cparams("parallel", "arbitrary"), name="even_tail",
    )(x, norm_w.reshape(1, d).astype(F32), mods, w_gu, w_gu, w_down, o_f, o_b, z, o_d,
      onorm.reshape(1, LANES).astype(F32), subln.reshape(1, LANES).astype(F32), w_out)


def _moe_kernel(x_ref, nw_ref, mods_ref, rw_ref, wgu_ref, wd_ref, *rest, rows, mod0, n_exp, sb, final):
    if final:
        fw_ref, o_ctx_ref, o_lat_ref, h_scr, g_scr, rk_scr, rkt_scr, acc_scr = rest
    else:
        o_ref, h_scr, g_scr, rk_scr, rkt_scr, acc_scr = rest
    i, e = pl.program_id(0), pl.program_id(1)
    tm, d = x_ref.shape
    g = rows.group(i)
    lane = lax.broadcasted_iota(jnp.int32, (1, LANES), 1)

    @pl.when(e == 0)
    def _():
        h = _modulated(x_ref, nw_ref, mods_ref, g, mod0, d)
        h_scr[...] = h.astype(BF16)
        acc_scr[...] = jnp.zeros_like(acc_scr)
        logits = jnp.where(lane < n_exp, _dot_f32(h, rw_ref[...]), -1e30)
        pe = jnp.exp(logits - jnp.max(logits, axis=-1, keepdims=True))
        probs = pe / jnp.sum(pe, axis=-1, keepdims=True)
        p1 = jnp.max(probs, axis=-1, keepdims=True)
        i1 = jnp.min(jnp.where(probs == p1, lane, LANES), axis=-1, keepdims=True)
        rest = jnp.where(lane == i1, -1.0, probs)
        p2 = jnp.max(rest, axis=-1, keepdims=True)
        i2 = jnp.min(jnp.where(rest == p2, lane, LANES), axis=-1, keepdims=True)
        den = p1 + p2
        gates = jnp.where(lane == i1, p1 / den, 0.0) + jnp.where(lane == i2, p2 / den, 0.0)
        g_scr[...] = gates
        t_r = lax.broadcasted_iota(jnp.int32, (tm, tm), 0)
        t_c = lax.broadcasted_iota(jnp.int32, (tm, tm), 1)
        earlier = jnp.where(t_r > t_c, 1.0, 0.0).astype(BF16)
        sel = gates > 0.0
        rank = jnp.where(sel, _dot(earlier, jnp.where(sel, 1.0, 0.0).astype(BF16)), -1.0)
        rk_scr[...] = rank
        rkt_scr[...] = jnp.transpose(rank)[:rkt_scr.shape[0], :]

    ff = wd_ref.shape[0]
    g_e = jnp.sum(jnp.where(lane == e, g_scr[...], 0.0), axis=-1, keepdims=True)
    rk_col = jnp.sum(jnp.where(lane == e, rk_scr[...], 0.0), axis=-1, keepdims=True)
    rk_row = rkt_scr[pl.ds(e, 1), :]
    n_rows = (jnp.max(rk_row) + 1.0).astype(jnp.int32)
    g_hi = g_e.astype(BF16).astype(F32)
    g2 = jnp.where(lane == 0, g_hi, jnp.where(lane == 1, g_e - g_hi, 0.0)).astype(BF16)
    r_sub = lax.broadcasted_iota(jnp.int32, (sb, 1), 0).astype(F32)
    r_lane = lax.broadcasted_iota(jnp.int32, (1, sb), 1).astype(F32)

    def body(jb, carry):
        base = (jb * sb).astype(F32)
        pick = jnp.where(rk_row == r_sub + base, 1.0, 0.0).astype(BF16)
        xg = _dot(pick, h_scr[...]).astype(BF16)
        gr = _dot(pick, g2)
        gate_r = gr[:, 0:1] + gr[:, 1:2]
        gu = _dot(xg, wgu_ref[...])
        act = _silu(gu[:, :ff]) * gu[:, ff:]
        y = _dot(act.astype(BF16), wd_ref[...])
        put = jnp.where(rk_col == r_lane + base, 1.0, 0.0).astype(BF16)
        acc_scr[...] += _dot(put, (y * gate_r).astype(BF16))
        return carry

    lax.fori_loop(0, (n_rows + sb - 1) // sb, body, 0)

    def result():
        return x_ref[...] + _mod_chunk(mods_ref, g, mod0 + 2, d) * acc_scr[...]

    last = e == n_exp - 1
    if final:
        @pl.when(last & (i < rows.ctx_tiles))
        def _():
            o_ctx_ref[...] = _rmsnorm_rows(result(), fw_ref[...])

        @pl.when(last & (i >= rows.ctx_tiles))
        def _():
            o_lat_ref[...] = _rmsnorm_rows(result(), fw_ref[...])
    else:
        @pl.when(last)
        def _():
            o_ref[...] = result()


def _moe(x, norm_w, mods, router_w, w_gu, w_down, rows, mod0, final_w=None):
    n, d = x.shape
    tm = rows.tm
    n_exp, _, ff2 = w_gu.shape
    rw = jnp.zeros((d, LANES), F32).at[:, :n_exp].set(router_w.astype(F32))
    final = final_w is not None
    args = [x, norm_w.reshape(1, d).astype(F32), mods, rw, w_gu, w_down]
    in_specs = [pl.BlockSpec((tm, d), lambda i, e: (i, 0)),
                pl.BlockSpec((1, d), lambda i, e: (0, 0)),
                pl.BlockSpec(mods.shape, lambda i, e: (0, 0)),
                pl.BlockSpec((d, LANES), lambda i, e: (0, 0)),
                pl.BlockSpec((None, d, ff2), lambda i, e: (e, 0, 0)),
                pl.BlockSpec((None, ff2 // 2, d), lambda i, e: (e, 0, 0))]
    if final:
        args.append(final_w.reshape(1, d).astype(F32))
        in_specs.append(pl.BlockSpec((1, d), lambda i, e: (0, 0)))
        ct = rows.ctx_tiles
        out_specs = [pl.BlockSpec((tm, d), lambda i, e: (jnp.minimum(i, ct - 1), 0), pipeline_mode=pl.Buffered(1)),
                     pl.BlockSpec((tm, d), lambda i, e: (jnp.maximum(i - ct, 0), 0), pipeline_mode=pl.Buffered(1))]
        out_shape = [jax.ShapeDtypeStruct((ct * tm, d), F32), jax.ShapeDtypeStruct((n - ct * tm, d), F32)]
    else:
        out_specs = pl.BlockSpec((tm, d), lambda i, e: (i, 0))
        out_shape = jax.ShapeDtypeStruct((n, d), F32)
    return pl.pallas_call(
        functools.partial(_moe_kernel, rows=rows, mod0=mod0, n_exp=n_exp, sb=MOE_ROW_BLOCK, final=final),
        grid=(n // tm, n_exp), in_specs=in_specs, out_specs=out_specs, out_shape=out_shape,
        scratch_shapes=[pltpu.VMEM((tm, d), BF16), pltpu.VMEM((tm, LANES), F32), pltpu.VMEM((tm, LANES), F32),
                        pltpu.VMEM((SUBLANES * ((n_exp + SUBLANES - 1) // SUBLANES), tm), F32),
                        pltpu.VMEM((tm, d), F32)],
        compiler_params=_cparams("arbitrary", "arbitrary"), name="moe",
    )(*args)


def _final_kernel(x_ref, w_ref, o_ref):
    o_ref[...] = _rmsnorm_rows(x_ref[...], w_ref[...])


def _final_norm(x, w, tm):
    n, d = x.shape
    return pl.pallas_call(
        _final_kernel, grid=(n // tm,),
        in_specs=[pl.BlockSpec((tm, d), lambda i: (i, 0)), pl.BlockSpec((1, d), lambda i: (0, 0))],
        out_specs=pl.BlockSpec((tm, d), lambda i: (i, 0)),
        out_shape=jax.ShapeDtypeStruct((n, d), F32),
        compiler_params=_cparams("parallel"), name="final_norm",
    )(x, w.reshape(1, d).astype(F32))


def kernel(x_prompt, x_sample, state_delta, cache_diff_k, cache_diff_v, cache_mla_ckv, cache_mla_krope, c, c_ctx,
           mod_w, mod_b, norm_mix, norm_ffn, final_norm, ev_w_in, ev_conv_w, ev_a_log, ev_dt_bias, ev_onorm,
           ev_lambda, ev_subln, ev_w_out, ffn_w_gu, ffn_w_down, od_w_in, od_q_norm, od_kv_norm, od_w_uq, od_w_ukv,
           od_w_out, moe_router, moe_w_gu, moe_w_down):
    bp, lp, d = x_prompt.shape
    bs, ls, _ = x_sample.shape
    past = cache_diff_k.shape[3]
    depth = mod_w.shape[0]
    n_p, n_s = bp * lp, bs * ls
    n = n_p + n_s
    assert n_p % ls == 0 and past == lp

    rows = _Rows(n_p, n_s, ls, _tile(math.gcd(n_p, ls), 512))
    rows_m = _Rows(n_p, n_s, ls, _tile(math.gcd(n_p, ls), 1024))
    rg = _tile(math.gcd(lp, ls), 256)
    cos_t, sin_t = _rope_tables(rows, ls)

    x = jnp.concatenate([x_prompt.reshape(n_p, d), x_sample.reshape(n_s, d)], axis=0)
    n_grp = 1 + bs
    cc = jnp.zeros((2 * SUBLANES * ((n_grp + 15) // 16), d), F32).at[0].set(c_ctx).at[1:n_grp].set(c)
    mods_all = _mods(cc, mod_w, mod_b)

    hq = H_A * DK_A
    hb2 = H_B * 2 * DQK_B
    sm0 = 4 * hq
    qb0 = sm0 + 4 * H_A
    q_scale = (DQK_B ** -0.5) * LOG2E
    mla_scale = ((NOPE_C + ROPE_C) ** -0.5) * LOG2E
    sd_new, dk_new, dv_new, ckv_new, kr_new = [], [], [], [], []

    for layer in range(depth):
        j = layer // 2
        mods = mods_all[layer]
        if layer % 2 == 0:
            lam_init = 0.8 - 0.6 * math.exp(-0.3 * layer)
            w = ev_w_in[j]
            sm = w[:, sm0:qb0].reshape(d, 4, H_A).transpose(0, 2, 1).reshape(d, 4 * H_A)
            w_in = jnp.concatenate([w[:, :sm0], sm, jnp.zeros((d, LANES - 4 * H_A), w.dtype), w[:, qb0:]],
                                   axis=1).astype(BF16)
            qkvn, z_gate, gcol, grow, att, kv_f = _even_proj(
                x, norm_mix[layer], mods, w_in, ev_conv_w[j], ev_a_log[j], ev_dt_bias[j], (cos_t, sin_t), rows,
                3 * hq, hq, hb2, q_scale, lp, ls, n_p)
            o_f, o_b, s_p = _delta_stage(qkvn, gcol, grow, jnp.zeros((bp, 2, H_A, DK_A, DV_A), F32), bp, lp, 0, rg,
                                         precise=True)
            o_f, o_b, _ = _delta_stage(qkvn, gcol, grow, state_delta[:, j].astype(F32), bs, ls, n_p, rg,
                                       into=(o_f, o_b))
            sd_new.append(s_p)
            lam_p = ev_lambda[j].astype(F32)
            nb = hb2 // LANES
            qp, kp, vp = [(att, lambda h: h)], [(att, lambda h: nb + h)], (att, lambda h: 2 * nb + h)
            o_d = _flash(qp, kp, vp, None, lam_p, n_seq=bp, n_heads=H_B, lq=lp, lk=lp, q_row0=0, k_row0=0,
                         tq=_tile(lp, 256), tk=_tile(lp, 512), diff=True, lam_init=lam_init, out_cols=hb2)
            ck = cache_diff_k[:, j].reshape(bs * H_B * past, 2 * DQK_B).astype(BF16)
            cv = cache_diff_v[:, j].reshape(bs * H_B * past, DV_B).astype(BF16)
            cache = ([(ck, lambda h: 0)], (cv, lambda h: 0), past, lambda b, h: b * H_B + h)
            o_d = _flash(qp, kp, vp, cache, lam_p, n_seq=bs, n_heads=H_B, lq=ls, lk=ls, q_row0=n_p, k_row0=n_p,
                         tq=_tile(ls, 512), tk=_tile(ls, 2048), diff=True, lam_init=lam_init, out_cols=hb2, into=o_d)
            dk_new.append(kv_f[:n_p, :hb2].reshape(bp, lp, H_B, 2 * DQK_B).transpose(0, 2, 1, 3))
            dv_new.append(kv_f[:n_p, hb2:].reshape(bp, lp, H_B, DV_B).transpose(0, 2, 1, 3))
            ff = ffn_w_down.shape[1]
            x = _even_tail(x, o_f, o_b, z_gate, o_d, ev_onorm[j], ev_subln[j], ev_w_out[j].astype(BF16),
                           norm_ffn[layer], mods, ffn_w_gu[j].astype(BF16), ffn_w_down[j].astype(BF16), rows, 2, 3,
                           ff // 2 if (ff // 2) % LANES == 0 else ff, lam_init)
        else:
            n_in = Q_LORA + KV_LORA + ROPE_C
            pad = (-n_in) % LANES
            w_in = jnp.concatenate([od_w_in[j], jnp.zeros((d, pad), od_w_in.dtype)], axis=1).astype(BF16)
            wq = od_w_uq[j].reshape(Q_LORA, H_C, NOPE_C + ROPE_C)
            wq_rope = jnp.concatenate([wq[:, :, NOPE_C:], jnp.zeros((Q_LORA, H_C, LANES - ROPE_C), wq.dtype)], axis=2)
            wq2 = jnp.concatenate([wq[:, :, :NOPE_C].reshape(Q_LORA, H_C * NOPE_C),
                                   wq_rope.reshape(Q_LORA, H_C * LANES)], axis=1).astype(BF16)
            q_all, kv_tok, kr_tok, ckv_n, kr_f = _odd_proj(
                x, norm_mix[layer], mods, w_in, od_q_norm[j], od_kv_norm[j], wq2, od_w_ukv[j].astype(BF16),
                (cos_t, sin_t), rows, mla_scale, H_C * NOPE_C)
            ckv_new.append(ckv_n[:n_p].reshape(bp, lp, KV_LORA))
            kr_new.append(kr_f[:n_p, :ROPE_C].reshape(bp, lp, ROPE_C))
            rows_k = _Rows(bs * past, 0, past, _tile(past, 256))
            kv_c = _norm_matmul(cache_mla_ckv[:, j].reshape(bs * past, KV_LORA), od_w_ukv[j].astype(BF16), rows_k,
                                tn=512, out_dtype=BF16)
            kr_c = jnp.concatenate([cache_mla_krope[:, j].reshape(bs * past, ROPE_C),
                                    jnp.zeros((bs * past, LANES - ROPE_C), F32)], axis=1).astype(BF16)
            qp = [(q_all, lambda h: h), (q_all, lambda h: H_C + h)]
            kp = [(kv_tok, lambda h: 2 * h), (kr_tok, lambda h: 0)]
            vp = (kv_tok, lambda h: 2 * h + 1)
            o = _flash(qp, kp, vp, None, None, n_seq=bp, n_heads=H_C, lq=lp, lk=lp, q_row0=0, k_row0=0,
                       tq=_tile(lp, 256), tk=_tile(lp, 512), diff=False, lam_init=0.0, out_cols=H_C * V_C)
            cache = ([(kv_c, lambda h: 2 * h), (kr_c, lambda h: 0)], (kv_c, lambda h: 2 * h + 1), past,
                     lambda b, h: b)
            o = _flash(qp, kp, vp, cache, None, n_seq=bs, n_heads=H_C, lq=ls, lk=ls, q_row0=n_p, k_row0=n_p,
                       tq=_tile(ls, 1024), tk=_tile(ls, 2048), diff=False, lam_init=0.0, out_cols=H_C * V_C, into=o)
            x = _matmul_residual(x, o, od_w_out[j].astype(BF16), mods, rows, 2)
            x = _moe(x, norm_ffn[layer], mods, moe_router[j], moe_w_gu[j].astype(BF16), moe_w_down[j].astype(BF16),
                     rows_m, 3, final_w=final_norm if layer == depth - 1 else None)

    if depth % 2 == 0:
        y_p, y_s = x
    else:
        y = _final_norm(x, final_norm, rows.tm)
        y_p, y_s = y[:n_p], y[n_p:]
    return (y_p.reshape(bp, lp, d), y_s.reshape(bs, ls, d), jnp.stack(sd_new, axis=1),
            jnp.stack(dk_new, axis=1), jnp.stack(dv_new, axis=1), jnp.stack(ckv_new, axis=1),
            jnp.stack(kr_new, axis=1))
```

```python
import functools
import math

import numpy as np
import jax
import jax.numpy as jnp
from jax import lax
from jax.experimental import pallas as pl
from jax.experimental.pallas import tpu as pltpu

F32 = jnp.float32
BF16 = jnp.bfloat16

EPS = 1e-6
LOG2E = 1.4426950408889634
GRID_W = 64
ROPE_BASE = 10000.0
H_A, DK_A, DV_A = 4, 128, 128
CONV_K = 3
CHUNK = 64
H_B, DQK_B, DV_B = 4, 64, 128
H_C, NOPE_C, ROPE_C, V_C = 8, 128, 64, 128
Q_LORA, KV_LORA = 384, 256
LANES = 128
SUBLANES = 8
VMEM_LIMIT_BYTES = 56 * 1024 * 1024


def _cparams(*sem):
    return pltpu.CompilerParams(dimension_semantics=sem, vmem_limit_bytes=VMEM_LIMIT_BYTES)


def _dot(a, b):
    return jnp.dot(a, b, preferred_element_type=F32)


def _dot_nt(a, b):
    return lax.dot_general(a, b, (((1,), (1,)), ((), ())), preferred_element_type=F32)


def _dot_tn(a, b):
    return lax.dot_general(a, b, (((0,), (0,)), ((), ())), preferred_element_type=F32)


def _split3(x):
    hi = x.astype(BF16)
    r = x - hi.astype(F32)
    mid = r.astype(BF16)
    lo = (r - mid.astype(F32)).astype(BF16)
    return hi, mid, lo


def _dot_f32(a, b):
    ah, am, al = _split3(a)
    bh, bm, bl = _split3(b)
    return (_dot(ah, bh) + (_dot(ah, bm) + _dot(am, bh))
            + (_dot(am, bm) + _dot(ah, bl) + _dot(al, bh)))


def _silu(x):
    return x * jax.nn.sigmoid(x)


def _tile(n, pref, mult=SUBLANES):
    t = min(n, pref)
    while t > mult and (n % t or t % mult):
        t -= mult
    assert n % t == 0, (n, pref)
    return t


class _Rows:
    def __init__(self, n_ctx, n_lat, lat_len, tm):
        assert n_ctx % tm == 0 and lat_len % tm == 0
        self.tm = tm
        self.n = n_ctx + n_lat
        self.ctx_tiles = n_ctx // tm
        self.seq_tiles = lat_len // tm

    def group(self, i):
        return jnp.where(i < self.ctx_tiles, 0, 1 + jnp.maximum(i - self.ctx_tiles, 0) // self.seq_tiles)

    def pos_block(self, i):
        return jnp.where(i < self.ctx_tiles, 0, 1 + jnp.maximum(i - self.ctx_tiles, 0) % self.seq_tiles)


def _mod_kernel(c_ref, w_ref, b_ref, o_ref):
    o_ref[...] = _dot_f32(_silu(c_ref[...]), w_ref[...]) + b_ref[...]


def _mods(cc, mod_w, mod_b):
    depth, d, n6 = mod_w.shape
    tn = _tile(n6, 512, LANES)
    return pl.pallas_call(
        _mod_kernel,
        grid=(depth, n6 // tn),
        in_specs=[pl.BlockSpec(cc.shape, lambda l, j: (0, 0)),
                  pl.BlockSpec((None, d, tn), lambda l, j: (l, 0, j)),
                  pl.BlockSpec((None, 1, tn), lambda l, j: (l, 0, j))],
        out_specs=pl.BlockSpec((None, cc.shape[0], tn), lambda l, j: (l, 0, j)),
        out_shape=jax.ShapeDtypeStruct((depth, cc.shape[0], n6), F32),
        compiler_params=_cparams("parallel", "parallel"),
        name="mods",
    )(cc, mod_w, mod_b.reshape(depth, 1, n6))


def _mod_chunk(mods_ref, g, idx, d):
    return mods_ref[pl.ds(g, 1), idx * d:(idx + 1) * d]


def _rmsnorm_rows(x, w):
    ms = jnp.mean(x * x, axis=-1, keepdims=True)
    return x * lax.rsqrt(ms + EPS) * w


def _rope(x, cos, sin_signed):
    lane = lax.broadcasted_iota(jnp.int32, (1, LANES), 1)
    low = (lane % 32) < 16
    outs = []
    for c in range(x.shape[1] // LANES):
        xc = x[:, c * LANES:(c + 1) * LANES]
        fwd = pltpu.roll(xc, LANES - 16, axis=1)
        bwd = pltpu.roll(xc, 16, axis=1)
        outs.append(xc * cos + jnp.where(low, fwd, bwd) * sin_signed)
    return outs[0] if len(outs) == 1 else jnp.concatenate(outs, axis=1)


def _rope_tables(rows, lat_len):
    pos = jnp.arange(lat_len)
    r = (pos // GRID_W).astype(F32)
    c = (pos % GRID_W).astype(F32)
    inv = ROPE_BASE ** (-jnp.arange(0, 32, 2, dtype=F32) / 32)
    ang_r = r[:, None] * inv[None, :]
    ang_c = c[:, None] * inv[None, :]
    ang = jnp.concatenate([ang_r, ang_r, ang_c, ang_c] * 2, axis=-1)
    sign = jnp.where((jnp.arange(LANES) % 32) < 16, -1.0, 1.0).astype(F32)
    cos = jnp.concatenate([jnp.ones((rows.tm, LANES), F32), jnp.cos(ang)], axis=0)
    sin = jnp.concatenate([jnp.zeros((rows.tm, LANES), F32), jnp.sin(ang) * sign[None, :]], axis=0)
    return cos, sin


def _nm_kernel(*refs, rows, d_mod, has_norm, mod_idx, has_rope, epi, emit_h):
    it = iter(refs)
    x_ref = next(it)
    w_ref = next(it)
    nw_ref = next(it) if has_norm else None
    mods_ref = next(it) if mod_idx is not None else None
    cos_ref, sin_ref = (next(it), next(it)) if has_rope else (None, None)
    o_ref = next(it)
    h_out_ref = next(it) if emit_h else None
    h_scr = next(it)
    i, j = pl.program_id(0), pl.program_id(1)

    @pl.when(j == 0)
    def _():
        x = x_ref[...].astype(F32)
        if has_norm:
            x = _rmsnorm_rows(x, nw_ref[...])
        if mod_idx is not None:
            g = rows.group(i)
            x = x * (1.0 + _mod_chunk(mods_ref, g, mod_idx[1], d_mod)) + _mod_chunk(mods_ref, g, mod_idx[0], d_mod)
        h_scr[...] = x.astype(BF16)
        if emit_h:
            h_out_ref[...] = x

    acc = _dot(h_scr[...], w_ref[...])
    kinds = sorted(set(epi))
    for kind in kinds:
        cond = None
        for jj, e in enumerate(epi):
            if e == kind:
                cond = (j == jj) if cond is None else (cond | (j == jj))

        def _store(kind=kind):
            y = acc
            if kind[0]:
                y = _rope(y, cos_ref[...], sin_ref[...])
            if kind[1] != 1.0:
                y = y * kind[1]
            o_ref[...] = y.astype(o_ref.dtype)

        if len(kinds) == 1:
            _store()
        else:
            pl.when(cond)(_store)


def _norm_matmul(x, w, rows, *, tn, norm_w=None, mods=None, mod_idx=None, rope=None, epi=None,
                 emit_h=False, out_dtype=F32):
    n, k = x.shape
    nout = w.shape[1]
    nj = nout // tn
    tm = rows.tm
    epi = tuple(epi) if epi is not None else ((False, 1.0),) * nj
    assert len(epi) == nj and n % tm == 0
    has_rope = any(e[0] for e in epi)
    args = [x, w]
    in_specs = [pl.BlockSpec((tm, k), lambda i, j: (i, 0)), pl.BlockSpec((k, tn), lambda i, j: (0, j))]
    if norm_w is not None:
        args.append(norm_w.reshape(1, k).astype(F32))
        in_specs.append(pl.BlockSpec((1, k), lambda i, j: (0, 0)))
    if mod_idx is not None:
        args.append(mods)
        in_specs.append(pl.BlockSpec(mods.shape, lambda i, j: (0, 0)))
    if has_rope:
        args += list(rope)
        in_specs += [pl.BlockSpec((tm, LANES), lambda i, j: (rows.pos_block(i), 0))] * 2
    out_shape = [jax.ShapeDtypeStruct((n, nout), out_dtype)]
    out_specs = [pl.BlockSpec((tm, tn), lambda i, j: (i, j))]
    if emit_h:
        out_shape.append(jax.ShapeDtypeStruct((n, k), F32))
        out_specs.append(pl.BlockSpec((tm, k), lambda i, j: (i, 0)))
    kern = functools.partial(_nm_kernel, rows=rows, d_mod=k, has_norm=norm_w is not None, mod_idx=mod_idx,
                             has_rope=has_rope, epi=epi, emit_h=emit_h)
    out = pl.pallas_call(
        kern, grid=(n // tm, nj), in_specs=in_specs, out_specs=out_specs, out_shape=out_shape,
        scratch_shapes=[pltpu.VMEM((tm, k), BF16)],
        compiler_params=_cparams("parallel", "arbitrary"), name="norm_matmul",
    )(*args)
    return out if emit_h else out[0]


PROJ_CHUNK = 512


def _pos_in_seq(r, length):
    return r & (length - 1) if length & (length - 1) == 0 else lax.rem(r, length)


def _even_proj_kernel(x_ref, xp_ref, xn_ref, nw_ref, mods_ref, w_ref, cw_ref, alog_ref, dt_ref, cos_ref, sin_ref,
                      qkv_ref, z_ref, gc_ref, gr_ref, att_ref, kv_ref,
                      *, rows, n_conv, n_z, hb2, q_scale, ctx_len, lat_len, n_ctx_rows):
    i = pl.program_id(0)
    tm, d = x_ref.shape
    g = rows.group(i)

    def modulated(ref):
        v = _rmsnorm_rows(ref[...], nw_ref[...])
        return (v * (1.0 + _mod_chunk(mods_ref, g, 1, d)) + _mod_chunk(mods_ref, g, 0, d)).astype(BF16)

    h = modulated(x_ref)
    halo = xp_ref.shape[0]
    p = jnp.concatenate([_dot(h, w_ref[:, c0:min(c0 + PROJ_CHUNK, n_conv)]) for c0 in range(0, n_conv, PROJ_CHUNK)],
                        axis=1)
    p_before = _dot(modulated(xp_ref), w_ref[:, :n_conv])[halo - 1:halo, :]
    p_after = _dot(modulated(xn_ref), w_ref[:, :n_conv])[0:1, :]
    row = lax.broadcasted_iota(jnp.int32, (tm, 1), 0)
    r_glob = row + i * tm
    in_ctx = i < rows.ctx_tiles
    pos = jnp.where(in_ctx, _pos_in_seq(r_glob, ctx_len), _pos_in_seq(r_glob - n_ctx_rows, lat_len))
    seq_last = jnp.where(in_ctx, ctx_len - 1, lat_len - 1)
    x_prev = jnp.where(pos == 0, 0.0, jnp.where(row == 0, p_before, pltpu.roll(p, 1, axis=0)))
    x_next = jnp.where(pos == seq_last, 0.0, jnp.where(row == tm - 1, p_after, pltpu.roll(p, tm - 1, axis=0)))
    cw = cw_ref[...]
    y = _silu(x_prev * cw[0:1, :] + p * cw[1:2, :] + x_next * cw[2:3, :])
    n_qk = 2 * H_A
    for hh in range(n_conv // LANES):
        yh = y[:, hh * LANES:(hh + 1) * LANES]
        if hh < n_qk:
            yh = yh * lax.rsqrt(jnp.sum(yh * yh, axis=-1, keepdims=True) + EPS)
            if hh < n_qk // 2:
                yh = yh * (DK_A ** -0.5)
        qkv_ref[:, hh * LANES:(hh + 1) * LANES] = yh
    z_ref[...] = _dot(h, w_ref[:, n_conv:n_conv + n_z])
    s = _dot(h, w_ref[:, n_conv + n_z:n_conv + n_z + LANES])
    lane = lax.broadcasted_iota(jnp.int32, (1, LANES), 1)
    zg = s + dt_ref[...]
    softplus = jnp.maximum(zg, 0.0) + jnp.log(1.0 + jnp.exp(-jnp.abs(zg)))
    gates = jnp.where((lane % 4) < 2, -jnp.exp(alog_ref[...]) * softplus, jax.nn.sigmoid(s))
    gc_ref[...] = gates
    gr_ref[...] = jnp.transpose(gates)[:gr_ref.shape[0], :]
    n_main = n_conv + n_z + LANES
    cos, sin = cos_ref[...], sin_ref[...]
    for part in range(3):
        for c0 in range(0, hb2, PROJ_CHUNK):
            c1 = min(c0 + PROJ_CHUNK, hb2)
            y = _dot(h, w_ref[:, n_main + part * hb2 + c0:n_main + part * hb2 + c1])
            if part < 2:
                y = _rope(y, cos, sin)
            if part == 0:
                y = y * q_scale
            else:
                kv_ref[:, (part - 1) * hb2 + c0:(part - 1) * hb2 + c1] = y
            att_ref[:, part * hb2 + c0:part * hb2 + c1] = y.astype(BF16)


def _even_proj(x, norm_w, mods, w, conv_w, a_log, dt_bias, rope, rows, n_conv, n_z, hb2, q_scale, ctx_len,
               lat_len, n_ctx_rows):
    n, d = x.shape
    tm = rows.tm
    halo = 2 * SUBLANES
    hb = tm // halo
    alog_row = jnp.zeros((1, LANES), F32).at[0, :4 * H_A].set(
        jnp.stack([a_log[0], a_log[1], a_log[0], a_log[1]], axis=-1).reshape(-1).astype(F32))
    dt_row = jnp.zeros((1, LANES), F32).at[0, :4 * H_A].set(
        jnp.stack([dt_bias[0], dt_bias[1], jnp.zeros_like(dt_bias[0]), jnp.zeros_like(dt_bias[0])],
                  axis=-1).reshape(-1).astype(F32))
    kern = functools.partial(_even_proj_kernel, rows=rows, n_conv=n_conv, n_z=n_z, hb2=hb2, q_scale=q_scale,
                             ctx_len=ctx_len, lat_len=lat_len, n_ctx_rows=n_ctx_rows)
    tab = pl.BlockSpec((tm, LANES), lambda i: (rows.pos_block(i), 0))
    full = lambda a: pl.BlockSpec(a.shape, lambda i: (0, 0))
    nw = norm_w.reshape(1, d).astype(F32)
    cw = conv_w.astype(F32)
    outs = [(n_conv, F32), (n_z, F32), (LANES, F32), None, (3 * hb2, BF16), (2 * hb2, F32)]
    out_specs = [pl.BlockSpec((tm, c[0]), lambda i: (i, 0)) if c else pl.BlockSpec((4 * H_A, tm), lambda i: (0, i))
                 for c in outs]
    out_shape = [jax.ShapeDtypeStruct((n, c[0]), c[1]) if c else jax.ShapeDtypeStruct((4 * H_A, n), F32)
                 for c in outs]
    return pl.pallas_call(
        kern, grid=(n // tm,),
        in_specs=[pl.BlockSpec((tm, d), lambda i: (i, 0)),
                  pl.BlockSpec((halo, d), lambda i: (jnp.maximum(i * hb - 1, 0), 0)),
                  pl.BlockSpec((halo, d), lambda i: (jnp.minimum((i + 1) * hb, n // halo - 1), 0)),
                  full(nw), full(mods), full(w), full(cw), full(alog_row), full(dt_row), tab, tab],
        out_specs=out_specs, out_shape=out_shape,
        compiler_params=_cparams("parallel"), name="even_proj",
    )(x, x, x, nw, mods, w, cw, alog_row, dt_row, *rope)


def _odd_proj_kernel(x_ref, nw_ref, mods_ref, w_ref, qn_ref, kvn_ref, wq_ref, wkv_ref, cos_ref, sin_ref,
                     q_ref, kv_ref, kr_ref, ckv_ref, krf_ref, *, rows, scale, n_nope):
    i = pl.program_id(0)
    d = x_ref.shape[1]
    h = _modulated(x_ref, nw_ref, mods_ref, rows.group(i), 0, d).astype(BF16)
    p = _dot(h, w_ref[...])
    cos, sin = cos_ref[...], sin_ref[...]
    cq = _rmsnorm_rows(p[:, :Q_LORA], qn_ref[...]).astype(BF16)
    for c0 in range(0, wq_ref.shape[1], PROJ_CHUNK):
        y = _dot(cq, wq_ref[:, c0:c0 + PROJ_CHUNK])
        if c0 >= n_nope:
            y = _rope(y, cos, sin)
        q_ref[:, c0:c0 + PROJ_CHUNK] = (y * scale).astype(BF16)
    ckv = _rmsnorm_rows(p[:, Q_LORA:Q_LORA + KV_LORA], kvn_ref[...])
    ckv_ref[...] = ckv
    ckv16 = ckv.astype(BF16)
    for c0 in range(0, wkv_ref.shape[1], PROJ_CHUNK):
        kv_ref[:, c0:c0 + PROJ_CHUNK] = _dot(ckv16, wkv_ref[:, c0:c0 + PROJ_CHUNK]).astype(BF16)
    kr = _rope(p[:, Q_LORA + KV_LORA:], cos, sin)
    krf_ref[...] = kr
    kr_ref[...] = kr.astype(BF16)


def _odd_proj(x, norm_w, mods, w_in, q_norm, kv_norm, wq, wkv, rope, rows, scale, n_nope):
    n, d = x.shape
    tm = rows.tm
    tab = pl.BlockSpec((tm, LANES), lambda i: (rows.pos_block(i), 0))
    full = lambda a: pl.BlockSpec(a.shape, lambda i: (0, 0))
    qn = q_norm.reshape(1, -1).astype(F32)
    kvn = kv_norm.reshape(1, -1).astype(F32)
    nw = norm_w.reshape(1, d).astype(F32)
    outs = [(wq.shape[1], BF16), (wkv.shape[1], BF16), (LANES, BF16), (KV_LORA, F32), (LANES, F32)]
    return pl.pallas_call(
        functools.partial(_odd_proj_kernel, rows=rows, scale=scale, n_nope=n_nope), grid=(n // tm,),
        in_specs=[pl.BlockSpec((tm, d), lambda i: (i, 0)), full(nw), full(mods), full(w_in), full(qn), full(kvn),
                  full(wq), full(wkv), tab, tab],
        out_specs=[pl.BlockSpec((tm, c), lambda i: (i, 0)) for c, _ in outs],
        out_shape=[jax.ShapeDtypeStruct((n, c), dt) for c, dt in outs],
        compiler_params=_cparams("parallel"), name="odd_proj",
    )(x, nw, mods, w_in, qn, kvn, wq, wkv, *rope)


def _split2(x):
    hi = x.astype(BF16)
    return hi, (x - hi.astype(F32)).astype(BF16)


def _mm(a, b, precise, dot=_dot):
    if not precise:
        return dot(a.astype(BF16), b.astype(BF16))
    ah, al = _split2(a)
    bh, bl = _split2(b)
    return dot(ah, bh) + (dot(ah, bl) + dot(al, bh))


SOLVE_BLOCK = 2 * SUBLANES
MOE_ROW_BLOCK = 256
DELTA_HEADS_PER_STEP = 4


def _tri_solve_many(ms, rs, revs):
    c, width = rs[0].shape
    blk = SOLVE_BLOCK
    nb, gpb, ng = c // blk, blk // SUBLANES, c // SUBLANES
    col = lax.broadcasted_iota(jnp.int32, (blk, c), 1)
    mgs = [[m[SUBLANES * g:SUBLANES * (g + 1), :] for g in range(ng)] for m in ms]
    xgs = [[r[SUBLANES * g:SUBLANES * (g + 1), :] for g in range(ng)] for r in rs]
    zero_blk = jnp.zeros((blk, width), BF16)
    fin_h = [[zero_blk] * nb for _ in ms]
    fin_l = [[zero_blk] * nb for _ in ms]
    for bi in range(nb):
        blocks = [nb - 1 - bi if rev else bi for rev in revs]
        if bi > 0:
            for s, (rev, b) in enumerate(zip(revs, blocks)):
                done = (col >= (b + 1) * blk) if rev else (col < b * blk)
                lh, ll = _split2(jnp.where(done, ms[s][b * blk:(b + 1) * blk, :], 0.0))
                xh = jnp.concatenate(fin_h[s], axis=0)
                xl = jnp.concatenate(fin_l[s], axis=0)
                upd = _dot(lh, xh) + (_dot(lh, xl) + _dot(ll, xh))
                for gg in range(gpb):
                    g = b * gpb + gg
                    xgs[s][g] = xgs[s][g] - upd[SUBLANES * gg:SUBLANES * (gg + 1), :]
        for t in range(blk - 1):
            for s, (rev, b) in enumerate(zip(revs, blocks)):
                j = b * blk + (blk - 1 - t if rev else t)
                xj = xgs[s][j // SUBLANES][j % SUBLANES:j % SUBLANES + 1, :]
                groups = range(b * gpb, (j - 1) // SUBLANES + 1) if rev else range((j + 1) // SUBLANES, (b + 1) * gpb)
                for g in groups:
                    xgs[s][g] = xgs[s][g] - mgs[s][g][:, j:j + 1] * xj
        if bi < nb - 1:
            for s, b in enumerate(blocks):
                fin_h[s][b], fin_l[s][b] = _split2(jnp.concatenate(xgs[s][b * gpb:(b + 1) * gpb], axis=0))
    return [jnp.concatenate(xg, axis=0) for xg in xgs]


def _delta_kernel(qf, kf, vf, gcf, grf, qb, kb, vb, gcb, grb, s0_ref, *rest, chunk, hpb, n_alias, precise):
    of_ref, ob_ref, so_ref, s_scr = rest[n_alias:]
    h0 = pl.program_id(1) * hpb
    i = pl.program_id(2)
    n_i = pl.num_programs(2)

    @pl.when(i == 0)
    def _():
        s_scr[...] = s0_ref[...]

    rg = qf.shape[0]
    nc = rg // chunk
    rowi = lax.broadcasted_iota(jnp.int32, (rg, rg), 0)
    coli = lax.broadcasted_iota(jnp.int32, (rg, rg), 1)
    same = (rowi // chunk) == (coli // chunk)
    lane = lax.broadcasted_iota(jnp.int32, (1, LANES), 1)
    r_c = lax.broadcasted_iota(jnp.int32, (chunk, chunk), 0)
    c_c = lax.broadcasted_iota(jnp.int32, (chunk, chunk), 1)

    prep = []
    for d, (q_ref, k_ref, v_ref, gc_ref, gr_ref) in enumerate(((qf, kf, vf, gcf, grf), (qb, kb, vb, gcb, grb))):
        rev = d == 1
        incl_big = same & ((rowi <= coli) if rev else (rowi >= coli))
        lm = jnp.where(incl_big, 1.0, 0.0).astype(BF16)
        gcols = gc_ref[...]
        grows = gr_ref[...]
        gh, gm, gl_ = _split3(gcols)
        cum_c = _dot(lm, gh) + _dot(lm, gm) + _dot(lm, gl_)
        th, tm_, tl = _split3(grows)
        cum_r = _dot_nt(th, lm) + _dot_nt(tm_, lm) + _dot_nt(tl, lm)

        def col(a, ln):
            return jnp.sum(jnp.where(lane == ln, a, 0.0), axis=1, keepdims=True)

        sub = lax.broadcasted_iota(jnp.int32, (cum_r.shape[0], 1), 0)
        incl = (r_c <= c_c) if rev else (r_c >= c_c)
        strict = (r_c < c_c) if rev else (r_c > c_c)
        for hh in range(hpb):
            h = h0 + hh
            hs = slice(hh * LANES, (hh + 1) * LANES)
            gcum = col(cum_c, 4 * h + d)
            beta = col(gcols, 4 * h + 2 + d)
            grow = jnp.sum(jnp.where(sub == 4 * h + d, cum_r, 0.0), axis=0, keepdims=True)
            q = q_ref[:, hs]
            k = k_ref[:, hs]
            v = v_ref[:, hs]
            kbeta = k * beta
            eg = jnp.exp(gcum)
            rhs = jnp.concatenate([v * beta, kbeta * eg], axis=1)
            ms, rs, a_in = [], [], []
            for c in range(nc):
                sl = slice(c * chunk, (c + 1) * chunk)
                e = jnp.exp(jnp.where(incl, gcum[sl] - grow[:, sl], 0.0))
                kk = _mm(kbeta[sl], k[sl], precise, _dot_nt)
                qk = _mm(q[sl], k[sl], precise, _dot_nt)
                a_in.append(jnp.where(incl, qk * e, 0.0))
                ms.append(jnp.where(strict, kk * e, 0.0))
                rs.append(rhs[sl])
            prep.append((rev, hh, ms, rs, a_in, q * eg, k, gcum))

    all_terms = []
    half = len(prep) // 2
    for wave in (prep[:half], prep[half:]) if half else (prep,):
        sols = _tri_solve_many([m for p in wave for m in p[2]], [r for p in wave for r in p[3]],
                               [p[0] for p in wave for _ in range(nc)])
        for n_u, (rev, hh, _, _, a_in, qg, k, gcum) in enumerate(wave):
            xs = sols[n_u * nc:(n_u + 1) * nc]
            terms = []
            for c in range(nc):
                sl = slice(c * chunk, (c + 1) * chunk)
                r_last = c * chunk if rev else (c + 1) * chunk - 1
                g_last = gcum[r_last:r_last + 1, :]
                k_dec = k[sl] * jnp.exp(g_last - gcum[sl])
                a_uw = _mm(a_in[c], xs[c], precise)
                kd_uw = _mm(k_dec, xs[c], precise, _dot_tn)
                lhs = jnp.concatenate([qg[sl] - a_uw[:, DV_A:], kd_uw[:, DV_A:]], axis=0)
                terms.append((lhs, a_uw[:, :DV_A], kd_uw[:, :DV_A], jnp.exp(g_last)))
            all_terms.append(terms)

    states = [s_scr[1 if p[0] else 0, p[1]] for p in prep]
    outs = [[None] * nc for _ in prep]
    for t in range(nc):
        for n_u, p in enumerate(prep):
            c = nc - 1 - t if p[0] else t
            lhs, o_0, q_0, decay = all_terms[n_u][c]
            prod = _mm(lhs, states[n_u], precise)
            outs[n_u][c] = o_0 + prod[:chunk]
            states[n_u] = states[n_u] * decay - prod[chunk:] + q_0
    for n_u, (rev, hh) in enumerate((p[0], p[1]) for p in prep):
        s_scr[1 if rev else 0, hh] = states[n_u]
        (ob_ref if rev else of_ref)[:, hh * LANES:(hh + 1) * LANES] = jnp.concatenate(outs[n_u], axis=0)

    @pl.when(i == n_i - 1)
    def _():
        so_ref[...] = s_scr[...]


def _delta_stage(qkvn, gcol, grow, s0, n_seq, seq_len, row0, rg, into=None, precise=False):
    ng = seq_len // rg
    b0 = row0 // rg

    def fwd_blk(b, i):
        return b0 + b * ng + i

    def bwd_blk(b, i):
        return b0 + b * ng + (ng - 1 - i)

    hpb = DELTA_HEADS_PER_STEP
    nhp = H_A // hpb
    hw = hpb * LANES

    def specs(blk):
        return [pl.BlockSpec((rg, hw), lambda b, h, i: (blk(b, i), h)),
                pl.BlockSpec((rg, hw), lambda b, h, i: (blk(b, i), nhp + h)),
                pl.BlockSpec((rg, hw), lambda b, h, i: (blk(b, i), 2 * nhp + h)),
                pl.BlockSpec((rg, LANES), lambda b, h, i: (blk(b, i), 0)),
                pl.BlockSpec((4 * H_A, rg), lambda b, h, i: (0, blk(b, i)))]

    st_spec = pl.BlockSpec((None, 2, hpb, DK_A, DV_A), lambda b, h, i: (b, 0, h, 0, 0))
    args = [qkvn, qkvn, qkvn, gcol, grow, qkvn, qkvn, qkvn, gcol, grow, s0]
    in_specs = specs(fwd_blk) + specs(bwd_blk) + [st_spec]
    aliases = {}
    if into is not None:
        for k_out, arr in enumerate(into):
            aliases[len(args)] = k_out
            args.append(arr)
            in_specs.append(pl.BlockSpec(memory_space=pl.ANY))
    n = qkvn.shape[0]
    return pl.pallas_call(
        functools.partial(_delta_kernel, chunk=CHUNK, hpb=hpb, n_alias=len(aliases), precise=precise),
        grid=(n_seq, nhp, ng),
        in_specs=in_specs,
        out_specs=[pl.BlockSpec((rg, hw), lambda b, h, i: (fwd_blk(b, i), h)),
                   pl.BlockSpec((rg, hw), lambda b, h, i: (bwd_blk(b, i), h)),
                   st_spec],
        out_shape=[jax.ShapeDtypeStruct((n, H_A * DV_A), F32),
                   jax.ShapeDtypeStruct((n, H_A * DV_A), F32),
                   jax.ShapeDtypeStruct(s0.shape, F32)],
        input_output_aliases=aliases,
        scratch_shapes=[pltpu.VMEM((2, hpb, DK_A, DV_A), F32)],
        compiler_params=_cparams("parallel", "parallel", "arbitrary"), name="gated_delta",
    )(*args)


def _flash_kernel(*refs, n_parts, has_cache, diff, tk, lam_init, aliased):
    it = iter(refs)
    q_refs = [next(it) for _ in range(n_parts)]
    k_refs = [next(it) for _ in range(n_parts)]
    v_ref = next(it)
    kc_refs = [next(it) for _ in range(n_parts)] if has_cache else []
    vc_ref = next(it) if has_cache else None
    lam_ref = next(it) if diff else None
    if aliased:
        next(it)
    o_ref = next(it)

    qs = [r[...] for r in q_refs]
    tq = qs[0].shape[0]
    if diff:
        lane = lax.broadcasted_iota(jnp.int32, (1, LANES), 1)
        q = qs[0]
        qs = [jnp.concatenate([jnp.where(lane < DQK_B, q, 0), jnp.where(lane >= DQK_B, q, 0)], axis=0)]
    nrow = qs[0].shape[0]
    q_cat = qs[0] if n_parts == 1 else jnp.concatenate(qs, axis=1)

    def block(carry, ks, vv):
        m, l, acc = carry
        s = _dot_nt(q_cat, ks[0] if n_parts == 1 else jnp.concatenate(ks, axis=1))
        m_new = jnp.maximum(m, jnp.max(s, axis=-1, keepdims=True))
        alpha = jnp.exp2(m - m_new)
        p_ = jnp.exp2(s - m_new)
        l = alpha * l + jnp.sum(p_, axis=-1, keepdims=True)
        acc = alpha * acc + _dot(p_.astype(BF16), vv)
        return m_new, l, acc

    carry = (jnp.full((nrow, 1), -1e30, F32), jnp.zeros((nrow, 1), F32), jnp.zeros((nrow, v_ref.shape[1]), F32))
    lk = k_refs[0].shape[0]

    def body(t, carry):
        off = pl.multiple_of(t * tk, tk)
        return block(carry, [r[pl.ds(off, tk), :] for r in k_refs], v_ref[pl.ds(off, tk), :])

    carry = lax.fori_loop(0, lk // tk, body, carry, unroll=True)
    if has_cache:
        carry = block(carry, [r[...] for r in kc_refs], vc_ref[...])
    _, l, acc = carry
    o = acc / l
    if diff:
        lp = lam_ref[...]
        lam = (jnp.exp(jnp.sum(lp[0:1] * lp[1:2], axis=-1, keepdims=True))
               - jnp.exp(jnp.sum(lp[2:3] * lp[3:4], axis=-1, keepdims=True)) + lam_init)
        o = o[:tq] - lam * o[tq:]
    o_ref[...] = o.astype(o_ref.dtype)


def _flash(q_parts, k_parts, v_part, cache, lam, *, n_seq, n_heads, lq, lk, q_row0, k_row0, tq, tk, diff,
           lam_init, out_cols, into=None):
    nqt = lq // tq
    qb0, kb0 = q_row0 // tq, k_row0 // lk
    args, in_specs = [], []
    for arr, cf in q_parts:
        args.append(arr)
        in_specs.append(pl.BlockSpec((tq, LANES), lambda b, h, t, cf=cf: (qb0 + b * nqt + t, cf(h))))
    for arr, cf in k_parts + [v_part]:
        args.append(arr)
        in_specs.append(pl.BlockSpec((lk, LANES), lambda b, h, t, cf=cf: (kb0 + b, cf(h))))
    if cache is not None:
        c_k, c_v, c_rows, c_blk = cache
        for arr, cf in c_k + [c_v]:
            args.append(arr)
            in_specs.append(pl.BlockSpec((c_rows, LANES), lambda b, h, t, cf=cf: (c_blk(b, h), cf(h))))
    if diff:
        args.append(lam)
        in_specs.append(pl.BlockSpec(lam.shape, lambda b, h, t: (0, 0)))
    aliases = {}
    if into is not None:
        aliases[len(args)] = 0
        args.append(into)
        in_specs.append(pl.BlockSpec(memory_space=pl.ANY))
    kern = functools.partial(_flash_kernel, n_parts=len(q_parts), has_cache=cache is not None, diff=diff, tk=tk,
                             lam_init=lam_init, aliased=into is not None)
    return pl.pallas_call(
        kern, grid=(n_seq, n_heads, nqt), in_specs=in_specs,
        out_specs=pl.BlockSpec((tq, LANES), lambda b, h, t: (qb0 + b * nqt + t, h)),
        out_shape=jax.ShapeDtypeStruct((q_parts[0][0].shape[0], out_cols), F32 if diff else BF16),
        input_output_aliases=aliases,
        compiler_params=_cparams("parallel", "parallel", "arbitrary"), name="flash_attention",
    )(*args)


def _mr_kernel(x_ref, a_ref, w_ref, mods_ref, o_ref, *, rows, gate_idx):
    i = pl.program_id(0)
    d = x_ref.shape[1]
    gate = _mod_chunk(mods_ref, rows.group(i), gate_idx, d)
    o_ref[...] = x_ref[...] + gate * _dot(a_ref[...].astype(BF16), w_ref[...])


def _matmul_residual(x, a, w, mods, rows, gate_idx):
    n, d = x.shape
    tm = rows.tm
    return pl.pallas_call(
        functools.partial(_mr_kernel, rows=rows, gate_idx=gate_idx), grid=(n // tm,),
        in_specs=[pl.BlockSpec((tm, d), lambda i: (i, 0)),
                  pl.BlockSpec((tm, a.shape[1]), lambda i: (i, 0)),
                  pl.BlockSpec(w.shape, lambda i: (0, 0)),
                  pl.BlockSpec(mods.shape, lambda i: (0, 0))],
        out_specs=pl.BlockSpec((tm, d), lambda i: (i, 0)),
        out_shape=jax.ShapeDtypeStruct((n, d), F32),
        compiler_params=_cparams("parallel"), name="matmul_residual",
    )(x, a, w, mods)


def _modulated(x_ref, nw_ref, mods_ref, g, idx, d):
    x = _rmsnorm_rows(x_ref[...], nw_ref[...])
    return x * (1.0 + _mod_chunk(mods_ref, g, idx + 1, d)) + _mod_chunk(mods_ref, g, idx, d)


def _ffn_kernel(x_ref, nw_ref, mods_ref, wg_ref, wu_ref, wd_ref, of_ref, ob_ref, z_ref, od_ref, on_ref, sn_ref,
                wo_ref, o_ref, h_scr, acc_scr, xn_scr, *, rows, mod0, mix_gate, lam_init):
    i, f = pl.program_id(0), pl.program_id(1)
    d = x_ref.shape[1]
    g = rows.group(i)

    @pl.when(f == 0)
    def _():
        oa = of_ref[...] + ob_ref[...]
        z = z_ref[...]
        od = od_ref[...]
        ya, yb = [], []
        for hh in range(H_A):
            sl = slice(hh * LANES, (hh + 1) * LANES)
            ya.append(_rmsnorm_rows(oa[:, sl], on_ref[...]) * _silu(z[:, sl]))
        for hh in range(H_B):
            sl = slice(hh * LANES, (hh + 1) * LANES)
            yb.append(_rmsnorm_rows(od[:, sl], sn_ref[...]) * (1.0 - lam_init))
        y = jnp.concatenate(ya + yb, axis=1).astype(BF16)
        x1 = x_ref[...] + _mod_chunk(mods_ref, g, mix_gate, d) * _dot(y, wo_ref[...])
        xn_scr[...] = x1
        v = _rmsnorm_rows(x1, nw_ref[...])
        h_scr[...] = (v * (1.0 + _mod_chunk(mods_ref, g, mod0 + 1, d)) + _mod_chunk(mods_ref, g, mod0, d)).astype(BF16)
        acc_scr[...] = jnp.zeros_like(acc_scr)

    h = h_scr[...]
    act = _silu(_dot(h, wg_ref[...])) * _dot(h, wu_ref[...])
    acc_scr[...] += _dot(act.astype(BF16), wd_ref[...])

    @pl.when(f == pl.num_programs(1) - 1)
    def _():
        o_ref[...] = xn_scr[...] + _mod_chunk(mods_ref, g, mod0 + 2, d) * acc_scr[...]


def _even_tail(x, o_f, o_b, z, o_d, onorm, subln, w_out, norm_w, mods, w_gu, w_down, rows, mix_gate, mod0, tf,
               lam_init):
    n, d = x.shape
    tm = rows.tm
    ff = w_down.shape[0]
    nf = ff // tf
    hw = H_A * DV_A
    row = lambda c: pl.BlockSpec((tm, c), lambda i, f: (i, 0))
    one = lambda c: pl.BlockSpec((1, c), lambda i, f: (0, 0))
    return pl.pallas_call(
        functools.partial(_ffn_kernel, rows=rows, mod0=mod0, mix_gate=mix_gate, lam_init=lam_init),
        grid=(n // tm, nf),
        in_specs=[row(d), one(d), pl.BlockSpec(mods.shape, lambda i, f: (0, 0)),
                  pl.BlockSpec((d, tf), lambda i, f: (0, f)),
                  pl.BlockSpec((d, tf), lambda i, f: (0, nf + f)),
                  pl.BlockSpec((tf, d), lambda i, f: (f, 0)),
                  row(hw), row(hw), row(hw), row(hw), one(LANES), one(LANES),
                  pl.BlockSpec(w_out.shape, lambda i, f: (0, 0))],
        out_specs=row(d),
        out_shape=jax.ShapeDtypeStruct((n, d), F32),
        scratch_shapes=[pltpu.VMEM((tm, d), BF16), pltpu.VMEM((tm, d), F32), pltpu.VMEM((tm, d), F32)],
        compiler_params=_cparams("parallel", "arbitrary"), name="even_tail",
    )(x, norm_w.reshape(1, d).astype(F32), mods, w_gu, w_gu, w_down, o_f, o_b, z, o_d,
      onorm.reshape(1, LANES).astype(F32), subln.reshape(1, LANES).astype(F32), w_out)


def _moe_kernel(x_ref, nw_ref, mods_ref, rw_ref, wgu_ref, wd_ref, *rest, rows, mod0, n_exp, sb, final):
    if final:
        fw_ref, o_ctx_ref, o_lat_ref, h_scr, g_scr, rk_scr, rkt_scr, acc_scr = rest
    else:
        o_ref, h_scr, g_scr, rk_scr, rkt_scr, acc_scr = rest
    i, e = pl.program_id(0), pl.program_id(1)
    tm, d = x_ref.shape
    g = rows.group(i)
    lane = lax.broadcasted_iota(jnp.int32, (1, LANES), 1)

    @pl.when(e == 0)
    def _():
        h = _modulated(x_ref, nw_ref, mods_ref, g, mod0, d)
        h_scr[...] = h.astype(BF16)
        acc_scr[...] = jnp.zeros_like(acc_scr)
        logits = jnp.where(lane < n_exp, _mm(h, rw_ref[...], True), -1e30)
        pe = jnp.exp(logits - jnp.max(logits, axis=-1, keepdims=True))
        probs = pe / jnp.sum(pe, axis=-1, keepdims=True)
        p1 = jnp.max(probs, axis=-1, keepdims=True)
        i1 = jnp.min(jnp.where(probs == p1, lane, LANES), axis=-1, keepdims=True)
        rest = jnp.where(lane == i1, -1.0, probs)
        p2 = jnp.max(rest, axis=-1, keepdims=True)
        i2 = jnp.min(jnp.where(rest == p2, lane, LANES), axis=-1, keepdims=True)
        den = p1 + p2
        gates = jnp.where(lane == i1, p1 / den, 0.0) + jnp.where(lane == i2, p2 / den, 0.0)
        g_scr[...] = gates
        t_r = lax.broadcasted_iota(jnp.int32, (tm, tm), 0)
        t_c = lax.broadcasted_iota(jnp.int32, (tm, tm), 1)
        earlier = jnp.where(t_r > t_c, 1.0, 0.0).astype(BF16)
        sel = gates > 0.0
        rank = jnp.where(sel, _dot(earlier, jnp.where(sel, 1.0, 0.0).astype(BF16)), -1.0)
        rk_scr[...] = rank
        rkt_scr[...] = jnp.transpose(rank)[:rkt_scr.shape[0], :]

    ff = wd_ref.shape[0]
    g_e = jnp.sum(jnp.where(lane == e, g_scr[...], 0.0), axis=-1, keepdims=True)
    rk_col = jnp.sum(jnp.where(lane == e, rk_scr[...], 0.0), axis=-1, keepdims=True)
    rk_row = rkt_scr[pl.ds(e, 1), :]
    n_rows = (jnp.max(rk_row) + 1.0).astype(jnp.int32)
    g_hi = g_e.astype(BF16).astype(F32)
    g2 = jnp.where(lane == 0, g_hi, jnp.where(lane == 1, g_e - g_hi, 0.0)).astype(BF16)
    def expert_rows(slot0, nr):
        base = slot0.astype(F32)
        r_sub = lax.broadcasted_iota(jnp.int32, (nr, 1), 0).astype(F32)
        r_lane = lax.broadcasted_iota(jnp.int32, (1, nr), 1).astype(F32)
        pick = jnp.where(rk_row == r_sub + base, 1.0, 0.0).astype(BF16)
        xg = _dot(pick, h_scr[...]).astype(BF16)
        gr = _dot(pick, g2)
        gate_r = gr[:, 0:1] + gr[:, 1:2]
        gu = _dot(xg, wgu_ref[...])
        act = _silu(gu[:, :ff]) * gu[:, ff:]
        y = _dot(act.astype(BF16), wd_ref[...])
        put = jnp.where(rk_col == r_lane + base, 1.0, 0.0).astype(BF16)
        acc_scr[...] += _dot(put, (y * gate_r).astype(BF16))

    half = sb // 2
    rem = n_rows % sb
    n_full = n_rows // sb + jnp.where(rem > half, 1, 0)

    def full_body(jb, carry):
        expert_rows(jb * sb, sb)
        return carry

    lax.fori_loop(0, n_full, full_body, 0)

    @pl.when((rem > 0) & (rem <= half))
    def _():
        expert_rows(n_full * sb, half)

    def result():
        return x_ref[...] + _mod_chunk(mods_ref, g, mod0 + 2, d) * acc_scr[...]

    last = e == n_exp - 1
    if final:
        @pl.when(last & (i < rows.ctx_tiles))
        def _():
            o_ctx_ref[...] = _rmsnorm_rows(result(), fw_ref[...])

        @pl.when(last & (i >= rows.ctx_tiles))
        def _():
            o_lat_ref[...] = _rmsnorm_rows(result(), fw_ref[...])
    else:
        @pl.when(last)
        def _():
            o_ref[...] = result()


def _moe(x, norm_w, mods, router_w, w_gu, w_down, rows, mod0, final_w=None):
    n, d = x.shape
    tm = rows.tm
    n_exp, _, ff2 = w_gu.shape
    rw = jnp.zeros((d, LANES), F32).at[:, :n_exp].set(router_w.astype(F32))
    final = final_w is not None
    args = [x, norm_w.reshape(1, d).astype(F32), mods, rw, w_gu, w_down]
    in_specs = [pl.BlockSpec((tm, d), lambda i, e: (i, 0)),
                pl.BlockSpec((1, d), lambda i, e: (0, 0)),
                pl.BlockSpec(mods.shape, lambda i, e: (0, 0)),
                pl.BlockSpec((d, LANES), lambda i, e: (0, 0)),
                pl.BlockSpec((None, d, ff2), lambda i, e: (e, 0, 0)),
                pl.BlockSpec((None, ff2 // 2, d), lambda i, e: (e, 0, 0))]
    if final:
        args.append(final_w.reshape(1, d).astype(F32))
        in_specs.append(pl.BlockSpec((1, d), lambda i, e: (0, 0)))
        ct = rows.ctx_tiles
        out_specs = [pl.BlockSpec((tm, d), lambda i, e: (jnp.minimum(i, ct - 1), 0), pipeline_mode=pl.Buffered(1)),
                     pl.BlockSpec((tm, d), lambda i, e: (jnp.maximum(i - ct, 0), 0), pipeline_mode=pl.Buffered(1))]
        out_shape = [jax.ShapeDtypeStruct((ct * tm, d), F32), jax.ShapeDtypeStruct((n - ct * tm, d), F32)]
    else:
        out_specs = pl.BlockSpec((tm, d), lambda i, e: (i, 0))
        out_shape = jax.ShapeDtypeStruct((n, d), F32)
    return pl.pallas_call(
        functools.partial(_moe_kernel, rows=rows, mod0=mod0, n_exp=n_exp, sb=MOE_ROW_BLOCK, final=final),
        grid=(n // tm, n_exp), in_specs=in_specs, out_specs=out_specs, out_shape=out_shape,
        scratch_shapes=[pltpu.VMEM((tm, d), BF16), pltpu.VMEM((tm, LANES), F32), pltpu.VMEM((tm, LANES), F32),
                        pltpu.VMEM((SUBLANES * ((n_exp + SUBLANES - 1) // SUBLANES), tm), F32),
                        pltpu.VMEM((tm, d), F32)],
        compiler_params=_cparams("arbitrary", "arbitrary"), name="moe",
    )(*args)


def _final_kernel(x_ref, w_ref, o_ref):
    o_ref[...] = _rmsnorm_rows(x_ref[...], w_ref[...])


def _final_norm(x, w, tm):
    n, d = x.shape
    return pl.pallas_call(
        _final_kernel, grid=(n // tm,),
        in_specs=[pl.BlockSpec((tm, d), lambda i: (i, 0)), pl.BlockSpec((1, d), lambda i: (0, 0))],
        out_specs=pl.BlockSpec((tm, d), lambda i: (i, 0)),
        out_shape=jax.ShapeDtypeStruct((n, d), F32),
        compiler_params=_cparams("parallel"), name="final_norm",
    )(x, w.reshape(1, d).astype(F32))


def kernel(x_prompt, x_sample, state_delta, cache_diff_k, cache_diff_v, cache_mla_ckv, cache_mla_krope, c, c_ctx,
           mod_w, mod_b, norm_mix, norm_ffn, final_norm, ev_w_in, ev_conv_w, ev_a_log, ev_dt_bias, ev_onorm,
           ev_lambda, ev_subln, ev_w_out, ffn_w_gu, ffn_w_down, od_w_in, od_q_norm, od_kv_norm, od_w_uq, od_w_ukv,
           od_w_out, moe_router, moe_w_gu, moe_w_down):
    bp, lp, d = x_prompt.shape
    bs, ls, _ = x_sample.shape
    past = cache_diff_k.shape[3]
    depth = mod_w.shape[0]
    n_p, n_s = bp * lp, bs * ls
    n = n_p + n_s
    assert n_p % ls == 0 and past == lp

    rows = _Rows(n_p, n_s, ls, _tile(math.gcd(n_p, ls), 512))
    rows_m = _Rows(n_p, n_s, ls, _tile(math.gcd(n_p, ls), 1024))
    rg = _tile(math.gcd(lp, ls), 256)
    cos_t, sin_t = _rope_tables(rows, ls)

    x = jnp.concatenate([x_prompt.reshape(n_p, d), x_sample.reshape(n_s, d)], axis=0)
    n_grp = 1 + bs
    cc = jnp.zeros((2 * SUBLANES * ((n_grp + 15) // 16), d), F32).at[0].set(c_ctx).at[1:n_grp].set(c)
    mods_all = _mods(cc, mod_w, mod_b)

    hq = H_A * DK_A
    hb2 = H_B * 2 * DQK_B
    sm0 = 4 * hq
    qb0 = sm0 + 4 * H_A
    q_scale = (DQK_B ** -0.5) * LOG2E
    mla_scale = ((NOPE_C + ROPE_C) ** -0.5) * LOG2E
    sd_new, dk_new, dv_new, ckv_new, kr_new = [], [], [], [], []

    for layer in range(depth):
        j = layer // 2
        mods = mods_all[layer]
        if layer % 2 == 0:
            lam_init = 0.8 - 0.6 * math.exp(-0.3 * layer)
            w = ev_w_in[j]
            sm = w[:, sm0:qb0].reshape(d, 4, H_A).transpose(0, 2, 1).reshape(d, 4 * H_A)
            w_in = jnp.concatenate([w[:, :sm0], sm, jnp.zeros((d, LANES - 4 * H_A), w.dtype), w[:, qb0:]],
                                   axis=1).astype(BF16)
            qkvn, z_gate, gcol, grow, att, kv_f = _even_proj(
                x, norm_mix[layer], mods, w_in, ev_conv_w[j], ev_a_log[j], ev_dt_bias[j], (cos_t, sin_t), rows,
                3 * hq, hq, hb2, q_scale, lp, ls, n_p)
            o_f, o_b, s_p = _delta_stage(qkvn, gcol, grow, jnp.zeros((bp, 2, H_A, DK_A, DV_A), F32), bp, lp, 0, rg,
                                         precise=True)
            o_f, o_b, _ = _delta_stage(qkvn, gcol, grow, state_delta[:, j].astype(F32), bs, ls, n_p, rg,
                                       into=(o_f, o_b))
            sd_new.append(s_p)
            lam_p = ev_lambda[j].astype(F32)
            nb = hb2 // LANES
            qp, kp, vp = [(att, lambda h: h)], [(att, lambda h: nb + h)], (att, lambda h: 2 * nb + h)
            o_d = _flash(qp, kp, vp, None, lam_p, n_seq=bp, n_heads=H_B, lq=lp, lk=lp, q_row0=0, k_row0=0,
                         tq=_tile(lp, 256), tk=_tile(lp, 512), diff=True, lam_init=lam_init, out_cols=hb2)
            ck = cache_diff_k[:, j].reshape(bs * H_B * past, 2 * DQK_B).astype(BF16)
            cv = cache_diff_v[:, j].reshape(bs * H_B * past, DV_B).astype(BF16)
            cache = ([(ck, lambda h: 0)], (cv, lambda h: 0), past, lambda b, h: b * H_B + h)
            o_d = _flash(qp, kp, vp, cache, lam_p, n_seq=bs, n_heads=H_B, lq=ls, lk=ls, q_row0=n_p, k_row0=n_p,
                         tq=_tile(ls, 512), tk=_tile(ls, 2048), diff=True, lam_init=lam_init, out_cols=hb2, into=o_d)
            dk_new.append(kv_f[:n_p, :hb2].reshape(bp, lp, H_B, 2 * DQK_B).transpose(0, 2, 1, 3))
            dv_new.append(kv_f[:n_p, hb2:].reshape(bp, lp, H_B, DV_B).transpose(0, 2, 1, 3))
            ff = ffn_w_down.shape[1]
            x = _even_tail(x, o_f, o_b, z_gate, o_d, ev_onorm[j], ev_subln[j], ev_w_out[j].astype(BF16),
                           norm_ffn[layer], mods, ffn_w_gu[j].astype(BF16), ffn_w_down[j].astype(BF16), rows, 2, 3,
                           ff // 2 if (ff // 2) % LANES == 0 else ff, lam_init)
        else:
            n_in = Q_LORA + KV_LORA + ROPE_C
            pad = (-n_in) % LANES
            w_in = jnp.concatenate([od_w_in[j], jnp.zeros((d, pad), od_w_in.dtype)], axis=1).astype(BF16)
            wq = od_w_uq[j].reshape(Q_LORA, H_C, NOPE_C + ROPE_C)
            wq_rope = jnp.concatenate([wq[:, :, NOPE_C:], jnp.zeros((Q_LORA, H_C, LANES - ROPE_C), wq.dtype)], axis=2)
            wq2 = jnp.concatenate([wq[:, :, :NOPE_C].reshape(Q_LORA, H_C * NOPE_C),
                                   wq_rope.reshape(Q_LORA, H_C * LANES)], axis=1).astype(BF16)
            q_all, kv_tok, kr_tok, ckv_n, kr_f = _odd_proj(
                x, norm_mix[layer], mods, w_in, od_q_norm[j], od_kv_norm[j], wq2, od_w_ukv[j].astype(BF16),
                (cos_t, sin_t), rows, mla_scale, H_C * NOPE_C)
            ckv_new.append(ckv_n[:n_p].reshape(bp, lp, KV_LORA))
            kr_new.append(kr_f[:n_p, :ROPE_C].reshape(bp, lp, ROPE_C))
            rows_k = _Rows(bs * past, 0, past, _tile(past, 256))
            kv_c = _norm_matmul(cache_mla_ckv[:, j].reshape(bs * past, KV_LORA), od_w_ukv[j].astype(BF16), rows_k,
                                tn=512, out_dtype=BF16)
            kr_c = jnp.concatenate([cache_mla_krope[:, j].reshape(bs * past, ROPE_C),
                                    jnp.zeros((bs * past, LANES - ROPE_C), F32)], axis=1).astype(BF16)
            qp = [(q_all, lambda h: h), (q_all, lambda h: H_C + h)]
            kp = [(kv_tok, lambda h: 2 * h), (kr_tok, lambda h: 0)]
            vp = (kv_tok, lambda h: 2 * h + 1)
            o = _flash(qp, kp, vp, None, None, n_seq=bp, n_heads=H_C, lq=lp, lk=lp, q_row0=0, k_row0=0,
                       tq=_tile(lp, 256), tk=_tile(lp, 512), diff=False, lam_init=0.0, out_cols=H_C * V_C)
            cache = ([(kv_c, lambda h: 2 * h), (kr_c, lambda h: 0)], (kv_c, lambda h: 2 * h + 1), past,
                     lambda b, h: b)
            o = _flash(qp, kp, vp, cache, None, n_seq=bs, n_heads=H_C, lq=ls, lk=ls, q_row0=n_p, k_row0=n_p,
                       tq=_tile(ls, 1024), tk=_tile(ls, 2048), diff=False, lam_init=0.0, out_cols=H_C * V_C, into=o)
            x = _matmul_residual(x, o, od_w_out[j].astype(BF16), mods, rows, 2)
            x = _moe(x, norm_ffn[layer], mods, moe_router[j], moe_w_gu[j].astype(BF16), moe_w_down[j].astype(BF16),
                     rows_m, 3, final_w=final_norm if layer == depth - 1 else None)

    if depth % 2 == 0:
        y_p, y_s = x
    else:
        y = _final_norm(x, final_norm, rows.tm)
        y_p, y_s = y[:n_p], y[n_p:]
    return (y_p.reshape(bp, lp, d), y_s.reshape(bs, ls, d), jnp.stack(sd_new, axis=1),
            jnp.stack(dk_new, axis=1), jnp.stack(dv_new, axis=1), jnp.stack(ckv_new, axis=1),
            jnp.stack(kr_new, axis=1))
```

```python
import functools
import math

import numpy as np
import jax
import jax.numpy as jnp
from jax import lax
from jax.experimental import pallas as pl
from jax.experimental.pallas import tpu as pltpu

F32 = jnp.float32
BF16 = jnp.bfloat16

EPS = 1e-6
LOG2E = 1.4426950408889634
GRID_W = 64
ROPE_BASE = 10000.0
H_A, DK_A, DV_A = 4, 128, 128
CONV_K = 3
CHUNK = 64
H_B, DQK_B, DV_B = 4, 64, 128
H_C, NOPE_C, ROPE_C, V_C = 8, 128, 64, 128
Q_LORA, KV_LORA = 384, 256
LANES = 128
SUBLANES = 8
VMEM_LIMIT_BYTES = 56 * 1024 * 1024


def _cparams(*sem):
    return pltpu.CompilerParams(dimension_semantics=sem, vmem_limit_bytes=VMEM_LIMIT_BYTES)


def _dot(a, b):
    return jnp.dot(a, b, preferred_element_type=F32)


def _dot_nt(a, b):
    return lax.dot_general(a, b, (((1,), (1,)), ((), ())), preferred_element_type=F32)


def _dot_tn(a, b):
    return lax.dot_general(a, b, (((0,), (0,)), ((), ())), preferred_element_type=F32)


def _split3(x):
    hi = x.astype(BF16)
    r = x - hi.astype(F32)
    mid = r.astype(BF16)
    lo = (r - mid.astype(F32)).astype(BF16)
    return hi, mid, lo


def _dot_f32(a, b):
    ah, am, al = _split3(a)
    bh, bm, bl = _split3(b)
    return (_dot(ah, bh) + (_dot(ah, bm) + _dot(am, bh))
            + (_dot(am, bm) + _dot(ah, bl) + _dot(al, bh)))


def _silu(x):
    return x * jax.nn.sigmoid(x)


def _tile(n, pref, mult=SUBLANES):
    t = min(n, pref)
    while t > mult and (n % t or t % mult):
        t -= mult
    assert n % t == 0, (n, pref)
    return t


class _Rows:
    def __init__(self, n_ctx, n_lat, lat_len, tm):
        assert n_ctx % tm == 0 and lat_len % tm == 0
        self.tm = tm
        self.n = n_ctx + n_lat
        self.ctx_tiles = n_ctx // tm
        self.seq_tiles = lat_len // tm

    def group(self, i):
        return jnp.where(i < self.ctx_tiles, 0, 1 + jnp.maximum(i - self.ctx_tiles, 0) // self.seq_tiles)

    def pos_block(self, i):
        return jnp.where(i < self.ctx_tiles, 0, 1 + jnp.maximum(i - self.ctx_tiles, 0) % self.seq_tiles)


def _mod_kernel(c_ref, w_ref, b_ref, o_ref):
    o_ref[...] = _dot_f32(_silu(c_ref[...]), w_ref[...]) + b_ref[...]


def _mods(cc, mod_w, mod_b):
    depth, d, n6 = mod_w.shape
    tn = _tile(n6, 512, LANES)
    return pl.pallas_call(
        _mod_kernel,
        grid=(depth, n6 // tn),
        in_specs=[pl.BlockSpec(cc.shape, lambda l, j: (0, 0)),
                  pl.BlockSpec((None, d, tn), lambda l, j: (l, 0, j)),
                  pl.BlockSpec((None, 1, tn), lambda l, j: (l, 0, j))],
        out_specs=pl.BlockSpec((None, cc.shape[0], tn), lambda l, j: (l, 0, j)),
        out_shape=jax.ShapeDtypeStruct((depth, cc.shape[0], n6), F32),
        compiler_params=_cparams("parallel", "parallel"),
        name="mods",
    )(cc, mod_w, mod_b.reshape(depth, 1, n6))


def _mod_chunk(mods_ref, g, idx, d):
    return mods_ref[pl.ds(g, 1), idx * d:(idx + 1) * d]


def _rmsnorm_rows(x, w):
    ms = jnp.mean(x * x, axis=-1, keepdims=True)
    return x * lax.rsqrt(ms + EPS) * w


def _rope(x, cos, sin_signed):
    lane = lax.broadcasted_iota(jnp.int32, (1, LANES), 1)
    low = (lane % 32) < 16
    outs = []
    for c in range(x.shape[1] // LANES):
        xc = x[:, c * LANES:(c + 1) * LANES]
        fwd = pltpu.roll(xc, LANES - 16, axis=1)
        bwd = pltpu.roll(xc, 16, axis=1)
        outs.append(xc * cos + jnp.where(low, fwd, bwd) * sin_signed)
    return outs[0] if len(outs) == 1 else jnp.concatenate(outs, axis=1)


def _rope_tables(rows, lat_len):
    pos = jnp.arange(lat_len)
    r = (pos // GRID_W).astype(F32)
    c = (pos % GRID_W).astype(F32)
    inv = ROPE_BASE ** (-jnp.arange(0, 32, 2, dtype=F32) / 32)
    ang_r = r[:, None] * inv[None, :]
    ang_c = c[:, None] * inv[None, :]
    ang = jnp.concatenate([ang_r, ang_r, ang_c, ang_c] * 2, axis=-1)
    sign = jnp.where((jnp.arange(LANES) % 32) < 16, -1.0, 1.0).astype(F32)
    cos = jnp.concatenate([jnp.ones((rows.tm, LANES), F32), jnp.cos(ang)], axis=0)
    sin = jnp.concatenate([jnp.zeros((rows.tm, LANES), F32), jnp.sin(ang) * sign[None, :]], axis=0)
    return cos, sin


def _nm_kernel(*refs, rows, d_mod, has_norm, mod_idx, has_rope, epi, emit_h):
    it = iter(refs)
    x_ref = next(it)
    w_ref = next(it)
    nw_ref = next(it) if has_norm else None
    mods_ref = next(it) if mod_idx is not None else None
    cos_ref, sin_ref = (next(it), next(it)) if has_rope else (None, None)
    o_ref = next(it)
    h_out_ref = next(it) if emit_h else None
    h_scr = next(it)
    i, j = pl.program_id(0), pl.program_id(1)

    @pl.when(j == 0)
    def _():
        x = x_ref[...].astype(F32)
        if has_norm:
            x = _rmsnorm_rows(x, nw_ref[...])
        if mod_idx is not None:
            g = rows.group(i)
            x = x * (1.0 + _mod_chunk(mods_ref, g, mod_idx[1], d_mod)) + _mod_chunk(mods_ref, g, mod_idx[0], d_mod)
        h_scr[...] = x.astype(BF16)
        if emit_h:
            h_out_ref[...] = x

    acc = _dot(h_scr[...], w_ref[...])
    kinds = sorted(set(epi))
    for kind in kinds:
        cond = None
        for jj, e in enumerate(epi):
            if e == kind:
                cond = (j == jj) if cond is None else (cond | (j == jj))

        def _store(kind=kind):
            y = acc
            if kind[0]:
                y = _rope(y, cos_ref[...], sin_ref[...])
            if kind[1] != 1.0:
                y = y * kind[1]
            o_ref[...] = y.astype(o_ref.dtype)

        if len(kinds) == 1:
            _store()
        else:
            pl.when(cond)(_store)


def _norm_matmul(x, w, rows, *, tn, norm_w=None, mods=None, mod_idx=None, rope=None, epi=None,
                 emit_h=False, out_dtype=F32):
    n, k = x.shape
    nout = w.shape[1]
    nj = nout // tn
    tm = rows.tm
    epi = tuple(epi) if epi is not None else ((False, 1.0),) * nj
    assert len(epi) == nj and n % tm == 0
    has_rope = any(e[0] for e in epi)
    args = [x, w]
    in_specs = [pl.BlockSpec((tm, k), lambda i, j: (i, 0)), pl.BlockSpec((k, tn), lambda i, j: (0, j))]
    if norm_w is not None:
        args.append(norm_w.reshape(1, k).astype(F32))
        in_specs.append(pl.BlockSpec((1, k), lambda i, j: (0, 0)))
    if mod_idx is not None:
        args.append(mods)
        in_specs.append(pl.BlockSpec(mods.shape, lambda i, j: (0, 0)))
    if has_rope:
        args += list(rope)
        in_specs += [pl.BlockSpec((tm, LANES), lambda i, j: (rows.pos_block(i), 0))] * 2
    out_shape = [jax.ShapeDtypeStruct((n, nout), out_dtype)]
    out_specs = [pl.BlockSpec((tm, tn), lambda i, j: (i, j))]
    if emit_h:
        out_shape.append(jax.ShapeDtypeStruct((n, k), F32))
        out_specs.append(pl.BlockSpec((tm, k), lambda i, j: (i, 0)))
    kern = functools.partial(_nm_kernel, rows=rows, d_mod=k, has_norm=norm_w is not None, mod_idx=mod_idx,
                             has_rope=has_rope, epi=epi, emit_h=emit_h)
    out = pl.pallas_call(
        kern, grid=(n // tm, nj), in_specs=in_specs, out_specs=out_specs, out_shape=out_shape,
        scratch_shapes=[pltpu.VMEM((tm, k), BF16)],
        compiler_params=_cparams("parallel", "arbitrary"), name="norm_matmul",
    )(*args)
    return out if emit_h else out[0]


PROJ_CHUNK = 512


def _pos_in_seq(r, length):
    return r & (length - 1) if length & (length - 1) == 0 else lax.rem(r, length)


def _even_proj_kernel(x_ref, xp_ref, xn_ref, nw_ref, mods_ref, w_ref, cw_ref, alog_ref, dt_ref, cos_ref, sin_ref,
                      qkv_ref, z_ref, gc_ref, gr_ref, att_ref, kv_ref,
                      *, rows, n_conv, n_z, hb2, q_scale, ctx_len, lat_len, n_ctx_rows):
    i = pl.program_id(0)
    tm, d = x_ref.shape
    g = rows.group(i)

    def modulated(ref):
        v = _rmsnorm_rows(ref[...], nw_ref[...])
        return (v * (1.0 + _mod_chunk(mods_ref, g, 1, d)) + _mod_chunk(mods_ref, g, 0, d)).astype(BF16)

    h = modulated(x_ref)
    halo = xp_ref.shape[0]
    p = jnp.concatenate([_dot(h, w_ref[:, c0:min(c0 + PROJ_CHUNK, n_conv)]) for c0 in range(0, n_conv, PROJ_CHUNK)],
                        axis=1)
    p_before = _dot(modulated(xp_ref), w_ref[:, :n_conv])[halo - 1:halo, :]
    p_after = _dot(modulated(xn_ref), w_ref[:, :n_conv])[0:1, :]
    row = lax.broadcasted_iota(jnp.int32, (tm, 1), 0)
    r_glob = row + i * tm
    in_ctx = i < rows.ctx_tiles
    pos = jnp.where(in_ctx, _pos_in_seq(r_glob, ctx_len), _pos_in_seq(r_glob - n_ctx_rows, lat_len))
    seq_last = jnp.where(in_ctx, ctx_len - 1, lat_len - 1)
    x_prev = jnp.where(pos == 0, 0.0, jnp.where(row == 0, p_before, pltpu.roll(p, 1, axis=0)))
    x_next = jnp.where(pos == seq_last, 0.0, jnp.where(row == tm - 1, p_after, pltpu.roll(p, tm - 1, axis=0)))
    cw = cw_ref[...]
    y = _silu(x_prev * cw[0:1, :] + p * cw[1:2, :] + x_next * cw[2:3, :])
    n_qk = 2 * H_A
    for hh in range(n_conv // LANES):
        yh = y[:, hh * LANES:(hh + 1) * LANES]
        if hh < n_qk:
            yh = yh * lax.rsqrt(jnp.sum(yh * yh, axis=-1, keepdims=True) + EPS)
            if hh < n_qk // 2:
                yh = yh * (DK_A ** -0.5)
        qkv_ref[:, hh * LANES:(hh + 1) * LANES] = yh
    z_ref[...] = _dot(h, w_ref[:, n_conv:n_conv + n_z])
    s = _dot(h, w_ref[:, n_conv + n_z:n_conv + n_z + LANES])
    lane = lax.broadcasted_iota(jnp.int32, (1, LANES), 1)
    zg = s + dt_ref[...]
    softplus = jnp.maximum(zg, 0.0) + jnp.log(1.0 + jnp.exp(-jnp.abs(zg)))
    gates = jnp.where((lane % 4) < 2, -jnp.exp(alog_ref[...]) * softplus, jax.nn.sigmoid(s))
    gc_ref[...] = gates
    gr_ref[...] = jnp.transpose(gates)[:gr_ref.shape[0], :]
    n_main = n_conv + n_z + LANES
    cos, sin = cos_ref[...], sin_ref[...]
    for part in range(3):
        for c0 in range(0, hb2, PROJ_CHUNK):
            c1 = min(c0 + PROJ_CHUNK, hb2)
            y = _dot(h, w_ref[:, n_main + part * hb2 + c0:n_main + part * hb2 + c1])
            if part < 2:
                y = _rope(y, cos, sin)
            if part == 0:
                y = y * q_scale
            else:
                kv_ref[:, (part - 1) * hb2 + c0:(part - 1) * hb2 + c1] = y
            att_ref[:, part * hb2 + c0:part * hb2 + c1] = y.astype(BF16)


def _even_proj(x, norm_w, mods, w, conv_w, a_log, dt_bias, rope, rows, n_conv, n_z, hb2, q_scale, ctx_len,
               lat_len, n_ctx_rows):
    n, d = x.shape
    tm = rows.tm
    halo = 2 * SUBLANES
    hb = tm // halo
    alog_row = jnp.zeros((1, LANES), F32).at[0, :4 * H_A].set(
        jnp.stack([a_log[0], a_log[1], a_log[0], a_log[1]], axis=-1).reshape(-1).astype(F32))
    dt_row = jnp.zeros((1, LANES), F32).at[0, :4 * H_A].set(
        jnp.stack([dt_bias[0], dt_bias[1], jnp.zeros_like(dt_bias[0]), jnp.zeros_like(dt_bias[0])],
                  axis=-1).reshape(-1).astype(F32))
    kern = functools.partial(_even_proj_kernel, rows=rows, n_conv=n_conv, n_z=n_z, hb2=hb2, q_scale=q_scale,
                             ctx_len=ctx_len, lat_len=lat_len, n_ctx_rows=n_ctx_rows)
    tab = pl.BlockSpec((tm, LANES), lambda i: (rows.pos_block(i), 0))
    full = lambda a: pl.BlockSpec(a.shape, lambda i: (0, 0))
    nw = norm_w.reshape(1, d).astype(F32)
    cw = conv_w.astype(F32)
    outs = [(n_conv, F32), (n_z, F32), (LANES, F32), None, (3 * hb2, BF16), (2 * hb2, F32)]
    out_specs = [pl.BlockSpec((tm, c[0]), lambda i: (i, 0)) if c else pl.BlockSpec((4 * H_A, tm), lambda i: (0, i))
                 for c in outs]
    out_shape = [jax.ShapeDtypeStruct((n, c[0]), c[1]) if c else jax.ShapeDtypeStruct((4 * H_A, n), F32)
                 for c in outs]
    return pl.pallas_call(
        kern, grid=(n // tm,),
        in_specs=[pl.BlockSpec((tm, d), lambda i: (i, 0)),
                  pl.BlockSpec((halo, d), lambda i: (jnp.maximum(i * hb - 1, 0), 0)),
                  pl.BlockSpec((halo, d), lambda i: (jnp.minimum((i + 1) * hb, n // halo - 1), 0)),
                  full(nw), full(mods), full(w), full(cw), full(alog_row), full(dt_row), tab, tab],
        out_specs=out_specs, out_shape=out_shape,
        compiler_params=_cparams("parallel"), name="even_proj",
    )(x, x, x, nw, mods, w, cw, alog_row, dt_row, *rope)


def _odd_proj_kernel(x_ref, nw_ref, mods_ref, w_ref, qn_ref, kvn_ref, wq_ref, wkv_ref, cos_ref, sin_ref,
                     q_ref, kv_ref, kr_ref, ckv_ref, krf_ref, *, rows, scale, n_nope):
    i = pl.program_id(0)
    d = x_ref.shape[1]
    h = _modulated(x_ref, nw_ref, mods_ref, rows.group(i), 0, d).astype(BF16)
    p = _dot(h, w_ref[...])
    cos, sin = cos_ref[...], sin_ref[...]
    cq = _rmsnorm_rows(p[:, :Q_LORA], qn_ref[...]).astype(BF16)
    for c0 in range(0, wq_ref.shape[1], PROJ_CHUNK):
        y = _dot(cq, wq_ref[:, c0:c0 + PROJ_CHUNK])
        if c0 >= n_nope:
            y = _rope(y, cos, sin)
        q_ref[:, c0:c0 + PROJ_CHUNK] = (y * scale).astype(BF16)
    ckv = _rmsnorm_rows(p[:, Q_LORA:Q_LORA + KV_LORA], kvn_ref[...])
    ckv_ref[...] = ckv
    ckv16 = ckv.astype(BF16)
    for c0 in range(0, wkv_ref.shape[1], PROJ_CHUNK):
        kv_ref[:, c0:c0 + PROJ_CHUNK] = _dot(ckv16, wkv_ref[:, c0:c0 + PROJ_CHUNK]).astype(BF16)
    kr = _rope(p[:, Q_LORA + KV_LORA:], cos, sin)
    krf_ref[...] = kr
    kr_ref[...] = kr.astype(BF16)


def _odd_proj(x, norm_w, mods, w_in, q_norm, kv_norm, wq, wkv, rope, rows, scale, n_nope):
    n, d = x.shape
    tm = rows.tm
    tab = pl.BlockSpec((tm, LANES), lambda i: (rows.pos_block(i), 0))
    full = lambda a: pl.BlockSpec(a.shape, lambda i: (0, 0))
    qn = q_norm.reshape(1, -1).astype(F32)
    kvn = kv_norm.reshape(1, -1).astype(F32)
    nw = norm_w.reshape(1, d).astype(F32)
    outs = [(wq.shape[1], BF16), (wkv.shape[1], BF16), (LANES, BF16), (KV_LORA, F32), (LANES, F32)]
    return pl.pallas_call(
        functools.partial(_odd_proj_kernel, rows=rows, scale=scale, n_nope=n_nope), grid=(n // tm,),
        in_specs=[pl.BlockSpec((tm, d), lambda i: (i, 0)), full(nw), full(mods), full(w_in), full(qn), full(kvn),
                  full(wq), full(wkv), tab, tab],
        out_specs=[pl.BlockSpec((tm, c), lambda i: (i, 0)) for c, _ in outs],
        out_shape=[jax.ShapeDtypeStruct((n, c), dt) for c, dt in outs],
        compiler_params=_cparams("parallel"), name="odd_proj",
    )(x, nw, mods, w_in, qn, kvn, wq, wkv, *rope)


def _split2(x):
    hi = x.astype(BF16)
    return hi, (x - hi.astype(F32)).astype(BF16)


def _mm(a, b, precise, dot=_dot):
    if not precise:
        return dot(a.astype(BF16), b.astype(BF16))
    ah, al = _split2(a)
    bh, bl = _split2(b)
    return dot(ah, bh) + (dot(ah, bl) + dot(al, bh))


SOLVE_BLOCK = SUBLANES
MOE_ROW_BLOCK = 256
DELTA_HEADS_PER_STEP = 4


def _tri_solve_many(ms, rs, revs):
    c, width = rs[0].shape
    blk = SOLVE_BLOCK
    nb, gpb, ng = c // blk, blk // SUBLANES, c // SUBLANES
    col = lax.broadcasted_iota(jnp.int32, (blk, c), 1)
    mgs = [[m[SUBLANES * g:SUBLANES * (g + 1), :] for g in range(ng)] for m in ms]
    xgs = [[r[SUBLANES * g:SUBLANES * (g + 1), :] for g in range(ng)] for r in rs]
    zero_blk = jnp.zeros((blk, width), BF16)
    fin_h = [[zero_blk] * nb for _ in ms]
    fin_l = [[zero_blk] * nb for _ in ms]
    for bi in range(nb):
        blocks = [nb - 1 - bi if rev else bi for rev in revs]
        if bi > 0:
            for s, (rev, b) in enumerate(zip(revs, blocks)):
                done = (col >= (b + 1) * blk) if rev else (col < b * blk)
                lh, ll = _split2(jnp.where(done, ms[s][b * blk:(b + 1) * blk, :], 0.0))
                xh = jnp.concatenate(fin_h[s], axis=0)
                xl = jnp.concatenate(fin_l[s], axis=0)
                upd = _dot(lh, xh) + (_dot(lh, xl) + _dot(ll, xh))
                for gg in range(gpb):
                    g = b * gpb + gg
                    xgs[s][g] = xgs[s][g] - upd[SUBLANES * gg:SUBLANES * (gg + 1), :]
        for t in range(blk - 1):
            for s, (rev, b) in enumerate(zip(revs, blocks)):
                j = b * blk + (blk - 1 - t if rev else t)
                xj = xgs[s][j // SUBLANES][j % SUBLANES:j % SUBLANES + 1, :]
                groups = range(b * gpb, (j - 1) // SUBLANES + 1) if rev else range((j + 1) // SUBLANES, (b + 1) * gpb)
                for g in groups:
                    xgs[s][g] = xgs[s][g] - mgs[s][g][:, j:j + 1] * xj
        if bi < nb - 1:
            for s, b in enumerate(blocks):
                fin_h[s][b], fin_l[s][b] = _split2(jnp.concatenate(xgs[s][b * gpb:(b + 1) * gpb], axis=0))
    return [jnp.concatenate(xg, axis=0) for xg in xgs]


def _delta_kernel(qf, kf, vf, gcf, grf, qb, kb, vb, gcb, grb, s0_ref, *rest, chunk, hpb, n_alias, precise):
    of_ref, ob_ref, so_ref, s_scr = rest[n_alias:]
    h0 = pl.program_id(1) * hpb
    i = pl.program_id(2)
    n_i = pl.num_programs(2)

    @pl.when(i == 0)
    def _():
        s_scr[...] = s0_ref[...]

    rg = qf.shape[0]
    nc = rg // chunk
    rowi = lax.broadcasted_iota(jnp.int32, (rg, rg), 0)
    coli = lax.broadcasted_iota(jnp.int32, (rg, rg), 1)
    same = (rowi // chunk) == (coli // chunk)
    lane = lax.broadcasted_iota(jnp.int32, (1, LANES), 1)
    r_c = lax.broadcasted_iota(jnp.int32, (chunk, chunk), 0)
    c_c = lax.broadcasted_iota(jnp.int32, (chunk, chunk), 1)

    prep = []
    for d, (q_ref, k_ref, v_ref, gc_ref, gr_ref) in enumerate(((qf, kf, vf, gcf, grf), (qb, kb, vb, gcb, grb))):
        rev = d == 1
        incl_big = same & ((rowi <= coli) if rev else (rowi >= coli))
        lm = jnp.where(incl_big, 1.0, 0.0).astype(BF16)
        gcols = gc_ref[...]
        grows = gr_ref[...]
        gh, gm, gl_ = _split3(gcols)
        cum_c = _dot(lm, gh) + _dot(lm, gm) + _dot(lm, gl_)
        th, tm_, tl = _split3(grows)
        cum_r = _dot_nt(th, lm) + _dot_nt(tm_, lm) + _dot_nt(tl, lm)

        def col(a, ln):
            return jnp.sum(jnp.where(lane == ln, a, 0.0), axis=1, keepdims=True)

        sub = lax.broadcasted_iota(jnp.int32, (cum_r.shape[0], 1), 0)
        incl = (r_c <= c_c) if rev else (r_c >= c_c)
        strict = (r_c < c_c) if rev else (r_c > c_c)
        for hh in range(hpb):
            h = h0 + hh
            hs = slice(hh * LANES, (hh + 1) * LANES)
            gcum = col(cum_c, 4 * h + d)
            beta = col(gcols, 4 * h + 2 + d)
            grow = jnp.sum(jnp.where(sub == 4 * h + d, cum_r, 0.0), axis=0, keepdims=True)
            q = q_ref[:, hs]
            k = k_ref[:, hs]
            v = v_ref[:, hs]
            kbeta = k * beta
            eg = jnp.exp(gcum)
            rhs = jnp.concatenate([v * beta, kbeta * eg], axis=1)
            ms, rs, a_in = [], [], []
            for c in range(nc):
                sl = slice(c * chunk, (c + 1) * chunk)
                e = jnp.exp(jnp.where(incl, gcum[sl] - grow[:, sl], 0.0))
                kk = _mm(kbeta[sl], k[sl], precise, _dot_nt)
                qk = _mm(q[sl], k[sl], precise, _dot_nt)
                a_in.append(jnp.where(incl, qk * e, 0.0))
                ms.append(jnp.where(strict, kk * e, 0.0))
                rs.append(rhs[sl])
            prep.append((rev, hh, ms, rs, a_in, q * eg, k, gcum))

    all_terms = []
    half = len(prep) // 2
    for wave in (prep[:half], prep[half:]) if half else (prep,):
        sols = _tri_solve_many([m for p in wave for m in p[2]], [r for p in wave for r in p[3]],
                               [p[0] for p in wave for _ in range(nc)])
        for n_u, (rev, hh, _, _, a_in, qg, k, gcum) in enumerate(wave):
            xs = sols[n_u * nc:(n_u + 1) * nc]
            terms = []
            for c in range(nc):
                sl = slice(c * chunk, (c + 1) * chunk)
                r_last = c * chunk if rev else (c + 1) * chunk - 1
                g_last = gcum[r_last:r_last + 1, :]
                k_dec = k[sl] * jnp.exp(g_last - gcum[sl])
                a_uw = _mm(a_in[c], xs[c], precise)
                kd_uw = _mm(k_dec, xs[c], precise, _dot_tn)
                lhs = jnp.concatenate([qg[sl] - a_uw[:, DV_A:], kd_uw[:, DV_A:]], axis=0)
                terms.append((lhs, a_uw[:, :DV_A], kd_uw[:, :DV_A], jnp.exp(g_last)))
            all_terms.append(terms)

    states = [s_scr[1 if p[0] else 0, p[1]] for p in prep]
    outs = [[None] * nc for _ in prep]
    for t in range(nc):
        for n_u, p in enumerate(prep):
            c = nc - 1 - t if p[0] else t
            lhs, o_0, q_0, decay = all_terms[n_u][c]
            prod = _mm(lhs, states[n_u], precise)
            outs[n_u][c] = o_0 + prod[:chunk]
            states[n_u] = states[n_u] * decay - prod[chunk:] + q_0
    for n_u, (rev, hh) in enumerate((p[0], p[1]) for p in prep):
        s_scr[1 if rev else 0, hh] = states[n_u]
        (ob_ref if rev else of_ref)[:, hh * LANES:(hh + 1) * LANES] = jnp.concatenate(outs[n_u], axis=0)

    @pl.when(i == n_i - 1)
    def _():
        so_ref[...] = s_scr[...]


def _delta_stage(qkvn, gcol, grow, s0, n_seq, seq_len, row0, rg, into=None, precise=False):
    ng = seq_len // rg
    b0 = row0 // rg

    def fwd_blk(b, i):
        return b0 + b * ng + i

    def bwd_blk(b, i):
        return b0 + b * ng + (ng - 1 - i)

    hpb = DELTA_HEADS_PER_STEP
    nhp = H_A // hpb
    hw = hpb * LANES

    def specs(blk):
        return [pl.BlockSpec((rg, hw), lambda b, h, i: (blk(b, i), h)),
                pl.BlockSpec((rg, hw), lambda b, h, i: (blk(b, i), nhp + h)),
                pl.BlockSpec((rg, hw), lambda b, h, i: (blk(b, i), 2 * nhp + h)),
                pl.BlockSpec((rg, LANES), lambda b, h, i: (blk(b, i), 0)),
                pl.BlockSpec((4 * H_A, rg), lambda b, h, i: (0, blk(b, i)))]

    st_spec = pl.BlockSpec((None, 2, hpb, DK_A, DV_A), lambda b, h, i: (b, 0, h, 0, 0))
    args = [qkvn, qkvn, qkvn, gcol, grow, qkvn, qkvn, qkvn, gcol, grow, s0]
    in_specs = specs(fwd_blk) + specs(bwd_blk) + [st_spec]
    aliases = {}
    if into is not None:
        for k_out, arr in enumerate(into):
            aliases[len(args)] = k_out
            args.append(arr)
            in_specs.append(pl.BlockSpec(memory_space=pl.ANY))
    n = qkvn.shape[0]
    return pl.pallas_call(
        functools.partial(_delta_kernel, chunk=CHUNK, hpb=hpb, n_alias=len(aliases), precise=precise),
        grid=(n_seq, nhp, ng),
        in_specs=in_specs,
        out_specs=[pl.BlockSpec((rg, hw), lambda b, h, i: (fwd_blk(b, i), h)),
                   pl.BlockSpec((rg, hw), lambda b, h, i: (bwd_blk(b, i), h)),
                   st_spec],
        out_shape=[jax.ShapeDtypeStruct((n, H_A * DV_A), F32),
                   jax.ShapeDtypeStruct((n, H_A * DV_A), F32),
                   jax.ShapeDtypeStruct(s0.shape, F32)],
        input_output_aliases=aliases,
        scratch_shapes=[pltpu.VMEM((2, hpb, DK_A, DV_A), F32)],
        compiler_params=_cparams("parallel", "parallel", "arbitrary"), name="gated_delta",
    )(*args)


def _flash_kernel(*refs, n_parts, has_cache, diff, tk, lam_init, aliased, head_lanes):
    it = iter(refs)
    q_refs = [next(it) for _ in range(n_parts)]
    k_refs = [next(it) for _ in range(n_parts)]
    v_ref = next(it)
    kc_refs = [next(it) for _ in range(n_parts)] if has_cache else []
    vc_ref = next(it) if has_cache else None
    lam_ref = next(it) if diff else None
    if aliased:
        next(it)
    o_ref = next(it)
    tq = q_refs[0].shape[0]
    lk = k_refs[0].shape[0]
    if diff:
        lp = lam_ref[...]
        lam = (jnp.exp(jnp.sum(lp[0:1] * lp[1:2], axis=-1, keepdims=True))
               - jnp.exp(jnp.sum(lp[2:3] * lp[3:4], axis=-1, keepdims=True)) + lam_init)
        lane = lax.broadcasted_iota(jnp.int32, (1, LANES), 1)

    def cols(ref, off, rows=None):
        return ref[:, off:off + LANES] if rows is None else ref[rows, off:off + LANES]

    for q_offs, k_offs, v_off, o_off in head_lanes:
        qs = [cols(r, off) for r, off in zip(q_refs, q_offs)]
        if diff:
            q = qs[0]
            qs = [jnp.concatenate([jnp.where(lane < DQK_B, q, 0), jnp.where(lane >= DQK_B, q, 0)], axis=0)]
        nrow = qs[0].shape[0]
        q_cat = qs[0] if n_parts == 1 else jnp.concatenate(qs, axis=1)

        def block(carry, ks, vv, q_cat=q_cat):
            m, l, acc = carry
            s = _dot_nt(q_cat, ks[0] if n_parts == 1 else jnp.concatenate(ks, axis=1))
            m_new = jnp.maximum(m, jnp.max(s, axis=-1, keepdims=True))
            alpha = jnp.exp2(m - m_new)
            p_ = jnp.exp2(s - m_new)
            l = alpha * l + jnp.sum(p_, axis=-1, keepdims=True)
            acc = alpha * acc + _dot(p_.astype(BF16), vv)
            return m_new, l, acc

        carry = (jnp.full((nrow, 1), -1e30, F32), jnp.zeros((nrow, 1), F32), jnp.zeros((nrow, LANES), F32))

        def body(t, carry, block=block, k_offs=k_offs, v_off=v_off):
            rows = pl.ds(pl.multiple_of(t * tk, tk), tk)
            return block(carry, [cols(r, off, rows) for r, off in zip(k_refs, k_offs)], cols(v_ref, v_off, rows))

        carry = lax.fori_loop(0, lk // tk, body, carry, unroll=True)
        if has_cache:
            carry = block(carry, [r[...] for r in kc_refs], vc_ref[...])
        _, l, acc = carry
        o = acc / l
        if diff:
            o = o[:tq] - lam * o[tq:]
        o_ref[:, o_off:o_off + LANES] = o.astype(o_ref.dtype)


def _flash(q_parts, k_parts, v_part, cache, lam, *, n_seq, n_heads, lq, lk, q_row0, k_row0, tq, tk, diff,
           lam_init, out_cols, into=None, all_heads=False):
    nqt = lq // tq
    qb0, kb0 = q_row0 // tq, k_row0 // lk
    args, in_specs = [], []
    if all_heads:
        assert cache is None
        for arr, _ in q_parts:
            args.append(arr)
            in_specs.append(pl.BlockSpec((tq, arr.shape[1]), lambda b, h, t: (qb0 + b * nqt + t, 0)))
        for arr, _ in k_parts + [v_part]:
            args.append(arr)
            in_specs.append(pl.BlockSpec((lk, arr.shape[1]), lambda b, h, t: (kb0 + b, 0)))
        head_lanes = tuple((tuple(cf(h) * LANES for _, cf in q_parts), tuple(cf(h) * LANES for _, cf in k_parts),
                            v_part[1](h) * LANES, h * LANES) for h in range(n_heads))
        grid = (n_seq, 1, nqt)
        out_spec = pl.BlockSpec((tq, out_cols), lambda b, h, t: (qb0 + b * nqt + t, 0))
    else:
        for arr, cf in q_parts:
            args.append(arr)
            in_specs.append(pl.BlockSpec((tq, LANES), lambda b, h, t, cf=cf: (qb0 + b * nqt + t, cf(h))))
        for arr, cf in k_parts + [v_part]:
            args.append(arr)
            in_specs.append(pl.BlockSpec((lk, LANES), lambda b, h, t, cf=cf: (kb0 + b, cf(h))))
        head_lanes = (((0,) * len(q_parts), (0,) * len(k_parts), 0, 0),)
        grid = (n_seq, n_heads, nqt)
        out_spec = pl.BlockSpec((tq, LANES), lambda b, h, t: (qb0 + b * nqt + t, h))
    if cache is not None:
        c_k, c_v, c_rows, c_blk = cache
        for arr, cf in c_k + [c_v]:
            args.append(arr)
            in_specs.append(pl.BlockSpec((c_rows, LANES), lambda b, h, t, cf=cf: (c_blk(b, h), cf(h))))
    if diff:
        args.append(lam)
        in_specs.append(pl.BlockSpec(lam.shape, lambda b, h, t: (0, 0)))
    aliases = {}
    if into is not None:
        aliases[len(args)] = 0
        args.append(into)
        in_specs.append(pl.BlockSpec(memory_space=pl.ANY))
    kern = functools.partial(_flash_kernel, n_parts=len(q_parts), has_cache=cache is not None, diff=diff, tk=tk,
                             lam_init=lam_init, aliased=into is not None, head_lanes=head_lanes)
    return pl.pallas_call(
        kern, grid=grid, in_specs=in_specs, out_specs=out_spec,
        out_shape=jax.ShapeDtypeStruct((q_parts[0][0].shape[0], out_cols), F32 if diff else BF16),
        input_output_aliases=aliases,
        compiler_params=_cparams("parallel", "parallel", "arbitrary"), name="flash_attention",
    )(*args)


def _mr_kernel(x_ref, a_ref, w_ref, mods_ref, o_ref, *, rows, gate_idx):
    i = pl.program_id(0)
    d = x_ref.shape[1]
    gate = _mod_chunk(mods_ref, rows.group(i), gate_idx, d)
    o_ref[...] = x_ref[...] + gate * _dot(a_ref[...].astype(BF16), w_ref[...])


def _matmul_residual(x, a, w, mods, rows, gate_idx):
    n, d = x.shape
    tm = rows.tm
    return pl.pallas_call(
        functools.partial(_mr_kernel, rows=rows, gate_idx=gate_idx), grid=(n // tm,),
        in_specs=[pl.BlockSpec((tm, d), lambda i: (i, 0)),
                  pl.BlockSpec((tm, a.shape[1]), lambda i: (i, 0)),
                  pl.BlockSpec(w.shape, lambda i: (0, 0)),
                  pl.BlockSpec(mods.shape, lambda i: (0, 0))],
        out_specs=pl.BlockSpec((tm, d), lambda i: (i, 0)),
        out_shape=jax.ShapeDtypeStruct((n, d), F32),
        compiler_params=_cparams("parallel"), name="matmul_residual",
    )(x, a, w, mods)


def _modulated(x_ref, nw_ref, mods_ref, g, idx, d):
    x = _rmsnorm_rows(x_ref[...], nw_ref[...])
    return x * (1.0 + _mod_chunk(mods_ref, g, idx + 1, d)) + _mod_chunk(mods_ref, g, idx, d)


def _ffn_kernel(x_ref, nw_ref, mods_ref, wg_ref, wu_ref, wd_ref, of_ref, ob_ref, z_ref, od_ref, on_ref, sn_ref,
                wo_ref, o_ref, h_scr, acc_scr, xn_scr, *, rows, mod0, mix_gate, lam_init):
    i, f = pl.program_id(0), pl.program_id(1)
    d = x_ref.shape[1]
    g = rows.group(i)

    @pl.when(f == 0)
    def _():
        oa = of_ref[...] + ob_ref[...]
        z = z_ref[...]
        od = od_ref[...]
        ya, yb = [], []
        for hh in range(H_A):
            sl = slice(hh * LANES, (hh + 1) * LANES)
            ya.append(_rmsnorm_rows(oa[:, sl], on_ref[...]) * _silu(z[:, sl]))
        for hh in range(H_B):
            sl = slice(hh * LANES, (hh + 1) * LANES)
            yb.append(_rmsnorm_rows(od[:, sl], sn_ref[...]) * (1.0 - lam_init))
        y = jnp.concatenate(ya + yb, axis=1).astype(BF16)
        x1 = x_ref[...] + _mod_chunk(mods_ref, g, mix_gate, d) * _dot(y, wo_ref[...])
        xn_scr[...] = x1
        v = _rmsnorm_rows(x1, nw_ref[...])
        h_scr[...] = (v * (1.0 + _mod_chunk(mods_ref, g, mod0 + 1, d)) + _mod_chunk(mods_ref, g, mod0, d)).astype(BF16)
        acc_scr[...] = jnp.zeros_like(acc_scr)

    h = h_scr[...]
    act = _silu(_dot(h, wg_ref[...])) * _dot(h, wu_ref[...])
    acc_scr[...] += _dot(act.astype(BF16), wd_ref[...])

    @pl.when(f == pl.num_programs(1) - 1)
    def _():
        o_ref[...] = xn_scr[...] + _mod_chunk(mods_ref, g, mod0 + 2, d) * acc_scr[...]


def _even_tail(x, o_f, o_b, z, o_d, onorm, subln, w_out, norm_w, mods, w_gu, w_down, rows, mix_gate, mod0, tf,
               lam_init):
    n, d = x.shape
    tm = rows.tm
    ff = w_down.shape[0]
    nf = ff // tf
    hw = H_A * DV_A
    row = lambda c: pl.BlockSpec((tm, c), lambda i, f: (i, 0))
    one = lambda c: pl.BlockSpec((1, c), lambda i, f: (0, 0))
    return pl.pallas_call(
        functools.partial(_ffn_kernel, rows=rows, mod0=mod0, mix_gate=mix_gate, lam_init=lam_init),
        grid=(n // tm, nf),
        in_specs=[row(d), one(d), pl.BlockSpec(mods.shape, lambda i, f: (0, 0)),
                  pl.BlockSpec((d, tf), lambda i, f: (0, f)),
                  pl.BlockSpec((d, tf), lambda i, f: (0, nf + f)),
                  pl.BlockSpec((tf, d), lambda i, f: (f, 0)),
                  row(hw), row(hw), row(hw), row(hw), one(LANES), one(LANES),
                  pl.BlockSpec(w_out.shape, lambda i, f: (0, 0))],
        out_specs=row(d),
        out_shape=jax.ShapeDtypeStruct((n, d), F32),
        scratch_shapes=[pltpu.VMEM((tm, d), BF16), pltpu.VMEM((tm, d), F32), pltpu.VMEM((tm, d), F32)],
        compiler_params=_cparams("parallel", "arbitrary"), name="even_tail",
    )(x, norm_w.reshape(1, d).astype(F32), mods, w_gu, w_gu, w_down, o_f, o_b, z, o_d,
      onorm.reshape(1, LANES).astype(F32), subln.reshape(1, LANES).astype(F32), w_out)


def _moe_kernel(x_ref, nw_ref, mods_ref, rw_ref, wgu_ref, wd_ref, *rest, rows, mod0, n_exp, sb, final):
    if final:
        fw_ref, o_ctx_ref, o_lat_ref, h_scr, g_scr, rk_scr, rkt_scr, acc_scr = rest
    else:
        o_ref, h_scr, g_scr, rk_scr, rkt_scr, acc_scr = rest
    i, e = pl.program_id(0), pl.program_id(1)
    tm, d = x_ref.shape
    g = rows.group(i)
    lane = lax.broadcasted_iota(jnp.int32, (1, LANES), 1)

    @pl.when(e == 0)
    def _():
        h = _modulated(x_ref, nw_ref, mods_ref, g, mod0, d)
        h_scr[...] = h.astype(BF16)
        acc_scr[...] = jnp.zeros_like(acc_scr)
        logits = jnp.where(lane < n_exp, _mm(h, rw_ref[...], True), -1e30)
        pe = jnp.exp(logits - jnp.max(logits, axis=-1, keepdims=True))
        probs = pe / jnp.sum(pe, axis=-1, keepdims=True)
        p1 = jnp.max(probs, axis=-1, keepdims=True)
        i1 = jnp.min(jnp.where(probs == p1, lane, LANES), axis=-1, keepdims=True)
        rest = jnp.where(lane == i1, -1.0, probs)
        p2 = jnp.max(rest, axis=-1, keepdims=True)
        i2 = jnp.min(jnp.where(rest == p2, lane, LANES), axis=-1, keepdims=True)
        den = p1 + p2
        gates = jnp.where(lane == i1, p1 / den, 0.0) + jnp.where(lane == i2, p2 / den, 0.0)
        g_scr[...] = gates
        t_r = lax.broadcasted_iota(jnp.int32, (tm, tm), 0)
        t_c = lax.broadcasted_iota(jnp.int32, (tm, tm), 1)
        earlier = jnp.where(t_r > t_c, 1.0, 0.0).astype(BF16)
        sel = gates > 0.0
        rank = jnp.where(sel, _dot(earlier, jnp.where(sel, 1.0, 0.0).astype(BF16)), -1.0)
        rk_scr[...] = rank
        rkt_scr[...] = jnp.transpose(rank)[:rkt_scr.shape[0], :]

    ff = wd_ref.shape[0]
    g_e = jnp.sum(jnp.where(lane == e, g_scr[...], 0.0), axis=-1, keepdims=True)
    rk_col = jnp.sum(jnp.where(lane == e, rk_scr[...], 0.0), axis=-1, keepdims=True)
    rk_row = rkt_scr[pl.ds(e, 1), :]
    n_rows = (jnp.max(rk_row) + 1.0).astype(jnp.int32)
    g_hi = g_e.astype(BF16).astype(F32)
    g2 = jnp.where(lane == 0, g_hi, jnp.where(lane == 1, g_e - g_hi, 0.0)).astype(BF16)
    def expert_rows(slot0, nr):
        base = slot0.astype(F32)
        r_sub = lax.broadcasted_iota(jnp.int32, (nr, 1), 0).astype(F32)
        r_lane = lax.broadcasted_iota(jnp.int32, (1, nr), 1).astype(F32)
        pick = jnp.where(rk_row == r_sub + base, 1.0, 0.0).astype(BF16)
        xg = _dot(pick, h_scr[...]).astype(BF16)
        gr = _dot(pick, g2)
        gate_r = gr[:, 0:1] + gr[:, 1:2]
        gu = _dot(xg, wgu_ref[...])
        act = _silu(gu[:, :ff]) * gu[:, ff:]
        y = _dot(act.astype(BF16), wd_ref[...])
        put = jnp.where(rk_col == r_lane + base, 1.0, 0.0).astype(BF16)
        acc_scr[...] += _dot(put, (y * gate_r).astype(BF16))

    half, quarter = sb // 2, sb // 4
    rem = n_rows % sb
    n_full = n_rows // sb + jnp.where(rem > half, 1, 0)

    def full_body(jb, carry):
        expert_rows(jb * sb, sb)
        return carry

    lax.fori_loop(0, n_full, full_body, 0)

    @pl.when((rem > quarter) & (rem <= half))
    def _():
        expert_rows(n_full * sb, half)

    @pl.when((rem > 0) & (rem <= quarter))
    def _():
        expert_rows(n_full * sb, quarter)

    def result():
        return x_ref[...] + _mod_chunk(mods_ref, g, mod0 + 2, d) * acc_scr[...]

    last = e == n_exp - 1
    if final:
        @pl.when(last & (i < rows.ctx_tiles))
        def _():
            o_ctx_ref[...] = _rmsnorm_rows(result(), fw_ref[...])

        @pl.when(last & (i >= rows.ctx_tiles))
        def _():
            o_lat_ref[...] = _rmsnorm_rows(result(), fw_ref[...])
    else:
        @pl.when(last)
        def _():
            o_ref[...] = result()


def _moe(x, norm_w, mods, router_w, w_gu, w_down, rows, mod0, final_w=None):
    n, d = x.shape
    tm = rows.tm
    n_exp, _, ff2 = w_gu.shape
    rw = jnp.zeros((d, LANES), F32).at[:, :n_exp].set(router_w.astype(F32))
    final = final_w is not None
    args = [x, norm_w.reshape(1, d).astype(F32), mods, rw, w_gu, w_down]
    in_specs = [pl.BlockSpec((tm, d), lambda i, e: (i, 0)),
                pl.BlockSpec((1, d), lambda i, e: (0, 0)),
                pl.BlockSpec(mods.shape, lambda i, e: (0, 0)),
                pl.BlockSpec((d, LANES), lambda i, e: (0, 0)),
                pl.BlockSpec((None, d, ff2), lambda i, e: (e, 0, 0)),
                pl.BlockSpec((None, ff2 // 2, d), lambda i, e: (e, 0, 0))]
    if final:
        args.append(final_w.reshape(1, d).astype(F32))
        in_specs.append(pl.BlockSpec((1, d), lambda i, e: (0, 0)))
        ct = rows.ctx_tiles
        out_specs = [pl.BlockSpec((tm, d), lambda i, e: (jnp.minimum(i, ct - 1), 0), pipeline_mode=pl.Buffered(1)),
                     pl.BlockSpec((tm, d), lambda i, e: (jnp.maximum(i - ct, 0), 0), pipeline_mode=pl.Buffered(1))]
        out_shape = [jax.ShapeDtypeStruct((ct * tm, d), F32), jax.ShapeDtypeStruct((n - ct * tm, d), F32)]
    else:
        out_specs = pl.BlockSpec((tm, d), lambda i, e: (i, 0))
        out_shape = jax.ShapeDtypeStruct((n, d), F32)
    return pl.pallas_call(
        functools.partial(_moe_kernel, rows=rows, mod0=mod0, n_exp=n_exp, sb=MOE_ROW_BLOCK, final=final),
        grid=(n // tm, n_exp), in_specs=in_specs, out_specs=out_specs, out_shape=out_shape,
        scratch_shapes=[pltpu.VMEM((tm, d), BF16), pltpu.VMEM((tm, LANES), F32), pltpu.VMEM((tm, LANES), F32),
                        pltpu.VMEM((SUBLANES * ((n_exp + SUBLANES - 1) // SUBLANES), tm), F32),
                        pltpu.VMEM((tm, d), F32)],
        compiler_params=_cparams("arbitrary", "arbitrary"), name="moe",
    )(*args)


def _final_kernel(x_ref, w_ref, o_ref):
    o_ref[...] = _rmsnorm_rows(x_ref[...], w_ref[...])


def _final_norm(x, w, tm):
    n, d = x.shape
    return pl.pallas_call(
        _final_kernel, grid=(n // tm,),
        in_specs=[pl.BlockSpec((tm, d), lambda i: (i, 0)), pl.BlockSpec((1, d), lambda i: (0, 0))],
        out_specs=pl.BlockSpec((tm, d), lambda i: (i, 0)),
        out_shape=jax.ShapeDtypeStruct((n, d), F32),
        compiler_params=_cparams("parallel"), name="final_norm",
    )(x, w.reshape(1, d).astype(F32))


def kernel(x_prompt, x_sample, state_delta, cache_diff_k, cache_diff_v, cache_mla_ckv, cache_mla_krope, c, c_ctx,
           mod_w, mod_b, norm_mix, norm_ffn, final_norm, ev_w_in, ev_conv_w, ev_a_log, ev_dt_bias, ev_onorm,
           ev_lambda, ev_subln, ev_w_out, ffn_w_gu, ffn_w_down, od_w_in, od_q_norm, od_kv_norm, od_w_uq, od_w_ukv,
           od_w_out, moe_router, moe_w_gu, moe_w_down):
    bp, lp, d = x_prompt.shape
    bs, ls, _ = x_sample.shape
    past = cache_diff_k.shape[3]
    depth = mod_w.shape[0]
    n_p, n_s = bp * lp, bs * ls
    n = n_p + n_s
    assert n_p % ls == 0 and past == lp

    rows = _Rows(n_p, n_s, ls, _tile(math.gcd(n_p, ls), 512))
    rows_m = _Rows(n_p, n_s, ls, _tile(math.gcd(n_p, ls), 1024))
    rg = _tile(math.gcd(lp, ls), 256)
    cos_t, sin_t = _rope_tables(rows, ls)

    x = jnp.concatenate([x_prompt.reshape(n_p, d), x_sample.reshape(n_s, d)], axis=0)
    n_grp = 1 + bs
    cc = jnp.zeros((2 * SUBLANES * ((n_grp + 15) // 16), d), F32).at[0].set(c_ctx).at[1:n_grp].set(c)
    mods_all = _mods(cc, mod_w, mod_b)

    hq = H_A * DK_A
    hb2 = H_B * 2 * DQK_B
    sm0 = 4 * hq
    qb0 = sm0 + 4 * H_A
    q_scale = (DQK_B ** -0.5) * LOG2E
    mla_scale = ((NOPE_C + ROPE_C) ** -0.5) * LOG2E
    sd_new, dk_new, dv_new, ckv_new, kr_new = [], [], [], [], []

    for layer in range(depth):
        j = layer // 2
        mods = mods_all[layer]
        if layer % 2 == 0:
            lam_init = 0.8 - 0.6 * math.exp(-0.3 * layer)
            w = ev_w_in[j]
            sm = w[:, sm0:qb0].reshape(d, 4, H_A).transpose(0, 2, 1).reshape(d, 4 * H_A)
            w_in = jnp.concatenate([w[:, :sm0], sm, jnp.zeros((d, LANES - 4 * H_A), w.dtype), w[:, qb0:]],
                                   axis=1).astype(BF16)
            qkvn, z_gate, gcol, grow, att, kv_f = _even_proj(
                x, norm_mix[layer], mods, w_in, ev_conv_w[j], ev_a_log[j], ev_dt_bias[j], (cos_t, sin_t), rows,
                3 * hq, hq, hb2, q_scale, lp, ls, n_p)
            o_f, o_b, s_p = _delta_stage(qkvn, gcol, grow, jnp.zeros((bp, 2, H_A, DK_A, DV_A), F32), bp, lp, 0, rg,
                                         precise=True)
            o_f, o_b, _ = _delta_stage(qkvn, gcol, grow, state_delta[:, j].astype(F32), bs, ls, n_p, rg,
                                       into=(o_f, o_b))
            sd_new.append(s_p)
            lam_p = ev_lambda[j].astype(F32)
            nb = hb2 // LANES
            qp, kp, vp = [(att, lambda h: h)], [(att, lambda h: nb + h)], (att, lambda h: 2 * nb + h)
            o_d = _flash(qp, kp, vp, None, lam_p, n_seq=bp, n_heads=H_B, lq=lp, lk=lp, q_row0=0, k_row0=0,
                         tq=_tile(lp, 256), tk=_tile(lp, 512), diff=True, lam_init=lam_init, out_cols=hb2,
                         all_heads=True)
            ck = cache_diff_k[:, j].reshape(bs * H_B * past, 2 * DQK_B).astype(BF16)
            cv = cache_diff_v[:, j].reshape(bs * H_B * past, DV_B).astype(BF16)
            cache = ([(ck, lambda h: 0)], (cv, lambda h: 0), past, lambda b, h: b * H_B + h)
            o_d = _flash(qp, kp, vp, cache, lam_p, n_seq=bs, n_heads=H_B, lq=ls, lk=ls, q_row0=n_p, k_row0=n_p,
                         tq=_tile(ls, 512), tk=_tile(ls, 2048), diff=True, lam_init=lam_init, out_cols=hb2, into=o_d)
            dk_new.append(kv_f[:n_p, :hb2].reshape(bp, lp, H_B, 2 * DQK_B).transpose(0, 2, 1, 3))
            dv_new.append(kv_f[:n_p, hb2:].reshape(bp, lp, H_B, DV_B).transpose(0, 2, 1, 3))
            ff = ffn_w_down.shape[1]
            x = _even_tail(x, o_f, o_b, z_gate, o_d, ev_onorm[j], ev_subln[j], ev_w_out[j].astype(BF16),
                           norm_ffn[layer], mods, ffn_w_gu[j].astype(BF16), ffn_w_down[j].astype(BF16), rows, 2, 3,
                           ff // 2 if (ff // 2) % LANES == 0 else ff, lam_init)
        else:
            n_in = Q_LORA + KV_LORA + ROPE_C
            pad = (-n_in) % LANES
            w_in = jnp.concatenate([od_w_in[j], jnp.zeros((d, pad), od_w_in.dtype)], axis=1).astype(BF16)
            wq = od_w_uq[j].reshape(Q_LORA, H_C, NOPE_C + ROPE_C)
            wq_rope = jnp.concatenate([wq[:, :, NOPE_C:], jnp.zeros((Q_LORA, H_C, LANES - ROPE_C), wq.dtype)], axis=2)
            wq2 = jnp.concatenate([wq[:, :, :NOPE_C].reshape(Q_LORA, H_C * NOPE_C),
                                   wq_rope.reshape(Q_LORA, H_C * LANES)], axis=1).astype(BF16)
            q_all, kv_tok, kr_tok, ckv_n, kr_f = _odd_proj(
                x, norm_mix[layer], mods, w_in, od_q_norm[j], od_kv_norm[j], wq2, od_w_ukv[j].astype(BF16),
                (cos_t, sin_t), rows, mla_scale, H_C * NOPE_C)
            ckv_new.append(ckv_n[:n_p].reshape(bp, lp, KV_LORA))
            kr_new.append(kr_f[:n_p, :ROPE_C].reshape(bp, lp, ROPE_C))
            rows_k = _Rows(bs * past, 0, past, _tile(past, 256))
            kv_c = _norm_matmul(cache_mla_ckv[:, j].reshape(bs * past, KV_LORA), od_w_ukv[j].astype(BF16), rows_k,
                                tn=512, out_dtype=BF16)
            kr_c = jnp.concatenate([cache_mla_krope[:, j].reshape(bs * past, ROPE_C),
                                    jnp.zeros((bs * past, LANES - ROPE_C), F32)], axis=1).astype(BF16)
            qp = [(q_all, lambda h: h), (q_all, lambda h: H_C + h)]
            kp = [(kv_tok, lambda h: 2 * h), (kr_tok, lambda h: 0)]
            vp = (kv_tok, lambda h: 2 * h + 1)
            o = _flash(qp, kp, vp, None, None, n_seq=bp, n_heads=H_C, lq=lp, lk=lp, q_row0=0, k_row0=0,
                       tq=_tile(lp, 256), tk=_tile(lp, 512), diff=False, lam_init=0.0, out_cols=H_C * V_C,
                       all_heads=True)
            cache = ([(kv_c, lambda h: 2 * h), (kr_c, lambda h: 0)], (kv_c, lambda h: 2 * h + 1), past,
                     lambda b, h: b)
            o = _flash(qp, kp, vp, cache, None, n_seq=bs, n_heads=H_C, lq=ls, lk=ls, q_row0=n_p, k_row0=n_p,
                       tq=_tile(ls, 1024), tk=_tile(ls, 2048), diff=False, lam_init=0.0, out_cols=H_C * V_C, into=o)
            x = _matmul_residual(x, o, od_w_out[j].astype(BF16), mods, rows, 2)
            x = _moe(x, norm_ffn[layer], mods, moe_router[j], moe_w_gu[j].astype(BF16), moe_w_down[j].astype(BF16),
                     rows_m, 3, final_w=final_norm if layer == depth - 1 else None)

    if depth % 2 == 0:
        y_p, y_s = x
    else:
        y = _final_norm(x, final_norm, rows.tm)
        y_p, y_s = y[:n_p], y[n_p:]
    return (y_p.reshape(bp, lp, d), y_s.reshape(bs, ls, d), jnp.stack(sd_new, axis=1),
            jnp.stack(dk_new, axis=1), jnp.stack(dv_new, axis=1), jnp.stack(ckv_new, axis=1),
            jnp.stack(kr_new, axis=1))
```

```python
import functools
import math

import numpy as np
import jax
import jax.numpy as jnp
from jax import lax
from jax.experimental import pallas as pl
from jax.experimental.pallas import tpu as pltpu

F32 = jnp.float32
BF16 = jnp.bfloat16

EPS = 1e-6
LOG2E = 1.4426950408889634
GRID_W = 64
ROPE_BASE = 10000.0
H_A, DK_A, DV_A = 4, 128, 128
CONV_K = 3
CHUNK = 64
H_B, DQK_B, DV_B = 4, 64, 128
H_C, NOPE_C, ROPE_C, V_C = 8, 128, 64, 128
Q_LORA, KV_LORA = 384, 256
LANES = 128
SUBLANES = 8
VMEM_LIMIT_BYTES = 56 * 1024 * 1024


def _cparams(*sem):
    return pltpu.CompilerParams(dimension_semantics=sem, vmem_limit_bytes=VMEM_LIMIT_BYTES)


def _dot(a, b):
    return jnp.dot(a, b, preferred_element_type=F32)


def _dot_nt(a, b):
    return lax.dot_general(a, b, (((1,), (1,)), ((), ())), preferred_element_type=F32)


def _dot_tn(a, b):
    return lax.dot_general(a, b, (((0,), (0,)), ((), ())), preferred_element_type=F32)


def _split3(x):
    hi = x.astype(BF16)
    r = x - hi.astype(F32)
    mid = r.astype(BF16)
    lo = (r - mid.astype(F32)).astype(BF16)
    return hi, mid, lo


def _dot_f32(a, b):
    ah, am, al = _split3(a)
    bh, bm, bl = _split3(b)
    return (_dot(ah, bh) + (_dot(ah, bm) + _dot(am, bh))
            + (_dot(am, bm) + _dot(ah, bl) + _dot(al, bh)))


def _silu(x):
    return x * jax.nn.sigmoid(x)


def _tile(n, pref, mult=SUBLANES):
    t = min(n, pref)
    while t > mult and (n % t or t % mult):
        t -= mult
    assert n % t == 0, (n, pref)
    return t


class _Rows:
    def __init__(self, n_ctx, n_lat, lat_len, tm):
        assert n_ctx % tm == 0 and lat_len % tm == 0
        self.tm = tm
        self.n = n_ctx + n_lat
        self.ctx_tiles = n_ctx // tm
        self.seq_tiles = lat_len // tm

    def group(self, i):
        return jnp.where(i < self.ctx_tiles, 0, 1 + jnp.maximum(i - self.ctx_tiles, 0) // self.seq_tiles)

    def pos_block(self, i):
        return jnp.where(i < self.ctx_tiles, 0, 1 + jnp.maximum(i - self.ctx_tiles, 0) % self.seq_tiles)


def _mod_kernel(c_ref, w_ref, b_ref, o_ref):
    o_ref[...] = _dot_f32(_silu(c_ref[...]), w_ref[...]) + b_ref[...]


def _mods(cc, mod_w, mod_b):
    depth, d, n6 = mod_w.shape
    tn = _tile(n6, 512, LANES)
    return pl.pallas_call(
        _mod_kernel,
        grid=(depth, n6 // tn),
        in_specs=[pl.BlockSpec(cc.shape, lambda l, j: (0, 0)),
                  pl.BlockSpec((None, d, tn), lambda l, j: (l, 0, j)),
                  pl.BlockSpec((None, 1, tn), lambda l, j: (l, 0, j))],
        out_specs=pl.BlockSpec((None, cc.shape[0], tn), lambda l, j: (l, 0, j)),
        out_shape=jax.ShapeDtypeStruct((depth, cc.shape[0], n6), F32),
        compiler_params=_cparams("parallel", "parallel"),
        name="mods",
    )(cc, mod_w, mod_b.reshape(depth, 1, n6))


def _mod_chunk(mods_ref, g, idx, d):
    return mods_ref[pl.ds(g, 1), idx * d:(idx + 1) * d]


def _rmsnorm_rows(x, w):
    ms = jnp.mean(x * x, axis=-1, keepdims=True)
    return x * lax.rsqrt(ms + EPS) * w


def _rope(x, cos, sin_signed):
    lane = lax.broadcasted_iota(jnp.int32, (1, LANES), 1)
    low = (lane % 32) < 16
    outs = []
    for c in range(x.shape[1] // LANES):
        xc = x[:, c * LANES:(c + 1) * LANES]
        fwd = pltpu.roll(xc, LANES - 16, axis=1)
        bwd = pltpu.roll(xc, 16, axis=1)
        outs.append(xc * cos + jnp.where(low, fwd, bwd) * sin_signed)
    return outs[0] if len(outs) == 1 else jnp.concatenate(outs, axis=1)


def _rope_tables(rows, lat_len):
    pos = jnp.arange(lat_len)
    r = (pos // GRID_W).astype(F32)
    c = (pos % GRID_W).astype(F32)
    inv = ROPE_BASE ** (-jnp.arange(0, 32, 2, dtype=F32) / 32)
    ang_r = r[:, None] * inv[None, :]
    ang_c = c[:, None] * inv[None, :]
    ang = jnp.concatenate([ang_r, ang_r, ang_c, ang_c] * 2, axis=-1)
    sign = jnp.where((jnp.arange(LANES) % 32) < 16, -1.0, 1.0).astype(F32)
    cos = jnp.concatenate([jnp.ones((rows.tm, LANES), F32), jnp.cos(ang)], axis=0)
    sin = jnp.concatenate([jnp.zeros((rows.tm, LANES), F32), jnp.sin(ang) * sign[None, :]], axis=0)
    return cos, sin


def _nm_kernel(*refs, rows, d_mod, has_norm, mod_idx, has_rope, epi, emit_h):
    it = iter(refs)
    x_ref = next(it)
    w_ref = next(it)
    nw_ref = next(it) if has_norm else None
    mods_ref = next(it) if mod_idx is not None else None
    cos_ref, sin_ref = (next(it), next(it)) if has_rope else (None, None)
    o_ref = next(it)
    h_out_ref = next(it) if emit_h else None
    h_scr = next(it)
    i, j = pl.program_id(0), pl.program_id(1)

    @pl.when(j == 0)
    def _():
        x = x_ref[...].astype(F32)
        if has_norm:
            x = _rmsnorm_rows(x, nw_ref[...])
        if mod_idx is not None:
            g = rows.group(i)
            x = x * (1.0 + _mod_chunk(mods_ref, g, mod_idx[1], d_mod)) + _mod_chunk(mods_ref, g, mod_idx[0], d_mod)
        h_scr[...] = x.astype(BF16)
        if emit_h:
            h_out_ref[...] = x

    acc = _dot(h_scr[...], w_ref[...])
    kinds = sorted(set(epi))
    for kind in kinds:
        cond = None
        for jj, e in enumerate(epi):
            if e == kind:
                cond = (j == jj) if cond is None else (cond | (j == jj))

        def _store(kind=kind):
            y = acc
            if kind[0]:
                y = _rope(y, cos_ref[...], sin_ref[...])
            if kind[1] != 1.0:
                y = y * kind[1]
            o_ref[...] = y.astype(o_ref.dtype)

        if len(kinds) == 1:
            _store()
        else:
            pl.when(cond)(_store)


def _norm_matmul(x, w, rows, *, tn, norm_w=None, mods=None, mod_idx=None, rope=None, epi=None,
                 emit_h=False, out_dtype=F32):
    n, k = x.shape
    nout = w.shape[1]
    nj = nout // tn
    tm = rows.tm
    epi = tuple(epi) if epi is not None else ((False, 1.0),) * nj
    assert len(epi) == nj and n % tm == 0
    has_rope = any(e[0] for e in epi)
    args = [x, w]
    in_specs = [pl.BlockSpec((tm, k), lambda i, j: (i, 0)), pl.BlockSpec((k, tn), lambda i, j: (0, j))]
    if norm_w is not None:
        args.append(norm_w.reshape(1, k).astype(F32))
        in_specs.append(pl.BlockSpec((1, k), lambda i, j: (0, 0)))
    if mod_idx is not None:
        args.append(mods)
        in_specs.append(pl.BlockSpec(mods.shape, lambda i, j: (0, 0)))
    if has_rope:
        args += list(rope)
        in_specs += [pl.BlockSpec((tm, LANES), lambda i, j: (rows.pos_block(i), 0))] * 2
    out_shape = [jax.ShapeDtypeStruct((n, nout), out_dtype)]
    out_specs = [pl.BlockSpec((tm, tn), lambda i, j: (i, j))]
    if emit_h:
        out_shape.append(jax.ShapeDtypeStruct((n, k), F32))
        out_specs.append(pl.BlockSpec((tm, k), lambda i, j: (i, 0)))
    kern = functools.partial(_nm_kernel, rows=rows, d_mod=k, has_norm=norm_w is not None, mod_idx=mod_idx,
                             has_rope=has_rope, epi=epi, emit_h=emit_h)
    out = pl.pallas_call(
        kern, grid=(n // tm, nj), in_specs=in_specs, out_specs=out_specs, out_shape=out_shape,
        scratch_shapes=[pltpu.VMEM((tm, k), BF16)],
        compiler_params=_cparams("parallel", "arbitrary"), name="norm_matmul",
    )(*args)
    return out if emit_h else out[0]


PROJ_CHUNK = 512


def _pos_in_seq(r, length):
    return r & (length - 1) if length & (length - 1) == 0 else lax.rem(r, length)


def _even_proj_kernel(x_ref, xp_ref, xn_ref, nw_ref, mods_ref, w_ref, cw_ref, alog_ref, dt_ref, cos_ref, sin_ref,
                      qkv_ref, z_ref, gc_ref, gr_ref, att_ref, kv_ref,
                      *, rows, n_conv, n_z, hb2, q_scale, ctx_len, lat_len, n_ctx_rows):
    i = pl.program_id(0)
    tm, d = x_ref.shape
    g = rows.group(i)

    def modulated(ref):
        v = _rmsnorm_rows(ref[...], nw_ref[...])
        return (v * (1.0 + _mod_chunk(mods_ref, g, 1, d)) + _mod_chunk(mods_ref, g, 0, d)).astype(BF16)

    h = modulated(x_ref)
    halo = xp_ref.shape[0]
    p = jnp.concatenate([_dot(h, w_ref[:, c0:min(c0 + PROJ_CHUNK, n_conv)]) for c0 in range(0, n_conv, PROJ_CHUNK)],
                        axis=1)
    p_before = _dot(modulated(xp_ref), w_ref[:, :n_conv])[halo - 1:halo, :]
    p_after = _dot(modulated(xn_ref), w_ref[:, :n_conv])[0:1, :]
    row = lax.broadcasted_iota(jnp.int32, (tm, 1), 0)
    r_glob = row + i * tm
    in_ctx = i < rows.ctx_tiles
    pos = jnp.where(in_ctx, _pos_in_seq(r_glob, ctx_len), _pos_in_seq(r_glob - n_ctx_rows, lat_len))
    seq_last = jnp.where(in_ctx, ctx_len - 1, lat_len - 1)
    x_prev = jnp.where(pos == 0, 0.0, jnp.where(row == 0, p_before, pltpu.roll(p, 1, axis=0)))
    x_next = jnp.where(pos == seq_last, 0.0, jnp.where(row == tm - 1, p_after, pltpu.roll(p, tm - 1, axis=0)))
    cw = cw_ref[...]
    y = _silu(x_prev * cw[0:1, :] + p * cw[1:2, :] + x_next * cw[2:3, :])
    n_qk = 2 * H_A
    for hh in range(n_conv // LANES):
        yh = y[:, hh * LANES:(hh + 1) * LANES]
        if hh < n_qk:
            yh = yh * lax.rsqrt(jnp.sum(yh * yh, axis=-1, keepdims=True) + EPS)
            if hh < n_qk // 2:
                yh = yh * (DK_A ** -0.5)
        qkv_ref[:, hh * LANES:(hh + 1) * LANES] = yh
    z_ref[...] = _dot(h, w_ref[:, n_conv:n_conv + n_z])
    s = _dot(h, w_ref[:, n_conv + n_z:n_conv + n_z + LANES])
    lane = lax.broadcasted_iota(jnp.int32, (1, LANES), 1)
    zg = s + dt_ref[...]
    softplus = jnp.maximum(zg, 0.0) + jnp.log(1.0 + jnp.exp(-jnp.abs(zg)))
    gates = jnp.where((lane % 4) < 2, -jnp.exp(alog_ref[...]) * softplus, jax.nn.sigmoid(s))
    gc_ref[...] = gates
    gr_ref[...] = jnp.transpose(gates)[:gr_ref.shape[0], :]
    n_main = n_conv + n_z + LANES
    cos, sin = cos_ref[...], sin_ref[...]
    for part in range(3):
        for c0 in range(0, hb2, PROJ_CHUNK):
            c1 = min(c0 + PROJ_CHUNK, hb2)
            y = _dot(h, w_ref[:, n_main + part * hb2 + c0:n_main + part * hb2 + c1])
            if part < 2:
                y = _rope(y, cos, sin)
            if part == 0:
                y = y * q_scale
            else:
                @pl.when(in_ctx)
                def _(y=y, lo=(part - 1) * hb2 + c0, hi=(part - 1) * hb2 + c1):
                    kv_ref[:, lo:hi] = y
            att_ref[:, part * hb2 + c0:part * hb2 + c1] = y.astype(BF16)


def _even_proj(x, norm_w, mods, w, conv_w, a_log, dt_bias, rope, rows, n_conv, n_z, hb2, q_scale, ctx_len,
               lat_len, n_ctx_rows):
    n, d = x.shape
    tm = rows.tm
    halo = 2 * SUBLANES
    hb = tm // halo
    alog_row = jnp.zeros((1, LANES), F32).at[0, :4 * H_A].set(
        jnp.stack([a_log[0], a_log[1], a_log[0], a_log[1]], axis=-1).reshape(-1).astype(F32))
    dt_row = jnp.zeros((1, LANES), F32).at[0, :4 * H_A].set(
        jnp.stack([dt_bias[0], dt_bias[1], jnp.zeros_like(dt_bias[0]), jnp.zeros_like(dt_bias[0])],
                  axis=-1).reshape(-1).astype(F32))
    kern = functools.partial(_even_proj_kernel, rows=rows, n_conv=n_conv, n_z=n_z, hb2=hb2, q_scale=q_scale,
                             ctx_len=ctx_len, lat_len=lat_len, n_ctx_rows=n_ctx_rows)
    tab = pl.BlockSpec((tm, LANES), lambda i: (rows.pos_block(i), 0))
    full = lambda a: pl.BlockSpec(a.shape, lambda i: (0, 0))
    nw = norm_w.reshape(1, d).astype(F32)
    cw = conv_w.astype(F32)
    outs = [(n_conv, F32), (n_z, F32), (LANES, F32), None, (3 * hb2, BF16), (2 * hb2, F32)]
    out_specs = [pl.BlockSpec((tm, c[0]), lambda i: (i, 0)) if c else pl.BlockSpec((4 * H_A, tm), lambda i: (0, i))
                 for c in outs]
    out_shape = [jax.ShapeDtypeStruct((n, c[0]), c[1]) if c else jax.ShapeDtypeStruct((4 * H_A, n), F32)
                 for c in outs]
    ct = rows.ctx_tiles
    out_specs[-1] = pl.BlockSpec((tm, 2 * hb2), lambda i: (jnp.minimum(i, ct - 1), 0), pipeline_mode=pl.Buffered(1))
    out_shape[-1] = jax.ShapeDtypeStruct((ct * tm, 2 * hb2), F32)
    return pl.pallas_call(
        kern, grid=(n // tm,),
        in_specs=[pl.BlockSpec((tm, d), lambda i: (i, 0)),
                  pl.BlockSpec((halo, d), lambda i: (jnp.maximum(i * hb - 1, 0), 0)),
                  pl.BlockSpec((halo, d), lambda i: (jnp.minimum((i + 1) * hb, n // halo - 1), 0)),
                  full(nw), full(mods), full(w), full(cw), full(alog_row), full(dt_row), tab, tab],
        out_specs=out_specs, out_shape=out_shape,
        compiler_params=_cparams("arbitrary"), name="even_proj",
    )(x, x, x, nw, mods, w, cw, alog_row, dt_row, *rope)


def _odd_proj_kernel(x_ref, nw_ref, mods_ref, w_ref, qn_ref, kvn_ref, wq_ref, wkv_ref, cos_ref, sin_ref,
                     q_ref, kv_ref, kr_ref, ckv_ref, krf_ref, *, rows, scale, n_nope):
    i = pl.program_id(0)
    d = x_ref.shape[1]
    h = _modulated(x_ref, nw_ref, mods_ref, rows.group(i), 0, d).astype(BF16)
    p = _dot(h, w_ref[...])
    cos, sin = cos_ref[...], sin_ref[...]
    cq = _rmsnorm_rows(p[:, :Q_LORA], qn_ref[...]).astype(BF16)
    for c0 in range(0, wq_ref.shape[1], PROJ_CHUNK):
        y = _dot(cq, wq_ref[:, c0:c0 + PROJ_CHUNK])
        if c0 >= n_nope:
            y = _rope(y, cos, sin)
        q_ref[:, c0:c0 + PROJ_CHUNK] = (y * scale).astype(BF16)
    ckv = _rmsnorm_rows(p[:, Q_LORA:Q_LORA + KV_LORA], kvn_ref[...])
    ckv_ref[...] = ckv
    ckv16 = ckv.astype(BF16)
    for c0 in range(0, wkv_ref.shape[1], PROJ_CHUNK):
        kv_ref[:, c0:c0 + PROJ_CHUNK] = _dot(ckv16, wkv_ref[:, c0:c0 + PROJ_CHUNK]).astype(BF16)
    kr = _rope(p[:, Q_LORA + KV_LORA:], cos, sin)
    krf_ref[...] = kr
    kr_ref[...] = kr.astype(BF16)


def _odd_proj(x, norm_w, mods, w_in, q_norm, kv_norm, wq, wkv, rope, rows, scale, n_nope):
    n, d = x.shape
    tm = rows.tm
    tab = pl.BlockSpec((tm, LANES), lambda i: (rows.pos_block(i), 0))
    full = lambda a: pl.BlockSpec(a.shape, lambda i: (0, 0))
    qn = q_norm.reshape(1, -1).astype(F32)
    kvn = kv_norm.reshape(1, -1).astype(F32)
    nw = norm_w.reshape(1, d).astype(F32)
    outs = [(wq.shape[1], BF16), (wkv.shape[1], BF16), (LANES, BF16), (KV_LORA, F32), (LANES, F32)]
    return pl.pallas_call(
        functools.partial(_odd_proj_kernel, rows=rows, scale=scale, n_nope=n_nope), grid=(n // tm,),
        in_specs=[pl.BlockSpec((tm, d), lambda i: (i, 0)), full(nw), full(mods), full(w_in), full(qn), full(kvn),
                  full(wq), full(wkv), tab, tab],
        out_specs=[pl.BlockSpec((tm, c), lambda i: (i, 0)) for c, _ in outs],
        out_shape=[jax.ShapeDtypeStruct((n, c), dt) for c, dt in outs],
        compiler_params=_cparams("parallel"), name="odd_proj",
    )(x, nw, mods, w_in, qn, kvn, wq, wkv, *rope)


def _split2(x):
    hi = x.astype(BF16)
    return hi, (x - hi.astype(F32)).astype(BF16)


def _mm(a, b, precise, dot=_dot):
    if not precise:
        return dot(a.astype(BF16), b.astype(BF16))
    ah, al = _split2(a)
    bh, bl = _split2(b)
    return dot(ah, bh) + (dot(ah, bl) + dot(al, bh))


SOLVE_BLOCK = SUBLANES
MOE_ROW_BLOCK = 256
DELTA_HEADS_PER_STEP = 4


def _tri_solve_many(ms, rs, revs):
    c, width = rs[0].shape
    blk = SOLVE_BLOCK
    nb, gpb, ng = c // blk, blk // SUBLANES, c // SUBLANES
    col = lax.broadcasted_iota(jnp.int32, (blk, c), 1)
    mgs = [[m[SUBLANES * g:SUBLANES * (g + 1), :] for g in range(ng)] for m in ms]
    xgs = [[r[SUBLANES * g:SUBLANES * (g + 1), :] for g in range(ng)] for r in rs]
    zero_blk = jnp.zeros((blk, width), BF16)
    fin_h = [[zero_blk] * nb for _ in ms]
    fin_l = [[zero_blk] * nb for _ in ms]
    for bi in range(nb):
        blocks = [nb - 1 - bi if rev else bi for rev in revs]
        if bi > 0:
            for s, (rev, b) in enumerate(zip(revs, blocks)):
                done = (col >= (b + 1) * blk) if rev else (col < b * blk)
                lh, ll = _split2(jnp.where(done, ms[s][b * blk:(b + 1) * blk, :], 0.0))
                xh = jnp.concatenate(fin_h[s], axis=0)
                xl = jnp.concatenate(fin_l[s], axis=0)
                upd = _dot(lh, xh) + (_dot(lh, xl) + _dot(ll, xh))
                for gg in range(gpb):
                    g = b * gpb + gg
                    xgs[s][g] = xgs[s][g] - upd[SUBLANES * gg:SUBLANES * (gg + 1), :]
        for t in range(blk - 1):
            for s, (rev, b) in enumerate(zip(revs, blocks)):
                j = b * blk + (blk - 1 - t if rev else t)
                xj = xgs[s][j // SUBLANES][j % SUBLANES:j % SUBLANES + 1, :]
                groups = range(b * gpb, (j - 1) // SUBLANES + 1) if rev else range((j + 1) // SUBLANES, (b + 1) * gpb)
                for g in groups:
                    xgs[s][g] = xgs[s][g] - mgs[s][g][:, j:j + 1] * xj
        if bi < nb - 1:
            for s, b in enumerate(blocks):
                fin_h[s][b], fin_l[s][b] = _split2(jnp.concatenate(xgs[s][b * gpb:(b + 1) * gpb], axis=0))
    return [jnp.concatenate(xg, axis=0) for xg in xgs]


def _delta_kernel(qf, kf, vf, gcf, grf, qb, kb, vb, gcb, grb, s0_ref, *rest, chunk, hpb, n_alias, precise):
    of_ref, ob_ref, so_ref, s_scr = rest[n_alias:]
    h0 = pl.program_id(1) * hpb
    i = pl.program_id(2)
    n_i = pl.num_programs(2)

    @pl.when(i == 0)
    def _():
        s_scr[...] = s0_ref[...]

    rg = qf.shape[0]
    nc = rg // chunk
    rowi = lax.broadcasted_iota(jnp.int32, (rg, rg), 0)
    coli = lax.broadcasted_iota(jnp.int32, (rg, rg), 1)
    same = (rowi // chunk) == (coli // chunk)
    lane = lax.broadcasted_iota(jnp.int32, (1, LANES), 1)
    r_c = lax.broadcasted_iota(jnp.int32, (chunk, chunk), 0)
    c_c = lax.broadcasted_iota(jnp.int32, (chunk, chunk), 1)

    prep = []
    for d, (q_ref, k_ref, v_ref, gc_ref, gr_ref) in enumerate(((qf, kf, vf, gcf, grf), (qb, kb, vb, gcb, grb))):
        rev = d == 1
        incl_big = same & ((rowi <= coli) if rev else (rowi >= coli))
        lm = jnp.where(incl_big, 1.0, 0.0).astype(BF16)
        gcols = gc_ref[...]
        grows = gr_ref[...]
        gh, gm, gl_ = _split3(gcols)
        cum_c = _dot(lm, gh) + _dot(lm, gm) + _dot(lm, gl_)
        th, tm_, tl = _split3(grows)
        cum_r = _dot_nt(th, lm) + _dot_nt(tm_, lm) + _dot_nt(tl, lm)

        def col(a, ln):
            return jnp.sum(jnp.where(lane == ln, a, 0.0), axis=1, keepdims=True)

        sub = lax.broadcasted_iota(jnp.int32, (cum_r.shape[0], 1), 0)
        incl = (r_c <= c_c) if rev else (r_c >= c_c)
        strict = (r_c < c_c) if rev else (r_c > c_c)
        for hh in range(hpb):
            h = h0 + hh
            hs = slice(hh * LANES, (hh + 1) * LANES)
            gcum = col(cum_c, 4 * h + d)
            beta = col(gcols, 4 * h + 2 + d)
            grow = jnp.sum(jnp.where(sub == 4 * h + d, cum_r, 0.0), axis=0, keepdims=True)
            q = q_ref[:, hs]
            k = k_ref[:, hs]
            v = v_ref[:, hs]
            kbeta = k * beta
            eg = jnp.exp(gcum)
            rhs = jnp.concatenate([v * beta, kbeta * eg], axis=1)
            ms, rs, a_in = [], [], []
            for c in range(nc):
                sl = slice(c * chunk, (c + 1) * chunk)
                e = jnp.exp(jnp.where(incl, gcum[sl] - grow[:, sl], 0.0))
                kk = _mm(kbeta[sl], k[sl], precise, _dot_nt)
                qk = _mm(q[sl], k[sl], precise, _dot_nt)
                a_in.append(jnp.where(incl, qk * e, 0.0))
                ms.append(jnp.where(strict, kk * e, 0.0))
                rs.append(rhs[sl])
            prep.append((rev, hh, ms, rs, a_in, q * eg, k, gcum))

    all_terms = []
    half = len(prep) // 2
    for wave in (prep[:half], prep[half:]) if half else (prep,):
        sols = _tri_solve_many([m for p in wave for m in p[2]], [r for p in wave for r in p[3]],
                               [p[0] for p in wave for _ in range(nc)])
        for n_u, (rev, hh, _, _, a_in, qg, k, gcum) in enumerate(wave):
            xs = sols[n_u * nc:(n_u + 1) * nc]
            terms = []
            for c in range(nc):
                sl = slice(c * chunk, (c + 1) * chunk)
                r_last = c * chunk if rev else (c + 1) * chunk - 1
                g_last = gcum[r_last:r_last + 1, :]
                k_dec = k[sl] * jnp.exp(g_last - gcum[sl])
                a_uw = _mm(a_in[c], xs[c], precise)
                kd_uw = _mm(k_dec, xs[c], precise, _dot_tn)
                lhs = jnp.concatenate([qg[sl] - a_uw[:, DV_A:], kd_uw[:, DV_A:]], axis=0)
                terms.append((lhs, a_uw[:, :DV_A], kd_uw[:, :DV_A], jnp.exp(g_last)))
            all_terms.append(terms)

    states = [s_scr[1 if p[0] else 0, p[1]] for p in prep]
    outs = [[None] * nc for _ in prep]
    for t in range(nc):
        for n_u, p in enumerate(prep):
            c = nc - 1 - t if p[0] else t
            lhs, o_0, q_0, decay = all_terms[n_u][c]
            prod = _mm(lhs, states[n_u], precise)
            outs[n_u][c] = o_0 + prod[:chunk]
            states[n_u] = states[n_u] * decay - prod[chunk:] + q_0
    for n_u, (rev, hh) in enumerate((p[0], p[1]) for p in prep):
        s_scr[1 if rev else 0, hh] = states[n_u]
        (ob_ref if rev else of_ref)[:, hh * LANES:(hh + 1) * LANES] = jnp.concatenate(outs[n_u], axis=0)

    @pl.when(i == n_i - 1)
    def _():
        so_ref[...] = s_scr[...]


def _delta_stage(qkvn, gcol, grow, s0, n_seq, seq_len, row0, rg, into=None, precise=False):
    ng = seq_len // rg
    b0 = row0 // rg

    def fwd_blk(b, i):
        return b0 + b * ng + i

    def bwd_blk(b, i):
        return b0 + b * ng + (ng - 1 - i)

    hpb = DELTA_HEADS_PER_STEP
    nhp = H_A // hpb
    hw = hpb * LANES

    def specs(blk):
        return [pl.BlockSpec((rg, hw), lambda b, h, i: (blk(b, i), h)),
                pl.BlockSpec((rg, hw), lambda b, h, i: (blk(b, i), nhp + h)),
                pl.BlockSpec((rg, hw), lambda b, h, i: (blk(b, i), 2 * nhp + h)),
                pl.BlockSpec((rg, LANES), lambda b, h, i: (blk(b, i), 0)),
                pl.BlockSpec((4 * H_A, rg), lambda b, h, i: (0, blk(b, i)))]

    st_spec = pl.BlockSpec((None, 2, hpb, DK_A, DV_A), lambda b, h, i: (b, 0, h, 0, 0))
    args = [qkvn, qkvn, qkvn, gcol, grow, qkvn, qkvn, qkvn, gcol, grow, s0]
    in_specs = specs(fwd_blk) + specs(bwd_blk) + [st_spec]
    aliases = {}
    if into is not None:
        for k_out, arr in enumerate(into):
            aliases[len(args)] = k_out
            args.append(arr)
            in_specs.append(pl.BlockSpec(memory_space=pl.ANY))
    n = qkvn.shape[0]
    return pl.pallas_call(
        functools.partial(_delta_kernel, chunk=CHUNK, hpb=hpb, n_alias=len(aliases), precise=precise),
        grid=(n_seq, nhp, ng),
        in_specs=in_specs,
        out_specs=[pl.BlockSpec((rg, hw), lambda b, h, i: (fwd_blk(b, i), h)),
                   pl.BlockSpec((rg, hw), lambda b, h, i: (bwd_blk(b, i), h)),
                   st_spec],
        out_shape=[jax.ShapeDtypeStruct((n, H_A * DV_A), F32),
                   jax.ShapeDtypeStruct((n, H_A * DV_A), F32),
                   jax.ShapeDtypeStruct(s0.shape, F32)],
        input_output_aliases=aliases,
        scratch_shapes=[pltpu.VMEM((2, hpb, DK_A, DV_A), F32)],
        compiler_params=_cparams("parallel", "parallel", "arbitrary"), name="gated_delta",
    )(*args)


def _flash_kernel(*refs, n_parts, has_cache, diff, tk, lam_init, aliased, head_lanes):
    it = iter(refs)
    q_refs = [next(it) for _ in range(n_parts)]
    k_refs = [next(it) for _ in range(n_parts)]
    v_ref = next(it)
    kc_refs = [next(it) for _ in range(n_parts)] if has_cache else []
    vc_ref = next(it) if has_cache else None
    lam_ref = next(it) if diff else None
    if aliased:
        next(it)
    o_ref = next(it)
    tq = q_refs[0].shape[0]
    lk = k_refs[0].shape[0]
    if diff:
        lp = lam_ref[...]
        lam = (jnp.exp(jnp.sum(lp[0:1] * lp[1:2], axis=-1, keepdims=True))
               - jnp.exp(jnp.sum(lp[2:3] * lp[3:4], axis=-1, keepdims=True)) + lam_init)
        lane = lax.broadcasted_iota(jnp.int32, (1, LANES), 1)

    def cols(ref, off, rows=None):
        return ref[:, off:off + LANES] if rows is None else ref[rows, off:off + LANES]

    for q_offs, k_offs, v_off, o_off in head_lanes:
        qs = [cols(r, off) for r, off in zip(q_refs, q_offs)]
        if diff:
            q = qs[0]
            qs = [jnp.concatenate([jnp.where(lane < DQK_B, q, 0), jnp.where(lane >= DQK_B, q, 0)], axis=0)]
        nrow = qs[0].shape[0]
        q_cat = qs[0] if n_parts == 1 else jnp.concatenate(qs, axis=1)

        def block(carry, ks, vv, q_cat=q_cat):
            m, l, acc = carry
            s = _dot_nt(q_cat, ks[0] if n_parts == 1 else jnp.concatenate(ks, axis=1))
            m_new = jnp.maximum(m, jnp.max(s, axis=-1, keepdims=True))
            alpha = jnp.exp2(m - m_new)
            p_ = jnp.exp2(s - m_new)
            l = alpha * l + jnp.sum(p_, axis=-1, keepdims=True)
            acc = alpha * acc + _dot(p_.astype(BF16), vv)
            return m_new, l, acc

        carry = (jnp.full((nrow, 1), -1e30, F32), jnp.zeros((nrow, 1), F32), jnp.zeros((nrow, LANES), F32))

        def body(t, carry, block=block, k_offs=k_offs, v_off=v_off):
            rows = pl.ds(pl.multiple_of(t * tk, tk), tk)
            return block(carry, [cols(r, off, rows) for r, off in zip(k_refs, k_offs)], cols(v_ref, v_off, rows))

        carry = lax.fori_loop(0, lk // tk, body, carry, unroll=True)
        if has_cache:
            carry = block(carry, [r[...] for r in kc_refs], vc_ref[...])
        _, l, acc = carry
        o = acc / l
        if diff:
            o = o[:tq] - lam * o[tq:]
        o_ref[:, o_off:o_off + LANES] = o.astype(o_ref.dtype)


def _flash(q_parts, k_parts, v_part, cache, lam, *, n_seq, n_heads, lq, lk, q_row0, k_row0, tq, tk, diff,
           lam_init, out_cols, into=None, all_heads=False):
    nqt = lq // tq
    qb0, kb0 = q_row0 // tq, k_row0 // lk
    args, in_specs = [], []
    if all_heads:
        assert cache is None
        for arr, _ in q_parts:
            args.append(arr)
            in_specs.append(pl.BlockSpec((tq, arr.shape[1]), lambda b, h, t: (qb0 + b * nqt + t, 0)))
        for arr, _ in k_parts + [v_part]:
            args.append(arr)
            in_specs.append(pl.BlockSpec((lk, arr.shape[1]), lambda b, h, t: (kb0 + b, 0)))
        head_lanes = tuple((tuple(cf(h) * LANES for _, cf in q_parts), tuple(cf(h) * LANES for _, cf in k_parts),
                            v_part[1](h) * LANES, h * LANES) for h in range(n_heads))
        grid = (n_seq, 1, nqt)
        out_spec = pl.BlockSpec((tq, out_cols), lambda b, h, t: (qb0 + b * nqt + t, 0))
    else:
        for arr, cf in q_parts:
            args.append(arr)
            in_specs.append(pl.BlockSpec((tq, LANES), lambda b, h, t, cf=cf: (qb0 + b * nqt + t, cf(h))))
        for arr, cf in k_parts + [v_part]:
            args.append(arr)
            in_specs.append(pl.BlockSpec((lk, LANES), lambda b, h, t, cf=cf: (kb0 + b, cf(h))))
        head_lanes = (((0,) * len(q_parts), (0,) * len(k_parts), 0, 0),)
        grid = (n_seq, n_heads, nqt)
        out_spec = pl.BlockSpec((tq, LANES), lambda b, h, t: (qb0 + b * nqt + t, h))
    if cache is not None:
        c_k, c_v, c_rows, c_blk = cache
        for arr, cf in c_k + [c_v]:
            args.append(arr)
            in_specs.append(pl.BlockSpec((c_rows, LANES), lambda b, h, t, cf=cf: (c_blk(b, h), cf(h))))
    if diff:
        args.append(lam)
        in_specs.append(pl.BlockSpec(lam.shape, lambda b, h, t: (0, 0)))
    aliases = {}
    if into is not None:
        aliases[len(args)] = 0
        args.append(into)
        in_specs.append(pl.BlockSpec(memory_space=pl.ANY))
    kern = functools.partial(_flash_kernel, n_parts=len(q_parts), has_cache=cache is not None, diff=diff, tk=tk,
                             lam_init=lam_init, aliased=into is not None, head_lanes=head_lanes)
    return pl.pallas_call(
        kern, grid=grid, in_specs=in_specs, out_specs=out_spec,
        out_shape=jax.ShapeDtypeStruct((q_parts[0][0].shape[0], out_cols), F32 if diff else BF16),
        input_output_aliases=aliases,
        compiler_params=_cparams("parallel", "parallel", "arbitrary"), name="flash_attention",
    )(*args)


def _modulated(x_ref, nw_ref, mods_ref, g, idx, d):
    x = _rmsnorm_rows(x_ref[...], nw_ref[...])
    return x * (1.0 + _mod_chunk(mods_ref, g, idx + 1, d)) + _mod_chunk(mods_ref, g, idx, d)


def _ffn_kernel(x_ref, nw_ref, mods_ref, wg_ref, wu_ref, wd_ref, of_ref, ob_ref, z_ref, od_ref, on_ref, sn_ref,
                wo_ref, o_ref, h_scr, acc_scr, xn_scr, *, rows, mod0, mix_gate, lam_init):
    i, f = pl.program_id(0), pl.program_id(1)
    d = x_ref.shape[1]
    g = rows.group(i)

    @pl.when(f == 0)
    def _():
        oa = of_ref[...] + ob_ref[...]
        z = z_ref[...]
        od = od_ref[...]
        ya, yb = [], []
        for hh in range(H_A):
            sl = slice(hh * LANES, (hh + 1) * LANES)
            ya.append(_rmsnorm_rows(oa[:, sl], on_ref[...]) * _silu(z[:, sl]))
        for hh in range(H_B):
            sl = slice(hh * LANES, (hh + 1) * LANES)
            yb.append(_rmsnorm_rows(od[:, sl], sn_ref[...]) * (1.0 - lam_init))
        y = jnp.concatenate(ya + yb, axis=1).astype(BF16)
        x1 = x_ref[...] + _mod_chunk(mods_ref, g, mix_gate, d) * _dot(y, wo_ref[...])
        xn_scr[...] = x1
        v = _rmsnorm_rows(x1, nw_ref[...])
        h_scr[...] = (v * (1.0 + _mod_chunk(mods_ref, g, mod0 + 1, d)) + _mod_chunk(mods_ref, g, mod0, d)).astype(BF16)
        acc_scr[...] = jnp.zeros_like(acc_scr)

    h = h_scr[...]
    act = _silu(_dot(h, wg_ref[...])) * _dot(h, wu_ref[...])
    acc_scr[...] += _dot(act.astype(BF16), wd_ref[...])

    @pl.when(f == pl.num_programs(1) - 1)
    def _():
        o_ref[...] = xn_scr[...] + _mod_chunk(mods_ref, g, mod0 + 2, d) * acc_scr[...]


def _even_tail(x, o_f, o_b, z, o_d, onorm, subln, w_out, norm_w, mods, w_gu, w_down, rows, mix_gate, mod0, tf,
               lam_init):
    n, d = x.shape
    tm = rows.tm
    ff = w_down.shape[0]
    nf = ff // tf
    hw = H_A * DV_A
    row = lambda c: pl.BlockSpec((tm, c), lambda i, f: (i, 0))
    one = lambda c: pl.BlockSpec((1, c), lambda i, f: (0, 0))
    return pl.pallas_call(
        functools.partial(_ffn_kernel, rows=rows, mod0=mod0, mix_gate=mix_gate, lam_init=lam_init),
        grid=(n // tm, nf),
        in_specs=[row(d), one(d), pl.BlockSpec(mods.shape, lambda i, f: (0, 0)),
                  pl.BlockSpec((d, tf), lambda i, f: (0, f)),
                  pl.BlockSpec((d, tf), lambda i, f: (0, nf + f)),
                  pl.BlockSpec((tf, d), lambda i, f: (f, 0)),
                  row(hw), row(hw), row(hw), row(hw), one(LANES), one(LANES),
                  pl.BlockSpec(w_out.shape, lambda i, f: (0, 0))],
        out_specs=row(d),
        out_shape=jax.ShapeDtypeStruct((n, d), F32),
        scratch_shapes=[pltpu.VMEM((tm, d), BF16), pltpu.VMEM((tm, d), F32), pltpu.VMEM((tm, d), F32)],
        compiler_params=_cparams("parallel", "arbitrary"), name="even_tail",
    )(x, norm_w.reshape(1, d).astype(F32), mods, w_gu, w_gu, w_down, o_f, o_b, z, o_d,
      onorm.reshape(1, LANES).astype(F32), subln.reshape(1, LANES).astype(F32), w_out)


def _moe_kernel(x_ref, a_ref, wo_ref, nw_ref, mods_ref, rw_ref, wgu_ref, wd_ref, *rest, rows, mix_gate, mod0, n_exp,
                sb, final):
    if final:
        fw_ref, o_ctx_ref, o_lat_ref, h_scr, g_scr, rk_scr, rkt_scr, acc_scr, xn_scr = rest
    else:
        o_ref, h_scr, g_scr, rk_scr, rkt_scr, acc_scr, xn_scr = rest
    i, e = pl.program_id(0), pl.program_id(1)
    tm, d = x_ref.shape
    g = rows.group(i)
    lane = lax.broadcasted_iota(jnp.int32, (1, LANES), 1)

    @pl.when(e == 0)
    def _():
        x1 = x_ref[...] + _mod_chunk(mods_ref, g, mix_gate, d) * _dot(a_ref[...], wo_ref[...])
        xn_scr[...] = x1
        h = (_rmsnorm_rows(x1, nw_ref[...]) * (1.0 + _mod_chunk(mods_ref, g, mod0 + 1, d))
             + _mod_chunk(mods_ref, g, mod0, d))
        h_scr[...] = h.astype(BF16)
        acc_scr[...] = jnp.zeros_like(acc_scr)
        logits = jnp.where(lane < n_exp, _mm(h, rw_ref[...], True), -1e30)
        pe = jnp.exp(logits - jnp.max(logits, axis=-1, keepdims=True))
        probs = pe / jnp.sum(pe, axis=-1, keepdims=True)
        p1 = jnp.max(probs, axis=-1, keepdims=True)
        i1 = jnp.min(jnp.where(probs == p1, lane, LANES), axis=-1, keepdims=True)
        rest = jnp.where(lane == i1, -1.0, probs)
        p2 = jnp.max(rest, axis=-1, keepdims=True)
        i2 = jnp.min(jnp.where(rest == p2, lane, LANES), axis=-1, keepdims=True)
        den = p1 + p2
        gates = jnp.where(lane == i1, p1 / den, 0.0) + jnp.where(lane == i2, p2 / den, 0.0)
        g_scr[...] = gates
        t_r = lax.broadcasted_iota(jnp.int32, (tm, tm), 0)
        t_c = lax.broadcasted_iota(jnp.int32, (tm, tm), 1)
        earlier = jnp.where(t_r > t_c, 1.0, 0.0).astype(BF16)
        sel = gates > 0.0
        rank = jnp.where(sel, _dot(earlier, jnp.where(sel, 1.0, 0.0).astype(BF16)), -1.0)
        rk_scr[...] = rank
        rkt_scr[...] = jnp.transpose(rank)[:rkt_scr.shape[0], :]

    ff = wd_ref.shape[0]
    g_e = jnp.sum(jnp.where(lane == e, g_scr[...], 0.0), axis=-1, keepdims=True)
    rk_col = jnp.sum(jnp.where(lane == e, rk_scr[...], 0.0), axis=-1, keepdims=True)
    rk_row = rkt_scr[pl.ds(e, 1), :]
    n_rows = (jnp.max(rk_row) + 1.0).astype(jnp.int32)
    g_hi = g_e.astype(BF16).astype(F32)
    g2 = jnp.where(lane == 0, g_hi, jnp.where(lane == 1, g_e - g_hi, 0.0)).astype(BF16)
    def expert_rows(slot0, nr):
        base = slot0.astype(F32)
        r_sub = lax.broadcasted_iota(jnp.int32, (nr, 1), 0).astype(F32)
        r_lane = lax.broadcasted_iota(jnp.int32, (1, nr), 1).astype(F32)
        pick = jnp.where(rk_row == r_sub + base, 1.0, 0.0).astype(BF16)
        xg = _dot(pick, h_scr[...]).astype(BF16)
        gr = _dot(pick, g2)
        gate_r = gr[:, 0:1] + gr[:, 1:2]
        gu = _dot(xg, wgu_ref[...])
        act = _silu(gu[:, :ff]) * gu[:, ff:]
        y = _dot(act.astype(BF16), wd_ref[...])
        put = jnp.where(rk_col == r_lane + base, 1.0, 0.0).astype(BF16)
        acc_scr[...] += _dot(put, (y * gate_r).astype(BF16))

    half, quarter = sb // 2, sb // 4
    rem = n_rows % sb
    n_full = n_rows // sb + jnp.where(rem > half, 1, 0)

    def full_body(jb, carry):
        expert_rows(jb * sb, sb)
        return carry

    lax.fori_loop(0, n_full, full_body, 0)

    @pl.when((rem > quarter) & (rem <= half))
    def _():
        expert_rows(n_full * sb, half)

    @pl.when((rem > 0) & (rem <= quarter))
    def _():
        expert_rows(n_full * sb, quarter)

    def result():
        return xn_scr[...] + _mod_chunk(mods_ref, g, mod0 + 2, d) * acc_scr[...]

    last = e == n_exp - 1
    if final:
        @pl.when(last & (i < rows.ctx_tiles))
        def _():
            o_ctx_ref[...] = _rmsnorm_rows(result(), fw_ref[...])

        @pl.when(last & (i >= rows.ctx_tiles))
        def _():
            o_lat_ref[...] = _rmsnorm_rows(result(), fw_ref[...])
    else:
        @pl.when(last)
        def _():
            o_ref[...] = result()


def _odd_tail(x, a, w_out, norm_w, mods, router_w, w_gu, w_down, rows, mix_gate, mod0, final_w=None):
    n, d = x.shape
    tm = rows.tm
    n_exp, _, ff2 = w_gu.shape
    rw = jnp.zeros((d, LANES), F32).at[:, :n_exp].set(router_w.astype(F32))
    final = final_w is not None
    args = [x, a, w_out, norm_w.reshape(1, d).astype(F32), mods, rw, w_gu, w_down]
    in_specs = [pl.BlockSpec((tm, d), lambda i, e: (i, 0)),
                pl.BlockSpec((tm, a.shape[1]), lambda i, e: (i, 0), pipeline_mode=pl.Buffered(1)),
                pl.BlockSpec(w_out.shape, lambda i, e: (0, 0), pipeline_mode=pl.Buffered(1)),
                pl.BlockSpec((1, d), lambda i, e: (0, 0)),
                pl.BlockSpec(mods.shape, lambda i, e: (0, 0)),
                pl.BlockSpec((d, LANES), lambda i, e: (0, 0)),
                pl.BlockSpec((None, d, ff2), lambda i, e: (e, 0, 0)),
                pl.BlockSpec((None, ff2 // 2, d), lambda i, e: (e, 0, 0))]
    if final:
        args.append(final_w.reshape(1, d).astype(F32))
        in_specs.append(pl.BlockSpec((1, d), lambda i, e: (0, 0)))
        ct = rows.ctx_tiles
        out_specs = [pl.BlockSpec((tm, d), lambda i, e: (jnp.minimum(i, ct - 1), 0), pipeline_mode=pl.Buffered(1)),
                     pl.BlockSpec((tm, d), lambda i, e: (jnp.maximum(i - ct, 0), 0), pipeline_mode=pl.Buffered(1))]
        out_shape = [jax.ShapeDtypeStruct((ct * tm, d), F32), jax.ShapeDtypeStruct((n - ct * tm, d), F32)]
    else:
        out_specs = pl.BlockSpec((tm, d), lambda i, e: (i, 0))
        out_shape = jax.ShapeDtypeStruct((n, d), F32)
    return pl.pallas_call(
        functools.partial(_moe_kernel, rows=rows, mix_gate=mix_gate, mod0=mod0, n_exp=n_exp, sb=MOE_ROW_BLOCK,
                          final=final),
        grid=(n // tm, n_exp), in_specs=in_specs, out_specs=out_specs, out_shape=out_shape,
        scratch_shapes=[pltpu.VMEM((tm, d), BF16), pltpu.VMEM((tm, LANES), F32), pltpu.VMEM((tm, LANES), F32),
                        pltpu.VMEM((SUBLANES * ((n_exp + SUBLANES - 1) // SUBLANES), tm), F32),
                        pltpu.VMEM((tm, d), F32), pltpu.VMEM((tm, d), F32)],
        compiler_params=_cparams("arbitrary", "arbitrary"), name="odd_tail",
    )(*args)


def _final_kernel(x_ref, w_ref, o_ref):
    o_ref[...] = _rmsnorm_rows(x_ref[...], w_ref[...])


def _final_norm(x, w, tm):
    n, d = x.shape
    return pl.pallas_call(
        _final_kernel, grid=(n // tm,),
        in_specs=[pl.BlockSpec((tm, d), lambda i: (i, 0)), pl.BlockSpec((1, d), lambda i: (0, 0))],
        out_specs=pl.BlockSpec((tm, d), lambda i: (i, 0)),
        out_shape=jax.ShapeDtypeStruct((n, d), F32),
        compiler_params=_cparams("parallel"), name="final_norm",
    )(x, w.reshape(1, d).astype(F32))


def kernel(x_prompt, x_sample, state_delta, cache_diff_k, cache_diff_v, cache_mla_ckv, cache_mla_krope, c, c_ctx,
           mod_w, mod_b, norm_mix, norm_ffn, final_norm, ev_w_in, ev_conv_w, ev_a_log, ev_dt_bias, ev_onorm,
           ev_lambda, ev_subln, ev_w_out, ffn_w_gu, ffn_w_down, od_w_in, od_q_norm, od_kv_norm, od_w_uq, od_w_ukv,
           od_w_out, moe_router, moe_w_gu, moe_w_down):
    bp, lp, d = x_prompt.shape
    bs, ls, _ = x_sample.shape
    past = cache_diff_k.shape[3]
    depth = mod_w.shape[0]
    n_p, n_s = bp * lp, bs * ls
    n = n_p + n_s
    assert n_p % ls == 0 and past == lp

    rows = _Rows(n_p, n_s, ls, _tile(math.gcd(n_p, ls), 512))
    rows_m = _Rows(n_p, n_s, ls, _tile(math.gcd(n_p, ls), 1024))
    rg = _tile(math.gcd(lp, ls), 256)
    cos_t, sin_t = _rope_tables(rows, ls)

    x = jnp.concatenate([x_prompt.reshape(n_p, d), x_sample.reshape(n_s, d)], axis=0)
    n_grp = 1 + bs
    cc = jnp.zeros((2 * SUBLANES * ((n_grp + 15) // 16), d), F32).at[0].set(c_ctx).at[1:n_grp].set(c)
    mods_all = _mods(cc, mod_w, mod_b)

    hq = H_A * DK_A
    hb2 = H_B * 2 * DQK_B
    sm0 = 4 * hq
    qb0 = sm0 + 4 * H_A
    q_scale = (DQK_B ** -0.5) * LOG2E
    mla_scale = ((NOPE_C + ROPE_C) ** -0.5) * LOG2E
    sd_new, dk_new, dv_new, ckv_new, kr_new = [], [], [], [], []

    for layer in range(depth):
        j = layer // 2
        mods = mods_all[layer]
        if layer % 2 == 0:
            lam_init = 0.8 - 0.6 * math.exp(-0.3 * layer)
            w = ev_w_in[j]
            sm = w[:, sm0:qb0].reshape(d, 4, H_A).transpose(0, 2, 1).reshape(d, 4 * H_A)
            w_in = jnp.concatenate([w[:, :sm0], sm, jnp.zeros((d, LANES - 4 * H_A), w.dtype), w[:, qb0:]],
                                   axis=1).astype(BF16)
            qkvn, z_gate, gcol, grow, att, kv_f = _even_proj(
                x, norm_mix[layer], mods, w_in, ev_conv_w[j], ev_a_log[j], ev_dt_bias[j], (cos_t, sin_t), rows,
                3 * hq, hq, hb2, q_scale, lp, ls, n_p)
            o_f, o_b, s_p = _delta_stage(qkvn, gcol, grow, jnp.zeros((bp, 2, H_A, DK_A, DV_A), F32), bp, lp, 0, rg,
                                         precise=True)
            o_f, o_b, _ = _delta_stage(qkvn, gcol, grow, state_delta[:, j].astype(F32), bs, ls, n_p, rg,
                                       into=(o_f, o_b))
            sd_new.append(s_p)
            lam_p = ev_lambda[j].astype(F32)
            nb = hb2 // LANES
            qp, kp, vp = [(att, lambda h: h)], [(att, lambda h: nb + h)], (att, lambda h: 2 * nb + h)
            o_d = _flash(qp, kp, vp, None, lam_p, n_seq=bp, n_heads=H_B, lq=lp, lk=lp, q_row0=0, k_row0=0,
                         tq=_tile(lp, 256), tk=_tile(lp, 512), diff=True, lam_init=lam_init, out_cols=hb2,
                         all_heads=True)
            ck = cache_diff_k[:, j].reshape(bs * H_B * past, 2 * DQK_B).astype(BF16)
            cv = cache_diff_v[:, j].reshape(bs * H_B * past, DV_B).astype(BF16)
            cache = ([(ck, lambda h: 0)], (cv, lambda h: 0), past, lambda b, h: b * H_B + h)
            o_d = _flash(qp, kp, vp, cache, lam_p, n_seq=bs, n_heads=H_B, lq=ls, lk=ls, q_row0=n_p, k_row0=n_p,
                         tq=_tile(ls, 512), tk=_tile(ls, 2048), diff=True, lam_init=lam_init, out_cols=hb2, into=o_d)
            dk_new.append(kv_f[:n_p, :hb2].reshape(bp, lp, H_B, 2 * DQK_B).transpose(0, 2, 1, 3))
            dv_new.append(kv_f[:n_p, hb2:].reshape(bp, lp, H_B, DV_B).transpose(0, 2, 1, 3))
            ff = ffn_w_down.shape[1]
            x = _even_tail(x, o_f, o_b, z_gate, o_d, ev_onorm[j], ev_subln[j], ev_w_out[j].astype(BF16),
                           norm_ffn[layer], mods, ffn_w_gu[j].astype(BF16), ffn_w_down[j].astype(BF16), rows, 2, 3,
                           ff // 2 if (ff // 2) % LANES == 0 else ff, lam_init)
        else:
            n_in = Q_LORA + KV_LORA + ROPE_C
            pad = (-n_in) % LANES
            w_in = jnp.concatenate([od_w_in[j], jnp.zeros((d, pad), od_w_in.dtype)], axis=1).astype(BF16)
            wq = od_w_uq[j].reshape(Q_LORA, H_C, NOPE_C + ROPE_C)
            wq_rope = jnp.concatenate([wq[:, :, NOPE_C:], jnp.zeros((Q_LORA, H_C, LANES - ROPE_C), wq.dtype)], axis=2)
            wq2 = jnp.concatenate([wq[:, :, :NOPE_C].reshape(Q_LORA, H_C * NOPE_C),
                                   wq_rope.reshape(Q_LORA, H_C * LANES)], axis=1).astype(BF16)
            q_all, kv_tok, kr_tok, ckv_n, kr_f = _odd_proj(
                x, norm_mix[layer], mods, w_in, od_q_norm[j], od_kv_norm[j], wq2, od_w_ukv[j].astype(BF16),
                (cos_t, sin_t), rows, mla_scale, H_C * NOPE_C)
            ckv_new.append(ckv_n[:n_p].reshape(bp, lp, KV_LORA))
            kr_new.append(kr_f[:n_p, :ROPE_C].reshape(bp, lp, ROPE_C))
            rows_k = _Rows(bs * past, 0, past, _tile(past, 256))
            kv_c = _norm_matmul(cache_mla_ckv[:, j].reshape(bs * past, KV_LORA), od_w_ukv[j].astype(BF16), rows_k,
                                tn=512, out_dtype=BF16)
            kr_c = jnp.concatenate([cache_mla_krope[:, j].reshape(bs * past, ROPE_C),
                                    jnp.zeros((bs * past, LANES - ROPE_C), F32)], axis=1).astype(BF16)
            qp = [(q_all, lambda h: h), (q_all, lambda h: H_C + h)]
            kp = [(kv_tok, lambda h: 2 * h), (kr_tok, lambda h: 0)]
            vp = (kv_tok, lambda h: 2 * h + 1)
            o = _flash(qp, kp, vp, None, None, n_seq=bp, n_heads=H_C, lq=lp, lk=lp, q_row0=0, k_row0=0,
                       tq=_tile(lp, 256), tk=_tile(lp, 512), diff=False, lam_init=0.0, out_cols=H_C * V_C,
                       all_heads=True)
            cache = ([(kv_c, lambda h: 2 * h), (kr_c, lambda h: 0)], (kv_c, lambda h: 2 * h + 1), past,
                     lambda b, h: b)
            o = _flash(qp, kp, vp, cache, None, n_seq=bs, n_heads=H_C, lq=ls, lk=ls, q_row0=n_p, k_row0=n_p,
                       tq=_tile(ls, 1024), tk=_tile(ls, 2048), diff=False, lam_init=0.0, out_cols=H_C * V_C, into=o)
            x = _odd_tail(x, o, od_w_out[j].astype(BF16), norm_ffn[layer], mods, moe_router[j],
                          moe_w_gu[j].astype(BF16), moe_w_down[j].astype(BF16), rows_m, 2, 3,
                          final_w=final_norm if layer == depth - 1 else None)

    if depth % 2 == 0:
        y_p, y_s = x
    else:
        y = _final_norm(x, final_norm, rows.tm)
        y_p, y_s = y[:n_p], y[n_p:]
    return (y_p.reshape(bp, lp, d), y_s.reshape(bs, ls, d), jnp.stack(sd_new, axis=1),
            jnp.stack(dk_new, axis=1), jnp.stack(dv_new, axis=1), jnp.stack(ckv_new, axis=1),
            jnp.stack(kr_new, axis=1))
```

```python
import functools
import math

import numpy as np
import jax
import jax.numpy as jnp
from jax import lax
from jax.experimental import pallas as pl
from jax.experimental.pallas import tpu as pltpu

F32 = jnp.float32
BF16 = jnp.bfloat16

EPS = 1e-6
LOG2E = 1.4426950408889634
GRID_W = 64
ROPE_BASE = 10000.0
H_A, DK_A, DV_A = 4, 128, 128
CONV_K = 3
CHUNK = 64
H_B, DQK_B, DV_B = 4, 64, 128
H_C, NOPE_C, ROPE_C, V_C = 8, 128, 64, 128
Q_LORA, KV_LORA = 384, 256
LANES = 128
SUBLANES = 8
VMEM_LIMIT_BYTES = 56 * 1024 * 1024


def _cparams(*sem):
    return pltpu.CompilerParams(dimension_semantics=sem, vmem_limit_bytes=VMEM_LIMIT_BYTES)


def _dot(a, b):
    return jnp.dot(a, b, preferred_element_type=F32)


def _dot_nt(a, b):
    return lax.dot_general(a, b, (((1,), (1,)), ((), ())), preferred_element_type=F32)


def _dot_tn(a, b):
    return lax.dot_general(a, b, (((0,), (0,)), ((), ())), preferred_element_type=F32)


def _split3(x):
    hi = x.astype(BF16)
    r = x - hi.astype(F32)
    mid = r.astype(BF16)
    lo = (r - mid.astype(F32)).astype(BF16)
    return hi, mid, lo


def _dot_f32(a, b):
    ah, am, al = _split3(a)
    bh, bm, bl = _split3(b)
    return (_dot(ah, bh) + (_dot(ah, bm) + _dot(am, bh))
            + (_dot(am, bm) + _dot(ah, bl) + _dot(al, bh)))


def _silu(x):
    return x * jax.nn.sigmoid(x)


def _tile(n, pref, mult=SUBLANES):
    t = min(n, pref)
    while t > mult and (n % t or t % mult):
        t -= mult
    assert n % t == 0, (n, pref)
    return t


class _Rows:
    def __init__(self, n_ctx, n_lat, lat_len, tm):
        assert n_ctx % tm == 0 and lat_len % tm == 0
        self.tm = tm
        self.n = n_ctx + n_lat
        self.ctx_tiles = n_ctx // tm
        self.seq_tiles = lat_len // tm

    def group(self, i):
        return jnp.where(i < self.ctx_tiles, 0, 1 + jnp.maximum(i - self.ctx_tiles, 0) // self.seq_tiles)

    def pos_block(self, i):
        return jnp.where(i < self.ctx_tiles, 0, 1 + jnp.maximum(i - self.ctx_tiles, 0) % self.seq_tiles)


def _mod_kernel(c_ref, w_ref, b_ref, o_ref):
    o_ref[...] = _dot_f32(_silu(c_ref[...]), w_ref[...]) + b_ref[...]


def _mods(cc, mod_w, mod_b):
    depth, d, n6 = mod_w.shape
    tn = _tile(n6, 512, LANES)
    return pl.pallas_call(
        _mod_kernel,
        grid=(depth, n6 // tn),
        in_specs=[pl.BlockSpec(cc.shape, lambda l, j: (0, 0)),
                  pl.BlockSpec((None, d, tn), lambda l, j: (l, 0, j)),
                  pl.BlockSpec((None, 1, tn), lambda l, j: (l, 0, j))],
        out_specs=pl.BlockSpec((None, cc.shape[0], tn), lambda l, j: (l, 0, j)),
        out_shape=jax.ShapeDtypeStruct((depth, cc.shape[0], n6), F32),
        compiler_params=_cparams("parallel", "parallel"),
        name="mods",
    )(cc, mod_w, mod_b.reshape(depth, 1, n6))


def _mod_chunk(mods_ref, g, idx, d):
    return mods_ref[pl.ds(g, 1), idx * d:(idx + 1) * d]


def _rmsnorm_rows(x, w):
    ms = jnp.mean(x * x, axis=-1, keepdims=True)
    return x * lax.rsqrt(ms + EPS) * w


def _rope(x, cos, sin_signed):
    lane = lax.broadcasted_iota(jnp.int32, (1, LANES), 1)
    low = (lane % 32) < 16
    outs = []
    for c in range(x.shape[1] // LANES):
        xc = x[:, c * LANES:(c + 1) * LANES]
        fwd = pltpu.roll(xc, LANES - 16, axis=1)
        bwd = pltpu.roll(xc, 16, axis=1)
        outs.append(xc * cos + jnp.where(low, fwd, bwd) * sin_signed)
    return outs[0] if len(outs) == 1 else jnp.concatenate(outs, axis=1)


def _rope_tables(rows, lat_len):
    pos = jnp.arange(lat_len)
    r = (pos // GRID_W).astype(F32)
    c = (pos % GRID_W).astype(F32)
    inv = ROPE_BASE ** (-jnp.arange(0, 32, 2, dtype=F32) / 32)
    ang_r = r[:, None] * inv[None, :]
    ang_c = c[:, None] * inv[None, :]
    ang = jnp.concatenate([ang_r, ang_r, ang_c, ang_c] * 2, axis=-1)
    sign = jnp.where((jnp.arange(LANES) % 32) < 16, -1.0, 1.0).astype(F32)
    cos = jnp.concatenate([jnp.ones((rows.tm, LANES), F32), jnp.cos(ang)], axis=0)
    sin = jnp.concatenate([jnp.zeros((rows.tm, LANES), F32), jnp.sin(ang) * sign[None, :]], axis=0)
    return cos, sin


def _nm_kernel(*refs, rows, d_mod, has_norm, mod_idx, has_rope, epi, emit_h):
    it = iter(refs)
    x_ref = next(it)
    w_ref = next(it)
    nw_ref = next(it) if has_norm else None
    mods_ref = next(it) if mod_idx is not None else None
    cos_ref, sin_ref = (next(it), next(it)) if has_rope else (None, None)
    o_ref = next(it)
    h_out_ref = next(it) if emit_h else None
    h_scr = next(it)
    i, j = pl.program_id(0), pl.program_id(1)

    @pl.when(j == 0)
    def _():
        x = x_ref[...].astype(F32)
        if has_norm:
            x = _rmsnorm_rows(x, nw_ref[...])
        if mod_idx is not None:
            g = rows.group(i)
            x = x * (1.0 + _mod_chunk(mods_ref, g, mod_idx[1], d_mod)) + _mod_chunk(mods_ref, g, mod_idx[0], d_mod)
        h_scr[...] = x.astype(BF16)
        if emit_h:
            h_out_ref[...] = x

    acc = _dot(h_scr[...], w_ref[...])
    kinds = sorted(set(epi))
    for kind in kinds:
        cond = None
        for jj, e in enumerate(epi):
            if e == kind:
                cond = (j == jj) if cond is None else (cond | (j == jj))

        def _store(kind=kind):
            y = acc
            if kind[0]:
                y = _rope(y, cos_ref[...], sin_ref[...])
            if kind[1] != 1.0:
                y = y * kind[1]
            o_ref[...] = y.astype(o_ref.dtype)

        if len(kinds) == 1:
            _store()
        else:
            pl.when(cond)(_store)


def _norm_matmul(x, w, rows, *, tn, norm_w=None, mods=None, mod_idx=None, rope=None, epi=None,
                 emit_h=False, out_dtype=F32):
    n, k = x.shape
    nout = w.shape[1]
    nj = nout // tn
    tm = rows.tm
    epi = tuple(epi) if epi is not None else ((False, 1.0),) * nj
    assert len(epi) == nj and n % tm == 0
    has_rope = any(e[0] for e in epi)
    args = [x, w]
    in_specs = [pl.BlockSpec((tm, k), lambda i, j: (i, 0)), pl.BlockSpec((k, tn), lambda i, j: (0, j))]
    if norm_w is not None:
        args.append(norm_w.reshape(1, k).astype(F32))
        in_specs.append(pl.BlockSpec((1, k), lambda i, j: (0, 0)))
    if mod_idx is not None:
        args.append(mods)
        in_specs.append(pl.BlockSpec(mods.shape, lambda i, j: (0, 0)))
    if has_rope:
        args += list(rope)
        in_specs += [pl.BlockSpec((tm, LANES), lambda i, j: (rows.pos_block(i), 0))] * 2
    out_shape = [jax.ShapeDtypeStruct((n, nout), out_dtype)]
    out_specs = [pl.BlockSpec((tm, tn), lambda i, j: (i, j))]
    if emit_h:
        out_shape.append(jax.ShapeDtypeStruct((n, k), F32))
        out_specs.append(pl.BlockSpec((tm, k), lambda i, j: (i, 0)))
    kern = functools.partial(_nm_kernel, rows=rows, d_mod=k, has_norm=norm_w is not None, mod_idx=mod_idx,
                             has_rope=has_rope, epi=epi, emit_h=emit_h)
    out = pl.pallas_call(
        kern, grid=(n // tm, nj), in_specs=in_specs, out_specs=out_specs, out_shape=out_shape,
        scratch_shapes=[pltpu.VMEM((tm, k), BF16)],
        compiler_params=_cparams("parallel", "arbitrary"), name="norm_matmul",
    )(*args)
    return out if emit_h else out[0]


PROJ_CHUNK = 512


def _pos_in_seq(r, length):
    return r & (length - 1) if length & (length - 1) == 0 else lax.rem(r, length)


def _even_proj_kernel(x_ref, xp_ref, xn_ref, nw_ref, mods_ref, w_ref, cw_ref, alog_ref, dt_ref, cos_ref, sin_ref,
                      qkv_ref, z_ref, gc_ref, gr_ref, att_ref, kv_ref,
                      *, rows, n_conv, n_z, hb2, q_scale, ctx_len, lat_len, n_ctx_rows):
    i = pl.program_id(0)
    tm, d = x_ref.shape
    g = rows.group(i)

    def modulated(ref):
        v = _rmsnorm_rows(ref[...], nw_ref[...])
        return (v * (1.0 + _mod_chunk(mods_ref, g, 1, d)) + _mod_chunk(mods_ref, g, 0, d)).astype(BF16)

    h = modulated(x_ref)
    halo = xp_ref.shape[0]
    h_ext = jnp.concatenate([h, modulated(xp_ref), modulated(xn_ref)], axis=0)
    p_ext = jnp.concatenate([_dot(h_ext, w_ref[:, c0:min(c0 + PROJ_CHUNK, n_conv)])
                             for c0 in range(0, n_conv, PROJ_CHUNK)], axis=1)
    p = p_ext[:tm]
    p_before = p_ext[tm + halo - 1:tm + halo, :]
    p_after = p_ext[tm + halo:tm + halo + 1, :]
    row = lax.broadcasted_iota(jnp.int32, (tm, 1), 0)
    r_glob = row + i * tm
    in_ctx = i < rows.ctx_tiles
    pos = jnp.where(in_ctx, _pos_in_seq(r_glob, ctx_len), _pos_in_seq(r_glob - n_ctx_rows, lat_len))
    seq_last = jnp.where(in_ctx, ctx_len - 1, lat_len - 1)
    x_prev = jnp.where(pos == 0, 0.0, jnp.where(row == 0, p_before, pltpu.roll(p, 1, axis=0)))
    x_next = jnp.where(pos == seq_last, 0.0, jnp.where(row == tm - 1, p_after, pltpu.roll(p, tm - 1, axis=0)))
    cw = cw_ref[...]
    y = _silu(x_prev * cw[0:1, :] + p * cw[1:2, :] + x_next * cw[2:3, :])
    n_qk = 2 * H_A
    for hh in range(n_conv // LANES):
        yh = y[:, hh * LANES:(hh + 1) * LANES]
        if hh < n_qk:
            yh = yh * lax.rsqrt(jnp.sum(yh * yh, axis=-1, keepdims=True) + EPS)
            if hh < n_qk // 2:
                yh = yh * (DK_A ** -0.5)
        qkv_ref[:, hh * LANES:(hh + 1) * LANES] = yh
    z_ref[...] = _dot(h, w_ref[:, n_conv:n_conv + n_z])
    s = _dot(h, w_ref[:, n_conv + n_z:n_conv + n_z + LANES])
    lane = lax.broadcasted_iota(jnp.int32, (1, LANES), 1)
    zg = s + dt_ref[...]
    softplus = jnp.maximum(zg, 0.0) + jnp.log(1.0 + jnp.exp(-jnp.abs(zg)))
    gates = jnp.where((lane % 4) < 2, -jnp.exp(alog_ref[...]) * softplus, jax.nn.sigmoid(s))
    gc_ref[...] = gates
    gr_ref[...] = jnp.transpose(gates)[:gr_ref.shape[0], :]
    n_main = n_conv + n_z + LANES
    cos, sin = cos_ref[...], sin_ref[...]
    for part in range(3):
        for c0 in range(0, hb2, PROJ_CHUNK):
            c1 = min(c0 + PROJ_CHUNK, hb2)
            y = _dot(h, w_ref[:, n_main + part * hb2 + c0:n_main + part * hb2 + c1])
            if part < 2:
                y = _rope(y, cos, sin)
            if part == 0:
                y = y * q_scale
            else:
                kv_ref[:, (part - 1) * hb2 + c0:(part - 1) * hb2 + c1] = y
            att_ref[:, part * hb2 + c0:part * hb2 + c1] = y.astype(BF16)


def _even_proj(x, norm_w, mods, w, conv_w, a_log, dt_bias, rope, rows, n_conv, n_z, hb2, q_scale, ctx_len,
               lat_len, n_ctx_rows):
    n, d = x.shape
    tm = rows.tm
    halo = 2 * SUBLANES
    hb = tm // halo
    alog_row = jnp.zeros((1, LANES), F32).at[0, :4 * H_A].set(
        jnp.stack([a_log[0], a_log[1], a_log[0], a_log[1]], axis=-1).reshape(-1).astype(F32))
    dt_row = jnp.zeros((1, LANES), F32).at[0, :4 * H_A].set(
        jnp.stack([dt_bias[0], dt_bias[1], jnp.zeros_like(dt_bias[0]), jnp.zeros_like(dt_bias[0])],
                  axis=-1).reshape(-1).astype(F32))
    kern = functools.partial(_even_proj_kernel, rows=rows, n_conv=n_conv, n_z=n_z, hb2=hb2, q_scale=q_scale,
                             ctx_len=ctx_len, lat_len=lat_len, n_ctx_rows=n_ctx_rows)
    tab = pl.BlockSpec((tm, LANES), lambda i: (rows.pos_block(i), 0))
    full = lambda a: pl.BlockSpec(a.shape, lambda i: (0, 0))
    nw = norm_w.reshape(1, d).astype(F32)
    cw = conv_w.astype(F32)
    outs = [(n_conv, F32), (n_z, F32), (LANES, F32), None, (3 * hb2, BF16), (2 * hb2, F32)]
    out_specs = [pl.BlockSpec((tm, c[0]), lambda i: (i, 0)) if c else pl.BlockSpec((4 * H_A, tm), lambda i: (0, i))
                 for c in outs]
    out_shape = [jax.ShapeDtypeStruct((n, c[0]), c[1]) if c else jax.ShapeDtypeStruct((4 * H_A, n), F32)
                 for c in outs]
    return pl.pallas_call(
        kern, grid=(n // tm,),
        in_specs=[pl.BlockSpec((tm, d), lambda i: (i, 0)),
                  pl.BlockSpec((halo, d), lambda i: (jnp.maximum(i * hb - 1, 0), 0)),
                  pl.BlockSpec((halo, d), lambda i: (jnp.minimum((i + 1) * hb, n // halo - 1), 0)),
                  full(nw), full(mods), full(w), full(cw), full(alog_row), full(dt_row), tab, tab],
        out_specs=out_specs, out_shape=out_shape,
        compiler_params=_cparams("parallel"), name="even_proj",
    )(x, x, x, nw, mods, w, cw, alog_row, dt_row, *rope)


def _odd_proj_kernel(x_ref, nw_ref, mods_ref, w_ref, qn_ref, kvn_ref, wq_ref, wkv_ref, cos_ref, sin_ref,
                     q_ref, kv_ref, kr_ref, ckv_ref, krf_ref, *, rows, scale, n_nope):
    i = pl.program_id(0)
    d = x_ref.shape[1]
    h = _modulated(x_ref, nw_ref, mods_ref, rows.group(i), 0, d).astype(BF16)
    p = _dot(h, w_ref[...])
    cos, sin = cos_ref[...], sin_ref[...]
    cq = _rmsnorm_rows(p[:, :Q_LORA], qn_ref[...]).astype(BF16)
    for c0 in range(0, wq_ref.shape[1], PROJ_CHUNK):
        y = _dot(cq, wq_ref[:, c0:c0 + PROJ_CHUNK])
        if c0 >= n_nope:
            y = _rope(y, cos, sin)
        q_ref[:, c0:c0 + PROJ_CHUNK] = (y * scale).astype(BF16)
    ckv = _rmsnorm_rows(p[:, Q_LORA:Q_LORA + KV_LORA], kvn_ref[...])
    ckv_ref[...] = ckv
    ckv16 = ckv.astype(BF16)
    for c0 in range(0, wkv_ref.shape[1], PROJ_CHUNK):
        kv_ref[:, c0:c0 + PROJ_CHUNK] = _dot(ckv16, wkv_ref[:, c0:c0 + PROJ_CHUNK]).astype(BF16)
    kr = _rope(p[:, Q_LORA + KV_LORA:], cos, sin)
    krf_ref[...] = kr
    kr_ref[...] = kr.astype(BF16)


def _odd_proj(x, norm_w, mods, w_in, q_norm, kv_norm, wq, wkv, rope, rows, scale, n_nope):
    n, d = x.shape
    tm = rows.tm
    tab = pl.BlockSpec((tm, LANES), lambda i: (rows.pos_block(i), 0))
    full = lambda a: pl.BlockSpec(a.shape, lambda i: (0, 0))
    qn = q_norm.reshape(1, -1).astype(F32)
    kvn = kv_norm.reshape(1, -1).astype(F32)
    nw = norm_w.reshape(1, d).astype(F32)
    outs = [(wq.shape[1], BF16), (wkv.shape[1], BF16), (LANES, BF16), (KV_LORA, F32), (LANES, F32)]
    return pl.pallas_call(
        functools.partial(_odd_proj_kernel, rows=rows, scale=scale, n_nope=n_nope), grid=(n // tm,),
        in_specs=[pl.BlockSpec((tm, d), lambda i: (i, 0)), full(nw), full(mods), full(w_in), full(qn), full(kvn),
                  full(wq), full(wkv), tab, tab],
        out_specs=[pl.BlockSpec((tm, c), lambda i: (i, 0)) for c, _ in outs],
        out_shape=[jax.ShapeDtypeStruct((n, c), dt) for c, dt in outs],
        compiler_params=_cparams("parallel"), name="odd_proj",
    )(x, nw, mods, w_in, qn, kvn, wq, wkv, *rope)


def _split2(x):
    hi = x.astype(BF16)
    return hi, (x - hi.astype(F32)).astype(BF16)


def _mm(a, b, precise, dot=_dot):
    if not precise:
        return dot(a.astype(BF16), b.astype(BF16))
    ah, al = _split2(a)
    bh, bl = _split2(b)
    return dot(ah, bh) + (dot(ah, bl) + dot(al, bh))


SOLVE_BLOCK = SUBLANES
MOE_ROW_BLOCK = 256
DELTA_HEADS_PER_STEP = 4


def _tri_solve_many(ms, rs, revs):
    c, width = rs[0].shape
    blk = SOLVE_BLOCK
    nb, gpb, ng = c // blk, blk // SUBLANES, c // SUBLANES
    col = lax.broadcasted_iota(jnp.int32, (blk, c), 1)
    mgs = [[m[SUBLANES * g:SUBLANES * (g + 1), :] for g in range(ng)] for m in ms]
    xgs = [[r[SUBLANES * g:SUBLANES * (g + 1), :] for g in range(ng)] for r in rs]
    zero_blk = jnp.zeros((blk, width), BF16)
    fin_h = [[zero_blk] * nb for _ in ms]
    fin_l = [[zero_blk] * nb for _ in ms]
    for bi in range(nb):
        blocks = [nb - 1 - bi if rev else bi for rev in revs]
        if bi > 0:
            for s, (rev, b) in enumerate(zip(revs, blocks)):
                done = (col >= (b + 1) * blk) if rev else (col < b * blk)
                lh, ll = _split2(jnp.where(done, ms[s][b * blk:(b + 1) * blk, :], 0.0))
                xh = jnp.concatenate(fin_h[s], axis=0)
                xl = jnp.concatenate(fin_l[s], axis=0)
                upd = _dot(lh, xh) + (_dot(lh, xl) + _dot(ll, xh))
                for gg in range(gpb):
                    g = b * gpb + gg
                    xgs[s][g] = xgs[s][g] - upd[SUBLANES * gg:SUBLANES * (gg + 1), :]
        for t in range(blk - 1):
            for s, (rev, b) in enumerate(zip(revs, blocks)):
                j = b * blk + (blk - 1 - t if rev else t)
                xj = xgs[s][j // SUBLANES][j % SUBLANES:j % SUBLANES + 1, :]
                groups = range(b * gpb, (j - 1) // SUBLANES + 1) if rev else range((j + 1) // SUBLANES, (b + 1) * gpb)
                for g in groups:
                    xgs[s][g] = xgs[s][g] - mgs[s][g][:, j:j + 1] * xj
        if bi < nb - 1:
            for s, b in enumerate(blocks):
                fin_h[s][b], fin_l[s][b] = _split2(jnp.concatenate(xgs[s][b * gpb:(b + 1) * gpb], axis=0))
    return [jnp.concatenate(xg, axis=0) for xg in xgs]


def _delta_kernel(qf, kf, vf, gcf, grf, qb, kb, vb, gcb, grb, s0_ref, *rest, chunk, hpb, n_alias, precise):
    of_ref, ob_ref, so_ref, s_scr = rest[n_alias:]
    h0 = pl.program_id(1) * hpb
    i = pl.program_id(2)
    n_i = pl.num_programs(2)

    @pl.when(i == 0)
    def _():
        s_scr[...] = s0_ref[...]

    rg = qf.shape[0]
    nc = rg // chunk
    rowi = lax.broadcasted_iota(jnp.int32, (rg, rg), 0)
    coli = lax.broadcasted_iota(jnp.int32, (rg, rg), 1)
    same = (rowi // chunk) == (coli // chunk)
    lane = lax.broadcasted_iota(jnp.int32, (1, LANES), 1)
    r_c = lax.broadcasted_iota(jnp.int32, (chunk, chunk), 0)
    c_c = lax.broadcasted_iota(jnp.int32, (chunk, chunk), 1)

    prep = []
    for d, (q_ref, k_ref, v_ref, gc_ref, gr_ref) in enumerate(((qf, kf, vf, gcf, grf), (qb, kb, vb, gcb, grb))):
        rev = d == 1
        incl_big = same & ((rowi <= coli) if rev else (rowi >= coli))
        lm = jnp.where(incl_big, 1.0, 0.0).astype(BF16)
        gcols = gc_ref[...]
        grows = gr_ref[...]
        gh, gm, gl_ = _split3(gcols)
        cum_c = _dot(lm, gh) + _dot(lm, gm) + _dot(lm, gl_)
        th, tm_, tl = _split3(grows)
        cum_r = _dot_nt(th, lm) + _dot_nt(tm_, lm) + _dot_nt(tl, lm)

        def col(a, ln):
            return jnp.sum(jnp.where(lane == ln, a, 0.0), axis=1, keepdims=True)

        sub = lax.broadcasted_iota(jnp.int32, (cum_r.shape[0], 1), 0)
        incl = (r_c <= c_c) if rev else (r_c >= c_c)
        strict = (r_c < c_c) if rev else (r_c > c_c)
        for hh in range(hpb):
            h = h0 + hh
            hs = slice(hh * LANES, (hh + 1) * LANES)
            gcum = col(cum_c, 4 * h + d)
            beta = col(gcols, 4 * h + 2 + d)
            grow = jnp.sum(jnp.where(sub == 4 * h + d, cum_r, 0.0), axis=0, keepdims=True)
            q = q_ref[:, hs]
            k = k_ref[:, hs]
            v = v_ref[:, hs]
            kbeta = k * beta
            eg = jnp.exp(gcum)
            rhs = jnp.concatenate([v * beta, kbeta * eg], axis=1)
            ms, rs, a_in = [], [], []
            for c in range(nc):
                sl = slice(c * chunk, (c + 1) * chunk)
                e = jnp.exp(jnp.where(incl, gcum[sl] - grow[:, sl], 0.0))
                kk = _mm(kbeta[sl], k[sl], precise, _dot_nt)
                qk = _mm(q[sl], k[sl], precise, _dot_nt)
                a_in.append(jnp.where(incl, qk * e, 0.0))
                ms.append(jnp.where(strict, kk * e, 0.0))
                rs.append(rhs[sl])
            prep.append((rev, hh, ms, rs, a_in, q * eg, k, gcum))

    all_terms = []
    half = len(prep) // 2
    for wave in (prep[:half], prep[half:]) if half else (prep,):
        sols = _tri_solve_many([m for p in wave for m in p[2]], [r for p in wave for r in p[3]],
                               [p[0] for p in wave for _ in range(nc)])
        for n_u, (rev, hh, _, _, a_in, qg, k, gcum) in enumerate(wave):
            xs = sols[n_u * nc:(n_u + 1) * nc]
            terms = []
            for c in range(nc):
                sl = slice(c * chunk, (c + 1) * chunk)
                r_last = c * chunk if rev else (c + 1) * chunk - 1
                g_last = gcum[r_last:r_last + 1, :]
                k_dec = k[sl] * jnp.exp(g_last - gcum[sl])
                a_uw = _mm(a_in[c], xs[c], precise)
                kd_uw = _mm(k_dec, xs[c], precise, _dot_tn)
                lhs = jnp.concatenate([qg[sl] - a_uw[:, DV_A:], kd_uw[:, DV_A:]], axis=0)
                terms.append((lhs, a_uw[:, :DV_A], kd_uw[:, :DV_A], jnp.exp(g_last)))
            all_terms.append(terms)

    states = [s_scr[1 if p[0] else 0, p[1]] for p in prep]
    outs = [[None] * nc for _ in prep]
    for t in range(nc):
        for n_u, p in enumerate(prep):
            c = nc - 1 - t if p[0] else t
            lhs, o_0, q_0, decay = all_terms[n_u][c]
            prod = _mm(lhs, states[n_u], precise)
            outs[n_u][c] = o_0 + prod[:chunk]
            states[n_u] = states[n_u] * decay - prod[chunk:] + q_0
    for n_u, (rev, hh) in enumerate((p[0], p[1]) for p in prep):
        s_scr[1 if rev else 0, hh] = states[n_u]
        (ob_ref if rev else of_ref)[:, hh * LANES:(hh + 1) * LANES] = jnp.concatenate(outs[n_u], axis=0)

    @pl.when(i == n_i - 1)
    def _():
        so_ref[...] = s_scr[...]


def _delta_stage(qkvn, gcol, grow, s0, n_seq, seq_len, row0, rg, into=None, precise=False):
    ng = seq_len // rg
    b0 = row0 // rg

    def fwd_blk(b, i):
        return b0 + b * ng + i

    def bwd_blk(b, i):
        return b0 + b * ng + (ng - 1 - i)

    hpb = DELTA_HEADS_PER_STEP
    nhp = H_A // hpb
    hw = hpb * LANES

    def specs(blk):
        return [pl.BlockSpec((rg, hw), lambda b, h, i: (blk(b, i), h)),
                pl.BlockSpec((rg, hw), lambda b, h, i: (blk(b, i), nhp + h)),
                pl.BlockSpec((rg, hw), lambda b, h, i: (blk(b, i), 2 * nhp + h)),
                pl.BlockSpec((rg, LANES), lambda b, h, i: (blk(b, i), 0)),
                pl.BlockSpec((4 * H_A, rg), lambda b, h, i: (0, blk(b, i)))]

    st_spec = pl.BlockSpec((None, 2, hpb, DK_A, DV_A), lambda b, h, i: (b, 0, h, 0, 0))
    args = [qkvn, qkvn, qkvn, gcol, grow, qkvn, qkvn, qkvn, gcol, grow, s0]
    in_specs = specs(fwd_blk) + specs(bwd_blk) + [st_spec]
    aliases = {}
    if into is not None:
        for k_out, arr in enumerate(into):
            aliases[len(args)] = k_out
            args.append(arr)
            in_specs.append(pl.BlockSpec(memory_space=pl.ANY))
    n = qkvn.shape[0]
    return pl.pallas_call(
        functools.partial(_delta_kernel, chunk=CHUNK, hpb=hpb, n_alias=len(aliases), precise=precise),
        grid=(n_seq, nhp, ng),
        in_specs=in_specs,
        out_specs=[pl.BlockSpec((rg, hw), lambda b, h, i: (fwd_blk(b, i), h)),
                   pl.BlockSpec((rg, hw), lambda b, h, i: (bwd_blk(b, i), h)),
                   st_spec],
        out_shape=[jax.ShapeDtypeStruct((n, H_A * DV_A), F32),
                   jax.ShapeDtypeStruct((n, H_A * DV_A), F32),
                   jax.ShapeDtypeStruct(s0.shape, F32)],
        input_output_aliases=aliases,
        scratch_shapes=[pltpu.VMEM((2, hpb, DK_A, DV_A), F32)],
        compiler_params=_cparams("parallel", "parallel", "arbitrary"), name="gated_delta",
    )(*args)


def _flash_kernel(*refs, n_parts, has_cache, diff, tk, lam_init, aliased, head_lanes):
    it = iter(refs)
    q_refs = [next(it) for _ in range(n_parts)]
    k_refs = [next(it) for _ in range(n_parts)]
    v_ref = next(it)
    kc_refs = [next(it) for _ in range(n_parts)] if has_cache else []
    vc_ref = next(it) if has_cache else None
    lam_ref = next(it) if diff else None
    if aliased:
        next(it)
    o_ref = next(it)
    tq = q_refs[0].shape[0]
    lk = k_refs[0].shape[0]
    if diff:
        lp = lam_ref[...]
        lam = (jnp.exp(jnp.sum(lp[0:1] * lp[1:2], axis=-1, keepdims=True))
               - jnp.exp(jnp.sum(lp[2:3] * lp[3:4], axis=-1, keepdims=True)) + lam_init)
        lane = lax.broadcasted_iota(jnp.int32, (1, LANES), 1)

    def cols(ref, off, rows=None):
        return ref[:, off:off + LANES] if rows is None else ref[rows, off:off + LANES]

    for q_offs, k_offs, v_off, o_off in head_lanes:
        qs = [cols(r, off) for r, off in zip(q_refs, q_offs)]
        if diff:
            q = qs[0]
            qs = [jnp.concatenate([jnp.where(lane < DQK_B, q, 0), jnp.where(lane >= DQK_B, q, 0)], axis=0)]
        nrow = qs[0].shape[0]
        q_cat = qs[0] if n_parts == 1 else jnp.concatenate(qs, axis=1)

        def block(carry, ks, vv, q_cat=q_cat):
            m, l, acc = carry
            s = _dot_nt(q_cat, ks[0] if n_parts == 1 else jnp.concatenate(ks, axis=1))
            m_new = jnp.maximum(m, jnp.max(s, axis=-1, keepdims=True))
            alpha = jnp.exp2(m - m_new)
            p_ = jnp.exp2(s - m_new)
            l = alpha * l + jnp.sum(p_, axis=-1, keepdims=True)
            acc = alpha * acc + _dot(p_.astype(BF16), vv)
            return m_new, l, acc

        carry = (jnp.full((nrow, 1), -1e30, F32), jnp.zeros((nrow, 1), F32), jnp.zeros((nrow, LANES), F32))

        def body(t, carry, block=block, k_offs=k_offs, v_off=v_off):
            rows = pl.ds(pl.multiple_of(t * tk, tk), tk)
            return block(carry, [cols(r, off, rows) for r, off in zip(k_refs, k_offs)], cols(v_ref, v_off, rows))

        carry = lax.fori_loop(0, lk // tk, body, carry, unroll=True)
        if has_cache:
            carry = block(carry, [r[...] for r in kc_refs], vc_ref[...])
        _, l, acc = carry
        o = acc / l
        if diff:
            o = o[:tq] - lam * o[tq:]
        o_ref[:, o_off:o_off + LANES] = o.astype(o_ref.dtype)


def _flash(q_parts, k_parts, v_part, cache, lam, *, n_seq, n_heads, lq, lk, q_row0, k_row0, tq, tk, diff,
           lam_init, out_cols, into=None, all_heads=False):
    nqt = lq // tq
    qb0, kb0 = q_row0 // tq, k_row0 // lk
    args, in_specs = [], []
    if all_heads:
        assert cache is None
        for arr, _ in q_parts:
            args.append(arr)
            in_specs.append(pl.BlockSpec((tq, arr.shape[1]), lambda b, h, t: (qb0 + b * nqt + t, 0)))
        for arr, _ in k_parts + [v_part]:
            args.append(arr)
            in_specs.append(pl.BlockSpec((lk, arr.shape[1]), lambda b, h, t: (kb0 + b, 0)))
        head_lanes = tuple((tuple(cf(h) * LANES for _, cf in q_parts), tuple(cf(h) * LANES for _, cf in k_parts),
                            v_part[1](h) * LANES, h * LANES) for h in range(n_heads))
        grid = (n_seq, 1, nqt)
        out_spec = pl.BlockSpec((tq, out_cols), lambda b, h, t: (qb0 + b * nqt + t, 0))
    else:
        for arr, cf in q_parts:
            args.append(arr)
            in_specs.append(pl.BlockSpec((tq, LANES), lambda b, h, t, cf=cf: (qb0 + b * nqt + t, cf(h))))
        for arr, cf in k_parts + [v_part]:
            args.append(arr)
            in_specs.append(pl.BlockSpec((lk, LANES), lambda b, h, t, cf=cf: (kb0 + b, cf(h))))
        head_lanes = (((0,) * len(q_parts), (0,) * len(k_parts), 0, 0),)
        grid = (n_seq, n_heads, nqt)
        out_spec = pl.BlockSpec((tq, LANES), lambda b, h, t: (qb0 + b * nqt + t, h))
    if cache is not None:
        c_k, c_v, c_rows, c_blk = cache
        for arr, cf in c_k + [c_v]:
            args.append(arr)
            in_specs.append(pl.BlockSpec((c_rows, LANES), lambda b, h, t, cf=cf: (c_blk(b, h), cf(h))))
    if diff:
        args.append(lam)
        in_specs.append(pl.BlockSpec(lam.shape, lambda b, h, t: (0, 0)))
    aliases = {}
    if into is not None:
        aliases[len(args)] = 0
        args.append(into)
        in_specs.append(pl.BlockSpec(memory_space=pl.ANY))
    kern = functools.partial(_flash_kernel, n_parts=len(q_parts), has_cache=cache is not None, diff=diff, tk=tk,
                             lam_init=lam_init, aliased=into is not None, head_lanes=head_lanes)
    return pl.pallas_call(
        kern, grid=grid, in_specs=in_specs, out_specs=out_spec,
        out_shape=jax.ShapeDtypeStruct((q_parts[0][0].shape[0], out_cols), F32 if diff else BF16),
        input_output_aliases=aliases,
        compiler_params=_cparams("parallel", "parallel", "arbitrary"), name="flash_attention",
    )(*args)


def _mr_kernel(x_ref, a_ref, w_ref, mods_ref, o_ref, *, rows, gate_idx):
    i = pl.program_id(0)
    d = x_ref.shape[1]
    gate = _mod_chunk(mods_ref, rows.group(i), gate_idx, d)
    o_ref[...] = x_ref[...] + gate * _dot(a_ref[...].astype(BF16), w_ref[...])


def _matmul_residual(x, a, w, mods, rows, gate_idx):
    n, d = x.shape
    tm = rows.tm
    return pl.pallas_call(
        functools.partial(_mr_kernel, rows=rows, gate_idx=gate_idx), grid=(n // tm,),
        in_specs=[pl.BlockSpec((tm, d), lambda i: (i, 0)),
                  pl.BlockSpec((tm, a.shape[1]), lambda i: (i, 0)),
                  pl.BlockSpec(w.shape, lambda i: (0, 0)),
                  pl.BlockSpec(mods.shape, lambda i: (0, 0))],
        out_specs=pl.BlockSpec((tm, d), lambda i: (i, 0)),
        out_shape=jax.ShapeDtypeStruct((n, d), F32),
        compiler_params=_cparams("parallel"), name="matmul_residual",
    )(x, a, w, mods)


def _modulated(x_ref, nw_ref, mods_ref, g, idx, d):
    x = _rmsnorm_rows(x_ref[...], nw_ref[...])
    return x * (1.0 + _mod_chunk(mods_ref, g, idx + 1, d)) + _mod_chunk(mods_ref, g, idx, d)


def _ffn_kernel(x_ref, nw_ref, mods_ref, wg_ref, wu_ref, wd_ref, of_ref, ob_ref, z_ref, od_ref, on_ref, sn_ref,
                wo_ref, o_ref, h_scr, acc_scr, xn_scr, *, rows, mod0, mix_gate, lam_init):
    i, f = pl.program_id(0), pl.program_id(1)
    d = x_ref.shape[1]
    g = rows.group(i)

    @pl.when(f == 0)
    def _():
        oa = of_ref[...] + ob_ref[...]
        z = z_ref[...]
        od = od_ref[...]
        ya, yb = [], []
        for hh in range(H_A):
            sl = slice(hh * LANES, (hh + 1) * LANES)
            ya.append(_rmsnorm_rows(oa[:, sl], on_ref[...]) * _silu(z[:, sl]))
        for hh in range(H_B):
            sl = slice(hh * LANES, (hh + 1) * LANES)
            yb.append(_rmsnorm_rows(od[:, sl], sn_ref[...]) * (1.0 - lam_init))
        y = jnp.concatenate(ya + yb, axis=1).astype(BF16)
        x1 = x_ref[...] + _mod_chunk(mods_ref, g, mix_gate, d) * _dot(y, wo_ref[...])
        xn_scr[...] = x1
        v = _rmsnorm_rows(x1, nw_ref[...])
        h_scr[...] = (v * (1.0 + _mod_chunk(mods_ref, g, mod0 + 1, d)) + _mod_chunk(mods_ref, g, mod0, d)).astype(BF16)
        acc_scr[...] = jnp.zeros_like(acc_scr)

    h = h_scr[...]
    act = _silu(_dot(h, wg_ref[...])) * _dot(h, wu_ref[...])
    acc_scr[...] += _dot(act.astype(BF16), wd_ref[...])

    @pl.when(f == pl.num_programs(1) - 1)
    def _():
        o_ref[...] = xn_scr[...] + _mod_chunk(mods_ref, g, mod0 + 2, d) * acc_scr[...]


def _even_tail(x, o_f, o_b, z, o_d, onorm, subln, w_out, norm_w, mods, w_gu, w_down, rows, mix_gate, mod0, tf,
               lam_init):
    n, d = x.shape
    tm = rows.tm
    ff = w_down.shape[0]
    nf = ff // tf
    hw = H_A * DV_A
    row = lambda c: pl.BlockSpec((tm, c), lambda i, f: (i, 0))
    one = lambda c: pl.BlockSpec((1, c), lambda i, f: (0, 0))
    return pl.pallas_call(
        functools.partial(_ffn_kernel, rows=rows, mod0=mod0, mix_gate=mix_gate, lam_init=lam_init),
        grid=(n // tm, nf),
        in_specs=[row(d), one(d), pl.BlockSpec(mods.shape, lambda i, f: (0, 0)),
                  pl.BlockSpec((d, tf), lambda i, f: (0, f)),
                  pl.BlockSpec((d, tf), lambda i, f: (0, nf + f)),
                  pl.BlockSpec((tf, d), lambda i, f: (f, 0)),
                  row(hw), row(hw), row(hw), row(hw), one(LANES), one(LANES),
                  pl.BlockSpec(w_out.shape, lambda i, f: (0, 0))],
        out_specs=row(d),
        out_shape=jax.ShapeDtypeStruct((n, d), F32),
        scratch_shapes=[pltpu.VMEM((tm, d), BF16), pltpu.VMEM((tm, d), F32), pltpu.VMEM((tm, d), F32)],
        compiler_params=_cparams("parallel", "arbitrary"), name="even_tail",
    )(x, norm_w.reshape(1, d).astype(F32), mods, w_gu, w_gu, w_down, o_f, o_b, z, o_d,
      onorm.reshape(1, LANES).astype(F32), subln.reshape(1, LANES).astype(F32), w_out)


def _moe_kernel(x_ref, nw_ref, mods_ref, rw_ref, wgu_ref, wd_ref, *rest, rows, mod0, n_exp, sb, final):
    if final:
        fw_ref, o_ctx_ref, o_lat_ref, h_scr, g_scr, rk_scr, rkt_scr, acc_scr = rest
    else:
        o_ref, h_scr, g_scr, rk_scr, rkt_scr, acc_scr = rest
    i, e = pl.program_id(0), pl.program_id(1)
    tm, d = x_ref.shape
    g = rows.group(i)
    lane = lax.broadcasted_iota(jnp.int32, (1, LANES), 1)

    @pl.when(e == 0)
    def _():
        h = _modulated(x_ref, nw_ref, mods_ref, g, mod0, d)
        h_scr[...] = h.astype(BF16)
        acc_scr[...] = jnp.zeros_like(acc_scr)
        logits = jnp.where(lane < n_exp, _mm(h, rw_ref[...], True), -1e30)
        pe = jnp.exp(logits - jnp.max(logits, axis=-1, keepdims=True))
        probs = pe / jnp.sum(pe, axis=-1, keepdims=True)
        p1 = jnp.max(probs, axis=-1, keepdims=True)
        i1 = jnp.min(jnp.where(probs == p1, lane, LANES), axis=-1, keepdims=True)
        rest = jnp.where(lane == i1, -1.0, probs)
        p2 = jnp.max(rest, axis=-1, keepdims=True)
        i2 = jnp.min(jnp.where(rest == p2, lane, LANES), axis=-1, keepdims=True)
        den = p1 + p2
        gates = jnp.where(lane == i1, p1 / den, 0.0) + jnp.where(lane == i2, p2 / den, 0.0)
        g_scr[...] = gates
        t_r = lax.broadcasted_iota(jnp.int32, (tm, tm), 0)
        t_c = lax.broadcasted_iota(jnp.int32, (tm, tm), 1)
        earlier = jnp.where(t_r > t_c, 1.0, 0.0).astype(BF16)
        sel = gates > 0.0
        rank = jnp.where(sel, _dot(earlier, jnp.where(sel, 1.0, 0.0).astype(BF16)), -1.0)
        rk_scr[...] = rank
        rkt_scr[...] = jnp.transpose(rank)[:rkt_scr.shape[0], :]

    ff = wd_ref.shape[0]
    g_e = jnp.sum(jnp.where(lane == e, g_scr[...], 0.0), axis=-1, keepdims=True)
    rk_col = jnp.sum(jnp.where(lane == e, rk_scr[...], 0.0), axis=-1, keepdims=True)
    rk_row = rkt_scr[pl.ds(e, 1), :]
    n_rows = (jnp.max(rk_row) + 1.0).astype(jnp.int32)
    g_hi = g_e.astype(BF16).astype(F32)
    g2 = jnp.where(lane == 0, g_hi, jnp.where(lane == 1, g_e - g_hi, 0.0)).astype(BF16)
    def expert_rows(slot0, nr):
        base = slot0.astype(F32)
        r_sub = lax.broadcasted_iota(jnp.int32, (nr, 1), 0).astype(F32)
        r_lane = lax.broadcasted_iota(jnp.int32, (1, nr), 1).astype(F32)
        pick = jnp.where(rk_row == r_sub + base, 1.0, 0.0).astype(BF16)
        xg = _dot(pick, h_scr[...]).astype(BF16)
        gr = _dot(pick, g2)
        gate_r = gr[:, 0:1] + gr[:, 1:2]
        gu = _dot(xg, wgu_ref[...])
        act = _silu(gu[:, :ff]) * gu[:, ff:]
        y = _dot(act.astype(BF16), wd_ref[...])
        put = jnp.where(rk_col == r_lane + base, 1.0, 0.0).astype(BF16)
        acc_scr[...] += _dot(put, (y * gate_r).astype(BF16))

    half, quarter = sb // 2, sb // 4
    rem = n_rows % sb
    n_full = n_rows // sb + jnp.where(rem > half, 1, 0)

    def full_body(jb, carry):
        expert_rows(jb * sb, sb)
        return carry

    lax.fori_loop(0, n_full, full_body, 0)

    @pl.when((rem > quarter) & (rem <= half))
    def _():
        expert_rows(n_full * sb, half)

    @pl.when((rem > 0) & (rem <= quarter))
    def _():
        expert_rows(n_full * sb, quarter)

    def result():
        return x_ref[...] + _mod_chunk(mods_ref, g, mod0 + 2, d) * acc_scr[...]

    last = e == n_exp - 1
    if final:
        @pl.when(last & (i < rows.ctx_tiles))
        def _():
            o_ctx_ref[...] = _rmsnorm_rows(result(), fw_ref[...])

        @pl.when(last & (i >= rows.ctx_tiles))
        def _():
            o_lat_ref[...] = _rmsnorm_rows(result(), fw_ref[...])
    else:
        @pl.when(last)
        def _():
            o_ref[...] = result()


def _moe(x, norm_w, mods, router_w, w_gu, w_down, rows, mod0, final_w=None):
    n, d = x.shape
    tm = rows.tm
    n_exp, _, ff2 = w_gu.shape
    rw = jnp.zeros((d, LANES), F32).at[:, :n_exp].set(router_w.astype(F32))
    final = final_w is not None
    args = [x, norm_w.reshape(1, d).astype(F32), mods, rw, w_gu, w_down]
    in_specs = [pl.BlockSpec((tm, d), lambda i, e: (i, 0)),
                pl.BlockSpec((1, d), lambda i, e: (0, 0)),
                pl.BlockSpec(mods.shape, lambda i, e: (0, 0)),
                pl.BlockSpec((d, LANES), lambda i, e: (0, 0)),
                pl.BlockSpec((None, d, ff2), lambda i, e: (e, 0, 0)),
                pl.BlockSpec((None, ff2 // 2, d), lambda i, e: (e, 0, 0))]
    if final:
        args.append(final_w.reshape(1, d).astype(F32))
        in_specs.append(pl.BlockSpec((1, d), lambda i, e: (0, 0)))
        ct = rows.ctx_tiles
        out_specs = [pl.BlockSpec((tm, d), lambda i, e: (jnp.minimum(i, ct - 1), 0), pipeline_mode=pl.Buffered(1)),
                     pl.BlockSpec((tm, d), lambda i, e: (jnp.maximum(i - ct, 0), 0), pipeline_mode=pl.Buffered(1))]
        out_shape = [jax.ShapeDtypeStruct((ct * tm, d), F32), jax.ShapeDtypeStruct((n - ct * tm, d), F32)]
    else:
        out_specs = pl.BlockSpec((tm, d), lambda i, e: (i, 0))
        out_shape = jax.ShapeDtypeStruct((n, d), F32)
    return pl.pallas_call(
        functools.partial(_moe_kernel, rows=rows, mod0=mod0, n_exp=n_exp, sb=MOE_ROW_BLOCK, final=final),
        grid=(n // tm, n_exp), in_specs=in_specs, out_specs=out_specs, out_shape=out_shape,
        scratch_shapes=[pltpu.VMEM((tm, d), BF16), pltpu.VMEM((tm, LANES), F32), pltpu.VMEM((tm, LANES), F32),
                        pltpu.VMEM((SUBLANES * ((n_exp + SUBLANES - 1) // SUBLANES), tm), F32),
                        pltpu.VMEM((tm, d), F32)],
        compiler_params=_cparams("arbitrary", "arbitrary"), name="moe",
    )(*args)


def _final_kernel(x_ref, w_ref, o_ref):
    o_ref[...] = _rmsnorm_rows(x_ref[...], w_ref[...])


def _final_norm(x, w, tm):
    n, d = x.shape
    return pl.pallas_call(
        _final_kernel, grid=(n // tm,),
        in_specs=[pl.BlockSpec((tm, d), lambda i: (i, 0)), pl.BlockSpec((1, d), lambda i: (0, 0))],
        out_specs=pl.BlockSpec((tm, d), lambda i: (i, 0)),
        out_shape=jax.ShapeDtypeStruct((n, d), F32),
        compiler_params=_cparams("parallel"), name="final_norm",
    )(x, w.reshape(1, d).astype(F32))


def kernel(x_prompt, x_sample, state_delta, cache_diff_k, cache_diff_v, cache_mla_ckv, cache_mla_krope, c, c_ctx,
           mod_w, mod_b, norm_mix, norm_ffn, final_norm, ev_w_in, ev_conv_w, ev_a_log, ev_dt_bias, ev_onorm,
           ev_lambda, ev_subln, ev_w_out, ffn_w_gu, ffn_w_down, od_w_in, od_q_norm, od_kv_norm, od_w_uq, od_w_ukv,
           od_w_out, moe_router, moe_w_gu, moe_w_down):
    bp, lp, d = x_prompt.shape
    bs, ls, _ = x_sample.shape
    past = cache_diff_k.shape[3]
    depth = mod_w.shape[0]
    n_p, n_s = bp * lp, bs * ls
    n = n_p + n_s
    assert n_p % ls == 0 and past == lp

    rows = _Rows(n_p, n_s, ls, _tile(math.gcd(n_p, ls), 512))
    rows_m = _Rows(n_p, n_s, ls, _tile(math.gcd(n_p, ls), 1024))
    rg = _tile(math.gcd(lp, ls), 256)
    cos_t, sin_t = _rope_tables(rows, ls)

    x = jnp.concatenate([x_prompt.reshape(n_p, d), x_sample.reshape(n_s, d)], axis=0)
    n_grp = 1 + bs
    cc = jnp.zeros((2 * SUBLANES * ((n_grp + 15) // 16), d), F32).at[0].set(c_ctx).at[1:n_grp].set(c)
    mods_all = _mods(cc, mod_w, mod_b)

    hq = H_A * DK_A
    hb2 = H_B * 2 * DQK_B
    sm0 = 4 * hq
    qb0 = sm0 + 4 * H_A
    q_scale = (DQK_B ** -0.5) * LOG2E
    mla_scale = ((NOPE_C + ROPE_C) ** -0.5) * LOG2E
    sd_new, dk_new, dv_new, ckv_new, kr_new = [], [], [], [], []

    for layer in range(depth):
        j = layer // 2
        mods = mods_all[layer]
        if layer % 2 == 0:
            lam_init = 0.8 - 0.6 * math.exp(-0.3 * layer)
            w = ev_w_in[j]
            sm = w[:, sm0:qb0].reshape(d, 4, H_A).transpose(0, 2, 1).reshape(d, 4 * H_A)
            w_in = jnp.concatenate([w[:, :sm0], sm, jnp.zeros((d, LANES - 4 * H_A), w.dtype), w[:, qb0:]],
                                   axis=1).astype(BF16)
            qkvn, z_gate, gcol, grow, att, kv_f = _even_proj(
                x, norm_mix[layer], mods, w_in, ev_conv_w[j], ev_a_log[j], ev_dt_bias[j], (cos_t, sin_t), rows,
                3 * hq, hq, hb2, q_scale, lp, ls, n_p)
            o_f, o_b, s_p = _delta_stage(qkvn, gcol, grow, jnp.zeros((bp, 2, H_A, DK_A, DV_A), F32), bp, lp, 0, rg,
                                         precise=True)
            o_f, o_b, _ = _delta_stage(qkvn, gcol, grow, state_delta[:, j].astype(F32), bs, ls, n_p, rg,
                                       into=(o_f, o_b))
            sd_new.append(s_p)
            lam_p = ev_lambda[j].astype(F32)
            nb = hb2 // LANES
            qp, kp, vp = [(att, lambda h: h)], [(att, lambda h: nb + h)], (att, lambda h: 2 * nb + h)
            o_d = _flash(qp, kp, vp, None, lam_p, n_seq=bp, n_heads=H_B, lq=lp, lk=lp, q_row0=0, k_row0=0,
                         tq=_tile(lp, 256), tk=_tile(lp, 512), diff=True, lam_init=lam_init, out_cols=hb2,
                         all_heads=True)
            ck = cache_diff_k[:, j].reshape(bs * H_B * past, 2 * DQK_B).astype(BF16)
            cv = cache_diff_v[:, j].reshape(bs * H_B * past, DV_B).astype(BF16)
            cache = ([(ck, lambda h: 0)], (cv, lambda h: 0), past, lambda b, h: b * H_B + h)
            o_d = _flash(qp, kp, vp, cache, lam_p, n_seq=bs, n_heads=H_B, lq=ls, lk=ls, q_row0=n_p, k_row0=n_p,
                         tq=_tile(ls, 512), tk=_tile(ls, 2048), diff=True, lam_init=lam_init, out_cols=hb2, into=o_d)
            dk_new.append(kv_f[:n_p, :hb2].reshape(bp, lp, H_B, 2 * DQK_B).transpose(0, 2, 1, 3))
            dv_new.append(kv_f[:n_p, hb2:].reshape(bp, lp, H_B, DV_B).transpose(0, 2, 1, 3))
            ff = ffn_w_down.shape[1]
            x = _even_tail(x, o_f, o_b, z_gate, o_d, ev_onorm[j], ev_subln[j], ev_w_out[j].astype(BF16),
                           norm_ffn[layer], mods, ffn_w_gu[j].astype(BF16), ffn_w_down[j].astype(BF16), rows, 2, 3,
                           ff // 2 if (ff // 2) % LANES == 0 else ff, lam_init)
        else:
            n_in = Q_LORA + KV_LORA + ROPE_C
            pad = (-n_in) % LANES
            w_in = jnp.concatenate([od_w_in[j], jnp.zeros((d, pad), od_w_in.dtype)], axis=1).astype(BF16)
            wq = od_w_uq[j].reshape(Q_LORA, H_C, NOPE_C + ROPE_C)
            wq_rope = jnp.concatenate([wq[:, :, NOPE_C:], jnp.zeros((Q_LORA, H_C, LANES - ROPE_C), wq.dtype)], axis=2)
            wq2 = jnp.concatenate([wq[:, :, :NOPE_C].reshape(Q_LORA, H_C * NOPE_C),
                                   wq_rope.reshape(Q_LORA, H_C * LANES)], axis=1).astype(BF16)
            q_all, kv_tok, kr_tok, ckv_n, kr_f = _odd_proj(
                x, norm_mix[layer], mods, w_in, od_q_norm[j], od_kv_norm[j], wq2, od_w_ukv[j].astype(BF16),
                (cos_t, sin_t), rows, mla_scale, H_C * NOPE_C)
            ckv_new.append(ckv_n[:n_p].reshape(bp, lp, KV_LORA))
            kr_new.append(kr_f[:n_p, :ROPE_C].reshape(bp, lp, ROPE_C))
            rows_k = _Rows(bs * past, 0, past, _tile(past, 256))
            kv_c = _norm_matmul(cache_mla_ckv[:, j].reshape(bs * past, KV_LORA), od_w_ukv[j].astype(BF16), rows_k,
                                tn=512, out_dtype=BF16)
            kr_c = jnp.concatenate([cache_mla_krope[:, j].reshape(bs * past, ROPE_C),
                                    jnp.zeros((bs * past, LANES - ROPE_C), F32)], axis=1).astype(BF16)
            qp = [(q_all, lambda h: h), (q_all, lambda h: H_C + h)]
            kp = [(kv_tok, lambda h: 2 * h), (kr_tok, lambda h: 0)]
            vp = (kv_tok, lambda h: 2 * h + 1)
            o = _flash(qp, kp, vp, None, None, n_seq=bp, n_heads=H_C, lq=lp, lk=lp, q_row0=0, k_row0=0,
                       tq=_tile(lp, 256), tk=_tile(lp, 512), diff=False, lam_init=0.0, out_cols=H_C * V_C,
                       all_heads=True)
            cache = ([(kv_c, lambda h: 2 * h), (kr_c, lambda h: 0)], (kv_c, lambda h: 2 * h + 1), past,
                     lambda b, h: b)
            o = _flash(qp, kp, vp, cache, None, n_seq=bs, n_heads=H_C, lq=ls, lk=ls, q_row0=n_p, k_row0=n_p,
                       tq=_tile(ls, 1024), tk=_tile(ls, 2048), diff=False, lam_init=0.0, out_cols=H_C * V_C, into=o)
            x = _matmul_residual(x, o, od_w_out[j].astype(BF16), mods, rows, 2)
            x = _moe(x, norm_ffn[layer], mods, moe_router[j], moe_w_gu[j].astype(BF16), moe_w_down[j].astype(BF16),
                     rows_m, 3, final_w=final_norm if layer == depth - 1 else None)

    if depth % 2 == 0:
        y_p, y_s = x
    else:
        y = _final_norm(x, final_norm, rows.tm)
        y_p, y_s = y[:n_p], y[n_p:]
    return (y_p.reshape(bp, lp, d), y_s.reshape(bs, ls, d), jnp.stack(sd_new, axis=1),
            jnp.stack(dk_new, axis=1), jnp.stack(dv_new, axis=1), jnp.stack(ckv_new, axis=1),
            jnp.stack(kr_new, axis=1))
```

```python
import functools
import math

import numpy as np
import jax
import jax.numpy as jnp
from jax import lax
from jax.experimental import pallas as pl
from jax.experimental.pallas import tpu as pltpu

F32 = jnp.float32
BF16 = jnp.bfloat16

EPS = 1e-6
LOG2E = 1.4426950408889634
GRID_W = 64
ROPE_BASE = 10000.0
H_A, DK_A, DV_A = 4, 128, 128
CONV_K = 3
CHUNK = 64
H_B, DQK_B, DV_B = 4, 64, 128
H_C, NOPE_C, ROPE_C, V_C = 8, 128, 64, 128
Q_LORA, KV_LORA = 384, 256
LANES = 128
SUBLANES = 8
VMEM_LIMIT_BYTES = 56 * 1024 * 1024


def _cparams(*sem):
    return pltpu.CompilerParams(dimension_semantics=sem, vmem_limit_bytes=VMEM_LIMIT_BYTES)


def _dot(a, b):
    return jnp.dot(a, b, preferred_element_type=F32)


def _dot_nt(a, b):
    return lax.dot_general(a, b, (((1,), (1,)), ((), ())), preferred_element_type=F32)


def _dot_tn(a, b):
    return lax.dot_general(a, b, (((0,), (0,)), ((), ())), preferred_element_type=F32)


def _split3(x):
    hi = x.astype(BF16)
    r = x - hi.astype(F32)
    mid = r.astype(BF16)
    lo = (r - mid.astype(F32)).astype(BF16)
    return hi, mid, lo


def _dot_f32(a, b):
    ah, am, al = _split3(a)
    bh, bm, bl = _split3(b)
    return (_dot(ah, bh) + (_dot(ah, bm) + _dot(am, bh))
            + (_dot(am, bm) + _dot(ah, bl) + _dot(al, bh)))


def _silu(x):
    return x * jax.nn.sigmoid(x)


def _tile(n, pref, mult=SUBLANES):
    t = min(n, pref)
    while t > mult and (n % t or t % mult):
        t -= mult
    assert n % t == 0, (n, pref)
    return t


class _Rows:
    def __init__(self, n_ctx, n_lat, lat_len, tm):
        assert n_ctx % tm == 0 and lat_len % tm == 0
        self.tm = tm
        self.n = n_ctx + n_lat
        self.ctx_tiles = n_ctx // tm
        self.seq_tiles = lat_len // tm

    def group(self, i):
        return jnp.where(i < self.ctx_tiles, 0, 1 + jnp.maximum(i - self.ctx_tiles, 0) // self.seq_tiles)

    def pos_block(self, i):
        return jnp.where(i < self.ctx_tiles, 0, 1 + jnp.maximum(i - self.ctx_tiles, 0) % self.seq_tiles)


def _mod_kernel(c_ref, w_ref, b_ref, o_ref):
    o_ref[...] = _dot_f32(_silu(c_ref[...]), w_ref[...]) + b_ref[...]


def _mods(cc, mod_w, mod_b):
    depth, d, n6 = mod_w.shape
    tn = _tile(n6, 512, LANES)
    return pl.pallas_call(
        _mod_kernel,
        grid=(depth, n6 // tn),
        in_specs=[pl.BlockSpec(cc.shape, lambda l, j: (0, 0)),
                  pl.BlockSpec((None, d, tn), lambda l, j: (l, 0, j)),
                  pl.BlockSpec((None, 1, tn), lambda l, j: (l, 0, j))],
        out_specs=pl.BlockSpec((None, cc.shape[0], tn), lambda l, j: (l, 0, j)),
        out_shape=jax.ShapeDtypeStruct((depth, cc.shape[0], n6), F32),
        compiler_params=_cparams("parallel", "parallel"),
        name="mods",
    )(cc, mod_w, mod_b.reshape(depth, 1, n6))


def _mod_chunk(mods_ref, g, idx, d):
    return mods_ref[pl.ds(g, 1), idx * d:(idx + 1) * d]


def _rmsnorm_rows(x, w):
    ms = jnp.mean(x * x, axis=-1, keepdims=True)
    return x * lax.rsqrt(ms + EPS) * w


def _rope(x, cos, sin_signed):
    lane = lax.broadcasted_iota(jnp.int32, (1, LANES), 1)
    low = (lane % 32) < 16
    outs = []
    for c in range(x.shape[1] // LANES):
        xc = x[:, c * LANES:(c + 1) * LANES]
        fwd = pltpu.roll(xc, LANES - 16, axis=1)
        bwd = pltpu.roll(xc, 16, axis=1)
        outs.append(xc * cos + jnp.where(low, fwd, bwd) * sin_signed)
    return outs[0] if len(outs) == 1 else jnp.concatenate(outs, axis=1)


def _rope_tables(rows, lat_len):
    pos = jnp.arange(lat_len)
    r = (pos // GRID_W).astype(F32)
    c = (pos % GRID_W).astype(F32)
    inv = ROPE_BASE ** (-jnp.arange(0, 32, 2, dtype=F32) / 32)
    ang_r = r[:, None] * inv[None, :]
    ang_c = c[:, None] * inv[None, :]
    ang = jnp.concatenate([ang_r, ang_r, ang_c, ang_c] * 2, axis=-1)
    sign = jnp.where((jnp.arange(LANES) % 32) < 16, -1.0, 1.0).astype(F32)
    cos = jnp.concatenate([jnp.ones((rows.tm, LANES), F32), jnp.cos(ang)], axis=0)
    sin = jnp.concatenate([jnp.zeros((rows.tm, LANES), F32), jnp.sin(ang) * sign[None, :]], axis=0)
    return cos, sin


def _nm_kernel(*refs, rows, d_mod, has_norm, mod_idx, has_rope, epi, emit_h):
    it = iter(refs)
    x_ref = next(it)
    w_ref = next(it)
    nw_ref = next(it) if has_norm else None
    mods_ref = next(it) if mod_idx is not None else None
    cos_ref, sin_ref = (next(it), next(it)) if has_rope else (None, None)
    o_ref = next(it)
    h_out_ref = next(it) if emit_h else None
    h_scr = next(it)
    i, j = pl.program_id(0), pl.program_id(1)

    @pl.when(j == 0)
    def _():
        x = x_ref[...].astype(F32)
        if has_norm:
            x = _rmsnorm_rows(x, nw_ref[...])
        if mod_idx is not None:
            g = rows.group(i)
            x = x * (1.0 + _mod_chunk(mods_ref, g, mod_idx[1], d_mod)) + _mod_chunk(mods_ref, g, mod_idx[0], d_mod)
        h_scr[...] = x.astype(BF16)
        if emit_h:
            h_out_ref[...] = x

    acc = _dot(h_scr[...], w_ref[...])
    kinds = sorted(set(epi))
    for kind in kinds:
        cond = None
        for jj, e in enumerate(epi):
            if e == kind:
                cond = (j == jj) if cond is None else (cond | (j == jj))

        def _store(kind=kind):
            y = acc
            if kind[0]:
                y = _rope(y, cos_ref[...], sin_ref[...])
            if kind[1] != 1.0:
                y = y * kind[1]
            o_ref[...] = y.astype(o_ref.dtype)

        if len(kinds) == 1:
            _store()
        else:
            pl.when(cond)(_store)


def _norm_matmul(x, w, rows, *, tn, norm_w=None, mods=None, mod_idx=None, rope=None, epi=None,
                 emit_h=False, out_dtype=F32):
    n, k = x.shape
    nout = w.shape[1]
    nj = nout // tn
    tm = rows.tm
    epi = tuple(epi) if epi is not None else ((False, 1.0),) * nj
    assert len(epi) == nj and n % tm == 0
    has_rope = any(e[0] for e in epi)
    args = [x, w]
    in_specs = [pl.BlockSpec((tm, k), lambda i, j: (i, 0)), pl.BlockSpec((k, tn), lambda i, j: (0, j))]
    if norm_w is not None:
        args.append(norm_w.reshape(1, k).astype(F32))
        in_specs.append(pl.BlockSpec((1, k), lambda i, j: (0, 0)))
    if mod_idx is not None:
        args.append(mods)
        in_specs.append(pl.BlockSpec(mods.shape, lambda i, j: (0, 0)))
    if has_rope:
        args += list(rope)
        in_specs += [pl.BlockSpec((tm, LANES), lambda i, j: (rows.pos_block(i), 0))] * 2
    out_shape = [jax.ShapeDtypeStruct((n, nout), out_dtype)]
    out_specs = [pl.BlockSpec((tm, tn), lambda i, j: (i, j))]
    if emit_h:
        out_shape.append(jax.ShapeDtypeStruct((n, k), F32))
        out_specs.append(pl.BlockSpec((tm, k), lambda i, j: (i, 0)))
    kern = functools.partial(_nm_kernel, rows=rows, d_mod=k, has_norm=norm_w is not None, mod_idx=mod_idx,
                             has_rope=has_rope, epi=epi, emit_h=emit_h)
    out = pl.pallas_call(
        kern, grid=(n // tm, nj), in_specs=in_specs, out_specs=out_specs, out_shape=out_shape,
        scratch_shapes=[pltpu.VMEM((tm, k), BF16)],
        compiler_params=_cparams("parallel", "arbitrary"), name="norm_matmul",
    )(*args)
    return out if emit_h else out[0]


PROJ_CHUNK = 512


def _pos_in_seq(r, length):
    return r & (length - 1) if length & (length - 1) == 0 else lax.rem(r, length)


def _even_proj_kernel(x_ref, xp_ref, xn_ref, nw_ref, mods_ref, w_ref, cw_ref, alog_ref, dt_ref, cos_ref, sin_ref,
                      qkv_ref, z_ref, gc_ref, gr_ref, att_ref, kv_ref,
                      *, rows, n_conv, n_z, hb2, q_scale, ctx_len, lat_len, n_ctx_rows):
    i = pl.program_id(0)
    tm, d = x_ref.shape
    g = rows.group(i)

    def modulated(ref):
        v = _rmsnorm_rows(ref[...], nw_ref[...])
        return (v * (1.0 + _mod_chunk(mods_ref, g, 1, d)) + _mod_chunk(mods_ref, g, 0, d)).astype(BF16)

    h = modulated(x_ref)
    halo = xp_ref.shape[0]
    h_ext = jnp.concatenate([h, modulated(xp_ref), modulated(xn_ref)], axis=0)
    p_ext = jnp.concatenate([_dot(h_ext, w_ref[:, c0:min(c0 + PROJ_CHUNK, n_conv)])
                             for c0 in range(0, n_conv, PROJ_CHUNK)], axis=1)
    p = p_ext[:tm]
    p_before = p_ext[tm + halo - 1:tm + halo, :]
    p_after = p_ext[tm + halo:tm + halo + 1, :]
    row = lax.broadcasted_iota(jnp.int32, (tm, 1), 0)
    r_glob = row + i * tm
    in_ctx = i < rows.ctx_tiles
    pos = jnp.where(in_ctx, _pos_in_seq(r_glob, ctx_len), _pos_in_seq(r_glob - n_ctx_rows, lat_len))
    seq_last = jnp.where(in_ctx, ctx_len - 1, lat_len - 1)
    x_prev = jnp.where(pos == 0, 0.0, jnp.where(row == 0, p_before, pltpu.roll(p, 1, axis=0)))
    x_next = jnp.where(pos == seq_last, 0.0, jnp.where(row == tm - 1, p_after, pltpu.roll(p, tm - 1, axis=0)))
    cw = cw_ref[...]
    y = _silu(x_prev * cw[0:1, :] + p * cw[1:2, :] + x_next * cw[2:3, :])
    n_qk = 2 * H_A
    for hh in range(n_conv // LANES):
        yh = y[:, hh * LANES:(hh + 1) * LANES]
        if hh < n_qk:
            yh = yh * lax.rsqrt(jnp.sum(yh * yh, axis=-1, keepdims=True) + EPS)
            if hh < n_qk // 2:
                yh = yh * (DK_A ** -0.5)
        qkv_ref[:, hh * LANES:(hh + 1) * LANES] = yh
    z_ref[...] = _dot(h, w_ref[:, n_conv:n_conv + n_z])
    s = _dot(h, w_ref[:, n_conv + n_z:n_conv + n_z + LANES])
    lane = lax.broadcasted_iota(jnp.int32, (1, LANES), 1)
    zg = s + dt_ref[...]
    softplus = jnp.maximum(zg, 0.0) + jnp.log(1.0 + jnp.exp(-jnp.abs(zg)))
    gates = jnp.where((lane % 4) < 2, -jnp.exp(alog_ref[...]) * softplus, jax.nn.sigmoid(s))
    gc_ref[...] = gates
    gr_ref[...] = jnp.transpose(gates)[:gr_ref.shape[0], :]
    n_main = n_conv + n_z + LANES
    cos, sin = cos_ref[...], sin_ref[...]
    for part in range(3):
        for c0 in range(0, hb2, PROJ_CHUNK):
            c1 = min(c0 + PROJ_CHUNK, hb2)
            y = _dot(h, w_ref[:, n_main + part * hb2 + c0:n_main + part * hb2 + c1])
            if part < 2:
                y = _rope(y, cos, sin)
            if part == 0:
                y = y * q_scale
            else:
                kv_ref[:, (part - 1) * hb2 + c0:(part - 1) * hb2 + c1] = y
            att_ref[:, part * hb2 + c0:part * hb2 + c1] = y.astype(BF16)


def _even_proj(x, norm_w, mods, w, conv_w, a_log, dt_bias, rope, rows, n_conv, n_z, hb2, q_scale, ctx_len,
               lat_len, n_ctx_rows):
    n, d = x.shape
    tm = rows.tm
    halo = 2 * SUBLANES
    hb = tm // halo
    alog_row = jnp.zeros((1, LANES), F32).at[0, :4 * H_A].set(
        jnp.stack([a_log[0], a_log[1], a_log[0], a_log[1]], axis=-1).reshape(-1).astype(F32))
    dt_row = jnp.zeros((1, LANES), F32).at[0, :4 * H_A].set(
        jnp.stack([dt_bias[0], dt_bias[1], jnp.zeros_like(dt_bias[0]), jnp.zeros_like(dt_bias[0])],
                  axis=-1).reshape(-1).astype(F32))
    kern = functools.partial(_even_proj_kernel, rows=rows, n_conv=n_conv, n_z=n_z, hb2=hb2, q_scale=q_scale,
                             ctx_len=ctx_len, lat_len=lat_len, n_ctx_rows=n_ctx_rows)
    tab = pl.BlockSpec((tm, LANES), lambda i: (rows.pos_block(i), 0))
    full = lambda a: pl.BlockSpec(a.shape, lambda i: (0, 0))
    nw = norm_w.reshape(1, d).astype(F32)
    cw = conv_w.astype(F32)
    outs = [(n_conv, F32), (n_z, F32), (LANES, F32), None, (3 * hb2, BF16), (2 * hb2, F32)]
    out_specs = [pl.BlockSpec((tm, c[0]), lambda i: (i, 0)) if c else pl.BlockSpec((4 * H_A, tm), lambda i: (0, i))
                 for c in outs]
    out_shape = [jax.ShapeDtypeStruct((n, c[0]), c[1]) if c else jax.ShapeDtypeStruct((4 * H_A, n), F32)
                 for c in outs]
    return pl.pallas_call(
        kern, grid=(n // tm,),
        in_specs=[pl.BlockSpec((tm, d), lambda i: (i, 0)),
                  pl.BlockSpec((halo, d), lambda i: (jnp.maximum(i * hb - 1, 0), 0)),
                  pl.BlockSpec((halo, d), lambda i: (jnp.minimum((i + 1) * hb, n // halo - 1), 0)),
                  full(nw), full(mods), full(w), full(cw), full(alog_row), full(dt_row), tab, tab],
        out_specs=out_specs, out_shape=out_shape,
        compiler_params=_cparams("parallel"), name="even_proj",
    )(x, x, x, nw, mods, w, cw, alog_row, dt_row, *rope)


def _odd_proj_kernel(x_ref, nw_ref, mods_ref, w_ref, qn_ref, kvn_ref, wq_ref, wkv_ref, cos_ref, sin_ref,
                     q_ref, kv_ref, kr_ref, ckv_ref, krf_ref, *, rows, scale, n_nope):
    i = pl.program_id(0)
    d = x_ref.shape[1]
    h = _modulated(x_ref, nw_ref, mods_ref, rows.group(i), 0, d).astype(BF16)
    p = _dot(h, w_ref[...])
    cos, sin = cos_ref[...], sin_ref[...]
    cq = _rmsnorm_rows(p[:, :Q_LORA], qn_ref[...]).astype(BF16)
    for c0 in range(0, wq_ref.shape[1], PROJ_CHUNK):
        y = _dot(cq, wq_ref[:, c0:c0 + PROJ_CHUNK])
        if c0 >= n_nope:
            y = _rope(y, cos, sin)
        q_ref[:, c0:c0 + PROJ_CHUNK] = (y * scale).astype(BF16)
    ckv = _rmsnorm_rows(p[:, Q_LORA:Q_LORA + KV_LORA], kvn_ref[...])
    ckv_ref[...] = ckv
    ckv16 = ckv.astype(BF16)
    for c0 in range(0, wkv_ref.shape[1], PROJ_CHUNK):
        kv_ref[:, c0:c0 + PROJ_CHUNK] = _dot(ckv16, wkv_ref[:, c0:c0 + PROJ_CHUNK]).astype(BF16)
    kr = _rope(p[:, Q_LORA + KV_LORA:], cos, sin)
    krf_ref[...] = kr
    kr_ref[...] = kr.astype(BF16)


def _odd_proj(x, norm_w, mods, w_in, q_norm, kv_norm, wq, wkv, rope, rows, scale, n_nope):
    n, d = x.shape
    tm = rows.tm
    tab = pl.BlockSpec((tm, LANES), lambda i: (rows.pos_block(i), 0))
    full = lambda a: pl.BlockSpec(a.shape, lambda i: (0, 0))
    qn = q_norm.reshape(1, -1).astype(F32)
    kvn = kv_norm.reshape(1, -1).astype(F32)
    nw = norm_w.reshape(1, d).astype(F32)
    outs = [(wq.shape[1], BF16), (wkv.shape[1], BF16), (LANES, BF16), (KV_LORA, F32), (LANES, F32)]
    return pl.pallas_call(
        functools.partial(_odd_proj_kernel, rows=rows, scale=scale, n_nope=n_nope), grid=(n // tm,),
        in_specs=[pl.BlockSpec((tm, d), lambda i: (i, 0)), full(nw), full(mods), full(w_in), full(qn), full(kvn),
                  full(wq), full(wkv), tab, tab],
        out_specs=[pl.BlockSpec((tm, c), lambda i: (i, 0)) for c, _ in outs],
        out_shape=[jax.ShapeDtypeStruct((n, c), dt) for c, dt in outs],
        compiler_params=_cparams("parallel"), name="odd_proj",
    )(x, nw, mods, w_in, qn, kvn, wq, wkv, *rope)


def _split2(x):
    hi = x.astype(BF16)
    return hi, (x - hi.astype(F32)).astype(BF16)


def _mm(a, b, precise, dot=_dot):
    if not precise:
        return dot(a.astype(BF16), b.astype(BF16))
    ah, al = _split2(a)
    bh, bl = _split2(b)
    return dot(ah, bh) + (dot(ah, bl) + dot(al, bh))


SOLVE_BLOCK = SUBLANES
MOE_ROW_BLOCK = 256
FLASH_HEADS_PER_STEP = 2
DELTA_HEADS_PER_STEP = 4


def _tri_solve_many(ms, rs, revs):
    c, width = rs[0].shape
    blk = SOLVE_BLOCK
    nb, gpb, ng = c // blk, blk // SUBLANES, c // SUBLANES
    col = lax.broadcasted_iota(jnp.int32, (blk, c), 1)
    mgs = [[m[SUBLANES * g:SUBLANES * (g + 1), :] for g in range(ng)] for m in ms]
    xgs = [[r[SUBLANES * g:SUBLANES * (g + 1), :] for g in range(ng)] for r in rs]
    zero_blk = jnp.zeros((blk, width), BF16)
    fin_h = [[zero_blk] * nb for _ in ms]
    fin_l = [[zero_blk] * nb for _ in ms]
    for bi in range(nb):
        blocks = [nb - 1 - bi if rev else bi for rev in revs]
        if bi > 0:
            for s, (rev, b) in enumerate(zip(revs, blocks)):
                done = (col >= (b + 1) * blk) if rev else (col < b * blk)
                lh, ll = _split2(jnp.where(done, ms[s][b * blk:(b + 1) * blk, :], 0.0))
                xh = jnp.concatenate(fin_h[s], axis=0)
                xl = jnp.concatenate(fin_l[s], axis=0)
                upd = _dot(lh, xh) + (_dot(lh, xl) + _dot(ll, xh))
                for gg in range(gpb):
                    g = b * gpb + gg
                    xgs[s][g] = xgs[s][g] - upd[SUBLANES * gg:SUBLANES * (gg + 1), :]
        for t in range(blk - 1):
            for s, (rev, b) in enumerate(zip(revs, blocks)):
                j = b * blk + (blk - 1 - t if rev else t)
                xj = xgs[s][j // SUBLANES][j % SUBLANES:j % SUBLANES + 1, :]
                groups = range(b * gpb, (j - 1) // SUBLANES + 1) if rev else range((j + 1) // SUBLANES, (b + 1) * gpb)
                for g in groups:
                    xgs[s][g] = xgs[s][g] - mgs[s][g][:, j:j + 1] * xj
        if bi < nb - 1:
            for s, b in enumerate(blocks):
                fin_h[s][b], fin_l[s][b] = _split2(jnp.concatenate(xgs[s][b * gpb:(b + 1) * gpb], axis=0))
    return [jnp.concatenate(xg, axis=0) for xg in xgs]


def _delta_kernel(qf, kf, vf, gcf, grf, qb, kb, vb, gcb, grb, s0_ref, *rest, chunk, hpb, n_alias, precise):
    of_ref, ob_ref, so_ref, s_scr = rest[n_alias:]
    h0 = pl.program_id(1) * hpb
    i = pl.program_id(2)
    n_i = pl.num_programs(2)

    @pl.when(i == 0)
    def _():
        s_scr[...] = s0_ref[...]

    rg = qf.shape[0]
    nc = rg // chunk
    rowi = lax.broadcasted_iota(jnp.int32, (rg, rg), 0)
    coli = lax.broadcasted_iota(jnp.int32, (rg, rg), 1)
    same = (rowi // chunk) == (coli // chunk)
    lane = lax.broadcasted_iota(jnp.int32, (1, LANES), 1)
    r_c = lax.broadcasted_iota(jnp.int32, (chunk, chunk), 0)
    c_c = lax.broadcasted_iota(jnp.int32, (chunk, chunk), 1)

    prep = []
    for d, (q_ref, k_ref, v_ref, gc_ref, gr_ref) in enumerate(((qf, kf, vf, gcf, grf), (qb, kb, vb, gcb, grb))):
        rev = d == 1
        incl_big = same & ((rowi <= coli) if rev else (rowi >= coli))
        lm = jnp.where(incl_big, 1.0, 0.0).astype(BF16)
        gcols = gc_ref[...]
        grows = gr_ref[...]
        gh, gm, gl_ = _split3(gcols)
        cum_c = _dot(lm, gh) + _dot(lm, gm) + _dot(lm, gl_)
        th, tm_, tl = _split3(grows)
        cum_r = _dot_nt(th, lm) + _dot_nt(tm_, lm) + _dot_nt(tl, lm)

        def col(a, ln):
            return jnp.sum(jnp.where(lane == ln, a, 0.0), axis=1, keepdims=True)

        sub = lax.broadcasted_iota(jnp.int32, (cum_r.shape[0], 1), 0)
        incl = (r_c <= c_c) if rev else (r_c >= c_c)
        strict = (r_c < c_c) if rev else (r_c > c_c)
        for hh in range(hpb):
            h = h0 + hh
            hs = slice(hh * LANES, (hh + 1) * LANES)
            gcum = col(cum_c, 4 * h + d)
            beta = col(gcols, 4 * h + 2 + d)
            grow = jnp.sum(jnp.where(sub == 4 * h + d, cum_r, 0.0), axis=0, keepdims=True)
            q = q_ref[:, hs]
            k = k_ref[:, hs]
            v = v_ref[:, hs]
            kbeta = k * beta
            eg = jnp.exp(gcum)
            rhs = jnp.concatenate([v * beta, kbeta * eg], axis=1)
            ms, rs, a_in = [], [], []
            for c in range(nc):
                sl = slice(c * chunk, (c + 1) * chunk)
                e = jnp.exp(jnp.where(incl, gcum[sl] - grow[:, sl], 0.0))
                kk = _mm(kbeta[sl], k[sl], precise, _dot_nt)
                qk = _mm(q[sl], k[sl], precise, _dot_nt)
                a_in.append(jnp.where(incl, qk * e, 0.0))
                ms.append(jnp.where(strict, kk * e, 0.0))
                rs.append(rhs[sl])
            prep.append((rev, hh, ms, rs, a_in, q * eg, k, gcum))

    all_terms = []
    half = len(prep) // 2
    for wave in (prep[:half], prep[half:]) if half else (prep,):
        sols = _tri_solve_many([m for p in wave for m in p[2]], [r for p in wave for r in p[3]],
                               [p[0] for p in wave for _ in range(nc)])
        for n_u, (rev, hh, _, _, a_in, qg, k, gcum) in enumerate(wave):
            xs = sols[n_u * nc:(n_u + 1) * nc]
            terms = []
            for c in range(nc):
                sl = slice(c * chunk, (c + 1) * chunk)
                r_last = c * chunk if rev else (c + 1) * chunk - 1
                g_last = gcum[r_last:r_last + 1, :]
                k_dec = k[sl] * jnp.exp(g_last - gcum[sl])
                a_uw = _mm(a_in[c], xs[c], precise)
                kd_uw = _mm(k_dec, xs[c], precise, _dot_tn)
                lhs = jnp.concatenate([qg[sl] - a_uw[:, DV_A:], kd_uw[:, DV_A:]], axis=0)
                terms.append((lhs, a_uw[:, :DV_A], kd_uw[:, :DV_A], jnp.exp(g_last)))
            all_terms.append(terms)

    states = [s_scr[1 if p[0] else 0, p[1]] for p in prep]
    outs = [[None] * nc for _ in prep]
    for t in range(nc):
        for n_u, p in enumerate(prep):
            c = nc - 1 - t if p[0] else t
            lhs, o_0, q_0, decay = all_terms[n_u][c]
            prod = _mm(lhs, states[n_u], precise)
            outs[n_u][c] = o_0 + prod[:chunk]
            states[n_u] = states[n_u] * decay - prod[chunk:] + q_0
    for n_u, (rev, hh) in enumerate((p[0], p[1]) for p in prep):
        s_scr[1 if rev else 0, hh] = states[n_u]
        (ob_ref if rev else of_ref)[:, hh * LANES:(hh + 1) * LANES] = jnp.concatenate(outs[n_u], axis=0)

    @pl.when(i == n_i - 1)
    def _():
        so_ref[...] = s_scr[...]


def _delta_stage(qkvn, gcol, grow, s0, n_seq, seq_len, row0, rg, into=None, precise=False):
    ng = seq_len // rg
    b0 = row0 // rg

    def fwd_blk(b, i):
        return b0 + b * ng + i

    def bwd_blk(b, i):
        return b0 + b * ng + (ng - 1 - i)

    hpb = DELTA_HEADS_PER_STEP
    nhp = H_A // hpb
    hw = hpb * LANES

    def specs(blk):
        return [pl.BlockSpec((rg, hw), lambda b, h, i: (blk(b, i), h)),
                pl.BlockSpec((rg, hw), lambda b, h, i: (blk(b, i), nhp + h)),
                pl.BlockSpec((rg, hw), lambda b, h, i: (blk(b, i), 2 * nhp + h)),
                pl.BlockSpec((rg, LANES), lambda b, h, i: (blk(b, i), 0)),
                pl.BlockSpec((4 * H_A, rg), lambda b, h, i: (0, blk(b, i)))]

    st_spec = pl.BlockSpec((None, 2, hpb, DK_A, DV_A), lambda b, h, i: (b, 0, h, 0, 0))
    args = [qkvn, qkvn, qkvn, gcol, grow, qkvn, qkvn, qkvn, gcol, grow, s0]
    in_specs = specs(fwd_blk) + specs(bwd_blk) + [st_spec]
    aliases = {}
    if into is not None:
        for k_out, arr in enumerate(into):
            aliases[len(args)] = k_out
            args.append(arr)
            in_specs.append(pl.BlockSpec(memory_space=pl.ANY))
    n = qkvn.shape[0]
    return pl.pallas_call(
        functools.partial(_delta_kernel, chunk=CHUNK, hpb=hpb, n_alias=len(aliases), precise=precise),
        grid=(n_seq, nhp, ng),
        in_specs=in_specs,
        out_specs=[pl.BlockSpec((rg, hw), lambda b, h, i: (fwd_blk(b, i), h)),
                   pl.BlockSpec((rg, hw), lambda b, h, i: (bwd_blk(b, i), h)),
                   st_spec],
        out_shape=[jax.ShapeDtypeStruct((n, H_A * DV_A), F32),
                   jax.ShapeDtypeStruct((n, H_A * DV_A), F32),
                   jax.ShapeDtypeStruct(s0.shape, F32)],
        input_output_aliases=aliases,
        scratch_shapes=[pltpu.VMEM((2, hpb, DK_A, DV_A), F32)],
        compiler_params=_cparams("parallel", "parallel", "arbitrary"), name="gated_delta",
    )(*args)


def _flash_kernel(*refs, n_parts, has_cache, diff, tk, lam_init, aliased, head_lanes):
    it = iter(refs)
    q_refs = [next(it) for _ in range(n_parts)]
    k_refs = [next(it) for _ in range(n_parts)]
    v_ref = next(it)
    kc_refs = [next(it) for _ in range(n_parts)] if has_cache else []
    vc_ref = next(it) if has_cache else None
    lam_ref = next(it) if diff else None
    if aliased:
        next(it)
    o_ref = next(it)
    tq = q_refs[0].shape[0]
    lk = k_refs[0].shape[0]
    if diff:
        lp = lam_ref[...]
        lam = (jnp.exp(jnp.sum(lp[0:1] * lp[1:2], axis=-1, keepdims=True))
               - jnp.exp(jnp.sum(lp[2:3] * lp[3:4], axis=-1, keepdims=True)) + lam_init)
        lane = lax.broadcasted_iota(jnp.int32, (1, LANES), 1)

    def cols(ref, off, rows=None):
        return ref[:, off:off + LANES] if rows is None else ref[rows, off:off + LANES]

    def block(carry, q_cat, ks, vv):
        m, l, acc = carry
        s = _dot_nt(q_cat, ks[0] if n_parts == 1 else jnp.concatenate(ks, axis=1))
        m_new = jnp.maximum(m, jnp.max(s, axis=-1, keepdims=True))
        alpha = jnp.exp2(m - m_new)
        p_ = jnp.exp2(s - m_new)
        l = alpha * l + jnp.sum(p_, axis=-1, keepdims=True)
        acc = alpha * acc + _dot(p_.astype(BF16), vv)
        return m_new, l, acc

    q_cats, carries = [], []
    for q_offs, *_ in head_lanes:
        qs = [cols(r, off) for r, off in zip(q_refs, q_offs)]
        if diff:
            q = qs[0]
            qs = [jnp.concatenate([jnp.where(lane < DQK_B, q, 0), jnp.where(lane >= DQK_B, q, 0)], axis=0)]
        nrow = qs[0].shape[0]
        q_cats.append(qs[0] if n_parts == 1 else jnp.concatenate(qs, axis=1))
        carries.append((jnp.full((nrow, 1), -1e30, F32), jnp.zeros((nrow, 1), F32), jnp.zeros((nrow, LANES), F32)))

    for t in range(lk // tk):
        rows = slice(t * tk, (t + 1) * tk)
        for n_h, (_, k_offs, v_off, _, _, _) in enumerate(head_lanes):
            carries[n_h] = block(carries[n_h], q_cats[n_h], [cols(r, off, rows) for r, off in zip(k_refs, k_offs)],
                                 cols(v_ref, v_off, rows))
    for n_h, (_, _, _, o_off, kc_offs, vc_off) in enumerate(head_lanes):
        carry = carries[n_h]
        if has_cache:
            carry = block(carry, q_cats[n_h], [cols(r, off) for r, off in zip(kc_refs, kc_offs)], cols(vc_ref, vc_off))
        _, l, acc = carry
        o = acc / l
        if diff:
            o = o[:tq] - lam * o[tq:]
        o_ref[:, o_off:o_off + LANES] = o.astype(o_ref.dtype)


def _flash(q_parts, k_parts, v_part, cache, lam, *, n_seq, n_heads, lq, lk, q_row0, k_row0, tq, tk, diff,
           lam_init, out_cols, into=None, heads_per_step=1):
    nqt = lq // tq
    qb0, kb0 = q_row0 // tq, k_row0 // lk
    hps = heads_per_step
    assert n_heads % hps == 0
    args, in_specs = [], []

    def add(arr, cf, n_rows, row_fn):
        span = 1
        while any(cf(g * hps) // span != cf(g * hps + hps - 1) // span for g in range(n_heads // hps)):
            span *= 2
        offs = tuple((cf(j) - cf(0) // span * span) * LANES for j in range(hps))
        for g in range(n_heads // hps):
            assert offs == tuple((cf(g * hps + j) - cf(g * hps) // span * span) * LANES for j in range(hps))
        args.append(arr)
        in_specs.append(pl.BlockSpec((n_rows, span * LANES),
                                     lambda b, hg, t: (row_fn(b, t), cf(hg * hps) // span)))
        return offs

    q_offs = [add(arr, cf, tq, lambda b, t: qb0 + b * nqt + t) for arr, cf in q_parts]
    k_offs = [add(arr, cf, lk, lambda b, t: kb0 + b) for arr, cf in k_parts]
    v_offs = add(v_part[0], v_part[1], lk, lambda b, t: kb0 + b)
    kc_offs, vc_offs = [(0,) * hps] * len(q_parts), (0,) * hps
    if cache is not None:
        c_k, c_v, c_rows = cache
        kc_offs = [add(arr, cf, c_rows, lambda b, t: b) for arr, cf in c_k]
        vc_offs = add(c_v[0], c_v[1], c_rows, lambda b, t: b)
    head_lanes = tuple((tuple(o[j] for o in q_offs), tuple(o[j] for o in k_offs), v_offs[j], j * LANES,
                        tuple(o[j] for o in kc_offs), vc_offs[j]) for j in range(hps))
    grid = (n_seq, n_heads // hps, nqt)
    out_spec = pl.BlockSpec((tq, hps * LANES), lambda b, hg, t: (qb0 + b * nqt + t, hg))
    if diff:
        args.append(lam)
        in_specs.append(pl.BlockSpec(lam.shape, lambda b, h, t: (0, 0)))
    aliases = {}
    if into is not None:
        aliases[len(args)] = 0
        args.append(into)
        in_specs.append(pl.BlockSpec(memory_space=pl.ANY))
    kern = functools.partial(_flash_kernel, n_parts=len(q_parts), has_cache=cache is not None, diff=diff, tk=tk,
                             lam_init=lam_init, aliased=into is not None, head_lanes=head_lanes)
    return pl.pallas_call(
        kern, grid=grid, in_specs=in_specs, out_specs=out_spec,
        out_shape=jax.ShapeDtypeStruct((q_parts[0][0].shape[0], out_cols), F32 if diff else BF16),
        input_output_aliases=aliases,
        compiler_params=_cparams("parallel", "parallel", "arbitrary"), name="flash_attention",
    )(*args)


def _mr_kernel(x_ref, a_ref, w_ref, mods_ref, o_ref, *, rows, gate_idx):
    i = pl.program_id(0)
    d = x_ref.shape[1]
    gate = _mod_chunk(mods_ref, rows.group(i), gate_idx, d)
    o_ref[...] = x_ref[...] + gate * _dot(a_ref[...].astype(BF16), w_ref[...])


def _matmul_residual(x, a, w, mods, rows, gate_idx):
    n, d = x.shape
    tm = rows.tm
    return pl.pallas_call(
        functools.partial(_mr_kernel, rows=rows, gate_idx=gate_idx), grid=(n // tm,),
        in_specs=[pl.BlockSpec((tm, d), lambda i: (i, 0)),
                  pl.BlockSpec((tm, a.shape[1]), lambda i: (i, 0)),
                  pl.BlockSpec(w.shape, lambda i: (0, 0)),
                  pl.BlockSpec(mods.shape, lambda i: (0, 0))],
        out_specs=pl.BlockSpec((tm, d), lambda i: (i, 0)),
        out_shape=jax.ShapeDtypeStruct((n, d), F32),
        compiler_params=_cparams("parallel"), name="matmul_residual",
    )(x, a, w, mods)


def _modulated(x_ref, nw_ref, mods_ref, g, idx, d):
    x = _rmsnorm_rows(x_ref[...], nw_ref[...])
    return x * (1.0 + _mod_chunk(mods_ref, g, idx + 1, d)) + _mod_chunk(mods_ref, g, idx, d)


def _ffn_kernel(x_ref, nw_ref, mods_ref, wg_ref, wu_ref, wd_ref, of_ref, ob_ref, z_ref, od_ref, on_ref, sn_ref,
                wo_ref, o_ref, h_scr, acc_scr, xn_scr, *, rows, mod0, mix_gate, lam_init):
    i, f = pl.program_id(0), pl.program_id(1)
    d = x_ref.shape[1]
    g = rows.group(i)

    @pl.when(f == 0)
    def _():
        oa = of_ref[...] + ob_ref[...]
        z = z_ref[...]
        od = od_ref[...]
        ya, yb = [], []
        for hh in range(H_A):
            sl = slice(hh * LANES, (hh + 1) * LANES)
            ya.append(_rmsnorm_rows(oa[:, sl], on_ref[...]) * _silu(z[:, sl]))
        for hh in range(H_B):
            sl = slice(hh * LANES, (hh + 1) * LANES)
            yb.append(_rmsnorm_rows(od[:, sl], sn_ref[...]) * (1.0 - lam_init))
        y = jnp.concatenate(ya + yb, axis=1).astype(BF16)
        x1 = x_ref[...] + _mod_chunk(mods_ref, g, mix_gate, d) * _dot(y, wo_ref[...])
        xn_scr[...] = x1
        v = _rmsnorm_rows(x1, nw_ref[...])
        h_scr[...] = (v * (1.0 + _mod_chunk(mods_ref, g, mod0 + 1, d)) + _mod_chunk(mods_ref, g, mod0, d)).astype(BF16)
        acc_scr[...] = jnp.zeros_like(acc_scr)

    h = h_scr[...]
    act = _silu(_dot(h, wg_ref[...])) * _dot(h, wu_ref[...])
    acc_scr[...] += _dot(act.astype(BF16), wd_ref[...])

    @pl.when(f == pl.num_programs(1) - 1)
    def _():
        o_ref[...] = xn_scr[...] + _mod_chunk(mods_ref, g, mod0 + 2, d) * acc_scr[...]


def _even_tail(x, o_f, o_b, z, o_d, onorm, subln, w_out, norm_w, mods, w_gu, w_down, rows, mix_gate, mod0, tf,
               lam_init):
    n, d = x.shape
    tm = rows.tm
    ff = w_down.shape[0]
    nf = ff // tf
    hw = H_A * DV_A
    row = lambda c: pl.BlockSpec((tm, c), lambda i, f: (i, 0))
    one = lambda c: pl.BlockSpec((1, c), lambda i, f: (0, 0))
    return pl.pallas_call(
        functools.partial(_ffn_kernel, rows=rows, mod0=mod0, mix_gate=mix_gate, lam_init=lam_init),
        grid=(n // tm, nf),
        in_specs=[row(d), one(d), pl.BlockSpec(mods.shape, lambda i, f: (0, 0)),
                  pl.BlockSpec((d, tf), lambda i, f: (0, f)),
                  pl.BlockSpec((d, tf), lambda i, f: (0, nf + f)),
                  pl.BlockSpec((tf, d), lambda i, f: (f, 0)),
                  row(hw), row(hw), row(hw), row(hw), one(LANES), one(LANES),
                  pl.BlockSpec(w_out.shape, lambda i, f: (0, 0))],
        out_specs=row(d),
        out_shape=jax.ShapeDtypeStruct((n, d), F32),
        scratch_shapes=[pltpu.VMEM((tm, d), BF16), pltpu.VMEM((tm, d), F32), pltpu.VMEM((tm, d), F32)],
        compiler_params=_cparams("parallel", "arbitrary"), name="even_tail",
    )(x, norm_w.reshape(1, d).astype(F32), mods, w_gu, w_gu, w_down, o_f, o_b, z, o_d,
      onorm.reshape(1, LANES).astype(F32), subln.reshape(1, LANES).astype(F32), w_out)


def _moe_kernel(x_ref, nw_ref, mods_ref, rw_ref, wgu_ref, wd_ref, *rest, rows, mod0, n_exp, sb, final):
    if final:
        fw_ref, o_ctx_ref, o_lat_ref, h_scr, g_scr, rk_scr, rkt_scr, acc_scr = rest
    else:
        o_ref, h_scr, g_scr, rk_scr, rkt_scr, acc_scr = rest
    i, e = pl.program_id(0), pl.program_id(1)
    tm, d = x_ref.shape
    g = rows.group(i)
    lane = lax.broadcasted_iota(jnp.int32, (1, LANES), 1)

    @pl.when(e == 0)
    def _():
        h = _modulated(x_ref, nw_ref, mods_ref, g, mod0, d)
        h_scr[...] = h.astype(BF16)
        acc_scr[...] = jnp.zeros_like(acc_scr)
        logits = jnp.where(lane < n_exp, _mm(h, rw_ref[...], True), -1e30)
        pe = jnp.exp(logits - jnp.max(logits, axis=-1, keepdims=True))
        probs = pe / jnp.sum(pe, axis=-1, keepdims=True)
        p1 = jnp.max(probs, axis=-1, keepdims=True)
        i1 = jnp.min(jnp.where(probs == p1, lane, LANES), axis=-1, keepdims=True)
        rest = jnp.where(lane == i1, -1.0, probs)
        p2 = jnp.max(rest, axis=-1, keepdims=True)
        i2 = jnp.min(jnp.where(rest == p2, lane, LANES), axis=-1, keepdims=True)
        den = p1 + p2
        gates = jnp.where(lane == i1, p1 / den, 0.0) + jnp.where(lane == i2, p2 / den, 0.0)
        g_scr[...] = gates
        t_r = lax.broadcasted_iota(jnp.int32, (tm, tm), 0)
        t_c = lax.broadcasted_iota(jnp.int32, (tm, tm), 1)
        earlier = jnp.where(t_r > t_c, 1.0, 0.0).astype(BF16)
        sel = gates > 0.0
        rank = jnp.where(sel, _dot(earlier, jnp.where(sel, 1.0, 0.0).astype(BF16)), -1.0)
        rk_scr[...] = rank
        rkt_scr[...] = jnp.transpose(rank)[:rkt_scr.shape[0], :]

    ff = wd_ref.shape[0]
    g_e = jnp.sum(jnp.where(lane == e, g_scr[...], 0.0), axis=-1, keepdims=True)
    rk_col = jnp.sum(jnp.where(lane == e, rk_scr[...], 0.0), axis=-1, keepdims=True)
    rk_row = rkt_scr[pl.ds(e, 1), :]
    n_rows = (jnp.max(rk_row) + 1.0).astype(jnp.int32)
    g_hi = g_e.astype(BF16).astype(F32)
    g2 = jnp.where(lane == 0, g_hi, jnp.where(lane == 1, g_e - g_hi, 0.0)).astype(BF16)
    def expert_rows(slot0, nr):
        base = slot0.astype(F32)
        r_sub = lax.broadcasted_iota(jnp.int32, (nr, 1), 0).astype(F32)
        r_lane = lax.broadcasted_iota(jnp.int32, (1, nr), 1).astype(F32)
        pick = jnp.where(rk_row == r_sub + base, 1.0, 0.0).astype(BF16)
        xg = _dot(pick, h_scr[...]).astype(BF16)
        gr = _dot(pick, g2)
        gate_r = gr[:, 0:1] + gr[:, 1:2]
        gu = _dot(xg, wgu_ref[...])
        act = _silu(gu[:, :ff]) * gu[:, ff:]
        y = _dot(act.astype(BF16), wd_ref[...])
        put = jnp.where(rk_col == r_lane + base, 1.0, 0.0).astype(BF16)
        acc_scr[...] += _dot(put, (y * gate_r).astype(BF16))

    half, quarter = sb // 2, sb // 4
    rem = n_rows % sb
    n_full = n_rows // sb + jnp.where(rem > half, 1, 0)

    def full_body(jb, carry):
        expert_rows(jb * sb, sb)
        return carry

    lax.fori_loop(0, n_full, full_body, 0)

    @pl.when((rem > quarter) & (rem <= half))
    def _():
        expert_rows(n_full * sb, half)

    @pl.when((rem > 0) & (rem <= quarter))
    def _():
        expert_rows(n_full * sb, quarter)

    def result():
        return x_ref[...] + _mod_chunk(mods_ref, g, mod0 + 2, d) * acc_scr[...]

    last = e == n_exp - 1
    if final:
        @pl.when(last & (i < rows.ctx_tiles))
        def _():
            o_ctx_ref[...] = _rmsnorm_rows(result(), fw_ref[...])

        @pl.when(last & (i >= rows.ctx_tiles))
        def _():
            o_lat_ref[...] = _rmsnorm_rows(result(), fw_ref[...])
    else:
        @pl.when(last)
        def _():
            o_ref[...] = result()


def _moe(x, norm_w, mods, router_w, w_gu, w_down, rows, mod0, final_w=None):
    n, d = x.shape
    tm = rows.tm
    n_exp, _, ff2 = w_gu.shape
    rw = jnp.zeros((d, LANES), F32).at[:, :n_exp].set(router_w.astype(F32))
    final = final_w is not None
    args = [x, norm_w.reshape(1, d).astype(F32), mods, rw, w_gu, w_down]
    in_specs = [pl.BlockSpec((tm, d), lambda i, e: (i, 0)),
                pl.BlockSpec((1, d), lambda i, e: (0, 0)),
                pl.BlockSpec(mods.shape, lambda i, e: (0, 0)),
                pl.BlockSpec((d, LANES), lambda i, e: (0, 0)),
                pl.BlockSpec((None, d, ff2), lambda i, e: (e, 0, 0)),
                pl.BlockSpec((None, ff2 // 2, d), lambda i, e: (e, 0, 0))]
    if final:
        args.append(final_w.reshape(1, d).astype(F32))
        in_specs.append(pl.BlockSpec((1, d), lambda i, e: (0, 0)))
        ct = rows.ctx_tiles
        out_specs = [pl.BlockSpec((tm, d), lambda i, e: (jnp.minimum(i, ct - 1), 0), pipeline_mode=pl.Buffered(1)),
                     pl.BlockSpec((tm, d), lambda i, e: (jnp.maximum(i - ct, 0), 0), pipeline_mode=pl.Buffered(1))]
        out_shape = [jax.ShapeDtypeStruct((ct * tm, d), F32), jax.ShapeDtypeStruct((n - ct * tm, d), F32)]
    else:
        out_specs = pl.BlockSpec((tm, d), lambda i, e: (i, 0))
        out_shape = jax.ShapeDtypeStruct((n, d), F32)
    return pl.pallas_call(
        functools.partial(_moe_kernel, rows=rows, mod0=mod0, n_exp=n_exp, sb=MOE_ROW_BLOCK, final=final),
        grid=(n // tm, n_exp), in_specs=in_specs, out_specs=out_specs, out_shape=out_shape,
        scratch_shapes=[pltpu.VMEM((tm, d), BF16), pltpu.VMEM((tm, LANES), F32), pltpu.VMEM((tm, LANES), F32),
                        pltpu.VMEM((SUBLANES * ((n_exp + SUBLANES - 1) // SUBLANES), tm), F32),
                        pltpu.VMEM((tm, d), F32)],
        compiler_params=_cparams("arbitrary", "arbitrary"), name="moe",
    )(*args)


def _final_kernel(x_ref, w_ref, o_ref):
    o_ref[...] = _rmsnorm_rows(x_ref[...], w_ref[...])


def _final_norm(x, w, tm):
    n, d = x.shape
    return pl.pallas_call(
        _final_kernel, grid=(n // tm,),
        in_specs=[pl.BlockSpec((tm, d), lambda i: (i, 0)), pl.BlockSpec((1, d), lambda i: (0, 0))],
        out_specs=pl.BlockSpec((tm, d), lambda i: (i, 0)),
        out_shape=jax.ShapeDtypeStruct((n, d), F32),
        compiler_params=_cparams("parallel"), name="final_norm",
    )(x, w.reshape(1, d).astype(F32))


def kernel(x_prompt, x_sample, state_delta, cache_diff_k, cache_diff_v, cache_mla_ckv, cache_mla_krope, c, c_ctx,
           mod_w, mod_b, norm_mix, norm_ffn, final_norm, ev_w_in, ev_conv_w, ev_a_log, ev_dt_bias, ev_onorm,
           ev_lambda, ev_subln, ev_w_out, ffn_w_gu, ffn_w_down, od_w_in, od_q_norm, od_kv_norm, od_w_uq, od_w_ukv,
           od_w_out, moe_router, moe_w_gu, moe_w_down):
    bp, lp, d = x_prompt.shape
    bs, ls, _ = x_sample.shape
    past = cache_diff_k.shape[3]
    depth = mod_w.shape[0]
    n_p, n_s = bp * lp, bs * ls
    n = n_p + n_s
    assert n_p % ls == 0 and past == lp

    rows = _Rows(n_p, n_s, ls, _tile(math.gcd(n_p, ls), 512))
    rows_m = _Rows(n_p, n_s, ls, _tile(math.gcd(n_p, ls), 1024))
    rg = _tile(math.gcd(lp, ls), 256)
    cos_t, sin_t = _rope_tables(rows, ls)

    x = jnp.concatenate([x_prompt.reshape(n_p, d), x_sample.reshape(n_s, d)], axis=0)
    n_grp = 1 + bs
    cc = jnp.zeros((2 * SUBLANES * ((n_grp + 15) // 16), d), F32).at[0].set(c_ctx).at[1:n_grp].set(c)
    mods_all = _mods(cc, mod_w, mod_b)

    hq = H_A * DK_A
    hb2 = H_B * 2 * DQK_B
    sm0 = 4 * hq
    qb0 = sm0 + 4 * H_A
    q_scale = (DQK_B ** -0.5) * LOG2E
    mla_scale = ((NOPE_C + ROPE_C) ** -0.5) * LOG2E
    sd_new, dk_new, dv_new, ckv_new, kr_new = [], [], [], [], []

    for layer in range(depth):
        j = layer // 2
        mods = mods_all[layer]
        if layer % 2 == 0:
            lam_init = 0.8 - 0.6 * math.exp(-0.3 * layer)
            w = ev_w_in[j]
            sm = w[:, sm0:qb0].reshape(d, 4, H_A).transpose(0, 2, 1).reshape(d, 4 * H_A)
            w_in = jnp.concatenate([w[:, :sm0], sm, jnp.zeros((d, LANES - 4 * H_A), w.dtype), w[:, qb0:]],
                                   axis=1).astype(BF16)
            qkvn, z_gate, gcol, grow, att, kv_f = _even_proj(
                x, norm_mix[layer], mods, w_in, ev_conv_w[j], ev_a_log[j], ev_dt_bias[j], (cos_t, sin_t), rows,
                3 * hq, hq, hb2, q_scale, lp, ls, n_p)
            o_f, o_b, s_p = _delta_stage(qkvn, gcol, grow, jnp.zeros((bp, 2, H_A, DK_A, DV_A), F32), bp, lp, 0, rg,
                                         precise=True)
            o_f, o_b, _ = _delta_stage(qkvn, gcol, grow, state_delta[:, j].astype(F32), bs, ls, n_p, rg,
                                       into=(o_f, o_b))
            sd_new.append(s_p)
            lam_p = ev_lambda[j].astype(F32)
            nb = hb2 // LANES
            qp, kp, vp = [(att, lambda h: h)], [(att, lambda h: nb + h)], (att, lambda h: 2 * nb + h)
            o_d = _flash(qp, kp, vp, None, lam_p, n_seq=bp, n_heads=H_B, lq=lp, lk=lp, q_row0=0, k_row0=0,
                         tq=_tile(lp, 256), tk=_tile(lp, 512), diff=True, lam_init=lam_init, out_cols=hb2,
                         heads_per_step=H_B)
            ck = cache_diff_k[:, j].transpose(0, 2, 1, 3).reshape(bs * past, H_B * 2 * DQK_B).astype(BF16)
            cv = cache_diff_v[:, j].transpose(0, 2, 1, 3).reshape(bs * past, H_B * DV_B).astype(BF16)
            cache = ([(ck, lambda h: h)], (cv, lambda h: h), past)
            o_d = _flash(qp, kp, vp, cache, lam_p, n_seq=bs, n_heads=H_B, lq=ls, lk=ls, q_row0=n_p, k_row0=n_p,
                         tq=_tile(ls, 512), tk=_tile(ls, 2048), diff=True, lam_init=lam_init, out_cols=hb2, into=o_d,
                         heads_per_step=FLASH_HEADS_PER_STEP)
            dk_new.append(kv_f[:n_p, :hb2].reshape(bp, lp, H_B, 2 * DQK_B).transpose(0, 2, 1, 3))
            dv_new.append(kv_f[:n_p, hb2:].reshape(bp, lp, H_B, DV_B).transpose(0, 2, 1, 3))
            ff = ffn_w_down.shape[1]
            x = _even_tail(x, o_f, o_b, z_gate, o_d, ev_onorm[j], ev_subln[j], ev_w_out[j].astype(BF16),
                           norm_ffn[layer], mods, ffn_w_gu[j].astype(BF16), ffn_w_down[j].astype(BF16), rows, 2, 3,
                           ff // 2 if (ff // 2) % LANES == 0 else ff, lam_init)
        else:
            n_in = Q_LORA + KV_LORA + ROPE_C
            pad = (-n_in) % LANES
            w_in = jnp.concatenate([od_w_in[j], jnp.zeros((d, pad), od_w_in.dtype)], axis=1).astype(BF16)
            wq = od_w_uq[j].reshape(Q_LORA, H_C, NOPE_C + ROPE_C)
            wq_rope = jnp.concatenate([wq[:, :, NOPE_C:], jnp.zeros((Q_LORA, H_C, LANES - ROPE_C), wq.dtype)], axis=2)
            wq2 = jnp.concatenate([wq[:, :, :NOPE_C].reshape(Q_LORA, H_C * NOPE_C),
                                   wq_rope.reshape(Q_LORA, H_C * LANES)], axis=1).astype(BF16)
            q_all, kv_tok, kr_tok, ckv_n, kr_f = _odd_proj(
                x, norm_mix[layer], mods, w_in, od_q_norm[j], od_kv_norm[j], wq2, od_w_ukv[j].astype(BF16),
                (cos_t, sin_t), rows, mla_scale, H_C * NOPE_C)
            ckv_new.append(ckv_n[:n_p].reshape(bp, lp, KV_LORA))
            kr_new.append(kr_f[:n_p, :ROPE_C].reshape(bp, lp, ROPE_C))
            rows_k = _Rows(bs * past, 0, past, _tile(past, 256))
            kv_c = _norm_matmul(cache_mla_ckv[:, j].reshape(bs * past, KV_LORA), od_w_ukv[j].astype(BF16), rows_k,
                                tn=512, out_dtype=BF16)
            kr_c = jnp.concatenate([cache_mla_krope[:, j].reshape(bs * past, ROPE_C),
                                    jnp.zeros((bs * past, LANES - ROPE_C), F32)], axis=1).astype(BF16)
            qp = [(q_all, lambda h: h), (q_all, lambda h: H_C + h)]
            kp = [(kv_tok, lambda h: 2 * h), (kr_tok, lambda h: 0)]
            vp = (kv_tok, lambda h: 2 * h + 1)
            o = _flash(qp, kp, vp, None, None, n_seq=bp, n_heads=H_C, lq=lp, lk=lp, q_row0=0, k_row0=0,
                       tq=_tile(lp, 256), tk=_tile(lp, 512), diff=False, lam_init=0.0, out_cols=H_C * V_C,
                       heads_per_step=H_C)
            cache = ([(kv_c, lambda h: 2 * h), (kr_c, lambda h: 0)], (kv_c, lambda h: 2 * h + 1), past)
            o = _flash(qp, kp, vp, cache, None, n_seq=bs, n_heads=H_C, lq=ls, lk=ls, q_row0=n_p, k_row0=n_p,
                       tq=_tile(ls, 1024), tk=_tile(ls, 2048), diff=False, lam_init=0.0, out_cols=H_C * V_C, into=o,
                       heads_per_step=FLASH_HEADS_PER_STEP)
            x = _matmul_residual(x, o, od_w_out[j].astype(BF16), mods, rows, 2)
            x = _moe(x, norm_ffn[layer], mods, moe_router[j], moe_w_gu[j].astype(BF16), moe_w_down[j].astype(BF16),
                     rows_m, 3, final_w=final_norm if layer == depth - 1 else None)

    if depth % 2 == 0:
        y_p, y_s = x
    else:
        y = _final_norm(x, final_norm, rows.tm)
        y_p, y_s = y[:n_p], y[n_p:]
    return (y_p.reshape(bp, lp, d), y_s.reshape(bs, ls, d), jnp.stack(sd_new, axis=1),
            jnp.stack(dk_new, axis=1), jnp.stack(dv_new, axis=1), jnp.stack(ckv_new, axis=1),
            jnp.stack(kr_new, axis=1))
```

```python
import functools
import math

import numpy as np
import jax
import jax.numpy as jnp
from jax import lax
from jax.experimental import pallas as pl
from jax.experimental.pallas import tpu as pltpu

F32 = jnp.float32
BF16 = jnp.bfloat16

EPS = 1e-6
LOG2E = 1.4426950408889634
GRID_W = 64
ROPE_BASE = 10000.0
H_A, DK_A, DV_A = 4, 128, 128
CONV_K = 3
CHUNK = 64
H_B, DQK_B, DV_B = 4, 64, 128
H_C, NOPE_C, ROPE_C, V_C = 8, 128, 64, 128
Q_LORA, KV_LORA = 384, 256
LANES = 128
SUBLANES = 8
VMEM_LIMIT_BYTES = 56 * 1024 * 1024


def _cparams(*sem):
    return pltpu.CompilerParams(dimension_semantics=sem, vmem_limit_bytes=VMEM_LIMIT_BYTES)


def _dot(a, b):
    return jnp.dot(a, b, preferred_element_type=F32)


def _dot_nt(a, b):
    return lax.dot_general(a, b, (((1,), (1,)), ((), ())), preferred_element_type=F32)


def _dot_tn(a, b):
    return lax.dot_general(a, b, (((0,), (0,)), ((), ())), preferred_element_type=F32)


def _split3(x):
    hi = x.astype(BF16)
    r = x - hi.astype(F32)
    mid = r.astype(BF16)
    lo = (r - mid.astype(F32)).astype(BF16)
    return hi, mid, lo


def _dot_f32(a, b):
    ah, am, al = _split3(a)
    bh, bm, bl = _split3(b)
    return (_dot(ah, bh) + (_dot(ah, bm) + _dot(am, bh))
            + (_dot(am, bm) + _dot(ah, bl) + _dot(al, bh)))


def _silu(x):
    return x * jax.nn.sigmoid(x)


def _tile(n, pref, mult=SUBLANES):
    t = min(n, pref)
    while t > mult and (n % t or t % mult):
        t -= mult
    assert n % t == 0, (n, pref)
    return t


class _Rows:
    def __init__(self, n_ctx, n_lat, lat_len, tm):
        assert n_ctx % tm == 0 and lat_len % tm == 0
        self.tm = tm
        self.n = n_ctx + n_lat
        self.ctx_tiles = n_ctx // tm
        self.seq_tiles = lat_len // tm

    def group(self, i):
        return jnp.where(i < self.ctx_tiles, 0, 1 + jnp.maximum(i - self.ctx_tiles, 0) // self.seq_tiles)

    def pos_block(self, i):
        return jnp.where(i < self.ctx_tiles, 0, 1 + jnp.maximum(i - self.ctx_tiles, 0) % self.seq_tiles)


def _mod_kernel(c_ref, w_ref, b_ref, o_ref):
    o_ref[...] = _dot_f32(_silu(c_ref[...]), w_ref[...]) + b_ref[...]


def _mods(cc, mod_w, mod_b):
    depth, d, n6 = mod_w.shape
    tn = _tile(n6, 512, LANES)
    return pl.pallas_call(
        _mod_kernel,
        grid=(depth, n6 // tn),
        in_specs=[pl.BlockSpec(cc.shape, lambda l, j: (0, 0)),
                  pl.BlockSpec((None, d, tn), lambda l, j: (l, 0, j)),
                  pl.BlockSpec((None, 1, tn), lambda l, j: (l, 0, j))],
        out_specs=pl.BlockSpec((None, cc.shape[0], tn), lambda l, j: (l, 0, j)),
        out_shape=jax.ShapeDtypeStruct((depth, cc.shape[0], n6), F32),
        compiler_params=_cparams("parallel", "parallel"),
        name="mods",
    )(cc, mod_w, mod_b.reshape(depth, 1, n6))


def _mod_chunk(mods_ref, g, idx, d):
    return mods_ref[pl.ds(g, 1), idx * d:(idx + 1) * d]


def _rmsnorm_rows(x, w):
    ms = jnp.mean(x * x, axis=-1, keepdims=True)
    return x * lax.rsqrt(ms + EPS) * w


def _rope(x, cos, sin_signed):
    lane = lax.broadcasted_iota(jnp.int32, (1, LANES), 1)
    low = (lane % 32) < 16
    outs = []
    for c in range(x.shape[1] // LANES):
        xc = x[:, c * LANES:(c + 1) * LANES]
        fwd = pltpu.roll(xc, LANES - 16, axis=1)
        bwd = pltpu.roll(xc, 16, axis=1)
        outs.append(xc * cos + jnp.where(low, fwd, bwd) * sin_signed)
    return outs[0] if len(outs) == 1 else jnp.concatenate(outs, axis=1)


def _rope_tables(rows, lat_len):
    pos = jnp.arange(lat_len)
    r = (pos // GRID_W).astype(F32)
    c = (pos % GRID_W).astype(F32)
    inv = ROPE_BASE ** (-jnp.arange(0, 32, 2, dtype=F32) / 32)
    ang_r = r[:, None] * inv[None, :]
    ang_c = c[:, None] * inv[None, :]
    ang = jnp.concatenate([ang_r, ang_r, ang_c, ang_c] * 2, axis=-1)
    sign = jnp.where((jnp.arange(LANES) % 32) < 16, -1.0, 1.0).astype(F32)
    cos = jnp.concatenate([jnp.ones((rows.tm, LANES), F32), jnp.cos(ang)], axis=0)
    sin = jnp.concatenate([jnp.zeros((rows.tm, LANES), F32), jnp.sin(ang) * sign[None, :]], axis=0)
    return cos, sin


def _nm_kernel(*refs, rows, d_mod, has_norm, mod_idx, has_rope, epi, emit_h):
    it = iter(refs)
    x_ref = next(it)
    w_ref = next(it)
    nw_ref = next(it) if has_norm else None
    mods_ref = next(it) if mod_idx is not None else None
    cos_ref, sin_ref = (next(it), next(it)) if has_rope else (None, None)
    o_ref = next(it)
    h_out_ref = next(it) if emit_h else None
    h_scr = next(it)
    i, j = pl.program_id(0), pl.program_id(1)

    @pl.when(j == 0)
    def _():
        x = x_ref[...].astype(F32)
        if has_norm:
            x = _rmsnorm_rows(x, nw_ref[...])
        if mod_idx is not None:
            g = rows.group(i)
            x = x * (1.0 + _mod_chunk(mods_ref, g, mod_idx[1], d_mod)) + _mod_chunk(mods_ref, g, mod_idx[0], d_mod)
        h_scr[...] = x.astype(BF16)
        if emit_h:
            h_out_ref[...] = x

    acc = _dot(h_scr[...], w_ref[...])
    kinds = sorted(set(epi))
    for kind in kinds:
        cond = None
        for jj, e in enumerate(epi):
            if e == kind:
                cond = (j == jj) if cond is None else (cond | (j == jj))

        def _store(kind=kind):
            y = acc
            if kind[0]:
                y = _rope(y, cos_ref[...], sin_ref[...])
            if kind[1] != 1.0:
                y = y * kind[1]
            o_ref[...] = y.astype(o_ref.dtype)

        if len(kinds) == 1:
            _store()
        else:
            pl.when(cond)(_store)


def _norm_matmul(x, w, rows, *, tn, norm_w=None, mods=None, mod_idx=None, rope=None, epi=None,
                 emit_h=False, out_dtype=F32):
    n, k = x.shape
    nout = w.shape[1]
    nj = nout // tn
    tm = rows.tm
    epi = tuple(epi) if epi is not None else ((False, 1.0),) * nj
    assert len(epi) == nj and n % tm == 0
    has_rope = any(e[0] for e in epi)
    args = [x, w]
    in_specs = [pl.BlockSpec((tm, k), lambda i, j: (i, 0)), pl.BlockSpec((k, tn), lambda i, j: (0, j))]
    if norm_w is not None:
        args.append(norm_w.reshape(1, k).astype(F32))
        in_specs.append(pl.BlockSpec((1, k), lambda i, j: (0, 0)))
    if mod_idx is not None:
        args.append(mods)
        in_specs.append(pl.BlockSpec(mods.shape, lambda i, j: (0, 0)))
    if has_rope:
        args += list(rope)
        in_specs += [pl.BlockSpec((tm, LANES), lambda i, j: (rows.pos_block(i), 0))] * 2
    out_shape = [jax.ShapeDtypeStruct((n, nout), out_dtype)]
    out_specs = [pl.BlockSpec((tm, tn), lambda i, j: (i, j))]
    if emit_h:
        out_shape.append(jax.ShapeDtypeStruct((n, k), F32))
        out_specs.append(pl.BlockSpec((tm, k), lambda i, j: (i, 0)))
    kern = functools.partial(_nm_kernel, rows=rows, d_mod=k, has_norm=norm_w is not None, mod_idx=mod_idx,
                             has_rope=has_rope, epi=epi, emit_h=emit_h)
    out = pl.pallas_call(
        kern, grid=(n // tm, nj), in_specs=in_specs, out_specs=out_specs, out_shape=out_shape,
        scratch_shapes=[pltpu.VMEM((tm, k), BF16)],
        compiler_params=_cparams("parallel", "arbitrary"), name="norm_matmul",
    )(*args)
    return out if emit_h else out[0]


PROJ_CHUNK = 512


def _pos_in_seq(r, length):
    return r & (length - 1) if length & (length - 1) == 0 else lax.rem(r, length)


def _even_proj_kernel(x_ref, xp_ref, xn_ref, nw_ref, mods_ref, w_ref, cw_ref, alog_ref, dt_ref, cos_ref, sin_ref,
                      qkv_ref, z_ref, gc_ref, gr_ref, att_ref, kv_ref,
                      *, rows, n_conv, n_z, hb2, q_scale, ctx_len, lat_len, n_ctx_rows):
    i = pl.program_id(0)
    tm, d = x_ref.shape
    g = rows.group(i)

    def modulated(ref):
        v = _rmsnorm_rows(ref[...], nw_ref[...])
        return (v * (1.0 + _mod_chunk(mods_ref, g, 1, d)) + _mod_chunk(mods_ref, g, 0, d)).astype(BF16)

    h = modulated(x_ref)
    halo = xp_ref.shape[0]
    h_ext = jnp.concatenate([h, modulated(xp_ref), modulated(xn_ref)], axis=0)
    p_ext = jnp.concatenate([_dot(h_ext, w_ref[:, c0:min(c0 + PROJ_CHUNK, n_conv)])
                             for c0 in range(0, n_conv, PROJ_CHUNK)], axis=1)
    p = p_ext[:tm]
    p_before = p_ext[tm + halo - 1:tm + halo, :]
    p_after = p_ext[tm + halo:tm + halo + 1, :]
    row = lax.broadcasted_iota(jnp.int32, (tm, 1), 0)
    r_glob = row + i * tm
    in_ctx = i < rows.ctx_tiles
    pos = jnp.where(in_ctx, _pos_in_seq(r_glob, ctx_len), _pos_in_seq(r_glob - n_ctx_rows, lat_len))
    seq_last = jnp.where(in_ctx, ctx_len - 1, lat_len - 1)
    x_prev = jnp.where(pos == 0, 0.0, jnp.where(row == 0, p_before, pltpu.roll(p, 1, axis=0)))
    x_next = jnp.where(pos == seq_last, 0.0, jnp.where(row == tm - 1, p_after, pltpu.roll(p, tm - 1, axis=0)))
    cw = cw_ref[...]
    y = _silu(x_prev * cw[0:1, :] + p * cw[1:2, :] + x_next * cw[2:3, :])
    n_qk = 2 * H_A
    for hh in range(n_conv // LANES):
        yh = y[:, hh * LANES:(hh + 1) * LANES]
        if hh < n_qk:
            yh = yh * lax.rsqrt(jnp.sum(yh * yh, axis=-1, keepdims=True) + EPS)
            if hh < n_qk // 2:
                yh = yh * (DK_A ** -0.5)
        qkv_ref[:, hh * LANES:(hh + 1) * LANES] = yh
    z_ref[...] = _dot(h, w_ref[:, n_conv:n_conv + n_z])
    s = _dot(h, w_ref[:, n_conv + n_z:n_conv + n_z + LANES])
    lane = lax.broadcasted_iota(jnp.int32, (1, LANES), 1)
    zg = s + dt_ref[...]
    softplus = jnp.maximum(zg, 0.0) + jnp.log(1.0 + jnp.exp(-jnp.abs(zg)))
    gates = jnp.where((lane % 4) < 2, -jnp.exp(alog_ref[...]) * softplus, jax.nn.sigmoid(s))
    gc_ref[...] = gates
    gr_ref[...] = jnp.transpose(gates)[:gr_ref.shape[0], :]
    n_main = n_conv + n_z + LANES
    cos, sin = cos_ref[...], sin_ref[...]
    for part in range(3):
        for c0 in range(0, hb2, PROJ_CHUNK):
            c1 = min(c0 + PROJ_CHUNK, hb2)
            y = _dot(h, w_ref[:, n_main + part * hb2 + c0:n_main + part * hb2 + c1])
            if part < 2:
                y = _rope(y, cos, sin)
            if part == 0:
                y = y * q_scale
            else:
                kv_ref[:, (part - 1) * hb2 + c0:(part - 1) * hb2 + c1] = y
            att_ref[:, part * hb2 + c0:part * hb2 + c1] = y.astype(BF16)


def _even_proj(x, norm_w, mods, w, conv_w, a_log, dt_bias, rope, rows, n_conv, n_z, hb2, q_scale, ctx_len,
               lat_len, n_ctx_rows):
    n, d = x.shape
    tm = rows.tm
    halo = 2 * SUBLANES
    hb = tm // halo
    alog_row = jnp.zeros((1, LANES), F32).at[0, :4 * H_A].set(
        jnp.stack([a_log[0], a_log[1], a_log[0], a_log[1]], axis=-1).reshape(-1).astype(F32))
    dt_row = jnp.zeros((1, LANES), F32).at[0, :4 * H_A].set(
        jnp.stack([dt_bias[0], dt_bias[1], jnp.zeros_like(dt_bias[0]), jnp.zeros_like(dt_bias[0])],
                  axis=-1).reshape(-1).astype(F32))
    kern = functools.partial(_even_proj_kernel, rows=rows, n_conv=n_conv, n_z=n_z, hb2=hb2, q_scale=q_scale,
                             ctx_len=ctx_len, lat_len=lat_len, n_ctx_rows=n_ctx_rows)
    tab = pl.BlockSpec((tm, LANES), lambda i: (rows.pos_block(i), 0))
    full = lambda a: pl.BlockSpec(a.shape, lambda i: (0, 0))
    nw = norm_w.reshape(1, d).astype(F32)
    cw = conv_w.astype(F32)
    outs = [(n_conv, F32), (n_z, F32), (LANES, F32), None, (3 * hb2, BF16), (2 * hb2, F32)]
    out_specs = [pl.BlockSpec((tm, c[0]), lambda i: (i, 0)) if c else pl.BlockSpec((4 * H_A, tm), lambda i: (0, i))
                 for c in outs]
    out_shape = [jax.ShapeDtypeStruct((n, c[0]), c[1]) if c else jax.ShapeDtypeStruct((4 * H_A, n), F32)
                 for c in outs]
    return pl.pallas_call(
        kern, grid=(n // tm,),
        in_specs=[pl.BlockSpec((tm, d), lambda i: (i, 0)),
                  pl.BlockSpec((halo, d), lambda i: (jnp.maximum(i * hb - 1, 0), 0)),
                  pl.BlockSpec((halo, d), lambda i: (jnp.minimum((i + 1) * hb, n // halo - 1), 0)),
                  full(nw), full(mods), full(w), full(cw), full(alog_row), full(dt_row), tab, tab],
        out_specs=out_specs, out_shape=out_shape,
        compiler_params=_cparams("parallel"), name="even_proj",
    )(x, x, x, nw, mods, w, cw, alog_row, dt_row, *rope)


def _odd_proj_kernel(x_ref, nw_ref, mods_ref, w_ref, qn_ref, kvn_ref, wq_ref, wkv_ref, cos_ref, sin_ref,
                     q_ref, kv_ref, kr_ref, ckv_ref, krf_ref, *, rows, scale, n_nope):
    i = pl.program_id(0)
    d = x_ref.shape[1]
    h = _modulated(x_ref, nw_ref, mods_ref, rows.group(i), 0, d).astype(BF16)
    p = _dot(h, w_ref[...])
    cos, sin = cos_ref[...], sin_ref[...]
    cq = _rmsnorm_rows(p[:, :Q_LORA], qn_ref[...]).astype(BF16)
    for c0 in range(0, wq_ref.shape[1], PROJ_CHUNK):
        y = _dot(cq, wq_ref[:, c0:c0 + PROJ_CHUNK])
        if c0 >= n_nope:
            y = _rope(y, cos, sin)
        q_ref[:, c0:c0 + PROJ_CHUNK] = (y * scale).astype(BF16)
    ckv = _rmsnorm_rows(p[:, Q_LORA:Q_LORA + KV_LORA], kvn_ref[...])
    ckv_ref[...] = ckv
    ckv16 = ckv.astype(BF16)
    for c0 in range(0, wkv_ref.shape[1], PROJ_CHUNK):
        kv_ref[:, c0:c0 + PROJ_CHUNK] = _dot(ckv16, wkv_ref[:, c0:c0 + PROJ_CHUNK]).astype(BF16)
    kr = _rope(p[:, Q_LORA + KV_LORA:], cos, sin)
    krf_ref[...] = kr
    kr_ref[...] = kr.astype(BF16)


def _odd_proj(x, norm_w, mods, w_in, q_norm, kv_norm, wq, wkv, rope, rows, scale, n_nope):
    n, d = x.shape
    tm = rows.tm
    tab = pl.BlockSpec((tm, LANES), lambda i: (rows.pos_block(i), 0))
    full = lambda a: pl.BlockSpec(a.shape, lambda i: (0, 0))
    qn = q_norm.reshape(1, -1).astype(F32)
    kvn = kv_norm.reshape(1, -1).astype(F32)
    nw = norm_w.reshape(1, d).astype(F32)
    outs = [(wq.shape[1], BF16), (wkv.shape[1], BF16), (LANES, BF16), (KV_LORA, F32), (LANES, F32)]
    return pl.pallas_call(
        functools.partial(_odd_proj_kernel, rows=rows, scale=scale, n_nope=n_nope), grid=(n // tm,),
        in_specs=[pl.BlockSpec((tm, d), lambda i: (i, 0)), full(nw), full(mods), full(w_in), full(qn), full(kvn),
                  full(wq), full(wkv), tab, tab],
        out_specs=[pl.BlockSpec((tm, c), lambda i: (i, 0)) for c, _ in outs],
        out_shape=[jax.ShapeDtypeStruct((n, c), dt) for c, dt in outs],
        compiler_params=_cparams("parallel"), name="odd_proj",
    )(x, nw, mods, w_in, qn, kvn, wq, wkv, *rope)


def _split2(x):
    hi = x.astype(BF16)
    return hi, (x - hi.astype(F32)).astype(BF16)


def _mm(a, b, precise, dot=_dot):
    if not precise:
        return dot(a.astype(BF16), b.astype(BF16))
    ah, al = _split2(a)
    bh, bl = _split2(b)
    return dot(ah, bh) + (dot(ah, bl) + dot(al, bh))


SOLVE_BLOCK = SUBLANES
MOE_ROW_BLOCK = 256
FLASH_HEADS_PER_STEP = 2
DELTA_HEADS_PER_STEP = 4


def _tri_solve_many(ms, rs, revs):
    c, width = rs[0].shape
    blk = SOLVE_BLOCK
    nb, gpb, ng = c // blk, blk // SUBLANES, c // SUBLANES
    col = lax.broadcasted_iota(jnp.int32, (blk, c), 1)
    mgs = [[m[SUBLANES * g:SUBLANES * (g + 1), :] for g in range(ng)] for m in ms]
    xgs = [[r[SUBLANES * g:SUBLANES * (g + 1), :] for g in range(ng)] for r in rs]
    zero_blk = jnp.zeros((blk, width), BF16)
    fin_h = [[zero_blk] * nb for _ in ms]
    fin_l = [[zero_blk] * nb for _ in ms]
    for bi in range(nb):
        blocks = [nb - 1 - bi if rev else bi for rev in revs]
        if bi > 0:
            for s, (rev, b) in enumerate(zip(revs, blocks)):
                done = (col >= (b + 1) * blk) if rev else (col < b * blk)
                lh, ll = _split2(jnp.where(done, ms[s][b * blk:(b + 1) * blk, :], 0.0))
                xh = jnp.concatenate(fin_h[s], axis=0)
                xl = jnp.concatenate(fin_l[s], axis=0)
                upd = _dot(lh, xh) + (_dot(lh, xl) + _dot(ll, xh))
                for gg in range(gpb):
                    g = b * gpb + gg
                    xgs[s][g] = xgs[s][g] - upd[SUBLANES * gg:SUBLANES * (gg + 1), :]
        for t in range(blk - 1):
            for s, (rev, b) in enumerate(zip(revs, blocks)):
                j = b * blk + (blk - 1 - t if rev else t)
                xj = xgs[s][j // SUBLANES][j % SUBLANES:j % SUBLANES + 1, :]
                groups = range(b * gpb, (j - 1) // SUBLANES + 1) if rev else range((j + 1) // SUBLANES, (b + 1) * gpb)
                for g in groups:
                    xgs[s][g] = xgs[s][g] - mgs[s][g][:, j:j + 1] * xj
        if bi < nb - 1:
            for s, b in enumerate(blocks):
                fin_h[s][b], fin_l[s][b] = _split2(jnp.concatenate(xgs[s][b * gpb:(b + 1) * gpb], axis=0))
    return [jnp.concatenate(xg, axis=0) for xg in xgs]


def _delta_kernel(qf, kf, vf, gcf, grf, qb, kb, vb, gcb, grb, s0_ref, *rest, chunk, hpb, n_alias, precise):
    of_ref, ob_ref, so_ref, s_scr = rest[n_alias:]
    h0 = pl.program_id(1) * hpb
    i = pl.program_id(2)
    n_i = pl.num_programs(2)

    @pl.when(i == 0)
    def _():
        s_scr[...] = s0_ref[...]

    rg = qf.shape[0]
    nc = rg // chunk
    rowi = lax.broadcasted_iota(jnp.int32, (rg, rg), 0)
    coli = lax.broadcasted_iota(jnp.int32, (rg, rg), 1)
    same = (rowi // chunk) == (coli // chunk)
    lane = lax.broadcasted_iota(jnp.int32, (1, LANES), 1)
    r_c = lax.broadcasted_iota(jnp.int32, (chunk, chunk), 0)
    c_c = lax.broadcasted_iota(jnp.int32, (chunk, chunk), 1)

    prep = []
    for d, (q_ref, k_ref, v_ref, gc_ref, gr_ref) in enumerate(((qf, kf, vf, gcf, grf), (qb, kb, vb, gcb, grb))):
        rev = d == 1
        incl_big = same & ((rowi <= coli) if rev else (rowi >= coli))
        lm = jnp.where(incl_big, 1.0, 0.0).astype(BF16)
        gcols = gc_ref[...]
        grows = gr_ref[...]
        gh, gm, gl_ = _split3(gcols)
        cum_c = _dot(lm, gh) + _dot(lm, gm) + _dot(lm, gl_)
        th, tm_, tl = _split3(grows)
        cum_r = _dot_nt(th, lm) + _dot_nt(tm_, lm) + _dot_nt(tl, lm)

        def col(a, ln):
            return jnp.sum(jnp.where(lane == ln, a, 0.0), axis=1, keepdims=True)

        sub = lax.broadcasted_iota(jnp.int32, (cum_r.shape[0], 1), 0)
        incl = (r_c <= c_c) if rev else (r_c >= c_c)
        strict = (r_c < c_c) if rev else (r_c > c_c)
        for hh in range(hpb):
            h = h0 + hh
            hs = slice(hh * LANES, (hh + 1) * LANES)
            gcum = col(cum_c, 4 * h + d)
            beta = col(gcols, 4 * h + 2 + d)
            grow = jnp.sum(jnp.where(sub == 4 * h + d, cum_r, 0.0), axis=0, keepdims=True)
            q = q_ref[:, hs]
            k = k_ref[:, hs]
            v = v_ref[:, hs]
            kbeta = k * beta
            eg = jnp.exp(gcum)
            rhs = jnp.concatenate([v * beta, kbeta * eg], axis=1)
            ms, rs, a_in = [], [], []
            for c in range(nc):
                sl = slice(c * chunk, (c + 1) * chunk)
                e = jnp.exp(jnp.where(incl, gcum[sl] - grow[:, sl], 0.0))
                kk = _mm(kbeta[sl], k[sl], precise, _dot_nt)
                qk = _mm(q[sl], k[sl], precise, _dot_nt)
                a_in.append(jnp.where(incl, qk * e, 0.0))
                ms.append(jnp.where(strict, kk * e, 0.0))
                rs.append(rhs[sl])
            prep.append((rev, hh, ms, rs, a_in, q * eg, k, gcum))

    all_terms = []
    half = len(prep) // 2
    for wave in (prep[:half], prep[half:]) if half else (prep,):
        sols = _tri_solve_many([m for p in wave for m in p[2]], [r for p in wave for r in p[3]],
                               [p[0] for p in wave for _ in range(nc)])
        for n_u, (rev, hh, _, _, a_in, qg, k, gcum) in enumerate(wave):
            xs = sols[n_u * nc:(n_u + 1) * nc]
            terms = []
            for c in range(nc):
                sl = slice(c * chunk, (c + 1) * chunk)
                r_last = c * chunk if rev else (c + 1) * chunk - 1
                g_last = gcum[r_last:r_last + 1, :]
                k_dec = k[sl] * jnp.exp(g_last - gcum[sl])
                a_uw = _mm(a_in[c], xs[c], precise)
                kd_uw = _mm(k_dec, xs[c], precise, _dot_tn)
                lhs = jnp.concatenate([qg[sl] - a_uw[:, DV_A:], kd_uw[:, DV_A:]], axis=0)
                terms.append((lhs, a_uw[:, :DV_A], kd_uw[:, :DV_A], jnp.exp(g_last)))
            all_terms.append(terms)

    states = [s_scr[1 if p[0] else 0, p[1]] for p in prep]
    outs = [[None] * nc for _ in prep]
    for t in range(nc):
        for n_u, p in enumerate(prep):
            c = nc - 1 - t if p[0] else t
            lhs, o_0, q_0, decay = all_terms[n_u][c]
            prod = _mm(lhs, states[n_u], precise)
            outs[n_u][c] = o_0 + prod[:chunk]
            states[n_u] = states[n_u] * decay - prod[chunk:] + q_0
    for n_u, (rev, hh) in enumerate((p[0], p[1]) for p in prep):
        s_scr[1 if rev else 0, hh] = states[n_u]
        (ob_ref if rev else of_ref)[:, hh * LANES:(hh + 1) * LANES] = jnp.concatenate(outs[n_u], axis=0)

    @pl.when(i == n_i - 1)
    def _():
        so_ref[...] = s_scr[...]


def _delta_stage(qkvn, gcol, grow, s0, n_seq, seq_len, row0, rg, into=None, precise=False):
    ng = seq_len // rg
    b0 = row0 // rg

    def fwd_blk(b, i):
        return b0 + b * ng + i

    def bwd_blk(b, i):
        return b0 + b * ng + (ng - 1 - i)

    hpb = DELTA_HEADS_PER_STEP
    nhp = H_A // hpb
    hw = hpb * LANES

    def specs(blk):
        return [pl.BlockSpec((rg, hw), lambda b, h, i: (blk(b, i), h)),
                pl.BlockSpec((rg, hw), lambda b, h, i: (blk(b, i), nhp + h)),
                pl.BlockSpec((rg, hw), lambda b, h, i: (blk(b, i), 2 * nhp + h)),
                pl.BlockSpec((rg, LANES), lambda b, h, i: (blk(b, i), 0)),
                pl.BlockSpec((4 * H_A, rg), lambda b, h, i: (0, blk(b, i)))]

    st_spec = pl.BlockSpec((None, 2, hpb, DK_A, DV_A), lambda b, h, i: (b, 0, h, 0, 0))
    args = [qkvn, qkvn, qkvn, gcol, grow, qkvn, qkvn, qkvn, gcol, grow, s0]
    in_specs = specs(fwd_blk) + specs(bwd_blk) + [st_spec]
    aliases = {}
    if into is not None:
        for k_out, arr in enumerate(into):
            aliases[len(args)] = k_out
            args.append(arr)
            in_specs.append(pl.BlockSpec(memory_space=pl.ANY))
    n = qkvn.shape[0]
    return pl.pallas_call(
        functools.partial(_delta_kernel, chunk=CHUNK, hpb=hpb, n_alias=len(aliases), precise=precise),
        grid=(n_seq, nhp, ng),
        in_specs=in_specs,
        out_specs=[pl.BlockSpec((rg, hw), lambda b, h, i: (fwd_blk(b, i), h)),
                   pl.BlockSpec((rg, hw), lambda b, h, i: (bwd_blk(b, i), h)),
                   st_spec],
        out_shape=[jax.ShapeDtypeStruct((n, H_A * DV_A), F32),
                   jax.ShapeDtypeStruct((n, H_A * DV_A), F32),
                   jax.ShapeDtypeStruct(s0.shape, F32)],
        input_output_aliases=aliases,
        scratch_shapes=[pltpu.VMEM((2, hpb, DK_A, DV_A), F32)],
        compiler_params=_cparams("parallel", "parallel", "arbitrary"), name="gated_delta",
    )(*args)


def _flash_kernel(*refs, n_parts, has_cache, diff, tk, lam_init, aliased, head_lanes):
    it = iter(refs)
    q_refs = [next(it) for _ in range(n_parts)]
    k_refs = [next(it) for _ in range(n_parts)]
    v_ref = next(it)
    kc_refs = [next(it) for _ in range(n_parts)] if has_cache else []
    vc_ref = next(it) if has_cache else None
    lam_ref = next(it) if diff else None
    if aliased:
        next(it)
    o_ref = next(it)
    tq = q_refs[0].shape[0]
    lk = k_refs[0].shape[0]
    if diff:
        lp = lam_ref[...]
        lam = (jnp.exp(jnp.sum(lp[0:1] * lp[1:2], axis=-1, keepdims=True))
               - jnp.exp(jnp.sum(lp[2:3] * lp[3:4], axis=-1, keepdims=True)) + lam_init)
        lane = lax.broadcasted_iota(jnp.int32, (1, LANES), 1)

    def cols(ref, off, rows=None):
        return ref[:, off:off + LANES] if rows is None else ref[rows, off:off + LANES]

    def block(carry, q_cat, ks, vv):
        m, l, acc = carry
        s = _dot_nt(q_cat, ks[0] if n_parts == 1 else jnp.concatenate(ks, axis=1))
        m_new = jnp.maximum(m, jnp.max(s, axis=-1, keepdims=True))
        alpha = jnp.exp2(m - m_new)
        p_ = jnp.exp2(s - m_new)
        l = alpha * l + jnp.sum(p_, axis=-1, keepdims=True)
        acc = alpha * acc + _dot(p_.astype(BF16), vv)
        return m_new, l, acc

    q_cats, carries = [], []
    for q_offs, *_ in head_lanes:
        qs = [cols(r, off) for r, off in zip(q_refs, q_offs)]
        if diff:
            q = qs[0]
            qs = [jnp.concatenate([jnp.where(lane < DQK_B, q, 0), jnp.where(lane >= DQK_B, q, 0)], axis=0)]
        nrow = qs[0].shape[0]
        q_cats.append(qs[0] if n_parts == 1 else jnp.concatenate(qs, axis=1))
        carries.append((jnp.full((nrow, 1), -1e30, F32), jnp.zeros((nrow, 1), F32), jnp.zeros((nrow, LANES), F32)))

    for t in range(lk // tk):
        rows = slice(t * tk, (t + 1) * tk)
        for n_h, (_, k_offs, v_off, _, _, _) in enumerate(head_lanes):
            carries[n_h] = block(carries[n_h], q_cats[n_h], [cols(r, off, rows) for r, off in zip(k_refs, k_offs)],
                                 cols(v_ref, v_off, rows))
    for n_h, (_, _, _, o_off, kc_offs, vc_off) in enumerate(head_lanes):
        carry = carries[n_h]
        if has_cache:
            carry = block(carry, q_cats[n_h], [cols(r, off) for r, off in zip(kc_refs, kc_offs)], cols(vc_ref, vc_off))
        _, l, acc = carry
        o = acc / l
        if diff:
            o = o[:tq] - lam * o[tq:]
        o_ref[:, o_off:o_off + LANES] = o.astype(o_ref.dtype)


def _flash(q_parts, k_parts, v_part, cache, lam, *, n_seq, n_heads, lq, lk, q_row0, k_row0, tq, tk, diff,
           lam_init, out_cols, into=None, heads_per_step=1):
    nqt = lq // tq
    qb0, kb0 = q_row0 // tq, k_row0 // lk
    hps = heads_per_step
    assert n_heads % hps == 0
    args, in_specs = [], []

    def add(arr, cf, n_rows, row_fn):
        span = 1
        while any(cf(g * hps) // span != cf(g * hps + hps - 1) // span for g in range(n_heads // hps)):
            span *= 2
        offs = tuple((cf(j) - cf(0) // span * span) * LANES for j in range(hps))
        for g in range(n_heads // hps):
            assert offs == tuple((cf(g * hps + j) - cf(g * hps) // span * span) * LANES for j in range(hps))
        args.append(arr)
        in_specs.append(pl.BlockSpec((n_rows, span * LANES),
                                     lambda b, hg, t: (row_fn(b, t), cf(hg * hps) // span)))
        return offs

    q_offs = [add(arr, cf, tq, lambda b, t: qb0 + b * nqt + t) for arr, cf in q_parts]
    k_offs = [add(arr, cf, lk, lambda b, t: kb0 + b) for arr, cf in k_parts]
    v_offs = add(v_part[0], v_part[1], lk, lambda b, t: kb0 + b)
    kc_offs, vc_offs = [(0,) * hps] * len(q_parts), (0,) * hps
    if cache is not None:
        c_k, c_v, c_rows = cache
        kc_offs = [add(arr, cf, c_rows, lambda b, t: b) for arr, cf in c_k]
        vc_offs = add(c_v[0], c_v[1], c_rows, lambda b, t: b)
    head_lanes = tuple((tuple(o[j] for o in q_offs), tuple(o[j] for o in k_offs), v_offs[j], j * LANES,
                        tuple(o[j] for o in kc_offs), vc_offs[j]) for j in range(hps))
    grid = (n_seq, n_heads // hps, nqt)
    out_spec = pl.BlockSpec((tq, hps * LANES), lambda b, hg, t: (qb0 + b * nqt + t, hg))
    if diff:
        args.append(lam)
        in_specs.append(pl.BlockSpec(lam.shape, lambda b, h, t: (0, 0)))
    aliases = {}
    if into is not None:
        aliases[len(args)] = 0
        args.append(into)
        in_specs.append(pl.BlockSpec(memory_space=pl.ANY))
    kern = functools.partial(_flash_kernel, n_parts=len(q_parts), has_cache=cache is not None, diff=diff, tk=tk,
                             lam_init=lam_init, aliased=into is not None, head_lanes=head_lanes)
    return pl.pallas_call(
        kern, grid=grid, in_specs=in_specs, out_specs=out_spec,
        out_shape=jax.ShapeDtypeStruct((q_parts[0][0].shape[0], out_cols), F32 if diff else BF16),
        input_output_aliases=aliases,
        compiler_params=_cparams("parallel", "parallel", "arbitrary"), name="flash_attention",
    )(*args)


def _mr_kernel(x_ref, a_ref, w_ref, mods_ref, o_ref, *, rows, gate_idx):
    i = pl.program_id(0)
    d = x_ref.shape[1]
    gate = _mod_chunk(mods_ref, rows.group(i), gate_idx, d)
    o_ref[...] = x_ref[...] + gate * _dot(a_ref[...].astype(BF16), w_ref[...])


def _matmul_residual(x, a, w, mods, rows, gate_idx):
    n, d = x.shape
    tm = rows.tm
    return pl.pallas_call(
        functools.partial(_mr_kernel, rows=rows, gate_idx=gate_idx), grid=(n // tm,),
        in_specs=[pl.BlockSpec((tm, d), lambda i: (i, 0)),
                  pl.BlockSpec((tm, a.shape[1]), lambda i: (i, 0)),
                  pl.BlockSpec(w.shape, lambda i: (0, 0)),
                  pl.BlockSpec(mods.shape, lambda i: (0, 0))],
        out_specs=pl.BlockSpec((tm, d), lambda i: (i, 0)),
        out_shape=jax.ShapeDtypeStruct((n, d), F32),
        compiler_params=_cparams("parallel"), name="matmul_residual",
    )(x, a, w, mods)


def _modulated(x_ref, nw_ref, mods_ref, g, idx, d):
    x = _rmsnorm_rows(x_ref[...], nw_ref[...])
    return x * (1.0 + _mod_chunk(mods_ref, g, idx + 1, d)) + _mod_chunk(mods_ref, g, idx, d)


def _ffn_kernel(x_ref, nw_ref, mods_ref, wg_ref, wu_ref, wd_ref, of_ref, ob_ref, z_ref, od_ref, on_ref, sn_ref,
                wo_ref, o_ref, h_scr, acc_scr, xn_scr, *, rows, mod0, mix_gate, lam_init):
    i, f = pl.program_id(0), pl.program_id(1)
    d = x_ref.shape[1]
    g = rows.group(i)

    @pl.when(f == 0)
    def _():
        oa = of_ref[...] + ob_ref[...]
        z = z_ref[...]
        od = od_ref[...]
        ya, yb = [], []
        for hh in range(H_A):
            sl = slice(hh * LANES, (hh + 1) * LANES)
            ya.append(_rmsnorm_rows(oa[:, sl], on_ref[...]) * _silu(z[:, sl]))
        for hh in range(H_B):
            sl = slice(hh * LANES, (hh + 1) * LANES)
            yb.append(_rmsnorm_rows(od[:, sl], sn_ref[...]) * (1.0 - lam_init))
        y = jnp.concatenate(ya + yb, axis=1).astype(BF16)
        x1 = x_ref[...] + _mod_chunk(mods_ref, g, mix_gate, d) * _dot(y, wo_ref[...])
        xn_scr[...] = x1
        v = _rmsnorm_rows(x1, nw_ref[...])
        h_scr[...] = (v * (1.0 + _mod_chunk(mods_ref, g, mod0 + 1, d)) + _mod_chunk(mods_ref, g, mod0, d)).astype(BF16)
        acc_scr[...] = jnp.zeros_like(acc_scr)

    h = h_scr[...]
    act = _silu(_dot(h, wg_ref[...])) * _dot(h, wu_ref[...])
    acc_scr[...] += _dot(act.astype(BF16), wd_ref[...])

    @pl.when(f == pl.num_programs(1) - 1)
    def _():
        o_ref[...] = xn_scr[...] + _mod_chunk(mods_ref, g, mod0 + 2, d) * acc_scr[...]


def _even_tail(x, o_f, o_b, z, o_d, onorm, subln, w_out, norm_w, mods, w_gu, w_down, rows, mix_gate, mod0, tf,
               lam_init):
    n, d = x.shape
    tm = rows.tm
    ff = w_down.shape[0]
    nf = ff // tf
    hw = H_A * DV_A
    row = lambda c: pl.BlockSpec((tm, c), lambda i, f: (i, 0))
    one = lambda c: pl.BlockSpec((1, c), lambda i, f: (0, 0))
    return pl.pallas_call(
        functools.partial(_ffn_kernel, rows=rows, mod0=mod0, mix_gate=mix_gate, lam_init=lam_init),
        grid=(n // tm, nf),
        in_specs=[row(d), one(d), pl.BlockSpec(mods.shape, lambda i, f: (0, 0)),
                  pl.BlockSpec((d, tf), lambda i, f: (0, f)),
                  pl.BlockSpec((d, tf), lambda i, f: (0, nf + f)),
                  pl.BlockSpec((tf, d), lambda i, f: (f, 0)),
                  row(hw), row(hw), row(hw), row(hw), one(LANES), one(LANES),
                  pl.BlockSpec(w_out.shape, lambda i, f: (0, 0))],
        out_specs=row(d),
        out_shape=jax.ShapeDtypeStruct((n, d), F32),
        scratch_shapes=[pltpu.VMEM((tm, d), BF16), pltpu.VMEM((tm, d), F32), pltpu.VMEM((tm, d), F32)],
        compiler_params=_cparams("parallel", "arbitrary"), name="even_tail",
    )(x, norm_w.reshape(1, d).astype(F32), mods, w_gu, w_gu, w_down, o_f, o_b, z, o_d,
      onorm.reshape(1, LANES).astype(F32), subln.reshape(1, LANES).astype(F32), w_out)


def _moe_kernel(x_ref, nw_ref, mods_ref, rw_ref, wgu_ref, wd_ref, *rest, rows, mod0, n_exp, sb, final):
    if final:
        fw_ref, o_ctx_ref, o_lat_ref, h_scr, g_scr, rk_scr, rkt_scr, acc_scr = rest
    else:
        o_ref, h_scr, g_scr, rk_scr, rkt_scr, acc_scr = rest
    i, e = pl.program_id(0), pl.program_id(1)
    tm, d = x_ref.shape
    g = rows.group(i)
    lane = lax.broadcasted_iota(jnp.int32, (1, LANES), 1)

    @pl.when(e == 0)
    def _():
        h = _modulated(x_ref, nw_ref, mods_ref, g, mod0, d)
        h_scr[...] = h.astype(BF16)
        acc_scr[...] = jnp.zeros_like(acc_scr)
        logits = jnp.where(lane < n_exp, _mm(h, rw_ref[...], True), -1e30)
        pe = jnp.exp(logits - jnp.max(logits, axis=-1, keepdims=True))
        probs = pe / jnp.sum(pe, axis=-1, keepdims=True)
        p1 = jnp.max(probs, axis=-1, keepdims=True)
        i1 = jnp.min(jnp.where(probs == p1, lane, LANES), axis=-1, keepdims=True)
        rest = jnp.where(lane == i1, -1.0, probs)
        p2 = jnp.max(rest, axis=-1, keepdims=True)
        i2 = jnp.min(jnp.where(rest == p2, lane, LANES), axis=-1, keepdims=True)
        den = p1 + p2
        gates = jnp.where(lane == i1, p1 / den, 0.0) + jnp.where(lane == i2, p2 / den, 0.0)
        g_hi = gates.astype(BF16).astype(F32)
        g_scr[...] = (g_hi + pltpu.roll(gates - g_hi, n_exp, axis=1)).astype(BF16)
        t_r = lax.broadcasted_iota(jnp.int32, (tm, tm), 0)
        t_c = lax.broadcasted_iota(jnp.int32, (tm, tm), 1)
        earlier = jnp.where(t_r > t_c, 1.0, 0.0).astype(BF16)
        sel = gates > 0.0
        rank = jnp.where(sel, _dot(earlier, jnp.where(sel, 1.0, 0.0).astype(BF16)), -1.0)
        rkt_scr[...] = jnp.transpose(rank)[:rkt_scr.shape[0], :]
        for ex in range(n_exp):
            rk_scr[ex] = jnp.broadcast_to(rank[:, ex:ex + 1], (tm, LANES))

    ff = wd_ref.shape[0]
    rk_row = rkt_scr[pl.ds(e, 1), :]
    n_rows = (jnp.max(rk_row) + 1.0).astype(jnp.int32)
    lane_f = lane.astype(F32)

    def expert_rows(slot0, nr):
        base = slot0.astype(F32)
        r_sub = lax.broadcasted_iota(jnp.int32, (nr, 1), 0).astype(F32)
        pick = jnp.where(rk_row == r_sub + base, 1.0, 0.0).astype(BF16)
        xg = _dot(pick, h_scr[...]).astype(BF16)
        gr = _dot(pick, g_scr[...])
        gate_r = jnp.sum(jnp.where((lane == e) | (lane == n_exp + e), gr, 0.0), axis=-1, keepdims=True)
        gu = _dot(xg, wgu_ref[...])
        act = _silu(gu[:, :ff]) * gu[:, ff:]
        y = _dot(act.astype(BF16), wd_ref[...])
        rk_rep = rk_scr[e]
        put = [jnp.where(rk_rep == lane_f + (base + c0), 1.0, 0.0).astype(BF16) for c0 in range(0, nr, LANES)]
        put = put[0][:, :nr] if nr < LANES else jnp.concatenate(put, axis=1)
        acc_scr[...] += _dot(put, (y * gate_r).astype(BF16))

    half, quarter = sb // 2, sb // 4
    rem = n_rows % sb
    n_full = n_rows // sb + jnp.where(rem > half, 1, 0)

    def full_body(jb, carry):
        expert_rows(jb * sb, sb)
        return carry

    lax.fori_loop(0, n_full, full_body, 0)

    @pl.when((rem > quarter) & (rem <= half))
    def _():
        expert_rows(n_full * sb, half)

    @pl.when((rem > 0) & (rem <= quarter))
    def _():
        expert_rows(n_full * sb, quarter)

    def result():
        return x_ref[...] + _mod_chunk(mods_ref, g, mod0 + 2, d) * acc_scr[...]

    last = e == n_exp - 1
    if final:
        @pl.when(last & (i < rows.ctx_tiles))
        def _():
            o_ctx_ref[...] = _rmsnorm_rows(result(), fw_ref[...])

        @pl.when(last & (i >= rows.ctx_tiles))
        def _():
            o_lat_ref[...] = _rmsnorm_rows(result(), fw_ref[...])
    else:
        @pl.when(last)
        def _():
            o_ref[...] = result()


def _moe(x, norm_w, mods, router_w, w_gu, w_down, rows, mod0, final_w=None):
    n, d = x.shape
    tm = rows.tm
    n_exp, _, ff2 = w_gu.shape
    rw = jnp.zeros((d, LANES), F32).at[:, :n_exp].set(router_w.astype(F32))
    final = final_w is not None
    args = [x, norm_w.reshape(1, d).astype(F32), mods, rw, w_gu, w_down]
    in_specs = [pl.BlockSpec((tm, d), lambda i, e: (i, 0)),
                pl.BlockSpec((1, d), lambda i, e: (0, 0)),
                pl.BlockSpec(mods.shape, lambda i, e: (0, 0)),
                pl.BlockSpec((d, LANES), lambda i, e: (0, 0)),
                pl.BlockSpec((None, d, ff2), lambda i, e: (e, 0, 0)),
                pl.BlockSpec((None, ff2 // 2, d), lambda i, e: (e, 0, 0))]
    if final:
        args.append(final_w.reshape(1, d).astype(F32))
        in_specs.append(pl.BlockSpec((1, d), lambda i, e: (0, 0)))
        ct = rows.ctx_tiles
        out_specs = [pl.BlockSpec((tm, d), lambda i, e: (jnp.minimum(i, ct - 1), 0), pipeline_mode=pl.Buffered(1)),
                     pl.BlockSpec((tm, d), lambda i, e: (jnp.maximum(i - ct, 0), 0), pipeline_mode=pl.Buffered(1))]
        out_shape = [jax.ShapeDtypeStruct((ct * tm, d), F32), jax.ShapeDtypeStruct((n - ct * tm, d), F32)]
    else:
        out_specs = pl.BlockSpec((tm, d), lambda i, e: (i, 0))
        out_shape = jax.ShapeDtypeStruct((n, d), F32)
    return pl.pallas_call(
        functools.partial(_moe_kernel, rows=rows, mod0=mod0, n_exp=n_exp, sb=MOE_ROW_BLOCK, final=final),
        grid=(n // tm, n_exp), in_specs=in_specs, out_specs=out_specs, out_shape=out_shape,
        scratch_shapes=[pltpu.VMEM((tm, d), BF16), pltpu.VMEM((tm, LANES), BF16), pltpu.VMEM((n_exp, tm, LANES), F32),
                        pltpu.VMEM((SUBLANES * ((n_exp + SUBLANES - 1) // SUBLANES), tm), F32),
                        pltpu.VMEM((tm, d), F32)],
        compiler_params=_cparams("arbitrary", "arbitrary"), name="moe",
    )(*args)


def _final_kernel(x_ref, w_ref, o_ref):
    o_ref[...] = _rmsnorm_rows(x_ref[...], w_ref[...])


def _final_norm(x, w, tm):
    n, d = x.shape
    return pl.pallas_call(
        _final_kernel, grid=(n // tm,),
        in_specs=[pl.BlockSpec((tm, d), lambda i: (i, 0)), pl.BlockSpec((1, d), lambda i: (0, 0))],
        out_specs=pl.BlockSpec((tm, d), lambda i: (i, 0)),
        out_shape=jax.ShapeDtypeStruct((n, d), F32),
        compiler_params=_cparams("parallel"), name="final_norm",
    )(x, w.reshape(1, d).astype(F32))


def kernel(x_prompt, x_sample, state_delta, cache_diff_k, cache_diff_v, cache_mla_ckv, cache_mla_krope, c, c_ctx,
           mod_w, mod_b, norm_mix, norm_ffn, final_norm, ev_w_in, ev_conv_w, ev_a_log, ev_dt_bias, ev_onorm,
           ev_lambda, ev_subln, ev_w_out, ffn_w_gu, ffn_w_down, od_w_in, od_q_norm, od_kv_norm, od_w_uq, od_w_ukv,
           od_w_out, moe_router, moe_w_gu, moe_w_down):
    bp, lp, d = x_prompt.shape
    bs, ls, _ = x_sample.shape
    past = cache_diff_k.shape[3]
    depth = mod_w.shape[0]
    n_p, n_s = bp * lp, bs * ls
    n = n_p + n_s
    assert n_p % ls == 0 and past == lp

    rows = _Rows(n_p, n_s, ls, _tile(math.gcd(n_p, ls), 512))
    rows_m = _Rows(n_p, n_s, ls, _tile(math.gcd(n_p, ls), 1024))
    rg = _tile(math.gcd(lp, ls), 256)
    cos_t, sin_t = _rope_tables(rows, ls)

    x = jnp.concatenate([x_prompt.reshape(n_p, d), x_sample.reshape(n_s, d)], axis=0)
    n_grp = 1 + bs
    cc = jnp.zeros((2 * SUBLANES * ((n_grp + 15) // 16), d), F32).at[0].set(c_ctx).at[1:n_grp].set(c)
    mods_all = _mods(cc, mod_w, mod_b)

    hq = H_A * DK_A
    hb2 = H_B * 2 * DQK_B
    sm0 = 4 * hq
    qb0 = sm0 + 4 * H_A
    q_scale = (DQK_B ** -0.5) * LOG2E
    mla_scale = ((NOPE_C + ROPE_C) ** -0.5) * LOG2E
    sd_new, dk_new, dv_new, ckv_new, kr_new = [], [], [], [], []

    for layer in range(depth):
        j = layer // 2
        mods = mods_all[layer]
        if layer % 2 == 0:
            lam_init = 0.8 - 0.6 * math.exp(-0.3 * layer)
            w = ev_w_in[j]
            sm = w[:, sm0:qb0].reshape(d, 4, H_A).transpose(0, 2, 1).reshape(d, 4 * H_A)
            w_in = jnp.concatenate([w[:, :sm0], sm, jnp.zeros((d, LANES - 4 * H_A), w.dtype), w[:, qb0:]],
                                   axis=1).astype(BF16)
            qkvn, z_gate, gcol, grow, att, kv_f = _even_proj(
                x, norm_mix[layer], mods, w_in, ev_conv_w[j], ev_a_log[j], ev_dt_bias[j], (cos_t, sin_t), rows,
                3 * hq, hq, hb2, q_scale, lp, ls, n_p)
            o_f, o_b, s_p = _delta_stage(qkvn, gcol, grow, jnp.zeros((bp, 2, H_A, DK_A, DV_A), F32), bp, lp, 0, rg,
                                         precise=True)
            o_f, o_b, _ = _delta_stage(qkvn, gcol, grow, state_delta[:, j].astype(F32), bs, ls, n_p, rg,
                                       into=(o_f, o_b))
            sd_new.append(s_p)
            lam_p = ev_lambda[j].astype(F32)
            nb = hb2 // LANES
            qp, kp, vp = [(att, lambda h: h)], [(att, lambda h: nb + h)], (att, lambda h: 2 * nb + h)
            o_d = _flash(qp, kp, vp, None, lam_p, n_seq=bp, n_heads=H_B, lq=lp, lk=lp, q_row0=0, k_row0=0,
                         tq=_tile(lp, 256), tk=_tile(lp, 512), diff=True, lam_init=lam_init, out_cols=hb2,
                         heads_per_step=H_B)
            ck = cache_diff_k[:, j].transpose(0, 2, 1, 3).reshape(bs * past, H_B * 2 * DQK_B).astype(BF16)
            cv = cache_diff_v[:, j].transpose(0, 2, 1, 3).reshape(bs * past, H_B * DV_B).astype(BF16)
            cache = ([(ck, lambda h: h)], (cv, lambda h: h), past)
            o_d = _flash(qp, kp, vp, cache, lam_p, n_seq=bs, n_heads=H_B, lq=ls, lk=ls, q_row0=n_p, k_row0=n_p,
                         tq=_tile(ls, 512), tk=_tile(ls, 2048), diff=True, lam_init=lam_init, out_cols=hb2, into=o_d,
                         heads_per_step=FLASH_HEADS_PER_STEP)
            dk_new.append(kv_f[:n_p, :hb2].reshape(bp, lp, H_B, 2 * DQK_B).transpose(0, 2, 1, 3))
            dv_new.append(kv_f[:n_p, hb2:].reshape(bp, lp, H_B, DV_B).transpose(0, 2, 1, 3))
            ff = ffn_w_down.shape[1]
            x = _even_tail(x, o_f, o_b, z_gate, o_d, ev_onorm[j], ev_subln[j], ev_w_out[j].astype(BF16),
                           norm_ffn[layer], mods, ffn_w_gu[j].astype(BF16), ffn_w_down[j].astype(BF16), rows, 2, 3,
                           ff // 2 if (ff // 2) % LANES == 0 else ff, lam_init)
        else:
            n_in = Q_LORA + KV_LORA + ROPE_C
            pad = (-n_in) % LANES
            w_in = jnp.concatenate([od_w_in[j], jnp.zeros((d, pad), od_w_in.dtype)], axis=1).astype(BF16)
            wq = od_w_uq[j].reshape(Q_LORA, H_C, NOPE_C + ROPE_C)
            wq_rope = jnp.concatenate([wq[:, :, NOPE_C:], jnp.zeros((Q_LORA, H_C, LANES - ROPE_C), wq.dtype)], axis=2)
            wq2 = jnp.concatenate([wq[:, :, :NOPE_C].reshape(Q_LORA, H_C * NOPE_C),
                                   wq_rope.reshape(Q_LORA, H_C * LANES)], axis=1).astype(BF16)
            q_all, kv_tok, kr_tok, ckv_n, kr_f = _odd_proj(
                x, norm_mix[layer], mods, w_in, od_q_norm[j], od_kv_norm[j], wq2, od_w_ukv[j].astype(BF16),
                (cos_t, sin_t), rows, mla_scale, H_C * NOPE_C)
            ckv_new.append(ckv_n[:n_p].reshape(bp, lp, KV_LORA))
            kr_new.append(kr_f[:n_p, :ROPE_C].reshape(bp, lp, ROPE_C))
            rows_k = _Rows(bs * past, 0, past, _tile(past, 256))
            kv_c = _norm_matmul(cache_mla_ckv[:, j].reshape(bs * past, KV_LORA), od_w_ukv[j].astype(BF16), rows_k,
                                tn=512, out_dtype=BF16)
            kr_c = jnp.concatenate([cache_mla_krope[:, j].reshape(bs * past, ROPE_C),
                                    jnp.zeros((bs * past, LANES - ROPE_C), F32)], axis=1).astype(BF16)
            qp = [(q_all, lambda h: h), (q_all, lambda h: H_C + h)]
            kp = [(kv_tok, lambda h: 2 * h), (kr_tok, lambda h: 0)]
            vp = (kv_tok, lambda h: 2 * h + 1)
            o = _flash(qp, kp, vp, None, None, n_seq=bp, n_heads=H_C, lq=lp, lk=lp, q_row0=0, k_row0=0,
                       tq=_tile(lp, 256), tk=_tile(lp, 512), diff=False, lam_init=0.0, out_cols=H_C * V_C,
                       heads_per_step=H_C)
            cache = ([(kv_c, lambda h: 2 * h), (kr_c, lambda h: 0)], (kv_c, lambda h: 2 * h + 1), past)
            o = _flash(qp, kp, vp, cache, None, n_seq=bs, n_heads=H_C, lq=ls, lk=ls, q_row0=n_p, k_row0=n_p,
                       tq=_tile(ls, 1024), tk=_tile(ls, 2048), diff=False, lam_init=0.0, out_cols=H_C * V_C, into=o,
                       heads_per_step=FLASH_HEADS_PER_STEP)
            x = _matmul_residual(x, o, od_w_out[j].astype(BF16), mods, rows, 2)
            x = _moe(x, norm_ffn[layer], mods, moe_router[j], moe_w_gu[j].astype(BF16), moe_w_down[j].astype(BF16),
                     rows_m, 3, final_w=final_norm if layer == depth - 1 else None)

    if depth % 2 == 0:
        y_p, y_s = x
    else:
        y = _final_norm(x, final_norm, rows.tm)
        y_p, y_s = y[:n_p], y[n_p:]
    return (y_p.reshape(bp, lp, d), y_s.reshape(bs, ls, d), jnp.stack(sd_new, axis=1),
            jnp.stack(dk_new, axis=1), jnp.stack(dv_new, axis=1), jnp.stack(ckv_new, axis=1),
            jnp.stack(kr_new, axis=1))
```

```python
import functools
import math

import jax
import jax.numpy as jnp
from jax import lax
from jax.experimental import pallas as pl
from jax.experimental.pallas import tpu as pltpu

F32 = jnp.float32
BF16 = jnp.bfloat16

EPS = 1e-6
LOG2E = 1.4426950408889634
GRID_W = 64
ROPE_BASE = 10000.0
H_A, DK_A, DV_A = 4, 128, 128
CONV_K = 3
CHUNK = 64
H_B, DQK_B, DV_B = 4, 64, 128
H_C, NOPE_C, ROPE_C, V_C = 8, 128, 64, 128
Q_LORA, KV_LORA = 384, 256
LANES = 128
SUBLANES = 8
VMEM_LIMIT_BYTES = 56 * 1024 * 1024


def _cparams(*sem):
    return pltpu.CompilerParams(dimension_semantics=sem, vmem_limit_bytes=VMEM_LIMIT_BYTES)


def _dot(a, b):
    return jnp.dot(a, b, preferred_element_type=F32)


def _dot_nt(a, b):
    return lax.dot_general(a, b, (((1,), (1,)), ((), ())), preferred_element_type=F32)


def _dot_tn(a, b):
    return lax.dot_general(a, b, (((0,), (0,)), ((), ())), preferred_element_type=F32)


def _split3(x):
    hi = x.astype(BF16)
    r = x - hi.astype(F32)
    mid = r.astype(BF16)
    lo = (r - mid.astype(F32)).astype(BF16)
    return hi, mid, lo


def _dot_f32(a, b):
    ah, am, al = _split3(a)
    bh, bm, bl = _split3(b)
    return (_dot(ah, bh) + (_dot(ah, bm) + _dot(am, bh))
            + (_dot(am, bm) + _dot(ah, bl) + _dot(al, bh)))


def _silu(x):
    return x * jax.nn.sigmoid(x)


def _tile(n, pref, mult=SUBLANES):
    t = min(n, pref)
    while t > mult and (n % t or t % mult):
        t -= mult
    assert n % t == 0, (n, pref)
    return t


class _Rows:
    def __init__(self, n_ctx, n_lat, lat_len, tm):
        assert n_ctx % tm == 0 and lat_len % tm == 0
        self.tm = tm
        self.n = n_ctx + n_lat
        self.ctx_tiles = n_ctx // tm
        self.seq_tiles = lat_len // tm

    def group(self, i):
        return jnp.where(i < self.ctx_tiles, 0, 1 + jnp.maximum(i - self.ctx_tiles, 0) // self.seq_tiles)

    def pos_block(self, i):
        return jnp.where(i < self.ctx_tiles, 0, 1 + jnp.maximum(i - self.ctx_tiles, 0) % self.seq_tiles)


def _mod_kernel(c_ref, w_ref, b_ref, o_ref):
    o_ref[...] = _dot_f32(_silu(c_ref[...]), w_ref[...]) + b_ref[...]


def _mods(cc, mod_w, mod_b):
    depth, d, n6 = mod_w.shape
    tn = _tile(n6, 512, LANES)
    return pl.pallas_call(
        _mod_kernel,
        grid=(depth, n6 // tn),
        in_specs=[pl.BlockSpec(cc.shape, lambda l, j: (0, 0)),
                  pl.BlockSpec((None, d, tn), lambda l, j: (l, 0, j)),
                  pl.BlockSpec((None, 1, tn), lambda l, j: (l, 0, j))],
        out_specs=pl.BlockSpec((None, cc.shape[0], tn), lambda l, j: (l, 0, j)),
        out_shape=jax.ShapeDtypeStruct((depth, cc.shape[0], n6), F32),
        compiler_params=_cparams("parallel", "parallel"),
        name="mods",
    )(cc, mod_w, mod_b.reshape(depth, 1, n6))


def _mod_chunk(mods_ref, g, idx, d):
    return mods_ref[pl.ds(g, 1), idx * d:(idx + 1) * d]


def _rmsnorm_rows(x, w):
    ms = jnp.mean(x * x, axis=-1, keepdims=True)
    return x * lax.rsqrt(ms + EPS) * w


def _rope(x, cos, sin_signed):
    lane = lax.broadcasted_iota(jnp.int32, (1, LANES), 1)
    low = (lane % 32) < 16
    outs = []
    for c in range(x.shape[1] // LANES):
        xc = x[:, c * LANES:(c + 1) * LANES]
        fwd = pltpu.roll(xc, LANES - 16, axis=1)
        bwd = pltpu.roll(xc, 16, axis=1)
        outs.append(xc * cos + jnp.where(low, fwd, bwd) * sin_signed)
    return outs[0] if len(outs) == 1 else jnp.concatenate(outs, axis=1)


def _rope_tables(rows, lat_len):
    pos = jnp.arange(lat_len)
    r = (pos // GRID_W).astype(F32)
    c = (pos % GRID_W).astype(F32)
    inv = ROPE_BASE ** (-jnp.arange(0, 32, 2, dtype=F32) / 32)
    ang_r = r[:, None] * inv[None, :]
    ang_c = c[:, None] * inv[None, :]
    ang = jnp.concatenate([ang_r, ang_r, ang_c, ang_c] * 2, axis=-1)
    sign = jnp.where((jnp.arange(LANES) % 32) < 16, -1.0, 1.0).astype(F32)
    cos = jnp.concatenate([jnp.ones((rows.tm, LANES), F32), jnp.cos(ang)], axis=0)
    sin = jnp.concatenate([jnp.zeros((rows.tm, LANES), F32), jnp.sin(ang) * sign[None, :]], axis=0)
    return cos, sin


def _nm_kernel(*refs, rows, d_mod, has_norm, mod_idx, has_rope, epi, emit_h):
    it = iter(refs)
    x_ref = next(it)
    w_ref = next(it)
    nw_ref = next(it) if has_norm else None
    mods_ref = next(it) if mod_idx is not None else None
    cos_ref, sin_ref = (next(it), next(it)) if has_rope else (None, None)
    o_ref = next(it)
    h_out_ref = next(it) if emit_h else None
    h_scr = next(it)
    i, j = pl.program_id(0), pl.program_id(1)

    @pl.when(j == 0)
    def _():
        x = x_ref[...].astype(F32)
        if has_norm:
            x = _rmsnorm_rows(x, nw_ref[...])
        if mod_idx is not None:
            g = rows.group(i)
            x = x * (1.0 + _mod_chunk(mods_ref, g, mod_idx[1], d_mod)) + _mod_chunk(mods_ref, g, mod_idx[0], d_mod)
        h_scr[...] = x.astype(BF16)
        if emit_h:
            h_out_ref[...] = x

    acc = _dot(h_scr[...], w_ref[...])
    kinds = sorted(set(epi))
    for kind in kinds:
        cond = None
        for jj, e in enumerate(epi):
            if e == kind:
                cond = (j == jj) if cond is None else (cond | (j == jj))

        def _store(kind=kind):
            y = acc
            if kind[0]:
                y = _rope(y, cos_ref[...], sin_ref[...])
            if kind[1] != 1.0:
                y = y * kind[1]
            o_ref[...] = y.astype(o_ref.dtype)

        if len(kinds) == 1:
            _store()
        else:
            pl.when(cond)(_store)


def _norm_matmul(x, w, rows, *, tn, norm_w=None, mods=None, mod_idx=None, rope=None, epi=None,
                 emit_h=False, out_dtype=F32):
    n, k = x.shape
    nout = w.shape[1]
    nj = nout // tn
    tm = rows.tm
    epi = tuple(epi) if epi is not None else ((False, 1.0),) * nj
    assert len(epi) == nj and n % tm == 0
    has_rope = any(e[0] for e in epi)
    args = [x, w]
    in_specs = [pl.BlockSpec((tm, k), lambda i, j: (i, 0)), pl.BlockSpec((k, tn), lambda i, j: (0, j))]
    if norm_w is not None:
        args.append(norm_w.reshape(1, k).astype(F32))
        in_specs.append(pl.BlockSpec((1, k), lambda i, j: (0, 0)))
    if mod_idx is not None:
        args.append(mods)
        in_specs.append(pl.BlockSpec(mods.shape, lambda i, j: (0, 0)))
    if has_rope:
        args += list(rope)
        in_specs += [pl.BlockSpec((tm, LANES), lambda i, j: (rows.pos_block(i), 0))] * 2
    out_shape = [jax.ShapeDtypeStruct((n, nout), out_dtype)]
    out_specs = [pl.BlockSpec((tm, tn), lambda i, j: (i, j))]
    if emit_h:
        out_shape.append(jax.ShapeDtypeStruct((n, k), F32))
        out_specs.append(pl.BlockSpec((tm, k), lambda i, j: (i, 0)))
    kern = functools.partial(_nm_kernel, rows=rows, d_mod=k, has_norm=norm_w is not None, mod_idx=mod_idx,
                             has_rope=has_rope, epi=epi, emit_h=emit_h)
    out = pl.pallas_call(
        kern, grid=(n // tm, nj), in_specs=in_specs, out_specs=out_specs, out_shape=out_shape,
        scratch_shapes=[pltpu.VMEM((tm, k), BF16)],
        compiler_params=_cparams("parallel", "arbitrary"), name="norm_matmul",
    )(*args)
    return out if emit_h else out[0]


PROJ_CHUNK = 512


def _pos_in_seq(r, length):
    return r & (length - 1) if length & (length - 1) == 0 else lax.rem(r, length)


def _even_proj_kernel(x_ref, xp_ref, xn_ref, nw_ref, mods_ref, w_ref, cw_ref, alog_ref, dt_ref, cos_ref, sin_ref,
                      qkv_ref, z_ref, gc_ref, gr_ref, att_ref, kv_ref,
                      *, rows, n_conv, n_z, hb2, q_scale, ctx_len, lat_len, n_ctx_rows):
    i = pl.program_id(0)
    tm, d = x_ref.shape
    g = rows.group(i)

    def modulated(ref):
        v = _rmsnorm_rows(ref[...], nw_ref[...])
        return (v * (1.0 + _mod_chunk(mods_ref, g, 1, d)) + _mod_chunk(mods_ref, g, 0, d)).astype(BF16)

    h = modulated(x_ref)
    halo = xp_ref.shape[0]
    h_ext = jnp.concatenate([h, modulated(xp_ref), modulated(xn_ref)], axis=0)
    p_ext = jnp.concatenate([_dot(h_ext, w_ref[:, c0:min(c0 + PROJ_CHUNK, n_conv)])
                             for c0 in range(0, n_conv, PROJ_CHUNK)], axis=1)
    p = p_ext[:tm]
    p_before = p_ext[tm + halo - 1:tm + halo, :]
    p_after = p_ext[tm + halo:tm + halo + 1, :]
    row = lax.broadcasted_iota(jnp.int32, (tm, 1), 0)
    r_glob = row + i * tm
    in_ctx = i < rows.ctx_tiles
    pos = jnp.where(in_ctx, _pos_in_seq(r_glob, ctx_len), _pos_in_seq(r_glob - n_ctx_rows, lat_len))
    seq_last = jnp.where(in_ctx, ctx_len - 1, lat_len - 1)
    x_prev = jnp.where(pos == 0, 0.0, jnp.where(row == 0, p_before, pltpu.roll(p, 1, axis=0)))
    x_next = jnp.where(pos == seq_last, 0.0, jnp.where(row == tm - 1, p_after, pltpu.roll(p, tm - 1, axis=0)))
    cw = cw_ref[...]
    y = _silu(x_prev * cw[0:1, :] + p * cw[1:2, :] + x_next * cw[2:3, :])
    n_qk = 2 * H_A
    for hh in range(n_conv // LANES):
        yh = y[:, hh * LANES:(hh + 1) * LANES]
        if hh < n_qk:
            yh = yh * lax.rsqrt(jnp.sum(yh * yh, axis=-1, keepdims=True) + EPS)
            if hh < n_qk // 2:
                yh = yh * (DK_A ** -0.5)
        qkv_ref[:, hh * LANES:(hh + 1) * LANES] = yh
    z_ref[...] = _dot(h, w_ref[:, n_conv:n_conv + n_z])
    s = _dot(h, w_ref[:, n_conv + n_z:n_conv + n_z + LANES])
    lane = lax.broadcasted_iota(jnp.int32, (1, LANES), 1)
    zg = s + dt_ref[...]
    softplus = jnp.maximum(zg, 0.0) + jnp.log(1.0 + jnp.exp(-jnp.abs(zg)))
    gates = jnp.where((lane % 4) < 2, -jnp.exp(alog_ref[...]) * softplus, jax.nn.sigmoid(s))
    gc_ref[...] = gates
    gr_ref[...] = jnp.transpose(gates)[:gr_ref.shape[0], :]
    n_main = n_conv + n_z + LANES
    cos, sin = cos_ref[...], sin_ref[...]
    for part in range(3):
        for c0 in range(0, hb2, PROJ_CHUNK):
            c1 = min(c0 + PROJ_CHUNK, hb2)
            y = _dot(h, w_ref[:, n_main + part * hb2 + c0:n_main + part * hb2 + c1])
            if part < 2:
                y = _rope(y, cos, sin)
            if part == 0:
                y = y * q_scale
            else:
                kv_ref[:, (part - 1) * hb2 + c0:(part - 1) * hb2 + c1] = y
            att_ref[:, part * hb2 + c0:part * hb2 + c1] = y.astype(BF16)


def _even_proj(x, norm_w, mods, w, conv_w, a_log, dt_bias, rope, rows, n_conv, n_z, hb2, q_scale, ctx_len,
               lat_len, n_ctx_rows):
    n, d = x.shape
    tm = rows.tm
    assert conv_w.shape[0] == CONV_K
    halo = 2 * SUBLANES
    hb = tm // halo
    alog_row = jnp.zeros((1, LANES), F32).at[0, :4 * H_A].set(
        jnp.stack([a_log[0], a_log[1], a_log[0], a_log[1]], axis=-1).reshape(-1).astype(F32))
    dt_row = jnp.zeros((1, LANES), F32).at[0, :4 * H_A].set(
        jnp.stack([dt_bias[0], dt_bias[1], jnp.zeros_like(dt_bias[0]), jnp.zeros_like(dt_bias[0])],
                  axis=-1).reshape(-1).astype(F32))
    kern = functools.partial(_even_proj_kernel, rows=rows, n_conv=n_conv, n_z=n_z, hb2=hb2, q_scale=q_scale,
                             ctx_len=ctx_len, lat_len=lat_len, n_ctx_rows=n_ctx_rows)
    tab = pl.BlockSpec((tm, LANES), lambda i: (rows.pos_block(i), 0))
    full = lambda a: pl.BlockSpec(a.shape, lambda i: (0, 0))
    nw = norm_w.reshape(1, d).astype(F32)
    cw = conv_w.astype(F32)
    outs = [(n_conv, F32), (n_z, F32), (LANES, F32), None, (3 * hb2, BF16), (2 * hb2, F32)]
    out_specs = [pl.BlockSpec((tm, c[0]), lambda i: (i, 0)) if c else pl.BlockSpec((4 * H_A, tm), lambda i: (0, i))
                 for c in outs]
    out_shape = [jax.ShapeDtypeStruct((n, c[0]), c[1]) if c else jax.ShapeDtypeStruct((4 * H_A, n), F32)
                 for c in outs]
    return pl.pallas_call(
        kern, grid=(n // tm,),
        in_specs=[pl.BlockSpec((tm, d), lambda i: (i, 0)),
                  pl.BlockSpec((halo, d), lambda i: (jnp.maximum(i * hb - 1, 0), 0)),
                  pl.BlockSpec((halo, d), lambda i: (jnp.minimum((i + 1) * hb, n // halo - 1), 0)),
                  full(nw), full(mods), full(w), full(cw), full(alog_row), full(dt_row), tab, tab],
        out_specs=out_specs, out_shape=out_shape,
        compiler_params=_cparams("parallel"), name="even_proj",
    )(x, x, x, nw, mods, w, cw, alog_row, dt_row, *rope)


def _odd_proj_kernel(x_ref, nw_ref, mods_ref, w_ref, qn_ref, kvn_ref, wq_ref, wkv_ref, cos_ref, sin_ref,
                     q_ref, kv_ref, kr_ref, ckv_ref, krf_ref, *, rows, scale, n_nope):
    i = pl.program_id(0)
    d = x_ref.shape[1]
    h = _modulated(x_ref, nw_ref, mods_ref, rows.group(i), 0, d).astype(BF16)
    p = _dot(h, w_ref[...])
    cos, sin = cos_ref[...], sin_ref[...]
    cq = _rmsnorm_rows(p[:, :Q_LORA], qn_ref[...]).astype(BF16)
    for c0 in range(0, wq_ref.shape[1], PROJ_CHUNK):
        y = _dot(cq, wq_ref[:, c0:c0 + PROJ_CHUNK])
        if c0 >= n_nope:
            y = _rope(y, cos, sin)
        q_ref[:, c0:c0 + PROJ_CHUNK] = (y * scale).astype(BF16)
    ckv = _rmsnorm_rows(p[:, Q_LORA:Q_LORA + KV_LORA], kvn_ref[...])
    ckv_ref[...] = ckv
    ckv16 = ckv.astype(BF16)
    for c0 in range(0, wkv_ref.shape[1], PROJ_CHUNK):
        kv_ref[:, c0:c0 + PROJ_CHUNK] = _dot(ckv16, wkv_ref[:, c0:c0 + PROJ_CHUNK]).astype(BF16)
    kr = _rope(p[:, Q_LORA + KV_LORA:], cos, sin)
    krf_ref[...] = kr
    kr_ref[...] = kr.astype(BF16)


def _odd_proj(x, norm_w, mods, w_in, q_norm, kv_norm, wq, wkv, rope, rows, scale, n_nope):
    n, d = x.shape
    tm = rows.tm
    tab = pl.BlockSpec((tm, LANES), lambda i: (rows.pos_block(i), 0))
    full = lambda a: pl.BlockSpec(a.shape, lambda i: (0, 0))
    qn = q_norm.reshape(1, -1).astype(F32)
    kvn = kv_norm.reshape(1, -1).astype(F32)
    nw = norm_w.reshape(1, d).astype(F32)
    outs = [(wq.shape[1], BF16), (wkv.shape[1], BF16), (LANES, BF16), (KV_LORA, F32), (LANES, F32)]
    return pl.pallas_call(
        functools.partial(_odd_proj_kernel, rows=rows, scale=scale, n_nope=n_nope), grid=(n // tm,),
        in_specs=[pl.BlockSpec((tm, d), lambda i: (i, 0)), full(nw), full(mods), full(w_in), full(qn), full(kvn),
                  full(wq), full(wkv), tab, tab],
        out_specs=[pl.BlockSpec((tm, c), lambda i: (i, 0)) for c, _ in outs],
        out_shape=[jax.ShapeDtypeStruct((n, c), dt) for c, dt in outs],
        compiler_params=_cparams("parallel"), name="odd_proj",
    )(x, nw, mods, w_in, qn, kvn, wq, wkv, *rope)


def _split2(x):
    hi = x.astype(BF16)
    return hi, (x - hi.astype(F32)).astype(BF16)


def _mm(a, b, precise, dot=_dot):
    if not precise:
        return dot(a.astype(BF16), b.astype(BF16))
    ah, al = _split2(a)
    bh, bl = _split2(b)
    return dot(ah, bh) + (dot(ah, bl) + dot(al, bh))


SOLVE_BLOCK = SUBLANES
MOE_ROW_BLOCK = 256
FLASH_HEADS_PER_STEP = 2
DELTA_HEADS_PER_STEP = 4


def _tri_solve_many(ms, rs, revs):
    c, width = rs[0].shape
    blk = SOLVE_BLOCK
    nb, gpb, ng = c // blk, blk // SUBLANES, c // SUBLANES
    col = lax.broadcasted_iota(jnp.int32, (blk, c), 1)
    mgs = [[m[SUBLANES * g:SUBLANES * (g + 1), :] for g in range(ng)] for m in ms]
    xgs = [[r[SUBLANES * g:SUBLANES * (g + 1), :] for g in range(ng)] for r in rs]
    zero_blk = jnp.zeros((blk, width), BF16)
    fin_h = [[zero_blk] * nb for _ in ms]
    fin_l = [[zero_blk] * nb for _ in ms]
    for bi in range(nb):
        blocks = [nb - 1 - bi if rev else bi for rev in revs]
        if bi > 0:
            for s, (rev, b) in enumerate(zip(revs, blocks)):
                done = (col >= (b + 1) * blk) if rev else (col < b * blk)
                lh, ll = _split2(jnp.where(done, ms[s][b * blk:(b + 1) * blk, :], 0.0))
                xh = jnp.concatenate(fin_h[s], axis=0)
                xl = jnp.concatenate(fin_l[s], axis=0)
                upd = _dot(lh, xh) + (_dot(lh, xl) + _dot(ll, xh))
                for gg in range(gpb):
                    g = b * gpb + gg
                    xgs[s][g] = xgs[s][g] - upd[SUBLANES * gg:SUBLANES * (gg + 1), :]
        for t in range(blk - 1):
            for s, (rev, b) in enumerate(zip(revs, blocks)):
                j = b * blk + (blk - 1 - t if rev else t)
                xj = xgs[s][j // SUBLANES][j % SUBLANES:j % SUBLANES + 1, :]
                groups = range(b * gpb, (j - 1) // SUBLANES + 1) if rev else range((j + 1) // SUBLANES, (b + 1) * gpb)
                for g in groups:
                    xgs[s][g] = xgs[s][g] - mgs[s][g][:, j:j + 1] * xj
        if bi < nb - 1:
            for s, b in enumerate(blocks):
                fin_h[s][b], fin_l[s][b] = _split2(jnp.concatenate(xgs[s][b * gpb:(b + 1) * gpb], axis=0))
    return [jnp.concatenate(xg, axis=0) for xg in xgs]


def _delta_kernel(qf, kf, vf, gcf, grf, qb, kb, vb, gcb, grb, s0_ref, *rest, chunk, hpb, n_alias, precise):
    of_ref, ob_ref, so_ref, s_scr = rest[n_alias:]
    h0 = pl.program_id(1) * hpb
    i = pl.program_id(2)
    n_i = pl.num_programs(2)

    @pl.when(i == 0)
    def _():
        s_scr[...] = s0_ref[...]

    rg = qf.shape[0]
    nc = rg // chunk
    rowi = lax.broadcasted_iota(jnp.int32, (rg, rg), 0)
    coli = lax.broadcasted_iota(jnp.int32, (rg, rg), 1)
    same = (rowi // chunk) == (coli // chunk)
    lane = lax.broadcasted_iota(jnp.int32, (1, LANES), 1)
    r_c = lax.broadcasted_iota(jnp.int32, (chunk, chunk), 0)
    c_c = lax.broadcasted_iota(jnp.int32, (chunk, chunk), 1)

    prep = []
    for d, (q_ref, k_ref, v_ref, gc_ref, gr_ref) in enumerate(((qf, kf, vf, gcf, grf), (qb, kb, vb, gcb, grb))):
        rev = d == 1
        incl_big = same & ((rowi <= coli) if rev else (rowi >= coli))
        lm = jnp.where(incl_big, 1.0, 0.0).astype(BF16)
        gcols = gc_ref[...]
        grows = gr_ref[...]
        gh, gm, gl_ = _split3(gcols)
        cum_c = _dot(lm, gh) + _dot(lm, gm) + _dot(lm, gl_)
        th, tm_, tl = _split3(grows)
        cum_r = _dot_nt(th, lm) + _dot_nt(tm_, lm) + _dot_nt(tl, lm)

        def col(a, ln):
            return jnp.sum(jnp.where(lane == ln, a, 0.0), axis=1, keepdims=True)

        sub = lax.broadcasted_iota(jnp.int32, (cum_r.shape[0], 1), 0)
        incl = (r_c <= c_c) if rev else (r_c >= c_c)
        strict = (r_c < c_c) if rev else (r_c > c_c)
        for hh in range(hpb):
            h = h0 + hh
            hs = slice(hh * LANES, (hh + 1) * LANES)
            gcum = col(cum_c, 4 * h + d)
            beta = col(gcols, 4 * h + 2 + d)
            grow = jnp.sum(jnp.where(sub == 4 * h + d, cum_r, 0.0), axis=0, keepdims=True)
            q = q_ref[:, hs]
            k = k_ref[:, hs]
            v = v_ref[:, hs]
            kbeta = k * beta
            eg = jnp.exp(gcum)
            rhs = jnp.concatenate([v * beta, kbeta * eg], axis=1)
            ms, rs, a_in = [], [], []
            for c in range(nc):
                sl = slice(c * chunk, (c + 1) * chunk)
                e = jnp.exp(jnp.where(incl, gcum[sl] - grow[:, sl], 0.0))
                kk = _mm(kbeta[sl], k[sl], precise, _dot_nt)
                qk = _mm(q[sl], k[sl], precise, _dot_nt)
                a_in.append(jnp.where(incl, qk * e, 0.0))
                ms.append(jnp.where(strict, kk * e, 0.0))
                rs.append(rhs[sl])
            prep.append((rev, hh, ms, rs, a_in, q * eg, k, gcum))

    all_terms = []
    half = len(prep) // 2
    for wave in (prep[:half], prep[half:]) if half else (prep,):
        sols = _tri_solve_many([m for p in wave for m in p[2]], [r for p in wave for r in p[3]],
                               [p[0] for p in wave for _ in range(nc)])
        for n_u, (rev, hh, _, _, a_in, qg, k, gcum) in enumerate(wave):
            xs = sols[n_u * nc:(n_u + 1) * nc]
            terms = []
            for c in range(nc):
                sl = slice(c * chunk, (c + 1) * chunk)
                r_last = c * chunk if rev else (c + 1) * chunk - 1
                g_last = gcum[r_last:r_last + 1, :]
                k_dec = k[sl] * jnp.exp(g_last - gcum[sl])
                a_uw = _mm(a_in[c], xs[c], precise)
                kd_uw = _mm(k_dec, xs[c], precise, _dot_tn)
                lhs = jnp.concatenate([qg[sl] - a_uw[:, DV_A:], kd_uw[:, DV_A:]], axis=0)
                terms.append((lhs, a_uw[:, :DV_A], kd_uw[:, :DV_A], jnp.exp(g_last)))
            all_terms.append(terms)

    states = [s_scr[1 if p[0] else 0, p[1]] for p in prep]
    outs = [[None] * nc for _ in prep]
    for t in range(nc):
        for n_u, p in enumerate(prep):
            c = nc - 1 - t if p[0] else t
            lhs, o_0, q_0, decay = all_terms[n_u][c]
            prod = _mm(lhs, states[n_u], precise)
            outs[n_u][c] = o_0 + prod[:chunk]
            states[n_u] = states[n_u] * decay - prod[chunk:] + q_0
    for n_u, (rev, hh) in enumerate((p[0], p[1]) for p in prep):
        s_scr[1 if rev else 0, hh] = states[n_u]
        (ob_ref if rev else of_ref)[:, hh * LANES:(hh + 1) * LANES] = jnp.concatenate(outs[n_u], axis=0)

    @pl.when(i == n_i - 1)
    def _():
        so_ref[...] = s_scr[...]


def _delta_stage(qkvn, gcol, grow, s0, n_seq, seq_len, row0, rg, into=None, precise=False):
    ng = seq_len // rg
    b0 = row0 // rg

    def fwd_blk(b, i):
        return b0 + b * ng + i

    def bwd_blk(b, i):
        return b0 + b * ng + (ng - 1 - i)

    hpb = DELTA_HEADS_PER_STEP
    nhp = H_A // hpb
    hw = hpb * LANES

    def specs(blk):
        return [pl.BlockSpec((rg, hw), lambda b, h, i: (blk(b, i), h)),
                pl.BlockSpec((rg, hw), lambda b, h, i: (blk(b, i), nhp + h)),
                pl.BlockSpec((rg, hw), lambda b, h, i: (blk(b, i), 2 * nhp + h)),
                pl.BlockSpec((rg, LANES), lambda b, h, i: (blk(b, i), 0)),
                pl.BlockSpec((4 * H_A, rg), lambda b, h, i: (0, blk(b, i)))]

    st_spec = pl.BlockSpec((None, 2, hpb, DK_A, DV_A), lambda b, h, i: (b, 0, h, 0, 0))
    args = [qkvn, qkvn, qkvn, gcol, grow, qkvn, qkvn, qkvn, gcol, grow, s0]
    in_specs = specs(fwd_blk) + specs(bwd_blk) + [st_spec]
    aliases = {}
    if into is not None:
        for k_out, arr in enumerate(into):
            aliases[len(args)] = k_out
            args.append(arr)
            in_specs.append(pl.BlockSpec(memory_space=pl.ANY))
    n = qkvn.shape[0]
    return pl.pallas_call(
        functools.partial(_delta_kernel, chunk=CHUNK, hpb=hpb, n_alias=len(aliases), precise=precise),
        grid=(n_seq, nhp, ng),
        in_specs=in_specs,
        out_specs=[pl.BlockSpec((rg, hw), lambda b, h, i: (fwd_blk(b, i), h)),
                   pl.BlockSpec((rg, hw), lambda b, h, i: (bwd_blk(b, i), h)),
                   st_spec],
        out_shape=[jax.ShapeDtypeStruct((n, H_A * DV_A), F32),
                   jax.ShapeDtypeStruct((n, H_A * DV_A), F32),
                   jax.ShapeDtypeStruct(s0.shape, F32)],
        input_output_aliases=aliases,
        scratch_shapes=[pltpu.VMEM((2, hpb, DK_A, DV_A), F32)],
        compiler_params=_cparams("parallel", "parallel", "arbitrary"), name="gated_delta",
    )(*args)


def _flash_kernel(*refs, n_parts, has_cache, diff, tk, lam_init, aliased, head_lanes):
    it = iter(refs)
    q_refs = [next(it) for _ in range(n_parts)]
    k_refs = [next(it) for _ in range(n_parts)]
    v_ref = next(it)
    kc_refs = [next(it) for _ in range(n_parts)] if has_cache else []
    vc_ref = next(it) if has_cache else None
    lam_ref = next(it) if diff else None
    if aliased:
        next(it)
    o_ref = next(it)
    tq = q_refs[0].shape[0]
    lk = k_refs[0].shape[0]
    if diff:
        lp = lam_ref[...]
        lam = (jnp.exp(jnp.sum(lp[0:1] * lp[1:2], axis=-1, keepdims=True))
               - jnp.exp(jnp.sum(lp[2:3] * lp[3:4], axis=-1, keepdims=True)) + lam_init)
        lane = lax.broadcasted_iota(jnp.int32, (1, LANES), 1)

    def cols(ref, off, rows=None):
        return ref[:, off:off + LANES] if rows is None else ref[rows, off:off + LANES]

    def block(carry, q_cat, ks, vv):
        m, l, acc = carry
        s = _dot_nt(q_cat, ks[0] if n_parts == 1 else jnp.concatenate(ks, axis=1))
        m_new = jnp.maximum(m, jnp.max(s, axis=-1, keepdims=True))
        alpha = jnp.exp2(m - m_new)
        p_ = jnp.exp2(s - m_new)
        l = alpha * l + jnp.sum(p_, axis=-1, keepdims=True)
        acc = alpha * acc + _dot(p_.astype(BF16), vv)
        return m_new, l, acc

    q_cats, carries = [], []
    for q_offs, *_ in head_lanes:
        qs = [cols(r, off) for r, off in zip(q_refs, q_offs)]
        if diff:
            q = qs[0]
            qs = [jnp.concatenate([jnp.where(lane < DQK_B, q, 0), jnp.where(lane >= DQK_B, q, 0)], axis=0)]
        nrow = qs[0].shape[0]
        q_cats.append(qs[0] if n_parts == 1 else jnp.concatenate(qs, axis=1))
        carries.append((jnp.full((nrow, 1), -1e30, F32), jnp.zeros((nrow, 1), F32), jnp.zeros((nrow, LANES), F32)))

    for t in range(lk // tk):
        rows = slice(t * tk, (t + 1) * tk)
        for n_h, (_, k_offs, v_off, _, _, _) in enumerate(head_lanes):
            carries[n_h] = block(carries[n_h], q_cats[n_h], [cols(r, off, rows) for r, off in zip(k_refs, k_offs)],
                                 cols(v_ref, v_off, rows))
    for n_h, (_, _, _, o_off, kc_offs, vc_off) in enumerate(head_lanes):
        carry = carries[n_h]
        if has_cache:
            carry = block(carry, q_cats[n_h], [cols(r, off) for r, off in zip(kc_refs, kc_offs)], cols(vc_ref, vc_off))
        _, l, acc = carry
        o = acc / l
        if diff:
            o = o[:tq] - lam * o[tq:]
        o_ref[:, o_off:o_off + LANES] = o.astype(o_ref.dtype)


def _flash(q_parts, k_parts, v_part, cache, lam, *, n_seq, n_heads, lq, lk, q_row0, k_row0, tq, tk, diff,
           lam_init, out_cols, into=None, heads_per_step=1):
    nqt = lq // tq
    qb0, kb0 = q_row0 // tq, k_row0 // lk
    hps = heads_per_step
    assert n_heads % hps == 0
    args, in_specs = [], []

    def add(arr, cf, n_rows, row_fn):
        span = 1
        while any(cf(g * hps) // span != cf(g * hps + hps - 1) // span for g in range(n_heads // hps)):
            span *= 2
        offs = tuple((cf(j) - cf(0) // span * span) * LANES for j in range(hps))
        for g in range(n_heads // hps):
            assert offs == tuple((cf(g * hps + j) - cf(g * hps) // span * span) * LANES for j in range(hps))
        args.append(arr)
        in_specs.append(pl.BlockSpec((n_rows, span * LANES),
                                     lambda b, hg, t: (row_fn(b, t), cf(hg * hps) // span)))
        return offs

    q_offs = [add(arr, cf, tq, lambda b, t: qb0 + b * nqt + t) for arr, cf in q_parts]
    k_offs = [add(arr, cf, lk, lambda b, t: kb0 + b) for arr, cf in k_parts]
    v_offs = add(v_part[0], v_part[1], lk, lambda b, t: kb0 + b)
    kc_offs, vc_offs = [(0,) * hps] * len(q_parts), (0,) * hps
    if cache is not None:
        c_k, c_v, c_rows = cache
        kc_offs = [add(arr, cf, c_rows, lambda b, t: b) for arr, cf in c_k]
        vc_offs = add(c_v[0], c_v[1], c_rows, lambda b, t: b)
    head_lanes = tuple((tuple(o[j] for o in q_offs), tuple(o[j] for o in k_offs), v_offs[j], j * LANES,
                        tuple(o[j] for o in kc_offs), vc_offs[j]) for j in range(hps))
    grid = (n_seq, n_heads // hps, nqt)
    out_spec = pl.BlockSpec((tq, hps * LANES), lambda b, hg, t: (qb0 + b * nqt + t, hg))
    if diff:
        args.append(lam)
        in_specs.append(pl.BlockSpec(lam.shape, lambda b, h, t: (0, 0)))
    aliases = {}
    if into is not None:
        aliases[len(args)] = 0
        args.append(into)
        in_specs.append(pl.BlockSpec(memory_space=pl.ANY))
    kern = functools.partial(_flash_kernel, n_parts=len(q_parts), has_cache=cache is not None, diff=diff, tk=tk,
                             lam_init=lam_init, aliased=into is not None, head_lanes=head_lanes)
    return pl.pallas_call(
        kern, grid=grid, in_specs=in_specs, out_specs=out_spec,
        out_shape=jax.ShapeDtypeStruct((q_parts[0][0].shape[0], out_cols), F32 if diff else BF16),
        input_output_aliases=aliases,
        compiler_params=_cparams("parallel", "parallel", "arbitrary"), name="flash_attention",
    )(*args)


def _mr_kernel(x_ref, a_ref, w_ref, mods_ref, o_ref, *, rows, gate_idx):
    i = pl.program_id(0)
    d = x_ref.shape[1]
    gate = _mod_chunk(mods_ref, rows.group(i), gate_idx, d)
    o_ref[...] = x_ref[...] + gate * _dot(a_ref[...].astype(BF16), w_ref[...])


def _matmul_residual(x, a, w, mods, rows, gate_idx):
    n, d = x.shape
    tm = rows.tm
    return pl.pallas_call(
        functools.partial(_mr_kernel, rows=rows, gate_idx=gate_idx), grid=(n // tm,),
        in_specs=[pl.BlockSpec((tm, d), lambda i: (i, 0)),
                  pl.BlockSpec((tm, a.shape[1]), lambda i: (i, 0)),
                  pl.BlockSpec(w.shape, lambda i: (0, 0)),
                  pl.BlockSpec(mods.shape, lambda i: (0, 0))],
        out_specs=pl.BlockSpec((tm, d), lambda i: (i, 0)),
        out_shape=jax.ShapeDtypeStruct((n, d), F32),
        compiler_params=_cparams("parallel"), name="matmul_residual",
    )(x, a, w, mods)


def _modulated(x_ref, nw_ref, mods_ref, g, idx, d):
    x = _rmsnorm_rows(x_ref[...], nw_ref[...])
    return x * (1.0 + _mod_chunk(mods_ref, g, idx + 1, d)) + _mod_chunk(mods_ref, g, idx, d)


def _ffn_kernel(x_ref, nw_ref, mods_ref, wg_ref, wu_ref, wd_ref, of_ref, ob_ref, z_ref, od_ref, on_ref, sn_ref,
                wo_ref, o_ref, h_scr, acc_scr, xn_scr, *, rows, mod0, mix_gate, lam_init):
    i, f = pl.program_id(0), pl.program_id(1)
    d = x_ref.shape[1]
    g = rows.group(i)

    @pl.when(f == 0)
    def _():
        oa = of_ref[...] + ob_ref[...]
        z = z_ref[...]
        od = od_ref[...]
        ya, yb = [], []
        for hh in range(H_A):
            sl = slice(hh * LANES, (hh + 1) * LANES)
            ya.append(_rmsnorm_rows(oa[:, sl], on_ref[...]) * _silu(z[:, sl]))
        for hh in range(H_B):
            sl = slice(hh * LANES, (hh + 1) * LANES)
            yb.append(_rmsnorm_rows(od[:, sl], sn_ref[...]) * (1.0 - lam_init))
        y = jnp.concatenate(ya + yb, axis=1).astype(BF16)
        x1 = x_ref[...] + _mod_chunk(mods_ref, g, mix_gate, d) * _dot(y, wo_ref[...])
        xn_scr[...] = x1
        v = _rmsnorm_rows(x1, nw_ref[...])
        h_scr[...] = (v * (1.0 + _mod_chunk(mods_ref, g, mod0 + 1, d)) + _mod_chunk(mods_ref, g, mod0, d)).astype(BF16)
        acc_scr[...] = jnp.zeros_like(acc_scr)

    h = h_scr[...]
    act = _silu(_dot(h, wg_ref[...])) * _dot(h, wu_ref[...])
    acc_scr[...] += _dot(act.astype(BF16), wd_ref[...])

    @pl.when(f == pl.num_programs(1) - 1)
    def _():
        o_ref[...] = xn_scr[...] + _mod_chunk(mods_ref, g, mod0 + 2, d) * acc_scr[...]


def _even_tail(x, o_f, o_b, z, o_d, onorm, subln, w_out, norm_w, mods, w_gu, w_down, rows, mix_gate, mod0, tf,
               lam_init):
    n, d = x.shape
    tm = rows.tm
    ff = w_down.shape[0]
    nf = ff // tf
    hw = H_A * DV_A
    row = lambda c: pl.BlockSpec((tm, c), lambda i, f: (i, 0))
    one = lambda c: pl.BlockSpec((1, c), lambda i, f: (0, 0))
    wbuf = dict(pipeline_mode=pl.Buffered(1)) if nf == 1 else {}
    return pl.pallas_call(
        functools.partial(_ffn_kernel, rows=rows, mod0=mod0, mix_gate=mix_gate, lam_init=lam_init),
        grid=(n // tm, nf),
        in_specs=[row(d), one(d), pl.BlockSpec(mods.shape, lambda i, f: (0, 0)),
                  pl.BlockSpec((d, tf), lambda i, f: (0, f), **wbuf),
                  pl.BlockSpec((d, tf), lambda i, f: (0, nf + f), **wbuf),
                  pl.BlockSpec((tf, d), lambda i, f: (f, 0), **wbuf),
                  row(hw), row(hw), row(hw), row(hw), one(LANES), one(LANES),
                  pl.BlockSpec(w_out.shape, lambda i, f: (0, 0))],
        out_specs=row(d),
        out_shape=jax.ShapeDtypeStruct((n, d), F32),
        scratch_shapes=[pltpu.VMEM((tm, d), BF16), pltpu.VMEM((tm, d), F32), pltpu.VMEM((tm, d), F32)],
        compiler_params=_cparams("parallel", "arbitrary"), name="even_tail",
    )(x, norm_w.reshape(1, d).astype(F32), mods, w_gu, w_gu, w_down, o_f, o_b, z, o_d,
      onorm.reshape(1, LANES).astype(F32), subln.reshape(1, LANES).astype(F32), w_out)


def _moe_kernel(x_ref, nw_ref, mods_ref, rw_ref, wgu_ref, wd_ref, *rest, rows, mod0, n_exp, sb, final):
    if final:
        fw_ref, o_ctx_ref, o_lat_ref, h_scr, g_scr, rk_scr, rkt_scr, acc_scr = rest
    else:
        o_ref, h_scr, g_scr, rk_scr, rkt_scr, acc_scr = rest
    i, e = pl.program_id(0), pl.program_id(1)
    tm, d = x_ref.shape
    g = rows.group(i)
    lane = lax.broadcasted_iota(jnp.int32, (1, LANES), 1)

    @pl.when(e == 0)
    def _():
        h = _modulated(x_ref, nw_ref, mods_ref, g, mod0, d)
        h_scr[...] = h.astype(BF16)
        acc_scr[...] = jnp.zeros_like(acc_scr)
        logits = jnp.where(lane < n_exp, _mm(h, rw_ref[...], True), -1e30)
        pe = jnp.exp(logits - jnp.max(logits, axis=-1, keepdims=True))
        probs = pe / jnp.sum(pe, axis=-1, keepdims=True)
        p1 = jnp.max(probs, axis=-1, keepdims=True)
        i1 = jnp.min(jnp.where(probs == p1, lane, LANES), axis=-1, keepdims=True)
        rest = jnp.where(lane == i1, -1.0, probs)
        p2 = jnp.max(rest, axis=-1, keepdims=True)
        i2 = jnp.min(jnp.where(rest == p2, lane, LANES), axis=-1, keepdims=True)
        den = p1 + p2
        gates = jnp.where(lane == i1, p1 / den, 0.0) + jnp.where(lane == i2, p2 / den, 0.0)
        g_scr[...] = gates
        t_r = lax.broadcasted_iota(jnp.int32, (tm, tm), 0)
        t_c = lax.broadcasted_iota(jnp.int32, (tm, tm), 1)
        earlier = jnp.where(t_r > t_c, 1.0, 0.0).astype(BF16)
        sel = gates > 0.0
        rank = jnp.where(sel, _dot(earlier, jnp.where(sel, 1.0, 0.0).astype(BF16)), -1.0)
        rk_scr[...] = rank
        rkt_scr[...] = jnp.transpose(rank)[:rkt_scr.shape[0], :]

    ff = wd_ref.shape[0]
    g_e = jnp.sum(jnp.where(lane == e, g_scr[...], 0.0), axis=-1, keepdims=True)
    rk_col = jnp.sum(jnp.where(lane == e, rk_scr[...], 0.0), axis=-1, keepdims=True)
    rk_row = rkt_scr[pl.ds(e, 1), :]
    n_rows = (jnp.max(rk_row) + 1.0).astype(jnp.int32)
    g_hi = g_e.astype(BF16).astype(F32)
    g2 = jnp.where(lane == 0, g_hi, jnp.where(lane == 1, g_e - g_hi, 0.0)).astype(BF16)
    def expert_rows(slot0, nr):
        base = slot0.astype(F32)
        r_sub = lax.broadcasted_iota(jnp.int32, (nr, 1), 0).astype(F32)
        r_lane = lax.broadcasted_iota(jnp.int32, (1, nr), 1).astype(F32)
        pick = jnp.where(rk_row == r_sub + base, 1.0, 0.0).astype(BF16)
        xg = _dot(pick, h_scr[...]).astype(BF16)
        gr = _dot(pick, g2)
        gate_r = gr[:, 0:1] + gr[:, 1:2]
        gu = _dot(xg, wgu_ref[...])
        act = _silu(gu[:, :ff]) * gu[:, ff:]
        y = _dot(act.astype(BF16), wd_ref[...])
        put = jnp.where(rk_col == r_lane + base, 1.0, 0.0).astype(BF16)
        acc_scr[...] += _dot(put, (y * gate_r).astype(BF16))

    half, quarter = sb // 2, sb // 4
    rem = n_rows % sb
    n_full = n_rows // sb + jnp.where(rem > half, 1, 0)

    def full_body(jb, carry):
        expert_rows(jb * sb, sb)
        return carry

    lax.fori_loop(0, n_full, full_body, 0)

    @pl.when((rem > quarter) & (rem <= half))
    def _():
        expert_rows(n_full * sb, half)

    @pl.when((rem > 0) & (rem <= quarter))
    def _():
        expert_rows(n_full * sb, quarter)

    def result():
        return x_ref[...] + _mod_chunk(mods_ref, g, mod0 + 2, d) * acc_scr[...]

    last = e == n_exp - 1
    if final:
        @pl.when(last & (i < rows.ctx_tiles))
        def _():
            o_ctx_ref[...] = _rmsnorm_rows(result(), fw_ref[...])

        @pl.when(last & (i >= rows.ctx_tiles))
        def _():
            o_lat_ref[...] = _rmsnorm_rows(result(), fw_ref[...])
    else:
        @pl.when(last)
        def _():
            o_ref[...] = result()


def _moe(x, norm_w, mods, router_w, w_gu, w_down, rows, mod0, final_w=None):
    n, d = x.shape
    tm = rows.tm
    n_exp, _, ff2 = w_gu.shape
    rw = jnp.zeros((d, LANES), F32).at[:, :n_exp].set(router_w.astype(F32))
    final = final_w is not None
    args = [x, norm_w.reshape(1, d).astype(F32), mods, rw, w_gu, w_down]
    in_specs = [pl.BlockSpec((tm, d), lambda i, e: (i, 0)),
                pl.BlockSpec((1, d), lambda i, e: (0, 0)),
                pl.BlockSpec(mods.shape, lambda i, e: (0, 0)),
                pl.BlockSpec((d, LANES), lambda i, e: (0, 0)),
                pl.BlockSpec((None, d, ff2), lambda i, e: (e, 0, 0)),
                pl.BlockSpec((None, ff2 // 2, d), lambda i, e: (e, 0, 0))]
    if final:
        args.append(final_w.reshape(1, d).astype(F32))
        in_specs.append(pl.BlockSpec((1, d), lambda i, e: (0, 0)))
        ct = rows.ctx_tiles
        out_specs = [pl.BlockSpec((tm, d), lambda i, e: (jnp.minimum(i, ct - 1), 0), pipeline_mode=pl.Buffered(1)),
                     pl.BlockSpec((tm, d), lambda i, e: (jnp.maximum(i - ct, 0), 0), pipeline_mode=pl.Buffered(1))]
        out_shape = [jax.ShapeDtypeStruct((ct * tm, d), F32), jax.ShapeDtypeStruct((n - ct * tm, d), F32)]
    else:
        out_specs = pl.BlockSpec((tm, d), lambda i, e: (i, 0))
        out_shape = jax.ShapeDtypeStruct((n, d), F32)
    return pl.pallas_call(
        functools.partial(_moe_kernel, rows=rows, mod0=mod0, n_exp=n_exp, sb=MOE_ROW_BLOCK, final=final),
        grid=(n // tm, n_exp), in_specs=in_specs, out_specs=out_specs, out_shape=out_shape,
        scratch_shapes=[pltpu.VMEM((tm, d), BF16), pltpu.VMEM((tm, LANES), F32), pltpu.VMEM((tm, LANES), F32),
                        pltpu.VMEM((SUBLANES * ((n_exp + SUBLANES - 1) // SUBLANES), tm), F32),
                        pltpu.VMEM((tm, d), F32)],
        compiler_params=_cparams("arbitrary", "arbitrary"), name="moe",
    )(*args)


def _final_kernel(x_ref, w_ref, o_ref):
    o_ref[...] = _rmsnorm_rows(x_ref[...], w_ref[...])


def _final_norm(x, w, tm):
    n, d = x.shape
    return pl.pallas_call(
        _final_kernel, grid=(n // tm,),
        in_specs=[pl.BlockSpec((tm, d), lambda i: (i, 0)), pl.BlockSpec((1, d), lambda i: (0, 0))],
        out_specs=pl.BlockSpec((tm, d), lambda i: (i, 0)),
        out_shape=jax.ShapeDtypeStruct((n, d), F32),
        compiler_params=_cparams("parallel"), name="final_norm",
    )(x, w.reshape(1, d).astype(F32))


def kernel(x_prompt, x_sample, state_delta, cache_diff_k, cache_diff_v, cache_mla_ckv, cache_mla_krope, c, c_ctx,
           mod_w, mod_b, norm_mix, norm_ffn, final_norm, ev_w_in, ev_conv_w, ev_a_log, ev_dt_bias, ev_onorm,
           ev_lambda, ev_subln, ev_w_out, ffn_w_gu, ffn_w_down, od_w_in, od_q_norm, od_kv_norm, od_w_uq, od_w_ukv,
           od_w_out, moe_router, moe_w_gu, moe_w_down):
    bp, lp, d = x_prompt.shape
    bs, ls, _ = x_sample.shape
    past = cache_diff_k.shape[3]
    depth = mod_w.shape[0]
    n_p, n_s = bp * lp, bs * ls
    n = n_p + n_s
    assert n_p % ls == 0 and past == lp

    rows = _Rows(n_p, n_s, ls, _tile(math.gcd(n_p, ls), 512))
    rows_m = _Rows(n_p, n_s, ls, _tile(math.gcd(n_p, ls), 1024))
    rg = _tile(math.gcd(lp, ls), 256)
    cos_t, sin_t = _rope_tables(rows, ls)

    x = jnp.concatenate([x_prompt.reshape(n_p, d), x_sample.reshape(n_s, d)], axis=0)
    n_grp = 1 + bs
    cc = jnp.zeros((2 * SUBLANES * ((n_grp + 15) // 16), d), F32).at[0].set(c_ctx).at[1:n_grp].set(c)
    mods_all = _mods(cc, mod_w, mod_b)

    hq = H_A * DK_A
    hb2 = H_B * 2 * DQK_B
    sm0 = 4 * hq
    qb0 = sm0 + 4 * H_A
    q_scale = (DQK_B ** -0.5) * LOG2E
    mla_scale = ((NOPE_C + ROPE_C) ** -0.5) * LOG2E
    sd_new, dk_new, dv_new, ckv_new, kr_new = [], [], [], [], []

    for layer in range(depth):
        j = layer // 2
        mods = mods_all[layer]
        if layer % 2 == 0:
            lam_init = 0.8 - 0.6 * math.exp(-0.3 * layer)
            w = ev_w_in[j]
            sm = w[:, sm0:qb0].reshape(d, 4, H_A).transpose(0, 2, 1).reshape(d, 4 * H_A)
            w_in = jnp.concatenate([w[:, :sm0], sm, jnp.zeros((d, LANES - 4 * H_A), w.dtype), w[:, qb0:]],
                                   axis=1).astype(BF16)
            qkvn, z_gate, gcol, grow, att, kv_f = _even_proj(
                x, norm_mix[layer], mods, w_in, ev_conv_w[j], ev_a_log[j], ev_dt_bias[j], (cos_t, sin_t), rows,
                3 * hq, hq, hb2, q_scale, lp, ls, n_p)
            o_f, o_b, s_p = _delta_stage(qkvn, gcol, grow, jnp.zeros((bp, 2, H_A, DK_A, DV_A), F32), bp, lp, 0, rg,
                                         precise=True)
            o_f, o_b, _ = _delta_stage(qkvn, gcol, grow, state_delta[:, j].astype(F32), bs, ls, n_p, rg,
                                       into=(o_f, o_b))
            sd_new.append(s_p)
            lam_p = ev_lambda[j].astype(F32)
            nb = hb2 // LANES
            qp, kp, vp = [(att, lambda h: h)], [(att, lambda h: nb + h)], (att, lambda h: 2 * nb + h)
            o_d = _flash(qp, kp, vp, None, lam_p, n_seq=bp, n_heads=H_B, lq=lp, lk=lp, q_row0=0, k_row0=0,
                         tq=_tile(lp, 256), tk=_tile(lp, 512), diff=True, lam_init=lam_init, out_cols=hb2,
                         heads_per_step=H_B)
            ck = cache_diff_k[:, j].transpose(0, 2, 1, 3).reshape(bs * past, H_B * 2 * DQK_B).astype(BF16)
            cv = cache_diff_v[:, j].transpose(0, 2, 1, 3).reshape(bs * past, H_B * DV_B).astype(BF16)
            cache = ([(ck, lambda h: h)], (cv, lambda h: h), past)
            o_d = _flash(qp, kp, vp, cache, lam_p, n_seq=bs, n_heads=H_B, lq=ls, lk=ls, q_row0=n_p, k_row0=n_p,
                         tq=_tile(ls, 512), tk=_tile(ls, 2048), diff=True, lam_init=lam_init, out_cols=hb2, into=o_d,
                         heads_per_step=FLASH_HEADS_PER_STEP)
            dk_new.append(kv_f[:n_p, :hb2].reshape(bp, lp, H_B, 2 * DQK_B).transpose(0, 2, 1, 3))
            dv_new.append(kv_f[:n_p, hb2:].reshape(bp, lp, H_B, DV_B).transpose(0, 2, 1, 3))
            ff = ffn_w_down.shape[1]
            x = _even_tail(x, o_f, o_b, z_gate, o_d, ev_onorm[j], ev_subln[j], ev_w_out[j].astype(BF16),
                           norm_ffn[layer], mods, ffn_w_gu[j].astype(BF16), ffn_w_down[j].astype(BF16), rows, 2, 3,
                           ff, lam_init)
        else:
            n_in = Q_LORA + KV_LORA + ROPE_C
            pad = (-n_in) % LANES
            w_in = jnp.concatenate([od_w_in[j], jnp.zeros((d, pad), od_w_in.dtype)], axis=1).astype(BF16)
            wq = od_w_uq[j].reshape(Q_LORA, H_C, NOPE_C + ROPE_C)
            wq_rope = jnp.concatenate([wq[:, :, NOPE_C:], jnp.zeros((Q_LORA, H_C, LANES - ROPE_C), wq.dtype)], axis=2)
            wq2 = jnp.concatenate([wq[:, :, :NOPE_C].reshape(Q_LORA, H_C * NOPE_C),
                                   wq_rope.reshape(Q_LORA, H_C * LANES)], axis=1).astype(BF16)
            q_all, kv_tok, kr_tok, ckv_n, kr_f = _odd_proj(
                x, norm_mix[layer], mods, w_in, od_q_norm[j], od_kv_norm[j], wq2, od_w_ukv[j].astype(BF16),
                (cos_t, sin_t), rows, mla_scale, H_C * NOPE_C)
            ckv_new.append(ckv_n[:n_p].reshape(bp, lp, KV_LORA))
            kr_new.append(kr_f[:n_p, :ROPE_C].reshape(bp, lp, ROPE_C))
            rows_k = _Rows(bs * past, 0, past, _tile(past, 256))
            kv_c = _norm_matmul(cache_mla_ckv[:, j].reshape(bs * past, KV_LORA), od_w_ukv[j].astype(BF16), rows_k,
                                tn=512, out_dtype=BF16)
            kr_c = jnp.concatenate([cache_mla_krope[:, j].reshape(bs * past, ROPE_C),
                                    jnp.zeros((bs * past, LANES - ROPE_C), F32)], axis=1).astype(BF16)
            qp = [(q_all, lambda h: h), (q_all, lambda h: H_C + h)]
            kp = [(kv_tok, lambda h: 2 * h), (kr_tok, lambda h: 0)]
            vp = (kv_tok, lambda h: 2 * h + 1)
            o = _flash(qp, kp, vp, None, None, n_seq=bp, n_heads=H_C, lq=lp, lk=lp, q_row0=0, k_row0=0,
                       tq=_tile(lp, 256), tk=_tile(lp, 512), diff=False, lam_init=0.0, out_cols=H_C * V_C,
                       heads_per_step=H_C)
            cache = ([(kv_c, lambda h: 2 * h), (kr_c, lambda h: 0)], (kv_c, lambda h: 2 * h + 1), past)
            o = _flash(qp, kp, vp, cache, None, n_seq=bs, n_heads=H_C, lq=ls, lk=ls, q_row0=n_p, k_row0=n_p,
                       tq=_tile(ls, 1024), tk=_tile(ls, 2048), diff=False, lam_init=0.0, out_cols=H_C * V_C, into=o,
                       heads_per_step=FLASH_HEADS_PER_STEP)
            x = _matmul_residual(x, o, od_w_out[j].astype(BF16), mods, rows, 2)
            x = _moe(x, norm_ffn[layer], mods, moe_router[j], moe_w_gu[j].astype(BF16), moe_w_down[j].astype(BF16),
                     rows_m, 3, final_w=final_norm if layer == depth - 1 else None)

    if depth % 2 == 0:
        y_p, y_s = x
    else:
        y = _final_norm(x, final_norm, rows.tm)
        y_p, y_s = y[:n_p], y[n_p:]
    return (y_p.reshape(bp, lp, d), y_s.reshape(bs, ls, d), jnp.stack(sd_new, axis=1),
            jnp.stack(dk_new, axis=1), jnp.stack(dv_new, axis=1), jnp.stack(ckv_new, axis=1),
            jnp.stack(kr_new, axis=1))
```

```python
import functools
import math

import jax
import jax.numpy as jnp
from jax import lax
from jax.experimental import pallas as pl
from jax.experimental.pallas import tpu as pltpu

F32 = jnp.float32
BF16 = jnp.bfloat16

EPS = 1e-6
LOG2E = 1.4426950408889634
GRID_W = 64
ROPE_BASE = 10000.0
H_A, DK_A, DV_A = 4, 128, 128
CONV_K = 3
CHUNK = 64
H_B, DQK_B, DV_B = 4, 64, 128
H_C, NOPE_C, ROPE_C, V_C = 8, 128, 64, 128
Q_LORA, KV_LORA = 384, 256
LANES = 128
SUBLANES = 8
VMEM_LIMIT_BYTES = 56 * 1024 * 1024


def _cparams(*sem):
    return pltpu.CompilerParams(dimension_semantics=sem, vmem_limit_bytes=VMEM_LIMIT_BYTES)


def _dot(a, b):
    return jnp.dot(a, b, preferred_element_type=F32)


def _dot_nt(a, b):
    return lax.dot_general(a, b, (((1,), (1,)), ((), ())), preferred_element_type=F32)


def _dot_tn(a, b):
    return lax.dot_general(a, b, (((0,), (0,)), ((), ())), preferred_element_type=F32)


def _split3(x):
    hi = x.astype(BF16)
    r = x - hi.astype(F32)
    mid = r.astype(BF16)
    lo = (r - mid.astype(F32)).astype(BF16)
    return hi, mid, lo


def _dot_f32(a, b):
    ah, am, al = _split3(a)
    bh, bm, bl = _split3(b)
    return (_dot(ah, bh) + (_dot(ah, bm) + _dot(am, bh))
            + (_dot(am, bm) + _dot(ah, bl) + _dot(al, bh)))


def _silu(x):
    return x * jax.nn.sigmoid(x)


def _tile(n, pref, mult=SUBLANES):
    t = min(n, pref)
    while t > mult and (n % t or t % mult):
        t -= mult
    assert n % t == 0, (n, pref)
    return t


class _Rows:
    def __init__(self, n_ctx, n_lat, lat_len, tm):
        assert n_ctx % tm == 0 and lat_len % tm == 0
        self.tm = tm
        self.n = n_ctx + n_lat
        self.ctx_tiles = n_ctx // tm
        self.seq_tiles = lat_len // tm

    def group(self, i):
        return jnp.where(i < self.ctx_tiles, 0, 1 + jnp.maximum(i - self.ctx_tiles, 0) // self.seq_tiles)

    def pos_block(self, i):
        return jnp.where(i < self.ctx_tiles, 0, 1 + jnp.maximum(i - self.ctx_tiles, 0) % self.seq_tiles)


def _mod_kernel(c_ref, w_ref, b_ref, o_ref):
    o_ref[...] = _dot_f32(_silu(c_ref[...]), w_ref[...]) + b_ref[...]


def _mods(cc, mod_w, mod_b):
    depth, d, n6 = mod_w.shape
    tn = _tile(n6, 512, LANES)
    return pl.pallas_call(
        _mod_kernel,
        grid=(depth, n6 // tn),
        in_specs=[pl.BlockSpec(cc.shape, lambda l, j: (0, 0)),
                  pl.BlockSpec((None, d, tn), lambda l, j: (l, 0, j)),
                  pl.BlockSpec((None, 1, tn), lambda l, j: (l, 0, j))],
        out_specs=pl.BlockSpec((None, cc.shape[0], tn), lambda l, j: (l, 0, j)),
        out_shape=jax.ShapeDtypeStruct((depth, cc.shape[0], n6), F32),
        compiler_params=_cparams("parallel", "parallel"),
        name="mods",
    )(cc, mod_w, mod_b.reshape(depth, 1, n6))


def _mod_chunk(mods_ref, g, idx, d):
    return mods_ref[pl.ds(g, 1), idx * d:(idx + 1) * d]


def _rmsnorm_rows(x, w):
    ms = jnp.mean(x * x, axis=-1, keepdims=True)
    return x * lax.rsqrt(ms + EPS) * w


def _rope(x, cos, sin_signed):
    lane = lax.broadcasted_iota(jnp.int32, (1, LANES), 1)
    low = (lane % 32) < 16
    outs = []
    for c in range(x.shape[1] // LANES):
        xc = x[:, c * LANES:(c + 1) * LANES]
        fwd = pltpu.roll(xc, LANES - 16, axis=1)
        bwd = pltpu.roll(xc, 16, axis=1)
        outs.append(xc * cos + jnp.where(low, fwd, bwd) * sin_signed)
    return outs[0] if len(outs) == 1 else jnp.concatenate(outs, axis=1)


def _rope_tables(rows, lat_len):
    pos = jnp.arange(lat_len)
    r = (pos // GRID_W).astype(F32)
    c = (pos % GRID_W).astype(F32)
    inv = ROPE_BASE ** (-jnp.arange(0, 32, 2, dtype=F32) / 32)
    ang_r = r[:, None] * inv[None, :]
    ang_c = c[:, None] * inv[None, :]
    ang = jnp.concatenate([ang_r, ang_r, ang_c, ang_c] * 2, axis=-1)
    sign = jnp.where((jnp.arange(LANES) % 32) < 16, -1.0, 1.0).astype(F32)
    cos = jnp.concatenate([jnp.ones((rows.tm, LANES), F32), jnp.cos(ang)], axis=0)
    sin = jnp.concatenate([jnp.zeros((rows.tm, LANES), F32), jnp.sin(ang) * sign[None, :]], axis=0)
    return cos, sin


def _nm_kernel(*refs, rows, d_mod, has_norm, mod_idx, has_rope, epi, emit_h):
    it = iter(refs)
    x_ref = next(it)
    w_ref = next(it)
    nw_ref = next(it) if has_norm else None
    mods_ref = next(it) if mod_idx is not None else None
    cos_ref, sin_ref = (next(it), next(it)) if has_rope else (None, None)
    o_ref = next(it)
    h_out_ref = next(it) if emit_h else None
    h_scr = next(it)
    i, j = pl.program_id(0), pl.program_id(1)

    @pl.when(j == 0)
    def _():
        x = x_ref[...].astype(F32)
        if has_norm:
            x = _rmsnorm_rows(x, nw_ref[...])
        if mod_idx is not None:
            g = rows.group(i)
            x = x * (1.0 + _mod_chunk(mods_ref, g, mod_idx[1], d_mod)) + _mod_chunk(mods_ref, g, mod_idx[0], d_mod)
        h_scr[...] = x.astype(BF16)
        if emit_h:
            h_out_ref[...] = x

    acc = _dot(h_scr[...], w_ref[...])
    kinds = sorted(set(epi))
    for kind in kinds:
        cond = None
        for jj, e in enumerate(epi):
            if e == kind:
                cond = (j == jj) if cond is None else (cond | (j == jj))

        def _store(kind=kind):
            y = acc
            if kind[0]:
                y = _rope(y, cos_ref[...], sin_ref[...])
            if kind[1] != 1.0:
                y = y * kind[1]
            o_ref[...] = y.astype(o_ref.dtype)

        if len(kinds) == 1:
            _store()
        else:
            pl.when(cond)(_store)


def _norm_matmul(x, w, rows, *, tn, norm_w=None, mods=None, mod_idx=None, rope=None, epi=None,
                 emit_h=False, out_dtype=F32):
    n, k = x.shape
    nout = w.shape[1]
    nj = nout // tn
    tm = rows.tm
    epi = tuple(epi) if epi is not None else ((False, 1.0),) * nj
    assert len(epi) == nj and n % tm == 0
    has_rope = any(e[0] for e in epi)
    args = [x, w]
    in_specs = [pl.BlockSpec((tm, k), lambda i, j: (i, 0)), pl.BlockSpec((k, tn), lambda i, j: (0, j))]
    if norm_w is not None:
        args.append(norm_w.reshape(1, k).astype(F32))
        in_specs.append(pl.BlockSpec((1, k), lambda i, j: (0, 0)))
    if mod_idx is not None:
        args.append(mods)
        in_specs.append(pl.BlockSpec(mods.shape, lambda i, j: (0, 0)))
    if has_rope:
        args += list(rope)
        in_specs += [pl.BlockSpec((tm, LANES), lambda i, j: (rows.pos_block(i), 0))] * 2
    out_shape = [jax.ShapeDtypeStruct((n, nout), out_dtype)]
    out_specs = [pl.BlockSpec((tm, tn), lambda i, j: (i, j))]
    if emit_h:
        out_shape.append(jax.ShapeDtypeStruct((n, k), F32))
        out_specs.append(pl.BlockSpec((tm, k), lambda i, j: (i, 0)))
    kern = functools.partial(_nm_kernel, rows=rows, d_mod=k, has_norm=norm_w is not None, mod_idx=mod_idx,
                             has_rope=has_rope, epi=epi, emit_h=emit_h)
    out = pl.pallas_call(
        kern, grid=(n // tm, nj), in_specs=in_specs, out_specs=out_specs, out_shape=out_shape,
        scratch_shapes=[pltpu.VMEM((tm, k), BF16)],
        compiler_params=_cparams("parallel", "arbitrary"), name="norm_matmul",
    )(*args)
    return out if emit_h else out[0]


PROJ_CHUNK = 512


def _pos_in_seq(r, length):
    return r & (length - 1) if length & (length - 1) == 0 else lax.rem(r, length)


def _even_proj_kernel(x_ref, xp_ref, xn_ref, nw_ref, mods_ref, w_ref, cw_ref, alog_ref, dt_ref, cos_ref, sin_ref,
                      qkv_ref, z_ref, gc_ref, gr_ref, att_ref, kv_ref,
                      *, rows, n_conv, n_z, hb2, q_scale, ctx_len, lat_len, n_ctx_rows):
    i = pl.program_id(0)
    tm, d = x_ref.shape
    g = rows.group(i)

    def modulated(ref):
        v = _rmsnorm_rows(ref[...], nw_ref[...])
        return (v * (1.0 + _mod_chunk(mods_ref, g, 1, d)) + _mod_chunk(mods_ref, g, 0, d)).astype(BF16)

    h = modulated(x_ref)
    halo = xp_ref.shape[0]
    h_ext = jnp.concatenate([h, modulated(xp_ref), modulated(xn_ref)], axis=0)
    p_ext = jnp.concatenate([_dot(h_ext, w_ref[:, c0:min(c0 + PROJ_CHUNK, n_conv)])
                             for c0 in range(0, n_conv, PROJ_CHUNK)], axis=1)
    p = p_ext[:tm]
    p_before = p_ext[tm + halo - 1:tm + halo, :]
    p_after = p_ext[tm + halo:tm + halo + 1, :]
    row = lax.broadcasted_iota(jnp.int32, (tm, 1), 0)
    r_glob = row + i * tm
    in_ctx = i < rows.ctx_tiles
    pos = jnp.where(in_ctx, _pos_in_seq(r_glob, ctx_len), _pos_in_seq(r_glob - n_ctx_rows, lat_len))
    seq_last = jnp.where(in_ctx, ctx_len - 1, lat_len - 1)
    x_prev = jnp.where(pos == 0, 0.0, jnp.where(row == 0, p_before, pltpu.roll(p, 1, axis=0)))
    x_next = jnp.where(pos == seq_last, 0.0, jnp.where(row == tm - 1, p_after, pltpu.roll(p, tm - 1, axis=0)))
    cw = cw_ref[...]
    y = _silu(x_prev * cw[0:1, :] + p * cw[1:2, :] + x_next * cw[2:3, :])
    n_qk = 2 * H_A
    for hh in range(n_conv // LANES):
        yh = y[:, hh * LANES:(hh + 1) * LANES]
        if hh < n_qk:
            yh = yh * lax.rsqrt(jnp.sum(yh * yh, axis=-1, keepdims=True) + EPS)
            if hh < n_qk // 2:
                yh = yh * (DK_A ** -0.5)
        qkv_ref[:, hh * LANES:(hh + 1) * LANES] = yh
    z_ref[...] = _dot(h, w_ref[:, n_conv:n_conv + n_z])
    s = _dot(h, w_ref[:, n_conv + n_z:n_conv + n_z + LANES])
    lane = lax.broadcasted_iota(jnp.int32, (1, LANES), 1)
    zg = s + dt_ref[...]
    softplus = jnp.maximum(zg, 0.0) + jnp.log(1.0 + jnp.exp(-jnp.abs(zg)))
    gates = jnp.where((lane % 4) < 2, -jnp.exp(alog_ref[...]) * softplus, jax.nn.sigmoid(s))
    gc_ref[...] = gates
    gr_ref[...] = jnp.transpose(gates)[:gr_ref.shape[0], :]
    n_main = n_conv + n_z + LANES
    cos, sin = cos_ref[...], sin_ref[...]
    for part in range(3):
        for c0 in range(0, hb2, PROJ_CHUNK):
            c1 = min(c0 + PROJ_CHUNK, hb2)
            y = _dot(h, w_ref[:, n_main + part * hb2 + c0:n_main + part * hb2 + c1])
            if part < 2:
                y = _rope(y, cos, sin)
            if part == 0:
                y = y * q_scale
            else:
                kv_ref[:, (part - 1) * hb2 + c0:(part - 1) * hb2 + c1] = y
            att_ref[:, part * hb2 + c0:part * hb2 + c1] = y.astype(BF16)


def _even_proj(x, norm_w, mods, w, conv_w, a_log, dt_bias, rope, rows, n_conv, n_z, hb2, q_scale, ctx_len,
               lat_len, n_ctx_rows):
    n, d = x.shape
    tm = rows.tm
    assert conv_w.shape[0] == CONV_K
    halo = 2 * SUBLANES
    hb = tm // halo
    alog_row = jnp.zeros((1, LANES), F32).at[0, :4 * H_A].set(
        jnp.stack([a_log[0], a_log[1], a_log[0], a_log[1]], axis=-1).reshape(-1).astype(F32))
    dt_row = jnp.zeros((1, LANES), F32).at[0, :4 * H_A].set(
        jnp.stack([dt_bias[0], dt_bias[1], jnp.zeros_like(dt_bias[0]), jnp.zeros_like(dt_bias[0])],
                  axis=-1).reshape(-1).astype(F32))
    kern = functools.partial(_even_proj_kernel, rows=rows, n_conv=n_conv, n_z=n_z, hb2=hb2, q_scale=q_scale,
                             ctx_len=ctx_len, lat_len=lat_len, n_ctx_rows=n_ctx_rows)
    tab = pl.BlockSpec((tm, LANES), lambda i: (rows.pos_block(i), 0))
    full = lambda a: pl.BlockSpec(a.shape, lambda i: (0, 0))
    nw = norm_w.reshape(1, d).astype(F32)
    cw = conv_w.astype(F32)
    outs = [(n_conv, F32), (n_z, F32), (LANES, F32), None, (3 * hb2, BF16), (2 * hb2, F32)]
    out_specs = [pl.BlockSpec((tm, c[0]), lambda i: (i, 0)) if c else pl.BlockSpec((4 * H_A, tm), lambda i: (0, i))
                 for c in outs]
    out_shape = [jax.ShapeDtypeStruct((n, c[0]), c[1]) if c else jax.ShapeDtypeStruct((4 * H_A, n), F32)
                 for c in outs]
    return pl.pallas_call(
        kern, grid=(n // tm,),
        in_specs=[pl.BlockSpec((tm, d), lambda i: (i, 0)),
                  pl.BlockSpec((halo, d), lambda i: (jnp.maximum(i * hb - 1, 0), 0)),
                  pl.BlockSpec((halo, d), lambda i: (jnp.minimum((i + 1) * hb, n // halo - 1), 0)),
                  full(nw), full(mods), full(w), full(cw), full(alog_row), full(dt_row), tab, tab],
        out_specs=out_specs, out_shape=out_shape,
        compiler_params=_cparams("parallel"), name="even_proj",
    )(x, x, x, nw, mods, w, cw, alog_row, dt_row, *rope)


def _odd_proj_kernel(x_ref, nw_ref, mods_ref, w_ref, qn_ref, kvn_ref, wq_ref, wkv_ref, cos_ref, sin_ref,
                     q_ref, kv_ref, kr_ref, ckv_ref, krf_ref, *, rows, scale, n_nope):
    i = pl.program_id(0)
    d = x_ref.shape[1]
    h = _modulated(x_ref, nw_ref, mods_ref, rows.group(i), 0, d).astype(BF16)
    p = _dot(h, w_ref[...])
    cos, sin = cos_ref[...], sin_ref[...]
    cq = _rmsnorm_rows(p[:, :Q_LORA], qn_ref[...]).astype(BF16)
    for c0 in range(0, wq_ref.shape[1], PROJ_CHUNK):
        y = _dot(cq, wq_ref[:, c0:c0 + PROJ_CHUNK])
        if c0 >= n_nope:
            y = _rope(y, cos, sin)
        q_ref[:, c0:c0 + PROJ_CHUNK] = (y * scale).astype(BF16)
    ckv = _rmsnorm_rows(p[:, Q_LORA:Q_LORA + KV_LORA], kvn_ref[...])
    ckv_ref[...] = ckv
    ckv16 = ckv.astype(BF16)
    for c0 in range(0, wkv_ref.shape[1], PROJ_CHUNK):
        kv_ref[:, c0:c0 + PROJ_CHUNK] = _dot(ckv16, wkv_ref[:, c0:c0 + PROJ_CHUNK]).astype(BF16)
    kr = _rope(p[:, Q_LORA + KV_LORA:], cos, sin)
    krf_ref[...] = kr
    kr_ref[...] = kr.astype(BF16)


def _odd_proj(x, norm_w, mods, w_in, q_norm, kv_norm, wq, wkv, rope, rows, scale, n_nope):
    n, d = x.shape
    tm = rows.tm
    tab = pl.BlockSpec((tm, LANES), lambda i: (rows.pos_block(i), 0))
    full = lambda a: pl.BlockSpec(a.shape, lambda i: (0, 0))
    qn = q_norm.reshape(1, -1).astype(F32)
    kvn = kv_norm.reshape(1, -1).astype(F32)
    nw = norm_w.reshape(1, d).astype(F32)
    outs = [(wq.shape[1], BF16), (wkv.shape[1], BF16), (LANES, BF16), (KV_LORA, F32), (LANES, F32)]
    return pl.pallas_call(
        functools.partial(_odd_proj_kernel, rows=rows, scale=scale, n_nope=n_nope), grid=(n // tm,),
        in_specs=[pl.BlockSpec((tm, d), lambda i: (i, 0)), full(nw), full(mods), full(w_in), full(qn), full(kvn),
                  full(wq), full(wkv), tab, tab],
        out_specs=[pl.BlockSpec((tm, c), lambda i: (i, 0)) for c, _ in outs],
        out_shape=[jax.ShapeDtypeStruct((n, c), dt) for c, dt in outs],
        compiler_params=_cparams("parallel"), name="odd_proj",
    )(x, nw, mods, w_in, qn, kvn, wq, wkv, *rope)


def _split2(x):
    hi = x.astype(BF16)
    return hi, (x - hi.astype(F32)).astype(BF16)


def _mm(a, b, precise, dot=_dot):
    if not precise:
        return dot(a.astype(BF16), b.astype(BF16))
    ah, al = _split2(a)
    bh, bl = _split2(b)
    return dot(ah, bh) + (dot(ah, bl) + dot(al, bh))


SOLVE_BLOCK = SUBLANES
MOE_ROW_BLOCK = 256
FLASH_HEADS_PER_STEP = 2
DELTA_HEADS_PER_STEP = 4


def _tri_solve_many(ms, rs, revs):
    c, width = rs[0].shape
    blk = SOLVE_BLOCK
    nb, gpb, ng = c // blk, blk // SUBLANES, c // SUBLANES
    col = lax.broadcasted_iota(jnp.int32, (blk, c), 1)
    mgs = [[m[SUBLANES * g:SUBLANES * (g + 1), :] for g in range(ng)] for m in ms]
    xgs = [[r[SUBLANES * g:SUBLANES * (g + 1), :] for g in range(ng)] for r in rs]
    zero_blk = jnp.zeros((blk, width), BF16)
    fin_h = [[zero_blk] * nb for _ in ms]
    fin_l = [[zero_blk] * nb for _ in ms]
    for bi in range(nb):
        blocks = [nb - 1 - bi if rev else bi for rev in revs]
        if bi > 0:
            for s, (rev, b) in enumerate(zip(revs, blocks)):
                done = (col >= (b + 1) * blk) if rev else (col < b * blk)
                lh, ll = _split2(jnp.where(done, ms[s][b * blk:(b + 1) * blk, :], 0.0))
                xh = jnp.concatenate(fin_h[s], axis=0)
                xl = jnp.concatenate(fin_l[s], axis=0)
                upd = _dot(lh, xh) + (_dot(lh, xl) + _dot(ll, xh))
                for gg in range(gpb):
                    g = b * gpb + gg
                    xgs[s][g] = xgs[s][g] - upd[SUBLANES * gg:SUBLANES * (gg + 1), :]
        for t in range(blk - 1):
            for s, (rev, b) in enumerate(zip(revs, blocks)):
                j = b * blk + (blk - 1 - t if rev else t)
                xj = xgs[s][j // SUBLANES][j % SUBLANES:j % SUBLANES + 1, :]
                groups = range(b * gpb, (j - 1) // SUBLANES + 1) if rev else range((j + 1) // SUBLANES, (b + 1) * gpb)
                for g in groups:
                    xgs[s][g] = xgs[s][g] - mgs[s][g][:, j:j + 1] * xj
        if bi < nb - 1:
            for s, b in enumerate(blocks):
                fin_h[s][b], fin_l[s][b] = _split2(jnp.concatenate(xgs[s][b * gpb:(b + 1) * gpb], axis=0))
    return [jnp.concatenate(xg, axis=0) for xg in xgs]


def _delta_kernel(qf, kf, vf, gcf, grf, qb, kb, vb, gcb, grb, s0_ref, *rest, chunk, hpb, n_alias, precise):
    of_ref, ob_ref, so_ref, s_scr = rest[n_alias:]
    h0 = pl.program_id(1) * hpb
    i = pl.program_id(2)
    n_i = pl.num_programs(2)

    @pl.when(i == 0)
    def _():
        s_scr[...] = s0_ref[...]

    rg = qf.shape[0]
    nc = rg // chunk
    rowi = lax.broadcasted_iota(jnp.int32, (rg, rg), 0)
    coli = lax.broadcasted_iota(jnp.int32, (rg, rg), 1)
    same = (rowi // chunk) == (coli // chunk)
    lane = lax.broadcasted_iota(jnp.int32, (1, LANES), 1)
    r_c = lax.broadcasted_iota(jnp.int32, (chunk, chunk), 0)
    c_c = lax.broadcasted_iota(jnp.int32, (chunk, chunk), 1)

    prep = []
    for d, (q_ref, k_ref, v_ref, gc_ref, gr_ref) in enumerate(((qf, kf, vf, gcf, grf), (qb, kb, vb, gcb, grb))):
        rev = d == 1
        incl_big = same & ((rowi <= coli) if rev else (rowi >= coli))
        lm = jnp.where(incl_big, 1.0, 0.0).astype(BF16)
        gcols = gc_ref[...]
        grows = gr_ref[...]
        gh, gm, gl_ = _split3(gcols)
        cum_c = _dot(lm, gh) + _dot(lm, gm) + _dot(lm, gl_)
        th, tm_, tl = _split3(grows)
        cum_r = _dot_nt(th, lm) + _dot_nt(tm_, lm) + _dot_nt(tl, lm)

        def col(a, ln):
            return jnp.sum(jnp.where(lane == ln, a, 0.0), axis=1, keepdims=True)

        sub = lax.broadcasted_iota(jnp.int32, (cum_r.shape[0], 1), 0)
        incl = (r_c <= c_c) if rev else (r_c >= c_c)
        strict = (r_c < c_c) if rev else (r_c > c_c)
        for hh in range(hpb):
            h = h0 + hh
            hs = slice(hh * LANES, (hh + 1) * LANES)
            gcum = col(cum_c, 4 * h + d)
            beta = col(gcols, 4 * h + 2 + d)
            grow = jnp.sum(jnp.where(sub == 4 * h + d, cum_r, 0.0), axis=0, keepdims=True)
            q = q_ref[:, hs]
            k = k_ref[:, hs]
            v = v_ref[:, hs]
            kbeta = k * beta
            eg = jnp.exp(gcum)
            rhs = jnp.concatenate([v * beta, kbeta * eg], axis=1)
            ms, rs, a_in = [], [], []
            for c in range(nc):
                sl = slice(c * chunk, (c + 1) * chunk)
                e = jnp.exp(jnp.where(incl, gcum[sl] - grow[:, sl], 0.0))
                kk = _mm(kbeta[sl], k[sl], precise, _dot_nt)
                qk = _mm(q[sl], k[sl], precise, _dot_nt)
                a_in.append(jnp.where(incl, qk * e, 0.0))
                ms.append(jnp.where(strict, kk * e, 0.0))
                rs.append(rhs[sl])
            prep.append((rev, hh, ms, rs, a_in, q * eg, k, gcum))

    all_terms = []
    half = len(prep) // 2
    for wave in (prep[:half], prep[half:]) if half else (prep,):
        sols = _tri_solve_many([m for p in wave for m in p[2]], [r for p in wave for r in p[3]],
                               [p[0] for p in wave for _ in range(nc)])
        for n_u, (rev, hh, _, _, a_in, qg, k, gcum) in enumerate(wave):
            xs = sols[n_u * nc:(n_u + 1) * nc]
            terms = []
            for c in range(nc):
                sl = slice(c * chunk, (c + 1) * chunk)
                r_last = c * chunk if rev else (c + 1) * chunk - 1
                g_last = gcum[r_last:r_last + 1, :]
                k_dec = k[sl] * jnp.exp(g_last - gcum[sl])
                a_uw = _mm(a_in[c], xs[c], precise)
                kd_uw = _mm(k_dec, xs[c], precise, _dot_tn)
                lhs = jnp.concatenate([qg[sl] - a_uw[:, DV_A:], kd_uw[:, DV_A:]], axis=0)
                terms.append((lhs, a_uw[:, :DV_A], kd_uw[:, :DV_A], jnp.exp(g_last)))
            all_terms.append(terms)

    states = [s_scr[1 if p[0] else 0, p[1]] for p in prep]
    outs = [[None] * nc for _ in prep]
    for t in range(nc):
        for n_u, p in enumerate(prep):
            c = nc - 1 - t if p[0] else t
            lhs, o_0, q_0, decay = all_terms[n_u][c]
            prod = _mm(lhs, states[n_u], precise)
            outs[n_u][c] = o_0 + prod[:chunk]
            states[n_u] = states[n_u] * decay - prod[chunk:] + q_0
    for n_u, (rev, hh) in enumerate((p[0], p[1]) for p in prep):
        s_scr[1 if rev else 0, hh] = states[n_u]
        (ob_ref if rev else of_ref)[:, hh * LANES:(hh + 1) * LANES] = jnp.concatenate(outs[n_u], axis=0)

    @pl.when(i == n_i - 1)
    def _():
        so_ref[...] = s_scr[...]


def _delta_stage(qkvn, gcol, grow, s0, n_seq, seq_len, row0, rg, into=None, precise=False):
    ng = seq_len // rg
    b0 = row0 // rg

    def fwd_blk(b, i):
        return b0 + b * ng + i

    def bwd_blk(b, i):
        return b0 + b * ng + (ng - 1 - i)

    hpb = DELTA_HEADS_PER_STEP
    nhp = H_A // hpb
    hw = hpb * LANES

    def specs(blk):
        return [pl.BlockSpec((rg, hw), lambda b, h, i: (blk(b, i), h)),
                pl.BlockSpec((rg, hw), lambda b, h, i: (blk(b, i), nhp + h)),
                pl.BlockSpec((rg, hw), lambda b, h, i: (blk(b, i), 2 * nhp + h)),
                pl.BlockSpec((rg, LANES), lambda b, h, i: (blk(b, i), 0)),
                pl.BlockSpec((4 * H_A, rg), lambda b, h, i: (0, blk(b, i)))]

    st_spec = pl.BlockSpec((None, 2, hpb, DK_A, DV_A), lambda b, h, i: (b, 0, h, 0, 0))
    args = [qkvn, qkvn, qkvn, gcol, grow, qkvn, qkvn, qkvn, gcol, grow, s0]
    in_specs = specs(fwd_blk) + specs(bwd_blk) + [st_spec]
    aliases = {}
    if into is not None:
        for k_out, arr in enumerate(into):
            aliases[len(args)] = k_out
            args.append(arr)
            in_specs.append(pl.BlockSpec(memory_space=pl.ANY))
    n = qkvn.shape[0]
    return pl.pallas_call(
        functools.partial(_delta_kernel, chunk=CHUNK, hpb=hpb, n_alias=len(aliases), precise=precise),
        grid=(n_seq, nhp, ng),
        in_specs=in_specs,
        out_specs=[pl.BlockSpec((rg, hw), lambda b, h, i: (fwd_blk(b, i), h)),
                   pl.BlockSpec((rg, hw), lambda b, h, i: (bwd_blk(b, i), h)),
                   st_spec],
        out_shape=[jax.ShapeDtypeStruct((n, H_A * DV_A), F32),
                   jax.ShapeDtypeStruct((n, H_A * DV_A), F32),
                   jax.ShapeDtypeStruct(s0.shape, F32)],
        input_output_aliases=aliases,
        scratch_shapes=[pltpu.VMEM((2, hpb, DK_A, DV_A), F32)],
        compiler_params=_cparams("parallel", "parallel", "arbitrary"), name="gated_delta",
    )(*args)


def _flash_kernel(*refs, n_parts, has_cache, diff, tk, lam_init, aliased, head_lanes):
    it = iter(refs)
    q_refs = [next(it) for _ in range(n_parts)]
    k_refs = [next(it) for _ in range(n_parts)]
    v_ref = next(it)
    kc_refs = [next(it) for _ in range(n_parts)] if has_cache else []
    vc_ref = next(it) if has_cache else None
    lam_ref = next(it) if diff else None
    if aliased:
        next(it)
    o_ref = next(it)
    tq = q_refs[0].shape[0]
    lk = k_refs[0].shape[0]
    if diff:
        lp = lam_ref[...]
        lam = (jnp.exp(jnp.sum(lp[0:1] * lp[1:2], axis=-1, keepdims=True))
               - jnp.exp(jnp.sum(lp[2:3] * lp[3:4], axis=-1, keepdims=True)) + lam_init)
        lane = lax.broadcasted_iota(jnp.int32, (1, LANES), 1)

    def cols(ref, off, rows=None):
        return ref[:, off:off + LANES] if rows is None else ref[rows, off:off + LANES]

    def block(carry, q_cat, ks, vv):
        m, l, acc = carry
        s = _dot_nt(q_cat, ks[0] if n_parts == 1 else jnp.concatenate(ks, axis=1))
        m_new = jnp.maximum(m, jnp.max(s, axis=-1, keepdims=True))
        alpha = jnp.exp2(m - m_new)
        p_ = jnp.exp2(s - m_new)
        l = alpha * l + jnp.sum(p_, axis=-1, keepdims=True)
        acc = alpha * acc + _dot(p_.astype(BF16), vv)
        return m_new, l, acc

    q_cats, carries = [], []
    for q_offs, *_ in head_lanes:
        qs = [cols(r, off) for r, off in zip(q_refs, q_offs)]
        if diff:
            q = qs[0]
            qs = [jnp.concatenate([jnp.where(lane < DQK_B, q, 0), jnp.where(lane >= DQK_B, q, 0)], axis=0)]
        nrow = qs[0].shape[0]
        q_cats.append(qs[0] if n_parts == 1 else jnp.concatenate(qs, axis=1))
        carries.append((jnp.full((nrow, 1), -1e30, F32), jnp.zeros((nrow, 1), F32), jnp.zeros((nrow, LANES), F32)))

    for t in range(lk // tk):
        rows = slice(t * tk, (t + 1) * tk)
        for n_h, (_, k_offs, v_off, _, _, _) in enumerate(head_lanes):
            carries[n_h] = block(carries[n_h], q_cats[n_h], [cols(r, off, rows) for r, off in zip(k_refs, k_offs)],
                                 cols(v_ref, v_off, rows))
    for n_h, (_, _, _, o_off, kc_offs, vc_off) in enumerate(head_lanes):
        carry = carries[n_h]
        if has_cache:
            carry = block(carry, q_cats[n_h], [cols(r, off) for r, off in zip(kc_refs, kc_offs)], cols(vc_ref, vc_off))
        _, l, acc = carry
        o = acc / l
        if diff:
            o = o[:tq] - lam * o[tq:]
        o_ref[:, o_off:o_off + LANES] = o.astype(o_ref.dtype)


def _flash(q_parts, k_parts, v_part, cache, lam, *, n_seq, n_heads, lq, lk, q_row0, k_row0, tq, tk, diff,
           lam_init, out_cols, into=None, heads_per_step=1):
    nqt = lq // tq
    qb0, kb0 = q_row0 // tq, k_row0 // lk
    hps = heads_per_step
    assert n_heads % hps == 0
    args, in_specs = [], []

    def add(arr, cf, n_rows, row_fn):
        span = 1
        while any(cf(g * hps) // span != cf(g * hps + hps - 1) // span for g in range(n_heads // hps)):
            span *= 2
        offs = tuple((cf(j) - cf(0) // span * span) * LANES for j in range(hps))
        for g in range(n_heads // hps):
            assert offs == tuple((cf(g * hps + j) - cf(g * hps) // span * span) * LANES for j in range(hps))
        args.append(arr)
        in_specs.append(pl.BlockSpec((n_rows, span * LANES),
                                     lambda b, hg, t: (row_fn(b, t), cf(hg * hps) // span)))
        return offs

    q_offs = [add(arr, cf, tq, lambda b, t: qb0 + b * nqt + t) for arr, cf in q_parts]
    k_offs = [add(arr, cf, lk, lambda b, t: kb0 + b) for arr, cf in k_parts]
    v_offs = add(v_part[0], v_part[1], lk, lambda b, t: kb0 + b)
    kc_offs, vc_offs = [(0,) * hps] * len(q_parts), (0,) * hps
    if cache is not None:
        c_k, c_v, c_rows = cache
        kc_offs = [add(arr, cf, c_rows, lambda b, t: b) for arr, cf in c_k]
        vc_offs = add(c_v[0], c_v[1], c_rows, lambda b, t: b)
    head_lanes = tuple((tuple(o[j] for o in q_offs), tuple(o[j] for o in k_offs), v_offs[j], j * LANES,
                        tuple(o[j] for o in kc_offs), vc_offs[j]) for j in range(hps))
    grid = (n_seq, n_heads // hps, nqt)
    out_spec = pl.BlockSpec((tq, hps * LANES), lambda b, hg, t: (qb0 + b * nqt + t, hg))
    if diff:
        args.append(lam)
        in_specs.append(pl.BlockSpec(lam.shape, lambda b, h, t: (0, 0)))
    aliases = {}
    if into is not None:
        aliases[len(args)] = 0
        args.append(into)
        in_specs.append(pl.BlockSpec(memory_space=pl.ANY))
    kern = functools.partial(_flash_kernel, n_parts=len(q_parts), has_cache=cache is not None, diff=diff, tk=tk,
                             lam_init=lam_init, aliased=into is not None, head_lanes=head_lanes)
    return pl.pallas_call(
        kern, grid=grid, in_specs=in_specs, out_specs=out_spec,
        out_shape=jax.ShapeDtypeStruct((q_parts[0][0].shape[0], out_cols), F32 if diff else BF16),
        input_output_aliases=aliases,
        compiler_params=_cparams("parallel", "parallel", "arbitrary"), name="flash_attention",
    )(*args)


def _mr_kernel(x_ref, a_ref, w_ref, mods_ref, o_ref, *, rows, gate_idx):
    i = pl.program_id(0)
    d = x_ref.shape[1]
    gate = _mod_chunk(mods_ref, rows.group(i), gate_idx, d)
    o_ref[...] = x_ref[...] + gate * _dot(a_ref[...].astype(BF16), w_ref[...])


def _matmul_residual(x, a, w, mods, rows, gate_idx):
    n, d = x.shape
    tm = rows.tm
    return pl.pallas_call(
        functools.partial(_mr_kernel, rows=rows, gate_idx=gate_idx), grid=(n // tm,),
        in_specs=[pl.BlockSpec((tm, d), lambda i: (i, 0)),
                  pl.BlockSpec((tm, a.shape[1]), lambda i: (i, 0)),
                  pl.BlockSpec(w.shape, lambda i: (0, 0)),
                  pl.BlockSpec(mods.shape, lambda i: (0, 0))],
        out_specs=pl.BlockSpec((tm, d), lambda i: (i, 0)),
        out_shape=jax.ShapeDtypeStruct((n, d), F32),
        compiler_params=_cparams("parallel"), name="matmul_residual",
    )(x, a, w, mods)


def _modulated(x_ref, nw_ref, mods_ref, g, idx, d):
    x = _rmsnorm_rows(x_ref[...], nw_ref[...])
    return x * (1.0 + _mod_chunk(mods_ref, g, idx + 1, d)) + _mod_chunk(mods_ref, g, idx, d)


def _ffn_kernel(x_ref, nw_ref, mods_ref, wg_ref, wu_ref, wd_ref, of_ref, ob_ref, z_ref, od_ref, on_ref, sn_ref,
                wo_ref, o_ref, h_scr, acc_scr, xn_scr, *, rows, mod0, mix_gate, lam_init):
    i, f = pl.program_id(0), pl.program_id(1)
    d = x_ref.shape[1]
    g = rows.group(i)

    @pl.when(f == 0)
    def _():
        oa = of_ref[...] + ob_ref[...]
        z = z_ref[...]
        od = od_ref[...]
        ya, yb = [], []
        for hh in range(H_A):
            sl = slice(hh * LANES, (hh + 1) * LANES)
            ya.append(_rmsnorm_rows(oa[:, sl], on_ref[...]) * _silu(z[:, sl]))
        for hh in range(H_B):
            sl = slice(hh * LANES, (hh + 1) * LANES)
            yb.append(_rmsnorm_rows(od[:, sl], sn_ref[...]) * (1.0 - lam_init))
        y = jnp.concatenate(ya + yb, axis=1).astype(BF16)
        x1 = x_ref[...] + _mod_chunk(mods_ref, g, mix_gate, d) * _dot(y, wo_ref[...])
        xn_scr[...] = x1
        v = _rmsnorm_rows(x1, nw_ref[...])
        h_scr[...] = (v * (1.0 + _mod_chunk(mods_ref, g, mod0 + 1, d)) + _mod_chunk(mods_ref, g, mod0, d)).astype(BF16)
        acc_scr[...] = jnp.zeros_like(acc_scr)

    h = h_scr[...]
    act = _silu(_dot(h, wg_ref[...])) * _dot(h, wu_ref[...])
    acc_scr[...] += _dot(act.astype(BF16), wd_ref[...])

    @pl.when(f == pl.num_programs(1) - 1)
    def _():
        o_ref[...] = xn_scr[...] + _mod_chunk(mods_ref, g, mod0 + 2, d) * acc_scr[...]


def _even_tail(x, o_f, o_b, z, o_d, onorm, subln, w_out, norm_w, mods, w_gu, w_down, rows, mix_gate, mod0, tf,
               lam_init):
    n, d = x.shape
    tm = rows.tm
    ff = w_down.shape[0]
    nf = ff // tf
    hw = H_A * DV_A
    row = lambda c: pl.BlockSpec((tm, c), lambda i, f: (i, 0))
    one = lambda c: pl.BlockSpec((1, c), lambda i, f: (0, 0))
    wbuf = dict(pipeline_mode=pl.Buffered(1)) if nf == 1 else {}
    return pl.pallas_call(
        functools.partial(_ffn_kernel, rows=rows, mod0=mod0, mix_gate=mix_gate, lam_init=lam_init),
        grid=(n // tm, nf),
        in_specs=[row(d), one(d), pl.BlockSpec(mods.shape, lambda i, f: (0, 0)),
                  pl.BlockSpec((d, tf), lambda i, f: (0, f), **wbuf),
                  pl.BlockSpec((d, tf), lambda i, f: (0, nf + f), **wbuf),
                  pl.BlockSpec((tf, d), lambda i, f: (f, 0), **wbuf),
                  row(hw), row(hw), row(hw), row(hw), one(LANES), one(LANES),
                  pl.BlockSpec(w_out.shape, lambda i, f: (0, 0))],
        out_specs=row(d),
        out_shape=jax.ShapeDtypeStruct((n, d), F32),
        scratch_shapes=[pltpu.VMEM((tm, d), BF16), pltpu.VMEM((tm, d), F32), pltpu.VMEM((tm, d), F32)],
        compiler_params=_cparams("parallel", "arbitrary"), name="even_tail",
    )(x, norm_w.reshape(1, d).astype(F32), mods, w_gu, w_gu, w_down, o_f, o_b, z, o_d,
      onorm.reshape(1, LANES).astype(F32), subln.reshape(1, LANES).astype(F32), w_out)


def _moe_kernel(x_ref, nw_ref, mods_ref, rw_ref, wgu_ref, wd_ref, *rest, rows, mod0, n_exp, sb, final):
    if final:
        fw_ref, o_ctx_ref, o_lat_ref, h_scr, g_scr, rk_scr, rkt_scr, acc_scr = rest
    else:
        o_ref, h_scr, g_scr, rk_scr, rkt_scr, acc_scr = rest
    i, e = pl.program_id(0), pl.program_id(1)
    tm, d = x_ref.shape
    g = rows.group(i)
    lane = lax.broadcasted_iota(jnp.int32, (1, LANES), 1)

    @pl.when(e == 0)
    def _():
        h = _modulated(x_ref, nw_ref, mods_ref, g, mod0, d)
        h_scr[...] = h.astype(BF16)
        acc_scr[...] = jnp.zeros_like(acc_scr)
        logits = jnp.where(lane < n_exp, _mm(h, rw_ref[...], True), -1e30)
        pe = jnp.exp(logits - jnp.max(logits, axis=-1, keepdims=True))
        probs = pe / jnp.sum(pe, axis=-1, keepdims=True)
        p1 = jnp.max(probs, axis=-1, keepdims=True)
        i1 = jnp.min(jnp.where(probs == p1, lane, LANES), axis=-1, keepdims=True)
        rest = jnp.where(lane == i1, -1.0, probs)
        p2 = jnp.max(rest, axis=-1, keepdims=True)
        i2 = jnp.min(jnp.where(rest == p2, lane, LANES), axis=-1, keepdims=True)
        den = p1 + p2
        gates = jnp.where(lane == i1, p1 / den, 0.0) + jnp.where(lane == i2, p2 / den, 0.0)
        g_scr[...] = gates
        t_r = lax.broadcasted_iota(jnp.int32, (tm, tm), 0)
        t_c = lax.broadcasted_iota(jnp.int32, (tm, tm), 1)
        earlier = jnp.where(t_r > t_c, 1.0, 0.0).astype(BF16)
        sel = gates > 0.0
        rank = jnp.where(sel, _dot(earlier, jnp.where(sel, 1.0, 0.0).astype(BF16)), -1.0)
        rk_scr[...] = rank
        rkt_scr[...] = jnp.transpose(rank)[:rkt_scr.shape[0], :]

    ff = wd_ref.shape[0]
    g_e = jnp.sum(jnp.where(lane == e, g_scr[...], 0.0), axis=-1, keepdims=True)
    rk_col = jnp.sum(jnp.where(lane == e, rk_scr[...], 0.0), axis=-1, keepdims=True)
    rk_row = rkt_scr[pl.ds(e, 1), :]
    n_rows = (jnp.max(rk_row) + 1.0).astype(jnp.int32)
    g_hi = g_e.astype(BF16).astype(F32)
    g2 = jnp.where(lane == 0, g_hi, jnp.where(lane == 1, g_e - g_hi, 0.0)).astype(BF16)
    def expert_rows(slot0, nr):
        base = slot0.astype(F32)
        r_sub = lax.broadcasted_iota(jnp.int32, (nr, 1), 0).astype(F32)
        r_lane = lax.broadcasted_iota(jnp.int32, (1, nr), 1).astype(F32)
        pick = jnp.where(rk_row == r_sub + base, 1.0, 0.0).astype(BF16)
        xg = _dot(pick, h_scr[...]).astype(BF16)
        gr = _dot(pick, g2)
        gate_r = gr[:, 0:1] + gr[:, 1:2]
        gu = _dot(xg, wgu_ref[...])
        act = _silu(gu[:, :ff]) * gu[:, ff:]
        y = _dot(act.astype(BF16), wd_ref[...])
        put = jnp.where(rk_col == r_lane + base, 1.0, 0.0).astype(BF16)
        acc_scr[...] += _dot(put, (y * gate_r).astype(BF16))

    half = sb // 2
    zero = jnp.int32(0)
    one_block = sb + sb // 8

    @pl.when((n_rows > 0) & (n_rows <= one_block))
    def _():
        expert_rows(zero, one_block)

    @pl.when(n_rows > one_block)
    def _():
        rem = n_rows % sb
        n_full = n_rows // sb + jnp.where(rem > half, 1, 0)

        def full_body(jb, carry):
            expert_rows(jb * sb, sb)
            return carry

        lax.fori_loop(0, n_full, full_body, 0)

        @pl.when((rem > 0) & (rem <= half))
        def _():
            expert_rows(n_full * sb, half)

    def result():
        return x_ref[...] + _mod_chunk(mods_ref, g, mod0 + 2, d) * acc_scr[...]

    last = e == n_exp - 1
    if final:
        @pl.when(last & (i < rows.ctx_tiles))
        def _():
            o_ctx_ref[...] = _rmsnorm_rows(result(), fw_ref[...])

        @pl.when(last & (i >= rows.ctx_tiles))
        def _():
            o_lat_ref[...] = _rmsnorm_rows(result(), fw_ref[...])
    else:
        @pl.when(last)
        def _():
            o_ref[...] = result()


def _moe(x, norm_w, mods, router_w, w_gu, w_down, rows, mod0, final_w=None):
    n, d = x.shape
    tm = rows.tm
    n_exp, _, ff2 = w_gu.shape
    rw = jnp.zeros((d, LANES), F32).at[:, :n_exp].set(router_w.astype(F32))
    final = final_w is not None
    args = [x, norm_w.reshape(1, d).astype(F32), mods, rw, w_gu, w_down]
    in_specs = [pl.BlockSpec((tm, d), lambda i, e: (i, 0)),
                pl.BlockSpec((1, d), lambda i, e: (0, 0)),
                pl.BlockSpec(mods.shape, lambda i, e: (0, 0)),
                pl.BlockSpec((d, LANES), lambda i, e: (0, 0)),
                pl.BlockSpec((None, d, ff2), lambda i, e: (e, 0, 0)),
                pl.BlockSpec((None, ff2 // 2, d), lambda i, e: (e, 0, 0))]
    if final:
        args.append(final_w.reshape(1, d).astype(F32))
        in_specs.append(pl.BlockSpec((1, d), lambda i, e: (0, 0)))
        ct = rows.ctx_tiles
        out_specs = [pl.BlockSpec((tm, d), lambda i, e: (jnp.minimum(i, ct - 1), 0), pipeline_mode=pl.Buffered(1)),
                     pl.BlockSpec((tm, d), lambda i, e: (jnp.maximum(i - ct, 0), 0), pipeline_mode=pl.Buffered(1))]
        out_shape = [jax.ShapeDtypeStruct((ct * tm, d), F32), jax.ShapeDtypeStruct((n - ct * tm, d), F32)]
    else:
        out_specs = pl.BlockSpec((tm, d), lambda i, e: (i, 0))
        out_shape = jax.ShapeDtypeStruct((n, d), F32)
    return pl.pallas_call(
        functools.partial(_moe_kernel, rows=rows, mod0=mod0, n_exp=n_exp, sb=MOE_ROW_BLOCK, final=final),
        grid=(n // tm, n_exp), in_specs=in_specs, out_specs=out_specs, out_shape=out_shape,
        scratch_shapes=[pltpu.VMEM((tm, d), BF16), pltpu.VMEM((tm, LANES), F32), pltpu.VMEM((tm, LANES), F32),
                        pltpu.VMEM((SUBLANES * ((n_exp + SUBLANES - 1) // SUBLANES), tm), F32),
                        pltpu.VMEM((tm, d), F32)],
        compiler_params=_cparams("arbitrary", "arbitrary"), name="moe",
    )(*args)


def _final_kernel(x_ref, w_ref, o_ref):
    o_ref[...] = _rmsnorm_rows(x_ref[...], w_ref[...])


def _final_norm(x, w, tm):
    n, d = x.shape
    return pl.pallas_call(
        _final_kernel, grid=(n // tm,),
        in_specs=[pl.BlockSpec((tm, d), lambda i: (i, 0)), pl.BlockSpec((1, d), lambda i: (0, 0))],
        out_specs=pl.BlockSpec((tm, d), lambda i: (i, 0)),
        out_shape=jax.ShapeDtypeStruct((n, d), F32),
        compiler_params=_cparams("parallel"), name="final_norm",
    )(x, w.reshape(1, d).astype(F32))


def kernel(x_prompt, x_sample, state_delta, cache_diff_k, cache_diff_v, cache_mla_ckv, cache_mla_krope, c, c_ctx,
           mod_w, mod_b, norm_mix, norm_ffn, final_norm, ev_w_in, ev_conv_w, ev_a_log, ev_dt_bias, ev_onorm,
           ev_lambda, ev_subln, ev_w_out, ffn_w_gu, ffn_w_down, od_w_in, od_q_norm, od_kv_norm, od_w_uq, od_w_ukv,
           od_w_out, moe_router, moe_w_gu, moe_w_down):
    bp, lp, d = x_prompt.shape
    bs, ls, _ = x_sample.shape
    past = cache_diff_k.shape[3]
    depth = mod_w.shape[0]
    n_p, n_s = bp * lp, bs * ls
    n = n_p + n_s
    assert n_p % ls == 0 and past == lp

    rows = _Rows(n_p, n_s, ls, _tile(math.gcd(n_p, ls), 512))
    rows_m = _Rows(n_p, n_s, ls, _tile(math.gcd(n_p, ls), 1024))
    rg = _tile(math.gcd(lp, ls), 256)
    cos_t, sin_t = _rope_tables(rows, ls)

    x = jnp.concatenate([x_prompt.reshape(n_p, d), x_sample.reshape(n_s, d)], axis=0)
    n_grp = 1 + bs
    cc = jnp.zeros((2 * SUBLANES * ((n_grp + 15) // 16), d), F32).at[0].set(c_ctx).at[1:n_grp].set(c)
    mods_all = _mods(cc, mod_w, mod_b)

    hq = H_A * DK_A
    hb2 = H_B * 2 * DQK_B
    sm0 = 4 * hq
    qb0 = sm0 + 4 * H_A
    q_scale = (DQK_B ** -0.5) * LOG2E
    mla_scale = ((NOPE_C + ROPE_C) ** -0.5) * LOG2E
    sd_new, dk_new, dv_new, ckv_new, kr_new = [], [], [], [], []

    for layer in range(depth):
        j = layer // 2
        mods = mods_all[layer]
        if layer % 2 == 0:
            lam_init = 0.8 - 0.6 * math.exp(-0.3 * layer)
            w = ev_w_in[j]
            sm = w[:, sm0:qb0].reshape(d, 4, H_A).transpose(0, 2, 1).reshape(d, 4 * H_A)
            w_in = jnp.concatenate([w[:, :sm0], sm, jnp.zeros((d, LANES - 4 * H_A), w.dtype), w[:, qb0:]],
                                   axis=1).astype(BF16)
            qkvn, z_gate, gcol, grow, att, kv_f = _even_proj(
                x, norm_mix[layer], mods, w_in, ev_conv_w[j], ev_a_log[j], ev_dt_bias[j], (cos_t, sin_t), rows,
                3 * hq, hq, hb2, q_scale, lp, ls, n_p)
            o_f, o_b, s_p = _delta_stage(qkvn, gcol, grow, jnp.zeros((bp, 2, H_A, DK_A, DV_A), F32), bp, lp, 0, rg,
                                         precise=True)
            o_f, o_b, _ = _delta_stage(qkvn, gcol, grow, state_delta[:, j].astype(F32), bs, ls, n_p, rg,
                                       into=(o_f, o_b))
            sd_new.append(s_p)
            lam_p = ev_lambda[j].astype(F32)
            nb = hb2 // LANES
            qp, kp, vp = [(att, lambda h: h)], [(att, lambda h: nb + h)], (att, lambda h: 2 * nb + h)
            o_d = _flash(qp, kp, vp, None, lam_p, n_seq=bp, n_heads=H_B, lq=lp, lk=lp, q_row0=0, k_row0=0,
                         tq=_tile(lp, 256), tk=_tile(lp, 512), diff=True, lam_init=lam_init, out_cols=hb2,
                         heads_per_step=H_B)
            ck = cache_diff_k[:, j].transpose(0, 2, 1, 3).reshape(bs * past, H_B * 2 * DQK_B).astype(BF16)
            cv = cache_diff_v[:, j].transpose(0, 2, 1, 3).reshape(bs * past, H_B * DV_B).astype(BF16)
            cache = ([(ck, lambda h: h)], (cv, lambda h: h), past)
            o_d = _flash(qp, kp, vp, cache, lam_p, n_seq=bs, n_heads=H_B, lq=ls, lk=ls, q_row0=n_p, k_row0=n_p,
                         tq=_tile(ls, 512), tk=_tile(ls, 2048), diff=True, lam_init=lam_init, out_cols=hb2, into=o_d,
                         heads_per_step=FLASH_HEADS_PER_STEP)
            dk_new.append(kv_f[:n_p, :hb2].reshape(bp, lp, H_B, 2 * DQK_B).transpose(0, 2, 1, 3))
            dv_new.append(kv_f[:n_p, hb2:].reshape(bp, lp, H_B, DV_B).transpose(0, 2, 1, 3))
            ff = ffn_w_down.shape[1]
            x = _even_tail(x, o_f, o_b, z_gate, o_d, ev_onorm[j], ev_subln[j], ev_w_out[j].astype(BF16),
                           norm_ffn[layer], mods, ffn_w_gu[j].astype(BF16), ffn_w_down[j].astype(BF16), rows, 2, 3,
                           ff, lam_init)
        else:
            n_in = Q_LORA + KV_LORA + ROPE_C
            pad = (-n_in) % LANES
            w_in = jnp.concatenate([od_w_in[j], jnp.zeros((d, pad), od_w_in.dtype)], axis=1).astype(BF16)
            wq = od_w_uq[j].reshape(Q_LORA, H_C, NOPE_C + ROPE_C)
            wq_rope = jnp.concatenate([wq[:, :, NOPE_C:], jnp.zeros((Q_LORA, H_C, LANES - ROPE_C), wq.dtype)], axis=2)
            wq2 = jnp.concatenate([wq[:, :, :NOPE_C].reshape(Q_LORA, H_C * NOPE_C),
                                   wq_rope.reshape(Q_LORA, H_C * LANES)], axis=1).astype(BF16)
            q_all, kv_tok, kr_tok, ckv_n, kr_f = _odd_proj(
                x, norm_mix[layer], mods, w_in, od_q_norm[j], od_kv_norm[j], wq2, od_w_ukv[j].astype(BF16),
                (cos_t, sin_t), rows, mla_scale, H_C * NOPE_C)
            ckv_new.append(ckv_n[:n_p].reshape(bp, lp, KV_LORA))
            kr_new.append(kr_f[:n_p, :ROPE_C].reshape(bp, lp, ROPE_C))
            rows_k = _Rows(bs * past, 0, past, _tile(past, 256))
            kv_c = _norm_matmul(cache_mla_ckv[:, j].reshape(bs * past, KV_LORA), od_w_ukv[j].astype(BF16), rows_k,
                                tn=512, out_dtype=BF16)
            kr_c = jnp.concatenate([cache_mla_krope[:, j].reshape(bs * past, ROPE_C),
                                    jnp.zeros((bs * past, LANES - ROPE_C), F32)], axis=1).astype(BF16)
            qp = [(q_all, lambda h: h), (q_all, lambda h: H_C + h)]
            kp = [(kv_tok, lambda h: 2 * h), (kr_tok, lambda h: 0)]
            vp = (kv_tok, lambda h: 2 * h + 1)
            o = _flash(qp, kp, vp, None, None, n_seq=bp, n_heads=H_C, lq=lp, lk=lp, q_row0=0, k_row0=0,
                       tq=_tile(lp, 256), tk=_tile(lp, 512), diff=False, lam_init=0.0, out_cols=H_C * V_C,
                       heads_per_step=H_C)
            cache = ([(kv_c, lambda h: 2 * h), (kr_c, lambda h: 0)], (kv_c, lambda h: 2 * h + 1), past)
            o = _flash(qp, kp, vp, cache, None, n_seq=bs, n_heads=H_C, lq=ls, lk=ls, q_row0=n_p, k_row0=n_p,
                       tq=_tile(ls, 1024), tk=_tile(ls, 2048), diff=False, lam_init=0.0, out_cols=H_C * V_C, into=o,
                       heads_per_step=FLASH_HEADS_PER_STEP)
            x = _matmul_residual(x, o, od_w_out[j].astype(BF16), mods, rows, 2)
            x = _moe(x, norm_ffn[layer], mods, moe_router[j], moe_w_gu[j].astype(BF16), moe_w_down[j].astype(BF16),
                     rows_m, 3, final_w=final_norm if layer == depth - 1 else None)

    if depth % 2 == 0:
        y_p, y_s = x
    else:
        y = _final_norm(x, final_norm, rows.tm)
        y_p, y_s = y[:n_p], y[n_p:]
    return (y_p.reshape(bp, lp, d), y_s.reshape(bs, ls, d), jnp.stack(sd_new, axis=1),
            jnp.stack(dk_new, axis=1), jnp.stack(dv_new, axis=1), jnp.stack(ckv_new, axis=1),
            jnp.stack(kr_new, axis=1))
```

```python
import functools
import math

import jax
import jax.numpy as jnp
from jax import lax
from jax.experimental import pallas as pl
from jax.experimental.pallas import tpu as pltpu

F32 = jnp.float32
BF16 = jnp.bfloat16

EPS = 1e-6
LOG2E = 1.4426950408889634
GRID_W = 64
ROPE_BASE = 10000.0
H_A, DK_A, DV_A = 4, 128, 128
CONV_K = 3
CHUNK = 64
H_B, DQK_B, DV_B = 4, 64, 128
H_C, NOPE_C, ROPE_C, V_C = 8, 128, 64, 128
Q_LORA, KV_LORA = 384, 256
LANES = 128
SUBLANES = 8
VMEM_LIMIT_BYTES = 56 * 1024 * 1024


def _cparams(*sem):
    return pltpu.CompilerParams(dimension_semantics=sem, vmem_limit_bytes=VMEM_LIMIT_BYTES)


def _dot(a, b):
    return jnp.dot(a, b, preferred_element_type=F32)


def _dot_nt(a, b):
    return lax.dot_general(a, b, (((1,), (1,)), ((), ())), preferred_element_type=F32)


def _dot_tn(a, b):
    return lax.dot_general(a, b, (((0,), (0,)), ((), ())), preferred_element_type=F32)


def _split3(x):
    hi = x.astype(BF16)
    r = x - hi.astype(F32)
    mid = r.astype(BF16)
    lo = (r - mid.astype(F32)).astype(BF16)
    return hi, mid, lo


def _dot_f32(a, b):
    ah, am, al = _split3(a)
    bh, bm, bl = _split3(b)
    return (_dot(ah, bh) + (_dot(ah, bm) + _dot(am, bh))
            + (_dot(am, bm) + _dot(ah, bl) + _dot(al, bh)))


def _silu(x):
    return x * jax.nn.sigmoid(x)


def _tile(n, pref, mult=SUBLANES):
    t = min(n, pref)
    while t > mult and (n % t or t % mult):
        t -= mult
    assert n % t == 0, (n, pref)
    return t


class _Rows:
    def __init__(self, n_ctx, n_lat, lat_len, tm):
        assert n_ctx % tm == 0 and lat_len % tm == 0
        self.tm = tm
        self.n = n_ctx + n_lat
        self.ctx_tiles = n_ctx // tm
        self.seq_tiles = lat_len // tm

    def group(self, i):
        return jnp.where(i < self.ctx_tiles, 0, 1 + jnp.maximum(i - self.ctx_tiles, 0) // self.seq_tiles)

    def pos_block(self, i):
        return jnp.where(i < self.ctx_tiles, 0, 1 + jnp.maximum(i - self.ctx_tiles, 0) % self.seq_tiles)


def _mod_kernel(c_ref, w_ref, b_ref, o_ref):
    o_ref[...] = _dot_f32(_silu(c_ref[...]), w_ref[...]) + b_ref[...]


def _mods(cc, mod_w, mod_b):
    depth, d, n6 = mod_w.shape
    tn = _tile(n6, 512, LANES)
    return pl.pallas_call(
        _mod_kernel,
        grid=(depth, n6 // tn),
        in_specs=[pl.BlockSpec(cc.shape, lambda l, j: (0, 0)),
                  pl.BlockSpec((None, d, tn), lambda l, j: (l, 0, j)),
                  pl.BlockSpec((None, 1, tn), lambda l, j: (l, 0, j))],
        out_specs=pl.BlockSpec((None, cc.shape[0], tn), lambda l, j: (l, 0, j)),
        out_shape=jax.ShapeDtypeStruct((depth, cc.shape[0], n6), F32),
        compiler_params=_cparams("parallel", "parallel"),
        name="mods",
    )(cc, mod_w, mod_b.reshape(depth, 1, n6))


def _mod_chunk(mods_ref, g, idx, d):
    return mods_ref[pl.ds(g, 1), idx * d:(idx + 1) * d]


def _rmsnorm_rows(x, w):
    ms = jnp.mean(x * x, axis=-1, keepdims=True)
    return x * lax.rsqrt(ms + EPS) * w


def _rope(x, cos, sin_signed):
    lane = lax.broadcasted_iota(jnp.int32, (1, LANES), 1)
    low = (lane % 32) < 16
    outs = []
    for c in range(x.shape[1] // LANES):
        xc = x[:, c * LANES:(c + 1) * LANES]
        fwd = pltpu.roll(xc, LANES - 16, axis=1)
        bwd = pltpu.roll(xc, 16, axis=1)
        outs.append(xc * cos + jnp.where(low, fwd, bwd) * sin_signed)
    return outs[0] if len(outs) == 1 else jnp.concatenate(outs, axis=1)


def _rope_tables(rows, lat_len):
    pos = jnp.arange(lat_len)
    r = (pos // GRID_W).astype(F32)
    c = (pos % GRID_W).astype(F32)
    inv = ROPE_BASE ** (-jnp.arange(0, 32, 2, dtype=F32) / 32)
    ang_r = r[:, None] * inv[None, :]
    ang_c = c[:, None] * inv[None, :]
    ang = jnp.concatenate([ang_r, ang_r, ang_c, ang_c] * 2, axis=-1)
    sign = jnp.where((jnp.arange(LANES) % 32) < 16, -1.0, 1.0).astype(F32)
    cos = jnp.concatenate([jnp.ones((rows.tm, LANES), F32), jnp.cos(ang)], axis=0)
    sin = jnp.concatenate([jnp.zeros((rows.tm, LANES), F32), jnp.sin(ang) * sign[None, :]], axis=0)
    return cos, sin


def _mm_kernel(x_ref, w_ref, o_ref):
    o_ref[...] = _dot(x_ref[...].astype(BF16), w_ref[...]).astype(o_ref.dtype)


def _matmul(x, w, tm, out_dtype):
    n, k = x.shape
    return pl.pallas_call(
        _mm_kernel, grid=(n // tm,),
        in_specs=[pl.BlockSpec((tm, k), lambda i: (i, 0)), pl.BlockSpec(w.shape, lambda i: (0, 0))],
        out_specs=pl.BlockSpec((tm, w.shape[1]), lambda i: (i, 0)),
        out_shape=jax.ShapeDtypeStruct((n, w.shape[1]), out_dtype),
        compiler_params=_cparams("parallel"), name="matmul",
    )(x, w)


PROJ_CHUNK = 512


def _pos_in_seq(r, length):
    return r & (length - 1) if length & (length - 1) == 0 else lax.rem(r, length)


def _even_proj_kernel(x_ref, xp_ref, xn_ref, nw_ref, mods_ref, w_ref, cw_ref, alog_ref, dt_ref, cos_ref, sin_ref,
                      qkv_ref, z_ref, gc_ref, gr_ref, att_ref, kv_ref,
                      *, rows, n_conv, n_z, hb2, q_scale, ctx_len, lat_len, n_ctx_rows):
    i = pl.program_id(0)
    tm, d = x_ref.shape
    g = rows.group(i)

    def modulated(ref):
        v = _rmsnorm_rows(ref[...], nw_ref[...])
        return (v * (1.0 + _mod_chunk(mods_ref, g, 1, d)) + _mod_chunk(mods_ref, g, 0, d)).astype(BF16)

    h = modulated(x_ref)
    halo = xp_ref.shape[0]
    h_ext = jnp.concatenate([h, modulated(xp_ref), modulated(xn_ref)], axis=0)
    p_ext = jnp.concatenate([_dot(h_ext, w_ref[:, c0:min(c0 + PROJ_CHUNK, n_conv)])
                             for c0 in range(0, n_conv, PROJ_CHUNK)], axis=1)
    p = p_ext[:tm]
    p_before = p_ext[tm + halo - 1:tm + halo, :]
    p_after = p_ext[tm + halo:tm + halo + 1, :]
    row = lax.broadcasted_iota(jnp.int32, (tm, 1), 0)
    r_glob = row + i * tm
    in_ctx = i < rows.ctx_tiles
    pos = jnp.where(in_ctx, _pos_in_seq(r_glob, ctx_len), _pos_in_seq(r_glob - n_ctx_rows, lat_len))
    seq_last = jnp.where(in_ctx, ctx_len - 1, lat_len - 1)
    x_prev = jnp.where(pos == 0, 0.0, jnp.where(row == 0, p_before, pltpu.roll(p, 1, axis=0)))
    x_next = jnp.where(pos == seq_last, 0.0, jnp.where(row == tm - 1, p_after, pltpu.roll(p, tm - 1, axis=0)))
    cw = cw_ref[...]
    y = _silu(x_prev * cw[0:1, :] + p * cw[1:2, :] + x_next * cw[2:3, :])
    n_qk = 2 * H_A
    for hh in range(n_conv // LANES):
        yh = y[:, hh * LANES:(hh + 1) * LANES]
        if hh < n_qk:
            yh = yh * lax.rsqrt(jnp.sum(yh * yh, axis=-1, keepdims=True) + EPS)
            if hh < n_qk // 2:
                yh = yh * (DK_A ** -0.5)
        qkv_ref[:, hh * LANES:(hh + 1) * LANES] = yh
    z_ref[...] = _dot(h, w_ref[:, n_conv:n_conv + n_z])
    s = _dot(h, w_ref[:, n_conv + n_z:n_conv + n_z + LANES])
    lane = lax.broadcasted_iota(jnp.int32, (1, LANES), 1)
    zg = s + dt_ref[...]
    softplus = jnp.maximum(zg, 0.0) + jnp.log(1.0 + jnp.exp(-jnp.abs(zg)))
    gates = jnp.where((lane % 4) < 2, -jnp.exp(alog_ref[...]) * softplus, jax.nn.sigmoid(s))
    gc_ref[...] = gates
    gr_ref[...] = jnp.transpose(gates)[:gr_ref.shape[0], :]
    n_main = n_conv + n_z + LANES
    cos, sin = cos_ref[...], sin_ref[...]
    for part in range(3):
        for c0 in range(0, hb2, PROJ_CHUNK):
            c1 = min(c0 + PROJ_CHUNK, hb2)
            y = _dot(h, w_ref[:, n_main + part * hb2 + c0:n_main + part * hb2 + c1])
            if part < 2:
                y = _rope(y, cos, sin)
            if part == 0:
                y = y * q_scale
            else:
                kv_ref[:, (part - 1) * hb2 + c0:(part - 1) * hb2 + c1] = y
            att_ref[:, part * hb2 + c0:part * hb2 + c1] = y.astype(BF16)


def _even_proj(x, norm_w, mods, w, conv_w, a_log, dt_bias, rope, rows, n_conv, n_z, hb2, q_scale, ctx_len,
               lat_len, n_ctx_rows):
    n, d = x.shape
    tm = rows.tm
    assert conv_w.shape[0] == CONV_K
    halo = 2 * SUBLANES
    hb = tm // halo
    alog_row = jnp.zeros((1, LANES), F32).at[0, :4 * H_A].set(
        jnp.stack([a_log[0], a_log[1], a_log[0], a_log[1]], axis=-1).reshape(-1).astype(F32))
    dt_row = jnp.zeros((1, LANES), F32).at[0, :4 * H_A].set(
        jnp.stack([dt_bias[0], dt_bias[1], jnp.zeros_like(dt_bias[0]), jnp.zeros_like(dt_bias[0])],
                  axis=-1).reshape(-1).astype(F32))
    kern = functools.partial(_even_proj_kernel, rows=rows, n_conv=n_conv, n_z=n_z, hb2=hb2, q_scale=q_scale,
                             ctx_len=ctx_len, lat_len=lat_len, n_ctx_rows=n_ctx_rows)
    tab = pl.BlockSpec((tm, LANES), lambda i: (rows.pos_block(i), 0))
    full = lambda a: pl.BlockSpec(a.shape, lambda i: (0, 0))
    nw = norm_w.reshape(1, d).astype(F32)
    cw = conv_w.astype(F32)
    outs = [(n_conv, F32), (n_z, F32), (LANES, F32), None, (3 * hb2, BF16), (2 * hb2, F32)]
    out_specs = [pl.BlockSpec((tm, c[0]), lambda i: (i, 0)) if c else pl.BlockSpec((4 * H_A, tm), lambda i: (0, i))
                 for c in outs]
    out_shape = [jax.ShapeDtypeStruct((n, c[0]), c[1]) if c else jax.ShapeDtypeStruct((4 * H_A, n), F32)
                 for c in outs]
    return pl.pallas_call(
        kern, grid=(n // tm,),
        in_specs=[pl.BlockSpec((tm, d), lambda i: (i, 0)),
                  pl.BlockSpec((halo, d), lambda i: (jnp.maximum(i * hb - 1, 0), 0)),
                  pl.BlockSpec((halo, d), lambda i: (jnp.minimum((i + 1) * hb, n // halo - 1), 0)),
                  full(nw), full(mods), full(w), full(cw), full(alog_row), full(dt_row), tab, tab],
        out_specs=out_specs, out_shape=out_shape,
        compiler_params=_cparams("parallel"), name="even_proj",
    )(x, x, x, nw, mods, w, cw, alog_row, dt_row, *rope)


def _odd_proj_kernel(x_ref, nw_ref, mods_ref, w_ref, qn_ref, kvn_ref, wq_ref, wkv_ref, cos_ref, sin_ref,
                     q_ref, kv_ref, kr_ref, ckv_ref, krf_ref, *, rows, scale, n_nope):
    i = pl.program_id(0)
    d = x_ref.shape[1]
    h = _modulated(x_ref, nw_ref, mods_ref, rows.group(i), 0, d).astype(BF16)
    p = _dot(h, w_ref[...])
    cos, sin = cos_ref[...], sin_ref[...]
    cq = _rmsnorm_rows(p[:, :Q_LORA], qn_ref[...]).astype(BF16)
    for c0 in range(0, wq_ref.shape[1], PROJ_CHUNK):
        y = _dot(cq, wq_ref[:, c0:c0 + PROJ_CHUNK])
        if c0 >= n_nope:
            y = _rope(y, cos, sin)
        q_ref[:, c0:c0 + PROJ_CHUNK] = (y * scale).astype(BF16)
    ckv = _rmsnorm_rows(p[:, Q_LORA:Q_LORA + KV_LORA], kvn_ref[...])
    ckv_ref[...] = ckv
    ckv16 = ckv.astype(BF16)
    for c0 in range(0, wkv_ref.shape[1], PROJ_CHUNK):
        kv_ref[:, c0:c0 + PROJ_CHUNK] = _dot(ckv16, wkv_ref[:, c0:c0 + PROJ_CHUNK]).astype(BF16)
    kr = _rope(p[:, Q_LORA + KV_LORA:], cos, sin)
    krf_ref[...] = kr
    kr_ref[...] = kr.astype(BF16)


def _odd_proj(x, norm_w, mods, w_in, q_norm, kv_norm, wq, wkv, rope, rows, scale, n_nope):
    n, d = x.shape
    tm = rows.tm
    tab = pl.BlockSpec((tm, LANES), lambda i: (rows.pos_block(i), 0))
    full = lambda a: pl.BlockSpec(a.shape, lambda i: (0, 0))
    qn = q_norm.reshape(1, -1).astype(F32)
    kvn = kv_norm.reshape(1, -1).astype(F32)
    nw = norm_w.reshape(1, d).astype(F32)
    outs = [(wq.shape[1], BF16), (wkv.shape[1], BF16), (LANES, BF16), (KV_LORA, F32), (LANES, F32)]
    return pl.pallas_call(
        functools.partial(_odd_proj_kernel, rows=rows, scale=scale, n_nope=n_nope), grid=(n // tm,),
        in_specs=[pl.BlockSpec((tm, d), lambda i: (i, 0)), full(nw), full(mods), full(w_in), full(qn), full(kvn),
                  full(wq), full(wkv), tab, tab],
        out_specs=[pl.BlockSpec((tm, c), lambda i: (i, 0)) for c, _ in outs],
        out_shape=[jax.ShapeDtypeStruct((n, c), dt) for c, dt in outs],
        compiler_params=_cparams("parallel"), name="odd_proj",
    )(x, nw, mods, w_in, qn, kvn, wq, wkv, *rope)


def _split2(x):
    hi = x.astype(BF16)
    return hi, (x - hi.astype(F32)).astype(BF16)


def _mm(a, b, precise, dot=_dot):
    if not precise:
        return dot(a.astype(BF16), b.astype(BF16))
    ah, al = _split2(a)
    bh, bl = _split2(b)
    return dot(ah, bh) + (dot(ah, bl) + dot(al, bh))


SOLVE_BLOCK = SUBLANES
MOE_ROW_BLOCK = 256
FLASH_HEADS_PER_STEP = 2
DELTA_HEADS_PER_STEP = 4


def _tri_solve_many(ms, rs, revs):
    c, width = rs[0].shape
    blk = SOLVE_BLOCK
    nb, gpb, ng = c // blk, blk // SUBLANES, c // SUBLANES
    col = lax.broadcasted_iota(jnp.int32, (blk, c), 1)
    mgs = [[m[SUBLANES * g:SUBLANES * (g + 1), :] for g in range(ng)] for m in ms]
    xgs = [[r[SUBLANES * g:SUBLANES * (g + 1), :] for g in range(ng)] for r in rs]
    zero_blk = jnp.zeros((blk, width), BF16)
    fin_h = [[zero_blk] * nb for _ in ms]
    fin_l = [[zero_blk] * nb for _ in ms]
    for bi in range(nb):
        blocks = [nb - 1 - bi if rev else bi for rev in revs]
        if bi > 0:
            for s, (rev, b) in enumerate(zip(revs, blocks)):
                done = (col >= (b + 1) * blk) if rev else (col < b * blk)
                lh, ll = _split2(jnp.where(done, ms[s][b * blk:(b + 1) * blk, :], 0.0))
                xh = jnp.concatenate(fin_h[s], axis=0)
                xl = jnp.concatenate(fin_l[s], axis=0)
                upd = _dot(lh, xh) + (_dot(lh, xl) + _dot(ll, xh))
                for gg in range(gpb):
                    g = b * gpb + gg
                    xgs[s][g] = xgs[s][g] - upd[SUBLANES * gg:SUBLANES * (gg + 1), :]
        for t in range(blk - 1):
            for s, (rev, b) in enumerate(zip(revs, blocks)):
                j = b * blk + (blk - 1 - t if rev else t)
                xj = xgs[s][j // SUBLANES][j % SUBLANES:j % SUBLANES + 1, :]
                groups = range(b * gpb, (j - 1) // SUBLANES + 1) if rev else range((j + 1) // SUBLANES, (b + 1) * gpb)
                for g in groups:
                    xgs[s][g] = xgs[s][g] - mgs[s][g][:, j:j + 1] * xj
        if bi < nb - 1:
            for s, b in enumerate(blocks):
                fin_h[s][b], fin_l[s][b] = _split2(jnp.concatenate(xgs[s][b * gpb:(b + 1) * gpb], axis=0))
    return [jnp.concatenate(xg, axis=0) for xg in xgs]


def _delta_kernel(qf, kf, vf, gcf, grf, qb, kb, vb, gcb, grb, s0_ref, *rest, chunk, hpb, n_alias, precise):
    of_ref, ob_ref, so_ref, s_scr = rest[n_alias:]
    h0 = pl.program_id(1) * hpb
    i = pl.program_id(2)
    n_i = pl.num_programs(2)

    @pl.when(i == 0)
    def _():
        s_scr[...] = s0_ref[...]

    rg = qf.shape[0]
    nc = rg // chunk
    rowi = lax.broadcasted_iota(jnp.int32, (rg, rg), 0)
    coli = lax.broadcasted_iota(jnp.int32, (rg, rg), 1)
    same = (rowi // chunk) == (coli // chunk)
    lane = lax.broadcasted_iota(jnp.int32, (1, LANES), 1)
    r_c = lax.broadcasted_iota(jnp.int32, (chunk, chunk), 0)
    c_c = lax.broadcasted_iota(jnp.int32, (chunk, chunk), 1)

    prep = []
    for d, (q_ref, k_ref, v_ref, gc_ref, gr_ref) in enumerate(((qf, kf, vf, gcf, grf), (qb, kb, vb, gcb, grb))):
        rev = d == 1
        incl_big = same & ((rowi <= coli) if rev else (rowi >= coli))
        lm = jnp.where(incl_big, 1.0, 0.0).astype(BF16)
        gcols = gc_ref[...]
        grows = gr_ref[...]
        gh, gm, gl_ = _split3(gcols)
        cum_c = _dot(lm, gh) + _dot(lm, gm) + _dot(lm, gl_)
        th, tm_, tl = _split3(grows)
        cum_r = _dot_nt(th, lm) + _dot_nt(tm_, lm) + _dot_nt(tl, lm)

        def col(a, ln):
            return jnp.sum(jnp.where(lane == ln, a, 0.0), axis=1, keepdims=True)

        sub = lax.broadcasted_iota(jnp.int32, (cum_r.shape[0], 1), 0)
        incl = (r_c <= c_c) if rev else (r_c >= c_c)
        strict = (r_c < c_c) if rev else (r_c > c_c)
        for hh in range(hpb):
            h = h0 + hh
            hs = slice(hh * LANES, (hh + 1) * LANES)
            gcum = col(cum_c, 4 * h + d)
            beta = col(gcols, 4 * h + 2 + d)
            grow = jnp.sum(jnp.where(sub == 4 * h + d, cum_r, 0.0), axis=0, keepdims=True)
            q = q_ref[:, hs]
            k = k_ref[:, hs]
            v = v_ref[:, hs]
            kbeta = k * beta
            eg = jnp.exp(gcum)
            rhs = jnp.concatenate([v * beta, kbeta * eg], axis=1)
            ms, rs, a_in = [], [], []
            for c in range(nc):
                sl = slice(c * chunk, (c + 1) * chunk)
                e = jnp.exp(jnp.where(incl, gcum[sl] - grow[:, sl], 0.0))
                kk = _mm(kbeta[sl], k[sl], precise, _dot_nt)
                qk = _mm(q[sl], k[sl], precise, _dot_nt)
                a_in.append(jnp.where(incl, qk * e, 0.0))
                ms.append(jnp.where(strict, kk * e, 0.0))
                rs.append(rhs[sl])
            prep.append((rev, hh, ms, rs, a_in, q * eg, k, gcum))

    all_terms = []
    half = len(prep) // 2
    for wave in (prep[:half], prep[half:]) if half else (prep,):
        sols = _tri_solve_many([m for p in wave for m in p[2]], [r for p in wave for r in p[3]],
                               [p[0] for p in wave for _ in range(nc)])
        for n_u, (rev, hh, _, _, a_in, qg, k, gcum) in enumerate(wave):
            xs = sols[n_u * nc:(n_u + 1) * nc]
            terms = []
            for c in range(nc):
                sl = slice(c * chunk, (c + 1) * chunk)
                r_last = c * chunk if rev else (c + 1) * chunk - 1
                g_last = gcum[r_last:r_last + 1, :]
                k_dec = k[sl] * jnp.exp(g_last - gcum[sl])
                a_uw = _mm(a_in[c], xs[c], precise)
                kd_uw = _mm(k_dec, xs[c], precise, _dot_tn)
                lhs = jnp.concatenate([qg[sl] - a_uw[:, DV_A:], kd_uw[:, DV_A:]], axis=0)
                terms.append((lhs, a_uw[:, :DV_A], kd_uw[:, :DV_A], jnp.exp(g_last)))
            all_terms.append(terms)

    states = [s_scr[1 if p[0] else 0, p[1]] for p in prep]
    outs = [[None] * nc for _ in prep]
    for t in range(nc):
        for n_u, p in enumerate(prep):
            c = nc - 1 - t if p[0] else t
            lhs, o_0, q_0, decay = all_terms[n_u][c]
            prod = _mm(lhs, states[n_u], precise)
            outs[n_u][c] = o_0 + prod[:chunk]
            states[n_u] = states[n_u] * decay - prod[chunk:] + q_0
    for n_u, (rev, hh) in enumerate((p[0], p[1]) for p in prep):
        s_scr[1 if rev else 0, hh] = states[n_u]
        (ob_ref if rev else of_ref)[:, hh * LANES:(hh + 1) * LANES] = jnp.concatenate(outs[n_u], axis=0)

    @pl.when(i == n_i - 1)
    def _():
        so_ref[...] = s_scr[...]


def _delta_stage(qkvn, gcol, grow, s0, n_seq, seq_len, row0, rg, into=None, precise=False):
    ng = seq_len // rg
    b0 = row0 // rg

    def fwd_blk(b, i):
        return b0 + b * ng + i

    def bwd_blk(b, i):
        return b0 + b * ng + (ng - 1 - i)

    hpb = DELTA_HEADS_PER_STEP
    nhp = H_A // hpb
    hw = hpb * LANES

    def specs(blk):
        return [pl.BlockSpec((rg, hw), lambda b, h, i: (blk(b, i), h)),
                pl.BlockSpec((rg, hw), lambda b, h, i: (blk(b, i), nhp + h)),
                pl.BlockSpec((rg, hw), lambda b, h, i: (blk(b, i), 2 * nhp + h)),
                pl.BlockSpec((rg, LANES), lambda b, h, i: (blk(b, i), 0)),
                pl.BlockSpec((4 * H_A, rg), lambda b, h, i: (0, blk(b, i)))]

    st_spec = pl.BlockSpec((None, 2, hpb, DK_A, DV_A), lambda b, h, i: (b, 0, h, 0, 0))
    args = [qkvn, qkvn, qkvn, gcol, grow, qkvn, qkvn, qkvn, gcol, grow, s0]
    in_specs = specs(fwd_blk) + specs(bwd_blk) + [st_spec]
    aliases = {}
    if into is not None:
        for k_out, arr in enumerate(into):
            aliases[len(args)] = k_out
            args.append(arr)
            in_specs.append(pl.BlockSpec(memory_space=pl.ANY))
    n = qkvn.shape[0]
    return pl.pallas_call(
        functools.partial(_delta_kernel, chunk=CHUNK, hpb=hpb, n_alias=len(aliases), precise=precise),
        grid=(n_seq, nhp, ng),
        in_specs=in_specs,
        out_specs=[pl.BlockSpec((rg, hw), lambda b, h, i: (fwd_blk(b, i), h)),
                   pl.BlockSpec((rg, hw), lambda b, h, i: (bwd_blk(b, i), h)),
                   st_spec],
        out_shape=[jax.ShapeDtypeStruct((n, H_A * DV_A), F32),
                   jax.ShapeDtypeStruct((n, H_A * DV_A), F32),
                   jax.ShapeDtypeStruct(s0.shape, F32)],
        input_output_aliases=aliases,
        scratch_shapes=[pltpu.VMEM((2, hpb, DK_A, DV_A), F32)],
        compiler_params=_cparams("parallel", "parallel", "arbitrary"), name="gated_delta",
    )(*args)


def _flash_kernel(*refs, n_parts, has_cache, diff, tk, lam_init, aliased, head_lanes):
    it = iter(refs)
    q_refs = [next(it) for _ in range(n_parts)]
    k_refs = [next(it) for _ in range(n_parts)]
    v_ref = next(it)
    kc_refs = [next(it) for _ in range(n_parts)] if has_cache else []
    vc_ref = next(it) if has_cache else None
    lam_ref = next(it) if diff else None
    if aliased:
        next(it)
    o_ref = next(it)
    tq = q_refs[0].shape[0]
    lk = k_refs[0].shape[0]
    if diff:
        lp = lam_ref[...]
        lam = (jnp.exp(jnp.sum(lp[0:1] * lp[1:2], axis=-1, keepdims=True))
               - jnp.exp(jnp.sum(lp[2:3] * lp[3:4], axis=-1, keepdims=True)) + lam_init)
        lane = lax.broadcasted_iota(jnp.int32, (1, LANES), 1)

    def cols(ref, off, rows=None):
        return ref[:, off:off + LANES] if rows is None else ref[rows, off:off + LANES]

    def block(carry, q_cat, ks, vv):
        m, l, acc = carry
        s = _dot_nt(q_cat, ks[0] if n_parts == 1 else jnp.concatenate(ks, axis=1))
        m_new = jnp.maximum(m, jnp.max(s, axis=-1, keepdims=True))
        alpha = jnp.exp2(m - m_new)
        p_ = jnp.exp2(s - m_new)
        l = alpha * l + jnp.sum(p_, axis=-1, keepdims=True)
        acc = alpha * acc + _dot(p_.astype(BF16), vv)
        return m_new, l, acc

    q_cats, carries = [], []
    for q_offs, *_ in head_lanes:
        qs = [cols(r, off) for r, off in zip(q_refs, q_offs)]
        if diff:
            q = qs[0]
            qs = [jnp.concatenate([jnp.where(lane < DQK_B, q, 0), jnp.where(lane >= DQK_B, q, 0)], axis=0)]
        nrow = qs[0].shape[0]
        q_cats.append(qs[0] if n_parts == 1 else jnp.concatenate(qs, axis=1))
        carries.append((jnp.full((nrow, 1), -1e30, F32), jnp.zeros((nrow, 1), F32), jnp.zeros((nrow, LANES), F32)))

    for t in range(lk // tk):
        rows = slice(t * tk, (t + 1) * tk)
        for n_h, (_, k_offs, v_off, _, _, _) in enumerate(head_lanes):
            carries[n_h] = block(carries[n_h], q_cats[n_h], [cols(r, off, rows) for r, off in zip(k_refs, k_offs)],
                                 cols(v_ref, v_off, rows))
    for n_h, (_, _, _, o_off, kc_offs, vc_off) in enumerate(head_lanes):
        carry = carries[n_h]
        if has_cache:
            carry = block(carry, q_cats[n_h], [cols(r, off) for r, off in zip(kc_refs, kc_offs)], cols(vc_ref, vc_off))
        _, l, acc = carry
        o = acc / l
        if diff:
            o = o[:tq] - lam * o[tq:]
        o_ref[:, o_off:o_off + LANES] = o.astype(o_ref.dtype)


def _flash(q_parts, k_parts, v_part, cache, lam, *, n_seq, n_heads, lq, lk, q_row0, k_row0, tq, tk, diff,
           lam_init, out_cols, into=None, heads_per_step=1):
    nqt = lq // tq
    qb0, kb0 = q_row0 // tq, k_row0 // lk
    hps = heads_per_step
    assert n_heads % hps == 0
    args, in_specs = [], []

    def add(arr, cf, n_rows, row_fn):
        span = 1
        while any(cf(g * hps) // span != cf(g * hps + hps - 1) // span for g in range(n_heads // hps)):
            span *= 2
        offs = tuple((cf(j) - cf(0) // span * span) * LANES for j in range(hps))
        for g in range(n_heads // hps):
            assert offs == tuple((cf(g * hps + j) - cf(g * hps) // span * span) * LANES for j in range(hps))
        args.append(arr)
        in_specs.append(pl.BlockSpec((n_rows, span * LANES),
                                     lambda b, hg, t: (row_fn(b, t), cf(hg * hps) // span)))
        return offs

    q_offs = [add(arr, cf, tq, lambda b, t: qb0 + b * nqt + t) for arr, cf in q_parts]
    k_offs = [add(arr, cf, lk, lambda b, t: kb0 + b) for arr, cf in k_parts]
    v_offs = add(v_part[0], v_part[1], lk, lambda b, t: kb0 + b)
    kc_offs, vc_offs = [(0,) * hps] * len(q_parts), (0,) * hps
    if cache is not None:
        c_k, c_v, c_rows = cache
        kc_offs = [add(arr, cf, c_rows, lambda b, t: b) for arr, cf in c_k]
        vc_offs = add(c_v[0], c_v[1], c_rows, lambda b, t: b)
    head_lanes = tuple((tuple(o[j] for o in q_offs), tuple(o[j] for o in k_offs), v_offs[j], j * LANES,
                        tuple(o[j] for o in kc_offs), vc_offs[j]) for j in range(hps))
    grid = (n_seq, n_heads // hps, nqt)
    out_spec = pl.BlockSpec((tq, hps * LANES), lambda b, hg, t: (qb0 + b * nqt + t, hg))
    if diff:
        args.append(lam)
        in_specs.append(pl.BlockSpec(lam.shape, lambda b, h, t: (0, 0)))
    aliases = {}
    if into is not None:
        aliases[len(args)] = 0
        args.append(into)
        in_specs.append(pl.BlockSpec(memory_space=pl.ANY))
    kern = functools.partial(_flash_kernel, n_parts=len(q_parts), has_cache=cache is not None, diff=diff, tk=tk,
                             lam_init=lam_init, aliased=into is not None, head_lanes=head_lanes)
    return pl.pallas_call(
        kern, grid=grid, in_specs=in_specs, out_specs=out_spec,
        out_shape=jax.ShapeDtypeStruct((q_parts[0][0].shape[0], out_cols), F32 if diff else BF16),
        input_output_aliases=aliases,
        compiler_params=_cparams("parallel", "parallel", "arbitrary"), name="flash_attention",
    )(*args)


def _mr_kernel(x_ref, a_ref, w_ref, mods_ref, o_ref, *, rows, gate_idx):
    i = pl.program_id(0)
    d = x_ref.shape[1]
    gate = _mod_chunk(mods_ref, rows.group(i), gate_idx, d)
    o_ref[...] = x_ref[...] + gate * _dot(a_ref[...].astype(BF16), w_ref[...])


def _matmul_residual(x, a, w, mods, rows, gate_idx):
    n, d = x.shape
    tm = rows.tm
    return pl.pallas_call(
        functools.partial(_mr_kernel, rows=rows, gate_idx=gate_idx), grid=(n // tm,),
        in_specs=[pl.BlockSpec((tm, d), lambda i: (i, 0)),
                  pl.BlockSpec((tm, a.shape[1]), lambda i: (i, 0)),
                  pl.BlockSpec(w.shape, lambda i: (0, 0)),
                  pl.BlockSpec(mods.shape, lambda i: (0, 0))],
        out_specs=pl.BlockSpec((tm, d), lambda i: (i, 0)),
        out_shape=jax.ShapeDtypeStruct((n, d), F32),
        compiler_params=_cparams("parallel"), name="matmul_residual",
    )(x, a, w, mods)


def _modulated(x_ref, nw_ref, mods_ref, g, idx, d):
    x = _rmsnorm_rows(x_ref[...], nw_ref[...])
    return x * (1.0 + _mod_chunk(mods_ref, g, idx + 1, d)) + _mod_chunk(mods_ref, g, idx, d)


def _ffn_kernel(x_ref, nw_ref, mods_ref, wg_ref, wu_ref, wd_ref, of_ref, ob_ref, z_ref, od_ref, on_ref, sn_ref,
                wo_ref, o_ref, h_scr, acc_scr, xn_scr, *, rows, mod0, mix_gate, lam_init):
    i, f = pl.program_id(0), pl.program_id(1)
    d = x_ref.shape[1]
    g = rows.group(i)

    @pl.when(f == 0)
    def _():
        oa = of_ref[...] + ob_ref[...]
        z = z_ref[...]
        od = od_ref[...]
        ya, yb = [], []
        for hh in range(H_A):
            sl = slice(hh * LANES, (hh + 1) * LANES)
            ya.append(_rmsnorm_rows(oa[:, sl], on_ref[...]) * _silu(z[:, sl]))
        for hh in range(H_B):
            sl = slice(hh * LANES, (hh + 1) * LANES)
            yb.append(_rmsnorm_rows(od[:, sl], sn_ref[...]) * (1.0 - lam_init))
        y = jnp.concatenate(ya + yb, axis=1).astype(BF16)
        x1 = x_ref[...] + _mod_chunk(mods_ref, g, mix_gate, d) * _dot(y, wo_ref[...])
        xn_scr[...] = x1
        v = _rmsnorm_rows(x1, nw_ref[...])
        h_scr[...] = (v * (1.0 + _mod_chunk(mods_ref, g, mod0 + 1, d)) + _mod_chunk(mods_ref, g, mod0, d)).astype(BF16)
        acc_scr[...] = jnp.zeros_like(acc_scr)

    h = h_scr[...]
    act = _silu(_dot(h, wg_ref[...])) * _dot(h, wu_ref[...])
    acc_scr[...] += _dot(act.astype(BF16), wd_ref[...])

    @pl.when(f == pl.num_programs(1) - 1)
    def _():
        o_ref[...] = xn_scr[...] + _mod_chunk(mods_ref, g, mod0 + 2, d) * acc_scr[...]


def _even_tail(x, o_f, o_b, z, o_d, onorm, subln, w_out, norm_w, mods, w_gu, w_down, rows, mix_gate, mod0, tf,
               lam_init):
    n, d = x.shape
    tm = rows.tm
    ff = w_down.shape[0]
    nf = ff // tf
    hw = H_A * DV_A
    row = lambda c: pl.BlockSpec((tm, c), lambda i, f: (i, 0))
    one = lambda c: pl.BlockSpec((1, c), lambda i, f: (0, 0))
    wbuf = dict(pipeline_mode=pl.Buffered(1)) if nf == 1 else {}
    return pl.pallas_call(
        functools.partial(_ffn_kernel, rows=rows, mod0=mod0, mix_gate=mix_gate, lam_init=lam_init),
        grid=(n // tm, nf),
        in_specs=[row(d), one(d), pl.BlockSpec(mods.shape, lambda i, f: (0, 0)),
                  pl.BlockSpec((d, tf), lambda i, f: (0, f), **wbuf),
                  pl.BlockSpec((d, tf), lambda i, f: (0, nf + f), **wbuf),
                  pl.BlockSpec((tf, d), lambda i, f: (f, 0), **wbuf),
                  row(hw), row(hw), row(hw), row(hw), one(LANES), one(LANES),
                  pl.BlockSpec(w_out.shape, lambda i, f: (0, 0))],
        out_specs=row(d),
        out_shape=jax.ShapeDtypeStruct((n, d), F32),
        scratch_shapes=[pltpu.VMEM((tm, d), BF16), pltpu.VMEM((tm, d), F32), pltpu.VMEM((tm, d), F32)],
        compiler_params=_cparams("parallel", "arbitrary"), name="even_tail",
    )(x, norm_w.reshape(1, d).astype(F32), mods, w_gu, w_gu, w_down, o_f, o_b, z, o_d,
      onorm.reshape(1, LANES).astype(F32), subln.reshape(1, LANES).astype(F32), w_out)


def _moe_kernel(x_ref, nw_ref, mods_ref, rw_ref, wgu_ref, wd_ref, *rest, rows, mod0, n_exp, sb, final):
    if final:
        fw_ref, o_ctx_ref, o_lat_ref, h_scr, g_scr, rk_scr, rkt_scr, acc_scr = rest
    else:
        o_ref, h_scr, g_scr, rk_scr, rkt_scr, acc_scr = rest
    i, e = pl.program_id(0), pl.program_id(1)
    tm, d = x_ref.shape
    g = rows.group(i)
    lane = lax.broadcasted_iota(jnp.int32, (1, LANES), 1)

    @pl.when(e == 0)
    def _():
        h = _modulated(x_ref, nw_ref, mods_ref, g, mod0, d)
        h_scr[...] = h.astype(BF16)
        acc_scr[...] = jnp.zeros_like(acc_scr)
        logits = jnp.where(lane < n_exp, _mm(h, rw_ref[...], True), -1e30)
        pe = jnp.exp(logits - jnp.max(logits, axis=-1, keepdims=True))
        probs = pe / jnp.sum(pe, axis=-1, keepdims=True)
        p1 = jnp.max(probs, axis=-1, keepdims=True)
        i1 = jnp.min(jnp.where(probs == p1, lane, LANES), axis=-1, keepdims=True)
        rest = jnp.where(lane == i1, -1.0, probs)
        p2 = jnp.max(rest, axis=-1, keepdims=True)
        i2 = jnp.min(jnp.where(rest == p2, lane, LANES), axis=-1, keepdims=True)
        den = p1 + p2
        gates = jnp.where(lane == i1, p1 / den, 0.0) + jnp.where(lane == i2, p2 / den, 0.0)
        g_scr[...] = gates
        t_r = lax.broadcasted_iota(jnp.int32, (tm, tm), 0)
        t_c = lax.broadcasted_iota(jnp.int32, (tm, tm), 1)
        earlier = jnp.where(t_r > t_c, 1.0, 0.0).astype(BF16)
        sel = gates > 0.0
        rank = jnp.where(sel, _dot(earlier, jnp.where(sel, 1.0, 0.0).astype(BF16)), -1.0)
        rk_scr[...] = rank
        rkt_scr[...] = jnp.transpose(rank)[:rkt_scr.shape[0], :]

    ff = wd_ref.shape[0]
    g_e = jnp.sum(jnp.where(lane == e, g_scr[...], 0.0), axis=-1, keepdims=True)
    rk_col = jnp.sum(jnp.where(lane == e, rk_scr[...], 0.0), axis=-1, keepdims=True)
    rk_row = rkt_scr[pl.ds(e, 1), :]
    n_rows = (jnp.max(rk_row) + 1.0).astype(jnp.int32)
    g_hi = g_e.astype(BF16).astype(F32)
    g2 = jnp.where(lane == 0, g_hi, jnp.where(lane == 1, g_e - g_hi, 0.0)).astype(BF16)
    def expert_rows(slot0, nr):
        base = slot0.astype(F32)
        r_sub = lax.broadcasted_iota(jnp.int32, (nr, 1), 0).astype(F32)
        r_lane = lax.broadcasted_iota(jnp.int32, (1, nr), 1).astype(F32)
        pick = jnp.where(rk_row == r_sub + base, 1.0, 0.0).astype(BF16)
        xg = _dot(pick, h_scr[...]).astype(BF16)
        gr = _dot(pick, g2)
        gate_r = gr[:, 0:1] + gr[:, 1:2]
        gu = _dot(xg, wgu_ref[...])
        act = _silu(gu[:, :ff]) * gu[:, ff:]
        y = _dot(act.astype(BF16), wd_ref[...])
        put = jnp.where(rk_col == r_lane + base, 1.0, 0.0).astype(BF16)
        acc_scr[...] += _dot(put, (y * gate_r).astype(BF16))

    half, quarter = sb // 2, sb // 4
    rem = n_rows % sb
    n_full = n_rows // sb + jnp.where(rem > half, 1, 0)

    def full_body(jb, carry):
        expert_rows(jb * sb, sb)
        return carry

    lax.fori_loop(0, n_full, full_body, 0)

    @pl.when((rem > quarter) & (rem <= half))
    def _():
        expert_rows(n_full * sb, half)

    @pl.when((rem > 0) & (rem <= quarter))
    def _():
        expert_rows(n_full * sb, quarter)

    def result():
        return x_ref[...] + _mod_chunk(mods_ref, g, mod0 + 2, d) * acc_scr[...]

    last = e == n_exp - 1
    if final:
        @pl.when(last & (i < rows.ctx_tiles))
        def _():
            o_ctx_ref[...] = _rmsnorm_rows(result(), fw_ref[...])

        @pl.when(last & (i >= rows.ctx_tiles))
        def _():
            o_lat_ref[...] = _rmsnorm_rows(result(), fw_ref[...])
    else:
        @pl.when(last)
        def _():
            o_ref[...] = result()


def _moe(x, norm_w, mods, router_w, w_gu, w_down, rows, mod0, final_w=None):
    n, d = x.shape
    tm = rows.tm
    n_exp, _, ff2 = w_gu.shape
    rw = jnp.zeros((d, LANES), F32).at[:, :n_exp].set(router_w.astype(F32))
    final = final_w is not None
    args = [x, norm_w.reshape(1, d).astype(F32), mods, rw, w_gu, w_down]
    in_specs = [pl.BlockSpec((tm, d), lambda i, e: (i, 0)),
                pl.BlockSpec((1, d), lambda i, e: (0, 0)),
                pl.BlockSpec(mods.shape, lambda i, e: (0, 0)),
                pl.BlockSpec((d, LANES), lambda i, e: (0, 0)),
                pl.BlockSpec((None, d, ff2), lambda i, e: (e, 0, 0)),
                pl.BlockSpec((None, ff2 // 2, d), lambda i, e: (e, 0, 0))]
    if final:
        args.append(final_w.reshape(1, d).astype(F32))
        in_specs.append(pl.BlockSpec((1, d), lambda i, e: (0, 0)))
        ct = rows.ctx_tiles
        out_specs = [pl.BlockSpec((tm, d), lambda i, e: (jnp.minimum(i, ct - 1), 0), pipeline_mode=pl.Buffered(1)),
                     pl.BlockSpec((tm, d), lambda i, e: (jnp.maximum(i - ct, 0), 0), pipeline_mode=pl.Buffered(1))]
        out_shape = [jax.ShapeDtypeStruct((ct * tm, d), F32), jax.ShapeDtypeStruct((n - ct * tm, d), F32)]
    else:
        out_specs = pl.BlockSpec((tm, d), lambda i, e: (i, 0))
        out_shape = jax.ShapeDtypeStruct((n, d), F32)
    return pl.pallas_call(
        functools.partial(_moe_kernel, rows=rows, mod0=mod0, n_exp=n_exp, sb=MOE_ROW_BLOCK, final=final),
        grid=(n // tm, n_exp), in_specs=in_specs, out_specs=out_specs, out_shape=out_shape,
        scratch_shapes=[pltpu.VMEM((tm, d), BF16), pltpu.VMEM((tm, LANES), F32), pltpu.VMEM((tm, LANES), F32),
                        pltpu.VMEM((SUBLANES * ((n_exp + SUBLANES - 1) // SUBLANES), tm), F32),
                        pltpu.VMEM((tm, d), F32)],
        compiler_params=_cparams("arbitrary", "arbitrary"), name="moe",
    )(*args)


def kernel(x_prompt, x_sample, state_delta, cache_diff_k, cache_diff_v, cache_mla_ckv, cache_mla_krope, c, c_ctx,
           mod_w, mod_b, norm_mix, norm_ffn, final_norm, ev_w_in, ev_conv_w, ev_a_log, ev_dt_bias, ev_onorm,
           ev_lambda, ev_subln, ev_w_out, ffn_w_gu, ffn_w_down, od_w_in, od_q_norm, od_kv_norm, od_w_uq, od_w_ukv,
           od_w_out, moe_router, moe_w_gu, moe_w_down):
    bp, lp, d = x_prompt.shape
    bs, ls, _ = x_sample.shape
    past = cache_diff_k.shape[3]
    depth = mod_w.shape[0]
    n_p, n_s = bp * lp, bs * ls
    n = n_p + n_s
    assert n_p % ls == 0 and past == lp and depth % 2 == 0

    rows = _Rows(n_p, n_s, ls, _tile(math.gcd(n_p, ls), 512))
    rows_m = _Rows(n_p, n_s, ls, _tile(math.gcd(n_p, ls), 1024))
    rg = _tile(math.gcd(lp, ls), 256)
    cos_t, sin_t = _rope_tables(rows, ls)

    x = jnp.concatenate([x_prompt.reshape(n_p, d), x_sample.reshape(n_s, d)], axis=0)
    n_grp = 1 + bs
    cc = jnp.zeros((2 * SUBLANES * ((n_grp + 15) // 16), d), F32).at[0].set(c_ctx).at[1:n_grp].set(c)
    mods_all = _mods(cc, mod_w, mod_b)

    hq = H_A * DK_A
    hb2 = H_B * 2 * DQK_B
    sm0 = 4 * hq
    qb0 = sm0 + 4 * H_A
    q_scale = (DQK_B ** -0.5) * LOG2E
    mla_scale = ((NOPE_C + ROPE_C) ** -0.5) * LOG2E
    sd_new, dk_new, dv_new, ckv_new, kr_new = [], [], [], [], []

    for layer in range(depth):
        j = layer // 2
        mods = mods_all[layer]
        if layer % 2 == 0:
            lam_init = 0.8 - 0.6 * math.exp(-0.3 * layer)
            w = ev_w_in[j]
            sm = w[:, sm0:qb0].reshape(d, 4, H_A).transpose(0, 2, 1).reshape(d, 4 * H_A)
            w_in = jnp.concatenate([w[:, :sm0], sm, jnp.zeros((d, LANES - 4 * H_A), w.dtype), w[:, qb0:]],
                                   axis=1).astype(BF16)
            qkvn, z_gate, gcol, grow, att, kv_f = _even_proj(
                x, norm_mix[layer], mods, w_in, ev_conv_w[j], ev_a_log[j], ev_dt_bias[j], (cos_t, sin_t), rows,
                3 * hq, hq, hb2, q_scale, lp, ls, n_p)
            o_f, o_b, s_p = _delta_stage(qkvn, gcol, grow, jnp.zeros((bp, 2, H_A, DK_A, DV_A), F32), bp, lp, 0, rg,
                                         precise=True)
            o_f, o_b, _ = _delta_stage(qkvn, gcol, grow, state_delta[:, j].astype(F32), bs, ls, n_p, rg,
                                       into=(o_f, o_b))
            sd_new.append(s_p)
            lam_p = ev_lambda[j].astype(F32)
            nb = hb2 // LANES
            qp, kp, vp = [(att, lambda h: h)], [(att, lambda h: nb + h)], (att, lambda h: 2 * nb + h)
            o_d = _flash(qp, kp, vp, None, lam_p, n_seq=bp, n_heads=H_B, lq=lp, lk=lp, q_row0=0, k_row0=0,
                         tq=_tile(lp, 256), tk=_tile(lp, 512), diff=True, lam_init=lam_init, out_cols=hb2,
                         heads_per_step=H_B)
            ck = cache_diff_k[:, j].transpose(0, 2, 1, 3).reshape(bs * past, H_B * 2 * DQK_B).astype(BF16)
            cv = cache_diff_v[:, j].transpose(0, 2, 1, 3).reshape(bs * past, H_B * DV_B).astype(BF16)
            cache = ([(ck, lambda h: h)], (cv, lambda h: h), past)
            o_d = _flash(qp, kp, vp, cache, lam_p, n_seq=bs, n_heads=H_B, lq=ls, lk=ls, q_row0=n_p, k_row0=n_p,
                         tq=_tile(ls, 512), tk=_tile(ls, 2048), diff=True, lam_init=lam_init, out_cols=hb2, into=o_d,
                         heads_per_step=FLASH_HEADS_PER_STEP)
            dk_new.append(kv_f[:n_p, :hb2].reshape(bp, lp, H_B, 2 * DQK_B).transpose(0, 2, 1, 3))
            dv_new.append(kv_f[:n_p, hb2:].reshape(bp, lp, H_B, DV_B).transpose(0, 2, 1, 3))
            ff = ffn_w_down.shape[1]
            x = _even_tail(x, o_f, o_b, z_gate, o_d, ev_onorm[j], ev_subln[j], ev_w_out[j].astype(BF16),
                           norm_ffn[layer], mods, ffn_w_gu[j].astype(BF16), ffn_w_down[j].astype(BF16), rows, 2, 3,
                           ff, lam_init)
        else:
            n_in = Q_LORA + KV_LORA + ROPE_C
            pad = (-n_in) % LANES
            w_in = jnp.concatenate([od_w_in[j], jnp.zeros((d, pad), od_w_in.dtype)], axis=1).astype(BF16)
            wq = od_w_uq[j].reshape(Q_LORA, H_C, NOPE_C + ROPE_C)
            wq_rope = jnp.concatenate([wq[:, :, NOPE_C:], jnp.zeros((Q_LORA, H_C, LANES - ROPE_C), wq.dtype)], axis=2)
            wq2 = jnp.concatenate([wq[:, :, :NOPE_C].reshape(Q_LORA, H_C * NOPE_C),
                                   wq_rope.reshape(Q_LORA, H_C * LANES)], axis=1).astype(BF16)
            q_all, kv_tok, kr_tok, ckv_n, kr_f = _odd_proj(
                x, norm_mix[layer], mods, w_in, od_q_norm[j], od_kv_norm[j], wq2, od_w_ukv[j].astype(BF16),
                (cos_t, sin_t), rows, mla_scale, H_C * NOPE_C)
            ckv_new.append(ckv_n[:n_p].reshape(bp, lp, KV_LORA))
            kr_new.append(kr_f[:n_p, :ROPE_C].reshape(bp, lp, ROPE_C))
            kv_c = _matmul(cache_mla_ckv[:, j].reshape(bs * past, KV_LORA), od_w_ukv[j].astype(BF16),
                           _tile(bs * past, 512), BF16)
            kr_c = jnp.concatenate([cache_mla_krope[:, j].reshape(bs * past, ROPE_C),
                                    jnp.zeros((bs * past, LANES - ROPE_C), F32)], axis=1).astype(BF16)
            qp = [(q_all, lambda h: h), (q_all, lambda h: H_C + h)]
            kp = [(kv_tok, lambda h: 2 * h), (kr_tok, lambda h: 0)]
            vp = (kv_tok, lambda h: 2 * h + 1)
            o = _flash(qp, kp, vp, None, None, n_seq=bp, n_heads=H_C, lq=lp, lk=lp, q_row0=0, k_row0=0,
                       tq=_tile(lp, 256), tk=_tile(lp, 512), diff=False, lam_init=0.0, out_cols=H_C * V_C,
                       heads_per_step=H_C)
            cache = ([(kv_c, lambda h: 2 * h), (kr_c, lambda h: 0)], (kv_c, lambda h: 2 * h + 1), past)
            o = _flash(qp, kp, vp, cache, None, n_seq=bs, n_heads=H_C, lq=ls, lk=ls, q_row0=n_p, k_row0=n_p,
                       tq=_tile(ls, 1024), tk=_tile(ls, 2048), diff=False, lam_init=0.0, out_cols=H_C * V_C, into=o,
                       heads_per_step=FLASH_HEADS_PER_STEP)
            x = _matmul_residual(x, o, od_w_out[j].astype(BF16), mods, rows, 2)
            x = _moe(x, norm_ffn[layer], mods, moe_router[j], moe_w_gu[j].astype(BF16), moe_w_down[j].astype(BF16),
                     rows_m, 3, final_w=final_norm if layer == depth - 1 else None)

    y_p, y_s = x
    return (y_p.reshape(bp, lp, d), y_s.reshape(bs, ls, d), jnp.stack(sd_new, axis=1),
            jnp.stack(dk_new, axis=1), jnp.stack(dv_new, axis=1), jnp.stack(ckv_new, axis=1),
            jnp.stack(kr_new, axis=1))
```

```python
import functools
import math

import jax
import jax.numpy as jnp
from jax import lax
from jax.experimental import pallas as pl
from jax.experimental.pallas import tpu as pltpu

F32 = jnp.float32
BF16 = jnp.bfloat16

EPS = 1e-6
LOG2E = 1.4426950408889634
GRID_W = 64
ROPE_BASE = 10000.0
H_A, DK_A, DV_A = 4, 128, 128
CONV_K = 3
CHUNK = 64
H_B, DQK_B, DV_B = 4, 64, 128
H_C, NOPE_C, ROPE_C, V_C = 8, 128, 64, 128
Q_LORA, KV_LORA = 384, 256
LANES = 128
SUBLANES = 8
VMEM_LIMIT_BYTES = 56 * 1024 * 1024


def _cparams(*sem):
    return pltpu.CompilerParams(dimension_semantics=sem, vmem_limit_bytes=VMEM_LIMIT_BYTES)


def _dot(a, b):
    return jnp.dot(a, b, preferred_element_type=F32)


def _dot_nt(a, b):
    return lax.dot_general(a, b, (((1,), (1,)), ((), ())), preferred_element_type=F32)


def _dot_tn(a, b):
    return lax.dot_general(a, b, (((0,), (0,)), ((), ())), preferred_element_type=F32)


def _split3(x):
    hi = x.astype(BF16)
    r = x - hi.astype(F32)
    mid = r.astype(BF16)
    lo = (r - mid.astype(F32)).astype(BF16)
    return hi, mid, lo


def _dot_f32(a, b):
    ah, am, al = _split3(a)
    bh, bm, bl = _split3(b)
    return (_dot(ah, bh) + (_dot(ah, bm) + _dot(am, bh))
            + (_dot(am, bm) + _dot(ah, bl) + _dot(al, bh)))


def _silu(x):
    return x * jax.nn.sigmoid(x)


def _tile(n, pref, mult=SUBLANES):
    t = min(n, pref)
    while t > mult and (n % t or t % mult):
        t -= mult
    assert n % t == 0, (n, pref)
    return t


class _Rows:
    def __init__(self, n_ctx, n_lat, lat_len, tm):
        assert n_ctx % tm == 0 and lat_len % tm == 0
        self.tm = tm
        self.n = n_ctx + n_lat
        self.ctx_tiles = n_ctx // tm
        self.seq_tiles = lat_len // tm

    def group(self, i):
        return jnp.where(i < self.ctx_tiles, 0, 1 + jnp.maximum(i - self.ctx_tiles, 0) // self.seq_tiles)

    def pos_block(self, i):
        return jnp.where(i < self.ctx_tiles, 0, 1 + jnp.maximum(i - self.ctx_tiles, 0) % self.seq_tiles)


def _mod_kernel(c_ref, w_ref, b_ref, o_ref):
    o_ref[...] = _dot_f32(_silu(c_ref[...]), w_ref[...]) + b_ref[...]


def _mods(cc, mod_w, mod_b):
    depth, d, n6 = mod_w.shape
    tn = _tile(n6, 1536, LANES)
    return pl.pallas_call(
        _mod_kernel,
        grid=(depth, n6 // tn),
        in_specs=[pl.BlockSpec(cc.shape, lambda l, j: (0, 0)),
                  pl.BlockSpec((None, d, tn), lambda l, j: (l, 0, j)),
                  pl.BlockSpec((None, 1, tn), lambda l, j: (l, 0, j))],
        out_specs=pl.BlockSpec((None, cc.shape[0], tn), lambda l, j: (l, 0, j)),
        out_shape=jax.ShapeDtypeStruct((depth, cc.shape[0], n6), F32),
        compiler_params=_cparams("parallel", "parallel"),
        name="mods",
    )(cc, mod_w, mod_b.reshape(depth, 1, n6))


def _mod_chunk(mods_ref, g, idx, d):
    return mods_ref[pl.ds(g, 1), idx * d:(idx + 1) * d]


def _rmsnorm_rows(x, w):
    ms = jnp.mean(x * x, axis=-1, keepdims=True)
    return x * lax.rsqrt(ms + EPS) * w


def _rope(x, cos, sin_signed):
    lane = lax.broadcasted_iota(jnp.int32, (1, LANES), 1)
    low = (lane % 32) < 16
    outs = []
    for c in range(x.shape[1] // LANES):
        xc = x[:, c * LANES:(c + 1) * LANES]
        fwd = pltpu.roll(xc, LANES - 16, axis=1)
        bwd = pltpu.roll(xc, 16, axis=1)
        outs.append(xc * cos + jnp.where(low, fwd, bwd) * sin_signed)
    return outs[0] if len(outs) == 1 else jnp.concatenate(outs, axis=1)


def _rope_tables(rows, lat_len):
    pos = jnp.arange(lat_len)
    r = (pos // GRID_W).astype(F32)
    c = (pos % GRID_W).astype(F32)
    inv = ROPE_BASE ** (-jnp.arange(0, 32, 2, dtype=F32) / 32)
    ang_r = r[:, None] * inv[None, :]
    ang_c = c[:, None] * inv[None, :]
    ang = jnp.concatenate([ang_r, ang_r, ang_c, ang_c] * 2, axis=-1)
    sign = jnp.where((jnp.arange(LANES) % 32) < 16, -1.0, 1.0).astype(F32)
    cos = jnp.concatenate([jnp.ones((rows.tm, LANES), F32), jnp.cos(ang)], axis=0)
    sin = jnp.concatenate([jnp.zeros((rows.tm, LANES), F32), jnp.sin(ang) * sign[None, :]], axis=0)
    return cos, sin


def _mm_kernel(x_ref, w_ref, o_ref):
    o_ref[...] = _dot(x_ref[...].astype(BF16), w_ref[...]).astype(o_ref.dtype)


def _matmul(x, w, tm, out_dtype):
    n, k = x.shape
    return pl.pallas_call(
        _mm_kernel, grid=(n // tm,),
        in_specs=[pl.BlockSpec((tm, k), lambda i: (i, 0)), pl.BlockSpec(w.shape, lambda i: (0, 0))],
        out_specs=pl.BlockSpec((tm, w.shape[1]), lambda i: (i, 0)),
        out_shape=jax.ShapeDtypeStruct((n, w.shape[1]), out_dtype),
        compiler_params=_cparams("parallel"), name="matmul",
    )(x, w)


PROJ_CHUNK = 512


def _pos_in_seq(r, length):
    return r & (length - 1) if length & (length - 1) == 0 else lax.rem(r, length)


def _even_proj_kernel(x_ref, xp_ref, xn_ref, nw_ref, mods_ref, w_ref, cw_ref, alog_ref, dt_ref, cos_ref, sin_ref,
                      qkv_ref, z_ref, gc_ref, gr_ref, att_ref, kv_ref,
                      *, rows, n_conv, n_z, hb2, q_scale, ctx_len, lat_len, n_ctx_rows):
    i = pl.program_id(0)
    tm, d = x_ref.shape
    g = rows.group(i)

    def modulated(ref):
        v = _rmsnorm_rows(ref[...], nw_ref[...])
        return (v * (1.0 + _mod_chunk(mods_ref, g, 1, d)) + _mod_chunk(mods_ref, g, 0, d)).astype(BF16)

    h = modulated(x_ref)
    halo = xp_ref.shape[0]
    h_ext = jnp.concatenate([h, modulated(xp_ref), modulated(xn_ref)], axis=0)
    p_ext = jnp.concatenate([_dot(h_ext, w_ref[:, c0:min(c0 + PROJ_CHUNK, n_conv)])
                             for c0 in range(0, n_conv, PROJ_CHUNK)], axis=1)
    p = p_ext[:tm]
    p_before = p_ext[tm + halo - 1:tm + halo, :]
    p_after = p_ext[tm + halo:tm + halo + 1, :]
    row = lax.broadcasted_iota(jnp.int32, (tm, 1), 0)
    r_glob = row + i * tm
    in_ctx = i < rows.ctx_tiles
    pos = jnp.where(in_ctx, _pos_in_seq(r_glob, ctx_len), _pos_in_seq(r_glob - n_ctx_rows, lat_len))
    seq_last = jnp.where(in_ctx, ctx_len - 1, lat_len - 1)
    x_prev = jnp.where(pos == 0, 0.0, jnp.where(row == 0, p_before, pltpu.roll(p, 1, axis=0)))
    x_next = jnp.where(pos == seq_last, 0.0, jnp.where(row == tm - 1, p_after, pltpu.roll(p, tm - 1, axis=0)))
    cw = cw_ref[...]
    y = _silu(x_prev * cw[0:1, :] + p * cw[1:2, :] + x_next * cw[2:3, :])
    n_qk = 2 * H_A
    for hh in range(n_conv // LANES):
        yh = y[:, hh * LANES:(hh + 1) * LANES]
        if hh < n_qk:
            yh = yh * lax.rsqrt(jnp.sum(yh * yh, axis=-1, keepdims=True) + EPS)
            if hh < n_qk // 2:
                yh = yh * (DK_A ** -0.5)
        qkv_ref[:, hh * LANES:(hh + 1) * LANES] = yh
    z_ref[...] = _dot(h, w_ref[:, n_conv:n_conv + n_z])
    s = _dot(h, w_ref[:, n_conv + n_z:n_conv + n_z + LANES])
    lane = lax.broadcasted_iota(jnp.int32, (1, LANES), 1)
    zg = s + dt_ref[...]
    softplus = jnp.maximum(zg, 0.0) + jnp.log(1.0 + jnp.exp(-jnp.abs(zg)))
    gates = jnp.where((lane % 4) < 2, -jnp.exp(alog_ref[...]) * softplus, jax.nn.sigmoid(s))
    gc_ref[...] = gates
    gr_ref[...] = jnp.transpose(gates)[:gr_ref.shape[0], :]
    n_main = n_conv + n_z + LANES
    cos, sin = cos_ref[...], sin_ref[...]
    for part in range(3):
        for c0 in range(0, hb2, PROJ_CHUNK):
            c1 = min(c0 + PROJ_CHUNK, hb2)
            y = _dot(h, w_ref[:, n_main + part * hb2 + c0:n_main + part * hb2 + c1])
            if part < 2:
                y = _rope(y, cos, sin)
            if part == 0:
                y = y * q_scale
            else:
                kv_ref[:, (part - 1) * hb2 + c0:(part - 1) * hb2 + c1] = y
            att_ref[:, part * hb2 + c0:part * hb2 + c1] = y.astype(BF16)


def _even_proj(x, norm_w, mods, w, conv_w, a_log, dt_bias, rope, rows, n_conv, n_z, hb2, q_scale, ctx_len,
               lat_len, n_ctx_rows):
    n, d = x.shape
    tm = rows.tm
    assert conv_w.shape[0] == CONV_K
    halo = 2 * SUBLANES
    hb = tm // halo
    alog_row = jnp.zeros((1, LANES), F32).at[0, :4 * H_A].set(
        jnp.stack([a_log[0], a_log[1], a_log[0], a_log[1]], axis=-1).reshape(-1).astype(F32))
    dt_row = jnp.zeros((1, LANES), F32).at[0, :4 * H_A].set(
        jnp.stack([dt_bias[0], dt_bias[1], jnp.zeros_like(dt_bias[0]), jnp.zeros_like(dt_bias[0])],
                  axis=-1).reshape(-1).astype(F32))
    kern = functools.partial(_even_proj_kernel, rows=rows, n_conv=n_conv, n_z=n_z, hb2=hb2, q_scale=q_scale,
                             ctx_len=ctx_len, lat_len=lat_len, n_ctx_rows=n_ctx_rows)
    tab = pl.BlockSpec((tm, LANES), lambda i: (rows.pos_block(i), 0))
    full = lambda a: pl.BlockSpec(a.shape, lambda i: (0, 0))
    nw = norm_w.reshape(1, d).astype(F32)
    cw = conv_w.astype(F32)
    outs = [(n_conv, F32), (n_z, F32), (LANES, F32), None, (3 * hb2, BF16), (2 * hb2, F32)]
    out_specs = [pl.BlockSpec((tm, c[0]), lambda i: (i, 0)) if c else pl.BlockSpec((4 * H_A, tm), lambda i: (0, i))
                 for c in outs]
    out_shape = [jax.ShapeDtypeStruct((n, c[0]), c[1]) if c else jax.ShapeDtypeStruct((4 * H_A, n), F32)
                 for c in outs]
    return pl.pallas_call(
        kern, grid=(n // tm,),
        in_specs=[pl.BlockSpec((tm, d), lambda i: (i, 0)),
                  pl.BlockSpec((halo, d), lambda i: (jnp.maximum(i * hb - 1, 0), 0)),
                  pl.BlockSpec((halo, d), lambda i: (jnp.minimum((i + 1) * hb, n // halo - 1), 0)),
                  full(nw), full(mods), full(w), full(cw), full(alog_row), full(dt_row), tab, tab],
        out_specs=out_specs, out_shape=out_shape,
        compiler_params=_cparams("parallel"), name="even_proj",
    )(x, x, x, nw, mods, w, cw, alog_row, dt_row, *rope)


def _odd_proj_kernel(x_ref, nw_ref, mods_ref, w_ref, qn_ref, kvn_ref, wq_ref, wkv_ref, cos_ref, sin_ref,
                     q_ref, kv_ref, kr_ref, ckv_ref, krf_ref, *, rows, scale, n_nope):
    i = pl.program_id(0)
    d = x_ref.shape[1]
    h = _modulated(x_ref, nw_ref, mods_ref, rows.group(i), 0, d).astype(BF16)
    p = _dot(h, w_ref[...])
    cos, sin = cos_ref[...], sin_ref[...]
    cq = _rmsnorm_rows(p[:, :Q_LORA], qn_ref[...]).astype(BF16)
    for c0 in range(0, wq_ref.shape[1], PROJ_CHUNK):
        y = _dot(cq, wq_ref[:, c0:c0 + PROJ_CHUNK])
        if c0 >= n_nope:
            y = _rope(y, cos, sin)
        q_ref[:, c0:c0 + PROJ_CHUNK] = (y * scale).astype(BF16)
    ckv = _rmsnorm_rows(p[:, Q_LORA:Q_LORA + KV_LORA], kvn_ref[...])
    ckv_ref[...] = ckv
    ckv16 = ckv.astype(BF16)
    for c0 in range(0, wkv_ref.shape[1], PROJ_CHUNK):
        kv_ref[:, c0:c0 + PROJ_CHUNK] = _dot(ckv16, wkv_ref[:, c0:c0 + PROJ_CHUNK]).astype(BF16)
    kr = _rope(p[:, Q_LORA + KV_LORA:], cos, sin)
    krf_ref[...] = kr
    kr_ref[...] = kr.astype(BF16)


def _odd_proj(x, norm_w, mods, w_in, q_norm, kv_norm, wq, wkv, rope, rows, scale, n_nope):
    n, d = x.shape
    tm = rows.tm
    tab = pl.BlockSpec((tm, LANES), lambda i: (rows.pos_block(i), 0))
    full = lambda a: pl.BlockSpec(a.shape, lambda i: (0, 0))
    qn = q_norm.reshape(1, -1).astype(F32)
    kvn = kv_norm.reshape(1, -1).astype(F32)
    nw = norm_w.reshape(1, d).astype(F32)
    outs = [(wq.shape[1], BF16), (wkv.shape[1], BF16), (LANES, BF16), (KV_LORA, F32), (LANES, F32)]
    return pl.pallas_call(
        functools.partial(_odd_proj_kernel, rows=rows, scale=scale, n_nope=n_nope), grid=(n // tm,),
        in_specs=[pl.BlockSpec((tm, d), lambda i: (i, 0)), full(nw), full(mods), full(w_in), full(qn), full(kvn),
                  full(wq), full(wkv), tab, tab],
        out_specs=[pl.BlockSpec((tm, c), lambda i: (i, 0)) for c, _ in outs],
        out_shape=[jax.ShapeDtypeStruct((n, c), dt) for c, dt in outs],
        compiler_params=_cparams("parallel"), name="odd_proj",
    )(x, nw, mods, w_in, qn, kvn, wq, wkv, *rope)


def _split2(x):
    hi = x.astype(BF16)
    return hi, (x - hi.astype(F32)).astype(BF16)


def _mm(a, b, precise, dot=_dot):
    if not precise:
        return dot(a.astype(BF16), b.astype(BF16))
    ah, al = _split2(a)
    bh, bl = _split2(b)
    return dot(ah, bh) + (dot(ah, bl) + dot(al, bh))


SOLVE_BLOCK = SUBLANES
MOE_ROW_BLOCK = 256
FLASH_HEADS_PER_STEP = 2
DELTA_HEADS_PER_STEP = 4


def _tri_solve_many(ms, rs, revs):
    c, width = rs[0].shape
    blk = SOLVE_BLOCK
    nb, gpb, ng = c // blk, blk // SUBLANES, c // SUBLANES
    col = lax.broadcasted_iota(jnp.int32, (blk, c), 1)
    mgs = [[m[SUBLANES * g:SUBLANES * (g + 1), :] for g in range(ng)] for m in ms]
    xgs = [[r[SUBLANES * g:SUBLANES * (g + 1), :] for g in range(ng)] for r in rs]
    zero_blk = jnp.zeros((blk, width), BF16)
    fin_h = [[zero_blk] * nb for _ in ms]
    fin_l = [[zero_blk] * nb for _ in ms]
    for bi in range(nb):
        blocks = [nb - 1 - bi if rev else bi for rev in revs]
        if bi > 0:
            for s, (rev, b) in enumerate(zip(revs, blocks)):
                done = (col >= (b + 1) * blk) if rev else (col < b * blk)
                lh, ll = _split2(jnp.where(done, ms[s][b * blk:(b + 1) * blk, :], 0.0))
                xh = jnp.concatenate(fin_h[s], axis=0)
                xl = jnp.concatenate(fin_l[s], axis=0)
                upd = _dot(lh, xh) + (_dot(lh, xl) + _dot(ll, xh))
                for gg in range(gpb):
                    g = b * gpb + gg
                    xgs[s][g] = xgs[s][g] - upd[SUBLANES * gg:SUBLANES * (gg + 1), :]
        for t in range(blk - 1):
            for s, (rev, b) in enumerate(zip(revs, blocks)):
                j = b * blk + (blk - 1 - t if rev else t)
                xj = xgs[s][j // SUBLANES][j % SUBLANES:j % SUBLANES + 1, :]
                groups = range(b * gpb, (j - 1) // SUBLANES + 1) if rev else range((j + 1) // SUBLANES, (b + 1) * gpb)
                for g in groups:
                    xgs[s][g] = xgs[s][g] - mgs[s][g][:, j:j + 1] * xj
        if bi < nb - 1:
            for s, b in enumerate(blocks):
                fin_h[s][b], fin_l[s][b] = _split2(jnp.concatenate(xgs[s][b * gpb:(b + 1) * gpb], axis=0))
    return [jnp.concatenate(xg, axis=0) for xg in xgs]


def _delta_kernel(qf, kf, vf, gcf, grf, qb, kb, vb, gcb, grb, s0_ref, *rest, chunk, hpb, n_alias, precise):
    of_ref, ob_ref, so_ref, s_scr = rest[n_alias:]
    h0 = pl.program_id(1) * hpb
    i = pl.program_id(2)
    n_i = pl.num_programs(2)

    @pl.when(i == 0)
    def _():
        s_scr[...] = s0_ref[...]

    rg = qf.shape[0]
    nc = rg // chunk
    rowi = lax.broadcasted_iota(jnp.int32, (rg, rg), 0)
    coli = lax.broadcasted_iota(jnp.int32, (rg, rg), 1)
    same = (rowi // chunk) == (coli // chunk)
    lane = lax.broadcasted_iota(jnp.int32, (1, LANES), 1)
    r_c = lax.broadcasted_iota(jnp.int32, (chunk, chunk), 0)
    c_c = lax.broadcasted_iota(jnp.int32, (chunk, chunk), 1)

    prep = []
    for d, (q_ref, k_ref, v_ref, gc_ref, gr_ref) in enumerate(((qf, kf, vf, gcf, grf), (qb, kb, vb, gcb, grb))):
        rev = d == 1
        incl_big = same & ((rowi <= coli) if rev else (rowi >= coli))
        lm = jnp.where(incl_big, 1.0, 0.0).astype(BF16)
        gcols = gc_ref[...]
        grows = gr_ref[...]
        gh, gm, gl_ = _split3(gcols)
        cum_c = _dot(lm, gh) + _dot(lm, gm) + _dot(lm, gl_)
        th, tm_, tl = _split3(grows)
        cum_r = _dot_nt(th, lm) + _dot_nt(tm_, lm) + _dot_nt(tl, lm)

        def col(a, ln):
            return jnp.sum(jnp.where(lane == ln, a, 0.0), axis=1, keepdims=True)

        sub = lax.broadcasted_iota(jnp.int32, (cum_r.shape[0], 1), 0)
        incl = (r_c <= c_c) if rev else (r_c >= c_c)
        strict = (r_c < c_c) if rev else (r_c > c_c)
        for hh in range(hpb):
            h = h0 + hh
            hs = slice(hh * LANES, (hh + 1) * LANES)
            gcum = col(cum_c, 4 * h + d)
            beta = col(gcols, 4 * h + 2 + d)
            grow = jnp.sum(jnp.where(sub == 4 * h + d, cum_r, 0.0), axis=0, keepdims=True)
            q = q_ref[:, hs]
            k = k_ref[:, hs]
            v = v_ref[:, hs]
            kbeta = k * beta
            eg = jnp.exp(gcum)
            rhs = jnp.concatenate([v * beta, kbeta * eg], axis=1)
            ms, rs, a_in = [], [], []
            for c in range(nc):
                sl = slice(c * chunk, (c + 1) * chunk)
                e = jnp.exp(jnp.where(incl, gcum[sl] - grow[:, sl], 0.0))
                kk = _mm(kbeta[sl], k[sl], precise, _dot_nt)
                qk = _mm(q[sl], k[sl], precise, _dot_nt)
                a_in.append(jnp.where(incl, qk * e, 0.0))
                ms.append(jnp.where(strict, kk * e, 0.0))
                rs.append(rhs[sl])
            prep.append((rev, hh, ms, rs, a_in, q * eg, k, gcum))

    all_terms = []
    half = len(prep) // 2
    for wave in (prep[:half], prep[half:]) if half else (prep,):
        sols = _tri_solve_many([m for p in wave for m in p[2]], [r for p in wave for r in p[3]],
                               [p[0] for p in wave for _ in range(nc)])
        for n_u, (rev, hh, _, _, a_in, qg, k, gcum) in enumerate(wave):
            xs = sols[n_u * nc:(n_u + 1) * nc]
            terms = []
            for c in range(nc):
                sl = slice(c * chunk, (c + 1) * chunk)
                r_last = c * chunk if rev else (c + 1) * chunk - 1
                g_last = gcum[r_last:r_last + 1, :]
                k_dec = k[sl] * jnp.exp(g_last - gcum[sl])
                a_uw = _mm(a_in[c], xs[c], precise)
                kd_uw = _mm(k_dec, xs[c], precise, _dot_tn)
                lhs = jnp.concatenate([qg[sl] - a_uw[:, DV_A:], kd_uw[:, DV_A:]], axis=0)
                terms.append((lhs, a_uw[:, :DV_A], kd_uw[:, :DV_A], jnp.exp(g_last)))
            all_terms.append(terms)

    states = [s_scr[1 if p[0] else 0, p[1]] for p in prep]
    outs = [[None] * nc for _ in prep]
    for t in range(nc):
        for n_u, p in enumerate(prep):
            c = nc - 1 - t if p[0] else t
            lhs, o_0, q_0, decay = all_terms[n_u][c]
            prod = _mm(lhs, states[n_u], precise)
            outs[n_u][c] = o_0 + prod[:chunk]
            states[n_u] = states[n_u] * decay - prod[chunk:] + q_0
    for n_u, (rev, hh) in enumerate((p[0], p[1]) for p in prep):
        s_scr[1 if rev else 0, hh] = states[n_u]
        (ob_ref if rev else of_ref)[:, hh * LANES:(hh + 1) * LANES] = jnp.concatenate(outs[n_u], axis=0)

    @pl.when(i == n_i - 1)
    def _():
        so_ref[...] = s_scr[...]


def _delta_stage(qkvn, gcol, grow, s0, n_seq, seq_len, row0, rg, into=None, precise=False):
    ng = seq_len // rg
    b0 = row0 // rg

    def fwd_blk(b, i):
        return b0 + b * ng + i

    def bwd_blk(b, i):
        return b0 + b * ng + (ng - 1 - i)

    hpb = DELTA_HEADS_PER_STEP
    nhp = H_A // hpb
    hw = hpb * LANES

    def specs(blk):
        return [pl.BlockSpec((rg, hw), lambda b, h, i: (blk(b, i), h)),
                pl.BlockSpec((rg, hw), lambda b, h, i: (blk(b, i), nhp + h)),
                pl.BlockSpec((rg, hw), lambda b, h, i: (blk(b, i), 2 * nhp + h)),
                pl.BlockSpec((rg, LANES), lambda b, h, i: (blk(b, i), 0)),
                pl.BlockSpec((4 * H_A, rg), lambda b, h, i: (0, blk(b, i)))]

    st_spec = pl.BlockSpec((None, 2, hpb, DK_A, DV_A), lambda b, h, i: (b, 0, h, 0, 0))
    args = [qkvn, qkvn, qkvn, gcol, grow, qkvn, qkvn, qkvn, gcol, grow, s0]
    in_specs = specs(fwd_blk) + specs(bwd_blk) + [st_spec]
    aliases = {}
    if into is not None:
        for k_out, arr in enumerate(into):
            aliases[len(args)] = k_out
            args.append(arr)
            in_specs.append(pl.BlockSpec(memory_space=pl.ANY))
    n = qkvn.shape[0]
    return pl.pallas_call(
        functools.partial(_delta_kernel, chunk=CHUNK, hpb=hpb, n_alias=len(aliases), precise=precise),
        grid=(n_seq, nhp, ng),
        in_specs=in_specs,
        out_specs=[pl.BlockSpec((rg, hw), lambda b, h, i: (fwd_blk(b, i), h)),
                   pl.BlockSpec((rg, hw), lambda b, h, i: (bwd_blk(b, i), h)),
                   st_spec],
        out_shape=[jax.ShapeDtypeStruct((n, H_A * DV_A), F32),
                   jax.ShapeDtypeStruct((n, H_A * DV_A), F32),
                   jax.ShapeDtypeStruct(s0.shape, F32)],
        input_output_aliases=aliases,
        scratch_shapes=[pltpu.VMEM((2, hpb, DK_A, DV_A), F32)],
        compiler_params=_cparams("parallel", "parallel", "arbitrary"), name="gated_delta",
    )(*args)


def _flash_kernel(*refs, n_parts, has_cache, diff, tk, lam_init, aliased, head_lanes):
    it = iter(refs)
    q_refs = [next(it) for _ in range(n_parts)]
    k_refs = [next(it) for _ in range(n_parts)]
    v_ref = next(it)
    kc_refs = [next(it) for _ in range(n_parts)] if has_cache else []
    vc_ref = next(it) if has_cache else None
    lam_ref = next(it) if diff else None
    if aliased:
        next(it)
    o_ref = next(it)
    tq = q_refs[0].shape[0]
    lk = k_refs[0].shape[0]
    if diff:
        lp = lam_ref[...]
        lam = (jnp.exp(jnp.sum(lp[0:1] * lp[1:2], axis=-1, keepdims=True))
               - jnp.exp(jnp.sum(lp[2:3] * lp[3:4], axis=-1, keepdims=True)) + lam_init)
        lane = lax.broadcasted_iota(jnp.int32, (1, LANES), 1)

    def cols(ref, off, rows=None):
        return ref[:, off:off + LANES] if rows is None else ref[rows, off:off + LANES]

    def block(carry, q_cat, ks, vv):
        m, l, acc = carry
        s = _dot_nt(q_cat, ks[0] if n_parts == 1 else jnp.concatenate(ks, axis=1))
        m_new = jnp.maximum(m, jnp.max(s, axis=-1, keepdims=True))
        alpha = jnp.exp2(m - m_new)
        p_ = jnp.exp2(s - m_new)
        l = alpha * l + jnp.sum(p_, axis=-1, keepdims=True)
        acc = alpha * acc + _dot(p_.astype(BF16), vv)
        return m_new, l, acc

    q_cats, carries = [], []
    for q_offs, *_ in head_lanes:
        qs = [cols(r, off) for r, off in zip(q_refs, q_offs)]
        if diff:
            q = qs[0]
            qs = [jnp.concatenate([jnp.where(lane < DQK_B, q, 0), jnp.where(lane >= DQK_B, q, 0)], axis=0)]
        nrow = qs[0].shape[0]
        q_cats.append(qs[0] if n_parts == 1 else jnp.concatenate(qs, axis=1))
        carries.append((jnp.full((nrow, 1), -1e30, F32), jnp.zeros((nrow, 1), F32), jnp.zeros((nrow, LANES), F32)))

    for t in range(lk // tk):
        rows = slice(t * tk, (t + 1) * tk)
        for n_h, (_, k_offs, v_off, _, _, _) in enumerate(head_lanes):
            carries[n_h] = block(carries[n_h], q_cats[n_h], [cols(r, off, rows) for r, off in zip(k_refs, k_offs)],
                                 cols(v_ref, v_off, rows))
    for n_h, (_, _, _, o_off, kc_offs, vc_off) in enumerate(head_lanes):
        carry = carries[n_h]
        if has_cache:
            carry = block(carry, q_cats[n_h], [cols(r, off) for r, off in zip(kc_refs, kc_offs)], cols(vc_ref, vc_off))
        _, l, acc = carry
        o = acc / l
        if diff:
            o = o[:tq] - lam * o[tq:]
        o_ref[:, o_off:o_off + LANES] = o.astype(o_ref.dtype)


def _flash(q_parts, k_parts, v_part, cache, lam, *, n_seq, n_heads, lq, lk, q_row0, k_row0, tq, tk, diff,
           lam_init, out_cols, into=None, heads_per_step=1):
    nqt = lq // tq
    qb0, kb0 = q_row0 // tq, k_row0 // lk
    hps = heads_per_step
    assert n_heads % hps == 0
    args, in_specs = [], []

    def add(arr, cf, n_rows, row_fn):
        span = 1
        while any(cf(g * hps) // span != cf(g * hps + hps - 1) // span for g in range(n_heads // hps)):
            span *= 2
        offs = tuple((cf(j) - cf(0) // span * span) * LANES for j in range(hps))
        for g in range(n_heads // hps):
            assert offs == tuple((cf(g * hps + j) - cf(g * hps) // span * span) * LANES for j in range(hps))
        args.append(arr)
        in_specs.append(pl.BlockSpec((n_rows, span * LANES),
                                     lambda b, hg, t: (row_fn(b, t), cf(hg * hps) // span)))
        return offs

    q_offs = [add(arr, cf, tq, lambda b, t: qb0 + b * nqt + t) for arr, cf in q_parts]
    k_offs = [add(arr, cf, lk, lambda b, t: kb0 + b) for arr, cf in k_parts]
    v_offs = add(v_part[0], v_part[1], lk, lambda b, t: kb0 + b)
    kc_offs, vc_offs = [(0,) * hps] * len(q_parts), (0,) * hps
    if cache is not None:
        c_k, c_v, c_rows = cache
        kc_offs = [add(arr, cf, c_rows, lambda b, t: b) for arr, cf in c_k]
        vc_offs = add(c_v[0], c_v[1], c_rows, lambda b, t: b)
    head_lanes = tuple((tuple(o[j] for o in q_offs), tuple(o[j] for o in k_offs), v_offs[j], j * LANES,
                        tuple(o[j] for o in kc_offs), vc_offs[j]) for j in range(hps))
    grid = (n_seq, n_heads // hps, nqt)
    out_spec = pl.BlockSpec((tq, hps * LANES), lambda b, hg, t: (qb0 + b * nqt + t, hg))
    if diff:
        args.append(lam)
        in_specs.append(pl.BlockSpec(lam.shape, lambda b, h, t: (0, 0)))
    aliases = {}
    if into is not None:
        aliases[len(args)] = 0
        args.append(into)
        in_specs.append(pl.BlockSpec(memory_space=pl.ANY))
    kern = functools.partial(_flash_kernel, n_parts=len(q_parts), has_cache=cache is not None, diff=diff, tk=tk,
                             lam_init=lam_init, aliased=into is not None, head_lanes=head_lanes)
    return pl.pallas_call(
        kern, grid=grid, in_specs=in_specs, out_specs=out_spec,
        out_shape=jax.ShapeDtypeStruct((q_parts[0][0].shape[0], out_cols), F32 if diff else BF16),
        input_output_aliases=aliases,
        compiler_params=_cparams("parallel", "parallel", "arbitrary"), name="flash_attention",
    )(*args)


def _mr_kernel(x_ref, a_ref, w_ref, mods_ref, o_ref, *, rows, gate_idx):
    i = pl.program_id(0)
    d = x_ref.shape[1]
    gate = _mod_chunk(mods_ref, rows.group(i), gate_idx, d)
    o_ref[...] = x_ref[...] + gate * _dot(a_ref[...].astype(BF16), w_ref[...])


def _matmul_residual(x, a, w, mods, rows, gate_idx):
    n, d = x.shape
    tm = rows.tm
    return pl.pallas_call(
        functools.partial(_mr_kernel, rows=rows, gate_idx=gate_idx), grid=(n // tm,),
        in_specs=[pl.BlockSpec((tm, d), lambda i: (i, 0)),
                  pl.BlockSpec((tm, a.shape[1]), lambda i: (i, 0)),
                  pl.BlockSpec(w.shape, lambda i: (0, 0)),
                  pl.BlockSpec(mods.shape, lambda i: (0, 0))],
        out_specs=pl.BlockSpec((tm, d), lambda i: (i, 0)),
        out_shape=jax.ShapeDtypeStruct((n, d), F32),
        compiler_params=_cparams("parallel"), name="matmul_residual",
    )(x, a, w, mods)


def _modulated(x_ref, nw_ref, mods_ref, g, idx, d):
    x = _rmsnorm_rows(x_ref[...], nw_ref[...])
    return x * (1.0 + _mod_chunk(mods_ref, g, idx + 1, d)) + _mod_chunk(mods_ref, g, idx, d)


def _ffn_kernel(x_ref, nw_ref, mods_ref, wg_ref, wu_ref, wd_ref, of_ref, ob_ref, z_ref, od_ref, on_ref, sn_ref,
                wo_ref, o_ref, h_scr, acc_scr, xn_scr, *, rows, mod0, mix_gate, lam_init):
    i, f = pl.program_id(0), pl.program_id(1)
    d = x_ref.shape[1]
    g = rows.group(i)

    @pl.when(f == 0)
    def _():
        oa = of_ref[...] + ob_ref[...]
        z = z_ref[...]
        od = od_ref[...]
        ya, yb = [], []
        for hh in range(H_A):
            sl = slice(hh * LANES, (hh + 1) * LANES)
            ya.append(_rmsnorm_rows(oa[:, sl], on_ref[...]) * _silu(z[:, sl]))
        for hh in range(H_B):
            sl = slice(hh * LANES, (hh + 1) * LANES)
            yb.append(_rmsnorm_rows(od[:, sl], sn_ref[...]) * (1.0 - lam_init))
        y = jnp.concatenate(ya + yb, axis=1).astype(BF16)
        x1 = x_ref[...] + _mod_chunk(mods_ref, g, mix_gate, d) * _dot(y, wo_ref[...])
        xn_scr[...] = x1
        v = _rmsnorm_rows(x1, nw_ref[...])
        h_scr[...] = (v * (1.0 + _mod_chunk(mods_ref, g, mod0 + 1, d)) + _mod_chunk(mods_ref, g, mod0, d)).astype(BF16)
        acc_scr[...] = jnp.zeros_like(acc_scr)

    h = h_scr[...]
    act = _silu(_dot(h, wg_ref[...])) * _dot(h, wu_ref[...])
    acc_scr[...] += _dot(act.astype(BF16), wd_ref[...])

    @pl.when(f == pl.num_programs(1) - 1)
    def _():
        o_ref[...] = xn_scr[...] + _mod_chunk(mods_ref, g, mod0 + 2, d) * acc_scr[...]


def _even_tail(x, o_f, o_b, z, o_d, onorm, subln, w_out, norm_w, mods, w_gu, w_down, rows, mix_gate, mod0, tf,
               lam_init):
    n, d = x.shape
    tm = rows.tm
    ff = w_down.shape[0]
    nf = ff // tf
    hw = H_A * DV_A
    row = lambda c: pl.BlockSpec((tm, c), lambda i, f: (i, 0))
    one = lambda c: pl.BlockSpec((1, c), lambda i, f: (0, 0))
    wbuf = dict(pipeline_mode=pl.Buffered(1)) if nf == 1 else {}
    return pl.pallas_call(
        functools.partial(_ffn_kernel, rows=rows, mod0=mod0, mix_gate=mix_gate, lam_init=lam_init),
        grid=(n // tm, nf),
        in_specs=[row(d), one(d), pl.BlockSpec(mods.shape, lambda i, f: (0, 0)),
                  pl.BlockSpec((d, tf), lambda i, f: (0, f), **wbuf),
                  pl.BlockSpec((d, tf), lambda i, f: (0, nf + f), **wbuf),
                  pl.BlockSpec((tf, d), lambda i, f: (f, 0), **wbuf),
                  row(hw), row(hw), row(hw), row(hw), one(LANES), one(LANES),
                  pl.BlockSpec(w_out.shape, lambda i, f: (0, 0))],
        out_specs=row(d),
        out_shape=jax.ShapeDtypeStruct((n, d), F32),
        scratch_shapes=[pltpu.VMEM((tm, d), BF16), pltpu.VMEM((tm, d), F32), pltpu.VMEM((tm, d), F32)],
        compiler_params=_cparams("parallel", "arbitrary"), name="even_tail",
    )(x, norm_w.reshape(1, d).astype(F32), mods, w_gu, w_gu, w_down, o_f, o_b, z, o_d,
      onorm.reshape(1, LANES).astype(F32), subln.reshape(1, LANES).astype(F32), w_out)


def _moe_kernel(x_ref, nw_ref, mods_ref, rw_ref, wgu_ref, wd_ref, *rest, rows, mod0, n_exp, sb, final):
    if final:
        fw_ref, o_ctx_ref, o_lat_ref, h_scr, g_scr, rk_scr, rkt_scr, acc_scr = rest
    else:
        o_ref, h_scr, g_scr, rk_scr, rkt_scr, acc_scr = rest
    i, e = pl.program_id(0), pl.program_id(1)
    tm, d = x_ref.shape
    g = rows.group(i)
    lane = lax.broadcasted_iota(jnp.int32, (1, LANES), 1)

    @pl.when(e == 0)
    def _():
        h = _modulated(x_ref, nw_ref, mods_ref, g, mod0, d)
        h_scr[...] = h.astype(BF16)
        acc_scr[...] = jnp.zeros_like(acc_scr)
        logits = jnp.where(lane < n_exp, _mm(h, rw_ref[...], True), -1e30)
        pe = jnp.exp(logits - jnp.max(logits, axis=-1, keepdims=True))
        probs = pe / jnp.sum(pe, axis=-1, keepdims=True)
        p1 = jnp.max(probs, axis=-1, keepdims=True)
        i1 = jnp.min(jnp.where(probs == p1, lane, LANES), axis=-1, keepdims=True)
        rest = jnp.where(lane == i1, -1.0, probs)
        p2 = jnp.max(rest, axis=-1, keepdims=True)
        i2 = jnp.min(jnp.where(rest == p2, lane, LANES), axis=-1, keepdims=True)
        den = p1 + p2
        gates = jnp.where(lane == i1, p1 / den, 0.0) + jnp.where(lane == i2, p2 / den, 0.0)
        g_scr[...] = gates
        t_r = lax.broadcasted_iota(jnp.int32, (tm, tm), 0)
        t_c = lax.broadcasted_iota(jnp.int32, (tm, tm), 1)
        earlier = jnp.where(t_r > t_c, 1.0, 0.0).astype(BF16)
        sel = gates > 0.0
        rank = jnp.where(sel, _dot(earlier, jnp.where(sel, 1.0, 0.0).astype(BF16)), -1.0)
        rk_scr[...] = rank
        rkt_scr[...] = jnp.transpose(rank)[:rkt_scr.shape[0], :]

    ff = wd_ref.shape[0]
    g_e = jnp.sum(jnp.where(lane == e, g_scr[...], 0.0), axis=-1, keepdims=True)
    rk_col = jnp.sum(jnp.where(lane == e, rk_scr[...], 0.0), axis=-1, keepdims=True)
    rk_row = rkt_scr[pl.ds(e, 1), :]
    n_rows = (jnp.max(rk_row) + 1.0).astype(jnp.int32)
    g_hi = g_e.astype(BF16).astype(F32)
    g2 = jnp.where(lane == 0, g_hi, jnp.where(lane == 1, g_e - g_hi, 0.0)).astype(BF16)
    def expert_rows(slot0, nr):
        base = slot0.astype(F32)
        r_sub = lax.broadcasted_iota(jnp.int32, (nr, 1), 0).astype(F32)
        r_lane = lax.broadcasted_iota(jnp.int32, (1, nr), 1).astype(F32)
        pick = jnp.where(rk_row == r_sub + base, 1.0, 0.0).astype(BF16)
        xg = _dot(pick, h_scr[...]).astype(BF16)
        gr = _dot(pick, g2)
        gate_r = gr[:, 0:1] + gr[:, 1:2]
        gu = _dot(xg, wgu_ref[...])
        act = _silu(gu[:, :ff]) * gu[:, ff:]
        y = _dot(act.astype(BF16), wd_ref[...])
        put = jnp.where(rk_col == r_lane + base, 1.0, 0.0).astype(BF16)
        acc_scr[...] += _dot(put, (y * gate_r).astype(BF16))

    half, quarter = sb // 2, sb // 4
    rem = n_rows % sb
    n_full = n_rows // sb + jnp.where(rem > half, 1, 0)

    def full_body(jb, carry):
        expert_rows(jb * sb, sb)
        return carry

    lax.fori_loop(0, n_full, full_body, 0)

    @pl.when((rem > quarter) & (rem <= half))
    def _():
        expert_rows(n_full * sb, half)

    @pl.when((rem > 0) & (rem <= quarter))
    def _():
        expert_rows(n_full * sb, quarter)

    def result():
        return x_ref[...] + _mod_chunk(mods_ref, g, mod0 + 2, d) * acc_scr[...]

    last = e == n_exp - 1
    if final:
        @pl.when(last & (i < rows.ctx_tiles))
        def _():
            o_ctx_ref[...] = _rmsnorm_rows(result(), fw_ref[...])

        @pl.when(last & (i >= rows.ctx_tiles))
        def _():
            o_lat_ref[...] = _rmsnorm_rows(result(), fw_ref[...])
    else:
        @pl.when(last)
        def _():
            o_ref[...] = result()


def _moe(x, norm_w, mods, router_w, w_gu, w_down, rows, mod0, final_w=None):
    n, d = x.shape
    tm = rows.tm
    n_exp, _, ff2 = w_gu.shape
    rw = jnp.zeros((d, LANES), F32).at[:, :n_exp].set(router_w.astype(F32))
    final = final_w is not None
    args = [x, norm_w.reshape(1, d).astype(F32), mods, rw, w_gu, w_down]
    in_specs = [pl.BlockSpec((tm, d), lambda i, e: (i, 0)),
                pl.BlockSpec((1, d), lambda i, e: (0, 0)),
                pl.BlockSpec(mods.shape, lambda i, e: (0, 0)),
                pl.BlockSpec((d, LANES), lambda i, e: (0, 0)),
                pl.BlockSpec((None, d, ff2), lambda i, e: (e, 0, 0)),
                pl.BlockSpec((None, ff2 // 2, d), lambda i, e: (e, 0, 0))]
    if final:
        args.append(final_w.reshape(1, d).astype(F32))
        in_specs.append(pl.BlockSpec((1, d), lambda i, e: (0, 0)))
        ct = rows.ctx_tiles
        out_specs = [pl.BlockSpec((tm, d), lambda i, e: (jnp.minimum(i, ct - 1), 0), pipeline_mode=pl.Buffered(1)),
                     pl.BlockSpec((tm, d), lambda i, e: (jnp.maximum(i - ct, 0), 0), pipeline_mode=pl.Buffered(1))]
        out_shape = [jax.ShapeDtypeStruct((ct * tm, d), F32), jax.ShapeDtypeStruct((n - ct * tm, d), F32)]
    else:
        out_specs = pl.BlockSpec((tm, d), lambda i, e: (i, 0))
        out_shape = jax.ShapeDtypeStruct((n, d), F32)
    return pl.pallas_call(
        functools.partial(_moe_kernel, rows=rows, mod0=mod0, n_exp=n_exp, sb=MOE_ROW_BLOCK, final=final),
        grid=(n // tm, n_exp), in_specs=in_specs, out_specs=out_specs, out_shape=out_shape,
        scratch_shapes=[pltpu.VMEM((tm, d), BF16), pltpu.VMEM((tm, LANES), F32), pltpu.VMEM((tm, LANES), F32),
                        pltpu.VMEM((SUBLANES * ((n_exp + SUBLANES - 1) // SUBLANES), tm), F32),
                        pltpu.VMEM((tm, d), F32)],
        compiler_params=_cparams("arbitrary", "arbitrary"), name="moe",
    )(*args)


def kernel(x_prompt, x_sample, state_delta, cache_diff_k, cache_diff_v, cache_mla_ckv, cache_mla_krope, c, c_ctx,
           mod_w, mod_b, norm_mix, norm_ffn, final_norm, ev_w_in, ev_conv_w, ev_a_log, ev_dt_bias, ev_onorm,
           ev_lambda, ev_subln, ev_w_out, ffn_w_gu, ffn_w_down, od_w_in, od_q_norm, od_kv_norm, od_w_uq, od_w_ukv,
           od_w_out, moe_router, moe_w_gu, moe_w_down):
    bp, lp, d = x_prompt.shape
    bs, ls, _ = x_sample.shape
    past = cache_diff_k.shape[3]
    depth = mod_w.shape[0]
    n_p, n_s = bp * lp, bs * ls
    n = n_p + n_s
    assert n_p % ls == 0 and past == lp and depth % 2 == 0

    rows = _Rows(n_p, n_s, ls, _tile(math.gcd(n_p, ls), 512))
    rows_m = _Rows(n_p, n_s, ls, _tile(math.gcd(n_p, ls), 1024))
    rg = _tile(math.gcd(lp, ls), 256)
    cos_t, sin_t = _rope_tables(rows, ls)

    x = jnp.concatenate([x_prompt.reshape(n_p, d), x_sample.reshape(n_s, d)], axis=0)
    n_grp = 1 + bs
    cc = jnp.zeros((2 * SUBLANES * ((n_grp + 15) // 16), d), F32).at[0].set(c_ctx).at[1:n_grp].set(c)
    mods_all = _mods(cc, mod_w, mod_b)

    hq = H_A * DK_A
    hb2 = H_B * 2 * DQK_B
    sm0 = 4 * hq
    qb0 = sm0 + 4 * H_A
    q_scale = (DQK_B ** -0.5) * LOG2E
    mla_scale = ((NOPE_C + ROPE_C) ** -0.5) * LOG2E
    sd_new, dk_new, dv_new, ckv_new, kr_new = [], [], [], [], []

    for layer in range(depth):
        j = layer // 2
        mods = mods_all[layer]
        if layer % 2 == 0:
            lam_init = 0.8 - 0.6 * math.exp(-0.3 * layer)
            w = ev_w_in[j]
            sm = w[:, sm0:qb0].reshape(d, 4, H_A).transpose(0, 2, 1).reshape(d, 4 * H_A)
            w_in = jnp.concatenate([w[:, :sm0], sm, jnp.zeros((d, LANES - 4 * H_A), w.dtype), w[:, qb0:]],
                                   axis=1).astype(BF16)
            qkvn, z_gate, gcol, grow, att, kv_f = _even_proj(
                x, norm_mix[layer], mods, w_in, ev_conv_w[j], ev_a_log[j], ev_dt_bias[j], (cos_t, sin_t), rows,
                3 * hq, hq, hb2, q_scale, lp, ls, n_p)
            o_f, o_b, s_p = _delta_stage(qkvn, gcol, grow, jnp.zeros((bp, 2, H_A, DK_A, DV_A), F32), bp, lp, 0, rg,
                                         precise=True)
            o_f, o_b, _ = _delta_stage(qkvn, gcol, grow, state_delta[:, j].astype(F32), bs, ls, n_p, rg,
                                       into=(o_f, o_b))
            sd_new.append(s_p)
            lam_p = ev_lambda[j].astype(F32)
            nb = hb2 // LANES
            qp, kp, vp = [(att, lambda h: h)], [(att, lambda h: nb + h)], (att, lambda h: 2 * nb + h)
            o_d = _flash(qp, kp, vp, None, lam_p, n_seq=bp, n_heads=H_B, lq=lp, lk=lp, q_row0=0, k_row0=0,
                         tq=_tile(lp, 256), tk=_tile(lp, 512), diff=True, lam_init=lam_init, out_cols=hb2,
                         heads_per_step=H_B)
            ck = cache_diff_k[:, j].transpose(0, 2, 1, 3).reshape(bs * past, H_B * 2 * DQK_B).astype(BF16)
            cv = cache_diff_v[:, j].transpose(0, 2, 1, 3).reshape(bs * past, H_B * DV_B).astype(BF16)
            cache = ([(ck, lambda h: h)], (cv, lambda h: h), past)
            o_d = _flash(qp, kp, vp, cache, lam_p, n_seq=bs, n_heads=H_B, lq=ls, lk=ls, q_row0=n_p, k_row0=n_p,
                         tq=_tile(ls, 512), tk=_tile(ls, 2048), diff=True, lam_init=lam_init, out_cols=hb2, into=o_d,
                         heads_per_step=FLASH_HEADS_PER_STEP)
            dk_new.append(kv_f[:n_p, :hb2].reshape(bp, lp, H_B, 2 * DQK_B).transpose(0, 2, 1, 3))
            dv_new.append(kv_f[:n_p, hb2:].reshape(bp, lp, H_B, DV_B).transpose(0, 2, 1, 3))
            ff = ffn_w_down.shape[1]
            x = _even_tail(x, o_f, o_b, z_gate, o_d, ev_onorm[j], ev_subln[j], ev_w_out[j].astype(BF16),
                           norm_ffn[layer], mods, ffn_w_gu[j].astype(BF16), ffn_w_down[j].astype(BF16), rows, 2, 3,
                           ff, lam_init)
        else:
            n_in = Q_LORA + KV_LORA + ROPE_C
            pad = (-n_in) % LANES
            w_in = jnp.concatenate([od_w_in[j], jnp.zeros((d, pad), od_w_in.dtype)], axis=1).astype(BF16)
            wq = od_w_uq[j].reshape(Q_LORA, H_C, NOPE_C + ROPE_C)
            wq_rope = jnp.concatenate([wq[:, :, NOPE_C:], jnp.zeros((Q_LORA, H_C, LANES - ROPE_C), wq.dtype)], axis=2)
            wq2 = jnp.concatenate([wq[:, :, :NOPE_C].reshape(Q_LORA, H_C * NOPE_C),
                                   wq_rope.reshape(Q_LORA, H_C * LANES)], axis=1).astype(BF16)
            q_all, kv_tok, kr_tok, ckv_n, kr_f = _odd_proj(
                x, norm_mix[layer], mods, w_in, od_q_norm[j], od_kv_norm[j], wq2, od_w_ukv[j].astype(BF16),
                _rope_tables(rows_m, ls), rows_m, mla_scale, H_C * NOPE_C)
            ckv_new.append(ckv_n[:n_p].reshape(bp, lp, KV_LORA))
            kr_new.append(kr_f[:n_p, :ROPE_C].reshape(bp, lp, ROPE_C))
            kv_c = _matmul(cache_mla_ckv[:, j].reshape(bs * past, KV_LORA), od_w_ukv[j].astype(BF16),
                           _tile(bs * past, 512), BF16)
            kr_c = jnp.concatenate([cache_mla_krope[:, j].reshape(bs * past, ROPE_C),
                                    jnp.zeros((bs * past, LANES - ROPE_C), F32)], axis=1).astype(BF16)
            qp = [(q_all, lambda h: h), (q_all, lambda h: H_C + h)]
            kp = [(kv_tok, lambda h: 2 * h), (kr_tok, lambda h: 0)]
            vp = (kv_tok, lambda h: 2 * h + 1)
            o = _flash(qp, kp, vp, None, None, n_seq=bp, n_heads=H_C, lq=lp, lk=lp, q_row0=0, k_row0=0,
                       tq=_tile(lp, 256), tk=_tile(lp, 512), diff=False, lam_init=0.0, out_cols=H_C * V_C,
                       heads_per_step=H_C)
            cache = ([(kv_c, lambda h: 2 * h), (kr_c, lambda h: 0)], (kv_c, lambda h: 2 * h + 1), past)
            o = _flash(qp, kp, vp, cache, None, n_seq=bs, n_heads=H_C, lq=ls, lk=ls, q_row0=n_p, k_row0=n_p,
                       tq=_tile(ls, 1024), tk=_tile(ls, 2048), diff=False, lam_init=0.0, out_cols=H_C * V_C, into=o,
                       heads_per_step=FLASH_HEADS_PER_STEP)
            x = _matmul_residual(x, o, od_w_out[j].astype(BF16), mods, rows, 2)
            x = _moe(x, norm_ffn[layer], mods, moe_router[j], moe_w_gu[j].astype(BF16), moe_w_down[j].astype(BF16),
                     rows_m, 3, final_w=final_norm if layer == depth - 1 else None)

    y_p, y_s = x
    return (y_p.reshape(bp, lp, d), y_s.reshape(bs, ls, d), jnp.stack(sd_new, axis=1),
            jnp.stack(dk_new, axis=1), jnp.stack(dv_new, axis=1), jnp.stack(ckv_new, axis=1),
            jnp.stack(kr_new, axis=1))
```

```python
import functools
import math

import jax
import jax.numpy as jnp
from jax import lax
from jax.experimental import pallas as pl
from jax.experimental.pallas import tpu as pltpu

F32 = jnp.float32
BF16 = jnp.bfloat16

EPS = 1e-6
LOG2E = 1.4426950408889634
GRID_W = 64
ROPE_BASE = 10000.0
H_A, DK_A, DV_A = 4, 128, 128
CONV_K = 3
CHUNK = 64
H_B, DQK_B, DV_B = 4, 64, 128
H_C, NOPE_C, ROPE_C, V_C = 8, 128, 64, 128
Q_LORA, KV_LORA = 384, 256
LANES = 128
SUBLANES = 8
VMEM_LIMIT_BYTES = 56 * 1024 * 1024


def _cparams(*sem):
    return pltpu.CompilerParams(dimension_semantics=sem, vmem_limit_bytes=VMEM_LIMIT_BYTES)


def _dot(a, b):
    return jnp.dot(a, b, preferred_element_type=F32)


def _dot_nt(a, b):
    return lax.dot_general(a, b, (((1,), (1,)), ((), ())), preferred_element_type=F32)


def _dot_tn(a, b):
    return lax.dot_general(a, b, (((0,), (0,)), ((), ())), preferred_element_type=F32)


def _split3(x):
    hi = x.astype(BF16)
    r = x - hi.astype(F32)
    mid = r.astype(BF16)
    lo = (r - mid.astype(F32)).astype(BF16)
    return hi, mid, lo


def _dot_f32(a, b):
    ah, am, al = _split3(a)
    bh, bm, bl = _split3(b)
    return (_dot(ah, bh) + (_dot(ah, bm) + _dot(am, bh))
            + (_dot(am, bm) + _dot(ah, bl) + _dot(al, bh)))


def _silu(x):
    return x * jax.nn.sigmoid(x)


def _tile(n, pref, mult=SUBLANES):
    t = min(n, pref)
    while t > mult and (n % t or t % mult):
        t -= mult
    assert n % t == 0, (n, pref)
    return t


class _Rows:
    def __init__(self, n_ctx, n_lat, lat_len, tm):
        assert n_ctx % tm == 0 and lat_len % tm == 0
        self.tm = tm
        self.n = n_ctx + n_lat
        self.ctx_tiles = n_ctx // tm
        self.seq_tiles = lat_len // tm

    def group(self, i):
        return jnp.where(i < self.ctx_tiles, 0, 1 + jnp.maximum(i - self.ctx_tiles, 0) // self.seq_tiles)

    def pos_block(self, i):
        return jnp.where(i < self.ctx_tiles, 0, 1 + jnp.maximum(i - self.ctx_tiles, 0) % self.seq_tiles)


def _mod_kernel(c_ref, w_ref, b_ref, o_ref):
    o_ref[...] = _dot_f32(_silu(c_ref[...]), w_ref[...]) + b_ref[...]


def _mods(cc, mod_w, mod_b):
    depth, d, n6 = mod_w.shape
    tn = _tile(n6, 1536, LANES)
    return pl.pallas_call(
        _mod_kernel,
        grid=(depth, n6 // tn),
        in_specs=[pl.BlockSpec(cc.shape, lambda l, j: (0, 0)),
                  pl.BlockSpec((None, d, tn), lambda l, j: (l, 0, j)),
                  pl.BlockSpec((None, 1, tn), lambda l, j: (l, 0, j))],
        out_specs=pl.BlockSpec((None, cc.shape[0], tn), lambda l, j: (l, 0, j)),
        out_shape=jax.ShapeDtypeStruct((depth, cc.shape[0], n6), F32),
        compiler_params=_cparams("parallel", "parallel"),
        name="mods",
    )(cc, mod_w, mod_b.reshape(depth, 1, n6))


def _mod_chunk(mods_ref, g, idx, d):
    return mods_ref[pl.ds(g, 1), idx * d:(idx + 1) * d]


def _rmsnorm_rows(x, w):
    ms = jnp.mean(x * x, axis=-1, keepdims=True)
    return x * lax.rsqrt(ms + EPS) * w


def _rope(x, cos, sin_signed):
    lane = lax.broadcasted_iota(jnp.int32, (1, LANES), 1)
    low = (lane % 32) < 16
    outs = []
    for c in range(x.shape[1] // LANES):
        xc = x[:, c * LANES:(c + 1) * LANES]
        fwd = pltpu.roll(xc, LANES - 16, axis=1)
        bwd = pltpu.roll(xc, 16, axis=1)
        outs.append(xc * cos + jnp.where(low, fwd, bwd) * sin_signed)
    return outs[0] if len(outs) == 1 else jnp.concatenate(outs, axis=1)


def _rope_tables(rows, lat_len):
    pos = jnp.arange(lat_len)
    r = (pos // GRID_W).astype(F32)
    c = (pos % GRID_W).astype(F32)
    inv = ROPE_BASE ** (-jnp.arange(0, 32, 2, dtype=F32) / 32)
    ang_r = r[:, None] * inv[None, :]
    ang_c = c[:, None] * inv[None, :]
    ang = jnp.concatenate([ang_r, ang_r, ang_c, ang_c] * 2, axis=-1)
    sign = jnp.where((jnp.arange(LANES) % 32) < 16, -1.0, 1.0).astype(F32)
    cos = jnp.concatenate([jnp.ones((rows.tm, LANES), F32), jnp.cos(ang)], axis=0)
    sin = jnp.concatenate([jnp.zeros((rows.tm, LANES), F32), jnp.sin(ang) * sign[None, :]], axis=0)
    return cos, sin


def _mm_kernel(x_ref, w_ref, o_ref):
    o_ref[...] = _dot(x_ref[...].astype(BF16), w_ref[...]).astype(o_ref.dtype)


def _matmul(x, w, tm, out_dtype):
    n, k = x.shape
    return pl.pallas_call(
        _mm_kernel, grid=(n // tm,),
        in_specs=[pl.BlockSpec((tm, k), lambda i: (i, 0)), pl.BlockSpec(w.shape, lambda i: (0, 0))],
        out_specs=pl.BlockSpec((tm, w.shape[1]), lambda i: (i, 0)),
        out_shape=jax.ShapeDtypeStruct((n, w.shape[1]), out_dtype),
        compiler_params=_cparams("parallel"), name="matmul",
    )(x, w)


PROJ_CHUNK = 512


def _pos_in_seq(r, length):
    return r & (length - 1) if length & (length - 1) == 0 else lax.rem(r, length)


def _even_proj_kernel(x_ref, xp_ref, xn_ref, nw_ref, mods_ref, w_ref, cw_ref, alog_ref, dt_ref, cos_ref, sin_ref,
                      qkv_ref, z_ref, gc_ref, gr_ref, att_ref, kv_ref,
                      *, rows, n_conv, n_z, hb2, q_scale, ctx_len, lat_len, n_ctx_rows):
    i = pl.program_id(0)
    tm, d = x_ref.shape
    g = rows.group(i)

    def modulated(ref):
        v = _rmsnorm_rows(ref[...], nw_ref[...])
        return (v * (1.0 + _mod_chunk(mods_ref, g, 1, d)) + _mod_chunk(mods_ref, g, 0, d)).astype(BF16)

    h = modulated(x_ref)
    halo = xp_ref.shape[0]
    h_ext = jnp.concatenate([h, modulated(xp_ref), modulated(xn_ref)], axis=0)
    p_ext = jnp.concatenate([_dot(h_ext, w_ref[:, c0:min(c0 + PROJ_CHUNK, n_conv)])
                             for c0 in range(0, n_conv, PROJ_CHUNK)], axis=1)
    p = p_ext[:tm]
    p_before = p_ext[tm + halo - 1:tm + halo, :]
    p_after = p_ext[tm + halo:tm + halo + 1, :]
    row = lax.broadcasted_iota(jnp.int32, (tm, 1), 0)
    r_glob = row + i * tm
    in_ctx = i < rows.ctx_tiles
    pos = jnp.where(in_ctx, _pos_in_seq(r_glob, ctx_len), _pos_in_seq(r_glob - n_ctx_rows, lat_len))
    seq_last = jnp.where(in_ctx, ctx_len - 1, lat_len - 1)
    x_prev = jnp.where(pos == 0, 0.0, jnp.where(row == 0, p_before, pltpu.roll(p, 1, axis=0)))
    x_next = jnp.where(pos == seq_last, 0.0, jnp.where(row == tm - 1, p_after, pltpu.roll(p, tm - 1, axis=0)))
    cw = cw_ref[...]
    y = _silu(x_prev * cw[0:1, :] + p * cw[1:2, :] + x_next * cw[2:3, :])
    n_qk = 2 * H_A
    for hh in range(n_conv // LANES):
        yh = y[:, hh * LANES:(hh + 1) * LANES]
        if hh < n_qk:
            yh = yh * lax.rsqrt(jnp.sum(yh * yh, axis=-1, keepdims=True) + EPS)
            if hh < n_qk // 2:
                yh = yh * (DK_A ** -0.5)
        qkv_ref[:, hh * LANES:(hh + 1) * LANES] = yh
    z_ref[...] = _dot(h, w_ref[:, n_conv:n_conv + n_z])
    s = _dot(h, w_ref[:, n_conv + n_z:n_conv + n_z + LANES])
    lane = lax.broadcasted_iota(jnp.int32, (1, LANES), 1)
    zg = s + dt_ref[...]
    softplus = jnp.maximum(zg, 0.0) + jnp.log(1.0 + jnp.exp(-jnp.abs(zg)))
    gates = jnp.where((lane % 4) < 2, -jnp.exp(alog_ref[...]) * softplus, jax.nn.sigmoid(s))
    gc_ref[...] = gates
    gr_ref[...] = jnp.transpose(gates)[:gr_ref.shape[0], :]
    n_main = n_conv + n_z + LANES
    cos, sin = cos_ref[...], sin_ref[...]
    for part in range(3):
        for c0 in range(0, hb2, PROJ_CHUNK):
            c1 = min(c0 + PROJ_CHUNK, hb2)
            y = _dot(h, w_ref[:, n_main + part * hb2 + c0:n_main + part * hb2 + c1])
            if part < 2:
                y = _rope(y, cos, sin)
            if part == 0:
                y = y * q_scale
            else:
                kv_ref[:, (part - 1) * hb2 + c0:(part - 1) * hb2 + c1] = y
            att_ref[:, part * hb2 + c0:part * hb2 + c1] = y.astype(BF16)


def _even_proj(x, norm_w, mods, w, conv_w, a_log, dt_bias, rope, rows, n_conv, n_z, hb2, q_scale, ctx_len,
               lat_len, n_ctx_rows):
    n, d = x.shape
    tm = rows.tm
    assert conv_w.shape[0] == CONV_K
    halo = 2 * SUBLANES
    hb = tm // halo
    alog_row = jnp.zeros((1, LANES), F32).at[0, :4 * H_A].set(
        jnp.stack([a_log[0], a_log[1], a_log[0], a_log[1]], axis=-1).reshape(-1).astype(F32))
    dt_row = jnp.zeros((1, LANES), F32).at[0, :4 * H_A].set(
        jnp.stack([dt_bias[0], dt_bias[1], jnp.zeros_like(dt_bias[0]), jnp.zeros_like(dt_bias[0])],
                  axis=-1).reshape(-1).astype(F32))
    kern = functools.partial(_even_proj_kernel, rows=rows, n_conv=n_conv, n_z=n_z, hb2=hb2, q_scale=q_scale,
                             ctx_len=ctx_len, lat_len=lat_len, n_ctx_rows=n_ctx_rows)
    tab = pl.BlockSpec((tm, LANES), lambda i: (rows.pos_block(i), 0))
    full = lambda a: pl.BlockSpec(a.shape, lambda i: (0, 0))
    nw = norm_w.reshape(1, d).astype(F32)
    cw = conv_w.astype(F32)
    outs = [(n_conv, F32), (n_z, F32), (LANES, F32), None, (3 * hb2, BF16), (2 * hb2, F32)]
    out_specs = [pl.BlockSpec((tm, c[0]), lambda i: (i, 0)) if c else pl.BlockSpec((4 * H_A, tm), lambda i: (0, i))
                 for c in outs]
    out_shape = [jax.ShapeDtypeStruct((n, c[0]), c[1]) if c else jax.ShapeDtypeStruct((4 * H_A, n), F32)
                 for c in outs]
    return pl.pallas_call(
        kern, grid=(n // tm,),
        in_specs=[pl.BlockSpec((tm, d), lambda i: (i, 0)),
                  pl.BlockSpec((halo, d), lambda i: (jnp.maximum(i * hb - 1, 0), 0)),
                  pl.BlockSpec((halo, d), lambda i: (jnp.minimum((i + 1) * hb, n // halo - 1), 0)),
                  full(nw), full(mods), full(w), full(cw), full(alog_row), full(dt_row), tab, tab],
        out_specs=out_specs, out_shape=out_shape,
        compiler_params=_cparams("parallel"), name="even_proj",
    )(x, x, x, nw, mods, w, cw, alog_row, dt_row, *rope)


def _odd_proj_kernel(x_ref, nw_ref, mods_ref, w_ref, qn_ref, kvn_ref, wq_ref, wkv_ref, cos_ref, sin_ref,
                     q_ref, kv_ref, kr_ref, ckv_ref, krf_ref, *, rows, scale, n_nope):
    i = pl.program_id(0)
    d = x_ref.shape[1]
    h = _modulated(x_ref, nw_ref, mods_ref, rows.group(i), 0, d).astype(BF16)
    p = _dot(h, w_ref[...])
    cos, sin = cos_ref[...], sin_ref[...]
    cq = _rmsnorm_rows(p[:, :Q_LORA], qn_ref[...]).astype(BF16)
    for c0 in range(0, wq_ref.shape[1], PROJ_CHUNK):
        y = _dot(cq, wq_ref[:, c0:c0 + PROJ_CHUNK])
        if c0 >= n_nope:
            y = _rope(y, cos, sin)
        q_ref[:, c0:c0 + PROJ_CHUNK] = (y * scale).astype(BF16)
    ckv = _rmsnorm_rows(p[:, Q_LORA:Q_LORA + KV_LORA], kvn_ref[...])
    ckv_ref[...] = ckv
    ckv16 = ckv.astype(BF16)
    for c0 in range(0, wkv_ref.shape[1], PROJ_CHUNK):
        kv_ref[:, c0:c0 + PROJ_CHUNK] = _dot(ckv16, wkv_ref[:, c0:c0 + PROJ_CHUNK]).astype(BF16)
    kr = _rope(p[:, Q_LORA + KV_LORA:], cos, sin)
    krf_ref[...] = kr
    kr_ref[...] = kr.astype(BF16)


def _odd_proj(x, norm_w, mods, w_in, q_norm, kv_norm, wq, wkv, rope, rows, scale, n_nope):
    n, d = x.shape
    tm = rows.tm
    tab = pl.BlockSpec((tm, LANES), lambda i: (rows.pos_block(i), 0))
    full = lambda a: pl.BlockSpec(a.shape, lambda i: (0, 0))
    qn = q_norm.reshape(1, -1).astype(F32)
    kvn = kv_norm.reshape(1, -1).astype(F32)
    nw = norm_w.reshape(1, d).astype(F32)
    outs = [(wq.shape[1], BF16), (wkv.shape[1], BF16), (LANES, BF16), (KV_LORA, F32), (LANES, F32)]
    return pl.pallas_call(
        functools.partial(_odd_proj_kernel, rows=rows, scale=scale, n_nope=n_nope), grid=(n // tm,),
        in_specs=[pl.BlockSpec((tm, d), lambda i: (i, 0)), full(nw), full(mods), full(w_in), full(qn), full(kvn),
                  full(wq), full(wkv), tab, tab],
        out_specs=[pl.BlockSpec((tm, c), lambda i: (i, 0)) for c, _ in outs],
        out_shape=[jax.ShapeDtypeStruct((n, c), dt) for c, dt in outs],
        compiler_params=_cparams("parallel"), name="odd_proj",
    )(x, nw, mods, w_in, qn, kvn, wq, wkv, *rope)


def _split2(x):
    hi = x.astype(BF16)
    return hi, (x - hi.astype(F32)).astype(BF16)


def _mm(a, b, precise, dot=_dot):
    if not precise:
        return dot(a.astype(BF16), b.astype(BF16))
    ah, al = _split2(a)
    bh, bl = _split2(b)
    return dot(ah, bh) + (dot(ah, bl) + dot(al, bh))


SOLVE_BLOCK = SUBLANES
MOE_ROW_BLOCK = 256
FLASH_HEADS_PER_STEP = 2
DELTA_HEADS_PER_STEP = 4


def _tri_solve_many(ms, rs, revs):
    c, width = rs[0].shape
    blk = SOLVE_BLOCK
    nb, gpb, ng = c // blk, blk // SUBLANES, c // SUBLANES
    col = lax.broadcasted_iota(jnp.int32, (blk, c), 1)
    mgs = [[m[SUBLANES * g:SUBLANES * (g + 1), :] for g in range(ng)] for m in ms]
    xgs = [[r[SUBLANES * g:SUBLANES * (g + 1), :] for g in range(ng)] for r in rs]
    zero_blk = jnp.zeros((blk, width), BF16)
    fin_h = [[zero_blk] * nb for _ in ms]
    fin_l = [[zero_blk] * nb for _ in ms]
    for bi in range(nb):
        blocks = [nb - 1 - bi if rev else bi for rev in revs]
        if bi > 0:
            for s, (rev, b) in enumerate(zip(revs, blocks)):
                done = (col >= (b + 1) * blk) if rev else (col < b * blk)
                lh, ll = _split2(jnp.where(done, ms[s][b * blk:(b + 1) * blk, :], 0.0))
                xh = jnp.concatenate(fin_h[s], axis=0)
                xl = jnp.concatenate(fin_l[s], axis=0)
                upd = _dot(lh, xh) + (_dot(lh, xl) + _dot(ll, xh))
                for gg in range(gpb):
                    g = b * gpb + gg
                    xgs[s][g] = xgs[s][g] - upd[SUBLANES * gg:SUBLANES * (gg + 1), :]
        for t in range(blk - 1):
            for s, (rev, b) in enumerate(zip(revs, blocks)):
                j = b * blk + (blk - 1 - t if rev else t)
                xj = xgs[s][j // SUBLANES][j % SUBLANES:j % SUBLANES + 1, :]
                groups = range(b * gpb, (j - 1) // SUBLANES + 1) if rev else range((j + 1) // SUBLANES, (b + 1) * gpb)
                for g in groups:
                    xgs[s][g] = xgs[s][g] - mgs[s][g][:, j:j + 1] * xj
        if bi < nb - 1:
            for s, b in enumerate(blocks):
                fin_h[s][b], fin_l[s][b] = _split2(jnp.concatenate(xgs[s][b * gpb:(b + 1) * gpb], axis=0))
    return [jnp.concatenate(xg, axis=0) for xg in xgs]


def _delta_kernel(qf, kf, vf, gcf, grf, qb, kb, vb, gcb, grb, s0_ref, *rest, chunk, hpb, n_alias, precise):
    of_ref, ob_ref, so_ref, s_scr = rest[n_alias:]
    h0 = pl.program_id(1) * hpb
    i = pl.program_id(2)
    n_i = pl.num_programs(2)

    @pl.when(i == 0)
    def _():
        s_scr[...] = s0_ref[...]

    rg = qf.shape[0]
    nc = rg // chunk
    rowi = lax.broadcasted_iota(jnp.int32, (rg, rg), 0)
    coli = lax.broadcasted_iota(jnp.int32, (rg, rg), 1)
    same = (rowi // chunk) == (coli // chunk)
    lane = lax.broadcasted_iota(jnp.int32, (1, LANES), 1)
    r_c = lax.broadcasted_iota(jnp.int32, (chunk, chunk), 0)
    c_c = lax.broadcasted_iota(jnp.int32, (chunk, chunk), 1)

    prep = []
    for d, (q_ref, k_ref, v_ref, gc_ref, gr_ref) in enumerate(((qf, kf, vf, gcf, grf), (qb, kb, vb, gcb, grb))):
        rev = d == 1
        incl_big = same & ((rowi <= coli) if rev else (rowi >= coli))
        lm = jnp.where(incl_big, 1.0, 0.0).astype(BF16)
        gcols = gc_ref[...]
        grows = gr_ref[...]
        gh, gm, gl_ = _split3(gcols)
        cum_c = _dot(lm, gh) + _dot(lm, gm) + _dot(lm, gl_)
        th, tm_, tl = _split3(grows)
        cum_r = _dot_nt(th, lm) + _dot_nt(tm_, lm) + _dot_nt(tl, lm)

        def col(a, ln):
            return jnp.sum(jnp.where(lane == ln, a, 0.0), axis=1, keepdims=True)

        sub = lax.broadcasted_iota(jnp.int32, (cum_r.shape[0], 1), 0)
        incl = (r_c <= c_c) if rev else (r_c >= c_c)
        strict = (r_c < c_c) if rev else (r_c > c_c)
        for hh in range(hpb):
            h = h0 + hh
            hs = slice(hh * LANES, (hh + 1) * LANES)
            gcum = col(cum_c, 4 * h + d)
            beta = col(gcols, 4 * h + 2 + d)
            grow = jnp.sum(jnp.where(sub == 4 * h + d, cum_r, 0.0), axis=0, keepdims=True)
            q = q_ref[:, hs]
            k = k_ref[:, hs]
            v = v_ref[:, hs]
            kbeta = k * beta
            eg = jnp.exp(gcum)
            rhs = jnp.concatenate([v * beta, kbeta * eg], axis=1)
            ms, rs, a_in = [], [], []
            for c in range(nc):
                sl = slice(c * chunk, (c + 1) * chunk)
                e = jnp.exp(jnp.where(incl, gcum[sl] - grow[:, sl], 0.0))
                kk = _mm(kbeta[sl], k[sl], precise, _dot_nt)
                qk = _mm(q[sl], k[sl], precise, _dot_nt)
                a_in.append(jnp.where(incl, qk * e, 0.0))
                ms.append(jnp.where(strict, kk * e, 0.0))
                rs.append(rhs[sl])
            prep.append((rev, hh, ms, rs, a_in, q * eg, k, gcum))

    all_terms = []
    half = len(prep) // 2
    for wave in (prep[:half], prep[half:]) if half else (prep,):
        sols = _tri_solve_many([m for p in wave for m in p[2]], [r for p in wave for r in p[3]],
                               [p[0] for p in wave for _ in range(nc)])
        for n_u, (rev, hh, _, _, a_in, qg, k, gcum) in enumerate(wave):
            xs = sols[n_u * nc:(n_u + 1) * nc]
            terms = []
            for c in range(nc):
                sl = slice(c * chunk, (c + 1) * chunk)
                r_last = c * chunk if rev else (c + 1) * chunk - 1
                g_last = gcum[r_last:r_last + 1, :]
                k_dec = k[sl] * jnp.exp(g_last - gcum[sl])
                a_uw = _mm(a_in[c], xs[c], precise)
                kd_uw = _mm(k_dec, xs[c], precise, _dot_tn)
                lhs = jnp.concatenate([qg[sl] - a_uw[:, DV_A:], kd_uw[:, DV_A:]], axis=0)
                terms.append((lhs, a_uw[:, :DV_A], kd_uw[:, :DV_A], jnp.exp(g_last)))
            all_terms.append(terms)

    states = [s_scr[1 if p[0] else 0, p[1]] for p in prep]
    outs = [[None] * nc for _ in prep]
    for t in range(nc):
        for n_u, p in enumerate(prep):
            c = nc - 1 - t if p[0] else t
            lhs, o_0, q_0, decay = all_terms[n_u][c]
            prod = _mm(lhs, states[n_u], precise)
            outs[n_u][c] = o_0 + prod[:chunk]
            states[n_u] = states[n_u] * decay - prod[chunk:] + q_0
    for n_u, (rev, hh) in enumerate((p[0], p[1]) for p in prep):
        s_scr[1 if rev else 0, hh] = states[n_u]
        (ob_ref if rev else of_ref)[:, hh * LANES:(hh + 1) * LANES] = jnp.concatenate(outs[n_u], axis=0)

    @pl.when(i == n_i - 1)
    def _():
        so_ref[...] = s_scr[...]


def _delta_stage(qkvn, gcol, grow, s0, n_seq, seq_len, row0, rg, into=None, precise=False):
    ng = seq_len // rg
    b0 = row0 // rg

    def fwd_blk(b, i):
        return b0 + b * ng + i

    def bwd_blk(b, i):
        return b0 + b * ng + (ng - 1 - i)

    hpb = DELTA_HEADS_PER_STEP
    nhp = H_A // hpb
    hw = hpb * LANES

    def specs(blk):
        return [pl.BlockSpec((rg, hw), lambda b, h, i: (blk(b, i), h)),
                pl.BlockSpec((rg, hw), lambda b, h, i: (blk(b, i), nhp + h)),
                pl.BlockSpec((rg, hw), lambda b, h, i: (blk(b, i), 2 * nhp + h)),
                pl.BlockSpec((rg, LANES), lambda b, h, i: (blk(b, i), 0)),
                pl.BlockSpec((4 * H_A, rg), lambda b, h, i: (0, blk(b, i)))]

    st_spec = pl.BlockSpec((None, 2, hpb, DK_A, DV_A), lambda b, h, i: (b, 0, h, 0, 0))
    args = [qkvn, qkvn, qkvn, gcol, grow, qkvn, qkvn, qkvn, gcol, grow, s0]
    in_specs = specs(fwd_blk) + specs(bwd_blk) + [st_spec]
    aliases = {}
    if into is not None:
        for k_out, arr in enumerate(into):
            aliases[len(args)] = k_out
            args.append(arr)
            in_specs.append(pl.BlockSpec(memory_space=pl.ANY))
    n = qkvn.shape[0]
    return pl.pallas_call(
        functools.partial(_delta_kernel, chunk=CHUNK, hpb=hpb, n_alias=len(aliases), precise=precise),
        grid=(n_seq, nhp, ng),
        in_specs=in_specs,
        out_specs=[pl.BlockSpec((rg, hw), lambda b, h, i: (fwd_blk(b, i), h)),
                   pl.BlockSpec((rg, hw), lambda b, h, i: (bwd_blk(b, i), h)),
                   st_spec],
        out_shape=[jax.ShapeDtypeStruct((n, H_A * DV_A), F32),
                   jax.ShapeDtypeStruct((n, H_A * DV_A), F32),
                   jax.ShapeDtypeStruct(s0.shape, F32)],
        input_output_aliases=aliases,
        scratch_shapes=[pltpu.VMEM((2, hpb, DK_A, DV_A), F32)],
        compiler_params=_cparams("parallel", "parallel", "arbitrary"), name="gated_delta",
    )(*args)


def _flash_kernel(*refs, n_parts, has_cache, diff, tk, lam_init, aliased, head_lanes):
    it = iter(refs)
    q_refs = [next(it) for _ in range(n_parts)]
    k_refs = [next(it) for _ in range(n_parts)]
    v_ref = next(it)
    kc_refs = [next(it) for _ in range(n_parts)] if has_cache else []
    vc_ref = next(it) if has_cache else None
    lam_ref = next(it) if diff else None
    if aliased:
        next(it)
    o_ref = next(it)
    tq = q_refs[0].shape[0]
    lk = k_refs[0].shape[0]
    if diff:
        lp = lam_ref[...]
        lam = (jnp.exp(jnp.sum(lp[0:1] * lp[1:2], axis=-1, keepdims=True))
               - jnp.exp(jnp.sum(lp[2:3] * lp[3:4], axis=-1, keepdims=True)) + lam_init)
        lane = lax.broadcasted_iota(jnp.int32, (1, LANES), 1)

    def cols(ref, off, rows=None):
        return ref[:, off:off + LANES] if rows is None else ref[rows, off:off + LANES]

    def block(carry, q_cat, ks, vv):
        s = _dot_nt(q_cat, ks[0] if n_parts == 1 else jnp.concatenate(ks, axis=1))
        if carry is None:
            m_new = jnp.max(s, axis=-1, keepdims=True)
            p_ = jnp.exp2(s - m_new)
            return m_new, jnp.sum(p_, axis=-1, keepdims=True), _dot(p_.astype(BF16), vv)
        m, l, acc = carry
        m_new = jnp.maximum(m, jnp.max(s, axis=-1, keepdims=True))
        alpha = jnp.exp2(m - m_new)
        p_ = jnp.exp2(s - m_new)
        l = alpha * l + jnp.sum(p_, axis=-1, keepdims=True)
        acc = alpha * acc + _dot(p_.astype(BF16), vv)
        return m_new, l, acc

    q_cats, carries = [], []
    for q_offs, *_ in head_lanes:
        qs = [cols(r, off) for r, off in zip(q_refs, q_offs)]
        if diff:
            q = qs[0]
            qs = [jnp.concatenate([jnp.where(lane < DQK_B, q, 0), jnp.where(lane >= DQK_B, q, 0)], axis=0)]
        q_cats.append(qs[0] if n_parts == 1 else jnp.concatenate(qs, axis=1))
        carries.append(None)

    for t in range(lk // tk):
        rows = slice(t * tk, (t + 1) * tk)
        for n_h, (_, k_offs, v_off, _, _, _) in enumerate(head_lanes):
            carries[n_h] = block(carries[n_h], q_cats[n_h], [cols(r, off, rows) for r, off in zip(k_refs, k_offs)],
                                 cols(v_ref, v_off, rows))
    for n_h, (_, _, _, o_off, kc_offs, vc_off) in enumerate(head_lanes):
        carry = carries[n_h]
        if has_cache:
            carry = block(carry, q_cats[n_h], [cols(r, off) for r, off in zip(kc_refs, kc_offs)], cols(vc_ref, vc_off))
        _, l, acc = carry
        o = acc / l
        if diff:
            o = o[:tq] - lam * o[tq:]
        o_ref[:, o_off:o_off + LANES] = o.astype(o_ref.dtype)


def _flash(q_parts, k_parts, v_part, cache, lam, *, n_seq, n_heads, lq, lk, q_row0, k_row0, tq, tk, diff,
           lam_init, out_cols, into=None, heads_per_step=1):
    nqt = lq // tq
    qb0, kb0 = q_row0 // tq, k_row0 // lk
    hps = heads_per_step
    assert n_heads % hps == 0
    args, in_specs = [], []

    def add(arr, cf, n_rows, row_fn):
        span = 1
        while any(cf(g * hps) // span != cf(g * hps + hps - 1) // span for g in range(n_heads // hps)):
            span *= 2
        offs = tuple((cf(j) - cf(0) // span * span) * LANES for j in range(hps))
        for g in range(n_heads // hps):
            assert offs == tuple((cf(g * hps + j) - cf(g * hps) // span * span) * LANES for j in range(hps))
        args.append(arr)
        in_specs.append(pl.BlockSpec((n_rows, span * LANES),
                                     lambda b, hg, t: (row_fn(b, t), cf(hg * hps) // span)))
        return offs

    q_offs = [add(arr, cf, tq, lambda b, t: qb0 + b * nqt + t) for arr, cf in q_parts]
    k_offs = [add(arr, cf, lk, lambda b, t: kb0 + b) for arr, cf in k_parts]
    v_offs = add(v_part[0], v_part[1], lk, lambda b, t: kb0 + b)
    kc_offs, vc_offs = [(0,) * hps] * len(q_parts), (0,) * hps
    if cache is not None:
        c_k, c_v, c_rows = cache
        kc_offs = [add(arr, cf, c_rows, lambda b, t: b) for arr, cf in c_k]
        vc_offs = add(c_v[0], c_v[1], c_rows, lambda b, t: b)
    head_lanes = tuple((tuple(o[j] for o in q_offs), tuple(o[j] for o in k_offs), v_offs[j], j * LANES,
                        tuple(o[j] for o in kc_offs), vc_offs[j]) for j in range(hps))
    grid = (n_seq, n_heads // hps, nqt)
    out_spec = pl.BlockSpec((tq, hps * LANES), lambda b, hg, t: (qb0 + b * nqt + t, hg))
    if diff:
        args.append(lam)
        in_specs.append(pl.BlockSpec(lam.shape, lambda b, h, t: (0, 0)))
    aliases = {}
    if into is not None:
        aliases[len(args)] = 0
        args.append(into)
        in_specs.append(pl.BlockSpec(memory_space=pl.ANY))
    kern = functools.partial(_flash_kernel, n_parts=len(q_parts), has_cache=cache is not None, diff=diff, tk=tk,
                             lam_init=lam_init, aliased=into is not None, head_lanes=head_lanes)
    return pl.pallas_call(
        kern, grid=grid, in_specs=in_specs, out_specs=out_spec,
        out_shape=jax.ShapeDtypeStruct((q_parts[0][0].shape[0], out_cols), F32 if diff else BF16),
        input_output_aliases=aliases,
        compiler_params=_cparams("parallel", "parallel", "arbitrary"), name="flash_attention",
    )(*args)


def _mr_kernel(x_ref, a_ref, w_ref, mods_ref, o_ref, *, rows, gate_idx):
    i = pl.program_id(0)
    d = x_ref.shape[1]
    gate = _mod_chunk(mods_ref, rows.group(i), gate_idx, d)
    o_ref[...] = x_ref[...] + gate * _dot(a_ref[...].astype(BF16), w_ref[...])


def _matmul_residual(x, a, w, mods, rows, gate_idx):
    n, d = x.shape
    tm = rows.tm
    return pl.pallas_call(
        functools.partial(_mr_kernel, rows=rows, gate_idx=gate_idx), grid=(n // tm,),
        in_specs=[pl.BlockSpec((tm, d), lambda i: (i, 0)),
                  pl.BlockSpec((tm, a.shape[1]), lambda i: (i, 0)),
                  pl.BlockSpec(w.shape, lambda i: (0, 0)),
                  pl.BlockSpec(mods.shape, lambda i: (0, 0))],
        out_specs=pl.BlockSpec((tm, d), lambda i: (i, 0)),
        out_shape=jax.ShapeDtypeStruct((n, d), F32),
        compiler_params=_cparams("parallel"), name="matmul_residual",
    )(x, a, w, mods)


def _modulated(x_ref, nw_ref, mods_ref, g, idx, d):
    x = _rmsnorm_rows(x_ref[...], nw_ref[...])
    return x * (1.0 + _mod_chunk(mods_ref, g, idx + 1, d)) + _mod_chunk(mods_ref, g, idx, d)


def _ffn_kernel(x_ref, nw_ref, mods_ref, wg_ref, wu_ref, wd_ref, of_ref, ob_ref, z_ref, od_ref, on_ref, sn_ref,
                wo_ref, o_ref, h_scr, acc_scr, xn_scr, *, rows, mod0, mix_gate, lam_init):
    i, f = pl.program_id(0), pl.program_id(1)
    d = x_ref.shape[1]
    g = rows.group(i)

    @pl.when(f == 0)
    def _():
        oa = of_ref[...] + ob_ref[...]
        z = z_ref[...]
        od = od_ref[...]
        ya, yb = [], []
        for hh in range(H_A):
            sl = slice(hh * LANES, (hh + 1) * LANES)
            ya.append(_rmsnorm_rows(oa[:, sl], on_ref[...]) * _silu(z[:, sl]))
        for hh in range(H_B):
            sl = slice(hh * LANES, (hh + 1) * LANES)
            yb.append(_rmsnorm_rows(od[:, sl], sn_ref[...]) * (1.0 - lam_init))
        y = jnp.concatenate(ya + yb, axis=1).astype(BF16)
        x1 = x_ref[...] + _mod_chunk(mods_ref, g, mix_gate, d) * _dot(y, wo_ref[...])
        xn_scr[...] = x1
        v = _rmsnorm_rows(x1, nw_ref[...])
        h_scr[...] = (v * (1.0 + _mod_chunk(mods_ref, g, mod0 + 1, d)) + _mod_chunk(mods_ref, g, mod0, d)).astype(BF16)
        acc_scr[...] = jnp.zeros_like(acc_scr)

    h = h_scr[...]
    act = _silu(_dot(h, wg_ref[...])) * _dot(h, wu_ref[...])
    acc_scr[...] += _dot(act.astype(BF16), wd_ref[...])

    @pl.when(f == pl.num_programs(1) - 1)
    def _():
        o_ref[...] = xn_scr[...] + _mod_chunk(mods_ref, g, mod0 + 2, d) * acc_scr[...]


def _even_tail(x, o_f, o_b, z, o_d, onorm, subln, w_out, norm_w, mods, w_gu, w_down, rows, mix_gate, mod0, tf,
               lam_init):
    n, d = x.shape
    tm = rows.tm
    ff = w_down.shape[0]
    nf = ff // tf
    hw = H_A * DV_A
    row = lambda c: pl.BlockSpec((tm, c), lambda i, f: (i, 0))
    one = lambda c: pl.BlockSpec((1, c), lambda i, f: (0, 0))
    wbuf = dict(pipeline_mode=pl.Buffered(1)) if nf == 1 else {}
    return pl.pallas_call(
        functools.partial(_ffn_kernel, rows=rows, mod0=mod0, mix_gate=mix_gate, lam_init=lam_init),
        grid=(n // tm, nf),
        in_specs=[row(d), one(d), pl.BlockSpec(mods.shape, lambda i, f: (0, 0)),
                  pl.BlockSpec((d, tf), lambda i, f: (0, f), **wbuf),
                  pl.BlockSpec((d, tf), lambda i, f: (0, nf + f), **wbuf),
                  pl.BlockSpec((tf, d), lambda i, f: (f, 0), **wbuf),
                  row(hw), row(hw), row(hw), row(hw), one(LANES), one(LANES),
                  pl.BlockSpec(w_out.shape, lambda i, f: (0, 0))],
        out_specs=row(d),
        out_shape=jax.ShapeDtypeStruct((n, d), F32),
        scratch_shapes=[pltpu.VMEM((tm, d), BF16), pltpu.VMEM((tm, d), F32), pltpu.VMEM((tm, d), F32)],
        compiler_params=_cparams("parallel", "arbitrary"), name="even_tail",
    )(x, norm_w.reshape(1, d).astype(F32), mods, w_gu, w_gu, w_down, o_f, o_b, z, o_d,
      onorm.reshape(1, LANES).astype(F32), subln.reshape(1, LANES).astype(F32), w_out)


def _moe_kernel(x_ref, nw_ref, mods_ref, rw_ref, wgu_ref, wd_ref, *rest, rows, mod0, n_exp, sb, final):
    if final:
        fw_ref, o_ctx_ref, o_lat_ref, h_scr, g_scr, rk_scr, rkt_scr, acc_scr = rest
    else:
        o_ref, h_scr, g_scr, rk_scr, rkt_scr, acc_scr = rest
    i, e = pl.program_id(0), pl.program_id(1)
    tm, d = x_ref.shape
    g = rows.group(i)
    lane = lax.broadcasted_iota(jnp.int32, (1, LANES), 1)

    @pl.when(e == 0)
    def _():
        h = _modulated(x_ref, nw_ref, mods_ref, g, mod0, d)
        h_scr[...] = h.astype(BF16)
        acc_scr[...] = jnp.zeros_like(acc_scr)
        logits = jnp.where(lane < n_exp, _mm(h, rw_ref[...], True), -1e30)
        pe = jnp.exp(logits - jnp.max(logits, axis=-1, keepdims=True))
        probs = pe / jnp.sum(pe, axis=-1, keepdims=True)
        p1 = jnp.max(probs, axis=-1, keepdims=True)
        i1 = jnp.min(jnp.where(probs == p1, lane, LANES), axis=-1, keepdims=True)
        rest = jnp.where(lane == i1, -1.0, probs)
        p2 = jnp.max(rest, axis=-1, keepdims=True)
        i2 = jnp.min(jnp.where(rest == p2, lane, LANES), axis=-1, keepdims=True)
        den = p1 + p2
        gates = jnp.where(lane == i1, p1 / den, 0.0) + jnp.where(lane == i2, p2 / den, 0.0)
        g_scr[...] = gates
        t_r = lax.broadcasted_iota(jnp.int32, (tm, tm), 0)
        t_c = lax.broadcasted_iota(jnp.int32, (tm, tm), 1)
        earlier = jnp.where(t_r > t_c, 1.0, 0.0).astype(BF16)
        sel = gates > 0.0
        rank = jnp.where(sel, _dot(earlier, jnp.where(sel, 1.0, 0.0).astype(BF16)), -1.0)
        rk_scr[...] = rank
        rkt_scr[...] = jnp.transpose(rank)[:rkt_scr.shape[0], :]

    ff = wd_ref.shape[0]
    g_e = jnp.sum(jnp.where(lane == e, g_scr[...], 0.0), axis=-1, keepdims=True)
    rk_col = jnp.sum(jnp.where(lane == e, rk_scr[...], 0.0), axis=-1, keepdims=True)
    rk_row = rkt_scr[pl.ds(e, 1), :]
    n_rows = (jnp.max(rk_row) + 1.0).astype(jnp.int32)
    g_hi = g_e.astype(BF16).astype(F32)
    g2 = jnp.where(lane == 0, g_hi, jnp.where(lane == 1, g_e - g_hi, 0.0)).astype(BF16)
    def expert_rows(slot0, nr):
        base = slot0.astype(F32)
        r_sub = lax.broadcasted_iota(jnp.int32, (nr, 1), 0).astype(F32)
        r_lane = lax.broadcasted_iota(jnp.int32, (1, nr), 1).astype(F32)
        pick = jnp.where(rk_row == r_sub + base, 1.0, 0.0).astype(BF16)
        xg = _dot(pick, h_scr[...]).astype(BF16)
        gr = _dot(pick, g2)
        gate_r = gr[:, 0:1] + gr[:, 1:2]
        gu = _dot(xg, wgu_ref[...])
        act = _silu(gu[:, :ff]) * gu[:, ff:]
        y = _dot(act.astype(BF16), wd_ref[...])
        put = jnp.where(rk_col == r_lane + base, 1.0, 0.0).astype(BF16)
        acc_scr[...] += _dot(put, (y * gate_r).astype(BF16))

    half, quarter = sb // 2, sb // 4
    rem = n_rows % sb
    n_full = n_rows // sb + jnp.where(rem > half, 1, 0)

    def full_body(jb, carry):
        expert_rows(jb * sb, sb)
        return carry

    lax.fori_loop(0, n_full, full_body, 0)

    @pl.when((rem > quarter) & (rem <= half))
    def _():
        expert_rows(n_full * sb, half)

    @pl.when((rem > 0) & (rem <= quarter))
    def _():
        expert_rows(n_full * sb, quarter)

    def result():
        return x_ref[...] + _mod_chunk(mods_ref, g, mod0 + 2, d) * acc_scr[...]

    last = e == n_exp - 1
    if final:
        @pl.when(last & (i < rows.ctx_tiles))
        def _():
            o_ctx_ref[...] = _rmsnorm_rows(result(), fw_ref[...])

        @pl.when(last & (i >= rows.ctx_tiles))
        def _():
            o_lat_ref[...] = _rmsnorm_rows(result(), fw_ref[...])
    else:
        @pl.when(last)
        def _():
            o_ref[...] = result()


def _moe(x, norm_w, mods, router_w, w_gu, w_down, rows, mod0, final_w=None):
    n, d = x.shape
    tm = rows.tm
    n_exp, _, ff2 = w_gu.shape
    rw = jnp.zeros((d, LANES), F32).at[:, :n_exp].set(router_w.astype(F32))
    final = final_w is not None
    args = [x, norm_w.reshape(1, d).astype(F32), mods, rw, w_gu, w_down]
    in_specs = [pl.BlockSpec((tm, d), lambda i, e: (i, 0)),
                pl.BlockSpec((1, d), lambda i, e: (0, 0)),
                pl.BlockSpec(mods.shape, lambda i, e: (0, 0)),
                pl.BlockSpec((d, LANES), lambda i, e: (0, 0)),
                pl.BlockSpec((None, d, ff2), lambda i, e: (e, 0, 0)),
                pl.BlockSpec((None, ff2 // 2, d), lambda i, e: (e, 0, 0))]
    if final:
        args.append(final_w.reshape(1, d).astype(F32))
        in_specs.append(pl.BlockSpec((1, d), lambda i, e: (0, 0)))
        ct = rows.ctx_tiles
        out_specs = [pl.BlockSpec((tm, d), lambda i, e: (jnp.minimum(i, ct - 1), 0), pipeline_mode=pl.Buffered(1)),
                     pl.BlockSpec((tm, d), lambda i, e: (jnp.maximum(i - ct, 0), 0), pipeline_mode=pl.Buffered(1))]
        out_shape = [jax.ShapeDtypeStruct((ct * tm, d), F32), jax.ShapeDtypeStruct((n - ct * tm, d), F32)]
    else:
        out_specs = pl.BlockSpec((tm, d), lambda i, e: (i, 0))
        out_shape = jax.ShapeDtypeStruct((n, d), F32)
    return pl.pallas_call(
        functools.partial(_moe_kernel, rows=rows, mod0=mod0, n_exp=n_exp, sb=MOE_ROW_BLOCK, final=final),
        grid=(n // tm, n_exp), in_specs=in_specs, out_specs=out_specs, out_shape=out_shape,
        scratch_shapes=[pltpu.VMEM((tm, d), BF16), pltpu.VMEM((tm, LANES), F32), pltpu.VMEM((tm, LANES), F32),
                        pltpu.VMEM((SUBLANES * ((n_exp + SUBLANES - 1) // SUBLANES), tm), F32),
                        pltpu.VMEM((tm, d), F32)],
        compiler_params=_cparams("arbitrary", "arbitrary"), name="moe",
    )(*args)


def kernel(x_prompt, x_sample, state_delta, cache_diff_k, cache_diff_v, cache_mla_ckv, cache_mla_krope, c, c_ctx,
           mod_w, mod_b, norm_mix, norm_ffn, final_norm, ev_w_in, ev_conv_w, ev_a_log, ev_dt_bias, ev_onorm,
           ev_lambda, ev_subln, ev_w_out, ffn_w_gu, ffn_w_down, od_w_in, od_q_norm, od_kv_norm, od_w_uq, od_w_ukv,
           od_w_out, moe_router, moe_w_gu, moe_w_down):
    bp, lp, d = x_prompt.shape
    bs, ls, _ = x_sample.shape
    past = cache_diff_k.shape[3]
    depth = mod_w.shape[0]
    n_p, n_s = bp * lp, bs * ls
    n = n_p + n_s
    assert n_p % ls == 0 and past == lp and depth % 2 == 0

    rows = _Rows(n_p, n_s, ls, _tile(math.gcd(n_p, ls), 512))
    rows_m = _Rows(n_p, n_s, ls, _tile(math.gcd(n_p, ls), 1024))
    rg = _tile(math.gcd(lp, ls), 256)
    cos_t, sin_t = _rope_tables(rows, ls)

    x = jnp.concatenate([x_prompt.reshape(n_p, d), x_sample.reshape(n_s, d)], axis=0)
    n_grp = 1 + bs
    cc = jnp.zeros((2 * SUBLANES * ((n_grp + 15) // 16), d), F32).at[0].set(c_ctx).at[1:n_grp].set(c)
    mods_all = _mods(cc, mod_w, mod_b)

    hq = H_A * DK_A
    hb2 = H_B * 2 * DQK_B
    sm0 = 4 * hq
    qb0 = sm0 + 4 * H_A
    q_scale = (DQK_B ** -0.5) * LOG2E
    mla_scale = ((NOPE_C + ROPE_C) ** -0.5) * LOG2E
    sd_new, dk_new, dv_new, ckv_new, kr_new = [], [], [], [], []

    for layer in range(depth):
        j = layer // 2
        mods = mods_all[layer]
        if layer % 2 == 0:
            lam_init = 0.8 - 0.6 * math.exp(-0.3 * layer)
            w = ev_w_in[j]
            sm = w[:, sm0:qb0].reshape(d, 4, H_A).transpose(0, 2, 1).reshape(d, 4 * H_A)
            w_in = jnp.concatenate([w[:, :sm0], sm, jnp.zeros((d, LANES - 4 * H_A), w.dtype), w[:, qb0:]],
                                   axis=1).astype(BF16)
            qkvn, z_gate, gcol, grow, att, kv_f = _even_proj(
                x, norm_mix[layer], mods, w_in, ev_conv_w[j], ev_a_log[j], ev_dt_bias[j], (cos_t, sin_t), rows,
                3 * hq, hq, hb2, q_scale, lp, ls, n_p)
            o_f, o_b, s_p = _delta_stage(qkvn, gcol, grow, jnp.zeros((bp, 2, H_A, DK_A, DV_A), F32), bp, lp, 0, rg,
                                         precise=True)
            o_f, o_b, _ = _delta_stage(qkvn, gcol, grow, state_delta[:, j].astype(F32), bs, ls, n_p, rg,
                                       into=(o_f, o_b))
            sd_new.append(s_p)
            lam_p = ev_lambda[j].astype(F32)
            nb = hb2 // LANES
            qp, kp, vp = [(att, lambda h: h)], [(att, lambda h: nb + h)], (att, lambda h: 2 * nb + h)
            o_d = _flash(qp, kp, vp, None, lam_p, n_seq=bp, n_heads=H_B, lq=lp, lk=lp, q_row0=0, k_row0=0,
                         tq=_tile(lp, 256), tk=_tile(lp, 512), diff=True, lam_init=lam_init, out_cols=hb2,
                         heads_per_step=H_B)
            ck = cache_diff_k[:, j].transpose(0, 2, 1, 3).reshape(bs * past, H_B * 2 * DQK_B).astype(BF16)
            cv = cache_diff_v[:, j].transpose(0, 2, 1, 3).reshape(bs * past, H_B * DV_B).astype(BF16)
            cache = ([(ck, lambda h: h)], (cv, lambda h: h), past)
            o_d = _flash(qp, kp, vp, cache, lam_p, n_seq=bs, n_heads=H_B, lq=ls, lk=ls, q_row0=n_p, k_row0=n_p,
                         tq=_tile(ls, 512), tk=_tile(ls, 2048), diff=True, lam_init=lam_init, out_cols=hb2, into=o_d,
                         heads_per_step=FLASH_HEADS_PER_STEP)
            dk_new.append(kv_f[:n_p, :hb2].reshape(bp, lp, H_B, 2 * DQK_B).transpose(0, 2, 1, 3))
            dv_new.append(kv_f[:n_p, hb2:].reshape(bp, lp, H_B, DV_B).transpose(0, 2, 1, 3))
            ff = ffn_w_down.shape[1]
            x = _even_tail(x, o_f, o_b, z_gate, o_d, ev_onorm[j], ev_subln[j], ev_w_out[j].astype(BF16),
                           norm_ffn[layer], mods, ffn_w_gu[j].astype(BF16), ffn_w_down[j].astype(BF16), rows, 2, 3,
                           ff, lam_init)
        else:
            n_in = Q_LORA + KV_LORA + ROPE_C
            pad = (-n_in) % LANES
            w_in = jnp.concatenate([od_w_in[j], jnp.zeros((d, pad), od_w_in.dtype)], axis=1).astype(BF16)
            wq = od_w_uq[j].reshape(Q_LORA, H_C, NOPE_C + ROPE_C)
            wq_rope = jnp.concatenate([wq[:, :, NOPE_C:], jnp.zeros((Q_LORA, H_C, LANES - ROPE_C), wq.dtype)], axis=2)
            wq2 = jnp.concatenate([wq[:, :, :NOPE_C].reshape(Q_LORA, H_C * NOPE_C),
                                   wq_rope.reshape(Q_LORA, H_C * LANES)], axis=1).astype(BF16)
            q_all, kv_tok, kr_tok, ckv_n, kr_f = _odd_proj(
                x, norm_mix[layer], mods, w_in, od_q_norm[j], od_kv_norm[j], wq2, od_w_ukv[j].astype(BF16),
                _rope_tables(rows_m, ls), rows_m, mla_scale, H_C * NOPE_C)
            ckv_new.append(ckv_n[:n_p].reshape(bp, lp, KV_LORA))
            kr_new.append(kr_f[:n_p, :ROPE_C].reshape(bp, lp, ROPE_C))
            kv_c = _matmul(cache_mla_ckv[:, j].reshape(bs * past, KV_LORA), od_w_ukv[j].astype(BF16),
                           _tile(bs * past, 512), BF16)
            kr_c = jnp.concatenate([cache_mla_krope[:, j].reshape(bs * past, ROPE_C),
                                    jnp.zeros((bs * past, LANES - ROPE_C), F32)], axis=1).astype(BF16)
            qp = [(q_all, lambda h: h), (q_all, lambda h: H_C + h)]
            kp = [(kv_tok, lambda h: 2 * h), (kr_tok, lambda h: 0)]
            vp = (kv_tok, lambda h: 2 * h + 1)
            o = _flash(qp, kp, vp, None, None, n_seq=bp, n_heads=H_C, lq=lp, lk=lp, q_row0=0, k_row0=0,
                       tq=_tile(lp, 256), tk=_tile(lp, 512), diff=False, lam_init=0.0, out_cols=H_C * V_C,
                       heads_per_step=H_C)
            cache = ([(kv_c, lambda h: 2 * h), (kr_c, lambda h: 0)], (kv_c, lambda h: 2 * h + 1), past)
            o = _flash(qp, kp, vp, cache, None, n_seq=bs, n_heads=H_C, lq=ls, lk=ls, q_row0=n_p, k_row0=n_p,
                       tq=_tile(ls, 1024), tk=_tile(ls, 2048), diff=False, lam_init=0.0, out_cols=H_C * V_C, into=o,
                       heads_per_step=FLASH_HEADS_PER_STEP)
            x = _matmul_residual(x, o, od_w_out[j].astype(BF16), mods, rows, 2)
            x = _moe(x, norm_ffn[layer], mods, moe_router[j], moe_w_gu[j].astype(BF16), moe_w_down[j].astype(BF16),
                     rows_m, 3, final_w=final_norm if layer == depth - 1 else None)

    y_p, y_s = x
    return (y_p.reshape(bp, lp, d), y_s.reshape(bs, ls, d), jnp.stack(sd_new, axis=1),
            jnp.stack(dk_new, axis=1), jnp.stack(dv_new, axis=1), jnp.stack(ckv_new, axis=1),
            jnp.stack(kr_new, axis=1))
```
